```python
import math
import jax, jax.numpy as jnp
from jax import lax
import numpy as np

D_MODEL = 1024
BATCH = 2
SEQ = 8192
DEPTH = 1

GRID_W = 64
CTX_LEN = 256
DN_HEADS = 8
DN_HEAD_DIM = 128
DN_DIM = DN_HEADS * DN_HEAD_DIM
DN_CONV = 5
CHUNK = 64
DT_MIN = 0.001
DT_MAX = 0.1
HY_DIM = 1024
HY_CONV = 3
HY_EMB = 33
HY_FILTER_WIDTH = 64
HY_DECAY_TARGET = 1e-2
HY_FAST_DECAY = 0.3
HY_SLOW_DECAY = 1.5
EPS = 1e-6

PROJ_NAMES = ('q', 'k', 'v', 'z_a', 'beta_f', 'beta_b', 'a_f', 'a_b',
              'hy_v', 'hy_x1', 'hy_x2', 'z_b', 'g_a', 'g_b')
PROJ_SIZES = (DN_DIM, DN_DIM, DN_DIM, DN_DIM, DN_HEADS, DN_HEADS, DN_HEADS, DN_HEADS,
              HY_DIM, HY_DIM, HY_DIM, HY_DIM, D_MODEL, D_MODEL)

kernel_name = 'hybrid_gdn_hyena_prefix_trunk'


def rmsnorm(x, g):
    x32 = x.astype(jnp.float32)
    y = x32 * lax.rsqrt(jnp.mean(x32 * x32, axis=-1, keepdims=True) + EPS)
    return (y * g.astype(jnp.float32)).astype(x.dtype)


def l2norm(t):
    return t * lax.rsqrt(jnp.sum(t * t, axis=-1, keepdims=True) + EPS)


def short_conv(u, w, rows, row_len):
    bsz, _, ch = u.shape
    k = w.shape[0]
    pad = k // 2
    r = jnp.pad(u.reshape(bsz, rows, row_len, ch), ((0, 0), (0, 0), (pad, pad), (0, 0)))
    out = r[:, :, 0:row_len] * w[0]
    for j in range(1, k):
        out = out + r[:, :, j:j + row_len] * w[j]
    return out.reshape(bsz, rows * row_len, ch)


def project(h, w_in):
    z = h @ w_in
    points = [int(s) for s in np.cumsum(PROJ_SIZES)[:-1]]
    return dict(zip(PROJ_NAMES, jnp.split(z, points, axis=-1)))


def chunk_gated_delta(q, k, v, g, beta, s0):
    bsz, seqlen, nh, _ = q.shape
    dv = v.shape[-1]
    n = seqlen // CHUNK

    def to_chunks(t):
        t = t.reshape((bsz, n, CHUNK, nh) + t.shape[3:])
        return jnp.moveaxis(t, (1, 3), (0, 2))

    qc, kc, vc, gc, bc = (to_chunks(t) for t in (q, k, v, g, beta))
    gc = jnp.cumsum(gc, axis=-1)
    idx = jnp.arange(CHUNK)
    incl = idx[:, None] >= idx[None, :]
    strict = idx[:, None] > idx[None, :]
    decay = jnp.exp(jnp.where(incl, gc[..., :, None] - gc[..., None, :], -jnp.inf))
    kb = kc * bc[..., None]
    lmat = jnp.where(strict, jnp.einsum('nbhid,nbhjd->nbhij', kb, kc) * decay, 0.0)
    a_sys = lmat + jnp.eye(CHUNK, dtype=lmat.dtype)
    rhs = jnp.concatenate([vc * bc[..., None], kb * jnp.exp(gc)[..., None]], axis=-1)
    sol = lax.linalg.triangular_solve(a_sys, rhs, left_side=True, lower=True, unit_diagonal=True)
    u, w = sol[..., :dv], sol[..., dv:]
    attn = jnp.einsum('nbhid,nbhjd->nbhij', qc, kc) * decay

    def step(s, inp):
        q_i, k_i, u_i, w_i, g_i, a_i = inp
        v_new = u_i - jnp.einsum('bhck,bhkv->bhcv', w_i, s)
        o_i = (jnp.einsum('bhck,bhkv->bhcv', q_i * jnp.exp(g_i)[..., None], s)
               + jnp.einsum('bhij,bhjv->bhiv', a_i, v_new))
        g_last = g_i[..., -1:]
        s = (s * jnp.exp(g_last)[..., None]
             + jnp.einsum('bhck,bhcv->bhkv', k_i * jnp.exp(g_last - g_i)[..., None], v_new))
        return s, o_i

    s_final, o = lax.scan(step, s0, (qc, kc, u, w, gc, attn))
    o = jnp.moveaxis(o, (0, 2), (1, 3)).reshape(bsz, seqlen, nh, dv)
    return o, s_final


def deltanet(p, lp, rows, row_len, states):
    f32 = jnp.float32
    bsz, seqlen, _ = p['q'].shape
    qkv = jax.nn.silu(short_conv(jnp.concatenate([p['q'], p['k'], p['v']], axis=-1),
                                 lp['dn_conv_w'], rows, row_len)).astype(f32)
    q, k, v = (t.reshape(bsz, seqlen, DN_HEADS, DN_HEAD_DIM) for t in jnp.split(qkv, 3, axis=-1))
    q = l2norm(q) * (DN_HEAD_DIM ** -0.5)
    k = l2norm(k)

    def gate_terms(a_raw, b_raw, d):
        g = -jnp.exp(lp['dn_a_log'][d].astype(f32)) * jax.nn.softplus(
            a_raw.astype(f32) + lp['dn_dt_bias'][d].astype(f32))
        return g, jax.nn.sigmoid(b_raw.astype(f32))

    g_f, beta_f = gate_terms(p['a_f'], p['beta_f'], 0)
    g_b, beta_b = gate_terms(p['a_b'], p['beta_b'], 1)
    flip = lambda t: jnp.flip(t, axis=1)
    o_f, s_f = chunk_gated_delta(q, k, v, g_f, beta_f, states[0])
    o_b, s_b = chunk_gated_delta(flip(q), flip(k), flip(v), flip(g_b), flip(beta_b), states[1])
    o = o_f + flip(o_b)
    o = o * lax.rsqrt(jnp.mean(o * o, axis=-1, keepdims=True) + EPS) * lp['dn_norm_g'].astype(f32)
    o = o.reshape(bsz, seqlen, DN_DIM) * jax.nn.silu(p['z_a'].astype(f32))
    return o, (s_f, s_b)


def hyena_filters(seqlen, lp):
    f32 = jnp.float32
    t = jnp.linspace(0.0, 1.0, seqlen, dtype=f32)[:, None]
    bands = (HY_EMB - 1) // 2
    wpos = (2.0 * math.pi / seqlen) * jnp.arange(seqlen, dtype=f32)[:, None]
    fb = jnp.linspace(1e-4, bands - 1, bands, dtype=f32)[None, :]
    pe = jnp.concatenate([t, jnp.cos(fb * wpos), -jnp.sin(fb * wpos)], axis=-1)
    freq = lp['hy_f_freq'].astype(f32)
    a = jnp.sin(freq * (pe @ lp['hy_f_w1'].astype(f32) + lp['hy_f_b1'].astype(f32)))
    a = jnp.sin(freq * (a @ lp['hy_f_w2'].astype(f32) + lp['hy_f_b2'].astype(f32)))
    a = jnp.sin(freq * (a @ lp['hy_f_w3'].astype(f32) + lp['hy_f_b3'].astype(f32)))
    h = (a @ lp['hy_f_wout'].astype(f32)).reshape(seqlen, 2, 2, HY_DIM)
    deltas = jnp.abs(jnp.linspace(math.log(HY_DECAY_TARGET) / HY_SLOW_DECAY,
                                  math.log(HY_DECAY_TARGET) / HY_FAST_DECAY, HY_DIM, dtype=f32))
    h = h * jnp.exp(-t * deltas)[:, None, None, :]
    full = jnp.concatenate([h[:, 0], jnp.zeros((1, 2, HY_DIM), f32), h[:0:-1, 1]], axis=0)
    full = full * lax.rsqrt(jnp.sum(full * full, axis=0, keepdims=True) + EPS)
    return jnp.fft.rfft(full, axis=0)


def long_conv(u, hf, bias):
    seqlen = u.shape[1]
    uf = jnp.fft.rfft(u, n=2 * seqlen, axis=1)
    y = jnp.fft.irfft(uf * hf[None], n=2 * seqlen, axis=1)[:, :seqlen]
    return y + u * bias


def hyena(p, lp, rows, row_len):
    f32 = jnp.float32
    seqlen = p['hy_v'].shape[1]
    u = short_conv(jnp.concatenate([p['hy_v'], p['hy_x1'], p['hy_x2']], axis=-1),
                   lp['hy_conv_w'], rows, row_len) + lp['hy_conv_b']
    v, x1, x2 = jnp.split(u.astype(f32), 3, axis=-1)
    hf = hyena_filters(seqlen, lp)
    bias = lp['hy_bias'].astype(f32)
    z = x1 * long_conv(v, hf[:, 0], bias[0])
    z = x2 * long_conv(z, hf[:, 1], bias[1])
    return z * jax.nn.silu(p['z_b'].astype(f32))


def merge(p, o_a, o_b, lp):
    dt = p['g_a'].dtype
    y_a = o_a.astype(dt) @ lp['w_pa']
    y_b = o_b.astype(dt) @ lp['w_pb']
    m = jax.nn.sigmoid(p['g_a']) * y_a + jax.nn.sigmoid(p['g_b']) * y_b
    return m @ lp['w_out']


def setup_inputs(seed: int = 0) -> dict:
    key = jax.random.key(seed)
    ks = jax.random.split(key, 32)
    f32 = jnp.float32
    F = HY_FILTER_WIDTH

    def nrm(k, shape, scale):
        return jax.random.normal(k, shape, f32) * scale

    n_in = sum(PROJ_SIZES)
    dt = jnp.exp(jax.random.uniform(ks[8], (DEPTH, 2, DN_HEADS), f32, math.log(DT_MIN), math.log(DT_MAX)))
    return {
        'x': nrm(ks[0], (BATCH, SEQ, D_MODEL), 1.0),
        'c': nrm(ks[1], (BATCH, D_MODEL), 1.0),
        'ctx': nrm(ks[2], (BATCH, CTX_LEN, D_MODEL), 1.0),
        'c_ctx': nrm(ks[3], (D_MODEL,), 1.0),
        'w_mod': nrm(ks[4], (DEPTH, D_MODEL, 3 * D_MODEL), D_MODEL ** -0.5),
        'b_mod': nrm(ks[5], (DEPTH, 3 * D_MODEL), 0.02),
        'norm_g': 1.0 + nrm(ks[6], (DEPTH, D_MODEL), 0.02),
        'w_in': nrm(ks[7], (DEPTH, D_MODEL, n_in), D_MODEL ** -0.5),
        'dn_conv_w': nrm(ks[9], (DEPTH, DN_CONV, 3 * DN_DIM), DN_CONV ** -0.5),
        'dn_a_log': jnp.log(jax.random.uniform(ks[10], (DEPTH, 2, DN_HEADS), f32, 1.0, 16.0)),
        'dn_dt_bias': dt + jnp.log(-jnp.expm1(-dt)),
        'dn_norm_g': 1.0 + nrm(ks[11], (DEPTH, DN_HEAD_DIM), 0.02),
        'hy_conv_w': nrm(ks[12], (DEPTH, HY_CONV, 3 * HY_DIM), HY_CONV ** -0.5),
        'hy_conv_b': nrm(ks[13], (DEPTH, 3 * HY_DIM), 0.02),
        'hy_f_w1': nrm(ks[14], (DEPTH, HY_EMB, F), HY_EMB ** -0.5),
        'hy_f_b1': nrm(ks[15], (DEPTH, F), 0.02),
        'hy_f_w2': nrm(ks[16], (DEPTH, F, F), F ** -0.5),
        'hy_f_b2': nrm(ks[17], (DEPTH, F), 0.02),
        'hy_f_w3': nrm(ks[18], (DEPTH, F, F), F ** -0.5),
        'hy_f_b3': nrm(ks[19], (DEPTH, F), 0.02),
        'hy_f_wout': nrm(ks[20], (DEPTH, F, 4 * HY_DIM), F ** -0.5),
        'hy_f_freq': 1.0 + nrm(ks[21], (DEPTH, F), 0.02),
        'hy_bias': nrm(ks[22], (DEPTH, 2, HY_DIM), 1.0),
        'w_pa': nrm(ks[23], (DEPTH, DN_DIM, D_MODEL), DN_DIM ** -0.5),
        'w_pb': nrm(ks[24], (DEPTH, HY_DIM, D_MODEL), HY_DIM ** -0.5),
        'w_out': nrm(ks[25], (DEPTH, D_MODEL, D_MODEL), D_MODEL ** -0.5),
        'final_g': 1.0 + nrm(ks[26], (D_MODEL,), 0.02),
    }


def reference(x, c, ctx, c_ctx, w_mod, b_mod, norm_g, w_in, dn_conv_w, dn_a_log, dn_dt_bias, dn_norm_g,
              hy_conv_w, hy_conv_b, hy_f_w1, hy_f_b1, hy_f_w2, hy_f_b2, hy_f_w3, hy_f_b3, hy_f_wout,
              hy_f_freq, hy_bias, w_pa, w_pb, w_out, final_g):
    bsz, n_lat, _ = x.shape
    rows = n_lat // GRID_W
    zero = jnp.zeros((bsz, DN_HEADS, DN_HEAD_DIM, DN_HEAD_DIM), jnp.float32)
    ctx_s = ctx
    for i in range(DEPTH):
        lp = {'dn_conv_w': dn_conv_w[i], 'dn_a_log': dn_a_log[i], 'dn_dt_bias': dn_dt_bias[i],
              'dn_norm_g': dn_norm_g[i], 'hy_conv_w': hy_conv_w[i], 'hy_conv_b': hy_conv_b[i],
              'hy_f_w1': hy_f_w1[i], 'hy_f_b1': hy_f_b1[i], 'hy_f_w2': hy_f_w2[i], 'hy_f_b2': hy_f_b2[i],
              'hy_f_w3': hy_f_w3[i], 'hy_f_b3': hy_f_b3[i], 'hy_f_wout': hy_f_wout[i],
              'hy_f_freq': hy_f_freq[i], 'hy_bias': hy_bias[i], 'w_pa': w_pa[i], 'w_pb': w_pb[i],
              'w_out': w_out[i]}
        mod_x = jax.nn.silu(c) @ w_mod[i] + b_mod[i]
        mod_c = jax.nn.silu(c_ctx) @ w_mod[i] + b_mod[i]
        sh_x, sc_x, gt_x = jnp.split(mod_x[:, None, :], 3, axis=-1)
        sh_c, sc_c, gt_c = jnp.split(mod_c[None, None, :], 3, axis=-1)
        hc = rmsnorm(ctx_s, norm_g[i]) * (1.0 + sc_c) + sh_c
        pc = project(hc, w_in[i])
        oc, ctx_states = deltanet(pc, lp, 1, CTX_LEN, (zero, zero))
        hx = rmsnorm(x, norm_g[i]) * (1.0 + sc_x) + sh_x
        px = project(hx, w_in[i])
        ox, _ = deltanet(px, lp, rows, GRID_W, ctx_states)
        yx = hyena(px, lp, rows, GRID_W)
        x = x + gt_x * merge(px, ox, yx, lp)
        if i < DEPTH - 1:
            yc = hyena(pc, lp, 1, CTX_LEN)
            ctx_s = ctx_s + gt_c * merge(pc, oc, yc, lp)
    return rmsnorm(x, final_g)
```

```python
import functools
import math

import numpy as np
import jax
import jax.numpy as jnp
from jax import lax
from jax.experimental import pallas as pl
from jax.experimental.pallas import tpu as pltpu

F32 = jnp.float32
BF16 = jnp.bfloat16
HIGHEST = lax.Precision.HIGHEST

EPS = 1e-6
HEADS = 8
HEAD_DIM = 128
CHUNK = 64
GRID_W = 64
HY_EMB = 33
HY_DECAY_TARGET = 1e-2
HY_FAST_DECAY = 0.3
HY_SLOW_DECAY = 1.5

LANES = 128
SUBLANES = 8
FFT_INNER = 128
FFT_STEPS = 8
SLAB_PAD = 8
VMEM_LIMIT = 56 * 1024 * 1024


def _cparams(sem):
    return pltpu.CompilerParams(dimension_semantics=sem, vmem_limit_bytes=VMEM_LIMIT)


def _bdot(a, b):
    return jnp.dot(a.astype(BF16), b.astype(BF16), preferred_element_type=F32)


def _silu(x):
    return x * jax.nn.sigmoid(x)


def _mod_kernel(c_ref, w_ref, b_ref, o_ref):
    s = _silu(c_ref[...])
    o_ref[...] = jnp.dot(s, w_ref[...], precision=HIGHEST, preferred_element_type=F32) + b_ref[...]


def _mod_call(cvec, w_mod, b_mod):
    rows, d = cvec.shape
    n = w_mod.shape[1]
    tn = 1024
    return pl.pallas_call(
        _mod_kernel,
        grid=(n // tn,),
        in_specs=[pl.BlockSpec((rows, d), lambda j: (0, 0)),
                  pl.BlockSpec((d, tn), lambda j: (0, j)),
                  pl.BlockSpec((1, tn), lambda j: (0, j))],
        out_specs=pl.BlockSpec((rows, tn), lambda j: (0, j)),
        out_shape=jax.ShapeDtypeStruct((rows, n), F32),
        compiler_params=_cparams(("arbitrary",)),
        name="mod",
    )(cvec, w_mod, b_mod.reshape(1, n))


def _hnorm_kernel(tok_ref, mod_ref, g_ref, o_ref, *, n_lat, tm, d, ctx_row):
    b = pl.program_id(0)
    t = pl.program_id(1)
    row = jnp.where(t * tm >= n_lat, ctx_row, b)
    m = mod_ref[pl.ds(row, 1), :]
    x = tok_ref[0]
    y = x * lax.rsqrt(jnp.mean(x * x, axis=-1, keepdims=True) + EPS) * g_ref[...]
    o_ref[0] = (y * (1.0 + m[:, d:2 * d]) + m[:, 0:d]).astype(BF16)


def _hnorm_call(tok, mod, norm_g, n_lat, ctx_row):
    bsz, lt, d = tok.shape
    tm = 256
    return pl.pallas_call(
        functools.partial(_hnorm_kernel, n_lat=n_lat, tm=tm, d=d, ctx_row=ctx_row),
        grid=(bsz, lt // tm),
        in_specs=[pl.BlockSpec((1, tm, d), lambda b, t: (b, t, 0)),
                  pl.BlockSpec(mod.shape, lambda b, t: (0, 0)),
                  pl.BlockSpec((1, d), lambda b, t: (0, 0))],
        out_specs=pl.BlockSpec((1, tm, d), lambda b, t: (b, t, 0)),
        out_shape=jax.ShapeDtypeStruct((bsz, lt, d), BF16),
        compiler_params=_cparams(("arbitrary", "arbitrary")),
        name="hnorm",
    )(tok, mod, norm_g.reshape(1, d))


def _row_conv(z, w_ref, pos, rowlen):
    taps = w_ref.shape[0]
    pad = taps // 2
    tm = z.shape[0]
    out = None
    for j in range(taps):
        d = j - pad
        if d == 0:
            term = z * w_ref[j:j + 1, :]
        else:
            zs = pltpu.roll(z, (-d) % tm, axis=0)
            ok = (pos + d >= 0) & (pos + d < rowlen)
            term = jnp.where(ok, zs, 0.0) * w_ref[j:j + 1, :]
        out = term if out is None else out + term
    return out


def _row_positions(t, tm, n_lat, n_ctx):
    r = lax.broadcasted_iota(jnp.int32, (tm, 1), 0) + t * tm
    is_ctx = r >= n_lat
    pos = jnp.where(is_ctx, r - n_lat, lax.rem(r, GRID_W))
    rowlen = jnp.where(is_ctx, n_ctx, GRID_W)
    return pos, rowlen


def _qkv_kernel(h_ref, w_ref, cw_ref, o_ref, *, n_lat, n_ctx, tm):
    j = pl.program_id(0)
    t = pl.program_id(2)
    z = jnp.dot(h_ref[0], w_ref[0], preferred_element_type=F32)
    pos, rowlen = _row_positions(t, tm, n_lat, n_ctx)
    y = _silu(_row_conv(z, cw_ref.at[0], pos, rowlen))

    @pl.when(j < 2)
    def _():
        scale = jnp.where(j == 0, HEAD_DIM ** -0.5, 1.0)
        for hh in range(HEADS):
            yh = y[:, hh * HEAD_DIM:(hh + 1) * HEAD_DIM]
            nrm = jnp.sum(yh * yh, axis=-1, keepdims=True)
            o_ref[0, 0, :, hh * HEAD_DIM:(hh + 1) * HEAD_DIM] = (
                yh * (lax.rsqrt(nrm + EPS) * scale)).astype(BF16)

    @pl.when(j == 2)
    def _():
        o_ref[0, 0] = y.astype(BF16)


def _qkv_call(h, w3, cw3, n_lat, n_ctx):
    bsz, lt, d = h.shape
    nw = w3.shape[2]
    tm = 768 if lt % 768 == 0 else 256
    return pl.pallas_call(
        functools.partial(_qkv_kernel, n_lat=n_lat, n_ctx=n_ctx, tm=tm),
        grid=(3, bsz, lt // tm),
        in_specs=[pl.BlockSpec((1, tm, d), lambda j, b, t: (b, t, 0)),
                  pl.BlockSpec((1, d, nw), lambda j, b, t: (j, 0, 0)),
                  pl.BlockSpec((1, cw3.shape[1], nw), lambda j, b, t: (j, 0, 0))],
        out_specs=pl.BlockSpec((1, 1, tm, nw), lambda j, b, t: (j, b, t, 0)),
        out_shape=jax.ShapeDtypeStruct((3, bsz, lt, nw), BF16),
        compiler_params=_cparams(("arbitrary", "arbitrary", "arbitrary")),
        name="qkv",
    )(h, w3, cw3)


def _gates_kernel(h_ref, w_ref, alog_ref, dtb_ref, o_ref):
    z = jnp.dot(h_ref[0], w_ref[...], preferred_element_type=F32)
    lane = lax.broadcasted_iota(jnp.int32, z.shape, 1)
    beta = jax.nn.sigmoid(z)
    u = z + dtb_ref[...]
    softplus = jnp.maximum(u, 0.0) + jnp.log(1.0 + jnp.exp(-jnp.abs(u)))
    g = -jnp.exp(alog_ref[...]) * softplus
    o_ref[0] = jnp.where(lane < 2 * HEADS, beta, g)


def _gates_call(h, wg, alog, dtb):
    bsz, lt, d = h.shape
    tm = 256
    return pl.pallas_call(
        _gates_kernel,
        grid=(bsz, lt // tm),
        in_specs=[pl.BlockSpec((1, tm, d), lambda b, t: (b, t, 0)),
                  pl.BlockSpec((d, LANES), lambda b, t: (0, 0)),
                  pl.BlockSpec((1, LANES), lambda b, t: (0, 0)),
                  pl.BlockSpec((1, LANES), lambda b, t: (0, 0))],
        out_specs=pl.BlockSpec((1, tm, LANES), lambda b, t: (b, t, 0)),
        out_shape=jax.ShapeDtypeStruct((bsz, lt, LANES), F32),
        compiler_params=_cparams(("arbitrary", "arbitrary")),
        name="gates",
    )(h, wg, alog, dtb)


def _hyproj_kernel(h_ref, w_ref, cw_ref, cb_ref, o_ref, *, n_lat, n_ctx, tm):
    t = pl.program_id(2)
    z = jnp.dot(h_ref[0], w_ref[0], preferred_element_type=F32)
    pos, rowlen = _row_positions(t, tm, n_lat, n_ctx)
    o_ref[0, 0] = (_row_conv(z, cw_ref.at[0], pos, rowlen) + cb_ref[0]).astype(BF16)


def _hyproj_call(h, w3, cw3, cb3, n_lat, n_ctx):
    bsz, _, d = h.shape
    nw = w3.shape[2]
    tm = 512 if n_lat % 512 == 0 else 256
    return pl.pallas_call(
        functools.partial(_hyproj_kernel, n_lat=n_lat, n_ctx=n_ctx, tm=tm),
        grid=(3, bsz, n_lat // tm),
        in_specs=[pl.BlockSpec((1, tm, d), lambda j, b, t: (b, t, 0)),
                  pl.BlockSpec((1, d, nw), lambda j, b, t: (j, 0, 0)),
                  pl.BlockSpec((1, cw3.shape[1], nw), lambda j, b, t: (j, 0, 0)),
                  pl.BlockSpec((1, 1, nw), lambda j, b, t: (j, 0, 0))],
        out_specs=pl.BlockSpec((1, 1, tm, nw), lambda j, b, t: (j, b, t, 0)),
        out_shape=jax.ShapeDtypeStruct((3, bsz, n_lat, nw), BF16),
        compiler_params=_cparams(("arbitrary", "arbitrary", "arbitrary")),
        name="hyproj",
    )(h, w3, cw3, cb3)


def _gproj_kernel(h_ref, w_ref, o_ref):
    j = pl.program_id(0)
    z = jnp.dot(h_ref[0], w_ref[0], preferred_element_type=F32)
    s = jax.nn.sigmoid(z)
    o_ref[0, 0] = jnp.where(j < 2, z * s, s).astype(BF16)


def _gproj_call(h, w4, n_lat):
    bsz, _, d = h.shape
    nw = w4.shape[2]
    tm = 512 if n_lat % 512 == 0 else 256
    return pl.pallas_call(
        _gproj_kernel,
        grid=(4, bsz, n_lat // tm),
        in_specs=[pl.BlockSpec((1, tm, d), lambda j, b, t: (b, t, 0)),
                  pl.BlockSpec((1, d, nw), lambda j, b, t: (j, 0, 0))],
        out_specs=pl.BlockSpec((1, 1, tm, nw), lambda j, b, t: (j, b, t, 0)),
        out_shape=jax.ShapeDtypeStruct((4, bsz, n_lat, nw), BF16),
        compiler_params=_cparams(("arbitrary", "arbitrary", "arbitrary")),
        name="gproj",
    )(h, w4)


def _unit_tri_inverse(l):
    c = l.shape[0]
    ri = lax.broadcasted_iota(jnp.int32, (c, c), 0)
    ci = lax.broadcasted_iota(jnp.int32, (c, c), 1)
    same = lambda n: (ri // n) == (ci // n)
    t = jnp.where(ri == ci, 1.0, 0.0) - jnp.where(same(2), l, 0.0)
    n = 2
    while n < c:
        off = jnp.where(same(2 * n) & jnp.logical_not(same(n)), l, 0.0)
        t = t - _bdot(_bdot(t, off), t)
        n *= 2
    return t


def _delta_chunk(q, k, v, beta, gc, gcr, glast, lower, s):
    c = q.shape[0]
    ri = lax.broadcasted_iota(jnp.int32, (c, c), 0)
    ci = lax.broadcasted_iota(jnp.int32, (c, c), 1)
    incl = (ri >= ci) if lower else (ri <= ci)
    strict = (ri > ci) if lower else (ri < ci)
    dec = jnp.exp(jnp.where(incl, gc - gcr, -jnp.inf))
    kb = k * beta
    nt = (((1,), (1,)), ((), ()))
    kk = lax.dot_general(kb.astype(BF16), k.astype(BF16), nt, preferred_element_type=F32)
    tinv = _unit_tri_inverse(jnp.where(strict, kk * dec, 0.0))
    attn = lax.dot_general(q.astype(BF16), k.astype(BF16), nt, preferred_element_type=F32) * dec
    eg = jnp.exp(gc)
    r = v * beta - _bdot(kb * eg, s)
    v_new = _bdot(tinv, r)
    o = _bdot(q * eg, s) + _bdot(attn, v_new)
    kd = k * jnp.exp(glast - gc)
    tn = (((0,), (0,)), ((), ()))
    s_new = s * jnp.exp(glast) + lax.dot_general(
        kd.astype(BF16), v_new.astype(BF16), tn, preferred_element_type=F32)
    return o, s_new


def _scan_kernel(qf_ref, qb_ref, gf_ref, gb_ref, gtf_ref, gtb_ref, of_ref, ob_ref, s_ref, *, bsz):
    i = pl.program_id(0)

    @pl.when(i == 0)
    def _():
        s_ref[...] = jnp.zeros_like(s_ref)

    c = CHUNK
    ri = lax.broadcasted_iota(jnp.int32, (c, c), 0)
    ci = lax.broadcasted_iota(jnp.int32, (c, c), 1)
    tri_l = jnp.where(ri >= ci, 1.0, 0.0)
    tri_u = jnp.where(ri <= ci, 1.0, 0.0)
    dirs = ((qf_ref, gf_ref, gtf_ref, of_ref, True), (qb_ref, gb_ref, gtb_ref, ob_ref, False))
    for d, (q_ref, g_ref, gt_ref, o_ref, lower) in enumerate(dirs):
        for b in range(bsz):
            g = g_ref[b]
            gt = gt_ref[b, 0]
            gcs = jnp.dot(tri_l if lower else tri_u, g, precision=HIGHEST, preferred_element_type=F32)
            gcts = jnp.dot(gt, tri_u if lower else tri_l, precision=HIGHEST, preferred_element_type=F32)
            for hh in range(HEADS):
                lb = d * HEADS + hh
                lg = 2 * HEADS + d * HEADS + hh
                hs = slice(hh * HEAD_DIM, (hh + 1) * HEAD_DIM)
                gcr = gcts[lg:lg + 1, :]
                glast = gcr[:, c - 1:c] if lower else gcr[:, 0:1]
                o, s_new = _delta_chunk(
                    q_ref[0, b, :, hs].astype(F32), q_ref[1, b, :, hs].astype(F32),
                    q_ref[2, b, :, hs].astype(F32), g[:, lb:lb + 1], gcs[:, lg:lg + 1], gcr, glast,
                    lower, s_ref[d, b, hh])
                s_ref[d, b, hh] = s_new
                o_ref[b, :, hs] = o.astype(BF16)


def _scan_call(qkv, gates, gates_t, n_lat, n_ctx):
    _, bsz, lt, dn = qkv.shape
    ncx = n_lat // CHUNK
    ncc = n_ctx // CHUNK
    nc = ncx + ncc

    def cf(i):
        return jnp.where(i < ncc, ncx + i, i - ncc)

    def cb(i):
        return jnp.where(i < ncc, nc - 1 - i, nc - 1 - i)

    qspec = lambda fn: pl.BlockSpec((3, bsz, CHUNK, dn), lambda i: (0, 0, fn(i), 0))
    gspec = lambda fn: pl.BlockSpec((bsz, CHUNK, LANES), lambda i: (0, fn(i), 0))
    gtspec = lambda fn: pl.BlockSpec((bsz, 1, 4 * HEADS, CHUNK), lambda i: (0, fn(i), 0, 0))
    ospec = lambda fn: pl.BlockSpec((bsz, CHUNK, dn), lambda i: (0, fn(i), 0))
    return pl.pallas_call(
        functools.partial(_scan_kernel, bsz=bsz),
        grid=(nc,),
        in_specs=[qspec(cf), qspec(cb), gspec(cf), gspec(cb), gtspec(cf), gtspec(cb)],
        out_specs=[ospec(cf), ospec(cb)],
        out_shape=[jax.ShapeDtypeStruct((bsz, lt, dn), BF16)] * 2,
        scratch_shapes=[pltpu.VMEM((2, bsz, HEADS, HEAD_DIM, HEAD_DIM), F32)],
        compiler_params=_cparams(("arbitrary",)),
        name="scan",
    )(qkv, qkv, gates, gates, gates_t, gates_t)


def _hidden_kernel(pe_ref, w1_ref, b1_ref, w2_ref, b2_ref, w3_ref, b3_ref, f_ref, o_ref):
    f = f_ref[...]
    dot = functools.partial(jnp.dot, precision=HIGHEST, preferred_element_type=F32)
    a = jnp.sin(f * (dot(pe_ref[...], w1_ref[...]) + b1_ref[...]))
    a = jnp.sin(f * (dot(a, w2_ref[...]) + b2_ref[...]))
    o_ref[...] = jnp.sin(f * (dot(a, w3_ref[...]) + b3_ref[...]))


def _hidden_call(pe, w1p, b1, w2, b2, w3, b3, freq):
    rows, pw = pe.shape
    fw = w2.shape[0]
    tr = 512
    full = lambda a: pl.BlockSpec(a.shape, lambda t: (0,) * a.ndim)
    args = (w1p, b1.reshape(1, fw), w2, b2.reshape(1, fw), w3, b3.reshape(1, fw), freq.reshape(1, fw))
    return pl.pallas_call(
        _hidden_kernel,
        grid=(rows // tr,),
        in_specs=[pl.BlockSpec((tr, pw), lambda t: (t, 0))] + [full(a) for a in args],
        out_specs=pl.BlockSpec((tr, fw), lambda t: (t, 0)),
        out_shape=jax.ShapeDtypeStruct((rows, fw), F32),
        compiler_params=_cparams(("arbitrary",)),
        name="hidden",
    )(pe, *args)


def _ftime_kernel(a_ref, t_ref, w_ref, dl_ref, h_ref, ss_ref, *, n_lat, tr):
    t = pl.program_id(0)
    h = _bdot(a_ref[...], w_ref[...]) * jnp.exp(-t_ref[...] * dl_ref[...])
    row = lax.broadcasted_iota(jnp.int32, (tr, 1), 0) + t * tr
    h = jnp.where(row == n_lat, 0.0, h)
    h_ref[...] = h

    @pl.when(t == 0)
    def _():
        ss_ref[...] = jnp.zeros_like(ss_ref)

    ss_ref[...] += jnp.sum(h * h, axis=0, keepdims=True)


def _ftime_call(a3, tcol, wout2, deltas2, n_lat):
    rows, fw = a3.shape
    ncol = wout2.shape[2]
    tr = 512
    side = lambda t: (t * tr) // n_lat
    return pl.pallas_call(
        functools.partial(_ftime_kernel, n_lat=n_lat, tr=tr),
        grid=(rows // tr,),
        in_specs=[pl.BlockSpec((tr, fw), lambda t: (t, 0)),
                  pl.BlockSpec((tr, 1), lambda t: (t, 0)),
                  pl.BlockSpec((None, fw, ncol), lambda t: (side(t), 0, 0)),
                  pl.BlockSpec((1, ncol), lambda t: (0, 0))],
        out_specs=[pl.BlockSpec((tr, ncol), lambda t: (t, 0)),
                   pl.BlockSpec((1, ncol), lambda t: (0, 0))],
        out_shape=[jax.ShapeDtypeStruct((rows, ncol), F32), jax.ShapeDtypeStruct((1, ncol), F32)],
        compiler_params=_cparams(("arbitrary",)),
        name="ftime",
    )(a3, tcol, wout2, deltas2)


@functools.lru_cache(maxsize=None)
def _fft_tables(n_lat):
    n = 2 * n_lat
    n1t = n // FFT_INNER
    h1 = n1t // 2
    n2 = np.arange(FFT_INNER, dtype=np.int64)[:, None, None]
    k1 = np.arange(n1t, dtype=np.int64)[None, :, None]
    n1 = np.arange(h1, dtype=np.int64)[None, None, :]
    ang = (-2.0 * np.pi / n) * ((k1 * (FFT_INNER * n1 + n2)) % n)
    er, ei = np.cos(ang), np.sin(ang)
    e1 = np.concatenate([np.concatenate([er, -ei], axis=2), np.concatenate([ei, er], axis=2)], axis=1)
    e4 = np.transpose(e1, (0, 2, 1))
    sgn = np.where(np.arange(n1t) % 2 == 0, 1.0, -1.0)[None, :, None]
    ef = np.concatenate([np.concatenate([er, sgn * er], axis=2), np.concatenate([ei, sgn * ei], axis=2)], axis=1)
    a = np.arange(FFT_INNER, dtype=np.int64)
    ang2 = (-2.0 * np.pi / FFT_INNER) * ((a[:, None] * a[None, :]) % FFT_INNER)
    gr, gi = np.cos(ang2), np.sin(ang2)
    g2 = np.block([[gr, -gi], [gi, gr]])
    f32 = lambda x: np.ascontiguousarray(x, dtype=np.float32)
    return f32(e1), f32(e4), f32(ef), f32(g2), f32(g2.T)


def _hold(p, j, first, last, nsteps):
    active = (p >= first) & (p <= last) & (lax.rem(p - first, 2) == 0)
    return jnp.where(active, j, jnp.where(p < first, 0, nsteps - 1))


def _spectrum_step(x_ref, g2_ref, k1, n1t, stride):
    re = x_ref[pl.ds(k1, FFT_INNER, stride=stride), :]
    im = x_ref[pl.ds(n1t + k1, FFT_INNER, stride=stride), :]
    return _bdot(g2_ref[...], jnp.concatenate([re, im], axis=0))


def _ffft_kernel(h_ref, ef_ref, g2_ref, ss_ref, o_ref, x_ref, *, n1t, grp, kgrp, stride, inv_n):
    p = pl.program_id(1)
    j = pl.program_id(2)

    @pl.when(p == 0)
    def _():
        for jj in range(grp):
            u = jnp.concatenate([h_ref[0, jj], h_ref[1, jj]], axis=0)
            base = pl.multiple_of((j * grp + jj) * stride, SUBLANES)
            x_ref[pl.ds(base, 2 * n1t), :] = _bdot(ef_ref[jj], u)

    @pl.when(p == 1)
    def _():
        scale = lax.rsqrt(ss_ref[...] + EPS) * inv_n
        for jj in range(kgrp):
            xf = _spectrum_step(x_ref, g2_ref, j * kgrp + jj, n1t, stride)
            o_ref[jj] = (xf * scale).astype(BF16)


def _fft_groups(n1t):
    ns = FFT_STEPS
    assert FFT_INNER % ns == 0 and n1t % ns == 0
    return ns, FFT_INNER // ns, n1t // ns


def _ffft_call(hfull, ssq, ef, g2, n_lat):
    ncol = hfull.shape[1]
    n1t = 2 * n_lat // FFT_INNER
    h1 = n1t // 2
    ct = LANES
    ns, grp, kgrp = _fft_groups(n1t)
    stride = 2 * n1t + SLAB_PAD
    hv = hfull.reshape(2, FFT_INNER, h1, ncol)
    nct = ncol // ct
    cpo = nct // 2
    return pl.pallas_call(
        functools.partial(_ffft_kernel, n1t=n1t, grp=grp, kgrp=kgrp, stride=stride, inv_n=1.0 / (2 * n_lat)),
        grid=(nct, 2, ns),
        in_specs=[pl.BlockSpec((2, grp, h1, ct), lambda c, p, j: (0, _hold(p, j, 0, 0, ns), 0, c)),
                  pl.BlockSpec((grp, 2 * n1t, n1t), lambda c, p, j: (_hold(p, j, 0, 0, ns), 0, 0)),
                  pl.BlockSpec((2 * FFT_INNER, 2 * FFT_INNER), lambda c, p, j: (0, 0)),
                  pl.BlockSpec((1, ct), lambda c, p, j: (0, c))],
        out_specs=pl.BlockSpec((None, kgrp, 2 * FFT_INNER, ct),
                               lambda c, p, j: (c // cpo, _hold(p, j, 1, 1, ns), 0, c % cpo)),
        out_shape=jax.ShapeDtypeStruct((2, n1t, 2 * FFT_INNER, ncol // 2), BF16),
        scratch_shapes=[pltpu.VMEM((FFT_INNER * stride, ct), F32)],
        compiler_params=_cparams(("arbitrary", "arbitrary", "arbitrary")),
        name="ffft",
    )(hv, ef, g2, ssq)


def _hyena_kernel(v_ref, x1_ref, x2_ref, hf_ref, e1_ref, e4_ref, g2_ref, g3_ref, bias_ref,
                  o_ref, x_ref, z_ref, *, n1t, grp, kgrp, stride):
    p = pl.program_id(1)
    j = pl.program_id(2)
    h1 = n1t // 2

    def gather(ref, jj):
        return jnp.concatenate([ref[0, jj], ref[1, jj]], axis=0).astype(F32)

    def slab_base(jj):
        return pl.multiple_of((j * grp + jj) * stride, SUBLANES)

    def spectrum_phase():
        for jj in range(kgrp):
            k1 = j * kgrp + jj
            xf = _spectrum_step(x_ref, g2_ref, k1, n1t, stride)
            hf = hf_ref[jj].astype(F32)
            xr, xi = xf[:FFT_INNER], xf[FFT_INNER:]
            hr, hi = hf[:FFT_INNER], hf[FFT_INNER:]
            y = jnp.concatenate([xr * hr - xi * hi, xr * hi + xi * hr], axis=0)
            bb = _bdot(g3_ref[...], y)
            x_ref[pl.ds(k1, FFT_INNER, stride=stride), :] = bb[:FFT_INNER]
            x_ref[pl.ds(n1t + k1, FFT_INNER, stride=stride), :] = bb[FFT_INNER:]

    @pl.when(p == 0)
    def _():
        for jj in range(grp):
            x_ref[pl.ds(slab_base(jj), 2 * n1t), :] = _bdot(e1_ref[jj], gather(v_ref, jj))

    @pl.when((p == 1) | (p == 3))
    def _():
        spectrum_phase()

    @pl.when(p == 2)
    def _():
        for jj in range(grp):
            y = _bdot(e4_ref[jj], x_ref[pl.ds(slab_base(jj), 2 * n1t), :])
            z1 = gather(x1_ref, jj) * (y + gather(v_ref, jj) * bias_ref[0:1, :])
            z_ref[j * grp + jj] = z1
            x_ref[pl.ds(slab_base(jj), 2 * n1t), :] = _bdot(e1_ref[jj], z1)

    @pl.when(p == 4)
    def _():
        for jj in range(grp):
            y = _bdot(e4_ref[jj], x_ref[pl.ds(slab_base(jj), 2 * n1t), :])
            out = gather(x2_ref, jj) * (y + z_ref[j * grp + jj] * bias_ref[1:2, :])
            o_ref[0, jj] = out[:h1].astype(o_ref.dtype)
            o_ref[1, jj] = out[h1:].astype(o_ref.dtype)


def _hyena_call(hyp, hf, e1, e4, g2, g3, bias, n_lat):
    _, bsz, _, ch = hyp.shape
    assert bsz == 2, "the two batch rows are packed as the real and imaginary parts of one FFT"
    n1t = 2 * n_lat // FFT_INNER
    h1 = n1t // 2
    ct = LANES
    ns, grp, kgrp = _fft_groups(n1t)
    stride = 2 * n1t + SLAB_PAD
    hv = jnp.transpose(hyp.reshape(3, bsz, h1, FFT_INNER, ch), (0, 1, 3, 2, 4))

    def hspec(which, first, last):
        return pl.BlockSpec((None, bsz, grp, h1, ct),
                            lambda c, p, j: (which, 0, _hold(p, j, first, last, ns), 0, c))

    def hf_map(c, p, j):
        order = jnp.where(p >= 2, 1, 0)
        step = jnp.where((p == 1) | (p == 3), j, jnp.where((p == 0) | (p == 2), 0, ns - 1))
        return (order, step, 0, c)

    out = pl.pallas_call(
        functools.partial(_hyena_kernel, n1t=n1t, grp=grp, kgrp=kgrp, stride=stride),
        grid=(ch // ct, 5, ns),
        in_specs=[hspec(0, 0, 2), hspec(1, 2, 2), hspec(2, 4, 4),
                  pl.BlockSpec((None, kgrp, 2 * FFT_INNER, ct), hf_map),
                  pl.BlockSpec((grp, 2 * n1t, n1t), lambda c, p, j: (_hold(p, j, 0, 2, ns), 0, 0)),
                  pl.BlockSpec((grp, n1t, 2 * n1t), lambda c, p, j: (_hold(p, j, 2, 4, ns), 0, 0)),
                  pl.BlockSpec((2 * FFT_INNER, 2 * FFT_INNER), lambda c, p, j: (0, 0)),
                  pl.BlockSpec((2 * FFT_INNER, 2 * FFT_INNER), lambda c, p, j: (0, 0)),
                  pl.BlockSpec((2, ct), lambda c, p, j: (0, c))],
        out_specs=pl.BlockSpec((bsz, grp, h1, ct), lambda c, p, j: (0, _hold(p, j, 4, 4, ns), 0, c)),
        out_shape=jax.ShapeDtypeStruct((bsz, FFT_INNER, h1, ch), BF16),
        scratch_shapes=[pltpu.VMEM((FFT_INNER * stride, ct), F32),
                        pltpu.VMEM((FFT_INNER, n1t, ct), F32)],
        compiler_params=_cparams(("arbitrary", "arbitrary", "arbitrary")),
        name="hyena",
    )(hv, hv, hv, hf, e1, e4, g2, g3, bias)
    return jnp.transpose(out, (0, 2, 1, 3)).reshape(bsz, n_lat, ch)


def _merge_kernel(of_ref, ob_ref, gp_ref, hy_ref, x_ref, mod_ref, dng_ref, wpa_ref, wpb_ref, wo_ref,
                  fg_ref, o_ref, oa_ref, *, d):
    b = pl.program_id(0)
    o = of_ref[0].astype(F32) + ob_ref[0].astype(F32)
    za = gp_ref[0, 0].astype(F32)
    for hh in range(HEADS):
        hs = slice(hh * HEAD_DIM, (hh + 1) * HEAD_DIM)
        oh = o[:, hs]
        ms = jnp.mean(oh * oh, axis=-1, keepdims=True)
        oa_ref[:, hs] = (oh * lax.rsqrt(ms + EPS) * dng_ref[...] * za[:, hs]).astype(BF16)
    y_a = jnp.dot(oa_ref[...], wpa_ref[...], preferred_element_type=F32)
    o_b = hy_ref[0].astype(F32) * gp_ref[1, 0].astype(F32)
    y_b = _bdot(o_b, wpb_ref[...])
    m = gp_ref[2, 0].astype(F32) * y_a + gp_ref[3, 0].astype(F32) * y_b
    y = _bdot(m, wo_ref[...])
    gate = mod_ref[pl.ds(b, 1), :][:, 2 * d:3 * d]
    xn = x_ref[0] + gate * y
    o_ref[0] = xn * lax.rsqrt(jnp.mean(xn * xn, axis=-1, keepdims=True) + EPS) * fg_ref[...]


def _merge_call(o_f, o_b, gp, hy, x, mod, dn_norm_g, w_pa, w_pb, w_out, final_g):
    bsz, n_lat, d = x.shape
    tm = 512 if n_lat % 512 == 0 else 256
    wfull = lambda a: pl.BlockSpec(a.shape, lambda b, t: (0,) * a.ndim)
    tok = pl.BlockSpec((1, tm, d), lambda b, t: (b, t, 0))
    return pl.pallas_call(
        functools.partial(_merge_kernel, d=d),
        grid=(bsz, n_lat // tm),
        in_specs=[tok, tok,
                  pl.BlockSpec((4, 1, tm, d), lambda b, t: (0, b, t, 0)),
                  tok, tok, wfull(mod),
                  pl.BlockSpec((1, HEAD_DIM), lambda b, t: (0, 0)),
                  wfull(w_pa), wfull(w_pb), wfull(w_out),
                  pl.BlockSpec((1, d), lambda b, t: (0, 0))],
        out_specs=tok,
        out_shape=jax.ShapeDtypeStruct((bsz, n_lat, d), F32),
        scratch_shapes=[pltpu.VMEM((tm, d), BF16)],
        compiler_params=_cparams(("arbitrary", "arbitrary")),
        name="merge",
    )(o_f, o_b, gp, hy, x, mod, dn_norm_g.reshape(1, HEAD_DIM), w_pa, w_pb, w_out, final_g.reshape(1, d))


def _position_features(n_lat):
    h1 = n_lat // FFT_INNER
    m = (np.arange(FFT_INNER)[:, None] + FFT_INNER * np.arange(h1)[None, :]).reshape(-1)
    lag = np.concatenate([m, n_lat - m]).astype(np.float64)
    lag[n_lat] = 0.0
    bands = (HY_EMB - 1) // 2
    t = (lag / (n_lat - 1))[:, None]
    wpos = (2.0 * math.pi / n_lat) * lag[:, None]
    fb = np.linspace(1e-4, bands - 1, bands)[None, :]
    pe = np.concatenate([t, np.cos(fb * wpos), -np.sin(fb * wpos)], axis=1)
    pe_pad = np.zeros((2 * n_lat, LANES), np.float32)
    pe_pad[:, :HY_EMB] = pe
    return pe_pad, np.ascontiguousarray(t, dtype=np.float32)


def kernel(x, c, ctx, c_ctx, w_mod, b_mod, norm_g, w_in, dn_conv_w, dn_a_log, dn_dt_bias, dn_norm_g,
           hy_conv_w, hy_conv_b, hy_f_w1, hy_f_b1, hy_f_w2, hy_f_b2, hy_f_w3, hy_f_b3, hy_f_wout,
           hy_f_freq, hy_bias, w_pa, w_pb, w_out, final_g):
    bsz, n_lat, d = x.shape
    n_ctx = ctx.shape[1]
    assert w_mod.shape[0] == 1, "single layer: the context stream is only read through its scan states"
    dn = HEADS * HEAD_DIM
    hy = hy_bias.shape[-1]
    assert n_lat % 256 == 0 and n_ctx % 256 == 0 and dn == d and hy == d

    cvec = jnp.zeros((SUBLANES, d), F32).at[:bsz].set(c).at[bsz].set(c_ctx)
    mod = _mod_call(cvec, w_mod[0], b_mod[0])

    tok = jnp.concatenate([x, ctx], axis=1)
    h = _hnorm_call(tok, mod, norm_g[0], n_lat, bsz)

    w = w_in[0].astype(BF16)
    o_qkv, o_gate, o_hy, o_gp = 0, 4 * dn, 4 * dn + 4 * HEADS, 4 * dn + 4 * HEADS + 3 * hy
    col3 = lambda a, off, n: jnp.transpose(a[:, off:off + 3 * n].reshape(a.shape[0], 3, n), (1, 0, 2))

    qkv = _qkv_call(h, col3(w, o_qkv, dn), col3(dn_conv_w[0], 0, dn), n_lat, n_ctx)

    wg = jnp.zeros((d, LANES), BF16).at[:, :4 * HEADS].set(w[:, o_gate:o_gate + 4 * HEADS])
    lane_pad = lambda a: jnp.zeros((1, LANES), F32).at[0, 2 * HEADS:4 * HEADS].set(a.reshape(-1))
    gates = _gates_call(h, wg, lane_pad(dn_a_log[0]), lane_pad(dn_dt_bias[0]))
    lt = n_lat + n_ctx
    gates_t = jnp.transpose(gates[:, :, :4 * HEADS].reshape(bsz, lt // CHUNK, CHUNK, 4 * HEADS), (0, 1, 3, 2))
    o_f, o_b = _scan_call(qkv, gates, gates_t, n_lat, n_ctx)

    hyp = _hyproj_call(h, col3(w, o_hy, hy), col3(hy_conv_w[0], 0, hy),
                       hy_conv_b[0].reshape(3, 1, hy), n_lat, n_ctx)
    w4 = jnp.stack([w[:, 3 * dn:4 * dn]] + [w[:, o_gp + i * d:o_gp + (i + 1) * d] for i in range(3)])
    gp = _gproj_call(h, w4, n_lat)

    pe, tcol = _position_features(n_lat)
    w1p = jnp.zeros((LANES, hy_f_w1.shape[-1]), F32).at[:HY_EMB].set(hy_f_w1[0])
    a3 = _hidden_call(jnp.asarray(pe), w1p, hy_f_b1[0], hy_f_w2[0], hy_f_b2[0], hy_f_w3[0], hy_f_b3[0],
                      hy_f_freq[0])
    deltas = np.abs(np.linspace(math.log(HY_DECAY_TARGET) / HY_SLOW_DECAY,
                                math.log(HY_DECAY_TARGET) / HY_FAST_DECAY, hy)).astype(np.float32)
    deltas2 = jnp.asarray(np.tile(deltas, 2)[None, :])
    wout2 = hy_f_wout[0].reshape(-1, 2, 2 * hy).transpose(1, 0, 2).astype(BF16)
    hfull, ssq = _ftime_call(a3, jnp.asarray(tcol), wout2, deltas2, n_lat)
    e1, e4, ef, g2, g3 = (jnp.asarray(t).astype(BF16) for t in _fft_tables(n_lat))
    hf = _ffft_call(hfull, ssq, ef, g2, n_lat)

    yh = _hyena_call(hyp, hf, e1, e4, g2, g3, hy_bias[0], n_lat)

    return _merge_call(o_f, o_b, gp, yh, x, mod, dn_norm_g[0], w_pa[0].astype(BF16), w_pb[0].astype(BF16),
                       w_out[0].astype(BF16), final_g)
```

```python
import functools
import math

import numpy as np
import jax
import jax.numpy as jnp
from jax import lax
from jax.experimental import pallas as pl
from jax.experimental.pallas import tpu as pltpu

F32 = jnp.float32
BF16 = jnp.bfloat16
HIGHEST = lax.Precision.HIGHEST

EPS = 1e-6
HEADS = 8
HEAD_DIM = 128
CHUNK = 64
GRID_W = 64
HY_EMB = 33
HY_DECAY_TARGET = 1e-2
HY_FAST_DECAY = 0.3
HY_SLOW_DECAY = 1.5

LANES = 128
SUBLANES = 8
FFT_INNER = 128
FFT_STEPS = 8
SLAB_PAD = 8
VMEM_LIMIT = 56 * 1024 * 1024


def _cparams(sem):
    return pltpu.CompilerParams(dimension_semantics=sem, vmem_limit_bytes=VMEM_LIMIT)


def _bdot(a, b):
    return jnp.dot(a.astype(BF16), b.astype(BF16), preferred_element_type=F32)


def _silu(x):
    return x * jax.nn.sigmoid(x)


def _mod_kernel(c_ref, w_ref, b_ref, o_ref):
    s = _silu(c_ref[...])
    o_ref[...] = jnp.dot(s, w_ref[...], precision=HIGHEST, preferred_element_type=F32) + b_ref[...]


def _mod_call(cvec, w_mod, b_mod):
    rows, d = cvec.shape
    n = w_mod.shape[1]
    tn = 1024
    return pl.pallas_call(
        _mod_kernel,
        grid=(n // tn,),
        in_specs=[pl.BlockSpec((rows, d), lambda j: (0, 0)),
                  pl.BlockSpec((d, tn), lambda j: (0, j)),
                  pl.BlockSpec((1, tn), lambda j: (0, j))],
        out_specs=pl.BlockSpec((rows, tn), lambda j: (0, j)),
        out_shape=jax.ShapeDtypeStruct((rows, n), F32),
        compiler_params=_cparams(("arbitrary",)),
        name="mod",
    )(cvec, w_mod, b_mod.reshape(1, n))


def _hnorm_kernel(tok_ref, mod_ref, g_ref, o_ref, *, n_lat, tm, d, ctx_row):
    b = pl.program_id(0)
    t = pl.program_id(1)
    row = jnp.where(t * tm >= n_lat, ctx_row, b)
    m = mod_ref[pl.ds(row, 1), :]
    x = tok_ref[0]
    y = x * lax.rsqrt(jnp.mean(x * x, axis=-1, keepdims=True) + EPS) * g_ref[...]
    o_ref[0] = (y * (1.0 + m[:, d:2 * d]) + m[:, 0:d]).astype(BF16)


def _hnorm_call(tok, mod, norm_g, n_lat, ctx_row):
    bsz, lt, d = tok.shape
    tm = 256
    return pl.pallas_call(
        functools.partial(_hnorm_kernel, n_lat=n_lat, tm=tm, d=d, ctx_row=ctx_row),
        grid=(bsz, lt // tm),
        in_specs=[pl.BlockSpec((1, tm, d), lambda b, t: (b, t, 0)),
                  pl.BlockSpec(mod.shape, lambda b, t: (0, 0)),
                  pl.BlockSpec((1, d), lambda b, t: (0, 0))],
        out_specs=pl.BlockSpec((1, tm, d), lambda b, t: (b, t, 0)),
        out_shape=jax.ShapeDtypeStruct((bsz, lt, d), BF16),
        compiler_params=_cparams(("arbitrary", "arbitrary")),
        name="hnorm",
    )(tok, mod, norm_g.reshape(1, d))


def _row_conv(z, w_ref, pos, rowlen):
    taps = w_ref.shape[0]
    pad = taps // 2
    tm = z.shape[0]
    out = None
    for j in range(taps):
        d = j - pad
        if d == 0:
            term = z * w_ref[j:j + 1, :]
        else:
            zs = pltpu.roll(z, (-d) % tm, axis=0)
            ok = (pos + d >= 0) & (pos + d < rowlen)
            term = jnp.where(ok, zs, 0.0) * w_ref[j:j + 1, :]
        out = term if out is None else out + term
    return out


def _row_positions(t, tm, n_lat, n_ctx):
    r = lax.broadcasted_iota(jnp.int32, (tm, 1), 0) + t * tm
    is_ctx = r >= n_lat
    pos = jnp.where(is_ctx, r - n_lat, lax.rem(r, GRID_W))
    rowlen = jnp.where(is_ctx, n_ctx, GRID_W)
    return pos, rowlen


def _qkv_kernel(h_ref, w_ref, cw_ref, o_ref, *, n_lat, n_ctx, tm):
    j = pl.program_id(0)
    t = pl.program_id(2)
    z = jnp.dot(h_ref[0], w_ref[0], preferred_element_type=F32)
    pos, rowlen = _row_positions(t, tm, n_lat, n_ctx)
    y = _silu(_row_conv(z, cw_ref.at[0], pos, rowlen))

    @pl.when(j < 2)
    def _():
        scale = jnp.where(j == 0, HEAD_DIM ** -0.5, 1.0)
        for hh in range(HEADS):
            yh = y[:, hh * HEAD_DIM:(hh + 1) * HEAD_DIM]
            nrm = jnp.sum(yh * yh, axis=-1, keepdims=True)
            o_ref[0, 0, :, hh * HEAD_DIM:(hh + 1) * HEAD_DIM] = (
                yh * (lax.rsqrt(nrm + EPS) * scale)).astype(BF16)

    @pl.when(j == 2)
    def _():
        o_ref[0, 0] = y.astype(BF16)


def _qkv_call(h, w3, cw3, n_lat, n_ctx):
    bsz, lt, d = h.shape
    nw = w3.shape[2]
    tm = 768 if lt % 768 == 0 else 256
    return pl.pallas_call(
        functools.partial(_qkv_kernel, n_lat=n_lat, n_ctx=n_ctx, tm=tm),
        grid=(3, bsz, lt // tm),
        in_specs=[pl.BlockSpec((1, tm, d), lambda j, b, t: (b, t, 0)),
                  pl.BlockSpec((1, d, nw), lambda j, b, t: (j, 0, 0)),
                  pl.BlockSpec((1, cw3.shape[1], nw), lambda j, b, t: (j, 0, 0))],
        out_specs=pl.BlockSpec((1, 1, tm, nw), lambda j, b, t: (j, b, t, 0)),
        out_shape=jax.ShapeDtypeStruct((3, bsz, lt, nw), BF16),
        compiler_params=_cparams(("arbitrary", "arbitrary", "arbitrary")),
        name="qkv",
    )(h, w3, cw3)


def _gates_kernel(h_ref, w_ref, alog_ref, dtb_ref, o_ref):
    z = jnp.dot(h_ref[0], w_ref[...], preferred_element_type=F32)
    lane = lax.broadcasted_iota(jnp.int32, z.shape, 1)
    beta = jax.nn.sigmoid(z)
    u = z + dtb_ref[...]
    softplus = jnp.maximum(u, 0.0) + jnp.log(1.0 + jnp.exp(-jnp.abs(u)))
    g = -jnp.exp(alog_ref[...]) * softplus
    o_ref[0] = jnp.where(lane < 2 * HEADS, beta, g)


def _gates_call(h, wg, alog, dtb):
    bsz, lt, d = h.shape
    tm = 256
    return pl.pallas_call(
        _gates_kernel,
        grid=(bsz, lt // tm),
        in_specs=[pl.BlockSpec((1, tm, d), lambda b, t: (b, t, 0)),
                  pl.BlockSpec((d, LANES), lambda b, t: (0, 0)),
                  pl.BlockSpec((1, LANES), lambda b, t: (0, 0)),
                  pl.BlockSpec((1, LANES), lambda b, t: (0, 0))],
        out_specs=pl.BlockSpec((1, tm, LANES), lambda b, t: (b, t, 0)),
        out_shape=jax.ShapeDtypeStruct((bsz, lt, LANES), F32),
        compiler_params=_cparams(("arbitrary", "arbitrary")),
        name="gates",
    )(h, wg, alog, dtb)


def _hyproj_kernel(h_ref, w_ref, cw_ref, cb_ref, o_ref, *, n_lat, n_ctx, tm):
    t = pl.program_id(2)
    z = jnp.dot(h_ref[0], w_ref[0], preferred_element_type=F32)
    pos, rowlen = _row_positions(t, tm, n_lat, n_ctx)
    o_ref[0, 0] = (_row_conv(z, cw_ref.at[0], pos, rowlen) + cb_ref[0]).astype(BF16)


def _hyproj_call(h, w3, cw3, cb3, n_lat, n_ctx):
    bsz, _, d = h.shape
    nw = w3.shape[2]
    tm = 512 if n_lat % 512 == 0 else 256
    return pl.pallas_call(
        functools.partial(_hyproj_kernel, n_lat=n_lat, n_ctx=n_ctx, tm=tm),
        grid=(3, bsz, n_lat // tm),
        in_specs=[pl.BlockSpec((1, tm, d), lambda j, b, t: (b, t, 0)),
                  pl.BlockSpec((1, d, nw), lambda j, b, t: (j, 0, 0)),
                  pl.BlockSpec((1, cw3.shape[1], nw), lambda j, b, t: (j, 0, 0)),
                  pl.BlockSpec((1, 1, nw), lambda j, b, t: (j, 0, 0))],
        out_specs=pl.BlockSpec((1, 1, tm, nw), lambda j, b, t: (j, b, t, 0)),
        out_shape=jax.ShapeDtypeStruct((3, bsz, n_lat, nw), BF16),
        compiler_params=_cparams(("arbitrary", "arbitrary", "arbitrary")),
        name="hyproj",
    )(h, w3, cw3, cb3)


def _gproj_kernel(h_ref, w_ref, o_ref):
    j = pl.program_id(0)
    z = jnp.dot(h_ref[0], w_ref[0], preferred_element_type=F32)
    s = jax.nn.sigmoid(z)
    o_ref[0, 0] = jnp.where(j < 2, z * s, s).astype(BF16)


def _gproj_call(h, w4, n_lat):
    bsz, _, d = h.shape
    nw = w4.shape[2]
    tm = 512 if n_lat % 512 == 0 else 256
    return pl.pallas_call(
        _gproj_kernel,
        grid=(4, bsz, n_lat // tm),
        in_specs=[pl.BlockSpec((1, tm, d), lambda j, b, t: (b, t, 0)),
                  pl.BlockSpec((1, d, nw), lambda j, b, t: (j, 0, 0))],
        out_specs=pl.BlockSpec((1, 1, tm, nw), lambda j, b, t: (j, b, t, 0)),
        out_shape=jax.ShapeDtypeStruct((4, bsz, n_lat, nw), BF16),
        compiler_params=_cparams(("arbitrary", "arbitrary", "arbitrary")),
        name="gproj",
    )(h, w4)


def _unit_tri_inverse(ls):
    c = ls[0].shape[0]
    ri = lax.broadcasted_iota(jnp.int32, (c, c), 0)
    ci = lax.broadcasted_iota(jnp.int32, (c, c), 1)
    same = lambda n: (ri // n) == (ci // n)
    eye = jnp.where(ri == ci, 1.0, 0.0)
    in2 = same(2)
    ts = [eye - jnp.where(in2, l, 0.0) for l in ls]
    n = 2
    while n < c:
        blk = same(2 * n) & jnp.logical_not(same(n))
        offs = [jnp.where(blk, l, 0.0).astype(BF16) for l in ls]
        prods = [_bdot(t, off) for t, off in zip(ts, offs)]
        ts = [t - _bdot(p, t) for p, t in zip(prods, ts)]
        n *= 2
    return ts


_NT = (((1,), (1,)), ((), ()))
_TN = (((0,), (0,)), ((), ()))


def _prep_kernel(qk_ref, g_ref, gt_ref, ta_ref, cf_ref, *, bsz):
    c = CHUNK
    ri = lax.broadcasted_iota(jnp.int32, (c, c), 0)
    ci = lax.broadcasted_iota(jnp.int32, (c, c), 1)
    tri_l = jnp.where(ri >= ci, 1.0, 0.0)
    tri_u = jnp.where(ri <= ci, 1.0, 0.0)
    incl = (ri >= ci, ri <= ci)
    strict = (ri > ci, ri < ci)
    lane = lax.broadcasted_iota(jnp.int32, (c, LANES), 1)
    fwd_lane = lane < 3 * HEADS
    hdot = functools.partial(jnp.dot, precision=HIGHEST, preferred_element_type=F32)

    gs = [g_ref[b] for b in range(bsz)]
    gts = [gt_ref[b, 0] for b in range(bsz)]
    gcol = [(hdot(tri_l, g), hdot(tri_u, g)) for g in gs]
    grow = [(hdot(gt, tri_u), hdot(gt, tri_l)) for gt in gts]
    for b in range(bsz):
        gc = jnp.where(fwd_lane, gcol[b][0], gcol[b][1])
        glast = jnp.where(fwd_lane[0:1], gcol[b][0][c - 1:c], gcol[b][1][0:1])
        cf_ref[0, b] = jnp.exp(gc)
        cf_ref[1, b] = jnp.exp(glast - gc)
        cf_ref[2, b] = jnp.broadcast_to(jnp.exp(glast), (c, LANES))

    bh = [(b, hh) for b in range(bsz) for hh in range(HEADS)]
    hs = lambda hh: slice(hh * HEAD_DIM, (hh + 1) * HEAD_DIM)
    ks = [qk_ref[1, b, :, hs(hh)] for b, hh in bh]
    qs = [qk_ref[0, b, :, hs(hh)] for b, hh in bh]
    kk = [lax.dot_general(k, k, _NT, preferred_element_type=F32) for k in ks]
    qk = [lax.dot_general(q, k, _NT, preferred_element_type=F32) for q, k in zip(qs, ks)]
    inst = [(d, b, hh, i) for d in range(2) for i, (b, hh) in enumerate(bh)]
    decs = []
    for d, b, hh, _ in inst:
        lg = 2 * HEADS + d * HEADS + hh
        diff = gcol[b][d][:, lg:lg + 1] - grow[b][d][lg:lg + 1, :]
        decs.append(jnp.exp(jnp.where(incl[d], diff, -jnp.inf)))
    ls = [jnp.where(strict[d], kk[i] * gs[b][:, d * HEADS + hh:d * HEADS + hh + 1] * dec, 0.0)
          for (d, b, hh, i), dec in zip(inst, decs)]
    ts = _unit_tri_inverse(ls)
    for (d, b, hh, i), t, dec in zip(inst, ts, decs):
        ta_ref[d, b, hh, 0:c, :] = t.astype(BF16)
        ta_ref[d, b, hh, c:2 * c, :] = (qk[i] * dec).astype(BF16)


def _prep_call(qkv, gates, gates_t):
    _, bsz, lt, dn = qkv.shape
    nc = lt // CHUNK
    return pl.pallas_call(
        functools.partial(_prep_kernel, bsz=bsz),
        grid=(nc,),
        in_specs=[pl.BlockSpec((2, bsz, CHUNK, dn), lambda i: (0, 0, i, 0)),
                  pl.BlockSpec((bsz, CHUNK, LANES), lambda i: (0, i, 0)),
                  pl.BlockSpec((bsz, 1, 4 * HEADS, CHUNK), lambda i: (0, i, 0, 0))],
        out_specs=[pl.BlockSpec((None, 2, bsz, HEADS, 2 * CHUNK, CHUNK), lambda i: (i, 0, 0, 0, 0, 0)),
                   pl.BlockSpec((3, bsz, CHUNK, LANES), lambda i: (0, 0, i, 0))],
        out_shape=[jax.ShapeDtypeStruct((nc, 2, bsz, HEADS, 2 * CHUNK, CHUNK), BF16),
                   jax.ShapeDtypeStruct((3, bsz, lt, LANES), F32)],
        compiler_params=_cparams(("arbitrary",)),
        name="prep",
    )(qkv, gates, gates_t)


def _scan_kernel(qf_ref, qb_ref, gf_ref, gb_ref, cff_ref, cfb_ref, taf_ref, tab_ref, of_ref, ob_ref, s_ref,
                 *, bsz):
    i = pl.program_id(0)

    @pl.when(i == 0)
    def _():
        s_ref[...] = jnp.zeros_like(s_ref)

    c = CHUNK
    dirs = ((qf_ref, gf_ref, cff_ref, taf_ref, of_ref), (qb_ref, gb_ref, cfb_ref, tab_ref, ob_ref))
    inst = [(d, b, hh) for d in range(2) for b in range(bsz) for hh in range(HEADS)]
    hs = lambda hh: slice(hh * HEAD_DIM, (hh + 1) * HEAD_DIM)
    qg_kg, vb, kd, el, s_old = [], [], [], [], []
    for d, b, hh in inst:
        q_ref, g_ref, cf_ref = dirs[d][0], dirs[d][1], dirs[d][2]
        lb = d * HEADS + hh
        lg = 2 * HEADS + lb
        beta = g_ref[b, :, lb:lb + 1]
        eg = cf_ref[0, b, :, lg:lg + 1]
        k = q_ref[1, b, :, hs(hh)].astype(F32)
        qg_kg.append(jnp.concatenate([q_ref[0, b, :, hs(hh)].astype(F32) * eg, k * (beta * eg)], axis=0).astype(BF16))
        vb.append(q_ref[2, b, :, hs(hh)].astype(F32) * beta)
        kd.append((k * cf_ref[1, b, :, lg:lg + 1]).astype(BF16))
        el.append(cf_ref[2, b, 0:1, lg:lg + 1])
        s_old.append(s_ref[d, b, hh])
    a1 = [_bdot(x, s) for x, s in zip(qg_kg, s_old)]
    v_new = [_bdot(dirs[d][3][b, hh, 0:c, :], r - a[c:]).astype(BF16)
             for (d, b, hh), r, a in zip(inst, vb, a1)]
    for (d, b, hh), a, vn, kdi, eli, s in zip(inst, a1, v_new, kd, el, s_old):
        o = a[:c] + jnp.dot(dirs[d][3][b, hh, c:2 * c, :], vn, preferred_element_type=F32)
        dirs[d][4][b, :, hs(hh)] = o.astype(BF16)
        s_ref[d, b, hh] = s * eli + lax.dot_general(kdi, vn, _TN, preferred_element_type=F32)


def _scan_call(qkv, gates, coef, ta, n_lat, n_ctx):
    _, bsz, lt, dn = qkv.shape
    ncx = n_lat // CHUNK
    ncc = n_ctx // CHUNK
    nc = ncx + ncc
    cf = lambda i: jnp.where(i < ncc, ncx + i, i - ncc)
    cb = lambda i: nc - 1 - i

    qspec = lambda fn: pl.BlockSpec((3, bsz, CHUNK, dn), lambda i: (0, 0, fn(i), 0))
    gspec = lambda fn: pl.BlockSpec((bsz, CHUNK, LANES), lambda i: (0, fn(i), 0))
    cspec = lambda fn: pl.BlockSpec((3, bsz, CHUNK, LANES), lambda i: (0, 0, fn(i), 0))
    tspec = lambda fn, d: pl.BlockSpec((None, None, bsz, HEADS, 2 * CHUNK, CHUNK),
                                       lambda i: (fn(i), d, 0, 0, 0, 0))
    ospec = lambda fn: pl.BlockSpec((bsz, CHUNK, dn), lambda i: (0, fn(i), 0))
    return pl.pallas_call(
        functools.partial(_scan_kernel, bsz=bsz),
        grid=(nc,),
        in_specs=[qspec(cf), qspec(cb), gspec(cf), gspec(cb), cspec(cf), cspec(cb), tspec(cf, 0), tspec(cb, 1)],
        out_specs=[ospec(cf), ospec(cb)],
        out_shape=[jax.ShapeDtypeStruct((bsz, lt, dn), BF16)] * 2,
        scratch_shapes=[pltpu.VMEM((2, bsz, HEADS, HEAD_DIM, HEAD_DIM), F32)],
        compiler_params=_cparams(("arbitrary",)),
        name="scan",
    )(qkv, qkv, gates, gates, coef, coef, ta, ta)


def _hidden_kernel(pe_ref, w1_ref, b1_ref, w2_ref, b2_ref, w3_ref, b3_ref, f_ref, o_ref):
    f = f_ref[...]
    dot = functools.partial(jnp.dot, precision=HIGHEST, preferred_element_type=F32)
    a = jnp.sin(f * (dot(pe_ref[...], w1_ref[...]) + b1_ref[...]))
    a = jnp.sin(f * (dot(a, w2_ref[...]) + b2_ref[...]))
    o_ref[...] = jnp.sin(f * (dot(a, w3_ref[...]) + b3_ref[...]))


def _hidden_call(pe, w1p, b1, w2, b2, w3, b3, freq):
    rows, pw = pe.shape
    fw = w2.shape[0]
    tr = 512
    full = lambda a: pl.BlockSpec(a.shape, lambda t: (0,) * a.ndim)
    args = (w1p, b1.reshape(1, fw), w2, b2.reshape(1, fw), w3, b3.reshape(1, fw), freq.reshape(1, fw))
    return pl.pallas_call(
        _hidden_kernel,
        grid=(rows // tr,),
        in_specs=[pl.BlockSpec((tr, pw), lambda t: (t, 0))] + [full(a) for a in args],
        out_specs=pl.BlockSpec((tr, fw), lambda t: (t, 0)),
        out_shape=jax.ShapeDtypeStruct((rows, fw), F32),
        compiler_params=_cparams(("arbitrary",)),
        name="hidden",
    )(pe, *args)


def _ftime_kernel(a_ref, t_ref, w_ref, dl_ref, h_ref, ss_ref, *, n_lat, tr):
    t = pl.program_id(0)
    h = _bdot(a_ref[...], w_ref[...]) * jnp.exp(-t_ref[...] * dl_ref[...])
    row = lax.broadcasted_iota(jnp.int32, (tr, 1), 0) + t * tr
    h = jnp.where(row == n_lat, 0.0, h)
    h_ref[...] = h

    @pl.when(t == 0)
    def _():
        ss_ref[...] = jnp.zeros_like(ss_ref)

    ss_ref[...] += jnp.sum(h * h, axis=0, keepdims=True)


def _ftime_call(a3, tcol, wout2, deltas2, n_lat):
    rows, fw = a3.shape
    ncol = wout2.shape[2]
    tr = 512
    side = lambda t: (t * tr) // n_lat
    return pl.pallas_call(
        functools.partial(_ftime_kernel, n_lat=n_lat, tr=tr),
        grid=(rows // tr,),
        in_specs=[pl.BlockSpec((tr, fw), lambda t: (t, 0)),
                  pl.BlockSpec((tr, 1), lambda t: (t, 0)),
                  pl.BlockSpec((None, fw, ncol), lambda t: (side(t), 0, 0)),
                  pl.BlockSpec((1, ncol), lambda t: (0, 0))],
        out_specs=[pl.BlockSpec((tr, ncol), lambda t: (t, 0)),
                   pl.BlockSpec((1, ncol), lambda t: (0, 0))],
        out_shape=[jax.ShapeDtypeStruct((rows, ncol), F32), jax.ShapeDtypeStruct((1, ncol), F32)],
        compiler_params=_cparams(("arbitrary",)),
        name="ftime",
    )(a3, tcol, wout2, deltas2)


@functools.lru_cache(maxsize=None)
def _fft_tables(n_lat):
    n = 2 * n_lat
    n1t = n // FFT_INNER
    h1 = n1t // 2
    n2 = np.arange(FFT_INNER, dtype=np.int64)[:, None, None]
    k1 = np.arange(n1t, dtype=np.int64)[None, :, None]
    n1 = np.arange(h1, dtype=np.int64)[None, None, :]
    ang = (-2.0 * np.pi / n) * ((k1 * (FFT_INNER * n1 + n2)) % n)
    er, ei = np.cos(ang), np.sin(ang)
    e1 = np.concatenate([np.concatenate([er, -ei], axis=2), np.concatenate([ei, er], axis=2)], axis=1)
    e4 = np.transpose(e1, (0, 2, 1))
    sgn = np.where(np.arange(n1t) % 2 == 0, 1.0, -1.0)[None, :, None]
    ef = np.concatenate([np.concatenate([er, sgn * er], axis=2), np.concatenate([ei, sgn * ei], axis=2)], axis=1)
    a = np.arange(FFT_INNER, dtype=np.int64)
    ang2 = (-2.0 * np.pi / FFT_INNER) * ((a[:, None] * a[None, :]) % FFT_INNER)
    gr, gi = np.cos(ang2), np.sin(ang2)
    g2 = np.block([[gr, -gi], [gi, gr]])
    f32 = lambda x: np.ascontiguousarray(x, dtype=np.float32)
    return f32(e1), f32(e4), f32(ef), f32(g2), f32(g2.T)


def _hold(p, j, first, last, nsteps):
    active = (p >= first) & (p <= last) & (lax.rem(p - first, 2) == 0)
    return jnp.where(active, j, jnp.where(p < first, 0, nsteps - 1))


def _spectrum_step(x_ref, g2_ref, k1, n1t, stride):
    re = x_ref[pl.ds(k1, FFT_INNER, stride=stride), :]
    im = x_ref[pl.ds(n1t + k1, FFT_INNER, stride=stride), :]
    return _bdot(g2_ref[...], jnp.concatenate([re, im], axis=0))


def _ffft_kernel(h_ref, ef_ref, g2_ref, ss_ref, o_ref, x_ref, *, n1t, grp, kgrp, stride, inv_n):
    p = pl.program_id(1)
    j = pl.program_id(2)

    @pl.when(p == 0)
    def _():
        for jj in range(grp):
            u = jnp.concatenate([h_ref[0, jj], h_ref[1, jj]], axis=0)
            base = pl.multiple_of((j * grp + jj) * stride, SUBLANES)
            x_ref[pl.ds(base, 2 * n1t), :] = _bdot(ef_ref[jj], u)

    @pl.when(p == 1)
    def _():
        scale = lax.rsqrt(ss_ref[...] + EPS) * inv_n
        for jj in range(kgrp):
            xf = _spectrum_step(x_ref, g2_ref, j * kgrp + jj, n1t, stride)
            o_ref[jj] = (xf * scale).astype(BF16)


def _fft_groups(n1t):
    ns = FFT_STEPS
    assert FFT_INNER % ns == 0 and n1t % ns == 0
    return ns, FFT_INNER // ns, n1t // ns


def _ffft_call(hfull, ssq, ef, g2, n_lat):
    ncol = hfull.shape[1]
    n1t = 2 * n_lat // FFT_INNER
    h1 = n1t // 2
    ct = LANES
    ns, grp, kgrp = _fft_groups(n1t)
    stride = 2 * n1t + SLAB_PAD
    hv = hfull.reshape(2, FFT_INNER, h1, ncol)
    nct = ncol // ct
    cpo = nct // 2
    return pl.pallas_call(
        functools.partial(_ffft_kernel, n1t=n1t, grp=grp, kgrp=kgrp, stride=stride, inv_n=1.0 / (2 * n_lat)),
        grid=(nct, 2, ns),
        in_specs=[pl.BlockSpec((2, grp, h1, ct), lambda c, p, j: (0, _hold(p, j, 0, 0, ns), 0, c)),
                  pl.BlockSpec((grp, 2 * n1t, n1t), lambda c, p, j: (_hold(p, j, 0, 0, ns), 0, 0)),
                  pl.BlockSpec((2 * FFT_INNER, 2 * FFT_INNER), lambda c, p, j: (0, 0)),
                  pl.BlockSpec((1, ct), lambda c, p, j: (0, c))],
        out_specs=pl.BlockSpec((None, kgrp, 2 * FFT_INNER, ct),
                               lambda c, p, j: (c // cpo, _hold(p, j, 1, 1, ns), 0, c % cpo)),
        out_shape=jax.ShapeDtypeStruct((2, n1t, 2 * FFT_INNER, ncol // 2), BF16),
        scratch_shapes=[pltpu.VMEM((FFT_INNER * stride, ct), F32)],
        compiler_params=_cparams(("arbitrary", "arbitrary", "arbitrary")),
        name="ffft",
    )(hv, ef, g2, ssq)


def _hyena_kernel(v_ref, x1_ref, x2_ref, hf_ref, e1_ref, e4_ref, g2_ref, g3_ref, bias_ref,
                  o_ref, x_ref, z_ref, *, n1t, grp, kgrp, stride):
    p = pl.program_id(1)
    j = pl.program_id(2)
    h1 = n1t // 2

    def gather(ref, jj):
        return jnp.concatenate([ref[0, jj], ref[1, jj]], axis=0).astype(F32)

    def slab_base(jj):
        return pl.multiple_of((j * grp + jj) * stride, SUBLANES)

    def spectrum_phase():
        for jj in range(kgrp):
            k1 = j * kgrp + jj
            xf = _spectrum_step(x_ref, g2_ref, k1, n1t, stride)
            hf = hf_ref[jj].astype(F32)
            xr, xi = xf[:FFT_INNER], xf[FFT_INNER:]
            hr, hi = hf[:FFT_INNER], hf[FFT_INNER:]
            y = jnp.concatenate([xr * hr - xi * hi, xr * hi + xi * hr], axis=0)
            bb = _bdot(g3_ref[...], y)
            x_ref[pl.ds(k1, FFT_INNER, stride=stride), :] = bb[:FFT_INNER]
            x_ref[pl.ds(n1t + k1, FFT_INNER, stride=stride), :] = bb[FFT_INNER:]

    @pl.when(p == 0)
    def _():
        for jj in range(grp):
            x_ref[pl.ds(slab_base(jj), 2 * n1t), :] = _bdot(e1_ref[jj], gather(v_ref, jj))

    @pl.when((p == 1) | (p == 3))
    def _():
        spectrum_phase()

    @pl.when(p == 2)
    def _():
        for jj in range(grp):
            y = _bdot(e4_ref[jj], x_ref[pl.ds(slab_base(jj), 2 * n1t), :])
            z1 = gather(x1_ref, jj) * (y + gather(v_ref, jj) * bias_ref[0:1, :])
            z_ref[j * grp + jj] = z1
            x_ref[pl.ds(slab_base(jj), 2 * n1t), :] = _bdot(e1_ref[jj], z1)

    @pl.when(p == 4)
    def _():
        for jj in range(grp):
            y = _bdot(e4_ref[jj], x_ref[pl.ds(slab_base(jj), 2 * n1t), :])
            out = gather(x2_ref, jj) * (y + z_ref[j * grp + jj] * bias_ref[1:2, :])
            o_ref[0, jj] = out[:h1].astype(o_ref.dtype)
            o_ref[1, jj] = out[h1:].astype(o_ref.dtype)


def _hyena_call(hyp, hf, e1, e4, g2, g3, bias, n_lat):
    _, bsz, _, ch = hyp.shape
    assert bsz == 2, "the two batch rows are packed as the real and imaginary parts of one FFT"
    n1t = 2 * n_lat // FFT_INNER
    h1 = n1t // 2
    ct = LANES
    ns, grp, kgrp = _fft_groups(n1t)
    stride = 2 * n1t + SLAB_PAD
    hv = jnp.transpose(hyp.reshape(3, bsz, h1, FFT_INNER, ch), (0, 1, 3, 2, 4))

    def hspec(which, first, last):
        return pl.BlockSpec((None, bsz, grp, h1, ct),
                            lambda c, p, j: (which, 0, _hold(p, j, first, last, ns), 0, c))

    def hf_map(c, p, j):
        order = jnp.where(p >= 2, 1, 0)
        step = jnp.where((p == 1) | (p == 3), j, jnp.where((p == 0) | (p == 2), 0, ns - 1))
        return (order, step, 0, c)

    out = pl.pallas_call(
        functools.partial(_hyena_kernel, n1t=n1t, grp=grp, kgrp=kgrp, stride=stride),
        grid=(ch // ct, 5, ns),
        in_specs=[hspec(0, 0, 2), hspec(1, 2, 2), hspec(2, 4, 4),
                  pl.BlockSpec((None, kgrp, 2 * FFT_INNER, ct), hf_map),
                  pl.BlockSpec((grp, 2 * n1t, n1t), lambda c, p, j: (_hold(p, j, 0, 2, ns), 0, 0)),
                  pl.BlockSpec((grp, n1t, 2 * n1t), lambda c, p, j: (_hold(p, j, 2, 4, ns), 0, 0)),
                  pl.BlockSpec((2 * FFT_INNER, 2 * FFT_INNER), lambda c, p, j: (0, 0)),
                  pl.BlockSpec((2 * FFT_INNER, 2 * FFT_INNER), lambda c, p, j: (0, 0)),
                  pl.BlockSpec((2, ct), lambda c, p, j: (0, c))],
        out_specs=pl.BlockSpec((bsz, grp, h1, ct), lambda c, p, j: (0, _hold(p, j, 4, 4, ns), 0, c)),
        out_shape=jax.ShapeDtypeStruct((bsz, FFT_INNER, h1, ch), BF16),
        scratch_shapes=[pltpu.VMEM((FFT_INNER * stride, ct), F32),
                        pltpu.VMEM((FFT_INNER, n1t, ct), F32)],
        compiler_params=_cparams(("arbitrary", "arbitrary", "arbitrary")),
        name="hyena",
    )(hv, hv, hv, hf, e1, e4, g2, g3, bias)
    return jnp.transpose(out, (0, 2, 1, 3)).reshape(bsz, n_lat, ch)


def _merge_kernel(of_ref, ob_ref, gp_ref, hy_ref, x_ref, mod_ref, dng_ref, wpa_ref, wpb_ref, wo_ref,
                  fg_ref, o_ref, oa_ref, *, d):
    b = pl.program_id(0)
    o = of_ref[0].astype(F32) + ob_ref[0].astype(F32)
    za = gp_ref[0, 0].astype(F32)
    for hh in range(HEADS):
        hs = slice(hh * HEAD_DIM, (hh + 1) * HEAD_DIM)
        oh = o[:, hs]
        ms = jnp.mean(oh * oh, axis=-1, keepdims=True)
        oa_ref[:, hs] = (oh * lax.rsqrt(ms + EPS) * dng_ref[...] * za[:, hs]).astype(BF16)
    y_a = jnp.dot(oa_ref[...], wpa_ref[...], preferred_element_type=F32)
    o_b = hy_ref[0].astype(F32) * gp_ref[1, 0].astype(F32)
    y_b = _bdot(o_b, wpb_ref[...])
    m = gp_ref[2, 0].astype(F32) * y_a + gp_ref[3, 0].astype(F32) * y_b
    y = _bdot(m, wo_ref[...])
    gate = mod_ref[pl.ds(b, 1), :][:, 2 * d:3 * d]
    xn = x_ref[0] + gate * y
    o_ref[0] = xn * lax.rsqrt(jnp.mean(xn * xn, axis=-1, keepdims=True) + EPS) * fg_ref[...]


def _merge_call(o_f, o_b, gp, hy, x, mod, dn_norm_g, w_pa, w_pb, w_out, final_g):
    bsz, n_lat, d = x.shape
    tm = 512 if n_lat % 512 == 0 else 256
    wfull = lambda a: pl.BlockSpec(a.shape, lambda b, t: (0,) * a.ndim)
    tok = pl.BlockSpec((1, tm, d), lambda b, t: (b, t, 0))
    return pl.pallas_call(
        functools.partial(_merge_kernel, d=d),
        grid=(bsz, n_lat // tm),
        in_specs=[tok, tok,
                  pl.BlockSpec((4, 1, tm, d), lambda b, t: (0, b, t, 0)),
                  tok, tok, wfull(mod),
                  pl.BlockSpec((1, HEAD_DIM), lambda b, t: (0, 0)),
                  wfull(w_pa), wfull(w_pb), wfull(w_out),
                  pl.BlockSpec((1, d), lambda b, t: (0, 0))],
        out_specs=tok,
        out_shape=jax.ShapeDtypeStruct((bsz, n_lat, d), F32),
        scratch_shapes=[pltpu.VMEM((tm, d), BF16)],
        compiler_params=_cparams(("arbitrary", "arbitrary")),
        name="merge",
    )(o_f, o_b, gp, hy, x, mod, dn_norm_g.reshape(1, HEAD_DIM), w_pa, w_pb, w_out, final_g.reshape(1, d))


def _position_features(n_lat):
    h1 = n_lat // FFT_INNER
    m = (np.arange(FFT_INNER)[:, None] + FFT_INNER * np.arange(h1)[None, :]).reshape(-1)
    lag = np.concatenate([m, n_lat - m]).astype(np.float64)
    lag[n_lat] = 0.0
    bands = (HY_EMB - 1) // 2
    t = (lag / (n_lat - 1))[:, None]
    wpos = (2.0 * math.pi / n_lat) * lag[:, None]
    fb = np.linspace(1e-4, bands - 1, bands)[None, :]
    pe = np.concatenate([t, np.cos(fb * wpos), -np.sin(fb * wpos)], axis=1)
    pe_pad = np.zeros((2 * n_lat, LANES), np.float32)
    pe_pad[:, :HY_EMB] = pe
    return pe_pad, np.ascontiguousarray(t, dtype=np.float32)


def kernel(x, c, ctx, c_ctx, w_mod, b_mod, norm_g, w_in, dn_conv_w, dn_a_log, dn_dt_bias, dn_norm_g,
           hy_conv_w, hy_conv_b, hy_f_w1, hy_f_b1, hy_f_w2, hy_f_b2, hy_f_w3, hy_f_b3, hy_f_wout,
           hy_f_freq, hy_bias, w_pa, w_pb, w_out, final_g):
    bsz, n_lat, d = x.shape
    n_ctx = ctx.shape[1]
    assert w_mod.shape[0] == 1, "single layer: the context stream is only read through its scan states"
    dn = HEADS * HEAD_DIM
    hy = hy_bias.shape[-1]
    assert n_lat % 256 == 0 and n_ctx % 256 == 0 and dn == d and hy == d

    cvec = jnp.zeros((SUBLANES, d), F32).at[:bsz].set(c).at[bsz].set(c_ctx)
    mod = _mod_call(cvec, w_mod[0], b_mod[0])

    tok = jnp.concatenate([x, ctx], axis=1)
    h = _hnorm_call(tok, mod, norm_g[0], n_lat, bsz)

    w = w_in[0].astype(BF16)
    o_qkv, o_gate, o_hy, o_gp = 0, 4 * dn, 4 * dn + 4 * HEADS, 4 * dn + 4 * HEADS + 3 * hy
    col3 = lambda a, off, n: jnp.transpose(a[:, off:off + 3 * n].reshape(a.shape[0], 3, n), (1, 0, 2))

    qkv = _qkv_call(h, col3(w, o_qkv, dn), col3(dn_conv_w[0], 0, dn), n_lat, n_ctx)

    wg = jnp.zeros((d, LANES), BF16).at[:, :4 * HEADS].set(w[:, o_gate:o_gate + 4 * HEADS])
    lane_pad = lambda a: jnp.zeros((1, LANES), F32).at[0, 2 * HEADS:4 * HEADS].set(a.reshape(-1))
    gates = _gates_call(h, wg, lane_pad(dn_a_log[0]), lane_pad(dn_dt_bias[0]))
    lt = n_lat + n_ctx
    gates_t = jnp.transpose(gates[:, :, :4 * HEADS].reshape(bsz, lt // CHUNK, CHUNK, 4 * HEADS), (0, 1, 3, 2))
    ta, coef = _prep_call(qkv, gates, gates_t)
    o_f, o_b = _scan_call(qkv, gates, coef, ta, n_lat, n_ctx)

    hyp = _hyproj_call(h, col3(w, o_hy, hy), col3(hy_conv_w[0], 0, hy),
                       hy_conv_b[0].reshape(3, 1, hy), n_lat, n_ctx)
    w4 = jnp.stack([w[:, 3 * dn:4 * dn]] + [w[:, o_gp + i * d:o_gp + (i + 1) * d] for i in range(3)])
    gp = _gproj_call(h, w4, n_lat)

    pe, tcol = _position_features(n_lat)
    w1p = jnp.zeros((LANES, hy_f_w1.shape[-1]), F32).at[:HY_EMB].set(hy_f_w1[0])
    a3 = _hidden_call(jnp.asarray(pe), w1p, hy_f_b1[0], hy_f_w2[0], hy_f_b2[0], hy_f_w3[0], hy_f_b3[0],
                      hy_f_freq[0])
    deltas = np.abs(np.linspace(math.log(HY_DECAY_TARGET) / HY_SLOW_DECAY,
                                math.log(HY_DECAY_TARGET) / HY_FAST_DECAY, hy)).astype(np.float32)
    deltas2 = jnp.asarray(np.tile(deltas, 2)[None, :])
    wout2 = hy_f_wout[0].reshape(-1, 2, 2 * hy).transpose(1, 0, 2).astype(BF16)
    hfull, ssq = _ftime_call(a3, jnp.asarray(tcol), wout2, deltas2, n_lat)
    e1, e4, ef, g2, g3 = (jnp.asarray(t).astype(BF16) for t in _fft_tables(n_lat))
    hf = _ffft_call(hfull, ssq, ef, g2, n_lat)

    yh = _hyena_call(hyp, hf, e1, e4, g2, g3, hy_bias[0], n_lat)

    return _merge_call(o_f, o_b, gp, yh, x, mod, dn_norm_g[0], w_pa[0].astype(BF16), w_pb[0].astype(BF16),
                       w_out[0].astype(BF16), final_g)
```

```python
import functools
import math

import numpy as np
import jax
import jax.numpy as jnp
from jax import lax
from jax.experimental import pallas as pl
from jax.experimental.pallas import tpu as pltpu

F32 = jnp.float32
BF16 = jnp.bfloat16
HIGHEST = lax.Precision.HIGHEST

EPS = 1e-6
HEADS = 8
HEAD_DIM = 128
CHUNK = 64
GRID_W = 64
HY_EMB = 33
HY_DECAY_TARGET = 1e-2
HY_FAST_DECAY = 0.3
HY_SLOW_DECAY = 1.5

LANES = 128
SUBLANES = 8
FFT_INNER = 128
FFT_STEPS = 8
PROJ_TILE = 1024
MXU_TILE = 256
SLAB_PAD = 8
VMEM_LIMIT = 56 * 1024 * 1024


def _cparams(sem):
    return pltpu.CompilerParams(dimension_semantics=sem, vmem_limit_bytes=VMEM_LIMIT)


def _bdot(a, b):
    return jnp.dot(a.astype(BF16), b.astype(BF16), preferred_element_type=F32)


def _silu(x):
    return x * jax.nn.sigmoid(x)


def _mod_kernel(c_ref, w_ref, b_ref, o_ref):
    s = _silu(c_ref[...])
    o_ref[...] = jnp.dot(s, w_ref[...], precision=HIGHEST, preferred_element_type=F32) + b_ref[...]


def _mod_call(cvec, w_mod, b_mod):
    rows, d = cvec.shape
    n = w_mod.shape[1]
    tn = 1024
    return pl.pallas_call(
        _mod_kernel,
        grid=(n // tn,),
        in_specs=[pl.BlockSpec((rows, d), lambda j: (0, 0)),
                  pl.BlockSpec((d, tn), lambda j: (0, j)),
                  pl.BlockSpec((1, tn), lambda j: (0, j))],
        out_specs=pl.BlockSpec((rows, tn), lambda j: (0, j)),
        out_shape=jax.ShapeDtypeStruct((rows, n), F32),
        compiler_params=_cparams(("arbitrary",)),
        name="mod",
    )(cvec, w_mod, b_mod.reshape(1, n))


def _hnorm_kernel(tok_ref, mod_ref, g_ref, o_ref, *, n_lat, tm, d, ctx_row):
    b = pl.program_id(0)
    t = pl.program_id(1)
    row = jnp.where(t * tm >= n_lat, ctx_row, b)
    m = mod_ref[pl.ds(row, 1), :]
    x = tok_ref[0]
    y = x * lax.rsqrt(jnp.mean(x * x, axis=-1, keepdims=True) + EPS) * g_ref[...]
    o_ref[0] = (y * (1.0 + m[:, d:2 * d]) + m[:, 0:d]).astype(BF16)


def _hnorm_call(tok, mod, norm_g, n_lat, ctx_row):
    bsz, lt, d = tok.shape
    tm = 256
    return pl.pallas_call(
        functools.partial(_hnorm_kernel, n_lat=n_lat, tm=tm, d=d, ctx_row=ctx_row),
        grid=(bsz, lt // tm),
        in_specs=[pl.BlockSpec((1, tm, d), lambda b, t: (b, t, 0)),
                  pl.BlockSpec(mod.shape, lambda b, t: (0, 0)),
                  pl.BlockSpec((1, d), lambda b, t: (0, 0))],
        out_specs=pl.BlockSpec((1, tm, d), lambda b, t: (b, t, 0)),
        out_shape=jax.ShapeDtypeStruct((bsz, lt, d), BF16),
        compiler_params=_cparams(("arbitrary", "arbitrary")),
        name="hnorm",
    )(tok, mod, norm_g.reshape(1, d))


@functools.lru_cache(maxsize=None)
def _shift_matrix(rowlen, taps, blk):
    t = np.arange(blk)
    mats = []
    for j in range(taps):
        d = j - taps // 2
        if d != 0:
            ok = ((t + d) >= 0) & ((t + d) < blk) & (t // rowlen == (t + d) // rowlen)
            s = np.zeros((blk, blk), np.float32)
            s[t[ok], t[ok] + d] = 1.0
            mats.append(s)
    return np.concatenate(mats, axis=1)


def _shift_block(rowlen, taps):
    return max(rowlen, MXU_TILE // (taps - 1) // rowlen * rowlen)


def _convproj_kernel(h_ref, w_ref, cw_ref, cb_ref, sh_ref, *rest, tm, qk_norm):
    o_ref = rest[-1]
    j = pl.program_id(0)
    taps = cw_ref.shape[1]
    mid = taps // 2
    blk = sh_ref.shape[0]
    mblk = max(blk, MXU_TILE)
    if qk_norm:
        scale = jnp.where(j == 0, HEAD_DIM ** -0.5, 1.0)
        is_qk = j < 2
    for r in range(tm // mblk):
        zm = jnp.dot(h_ref[0, r * mblk:(r + 1) * mblk, :], w_ref[0], preferred_element_type=F32)
        for g in range(mblk // blk):
            rows = slice(r * mblk + g * blk, r * mblk + (g + 1) * blk)
            zb = zm[g * blk:(g + 1) * blk]
            side = jnp.concatenate(
                [(zb * cw_ref[0, jt:jt + 1, :]).astype(BF16) for jt in range(taps) if jt != mid], axis=0)
            y = zb * cw_ref[0, mid:mid + 1, :] + jnp.dot(sh_ref[...], side, preferred_element_type=F32)
            if not qk_norm:
                o_ref[0, 0, rows, :] = (y + cb_ref[0]).astype(o_ref.dtype)
                continue
            y = _silu(y)
            for hh in range(HEADS):
                yh = y[:, hh * HEAD_DIM:(hh + 1) * HEAD_DIM]
                nrm = jnp.sum(yh * yh, axis=-1, keepdims=True)
                f = jnp.where(is_qk, lax.rsqrt(nrm + EPS) * scale, 1.0)
                o_ref[0, 0, rows, hh * HEAD_DIM:(hh + 1) * HEAD_DIM] = (yh * f).astype(o_ref.dtype)


def _convproj_call(h, w3, cw3, cb3, *, rowlen, tm, tile0, ntiles, out_rows, qk_norm, prev=None, name):
    bsz, _, d = h.shape
    nw = w3.shape[2]
    taps = cw3.shape[1]
    blk = _shift_block(rowlen, taps)
    assert tm % max(blk, MXU_TILE) == 0 and max(blk, MXU_TILE) % blk == 0
    sh = jnp.asarray(_shift_matrix(rowlen, taps, blk)).astype(BF16)
    in_specs = [pl.BlockSpec((1, tm, d), lambda j, b, t: (b, t + tile0, 0)),
                pl.BlockSpec((1, d, nw), lambda j, b, t: (j, 0, 0)),
                pl.BlockSpec((1, cw3.shape[1], nw), lambda j, b, t: (j, 0, 0)),
                pl.BlockSpec((1, 1, nw), lambda j, b, t: (j, 0, 0)),
                pl.BlockSpec(sh.shape, lambda j, b, t: (0, 0))]
    args = [h, w3, cw3, cb3, sh]
    aliases = {}
    if prev is not None:
        in_specs.append(pl.BlockSpec(memory_space=pl.ANY))
        args.append(prev)
        aliases = {5: 0}
    return pl.pallas_call(
        functools.partial(_convproj_kernel, tm=tm, qk_norm=qk_norm),
        grid=(3, bsz, ntiles),
        in_specs=in_specs,
        out_specs=pl.BlockSpec((1, 1, tm, nw), lambda j, b, t: (j, b, t + tile0, 0)),
        out_shape=jax.ShapeDtypeStruct((3, bsz, out_rows, nw), BF16),
        input_output_aliases=aliases,
        compiler_params=_cparams(("arbitrary", "arbitrary", "arbitrary")),
        name=name,
    )(*args)


def _gates_kernel(h_ref, w_ref, alog_ref, dtb_ref, o_ref):
    z = jnp.dot(h_ref[0], w_ref[...], preferred_element_type=F32)
    lane = lax.broadcasted_iota(jnp.int32, z.shape, 1)
    beta = jax.nn.sigmoid(z)
    u = z + dtb_ref[...]
    softplus = jnp.maximum(u, 0.0) + jnp.log(1.0 + jnp.exp(-jnp.abs(u)))
    g = -jnp.exp(alog_ref[...]) * softplus
    o_ref[0] = jnp.where(lane < 2 * HEADS, beta, g)


def _gates_call(h, wg, alog, dtb):
    bsz, lt, d = h.shape
    tm = 256
    return pl.pallas_call(
        _gates_kernel,
        grid=(bsz, lt // tm),
        in_specs=[pl.BlockSpec((1, tm, d), lambda b, t: (b, t, 0)),
                  pl.BlockSpec((d, LANES), lambda b, t: (0, 0)),
                  pl.BlockSpec((1, LANES), lambda b, t: (0, 0)),
                  pl.BlockSpec((1, LANES), lambda b, t: (0, 0))],
        out_specs=pl.BlockSpec((1, tm, LANES), lambda b, t: (b, t, 0)),
        out_shape=jax.ShapeDtypeStruct((bsz, lt, LANES), F32),
        compiler_params=_cparams(("arbitrary", "arbitrary")),
        name="gates",
    )(h, wg, alog, dtb)


def _gproj_kernel(h_ref, w_ref, o_ref):
    j = pl.program_id(0)
    for r in range(h_ref.shape[1] // MXU_TILE):
        rows = slice(r * MXU_TILE, (r + 1) * MXU_TILE)
        z = jnp.dot(h_ref[0, rows, :], w_ref[0], preferred_element_type=F32)
        s = jax.nn.sigmoid(z)
        o_ref[0, 0, rows, :] = jnp.where(j < 2, z * s, s).astype(BF16)


def _gproj_call(h, w4, n_lat):
    bsz, _, d = h.shape
    nw = w4.shape[2]
    tm = PROJ_TILE if n_lat % PROJ_TILE == 0 else 256
    return pl.pallas_call(
        _gproj_kernel,
        grid=(4, bsz, n_lat // tm),
        in_specs=[pl.BlockSpec((1, tm, d), lambda j, b, t: (b, t, 0)),
                  pl.BlockSpec((1, d, nw), lambda j, b, t: (j, 0, 0))],
        out_specs=pl.BlockSpec((1, 1, tm, nw), lambda j, b, t: (j, b, t, 0)),
        out_shape=jax.ShapeDtypeStruct((4, bsz, n_lat, nw), BF16),
        compiler_params=_cparams(("arbitrary", "arbitrary", "arbitrary")),
        name="gproj",
    )(h, w4)


def _unit_tri_inverse(ls):
    c = ls[0].shape[0]
    ri = lax.broadcasted_iota(jnp.int32, (c, c), 0)
    ci = lax.broadcasted_iota(jnp.int32, (c, c), 1)
    same = lambda n: (ri // n) == (ci // n)
    eye = jnp.where(ri == ci, 1.0, 0.0)
    in2 = same(2)
    ts = [eye - jnp.where(in2, l, 0.0) for l in ls]
    n = 2
    while n < c:
        blk = same(2 * n) & jnp.logical_not(same(n))
        offs = [jnp.where(blk, l, 0.0).astype(BF16) for l in ls]
        prods = [_bdot(t, off) for t, off in zip(ts, offs)]
        ts = [t - _bdot(p, t) for p, t in zip(prods, ts)]
        n *= 2
    return ts


_NT = (((1,), (1,)), ((), ()))
_TN = (((0,), (0,)), ((), ()))


def _prep_kernel(qk_ref, g_ref, gt_ref, ta_ref, cf_ref, *, bsz):
    c = CHUNK
    ri = lax.broadcasted_iota(jnp.int32, (c, c), 0)
    ci = lax.broadcasted_iota(jnp.int32, (c, c), 1)
    tri_l = jnp.where(ri >= ci, 1.0, 0.0)
    tri_u = jnp.where(ri <= ci, 1.0, 0.0)
    incl = (ri >= ci, ri <= ci)
    strict = (ri > ci, ri < ci)
    lane = lax.broadcasted_iota(jnp.int32, (c, LANES), 1)
    fwd_lane = lane < 3 * HEADS
    hdot = functools.partial(jnp.dot, precision=HIGHEST, preferred_element_type=F32)

    gs = [g_ref[b] for b in range(bsz)]
    gts = [gt_ref[b, 0] for b in range(bsz)]
    gcol = [(hdot(tri_l, g), hdot(tri_u, g)) for g in gs]
    grow = [(hdot(gt, tri_u), hdot(gt, tri_l)) for gt in gts]
    for b in range(bsz):
        gc = jnp.where(fwd_lane, gcol[b][0], gcol[b][1])
        glast = jnp.where(fwd_lane[0:1], gcol[b][0][c - 1:c], gcol[b][1][0:1])
        cf_ref[0, b] = jnp.exp(gc)
        cf_ref[1, b] = jnp.exp(glast - gc)
        cf_ref[2, b] = jnp.broadcast_to(jnp.exp(glast), (c, LANES))

    bh = [(b, hh) for b in range(bsz) for hh in range(HEADS)]
    hs = lambda hh: slice(hh * HEAD_DIM, (hh + 1) * HEAD_DIM)
    ks = [qk_ref[1, b, :, hs(hh)] for b, hh in bh]
    qs = [qk_ref[0, b, :, hs(hh)] for b, hh in bh]
    kk = [lax.dot_general(k, k, _NT, preferred_element_type=F32) for k in ks]
    qk = [lax.dot_general(q, k, _NT, preferred_element_type=F32) for q, k in zip(qs, ks)]
    inst = [(d, b, hh, i) for d in range(2) for i, (b, hh) in enumerate(bh)]
    decs = []
    for d, b, hh, _ in inst:
        lg = 2 * HEADS + d * HEADS + hh
        diff = gcol[b][d][:, lg:lg + 1] - grow[b][d][lg:lg + 1, :]
        decs.append(jnp.exp(jnp.where(incl[d], diff, -jnp.inf)))
    ls = [jnp.where(strict[d], kk[i] * gs[b][:, d * HEADS + hh:d * HEADS + hh + 1] * dec, 0.0)
          for (d, b, hh, i), dec in zip(inst, decs)]
    ts = _unit_tri_inverse(ls)
    for (d, b, hh, i), t, dec in zip(inst, ts, decs):
        ta_ref[d, b, hh, 0:c, :] = t.astype(BF16)
        ta_ref[d, b, hh, c:2 * c, :] = (qk[i] * dec).astype(BF16)


def _prep_call(qkv, gates, gates_t):
    _, bsz, lt, dn = qkv.shape
    nc = lt // CHUNK
    return pl.pallas_call(
        functools.partial(_prep_kernel, bsz=bsz),
        grid=(nc,),
        in_specs=[pl.BlockSpec((2, bsz, CHUNK, dn), lambda i: (0, 0, i, 0)),
                  pl.BlockSpec((bsz, CHUNK, LANES), lambda i: (0, i, 0)),
                  pl.BlockSpec((bsz, 1, 4 * HEADS, CHUNK), lambda i: (0, i, 0, 0))],
        out_specs=[pl.BlockSpec((None, 2, bsz, HEADS, 2 * CHUNK, CHUNK), lambda i: (i, 0, 0, 0, 0, 0)),
                   pl.BlockSpec((3, bsz, CHUNK, LANES), lambda i: (0, 0, i, 0))],
        out_shape=[jax.ShapeDtypeStruct((nc, 2, bsz, HEADS, 2 * CHUNK, CHUNK), BF16),
                   jax.ShapeDtypeStruct((3, bsz, lt, LANES), F32)],
        compiler_params=_cparams(("arbitrary",)),
        name="prep",
    )(qkv, gates, gates_t)


def _scan_kernel(qf_ref, qb_ref, gf_ref, gb_ref, cff_ref, cfb_ref, taf_ref, tab_ref, of_ref, ob_ref, s_ref,
                 *, bsz):
    i = pl.program_id(0)

    @pl.when(i == 0)
    def _():
        s_ref[...] = jnp.zeros_like(s_ref)

    c = CHUNK
    dirs = ((qf_ref, gf_ref, cff_ref, taf_ref, of_ref), (qb_ref, gb_ref, cfb_ref, tab_ref, ob_ref))
    inst = [(d, b, hh) for d in range(2) for b in range(bsz) for hh in range(HEADS)]
    hs = lambda hh: slice(hh * HEAD_DIM, (hh + 1) * HEAD_DIM)
    qg_kg, vb, kd, el, s_old = [], [], [], [], []
    for d, b, hh in inst:
        q_ref, g_ref, cf_ref = dirs[d][0], dirs[d][1], dirs[d][2]
        lb = d * HEADS + hh
        lg = 2 * HEADS + lb
        beta = g_ref[b, :, lb:lb + 1]
        eg = cf_ref[0, b, :, lg:lg + 1]
        k = q_ref[1, b, :, hs(hh)].astype(F32)
        qg_kg.append(jnp.concatenate([q_ref[0, b, :, hs(hh)].astype(F32) * eg, k * (beta * eg)], axis=0).astype(BF16))
        vb.append(q_ref[2, b, :, hs(hh)].astype(F32) * beta)
        kd.append((k * cf_ref[1, b, :, lg:lg + 1]).astype(BF16))
        el.append(cf_ref[2, b, 0:1, lg:lg + 1])
        s_old.append(s_ref[d, b, hh])
    a1 = [_bdot(x, s) for x, s in zip(qg_kg, s_old)]
    v_new = [_bdot(dirs[d][3][b, hh, 0:c, :], r - a[c:]).astype(BF16)
             for (d, b, hh), r, a in zip(inst, vb, a1)]
    for (d, b, hh), a, vn, kdi, eli, s in zip(inst, a1, v_new, kd, el, s_old):
        o = a[:c] + jnp.dot(dirs[d][3][b, hh, c:2 * c, :], vn, preferred_element_type=F32)
        dirs[d][4][b, :, hs(hh)] = o.astype(BF16)
        s_ref[d, b, hh] = s * eli + lax.dot_general(kdi, vn, _TN, preferred_element_type=F32)


def _scan_call(qkv, gates, coef, ta, n_lat, n_ctx):
    _, bsz, lt, dn = qkv.shape
    ncx = n_lat // CHUNK
    ncc = n_ctx // CHUNK
    nc = ncx + ncc
    cf = lambda i: jnp.where(i < ncc, ncx + i, i - ncc)
    cb = lambda i: nc - 1 - i

    qspec = lambda fn: pl.BlockSpec((3, bsz, CHUNK, dn), lambda i: (0, 0, fn(i), 0))
    gspec = lambda fn: pl.BlockSpec((bsz, CHUNK, LANES), lambda i: (0, fn(i), 0))
    cspec = lambda fn: pl.BlockSpec((3, bsz, CHUNK, LANES), lambda i: (0, 0, fn(i), 0))
    tspec = lambda fn, d: pl.BlockSpec((None, None, bsz, HEADS, 2 * CHUNK, CHUNK),
                                       lambda i: (fn(i), d, 0, 0, 0, 0))
    ospec = lambda fn: pl.BlockSpec((bsz, CHUNK, dn), lambda i: (0, fn(i), 0))
    return pl.pallas_call(
        functools.partial(_scan_kernel, bsz=bsz),
        grid=(nc,),
        in_specs=[qspec(cf), qspec(cb), gspec(cf), gspec(cb), cspec(cf), cspec(cb), tspec(cf, 0), tspec(cb, 1)],
        out_specs=[ospec(cf), ospec(cb)],
        out_shape=[jax.ShapeDtypeStruct((bsz, lt, dn), BF16)] * 2,
        scratch_shapes=[pltpu.VMEM((2, bsz, HEADS, HEAD_DIM, HEAD_DIM), F32)],
        compiler_params=_cparams(("arbitrary",)),
        name="scan",
    )(qkv, qkv, gates, gates, coef, coef, ta, ta)


def _hidden_kernel(pe_ref, w1_ref, b1_ref, w2_ref, b2_ref, w3_ref, b3_ref, f_ref, o_ref):
    f = f_ref[...]
    dot = functools.partial(jnp.dot, precision=HIGHEST, preferred_element_type=F32)
    a = jnp.sin(f * (dot(pe_ref[...], w1_ref[...]) + b1_ref[...]))
    a = jnp.sin(f * (dot(a, w2_ref[...]) + b2_ref[...]))
    o_ref[...] = jnp.sin(f * (dot(a, w3_ref[...]) + b3_ref[...]))


def _hidden_call(pe, w1p, b1, w2, b2, w3, b3, freq):
    rows, pw = pe.shape
    fw = w2.shape[0]
    tr = 512
    full = lambda a: pl.BlockSpec(a.shape, lambda t: (0,) * a.ndim)
    args = (w1p, b1.reshape(1, fw), w2, b2.reshape(1, fw), w3, b3.reshape(1, fw), freq.reshape(1, fw))
    return pl.pallas_call(
        _hidden_kernel,
        grid=(rows // tr,),
        in_specs=[pl.BlockSpec((tr, pw), lambda t: (t, 0))] + [full(a) for a in args],
        out_specs=pl.BlockSpec((tr, fw), lambda t: (t, 0)),
        out_shape=jax.ShapeDtypeStruct((rows, fw), F32),
        compiler_params=_cparams(("arbitrary",)),
        name="hidden",
    )(pe, *args)


def _ftime_kernel(a_ref, t_ref, w_ref, dl_ref, h_ref, ss_ref, *, n_lat, tr):
    t = pl.program_id(0)
    h = _bdot(a_ref[...], w_ref[...]) * jnp.exp(-t_ref[...] * dl_ref[...])
    row = lax.broadcasted_iota(jnp.int32, (tr, 1), 0) + t * tr
    h = jnp.where(row == n_lat, 0.0, h)
    h_ref[...] = h

    @pl.when(t == 0)
    def _():
        ss_ref[...] = jnp.zeros_like(ss_ref)

    ss_ref[...] += jnp.sum(h * h, axis=0, keepdims=True)


def _ftime_call(a3, tcol, wout2, deltas2, n_lat):
    rows, fw = a3.shape
    ncol = wout2.shape[2]
    tr = 512
    side = lambda t: (t * tr) // n_lat
    return pl.pallas_call(
        functools.partial(_ftime_kernel, n_lat=n_lat, tr=tr),
        grid=(rows // tr,),
        in_specs=[pl.BlockSpec((tr, fw), lambda t: (t, 0)),
                  pl.BlockSpec((tr, 1), lambda t: (t, 0)),
                  pl.BlockSpec((None, fw, ncol), lambda t: (side(t), 0, 0)),
                  pl.BlockSpec((1, ncol), lambda t: (0, 0))],
        out_specs=[pl.BlockSpec((tr, ncol), lambda t: (t, 0)),
                   pl.BlockSpec((1, ncol), lambda t: (0, 0))],
        out_shape=[jax.ShapeDtypeStruct((rows, ncol), F32), jax.ShapeDtypeStruct((1, ncol), F32)],
        compiler_params=_cparams(("arbitrary",)),
        name="ftime",
    )(a3, tcol, wout2, deltas2)


@functools.lru_cache(maxsize=None)
def _fft_tables(n_lat):
    n = 2 * n_lat
    n1t = n // FFT_INNER
    h1 = n1t // 2
    n2 = np.arange(FFT_INNER, dtype=np.int64)[:, None, None]
    k1 = np.arange(n1t, dtype=np.int64)[None, :, None]
    n1 = np.arange(h1, dtype=np.int64)[None, None, :]
    ang = (-2.0 * np.pi / n) * ((k1 * (FFT_INNER * n1 + n2)) % n)
    er, ei = np.cos(ang), np.sin(ang)
    e1 = np.concatenate([np.concatenate([er, -ei], axis=2), np.concatenate([ei, er], axis=2)], axis=1)
    e4 = np.transpose(e1, (0, 2, 1))
    sgn = np.where(np.arange(n1t) % 2 == 0, 1.0, -1.0)[None, :, None]
    ef = np.concatenate([np.concatenate([er, sgn * er], axis=2), np.concatenate([ei, sgn * ei], axis=2)], axis=1)
    a = np.arange(FFT_INNER, dtype=np.int64)
    ang2 = (-2.0 * np.pi / FFT_INNER) * ((a[:, None] * a[None, :]) % FFT_INNER)
    gr, gi = np.cos(ang2), np.sin(ang2)
    g2 = np.block([[gr, -gi], [gi, gr]])
    f32 = lambda x: np.ascontiguousarray(x, dtype=np.float32)
    return f32(e1), f32(e4), f32(ef), f32(g2), f32(g2.T)


def _hold(p, j, first, last, nsteps):
    active = (p >= first) & (p <= last) & (lax.rem(p - first, 2) == 0)
    return jnp.where(active, j, jnp.where(p < first, 0, nsteps - 1))


def _spectrum_step(x_ref, g2_ref, k1, n1t, stride):
    re = x_ref[pl.ds(k1, FFT_INNER, stride=stride), :]
    im = x_ref[pl.ds(n1t + k1, FFT_INNER, stride=stride), :]
    return _bdot(g2_ref[...], jnp.concatenate([re, im], axis=0))


def _ffft_kernel(h_ref, ef_ref, g2_ref, ss_ref, o_ref, x_ref, *, n1t, grp, kgrp, stride, inv_n):
    p = pl.program_id(1)
    j = pl.program_id(2)

    @pl.when(p == 0)
    def _():
        for jj in range(grp):
            u = jnp.concatenate([h_ref[0, jj], h_ref[1, jj]], axis=0)
            base = pl.multiple_of((j * grp + jj) * stride, SUBLANES)
            x_ref[pl.ds(base, 2 * n1t), :] = _bdot(ef_ref[jj], u)

    @pl.when(p == 1)
    def _():
        scale = lax.rsqrt(ss_ref[...] + EPS) * inv_n
        for jj in range(kgrp):
            xf = _spectrum_step(x_ref, g2_ref, j * kgrp + jj, n1t, stride)
            o_ref[jj] = (xf * scale).astype(BF16)


def _fft_groups(n1t):
    ns = FFT_STEPS
    assert FFT_INNER % ns == 0 and n1t % ns == 0
    return ns, FFT_INNER // ns, n1t // ns


def _ffft_call(hfull, ssq, ef, g2, n_lat):
    ncol = hfull.shape[1]
    n1t = 2 * n_lat // FFT_INNER
    h1 = n1t // 2
    ct = LANES
    ns, grp, kgrp = _fft_groups(n1t)
    stride = 2 * n1t + SLAB_PAD
    hv = hfull.reshape(2, FFT_INNER, h1, ncol)
    nct = ncol // ct
    cpo = nct // 2
    return pl.pallas_call(
        functools.partial(_ffft_kernel, n1t=n1t, grp=grp, kgrp=kgrp, stride=stride, inv_n=1.0 / (2 * n_lat)),
        grid=(nct, 2, ns),
        in_specs=[pl.BlockSpec((2, grp, h1, ct), lambda c, p, j: (0, _hold(p, j, 0, 0, ns), 0, c)),
                  pl.BlockSpec((grp, 2 * n1t, n1t), lambda c, p, j: (_hold(p, j, 0, 0, ns), 0, 0)),
                  pl.BlockSpec((2 * FFT_INNER, 2 * FFT_INNER), lambda c, p, j: (0, 0)),
                  pl.BlockSpec((1, ct), lambda c, p, j: (0, c))],
        out_specs=pl.BlockSpec((None, kgrp, 2 * FFT_INNER, ct),
                               lambda c, p, j: (c // cpo, _hold(p, j, 1, 1, ns), 0, c % cpo)),
        out_shape=jax.ShapeDtypeStruct((2, n1t, 2 * FFT_INNER, ncol // 2), BF16),
        scratch_shapes=[pltpu.VMEM((FFT_INNER * stride, ct), F32)],
        compiler_params=_cparams(("arbitrary", "arbitrary", "arbitrary")),
        name="ffft",
    )(hv, ef, g2, ssq)


def _hyena_kernel(v_ref, x1_ref, x2_ref, hf_ref, e1_ref, e4_ref, g2_ref, g3_ref, bias_ref,
                  o_ref, x_ref, z_ref, *, n1t, grp, kgrp, stride):
    p = pl.program_id(1)
    j = pl.program_id(2)
    h1 = n1t // 2

    def gather(ref, jj):
        return jnp.concatenate([ref[0, jj], ref[1, jj]], axis=0).astype(F32)

    def slab_base(jj):
        return pl.multiple_of((j * grp + jj) * stride, SUBLANES)

    def spectrum_phase():
        for jj in range(kgrp):
            k1 = j * kgrp + jj
            xf = _spectrum_step(x_ref, g2_ref, k1, n1t, stride)
            hf = hf_ref[jj].astype(F32)
            xr, xi = xf[:FFT_INNER], xf[FFT_INNER:]
            hr, hi = hf[:FFT_INNER], hf[FFT_INNER:]
            y = jnp.concatenate([xr * hr - xi * hi, xr * hi + xi * hr], axis=0)
            bb = _bdot(g3_ref[...], y)
            x_ref[pl.ds(k1, FFT_INNER, stride=stride), :] = bb[:FFT_INNER]
            x_ref[pl.ds(n1t + k1, FFT_INNER, stride=stride), :] = bb[FFT_INNER:]

    @pl.when(p == 0)
    def _():
        for jj in range(grp):
            x_ref[pl.ds(slab_base(jj), 2 * n1t), :] = _bdot(e1_ref[jj], gather(v_ref, jj))

    @pl.when((p == 1) | (p == 3))
    def _():
        spectrum_phase()

    @pl.when(p == 2)
    def _():
        for jj in range(grp):
            y = _bdot(e4_ref[jj], x_ref[pl.ds(slab_base(jj), 2 * n1t), :])
            z1 = gather(x1_ref, jj) * (y + gather(v_ref, jj) * bias_ref[0:1, :])
            z_ref[j * grp + jj] = z1
            x_ref[pl.ds(slab_base(jj), 2 * n1t), :] = _bdot(e1_ref[jj], z1)

    @pl.when(p == 4)
    def _():
        for jj in range(grp):
            y = _bdot(e4_ref[jj], x_ref[pl.ds(slab_base(jj), 2 * n1t), :])
            out = gather(x2_ref, jj) * (y + z_ref[j * grp + jj] * bias_ref[1:2, :])
            o_ref[0, jj] = out[:h1].astype(o_ref.dtype)
            o_ref[1, jj] = out[h1:].astype(o_ref.dtype)


def _hyena_call(hyp, hf, e1, e4, g2, g3, bias, n_lat):
    _, bsz, _, ch = hyp.shape
    assert bsz == 2, "the two batch rows are packed as the real and imaginary parts of one FFT"
    n1t = 2 * n_lat // FFT_INNER
    h1 = n1t // 2
    ct = LANES
    ns, grp, kgrp = _fft_groups(n1t)
    stride = 2 * n1t + SLAB_PAD
    hv = jnp.transpose(hyp.reshape(3, bsz, h1, FFT_INNER, ch), (0, 1, 3, 2, 4))

    def hspec(which, first, last):
        return pl.BlockSpec((None, bsz, grp, h1, ct),
                            lambda c, p, j: (which, 0, _hold(p, j, first, last, ns), 0, c))

    def hf_map(c, p, j):
        order = jnp.where(p >= 2, 1, 0)
        step = jnp.where((p == 1) | (p == 3), j, jnp.where((p == 0) | (p == 2), 0, ns - 1))
        return (order, step, 0, c)

    out = pl.pallas_call(
        functools.partial(_hyena_kernel, n1t=n1t, grp=grp, kgrp=kgrp, stride=stride),
        grid=(ch // ct, 5, ns),
        in_specs=[hspec(0, 0, 2), hspec(1, 2, 2), hspec(2, 4, 4),
                  pl.BlockSpec((None, kgrp, 2 * FFT_INNER, ct), hf_map),
                  pl.BlockSpec((grp, 2 * n1t, n1t), lambda c, p, j: (_hold(p, j, 0, 2, ns), 0, 0)),
                  pl.BlockSpec((grp, n1t, 2 * n1t), lambda c, p, j: (_hold(p, j, 2, 4, ns), 0, 0)),
                  pl.BlockSpec((2 * FFT_INNER, 2 * FFT_INNER), lambda c, p, j: (0, 0)),
                  pl.BlockSpec((2 * FFT_INNER, 2 * FFT_INNER), lambda c, p, j: (0, 0)),
                  pl.BlockSpec((2, ct), lambda c, p, j: (0, c))],
        out_specs=pl.BlockSpec((bsz, grp, h1, ct), lambda c, p, j: (0, _hold(p, j, 4, 4, ns), 0, c)),
        out_shape=jax.ShapeDtypeStruct((bsz, FFT_INNER, h1, ch), BF16),
        scratch_shapes=[pltpu.VMEM((FFT_INNER * stride, ct), F32),
                        pltpu.VMEM((FFT_INNER, n1t, ct), F32)],
        compiler_params=_cparams(("arbitrary", "arbitrary", "arbitrary")),
        name="hyena",
    )(hv, hv, hv, hf, e1, e4, g2, g3, bias)
    return jnp.transpose(out, (0, 2, 1, 3)).reshape(bsz, n_lat, ch)


def _merge_kernel(of_ref, ob_ref, gp_ref, hy_ref, x_ref, mod_ref, dng_ref, wpa_ref, wpb_ref, wo_ref,
                  fg_ref, o_ref, oa_ref, *, d):
    b = pl.program_id(0)
    o = of_ref[0].astype(F32) + ob_ref[0].astype(F32)
    za = gp_ref[0, 0].astype(F32)
    for hh in range(HEADS):
        hs = slice(hh * HEAD_DIM, (hh + 1) * HEAD_DIM)
        oh = o[:, hs]
        ms = jnp.mean(oh * oh, axis=-1, keepdims=True)
        oa_ref[:, hs] = (oh * lax.rsqrt(ms + EPS) * dng_ref[...] * za[:, hs]).astype(BF16)
    y_a = jnp.dot(oa_ref[...], wpa_ref[...], preferred_element_type=F32)
    o_b = hy_ref[0].astype(F32) * gp_ref[1, 0].astype(F32)
    y_b = _bdot(o_b, wpb_ref[...])
    m = gp_ref[2, 0].astype(F32) * y_a + gp_ref[3, 0].astype(F32) * y_b
    y = _bdot(m, wo_ref[...])
    gate = mod_ref[pl.ds(b, 1), :][:, 2 * d:3 * d]
    xn = x_ref[0] + gate * y
    o_ref[0] = xn * lax.rsqrt(jnp.mean(xn * xn, axis=-1, keepdims=True) + EPS) * fg_ref[...]


def _merge_call(o_f, o_b, gp, hy, x, mod, dn_norm_g, w_pa, w_pb, w_out, final_g):
    bsz, n_lat, d = x.shape
    tm = 512 if n_lat % 512 == 0 else 256
    wfull = lambda a: pl.BlockSpec(a.shape, lambda b, t: (0,) * a.ndim)
    tok = pl.BlockSpec((1, tm, d), lambda b, t: (b, t, 0))
    return pl.pallas_call(
        functools.partial(_merge_kernel, d=d),
        grid=(bsz, n_lat // tm),
        in_specs=[tok, tok,
                  pl.BlockSpec((4, 1, tm, d), lambda b, t: (0, b, t, 0)),
                  tok, tok, wfull(mod),
                  pl.BlockSpec((1, HEAD_DIM), lambda b, t: (0, 0)),
                  wfull(w_pa), wfull(w_pb), wfull(w_out),
                  pl.BlockSpec((1, d), lambda b, t: (0, 0))],
        out_specs=tok,
        out_shape=jax.ShapeDtypeStruct((bsz, n_lat, d), F32),
        scratch_shapes=[pltpu.VMEM((tm, d), BF16)],
        compiler_params=_cparams(("arbitrary", "arbitrary")),
        name="merge",
    )(o_f, o_b, gp, hy, x, mod, dn_norm_g.reshape(1, HEAD_DIM), w_pa, w_pb, w_out, final_g.reshape(1, d))


def _position_features(n_lat):
    h1 = n_lat // FFT_INNER
    m = (np.arange(FFT_INNER)[:, None] + FFT_INNER * np.arange(h1)[None, :]).reshape(-1)
    lag = np.concatenate([m, n_lat - m]).astype(np.float64)
    lag[n_lat] = 0.0
    bands = (HY_EMB - 1) // 2
    t = (lag / (n_lat - 1))[:, None]
    wpos = (2.0 * math.pi / n_lat) * lag[:, None]
    fb = np.linspace(1e-4, bands - 1, bands)[None, :]
    pe = np.concatenate([t, np.cos(fb * wpos), -np.sin(fb * wpos)], axis=1)
    pe_pad = np.zeros((2 * n_lat, LANES), np.float32)
    pe_pad[:, :HY_EMB] = pe
    return pe_pad, np.ascontiguousarray(t, dtype=np.float32)


def kernel(x, c, ctx, c_ctx, w_mod, b_mod, norm_g, w_in, dn_conv_w, dn_a_log, dn_dt_bias, dn_norm_g,
           hy_conv_w, hy_conv_b, hy_f_w1, hy_f_b1, hy_f_w2, hy_f_b2, hy_f_w3, hy_f_b3, hy_f_wout,
           hy_f_freq, hy_bias, w_pa, w_pb, w_out, final_g):
    bsz, n_lat, d = x.shape
    n_ctx = ctx.shape[1]
    assert w_mod.shape[0] == 1, "single layer: the context stream is only read through its scan states"
    dn = HEADS * HEAD_DIM
    hy = hy_bias.shape[-1]
    assert n_lat % 256 == 0 and n_ctx % 256 == 0 and dn == d and hy == d

    cvec = jnp.zeros((SUBLANES, d), F32).at[:bsz].set(c).at[bsz].set(c_ctx)
    mod = _mod_call(cvec, w_mod[0], b_mod[0])

    tok = jnp.concatenate([x, ctx], axis=1)
    h = _hnorm_call(tok, mod, norm_g[0], n_lat, bsz)

    w = w_in[0].astype(BF16)
    o_qkv, o_gate, o_hy, o_gp = 0, 4 * dn, 4 * dn + 4 * HEADS, 4 * dn + 4 * HEADS + 3 * hy
    col3 = lambda a, off, n: jnp.transpose(a[:, off:off + 3 * n].reshape(a.shape[0], 3, n), (1, 0, 2))

    lt = n_lat + n_ctx
    tm = PROJ_TILE if n_lat % PROJ_TILE == 0 else 256
    w_qkv, cw_qkv, no_bias = col3(w, o_qkv, dn), col3(dn_conv_w[0], 0, dn), jnp.zeros((3, 1, dn), F32)
    qkv = _convproj_call(h, w_qkv, cw_qkv, no_bias, rowlen=GRID_W, tm=tm, tile0=0, ntiles=n_lat // tm,
                         out_rows=lt, qk_norm=True, name="qkv")
    qkv = _convproj_call(h, w_qkv, cw_qkv, no_bias, rowlen=n_ctx, tm=n_ctx, tile0=n_lat // n_ctx, ntiles=1,
                         out_rows=lt, qk_norm=True, prev=qkv, name="qkv_ctx")

    wg = jnp.zeros((d, LANES), BF16).at[:, :4 * HEADS].set(w[:, o_gate:o_gate + 4 * HEADS])
    lane_pad = lambda a: jnp.zeros((1, LANES), F32).at[0, 2 * HEADS:4 * HEADS].set(a.reshape(-1))
    gates = _gates_call(h, wg, lane_pad(dn_a_log[0]), lane_pad(dn_dt_bias[0]))
    gates_t = jnp.transpose(gates[:, :, :4 * HEADS].reshape(bsz, lt // CHUNK, CHUNK, 4 * HEADS), (0, 1, 3, 2))
    ta, coef = _prep_call(qkv, gates, gates_t)
    o_f, o_b = _scan_call(qkv, gates, coef, ta, n_lat, n_ctx)

    hyp = _convproj_call(h, col3(w, o_hy, hy), col3(hy_conv_w[0], 0, hy), hy_conv_b[0].reshape(3, 1, hy),
                         rowlen=GRID_W, tm=tm, tile0=0, ntiles=n_lat // tm, out_rows=n_lat, qk_norm=False,
                         name="hyproj")
    w4 = jnp.stack([w[:, 3 * dn:4 * dn]] + [w[:, o_gp + i * d:o_gp + (i + 1) * d] for i in range(3)])
    gp = _gproj_call(h, w4, n_lat)

    pe, tcol = _position_features(n_lat)
    w1p = jnp.zeros((LANES, hy_f_w1.shape[-1]), F32).at[:HY_EMB].set(hy_f_w1[0])
    a3 = _hidden_call(jnp.asarray(pe), w1p, hy_f_b1[0], hy_f_w2[0], hy_f_b2[0], hy_f_w3[0], hy_f_b3[0],
                      hy_f_freq[0])
    deltas = np.abs(np.linspace(math.log(HY_DECAY_TARGET) / HY_SLOW_DECAY,
                                math.log(HY_DECAY_TARGET) / HY_FAST_DECAY, hy)).astype(np.float32)
    deltas2 = jnp.asarray(np.tile(deltas, 2)[None, :])
    wout2 = hy_f_wout[0].reshape(-1, 2, 2 * hy).transpose(1, 0, 2).astype(BF16)
    hfull, ssq = _ftime_call(a3, jnp.asarray(tcol), wout2, deltas2, n_lat)
    e1, e4, ef, g2, g3 = (jnp.asarray(t).astype(BF16) for t in _fft_tables(n_lat))
    hf = _ffft_call(hfull, ssq, ef, g2, n_lat)

    yh = _hyena_call(hyp, hf, e1, e4, g2, g3, hy_bias[0], n_lat)

    return _merge_call(o_f, o_b, gp, yh, x, mod, dn_norm_g[0], w_pa[0].astype(BF16), w_pb[0].astype(BF16),
                       w_out[0].astype(BF16), final_g)
```

```python
import functools
import math

import numpy as np
import jax
import jax.numpy as jnp
from jax import lax
from jax.experimental import pallas as pl
from jax.experimental.pallas import tpu as pltpu

F32 = jnp.float32
BF16 = jnp.bfloat16
HIGHEST = lax.Precision.HIGHEST

EPS = 1e-6
HEADS = 8
HEAD_DIM = 128
CHUNK = 64
GRID_W = 64
HY_EMB = 33
HY_DECAY_TARGET = 1e-2
HY_FAST_DECAY = 0.3
HY_SLOW_DECAY = 1.5

LANES = 128
SUBLANES = 8
FFT_INNER = 128
FFT_STEPS = 16
FFT_CHANNELS = 256
FFT_INTERLEAVE = 8
PROJ_TILE = 1024
MXU_TILE = 256
SLAB_PAD = 8
VMEM_LIMIT = 56 * 1024 * 1024


def _cparams(sem):
    return pltpu.CompilerParams(dimension_semantics=sem, vmem_limit_bytes=VMEM_LIMIT)


def _bdot(a, b):
    return jnp.dot(a.astype(BF16), b.astype(BF16), preferred_element_type=F32)


def _silu(x):
    return x * jax.nn.sigmoid(x)


def _mod_kernel(c_ref, w_ref, b_ref, o_ref):
    s = _silu(c_ref[...])
    o_ref[...] = jnp.dot(s, w_ref[...], precision=HIGHEST, preferred_element_type=F32) + b_ref[...]


def _mod_call(cvec, w_mod, b_mod):
    rows, d = cvec.shape
    n = w_mod.shape[1]
    tn = 1024
    return pl.pallas_call(
        _mod_kernel,
        grid=(n // tn,),
        in_specs=[pl.BlockSpec((rows, d), lambda j: (0, 0)),
                  pl.BlockSpec((d, tn), lambda j: (0, j)),
                  pl.BlockSpec((1, tn), lambda j: (0, j))],
        out_specs=pl.BlockSpec((rows, tn), lambda j: (0, j)),
        out_shape=jax.ShapeDtypeStruct((rows, n), F32),
        compiler_params=_cparams(("arbitrary",)),
        name="mod",
    )(cvec, w_mod, b_mod.reshape(1, n))


def _hnorm_kernel(tok_ref, mod_ref, g_ref, o_ref, *, n_lat, tm, d, ctx_row):
    b = pl.program_id(0)
    t = pl.program_id(1)
    row = jnp.where(t * tm >= n_lat, ctx_row, b)
    m = mod_ref[pl.ds(row, 1), :]
    x = tok_ref[0]
    y = x * lax.rsqrt(jnp.mean(x * x, axis=-1, keepdims=True) + EPS) * g_ref[...]
    o_ref[0] = (y * (1.0 + m[:, d:2 * d]) + m[:, 0:d]).astype(BF16)


def _hnorm_call(tok, mod, norm_g, n_lat, ctx_row):
    bsz, lt, d = tok.shape
    tm = 256
    return pl.pallas_call(
        functools.partial(_hnorm_kernel, n_lat=n_lat, tm=tm, d=d, ctx_row=ctx_row),
        grid=(bsz, lt // tm),
        in_specs=[pl.BlockSpec((1, tm, d), lambda b, t: (b, t, 0)),
                  pl.BlockSpec(mod.shape, lambda b, t: (0, 0)),
                  pl.BlockSpec((1, d), lambda b, t: (0, 0))],
        out_specs=pl.BlockSpec((1, tm, d), lambda b, t: (b, t, 0)),
        out_shape=jax.ShapeDtypeStruct((bsz, lt, d), BF16),
        compiler_params=_cparams(("arbitrary", "arbitrary")),
        name="hnorm",
    )(tok, mod, norm_g.reshape(1, d))


@functools.lru_cache(maxsize=None)
def _shift_matrix(rowlen, taps, blk):
    t = np.arange(blk)
    mats = []
    for j in range(taps):
        d = j - taps // 2
        if d != 0:
            ok = ((t + d) >= 0) & ((t + d) < blk) & (t // rowlen == (t + d) // rowlen)
            s = np.zeros((blk, blk), np.float32)
            s[t[ok], t[ok] + d] = 1.0
            mats.append(s)
    return np.concatenate(mats, axis=1)


def _shift_block(rowlen, taps):
    return max(rowlen, MXU_TILE // (taps - 1) // rowlen * rowlen)


def _convproj_kernel(h_ref, w_ref, cw_ref, cb_ref, sh_ref, *rest, tm, qk_norm):
    o_ref = rest[-1]
    j = pl.program_id(0)
    taps = cw_ref.shape[1]
    mid = taps // 2
    blk = sh_ref.shape[0]
    mblk = max(blk, MXU_TILE)
    if qk_norm:
        scale = jnp.where(j == 0, HEAD_DIM ** -0.5, 1.0)
        is_qk = j < 2
    for r in range(tm // mblk):
        zm = jnp.dot(h_ref[0, r * mblk:(r + 1) * mblk, :], w_ref[0], preferred_element_type=F32)
        for g in range(mblk // blk):
            rows = slice(r * mblk + g * blk, r * mblk + (g + 1) * blk)
            zb = zm[g * blk:(g + 1) * blk]
            side = jnp.concatenate(
                [(zb * cw_ref[0, jt:jt + 1, :]).astype(BF16) for jt in range(taps) if jt != mid], axis=0)
            y = zb * cw_ref[0, mid:mid + 1, :] + jnp.dot(sh_ref[...], side, preferred_element_type=F32)
            if not qk_norm:
                o_ref[0, 0, rows, :] = (y + cb_ref[0]).astype(o_ref.dtype)
                continue
            y = _silu(y)
            for hh in range(HEADS):
                yh = y[:, hh * HEAD_DIM:(hh + 1) * HEAD_DIM]
                nrm = jnp.sum(yh * yh, axis=-1, keepdims=True)
                f = jnp.where(is_qk, lax.rsqrt(nrm + EPS) * scale, 1.0)
                o_ref[0, 0, rows, hh * HEAD_DIM:(hh + 1) * HEAD_DIM] = (yh * f).astype(o_ref.dtype)


def _convproj_call(h, w3, cw3, cb3, *, rowlen, tm, tile0, ntiles, out_rows, qk_norm, prev=None, name):
    bsz, _, d = h.shape
    nw = w3.shape[2]
    taps = cw3.shape[1]
    blk = _shift_block(rowlen, taps)
    assert tm % max(blk, MXU_TILE) == 0 and max(blk, MXU_TILE) % blk == 0
    sh = jnp.asarray(_shift_matrix(rowlen, taps, blk)).astype(BF16)
    in_specs = [pl.BlockSpec((1, tm, d), lambda j, b, t: (b, t + tile0, 0)),
                pl.BlockSpec((1, d, nw), lambda j, b, t: (j, 0, 0)),
                pl.BlockSpec((1, cw3.shape[1], nw), lambda j, b, t: (j, 0, 0)),
                pl.BlockSpec((1, 1, nw), lambda j, b, t: (j, 0, 0)),
                pl.BlockSpec(sh.shape, lambda j, b, t: (0, 0))]
    args = [h, w3, cw3, cb3, sh]
    aliases = {}
    if prev is not None:
        in_specs.append(pl.BlockSpec(memory_space=pl.ANY))
        args.append(prev)
        aliases = {5: 0}
    return pl.pallas_call(
        functools.partial(_convproj_kernel, tm=tm, qk_norm=qk_norm),
        grid=(3, bsz, ntiles),
        in_specs=in_specs,
        out_specs=pl.BlockSpec((1, 1, tm, nw), lambda j, b, t: (j, b, t + tile0, 0)),
        out_shape=jax.ShapeDtypeStruct((3, bsz, out_rows, nw), BF16),
        input_output_aliases=aliases,
        compiler_params=_cparams(("arbitrary", "arbitrary", "arbitrary")),
        name=name,
    )(*args)


def _gates_kernel(h_ref, w_ref, alog_ref, dtb_ref, o_ref):
    z = jnp.dot(h_ref[0], w_ref[...], preferred_element_type=F32)
    lane = lax.broadcasted_iota(jnp.int32, z.shape, 1)
    beta = jax.nn.sigmoid(z)
    u = z + dtb_ref[...]
    softplus = jnp.maximum(u, 0.0) + jnp.log(1.0 + jnp.exp(-jnp.abs(u)))
    g = -jnp.exp(alog_ref[...]) * softplus
    o_ref[0] = jnp.where(lane < 2 * HEADS, beta, g)


def _gates_call(h, wg, alog, dtb):
    bsz, lt, d = h.shape
    tm = 256
    return pl.pallas_call(
        _gates_kernel,
        grid=(bsz, lt // tm),
        in_specs=[pl.BlockSpec((1, tm, d), lambda b, t: (b, t, 0)),
                  pl.BlockSpec((d, LANES), lambda b, t: (0, 0)),
                  pl.BlockSpec((1, LANES), lambda b, t: (0, 0)),
                  pl.BlockSpec((1, LANES), lambda b, t: (0, 0))],
        out_specs=pl.BlockSpec((1, tm, LANES), lambda b, t: (b, t, 0)),
        out_shape=jax.ShapeDtypeStruct((bsz, lt, LANES), F32),
        compiler_params=_cparams(("arbitrary", "arbitrary")),
        name="gates",
    )(h, wg, alog, dtb)


def _gproj_kernel(h_ref, w_ref, o_ref):
    j = pl.program_id(0)
    for r in range(h_ref.shape[1] // MXU_TILE):
        rows = slice(r * MXU_TILE, (r + 1) * MXU_TILE)
        z = jnp.dot(h_ref[0, rows, :], w_ref[0], preferred_element_type=F32)
        s = jax.nn.sigmoid(z)
        o_ref[0, 0, rows, :] = jnp.where(j < 2, z * s, s).astype(BF16)


def _gproj_call(h, w4, n_lat):
    bsz, _, d = h.shape
    nw = w4.shape[2]
    tm = PROJ_TILE if n_lat % PROJ_TILE == 0 else 256
    return pl.pallas_call(
        _gproj_kernel,
        grid=(4, bsz, n_lat // tm),
        in_specs=[pl.BlockSpec((1, tm, d), lambda j, b, t: (b, t, 0)),
                  pl.BlockSpec((1, d, nw), lambda j, b, t: (j, 0, 0))],
        out_specs=pl.BlockSpec((1, 1, tm, nw), lambda j, b, t: (j, b, t, 0)),
        out_shape=jax.ShapeDtypeStruct((4, bsz, n_lat, nw), BF16),
        compiler_params=_cparams(("arbitrary", "arbitrary", "arbitrary")),
        name="gproj",
    )(h, w4)


def _unit_tri_inverse(ls):
    c = ls[0].shape[0]
    ri = lax.broadcasted_iota(jnp.int32, (c, c), 0)
    ci = lax.broadcasted_iota(jnp.int32, (c, c), 1)
    same = lambda n: (ri // n) == (ci // n)
    eye = jnp.where(ri == ci, 1.0, 0.0)
    in2 = same(2)
    ts = [eye - jnp.where(in2, l, 0.0) for l in ls]
    n = 2
    while n < c:
        blk = same(2 * n) & jnp.logical_not(same(n))
        offs = [jnp.where(blk, l, 0.0).astype(BF16) for l in ls]
        prods = [_bdot(t, off) for t, off in zip(ts, offs)]
        ts = [t - _bdot(p, t) for p, t in zip(prods, ts)]
        n *= 2
    return ts


_NT = (((1,), (1,)), ((), ()))
_TN = (((0,), (0,)), ((), ()))


def _prep_kernel(qk_ref, g_ref, gt_ref, ta_ref, cf_ref, *, bsz):
    c = CHUNK
    ri = lax.broadcasted_iota(jnp.int32, (c, c), 0)
    ci = lax.broadcasted_iota(jnp.int32, (c, c), 1)
    tri_l = jnp.where(ri >= ci, 1.0, 0.0)
    tri_u = jnp.where(ri <= ci, 1.0, 0.0)
    incl = (ri >= ci, ri <= ci)
    strict = (ri > ci, ri < ci)
    lane = lax.broadcasted_iota(jnp.int32, (c, LANES), 1)
    fwd_lane = lane < 3 * HEADS
    hdot = functools.partial(jnp.dot, precision=HIGHEST, preferred_element_type=F32)

    gs = [g_ref[b] for b in range(bsz)]
    gts = [gt_ref[b, 0] for b in range(bsz)]
    gcol = [(hdot(tri_l, g), hdot(tri_u, g)) for g in gs]
    grow = [(hdot(gt, tri_u), hdot(gt, tri_l)) for gt in gts]
    for b in range(bsz):
        gc = jnp.where(fwd_lane, gcol[b][0], gcol[b][1])
        glast = jnp.where(fwd_lane[0:1], gcol[b][0][c - 1:c], gcol[b][1][0:1])
        cf_ref[0, b] = jnp.exp(gc)
        cf_ref[1, b] = jnp.exp(glast - gc)
        cf_ref[2, b] = jnp.broadcast_to(jnp.exp(glast), (c, LANES))

    bh = [(b, hh) for b in range(bsz) for hh in range(HEADS)]
    hs = lambda hh: slice(hh * HEAD_DIM, (hh + 1) * HEAD_DIM)
    ks = [qk_ref[1, b, :, hs(hh)] for b, hh in bh]
    qs = [qk_ref[0, b, :, hs(hh)] for b, hh in bh]
    kk = [lax.dot_general(k, k, _NT, preferred_element_type=F32) for k in ks]
    qk = [lax.dot_general(q, k, _NT, preferred_element_type=F32) for q, k in zip(qs, ks)]
    inst = [(d, b, hh, i) for d in range(2) for i, (b, hh) in enumerate(bh)]
    decs = []
    for d, b, hh, _ in inst:
        lg = 2 * HEADS + d * HEADS + hh
        diff = gcol[b][d][:, lg:lg + 1] - grow[b][d][lg:lg + 1, :]
        decs.append(jnp.exp(jnp.where(incl[d], diff, -jnp.inf)))
    ls = [jnp.where(strict[d], kk[i] * gs[b][:, d * HEADS + hh:d * HEADS + hh + 1] * dec, 0.0)
          for (d, b, hh, i), dec in zip(inst, decs)]
    ts = _unit_tri_inverse(ls)
    for (d, b, hh, i), t, dec in zip(inst, ts, decs):
        ta_ref[d, b, hh, 0:c, :] = t.astype(BF16)
        ta_ref[d, b, hh, c:2 * c, :] = (qk[i] * dec).astype(BF16)


def _prep_call(qkv, gates, gates_t):
    _, bsz, lt, dn = qkv.shape
    nc = lt // CHUNK
    return pl.pallas_call(
        functools.partial(_prep_kernel, bsz=bsz),
        grid=(nc,),
        in_specs=[pl.BlockSpec((2, bsz, CHUNK, dn), lambda i: (0, 0, i, 0)),
                  pl.BlockSpec((bsz, CHUNK, LANES), lambda i: (0, i, 0)),
                  pl.BlockSpec((bsz, 1, 4 * HEADS, CHUNK), lambda i: (0, i, 0, 0))],
        out_specs=[pl.BlockSpec((None, 2, bsz, HEADS, 2 * CHUNK, CHUNK), lambda i: (i, 0, 0, 0, 0, 0)),
                   pl.BlockSpec((3, bsz, CHUNK, LANES), lambda i: (0, 0, i, 0))],
        out_shape=[jax.ShapeDtypeStruct((nc, 2, bsz, HEADS, 2 * CHUNK, CHUNK), BF16),
                   jax.ShapeDtypeStruct((3, bsz, lt, LANES), F32)],
        compiler_params=_cparams(("arbitrary",)),
        name="prep",
    )(qkv, gates, gates_t)


def _scan_kernel(qf_ref, qb_ref, gf_ref, gb_ref, cff_ref, cfb_ref, taf_ref, tab_ref, of_ref, ob_ref, s_ref,
                 *, bsz):
    i = pl.program_id(0)

    @pl.when(i == 0)
    def _():
        s_ref[...] = jnp.zeros_like(s_ref)

    c = CHUNK
    dirs = ((qf_ref, gf_ref, cff_ref, taf_ref, of_ref), (qb_ref, gb_ref, cfb_ref, tab_ref, ob_ref))
    inst = [(d, b, hh) for d in range(2) for b in range(bsz) for hh in range(HEADS)]
    hs = lambda hh: slice(hh * HEAD_DIM, (hh + 1) * HEAD_DIM)
    qg_kg, vb, kd, el, s_old = [], [], [], [], []
    for d, b, hh in inst:
        q_ref, g_ref, cf_ref = dirs[d][0], dirs[d][1], dirs[d][2]
        lb = d * HEADS + hh
        lg = 2 * HEADS + lb
        beta = g_ref[b, :, lb:lb + 1]
        eg = cf_ref[0, b, :, lg:lg + 1]
        k = q_ref[1, b, :, hs(hh)].astype(F32)
        qg_kg.append(jnp.concatenate([q_ref[0, b, :, hs(hh)].astype(F32) * eg, k * (beta * eg)], axis=0).astype(BF16))
        vb.append(q_ref[2, b, :, hs(hh)].astype(F32) * beta)
        kd.append((k * cf_ref[1, b, :, lg:lg + 1]).astype(BF16))
        el.append(cf_ref[2, b, 0:1, lg:lg + 1])
        s_old.append(s_ref[d, b, hh])
    a1 = [_bdot(x, s) for x, s in zip(qg_kg, s_old)]
    v_new = [_bdot(dirs[d][3][b, hh, 0:c, :], r - a[c:]).astype(BF16)
             for (d, b, hh), r, a in zip(inst, vb, a1)]
    for (d, b, hh), a, vn, kdi, eli, s in zip(inst, a1, v_new, kd, el, s_old):
        o = a[:c] + jnp.dot(dirs[d][3][b, hh, c:2 * c, :], vn, preferred_element_type=F32)
        dirs[d][4][b, :, hs(hh)] = o.astype(BF16)
        s_ref[d, b, hh] = s * eli + lax.dot_general(kdi, vn, _TN, preferred_element_type=F32)


def _scan_call(qkv, gates, coef, ta, n_lat, n_ctx):
    _, bsz, lt, dn = qkv.shape
    ncx = n_lat // CHUNK
    ncc = n_ctx // CHUNK
    nc = ncx + ncc
    cf = lambda i: jnp.where(i < ncc, ncx + i, i - ncc)
    cb = lambda i: nc - 1 - i

    qspec = lambda fn: pl.BlockSpec((3, bsz, CHUNK, dn), lambda i: (0, 0, fn(i), 0))
    gspec = lambda fn: pl.BlockSpec((bsz, CHUNK, LANES), lambda i: (0, fn(i), 0))
    cspec = lambda fn: pl.BlockSpec((3, bsz, CHUNK, LANES), lambda i: (0, 0, fn(i), 0))
    tspec = lambda fn, d: pl.BlockSpec((None, None, bsz, HEADS, 2 * CHUNK, CHUNK),
                                       lambda i: (fn(i), d, 0, 0, 0, 0))
    ospec = lambda fn: pl.BlockSpec((bsz, CHUNK, dn), lambda i: (0, fn(i), 0))
    return pl.pallas_call(
        functools.partial(_scan_kernel, bsz=bsz),
        grid=(nc,),
        in_specs=[qspec(cf), qspec(cb), gspec(cf), gspec(cb), cspec(cf), cspec(cb), tspec(cf, 0), tspec(cb, 1)],
        out_specs=[ospec(cf), ospec(cb)],
        out_shape=[jax.ShapeDtypeStruct((bsz, lt, dn), BF16)] * 2,
        scratch_shapes=[pltpu.VMEM((2, bsz, HEADS, HEAD_DIM, HEAD_DIM), F32)],
        compiler_params=_cparams(("arbitrary",)),
        name="scan",
    )(qkv, qkv, gates, gates, coef, coef, ta, ta)


def _hidden_kernel(pe_ref, w1_ref, b1_ref, w2_ref, b2_ref, w3_ref, b3_ref, f_ref, o_ref):
    f = f_ref[...]
    dot = functools.partial(jnp.dot, precision=HIGHEST, preferred_element_type=F32)
    a = jnp.sin(f * (dot(pe_ref[...], w1_ref[...]) + b1_ref[...]))
    a = jnp.sin(f * (dot(a, w2_ref[...]) + b2_ref[...]))
    o_ref[...] = jnp.sin(f * (dot(a, w3_ref[...]) + b3_ref[...]))


def _hidden_call(pe, w1p, b1, w2, b2, w3, b3, freq):
    rows, pw = pe.shape
    fw = 2 * w2.shape[0]
    tr = 512
    full = lambda a: pl.BlockSpec(a.shape, lambda t: (0,) * a.ndim)
    two = lambda a: jnp.tile(a.reshape(1, -1), (1, 2))
    diag2 = lambda a: jnp.kron(jnp.eye(2, dtype=a.dtype), a)
    args = (diag2(w1p), two(b1), diag2(w2), two(b2), diag2(w3), two(b3), two(freq))
    return pl.pallas_call(
        _hidden_kernel,
        grid=(rows // tr,),
        in_specs=[pl.BlockSpec((tr, pw), lambda t: (t, 0))] + [full(a) for a in args],
        out_specs=pl.BlockSpec((tr, fw), lambda t: (t, 0)),
        out_shape=jax.ShapeDtypeStruct((rows, fw), F32),
        compiler_params=_cparams(("arbitrary",)),
        name="hidden",
    )(pe, *args)


def _ftime_kernel(a_ref, t_ref, w_ref, dl_ref, h_ref, ss_ref, *, n_lat, tr):
    t = pl.program_id(0)
    h = _bdot(a_ref[...], w_ref[...]) * jnp.exp(-t_ref[...] * dl_ref[...])
    row = lax.broadcasted_iota(jnp.int32, (tr, 1), 0) + t * tr
    h = jnp.where(row == n_lat, 0.0, h)
    h_ref[...] = h

    @pl.when(t == 0)
    def _():
        ss_ref[...] = jnp.zeros_like(ss_ref)

    ss_ref[...] += jnp.sum(h * h, axis=0, keepdims=True)


def _ftime_call(a3, tcol, wout2, deltas2, n_lat):
    rows = tcol.shape[0]
    fw = a3.shape[1]
    ncol = wout2.shape[2]
    tr = 512
    side = lambda t: (t * tr) // n_lat
    return pl.pallas_call(
        functools.partial(_ftime_kernel, n_lat=n_lat, tr=tr),
        grid=(rows // tr,),
        in_specs=[pl.BlockSpec((tr, fw), lambda t: (t % (n_lat // tr), 0)),
                  pl.BlockSpec((tr, 1), lambda t: (t, 0)),
                  pl.BlockSpec((None, fw, ncol), lambda t: (side(t), 0, 0)),
                  pl.BlockSpec((1, ncol), lambda t: (0, 0))],
        out_specs=[pl.BlockSpec((tr, ncol), lambda t: (t, 0)),
                   pl.BlockSpec((1, ncol), lambda t: (0, 0))],
        out_shape=[jax.ShapeDtypeStruct((rows, ncol), F32), jax.ShapeDtypeStruct((1, ncol), F32)],
        compiler_params=_cparams(("arbitrary",)),
        name="ftime",
    )(a3, tcol, wout2, deltas2)


@functools.lru_cache(maxsize=None)
def _fft_tables(n_lat):
    n = 2 * n_lat
    n1t = n // FFT_INNER
    h1 = n1t // 2
    n2 = np.arange(FFT_INNER, dtype=np.int64)[:, None, None]
    k1 = np.arange(n1t, dtype=np.int64)[None, :, None]
    n1 = np.arange(h1, dtype=np.int64)[None, None, :]
    ang = (-2.0 * np.pi / n) * ((k1 * (FFT_INNER * n1 + n2)) % n)
    er, ei = np.cos(ang), np.sin(ang)
    e1 = np.concatenate([np.concatenate([er, -ei], axis=2), np.concatenate([ei, er], axis=2)], axis=1)
    e4 = np.transpose(e1, (0, 2, 1))
    sgn = np.where(np.arange(n1t) % 2 == 0, 1.0, -1.0)[None, :, None]
    ef = np.concatenate([np.concatenate([er, sgn * er], axis=2), np.concatenate([ei, sgn * ei], axis=2)], axis=1)
    a = np.arange(FFT_INNER, dtype=np.int64)
    ang2 = (-2.0 * np.pi / FFT_INNER) * ((a[:, None] * a[None, :]) % FFT_INNER)
    gr, gi = np.cos(ang2), np.sin(ang2)
    g2 = np.block([[gr, -gi], [gi, gr]])
    f32 = lambda x: np.ascontiguousarray(x, dtype=np.float32)
    return f32(e1), f32(e4), f32(ef), f32(g2), f32(g2.T)


def _hold(p, j, first, last, nsteps):
    active = (p >= first) & (p <= last) & (lax.rem(p - first, 2) == 0)
    return jnp.where(active, j, jnp.where(p < first, 0, nsteps - 1))


def _slab_read(x_ref, rows):
    return jnp.concatenate([x_ref[c, rows, :] for c in range(x_ref.shape[0])], axis=1)


def _slab_write(x_ref, rows, val):
    for c in range(x_ref.shape[0]):
        x_ref[c, rows, :] = val[:, c * LANES:(c + 1) * LANES]


def _spectrum_steps(x_ref, g2_ref, k1s, n1t, stride):
    cols = [jnp.concatenate([_slab_read(x_ref, pl.ds(k1, FFT_INNER, stride=stride)),
                             _slab_read(x_ref, pl.ds(n1t + k1, FFT_INNER, stride=stride))], axis=0).astype(BF16)
            for k1 in k1s]
    return [jnp.dot(g2_ref[...], c, preferred_element_type=F32) for c in cols]


def _ffft_kernel(h_ref, ef_ref, g2_ref, ss_ref, o_ref, x_ref, *, n1t, grp, kgrp, stride, inv_n):
    p = pl.program_id(1)
    j = pl.program_id(2)

    @pl.when(p == 0)
    def _():
        us = [jnp.concatenate([h_ref[0, jj], h_ref[1, jj]], axis=0) for jj in range(grp)]
        slabs = [_bdot(ef_ref[jj], u) for jj, u in enumerate(us)]
        for jj, slab in enumerate(slabs):
            base = pl.multiple_of((j * grp + jj) * stride, SUBLANES)
            _slab_write(x_ref, pl.ds(base, 2 * n1t), slab)

    @pl.when(p == 1)
    def _():
        scale = lax.rsqrt(ss_ref[...] + EPS) * inv_n
        xfs = _spectrum_steps(x_ref, g2_ref, [j * kgrp + jj for jj in range(kgrp)], n1t, stride)
        for jj, xf in enumerate(xfs):
            o_ref[jj] = (xf * scale).astype(BF16)


def _fft_groups(n1t):
    ns = FFT_STEPS
    assert FFT_INNER % ns == 0 and n1t % ns == 0
    return ns, FFT_INNER // ns, n1t // ns


def _ffft_call(hfull, ssq, ef, g2, n_lat):
    ncol = hfull.shape[1]
    n1t = 2 * n_lat // FFT_INNER
    h1 = n1t // 2
    ct = FFT_CHANNELS
    ns, grp, kgrp = _fft_groups(n1t)
    stride = 2 * n1t + SLAB_PAD
    hv = hfull.reshape(2, FFT_INNER, h1, ncol)
    nct = ncol // ct
    cpo = nct // 2
    return pl.pallas_call(
        functools.partial(_ffft_kernel, n1t=n1t, grp=grp, kgrp=kgrp, stride=stride, inv_n=1.0 / (2 * n_lat)),
        grid=(nct, 2, ns),
        in_specs=[pl.BlockSpec((2, grp, h1, ct), lambda c, p, j: (0, _hold(p, j, 0, 0, ns), 0, c)),
                  pl.BlockSpec((grp, 2 * n1t, n1t), lambda c, p, j: (_hold(p, j, 0, 0, ns), 0, 0)),
                  pl.BlockSpec((2 * FFT_INNER, 2 * FFT_INNER), lambda c, p, j: (0, 0)),
                  pl.BlockSpec((1, ct), lambda c, p, j: (0, c))],
        out_specs=pl.BlockSpec((None, kgrp, 2 * FFT_INNER, ct),
                               lambda c, p, j: (c // cpo, _hold(p, j, 1, 1, ns), 0, c % cpo)),
        out_shape=jax.ShapeDtypeStruct((2, n1t, 2 * FFT_INNER, ncol // 2), BF16),
        scratch_shapes=[pltpu.VMEM((ct // LANES, FFT_INNER * stride, LANES), F32)],
        compiler_params=_cparams(("arbitrary", "arbitrary", "arbitrary")),
        name="ffft",
    )(hv, ef, g2, ssq)


def _hyena_kernel(v_ref, x1_ref, x2_ref, hf_ref, e1_ref, e4_ref, g2_ref, g3_ref, bias_ref,
                  o_ref, x_ref, z_ref, *, n1t, grp, kgrp, stride):
    p = pl.program_id(1)
    j = pl.program_id(2)
    h1 = n1t // 2

    def gather(ref, jj):
        return jnp.concatenate([ref[0, jj], ref[1, jj]], axis=0).astype(F32)

    def slab_base(jj):
        return pl.multiple_of((j * grp + jj) * stride, SUBLANES)

    def time_rows():
        slabs = [_slab_read(x_ref, pl.ds(slab_base(jj), 2 * n1t)).astype(BF16) for jj in range(grp)]
        return [jnp.dot(e4_ref[jj], s, preferred_element_type=F32) for jj, s in enumerate(slabs)]

    def write_slabs(us):
        slabs = [jnp.dot(e1_ref[jj], u, preferred_element_type=F32) for jj, u in enumerate(us)]
        for jj, slab in enumerate(slabs):
            _slab_write(x_ref, pl.ds(slab_base(jj), 2 * n1t), slab)

    @pl.when(p == 0)
    def _():
        write_slabs([gather(v_ref, jj).astype(BF16) for jj in range(grp)])

    @pl.when((p == 1) | (p == 3))
    def _():
        for j0 in range(0, kgrp, FFT_INTERLEAVE):
            jjs = list(range(j0, min(j0 + FFT_INTERLEAVE, kgrp)))
            k1s = [j * kgrp + jj for jj in jjs]
            xfs = _spectrum_steps(x_ref, g2_ref, k1s, n1t, stride)
            ys = []
            for jj, xf in zip(jjs, xfs):
                hf = hf_ref[jj].astype(F32)
                xr, xi = xf[:FFT_INNER], xf[FFT_INNER:]
                hr, hi = hf[:FFT_INNER], hf[FFT_INNER:]
                ys.append(jnp.concatenate([xr * hr - xi * hi, xr * hi + xi * hr], axis=0).astype(BF16))
            bbs = [jnp.dot(g3_ref[...], y, preferred_element_type=F32) for y in ys]
            for k1, bb in zip(k1s, bbs):
                _slab_write(x_ref, pl.ds(k1, FFT_INNER, stride=stride), bb[:FFT_INNER])
                _slab_write(x_ref, pl.ds(n1t + k1, FFT_INNER, stride=stride), bb[FFT_INNER:])

    @pl.when(p == 2)
    def _():
        z1s = [gather(x1_ref, jj) * (y + gather(v_ref, jj) * bias_ref[0:1, :]) for jj, y in enumerate(time_rows())]
        z1s = [z1.astype(BF16) for z1 in z1s]
        for jj, z1 in enumerate(z1s):
            z_ref[j * grp + jj] = z1
        write_slabs(z1s)

    @pl.when(p == 4)
    def _():
        for jj, y in enumerate(time_rows()):
            out = gather(x2_ref, jj) * (y + z_ref[j * grp + jj].astype(F32) * bias_ref[1:2, :])
            o_ref[0, jj] = out[:h1].astype(o_ref.dtype)
            o_ref[1, jj] = out[h1:].astype(o_ref.dtype)


def _hyena_call(hyp, hf, e1, e4, g2, g3, bias, n_lat):
    _, bsz, _, ch = hyp.shape
    assert bsz == 2, "the two batch rows are packed as the real and imaginary parts of one FFT"
    n1t = 2 * n_lat // FFT_INNER
    h1 = n1t // 2
    ct = FFT_CHANNELS
    ns, grp, kgrp = _fft_groups(n1t)
    stride = 2 * n1t + SLAB_PAD
    hv = jnp.transpose(hyp.reshape(3, bsz, h1, FFT_INNER, ch), (0, 1, 3, 2, 4))

    def hspec(which, first, last):
        return pl.BlockSpec((None, bsz, grp, h1, ct),
                            lambda c, p, j: (which, 0, _hold(p, j, first, last, ns), 0, c))

    def hf_map(c, p, j):
        order = jnp.where(p >= 2, 1, 0)
        step = jnp.where((p == 1) | (p == 3), j, jnp.where((p == 0) | (p == 2), 0, ns - 1))
        return (order, step, 0, c)

    out = pl.pallas_call(
        functools.partial(_hyena_kernel, n1t=n1t, grp=grp, kgrp=kgrp, stride=stride),
        grid=(ch // ct, 5, ns),
        in_specs=[hspec(0, 0, 2), hspec(1, 2, 2), hspec(2, 4, 4),
                  pl.BlockSpec((None, kgrp, 2 * FFT_INNER, ct), hf_map),
                  pl.BlockSpec((grp, 2 * n1t, n1t), lambda c, p, j: (_hold(p, j, 0, 2, ns), 0, 0)),
                  pl.BlockSpec((grp, n1t, 2 * n1t), lambda c, p, j: (_hold(p, j, 2, 4, ns), 0, 0)),
                  pl.BlockSpec((2 * FFT_INNER, 2 * FFT_INNER), lambda c, p, j: (0, 0)),
                  pl.BlockSpec((2 * FFT_INNER, 2 * FFT_INNER), lambda c, p, j: (0, 0)),
                  pl.BlockSpec((2, ct), lambda c, p, j: (0, c))],
        out_specs=pl.BlockSpec((bsz, grp, h1, ct), lambda c, p, j: (0, _hold(p, j, 4, 4, ns), 0, c)),
        out_shape=jax.ShapeDtypeStruct((bsz, FFT_INNER, h1, ch), BF16),
        scratch_shapes=[pltpu.VMEM((ct // LANES, FFT_INNER * stride, LANES), F32),
                        pltpu.VMEM((FFT_INNER, n1t, ct), BF16)],
        compiler_params=_cparams(("arbitrary", "arbitrary", "arbitrary")),
        name="hyena",
    )(hv, hv, hv, hf, e1, e4, g2, g3, bias)
    return jnp.transpose(out, (0, 2, 1, 3)).reshape(bsz, n_lat, ch)


def _merge_kernel(of_ref, ob_ref, gp_ref, hy_ref, x_ref, mod_ref, dng_ref, wpa_ref, wpb_ref, wo_ref,
                  fg_ref, o_ref, oa_ref, *, d):
    b = pl.program_id(0)
    o = of_ref[0].astype(F32) + ob_ref[0].astype(F32)
    za = gp_ref[0, 0].astype(F32)
    for hh in range(HEADS):
        hs = slice(hh * HEAD_DIM, (hh + 1) * HEAD_DIM)
        oh = o[:, hs]
        ms = jnp.mean(oh * oh, axis=-1, keepdims=True)
        oa_ref[:, hs] = (oh * lax.rsqrt(ms + EPS) * dng_ref[...] * za[:, hs]).astype(BF16)
    y_a = jnp.dot(oa_ref[...], wpa_ref[...], preferred_element_type=F32)
    o_b = hy_ref[0].astype(F32) * gp_ref[1, 0].astype(F32)
    y_b = _bdot(o_b, wpb_ref[...])
    m = gp_ref[2, 0].astype(F32) * y_a + gp_ref[3, 0].astype(F32) * y_b
    y = _bdot(m, wo_ref[...])
    gate = mod_ref[pl.ds(b, 1), :][:, 2 * d:3 * d]
    xn = x_ref[0] + gate * y
    o_ref[0] = xn * lax.rsqrt(jnp.mean(xn * xn, axis=-1, keepdims=True) + EPS) * fg_ref[...]


def _merge_call(o_f, o_b, gp, hy, x, mod, dn_norm_g, w_pa, w_pb, w_out, final_g):
    bsz, n_lat, d = x.shape
    tm = 512 if n_lat % 512 == 0 else 256
    wfull = lambda a: pl.BlockSpec(a.shape, lambda b, t: (0,) * a.ndim)
    tok = pl.BlockSpec((1, tm, d), lambda b, t: (b, t, 0))
    return pl.pallas_call(
        functools.partial(_merge_kernel, d=d),
        grid=(bsz, n_lat // tm),
        in_specs=[tok, tok,
                  pl.BlockSpec((4, 1, tm, d), lambda b, t: (0, b, t, 0)),
                  tok, tok, wfull(mod),
                  pl.BlockSpec((1, HEAD_DIM), lambda b, t: (0, 0)),
                  wfull(w_pa), wfull(w_pb), wfull(w_out),
                  pl.BlockSpec((1, d), lambda b, t: (0, 0))],
        out_specs=tok,
        out_shape=jax.ShapeDtypeStruct((bsz, n_lat, d), F32),
        scratch_shapes=[pltpu.VMEM((tm, d), BF16)],
        compiler_params=_cparams(("arbitrary", "arbitrary")),
        name="merge",
    )(o_f, o_b, gp, hy, x, mod, dn_norm_g.reshape(1, HEAD_DIM), w_pa, w_pb, w_out, final_g.reshape(1, d))


def _position_features(n_lat):
    h1 = n_lat // FFT_INNER
    m = (np.arange(FFT_INNER)[:, None] + FFT_INNER * np.arange(h1)[None, :]).reshape(-1)
    lag = np.concatenate([m, n_lat - m]).astype(np.float64)
    lag[n_lat] = 0.0
    bands = (HY_EMB - 1) // 2
    t = (lag / (n_lat - 1))[:, None]
    wpos = (2.0 * math.pi / n_lat) * lag[:, None]
    fb = np.linspace(1e-4, bands - 1, bands)[None, :]
    pe = np.concatenate([t, np.cos(fb * wpos), -np.sin(fb * wpos)], axis=1)
    pe_pad = np.zeros((2 * n_lat, LANES), np.float32)
    pe_pad[:, :HY_EMB] = pe
    return pe_pad, np.ascontiguousarray(t, dtype=np.float32)


def kernel(x, c, ctx, c_ctx, w_mod, b_mod, norm_g, w_in, dn_conv_w, dn_a_log, dn_dt_bias, dn_norm_g,
           hy_conv_w, hy_conv_b, hy_f_w1, hy_f_b1, hy_f_w2, hy_f_b2, hy_f_w3, hy_f_b3, hy_f_wout,
           hy_f_freq, hy_bias, w_pa, w_pb, w_out, final_g):
    bsz, n_lat, d = x.shape
    n_ctx = ctx.shape[1]
    assert w_mod.shape[0] == 1, "single layer: the context stream is only read through its scan states"
    dn = HEADS * HEAD_DIM
    hy = hy_bias.shape[-1]
    assert n_lat % 256 == 0 and n_ctx % 256 == 0 and dn == d and hy == d

    cvec = jnp.zeros((SUBLANES, d), F32).at[:bsz].set(c).at[bsz].set(c_ctx)
    mod = _mod_call(cvec, w_mod[0], b_mod[0])

    tok = jnp.concatenate([x, ctx], axis=1)
    h = _hnorm_call(tok, mod, norm_g[0], n_lat, bsz)

    w = w_in[0].astype(BF16)
    o_qkv, o_gate, o_hy, o_gp = 0, 4 * dn, 4 * dn + 4 * HEADS, 4 * dn + 4 * HEADS + 3 * hy
    col3 = lambda a, off, n: jnp.transpose(a[:, off:off + 3 * n].reshape(a.shape[0], 3, n), (1, 0, 2))

    lt = n_lat + n_ctx
    tm = PROJ_TILE if n_lat % PROJ_TILE == 0 else 256
    w_qkv, cw_qkv, no_bias = col3(w, o_qkv, dn), col3(dn_conv_w[0], 0, dn), jnp.zeros((3, 1, dn), F32)
    qkv = _convproj_call(h, w_qkv, cw_qkv, no_bias, rowlen=GRID_W, tm=tm, tile0=0, ntiles=n_lat // tm,
                         out_rows=lt, qk_norm=True, name="qkv")
    qkv = _convproj_call(h, w_qkv, cw_qkv, no_bias, rowlen=n_ctx, tm=n_ctx, tile0=n_lat // n_ctx, ntiles=1,
                         out_rows=lt, qk_norm=True, prev=qkv, name="qkv_ctx")

    wg = jnp.zeros((d, LANES), BF16).at[:, :4 * HEADS].set(w[:, o_gate:o_gate + 4 * HEADS])
    lane_pad = lambda a: jnp.zeros((1, LANES), F32).at[0, 2 * HEADS:4 * HEADS].set(a.reshape(-1))
    gates = _gates_call(h, wg, lane_pad(dn_a_log[0]), lane_pad(dn_dt_bias[0]))
    gates_t = jnp.transpose(gates[:, :, :4 * HEADS].reshape(bsz, lt // CHUNK, CHUNK, 4 * HEADS), (0, 1, 3, 2))
    ta, coef = _prep_call(qkv, gates, gates_t)
    o_f, o_b = _scan_call(qkv, gates, coef, ta, n_lat, n_ctx)

    hyp = _convproj_call(h, col3(w, o_hy, hy), col3(hy_conv_w[0], 0, hy), hy_conv_b[0].reshape(3, 1, hy),
                         rowlen=GRID_W, tm=tm, tile0=0, ntiles=n_lat // tm, out_rows=n_lat, qk_norm=False,
                         name="hyproj")
    w4 = jnp.stack([w[:, 3 * dn:4 * dn]] + [w[:, o_gp + i * d:o_gp + (i + 1) * d] for i in range(3)])
    gp = _gproj_call(h, w4, n_lat)

    pe, tcol = _position_features(n_lat)
    w1p = jnp.zeros((LANES, hy_f_w1.shape[-1]), F32).at[:HY_EMB].set(hy_f_w1[0])
    pe2 = jnp.asarray(np.concatenate([pe[:n_lat], pe[n_lat:]], axis=1))
    a3 = _hidden_call(pe2, w1p, hy_f_b1[0], hy_f_w2[0], hy_f_b2[0], hy_f_w3[0], hy_f_b3[0], hy_f_freq[0])
    deltas = np.abs(np.linspace(math.log(HY_DECAY_TARGET) / HY_SLOW_DECAY,
                                math.log(HY_DECAY_TARGET) / HY_FAST_DECAY, hy)).astype(np.float32)
    deltas2 = jnp.asarray(np.tile(deltas, 2)[None, :])
    wside = hy_f_wout[0].reshape(-1, 2, 2 * hy).transpose(1, 0, 2).astype(BF16)
    zside = jnp.zeros_like(wside[0])
    wout2 = jnp.stack([jnp.concatenate([wside[0], zside]), jnp.concatenate([zside, wside[1]])])
    hfull, ssq = _ftime_call(a3, jnp.asarray(tcol), wout2, deltas2, n_lat)
    e1, e4, ef, g2, g3 = (jnp.asarray(t).astype(BF16) for t in _fft_tables(n_lat))
    hf = _ffft_call(hfull, ssq, ef, g2, n_lat)

    yh = _hyena_call(hyp, hf, e1, e4, g2, g3, hy_bias[0], n_lat)

    return _merge_call(o_f, o_b, gp, yh, x, mod, dn_norm_g[0], w_pa[0].astype(BF16), w_pb[0].astype(BF16),
                       w_out[0].astype(BF16), final_g)
```

```python
import functools
import math

import numpy as np
import jax
import jax.numpy as jnp
from jax import lax
from jax.experimental import pallas as pl
from jax.experimental.pallas import tpu as pltpu

F32 = jnp.float32
BF16 = jnp.bfloat16
HIGHEST = lax.Precision.HIGHEST

EPS = 1e-6
HEADS = 8
HEAD_DIM = 128
CHUNK = 64
GRID_W = 64
HY_EMB = 33
HY_DECAY_TARGET = 1e-2
HY_FAST_DECAY = 0.3
HY_SLOW_DECAY = 1.5

LANES = 128
SUBLANES = 8
FFT_INNER = 128
CONV_FFT_STEPS = 8
FILTER_FFT_STEPS = 8
FFT_CHANNELS = 256
FFT_INTERLEAVE = 8
PROJ_TILE = 1024
MXU_TILE = 256
SLAB_PAD = 8
VMEM_LIMIT = 56 * 1024 * 1024
VMEM_LIMIT_CONV = 62 * 1024 * 1024


def _cparams(sem, vmem=VMEM_LIMIT):
    return pltpu.CompilerParams(dimension_semantics=sem, vmem_limit_bytes=vmem)


def _bdot(a, b):
    return jnp.dot(a.astype(BF16), b.astype(BF16), preferred_element_type=F32)


def _silu(x):
    return x * jax.nn.sigmoid(x)


def _mod_kernel(c_ref, w_ref, b_ref, o_ref):
    s = _silu(c_ref[...])
    o_ref[...] = jnp.dot(s, w_ref[...], precision=HIGHEST, preferred_element_type=F32) + b_ref[...]


def _mod_call(cvec, w_mod, b_mod):
    rows, d = cvec.shape
    n = w_mod.shape[1]
    tn = 1024
    return pl.pallas_call(
        _mod_kernel,
        grid=(n // tn,),
        in_specs=[pl.BlockSpec((rows, d), lambda j: (0, 0)),
                  pl.BlockSpec((d, tn), lambda j: (0, j)),
                  pl.BlockSpec((1, tn), lambda j: (0, j))],
        out_specs=pl.BlockSpec((rows, tn), lambda j: (0, j)),
        out_shape=jax.ShapeDtypeStruct((rows, n), F32),
        compiler_params=_cparams(("arbitrary",)),
        name="mod",
    )(cvec, w_mod, b_mod.reshape(1, n))


def _hnorm_kernel(tok_ref, mod_ref, g_ref, o_ref, *, n_lat, tm, d, ctx_row):
    b = pl.program_id(0)
    t = pl.program_id(1)
    row = jnp.where(t * tm >= n_lat, ctx_row, b)
    m = mod_ref[pl.ds(row, 1), :]
    x = tok_ref[0]
    y = x * lax.rsqrt(jnp.mean(x * x, axis=-1, keepdims=True) + EPS) * g_ref[...]
    o_ref[0] = (y * (1.0 + m[:, d:2 * d]) + m[:, 0:d]).astype(BF16)


def _hnorm_call(tok, mod, norm_g, n_lat, ctx_row):
    bsz, lt, d = tok.shape
    tm = 256
    return pl.pallas_call(
        functools.partial(_hnorm_kernel, n_lat=n_lat, tm=tm, d=d, ctx_row=ctx_row),
        grid=(bsz, lt // tm),
        in_specs=[pl.BlockSpec((1, tm, d), lambda b, t: (b, t, 0)),
                  pl.BlockSpec(mod.shape, lambda b, t: (0, 0)),
                  pl.BlockSpec((1, d), lambda b, t: (0, 0))],
        out_specs=pl.BlockSpec((1, tm, d), lambda b, t: (b, t, 0)),
        out_shape=jax.ShapeDtypeStruct((bsz, lt, d), BF16),
        compiler_params=_cparams(("arbitrary", "arbitrary")),
        name="hnorm",
    )(tok, mod, norm_g.reshape(1, d))


@functools.lru_cache(maxsize=None)
def _shift_matrix(rowlen, taps, blk):
    t = np.arange(blk)
    mats = []
    for j in range(taps):
        d = j - taps // 2
        if d != 0:
            ok = ((t + d) >= 0) & ((t + d) < blk) & (t // rowlen == (t + d) // rowlen)
            s = np.zeros((blk, blk), np.float32)
            s[t[ok], t[ok] + d] = 1.0
            mats.append(s)
    return np.concatenate(mats, axis=1)


def _shift_block(rowlen, taps):
    return max(rowlen, MXU_TILE // (taps - 1) // rowlen * rowlen)


def _convproj_kernel(h_ref, w_ref, cw_ref, cb_ref, sh_ref, *rest, tm, qk_norm):
    o_ref = rest[-1]
    j = pl.program_id(0)
    taps = cw_ref.shape[1]
    mid = taps // 2
    blk = sh_ref.shape[0]
    mblk = max(blk, MXU_TILE)
    if qk_norm:
        scale = jnp.where(j == 0, HEAD_DIM ** -0.5, 1.0)
        is_qk = j < 2
    for r in range(tm // mblk):
        zm = jnp.dot(h_ref[0, r * mblk:(r + 1) * mblk, :], w_ref[0], preferred_element_type=F32)
        for g in range(mblk // blk):
            rows = slice(r * mblk + g * blk, r * mblk + (g + 1) * blk)
            zb = zm[g * blk:(g + 1) * blk]
            side = jnp.concatenate(
                [(zb * cw_ref[0, jt:jt + 1, :]).astype(BF16) for jt in range(taps) if jt != mid], axis=0)
            y = zb * cw_ref[0, mid:mid + 1, :] + jnp.dot(sh_ref[...], side, preferred_element_type=F32)
            if not qk_norm:
                o_ref[0, 0, rows, :] = (y + cb_ref[0]).astype(o_ref.dtype)
                continue
            y = _silu(y)
            for hh in range(HEADS):
                yh = y[:, hh * HEAD_DIM:(hh + 1) * HEAD_DIM]
                nrm = jnp.sum(yh * yh, axis=-1, keepdims=True)
                f = jnp.where(is_qk, lax.rsqrt(nrm + EPS) * scale, 1.0)
                o_ref[0, 0, rows, hh * HEAD_DIM:(hh + 1) * HEAD_DIM] = (yh * f).astype(o_ref.dtype)


def _convproj_call(h, w3, cw3, cb3, *, rowlen, tm, tile0, ntiles, out_rows, qk_norm, prev=None, name):
    bsz, _, d = h.shape
    nw = w3.shape[2]
    taps = cw3.shape[1]
    blk = _shift_block(rowlen, taps)
    assert tm % max(blk, MXU_TILE) == 0 and max(blk, MXU_TILE) % blk == 0
    sh = jnp.asarray(_shift_matrix(rowlen, taps, blk)).astype(BF16)
    in_specs = [pl.BlockSpec((1, tm, d), lambda j, b, t: (b, t + tile0, 0)),
                pl.BlockSpec((1, d, nw), lambda j, b, t: (j, 0, 0)),
                pl.BlockSpec((1, cw3.shape[1], nw), lambda j, b, t: (j, 0, 0)),
                pl.BlockSpec((1, 1, nw), lambda j, b, t: (j, 0, 0)),
                pl.BlockSpec(sh.shape, lambda j, b, t: (0, 0))]
    args = [h, w3, cw3, cb3, sh]
    aliases = {}
    if prev is not None:
        in_specs.append(pl.BlockSpec(memory_space=pl.ANY))
        args.append(prev)
        aliases = {5: 0}
    return pl.pallas_call(
        functools.partial(_convproj_kernel, tm=tm, qk_norm=qk_norm),
        grid=(3, bsz, ntiles),
        in_specs=in_specs,
        out_specs=pl.BlockSpec((1, 1, tm, nw), lambda j, b, t: (j, b, t + tile0, 0)),
        out_shape=jax.ShapeDtypeStruct((3, bsz, out_rows, nw), BF16),
        input_output_aliases=aliases,
        compiler_params=_cparams(("arbitrary", "arbitrary", "arbitrary")),
        name=name,
    )(*args)


def _gates_kernel(h_ref, w_ref, alog_ref, dtb_ref, o_ref):
    z = jnp.dot(h_ref[0], w_ref[...], preferred_element_type=F32)
    lane = lax.broadcasted_iota(jnp.int32, z.shape, 1)
    beta = jax.nn.sigmoid(z)
    u = z + dtb_ref[...]
    softplus = jnp.maximum(u, 0.0) + jnp.log(1.0 + jnp.exp(-jnp.abs(u)))
    g = -jnp.exp(alog_ref[...]) * softplus
    o_ref[0] = jnp.where(lane < 2 * HEADS, beta, g)


def _gates_call(h, wg, alog, dtb):
    bsz, lt, d = h.shape
    tm = 256
    return pl.pallas_call(
        _gates_kernel,
        grid=(bsz, lt // tm),
        in_specs=[pl.BlockSpec((1, tm, d), lambda b, t: (b, t, 0)),
                  pl.BlockSpec((d, LANES), lambda b, t: (0, 0)),
                  pl.BlockSpec((1, LANES), lambda b, t: (0, 0)),
                  pl.BlockSpec((1, LANES), lambda b, t: (0, 0))],
        out_specs=pl.BlockSpec((1, tm, LANES), lambda b, t: (b, t, 0)),
        out_shape=jax.ShapeDtypeStruct((bsz, lt, LANES), F32),
        compiler_params=_cparams(("arbitrary", "arbitrary")),
        name="gates",
    )(h, wg, alog, dtb)


def _gproj_kernel(h_ref, w_ref, o_ref):
    j = pl.program_id(0)
    for r in range(h_ref.shape[1] // MXU_TILE):
        rows = slice(r * MXU_TILE, (r + 1) * MXU_TILE)
        z = jnp.dot(h_ref[0, rows, :], w_ref[0], preferred_element_type=F32)
        s = jax.nn.sigmoid(z)
        o_ref[0, 0, rows, :] = jnp.where(j < 2, z * s, s).astype(BF16)


def _gproj_call(h, w4, n_lat):
    bsz, _, d = h.shape
    nw = w4.shape[2]
    tm = PROJ_TILE if n_lat % PROJ_TILE == 0 else 256
    return pl.pallas_call(
        _gproj_kernel,
        grid=(4, bsz, n_lat // tm),
        in_specs=[pl.BlockSpec((1, tm, d), lambda j, b, t: (b, t, 0)),
                  pl.BlockSpec((1, d, nw), lambda j, b, t: (j, 0, 0))],
        out_specs=pl.BlockSpec((1, 1, tm, nw), lambda j, b, t: (j, b, t, 0)),
        out_shape=jax.ShapeDtypeStruct((4, bsz, n_lat, nw), BF16),
        compiler_params=_cparams(("arbitrary", "arbitrary", "arbitrary")),
        name="gproj",
    )(h, w4)


def _unit_tri_inverse(ls):
    c = ls[0].shape[0]
    ri = lax.broadcasted_iota(jnp.int32, (c, c), 0)
    ci = lax.broadcasted_iota(jnp.int32, (c, c), 1)
    same = lambda n: (ri // n) == (ci // n)
    eye = jnp.where(ri == ci, 1.0, 0.0)
    in2 = same(2)
    ts = [eye - jnp.where(in2, l, 0.0) for l in ls]
    n = 2
    while n < c:
        blk = same(2 * n) & jnp.logical_not(same(n))
        offs = [jnp.where(blk, l, 0.0).astype(BF16) for l in ls]
        prods = [_bdot(t, off) for t, off in zip(ts, offs)]
        ts = [t - _bdot(p, t) for p, t in zip(prods, ts)]
        n *= 2
    return ts


_NT = (((1,), (1,)), ((), ()))
_TN = (((0,), (0,)), ((), ()))


def _prep_kernel(qk_ref, g_ref, gt_ref, ta_ref, cf_ref, *, bsz):
    c = CHUNK
    ri = lax.broadcasted_iota(jnp.int32, (c, c), 0)
    ci = lax.broadcasted_iota(jnp.int32, (c, c), 1)
    tri_l = jnp.where(ri >= ci, 1.0, 0.0)
    tri_u = jnp.where(ri <= ci, 1.0, 0.0)
    incl = (ri >= ci, ri <= ci)
    strict = (ri > ci, ri < ci)
    lane = lax.broadcasted_iota(jnp.int32, (c, LANES), 1)
    fwd_lane = lane < 3 * HEADS
    hdot = functools.partial(jnp.dot, precision=HIGHEST, preferred_element_type=F32)

    gs = [g_ref[b] for b in range(bsz)]
    gts = [gt_ref[b, 0] for b in range(bsz)]
    gcol = [(hdot(tri_l, g), hdot(tri_u, g)) for g in gs]
    grow = [(hdot(gt, tri_u), hdot(gt, tri_l)) for gt in gts]
    for b in range(bsz):
        gc = jnp.where(fwd_lane, gcol[b][0], gcol[b][1])
        glast = jnp.where(fwd_lane[0:1], gcol[b][0][c - 1:c], gcol[b][1][0:1])
        cf_ref[0, b] = jnp.exp(gc)
        cf_ref[1, b] = jnp.exp(glast - gc)
        cf_ref[2, b] = jnp.broadcast_to(jnp.exp(glast), (c, LANES))

    bh = [(b, hh) for b in range(bsz) for hh in range(HEADS)]
    hs = lambda hh: slice(hh * HEAD_DIM, (hh + 1) * HEAD_DIM)
    ks = [qk_ref[1, b, :, hs(hh)] for b, hh in bh]
    qs = [qk_ref[0, b, :, hs(hh)] for b, hh in bh]
    kk = [lax.dot_general(k, k, _NT, preferred_element_type=F32) for k in ks]
    qk = [lax.dot_general(q, k, _NT, preferred_element_type=F32) for q, k in zip(qs, ks)]
    inst = [(d, b, hh, i) for d in range(2) for i, (b, hh) in enumerate(bh)]
    decs = []
    for d, b, hh, _ in inst:
        lg = 2 * HEADS + d * HEADS + hh
        diff = gcol[b][d][:, lg:lg + 1] - grow[b][d][lg:lg + 1, :]
        decs.append(jnp.exp(jnp.where(incl[d], diff, -jnp.inf)))
    ls = [jnp.where(strict[d], kk[i] * gs[b][:, d * HEADS + hh:d * HEADS + hh + 1] * dec, 0.0)
          for (d, b, hh, i), dec in zip(inst, decs)]
    ts = _unit_tri_inverse(ls)
    for (d, b, hh, i), t, dec in zip(inst, ts, decs):
        ta_ref[d, b, hh, 0:c, :] = t.astype(BF16)
        ta_ref[d, b, hh, c:2 * c, :] = (qk[i] * dec).astype(BF16)


def _prep_call(qkv, gates, gates_t):
    _, bsz, lt, dn = qkv.shape
    nc = lt // CHUNK
    return pl.pallas_call(
        functools.partial(_prep_kernel, bsz=bsz),
        grid=(nc,),
        in_specs=[pl.BlockSpec((2, bsz, CHUNK, dn), lambda i: (0, 0, i, 0)),
                  pl.BlockSpec((bsz, CHUNK, LANES), lambda i: (0, i, 0)),
                  pl.BlockSpec((bsz, 1, 4 * HEADS, CHUNK), lambda i: (0, i, 0, 0))],
        out_specs=[pl.BlockSpec((None, 2, bsz, HEADS, 2 * CHUNK, CHUNK), lambda i: (i, 0, 0, 0, 0, 0)),
                   pl.BlockSpec((3, bsz, CHUNK, LANES), lambda i: (0, 0, i, 0))],
        out_shape=[jax.ShapeDtypeStruct((nc, 2, bsz, HEADS, 2 * CHUNK, CHUNK), BF16),
                   jax.ShapeDtypeStruct((3, bsz, lt, LANES), F32)],
        compiler_params=_cparams(("arbitrary",)),
        name="prep",
    )(qkv, gates, gates_t)


def _scan_kernel(qf_ref, qb_ref, gf_ref, gb_ref, cff_ref, cfb_ref, taf_ref, tab_ref, of_ref, ob_ref, s_ref,
                 *, bsz):
    i = pl.program_id(0)

    @pl.when(i == 0)
    def _():
        s_ref[...] = jnp.zeros_like(s_ref)

    c = CHUNK
    dirs = ((qf_ref, gf_ref, cff_ref, taf_ref, of_ref), (qb_ref, gb_ref, cfb_ref, tab_ref, ob_ref))
    inst = [(d, b, hh) for d in range(2) for b in range(bsz) for hh in range(HEADS)]
    hs = lambda hh: slice(hh * HEAD_DIM, (hh + 1) * HEAD_DIM)
    qg_kg, vb, kd, el, s_old = [], [], [], [], []
    for d, b, hh in inst:
        q_ref, g_ref, cf_ref = dirs[d][0], dirs[d][1], dirs[d][2]
        lb = d * HEADS + hh
        lg = 2 * HEADS + lb
        beta = g_ref[b, :, lb:lb + 1]
        eg = cf_ref[0, b, :, lg:lg + 1]
        k = q_ref[1, b, :, hs(hh)].astype(F32)
        qg_kg.append(jnp.concatenate([q_ref[0, b, :, hs(hh)].astype(F32) * eg, k * (beta * eg)], axis=0).astype(BF16))
        vb.append(q_ref[2, b, :, hs(hh)].astype(F32) * beta)
        kd.append((k * cf_ref[1, b, :, lg:lg + 1]).astype(BF16))
        el.append(cf_ref[2, b, 0:1, lg:lg + 1])
        s_old.append(s_ref[d, b, hh])
    a1 = [_bdot(x, s) for x, s in zip(qg_kg, s_old)]
    v_new = [_bdot(dirs[d][3][b, hh, 0:c, :], r - a[c:]).astype(BF16)
             for (d, b, hh), r, a in zip(inst, vb, a1)]
    for (d, b, hh), a, vn, kdi, eli, s in zip(inst, a1, v_new, kd, el, s_old):
        o = a[:c] + jnp.dot(dirs[d][3][b, hh, c:2 * c, :], vn, preferred_element_type=F32)
        dirs[d][4][b, :, hs(hh)] = o.astype(BF16)
        s_ref[d, b, hh] = s * eli + lax.dot_general(kdi, vn, _TN, preferred_element_type=F32)


def _scan_call(qkv, gates, coef, ta, n_lat, n_ctx):
    _, bsz, lt, dn = qkv.shape
    ncx = n_lat // CHUNK
    ncc = n_ctx // CHUNK
    nc = ncx + ncc
    cf = lambda i: jnp.where(i < ncc, ncx + i, i - ncc)
    cb = lambda i: nc - 1 - i

    qspec = lambda fn: pl.BlockSpec((3, bsz, CHUNK, dn), lambda i: (0, 0, fn(i), 0))
    gspec = lambda fn: pl.BlockSpec((bsz, CHUNK, LANES), lambda i: (0, fn(i), 0))
    cspec = lambda fn: pl.BlockSpec((3, bsz, CHUNK, LANES), lambda i: (0, 0, fn(i), 0))
    tspec = lambda fn, d: pl.BlockSpec((None, None, bsz, HEADS, 2 * CHUNK, CHUNK),
                                       lambda i: (fn(i), d, 0, 0, 0, 0))
    ospec = lambda fn: pl.BlockSpec((bsz, CHUNK, dn), lambda i: (0, fn(i), 0))
    return pl.pallas_call(
        functools.partial(_scan_kernel, bsz=bsz),
        grid=(nc,),
        in_specs=[qspec(cf), qspec(cb), gspec(cf), gspec(cb), cspec(cf), cspec(cb), tspec(cf, 0), tspec(cb, 1)],
        out_specs=[ospec(cf), ospec(cb)],
        out_shape=[jax.ShapeDtypeStruct((bsz, lt, dn), BF16)] * 2,
        scratch_shapes=[pltpu.VMEM((2, bsz, HEADS, HEAD_DIM, HEAD_DIM), F32)],
        compiler_params=_cparams(("arbitrary",)),
        name="scan",
    )(qkv, qkv, gates, gates, coef, coef, ta, ta)


def _hidden_kernel(pe_ref, w1_ref, b1_ref, w2_ref, b2_ref, w3_ref, b3_ref, f_ref, o_ref):
    f = f_ref[...]
    dot = functools.partial(jnp.dot, precision=HIGHEST, preferred_element_type=F32)
    a = jnp.sin(f * (dot(pe_ref[...], w1_ref[...]) + b1_ref[...]))
    a = jnp.sin(f * (dot(a, w2_ref[...]) + b2_ref[...]))
    o_ref[...] = jnp.sin(f * (dot(a, w3_ref[...]) + b3_ref[...]))


def _hidden_call(pe, w1p, b1, w2, b2, w3, b3, freq):
    rows, pw = pe.shape
    fw = 2 * w2.shape[0]
    tr = 512
    full = lambda a: pl.BlockSpec(a.shape, lambda t: (0,) * a.ndim)
    two = lambda a: jnp.tile(a.reshape(1, -1), (1, 2))
    diag2 = lambda a: jnp.kron(jnp.eye(2, dtype=a.dtype), a)
    args = (diag2(w1p), two(b1), diag2(w2), two(b2), diag2(w3), two(b3), two(freq))
    return pl.pallas_call(
        _hidden_kernel,
        grid=(rows // tr,),
        in_specs=[pl.BlockSpec((tr, pw), lambda t: (t, 0))] + [full(a) for a in args],
        out_specs=pl.BlockSpec((tr, fw), lambda t: (t, 0)),
        out_shape=jax.ShapeDtypeStruct((rows, fw), F32),
        compiler_params=_cparams(("arbitrary",)),
        name="hidden",
    )(pe, *args)


def _ftime_kernel(a_ref, t_ref, w_ref, dl_ref, h_ref, ss_ref, *, n_lat, tr):
    t = pl.program_id(0)
    h = _bdot(a_ref[...], w_ref[...]) * jnp.exp(-t_ref[...] * dl_ref[...])
    row = lax.broadcasted_iota(jnp.int32, (tr, 1), 0) + t * tr
    h = jnp.where(row == n_lat, 0.0, h)
    h_ref[...] = h.astype(h_ref.dtype)

    @pl.when(t == 0)
    def _():
        ss_ref[...] = jnp.zeros_like(ss_ref)

    ss_ref[...] += jnp.sum(h * h, axis=0, keepdims=True)


def _ftime_call(a3, tcol, wout2, deltas2, n_lat):
    rows = tcol.shape[0]
    fw = a3.shape[1]
    ncol = wout2.shape[2]
    tr = 512
    side = lambda t: (t * tr) // n_lat
    return pl.pallas_call(
        functools.partial(_ftime_kernel, n_lat=n_lat, tr=tr),
        grid=(rows // tr,),
        in_specs=[pl.BlockSpec((tr, fw), lambda t: (t % (n_lat // tr), 0)),
                  pl.BlockSpec((tr, 1), lambda t: (t, 0)),
                  pl.BlockSpec((None, fw, ncol), lambda t: (side(t), 0, 0)),
                  pl.BlockSpec((1, ncol), lambda t: (0, 0))],
        out_specs=[pl.BlockSpec((tr, ncol), lambda t: (t, 0)),
                   pl.BlockSpec((1, ncol), lambda t: (0, 0))],
        out_shape=[jax.ShapeDtypeStruct((rows, ncol), BF16), jax.ShapeDtypeStruct((1, ncol), F32)],
        compiler_params=_cparams(("arbitrary",)),
        name="ftime",
    )(a3, tcol, wout2, deltas2)


@functools.lru_cache(maxsize=None)
def _fft_tables(n_lat):
    n = 2 * n_lat
    n1t = n // FFT_INNER
    h1 = n1t // 2
    n2 = np.arange(FFT_INNER, dtype=np.int64)[:, None, None]
    k1 = np.arange(n1t, dtype=np.int64)[None, :, None]
    n1 = np.arange(h1, dtype=np.int64)[None, None, :]
    ang = (-2.0 * np.pi / n) * ((k1 * (FFT_INNER * n1 + n2)) % n)
    er, ei = np.cos(ang), np.sin(ang)
    e1 = np.concatenate([np.concatenate([er, -ei], axis=2), np.concatenate([ei, er], axis=2)], axis=1)
    e4 = np.transpose(e1, (0, 2, 1))
    sgn = np.where(np.arange(n1t) % 2 == 0, 1.0, -1.0)[None, :, None]
    ef = np.concatenate([np.concatenate([er, sgn * er], axis=2), np.concatenate([ei, sgn * ei], axis=2)], axis=1)
    a = np.arange(FFT_INNER, dtype=np.int64)
    ang2 = (-2.0 * np.pi / FFT_INNER) * ((a[:, None] * a[None, :]) % FFT_INNER)
    gr, gi = np.cos(ang2), np.sin(ang2)
    g2 = np.block([[gr, -gi], [gi, gr]])
    f32 = lambda x: np.ascontiguousarray(x, dtype=np.float32)
    return f32(e1), f32(e4), f32(ef), f32(g2), f32(g2.T)


def _hold(p, j, first, last, nsteps):
    active = (p >= first) & (p <= last) & (lax.rem(p - first, 2) == 0)
    return jnp.where(active, j, jnp.where(p < first, 0, nsteps - 1))


def _slab_read(x_ref, rows):
    return jnp.concatenate([x_ref[c, rows, :] for c in range(x_ref.shape[0])], axis=1)


def _slab_write(x_ref, rows, val):
    for c in range(x_ref.shape[0]):
        x_ref[c, rows, :] = val[:, c * LANES:(c + 1) * LANES]


def _spectrum_steps(x_ref, g2_ref, k1s, n1t, stride):
    cols = [jnp.concatenate([_slab_read(x_ref, pl.ds(k1, FFT_INNER, stride=stride)),
                             _slab_read(x_ref, pl.ds(n1t + k1, FFT_INNER, stride=stride))], axis=0).astype(BF16)
            for k1 in k1s]
    return [jnp.dot(g2_ref[...], c, preferred_element_type=F32) for c in cols]


def _ffft_kernel(h_ref, ef_ref, g2_ref, ss_ref, o_ref, x_ref, *, n1t, grp, kgrp, stride, inv_n):
    p = pl.program_id(1)
    j = pl.program_id(2)

    @pl.when(p == 0)
    def _():
        for j0 in range(0, grp, FFT_INTERLEAVE):
            jjs = list(range(j0, min(j0 + FFT_INTERLEAVE, grp)))
            us = [jnp.concatenate([h_ref[0, jj], h_ref[1, jj]], axis=0) for jj in jjs]
            slabs = [_bdot(ef_ref[jj], u) for jj, u in zip(jjs, us)]
            for jj, slab in zip(jjs, slabs):
                base = pl.multiple_of((j * grp + jj) * stride, SUBLANES)
                _slab_write(x_ref, pl.ds(base, 2 * n1t), slab)

    @pl.when(p == 1)
    def _():
        scale = lax.rsqrt(ss_ref[...] + EPS) * inv_n
        for j0 in range(0, kgrp, FFT_INTERLEAVE):
            jjs = list(range(j0, min(j0 + FFT_INTERLEAVE, kgrp)))
            xfs = _spectrum_steps(x_ref, g2_ref, [j * kgrp + jj for jj in jjs], n1t, stride)
            for jj, xf in zip(jjs, xfs):
                o_ref[jj] = (xf * scale).astype(BF16)


def _fft_groups(n1t, ns):
    ns = min(ns, n1t)
    assert FFT_INNER % ns == 0 and n1t % ns == 0
    return ns, FFT_INNER // ns, n1t // ns


def _ffft_call(hfull, ssq, ef, g2, n_lat):
    ncol = hfull.shape[1]
    n1t = 2 * n_lat // FFT_INNER
    h1 = n1t // 2
    ct = FFT_CHANNELS
    ns, grp, kgrp = _fft_groups(n1t, FILTER_FFT_STEPS)
    stride = 2 * n1t + SLAB_PAD
    hv = hfull.reshape(2, FFT_INNER, h1, ncol)
    nct = ncol // ct
    cpo = nct // 2
    return pl.pallas_call(
        functools.partial(_ffft_kernel, n1t=n1t, grp=grp, kgrp=kgrp, stride=stride, inv_n=1.0 / (2 * n_lat)),
        grid=(nct, 2, ns),
        in_specs=[pl.BlockSpec((2, grp, h1, ct), lambda c, p, j: (0, _hold(p, j, 0, 0, ns), 0, c)),
                  pl.BlockSpec((grp, 2 * n1t, n1t), lambda c, p, j: (_hold(p, j, 0, 0, ns), 0, 0)),
                  pl.BlockSpec((2 * FFT_INNER, 2 * FFT_INNER), lambda c, p, j: (0, 0)),
                  pl.BlockSpec((1, ct), lambda c, p, j: (0, c))],
        out_specs=pl.BlockSpec((None, kgrp, 2 * FFT_INNER, ct),
                               lambda c, p, j: (c // cpo, _hold(p, j, 1, 1, ns), 0, c % cpo)),
        out_shape=jax.ShapeDtypeStruct((2, n1t, 2 * FFT_INNER, ncol // 2), BF16),
        scratch_shapes=[pltpu.VMEM((ct // LANES, FFT_INNER * stride, LANES), F32)],
        compiler_params=_cparams(("arbitrary", "arbitrary", "arbitrary")),
        name="ffft",
    )(hv, ef, g2, ssq)


def _hyena_kernel(v_ref, x1_ref, x2_ref, hf_ref, e1_ref, e4_ref, g2_ref, g3_ref, bias_ref,
                  o_ref, x_ref, z_ref, *, n1t, grp, kgrp, stride):
    p = pl.program_id(1)
    j = pl.program_id(2)
    h1 = n1t // 2

    def gather(ref, jj):
        return jnp.concatenate([ref[0, jj], ref[1, jj]], axis=0).astype(F32)

    def slab_base(jj):
        return pl.multiple_of((j * grp + jj) * stride, SUBLANES)

    def time_rows(jjs):
        slabs = [_slab_read(x_ref, pl.ds(slab_base(jj), 2 * n1t)).astype(BF16) for jj in jjs]
        return [jnp.dot(e4_ref[jj], s, preferred_element_type=F32) for jj, s in zip(jjs, slabs)]

    def write_slabs(jjs, us):
        slabs = [jnp.dot(e1_ref[jj], u, preferred_element_type=F32) for jj, u in zip(jjs, us)]
        for jj, slab in zip(jjs, slabs):
            _slab_write(x_ref, pl.ds(slab_base(jj), 2 * n1t), slab)

    def chunks(n):
        return [list(range(j0, min(j0 + FFT_INTERLEAVE, n))) for j0 in range(0, n, FFT_INTERLEAVE)]

    @pl.when(p == 0)
    def _():
        for jjs in chunks(grp):
            write_slabs(jjs, [gather(v_ref, jj).astype(BF16) for jj in jjs])

    @pl.when((p == 1) | (p == 3))
    def _():
        for jjs in chunks(kgrp):
            k1s = [j * kgrp + jj for jj in jjs]
            xfs = _spectrum_steps(x_ref, g2_ref, k1s, n1t, stride)
            ys = []
            for jj, xf in zip(jjs, xfs):
                hf = hf_ref[jj].astype(F32)
                xr, xi = xf[:FFT_INNER], xf[FFT_INNER:]
                hr, hi = hf[:FFT_INNER], hf[FFT_INNER:]
                ys.append(jnp.concatenate([xr * hr - xi * hi, xr * hi + xi * hr], axis=0).astype(BF16))
            bbs = [jnp.dot(g3_ref[...], y, preferred_element_type=F32) for y in ys]
            for k1, bb in zip(k1s, bbs):
                _slab_write(x_ref, pl.ds(k1, FFT_INNER, stride=stride), bb[:FFT_INNER])
                _slab_write(x_ref, pl.ds(n1t + k1, FFT_INNER, stride=stride), bb[FFT_INNER:])

    @pl.when(p == 2)
    def _():
        for jjs in chunks(grp):
            z1s = [(gather(x1_ref, jj) * (y + gather(v_ref, jj) * bias_ref[0:1, :])).astype(BF16)
                   for jj, y in zip(jjs, time_rows(jjs))]
            for jj, z1 in zip(jjs, z1s):
                z_ref[j * grp + jj] = z1
            write_slabs(jjs, z1s)

    @pl.when(p == 4)
    def _():
        for jjs in chunks(grp):
            for jj, y in zip(jjs, time_rows(jjs)):
                out = gather(x2_ref, jj) * (y + z_ref[j * grp + jj].astype(F32) * bias_ref[1:2, :])
                o_ref[0, jj] = out[:h1].astype(o_ref.dtype)
                o_ref[1, jj] = out[h1:].astype(o_ref.dtype)


def _hyena_call(hyp, hf, e1, e4, g2, g3, bias, n_lat):
    _, bsz, _, ch = hyp.shape
    assert bsz == 2, "the two batch rows are packed as the real and imaginary parts of one FFT"
    n1t = 2 * n_lat // FFT_INNER
    h1 = n1t // 2
    ct = FFT_CHANNELS
    ns, grp, kgrp = _fft_groups(n1t, CONV_FFT_STEPS)
    stride = 2 * n1t + SLAB_PAD
    hv = jnp.transpose(hyp.reshape(3, bsz, h1, FFT_INNER, ch), (0, 1, 3, 2, 4))

    def hspec(which, first, last):
        return pl.BlockSpec((None, bsz, grp, h1, ct),
                            lambda c, p, j: (which, 0, _hold(p, j, first, last, ns), 0, c))

    def hf_map(c, p, j):
        order = jnp.where(p >= 2, 1, 0)
        step = jnp.where((p == 1) | (p == 3), j, jnp.where((p == 0) | (p == 2), 0, ns - 1))
        return (order, step, 0, c)

    out = pl.pallas_call(
        functools.partial(_hyena_kernel, n1t=n1t, grp=grp, kgrp=kgrp, stride=stride),
        grid=(ch // ct, 5, ns),
        in_specs=[hspec(0, 0, 2), hspec(1, 2, 2), hspec(2, 4, 4),
                  pl.BlockSpec((None, kgrp, 2 * FFT_INNER, ct), hf_map),
                  pl.BlockSpec((grp, 2 * n1t, n1t), lambda c, p, j: (_hold(p, j, 0, 2, ns), 0, 0)),
                  pl.BlockSpec((grp, n1t, 2 * n1t), lambda c, p, j: (_hold(p, j, 2, 4, ns), 0, 0)),
                  pl.BlockSpec((2 * FFT_INNER, 2 * FFT_INNER), lambda c, p, j: (0, 0)),
                  pl.BlockSpec((2 * FFT_INNER, 2 * FFT_INNER), lambda c, p, j: (0, 0)),
                  pl.BlockSpec((2, ct), lambda c, p, j: (0, c))],
        out_specs=pl.BlockSpec((bsz, grp, h1, ct), lambda c, p, j: (0, _hold(p, j, 4, 4, ns), 0, c)),
        out_shape=jax.ShapeDtypeStruct((bsz, FFT_INNER, h1, ch), BF16),
        scratch_shapes=[pltpu.VMEM((ct // LANES, FFT_INNER * stride, LANES), F32),
                        pltpu.VMEM((FFT_INNER, n1t, ct), BF16)],
        compiler_params=_cparams(("arbitrary", "arbitrary", "arbitrary"), VMEM_LIMIT_CONV),
        name="hyena",
    )(hv, hv, hv, hf, e1, e4, g2, g3, bias)
    return jnp.transpose(out, (0, 2, 1, 3)).reshape(bsz, n_lat, ch)


def _merge_kernel(of_ref, ob_ref, gp_ref, hy_ref, x_ref, mod_ref, dng_ref, wpa_ref, wpb_ref, wo_ref,
                  fg_ref, o_ref, oa_ref, *, d):
    b = pl.program_id(0)
    o = of_ref[0].astype(F32) + ob_ref[0].astype(F32)
    za = gp_ref[0, 0].astype(F32)
    for hh in range(HEADS):
        hs = slice(hh * HEAD_DIM, (hh + 1) * HEAD_DIM)
        oh = o[:, hs]
        ms = jnp.mean(oh * oh, axis=-1, keepdims=True)
        oa_ref[:, hs] = (oh * lax.rsqrt(ms + EPS) * dng_ref[...] * za[:, hs]).astype(BF16)
    y_a = jnp.dot(oa_ref[...], wpa_ref[...], preferred_element_type=F32)
    o_b = hy_ref[0].astype(F32) * gp_ref[1, 0].astype(F32)
    y_b = _bdot(o_b, wpb_ref[...])
    m = gp_ref[2, 0].astype(F32) * y_a + gp_ref[3, 0].astype(F32) * y_b
    y = _bdot(m, wo_ref[...])
    gate = mod_ref[pl.ds(b, 1), :][:, 2 * d:3 * d]
    xn = x_ref[0] + gate * y
    o_ref[0] = xn * lax.rsqrt(jnp.mean(xn * xn, axis=-1, keepdims=True) + EPS) * fg_ref[...]


def _merge_call(o_f, o_b, gp, hy, x, mod, dn_norm_g, w_pa, w_pb, w_out, final_g):
    bsz, n_lat, d = x.shape
    tm = 512 if n_lat % 512 == 0 else 256
    wfull = lambda a: pl.BlockSpec(a.shape, lambda b, t: (0,) * a.ndim)
    tok = pl.BlockSpec((1, tm, d), lambda b, t: (b, t, 0))
    return pl.pallas_call(
        functools.partial(_merge_kernel, d=d),
        grid=(bsz, n_lat // tm),
        in_specs=[tok, tok,
                  pl.BlockSpec((4, 1, tm, d), lambda b, t: (0, b, t, 0)),
                  tok, tok, wfull(mod),
                  pl.BlockSpec((1, HEAD_DIM), lambda b, t: (0, 0)),
                  wfull(w_pa), wfull(w_pb), wfull(w_out),
                  pl.BlockSpec((1, d), lambda b, t: (0, 0))],
        out_specs=tok,
        out_shape=jax.ShapeDtypeStruct((bsz, n_lat, d), F32),
        scratch_shapes=[pltpu.VMEM((tm, d), BF16)],
        compiler_params=_cparams(("arbitrary", "arbitrary")),
        name="merge",
    )(o_f, o_b, gp, hy, x, mod, dn_norm_g.reshape(1, HEAD_DIM), w_pa, w_pb, w_out, final_g.reshape(1, d))


def _position_features(n_lat):
    h1 = n_lat // FFT_INNER
    m = (np.arange(FFT_INNER)[:, None] + FFT_INNER * np.arange(h1)[None, :]).reshape(-1)
    lag = np.concatenate([m, n_lat - m]).astype(np.float64)
    lag[n_lat] = 0.0
    bands = (HY_EMB - 1) // 2
    t = (lag / (n_lat - 1))[:, None]
    wpos = (2.0 * math.pi / n_lat) * lag[:, None]
    fb = np.linspace(1e-4, bands - 1, bands)[None, :]
    pe = np.concatenate([t, np.cos(fb * wpos), -np.sin(fb * wpos)], axis=1)
    pe_pad = np.zeros((2 * n_lat, LANES), np.float32)
    pe_pad[:, :HY_EMB] = pe
    return pe_pad, np.ascontiguousarray(t, dtype=np.float32)


def kernel(x, c, ctx, c_ctx, w_mod, b_mod, norm_g, w_in, dn_conv_w, dn_a_log, dn_dt_bias, dn_norm_g,
           hy_conv_w, hy_conv_b, hy_f_w1, hy_f_b1, hy_f_w2, hy_f_b2, hy_f_w3, hy_f_b3, hy_f_wout,
           hy_f_freq, hy_bias, w_pa, w_pb, w_out, final_g):
    bsz, n_lat, d = x.shape
    n_ctx = ctx.shape[1]
    assert w_mod.shape[0] == 1, "single layer: the context stream is only read through its scan states"
    dn = HEADS * HEAD_DIM
    hy = hy_bias.shape[-1]
    assert n_lat % 256 == 0 and n_ctx % 256 == 0 and dn == d and hy == d

    cvec = jnp.zeros((SUBLANES, d), F32).at[:bsz].set(c).at[bsz].set(c_ctx)
    mod = _mod_call(cvec, w_mod[0], b_mod[0])

    tok = jnp.concatenate([x, ctx], axis=1)
    h = _hnorm_call(tok, mod, norm_g[0], n_lat, bsz)

    w = w_in[0].astype(BF16)
    o_qkv, o_gate, o_hy, o_gp = 0, 4 * dn, 4 * dn + 4 * HEADS, 4 * dn + 4 * HEADS + 3 * hy
    col3 = lambda a, off, n: jnp.transpose(a[:, off:off + 3 * n].reshape(a.shape[0], 3, n), (1, 0, 2))

    lt = n_lat + n_ctx
    tm = PROJ_TILE if n_lat % PROJ_TILE == 0 else 256
    w_qkv, cw_qkv, no_bias = col3(w, o_qkv, dn), col3(dn_conv_w[0], 0, dn), jnp.zeros((3, 1, dn), F32)
    qkv = _convproj_call(h, w_qkv, cw_qkv, no_bias, rowlen=GRID_W, tm=tm, tile0=0, ntiles=n_lat // tm,
                         out_rows=lt, qk_norm=True, name="qkv")
    qkv = _convproj_call(h, w_qkv, cw_qkv, no_bias, rowlen=n_ctx, tm=n_ctx, tile0=n_lat // n_ctx, ntiles=1,
                         out_rows=lt, qk_norm=True, prev=qkv, name="qkv_ctx")

    wg = jnp.zeros((d, LANES), BF16).at[:, :4 * HEADS].set(w[:, o_gate:o_gate + 4 * HEADS])
    lane_pad = lambda a: jnp.zeros((1, LANES), F32).at[0, 2 * HEADS:4 * HEADS].set(a.reshape(-1))
    gates = _gates_call(h, wg, lane_pad(dn_a_log[0]), lane_pad(dn_dt_bias[0]))
    gates_t = jnp.transpose(gates[:, :, :4 * HEADS].reshape(bsz, lt // CHUNK, CHUNK, 4 * HEADS), (0, 1, 3, 2))
    ta, coef = _prep_call(qkv, gates, gates_t)
    o_f, o_b = _scan_call(qkv, gates, coef, ta, n_lat, n_ctx)

    hyp = _convproj_call(h, col3(w, o_hy, hy), col3(hy_conv_w[0], 0, hy), hy_conv_b[0].reshape(3, 1, hy),
                         rowlen=GRID_W, tm=tm, tile0=0, ntiles=n_lat // tm, out_rows=n_lat, qk_norm=False,
                         name="hyproj")
    w4 = jnp.stack([w[:, 3 * dn:4 * dn]] + [w[:, o_gp + i * d:o_gp + (i + 1) * d] for i in range(3)])
    gp = _gproj_call(h, w4, n_lat)

    pe, tcol = _position_features(n_lat)
    w1p = jnp.zeros((LANES, hy_f_w1.shape[-1]), F32).at[:HY_EMB].set(hy_f_w1[0])
    pe2 = jnp.asarray(np.concatenate([pe[:n_lat], pe[n_lat:]], axis=1))
    a3 = _hidden_call(pe2, w1p, hy_f_b1[0], hy_f_w2[0], hy_f_b2[0], hy_f_w3[0], hy_f_b3[0], hy_f_freq[0])
    deltas = np.abs(np.linspace(math.log(HY_DECAY_TARGET) / HY_SLOW_DECAY,
                                math.log(HY_DECAY_TARGET) / HY_FAST_DECAY, hy)).astype(np.float32)
    deltas2 = jnp.asarray(np.tile(deltas, 2)[None, :])
    wside = hy_f_wout[0].reshape(-1, 2, 2 * hy).transpose(1, 0, 2).astype(BF16)
    zside = jnp.zeros_like(wside[0])
    wout2 = jnp.stack([jnp.concatenate([wside[0], zside]), jnp.concatenate([zside, wside[1]])])
    hfull, ssq = _ftime_call(a3, jnp.asarray(tcol), wout2, deltas2, n_lat)
    e1, e4, ef, g2, g3 = (jnp.asarray(t).astype(BF16) for t in _fft_tables(n_lat))
    hf = _ffft_call(hfull, ssq, ef, g2, n_lat)

    yh = _hyena_call(hyp, hf, e1, e4, g2, g3, hy_bias[0], n_lat)

    return _merge_call(o_f, o_b, gp, yh, x, mod, dn_norm_g[0], w_pa[0].astype(BF16), w_pb[0].astype(BF16),
                       w_out[0].astype(BF16), final_g)
```

```python
import functools
import math

import numpy as np
import jax
import jax.numpy as jnp
from jax import lax
from jax.experimental import pallas as pl
from jax.experimental.pallas import tpu as pltpu

F32 = jnp.float32
BF16 = jnp.bfloat16
HIGHEST = lax.Precision.HIGHEST

EPS = 1e-6
HEADS = 8
HEAD_DIM = 128
CHUNK = 64
GRID_W = 64
HY_EMB = 33
HY_DECAY_TARGET = 1e-2
HY_FAST_DECAY = 0.3
HY_SLOW_DECAY = 1.5

LANES = 128
SUBLANES = 8
FFT_INNER = 128
CONV_FFT_STEPS = 8
FILTER_FFT_STEPS = 8
FFT_CHANNELS = 256
FFT_INTERLEAVE = 8
PROJ_TILE = 1024
MXU_TILE = 256
SLAB_PAD = 8
VMEM_LIMIT = 56 * 1024 * 1024
VMEM_LIMIT_CONV = 62 * 1024 * 1024


def _cparams(sem, vmem=VMEM_LIMIT):
    return pltpu.CompilerParams(dimension_semantics=sem, vmem_limit_bytes=vmem)


def _bdot(a, b):
    return jnp.dot(a.astype(BF16), b.astype(BF16), preferred_element_type=F32)


def _silu(x):
    return x * jax.nn.sigmoid(x)


def _mod_kernel(c_ref, w_ref, b_ref, o_ref):
    s = _silu(c_ref[...])
    o_ref[...] = jnp.dot(s, w_ref[...], precision=HIGHEST, preferred_element_type=F32) + b_ref[...]


def _mod_call(cvec, w_mod, b_mod):
    rows, d = cvec.shape
    n = w_mod.shape[1]
    tn = 1024
    return pl.pallas_call(
        _mod_kernel,
        grid=(n // tn,),
        in_specs=[pl.BlockSpec((rows, d), lambda j: (0, 0)),
                  pl.BlockSpec((d, tn), lambda j: (0, j)),
                  pl.BlockSpec((1, tn), lambda j: (0, j))],
        out_specs=pl.BlockSpec((rows, tn), lambda j: (0, j)),
        out_shape=jax.ShapeDtypeStruct((rows, n), F32),
        compiler_params=_cparams(("arbitrary",)),
        name="mod",
    )(cvec, w_mod, b_mod.reshape(1, n))


def _hnorm_kernel(tok_ref, mod_ref, g_ref, o_ref, *, n_lat, tm, d, ctx_row):
    b = pl.program_id(0)
    t = pl.program_id(1)
    row = jnp.where(t * tm >= n_lat, ctx_row, b)
    m = mod_ref[pl.ds(row, 1), :]
    x = tok_ref[0]
    y = x * lax.rsqrt(jnp.mean(x * x, axis=-1, keepdims=True) + EPS) * g_ref[...]
    o_ref[0] = (y * (1.0 + m[:, d:2 * d]) + m[:, 0:d]).astype(BF16)


def _hnorm_call(tok, mod, norm_g, n_lat, ctx_row):
    bsz, lt, d = tok.shape
    tm = 256
    return pl.pallas_call(
        functools.partial(_hnorm_kernel, n_lat=n_lat, tm=tm, d=d, ctx_row=ctx_row),
        grid=(bsz, lt // tm),
        in_specs=[pl.BlockSpec((1, tm, d), lambda b, t: (b, t, 0)),
                  pl.BlockSpec(mod.shape, lambda b, t: (0, 0)),
                  pl.BlockSpec((1, d), lambda b, t: (0, 0))],
        out_specs=pl.BlockSpec((1, tm, d), lambda b, t: (b, t, 0)),
        out_shape=jax.ShapeDtypeStruct((bsz, lt, d), BF16),
        compiler_params=_cparams(("arbitrary", "arbitrary")),
        name="hnorm",
    )(tok, mod, norm_g.reshape(1, d))


@functools.lru_cache(maxsize=None)
def _shift_matrix(rowlen, taps, blk):
    t = np.arange(blk)
    mats = []
    for j in range(taps):
        d = j - taps // 2
        if d != 0:
            ok = ((t + d) >= 0) & ((t + d) < blk) & (t // rowlen == (t + d) // rowlen)
            s = np.zeros((blk, blk), np.float32)
            s[t[ok], t[ok] + d] = 1.0
            mats.append(s)
    return np.concatenate(mats, axis=1)


def _shift_block(rowlen, taps):
    return max(rowlen, MXU_TILE // (taps - 1) // rowlen * rowlen)


def _convproj_kernel(h_ref, w_ref, cw_ref, cb_ref, sh_ref, *rest, tm, qk_norm):
    o_ref = rest[-1]
    j = pl.program_id(0)
    taps = cw_ref.shape[1]
    mid = taps // 2
    blk = sh_ref.shape[0]
    mblk = max(blk, MXU_TILE)
    if qk_norm:
        scale = jnp.where(j == 0, HEAD_DIM ** -0.5, 1.0)
        is_qk = j < 2
    for r in range(tm // mblk):
        zm = jnp.dot(h_ref[0, r * mblk:(r + 1) * mblk, :], w_ref[0], preferred_element_type=F32)
        for g in range(mblk // blk):
            rows = slice(r * mblk + g * blk, r * mblk + (g + 1) * blk)
            zb = zm[g * blk:(g + 1) * blk]
            side = jnp.concatenate(
                [(zb * cw_ref[0, jt:jt + 1, :]).astype(BF16) for jt in range(taps) if jt != mid], axis=0)
            y = zb * cw_ref[0, mid:mid + 1, :] + jnp.dot(sh_ref[...], side, preferred_element_type=F32)
            if not qk_norm:
                o_ref[0, 0, rows, :] = (y + cb_ref[0]).astype(o_ref.dtype)
                continue
            y = _silu(y)
            for hh in range(HEADS):
                yh = y[:, hh * HEAD_DIM:(hh + 1) * HEAD_DIM]
                nrm = jnp.sum(yh * yh, axis=-1, keepdims=True)
                f = jnp.where(is_qk, lax.rsqrt(nrm + EPS) * scale, 1.0)
                o_ref[0, 0, rows, hh * HEAD_DIM:(hh + 1) * HEAD_DIM] = (yh * f).astype(o_ref.dtype)


def _convproj_call(h, w3, cw3, cb3, *, rowlen, tm, tile0, ntiles, out_rows, qk_norm, prev=None, name):
    bsz, _, d = h.shape
    nw = w3.shape[2]
    taps = cw3.shape[1]
    blk = _shift_block(rowlen, taps)
    assert tm % max(blk, MXU_TILE) == 0 and max(blk, MXU_TILE) % blk == 0
    sh = jnp.asarray(_shift_matrix(rowlen, taps, blk)).astype(BF16)
    in_specs = [pl.BlockSpec((1, tm, d), lambda j, b, t: (b, t + tile0, 0)),
                pl.BlockSpec((1, d, nw), lambda j, b, t: (j, 0, 0)),
                pl.BlockSpec((1, cw3.shape[1], nw), lambda j, b, t: (j, 0, 0)),
                pl.BlockSpec((1, 1, nw), lambda j, b, t: (j, 0, 0)),
                pl.BlockSpec(sh.shape, lambda j, b, t: (0, 0))]
    args = [h, w3, cw3, cb3, sh]
    aliases = {}
    if prev is not None:
        in_specs.append(pl.BlockSpec(memory_space=pl.ANY))
        args.append(prev)
        aliases = {5: 0}
    return pl.pallas_call(
        functools.partial(_convproj_kernel, tm=tm, qk_norm=qk_norm),
        grid=(3, bsz, ntiles),
        in_specs=in_specs,
        out_specs=pl.BlockSpec((1, 1, tm, nw), lambda j, b, t: (j, b, t + tile0, 0)),
        out_shape=jax.ShapeDtypeStruct((3, bsz, out_rows, nw), BF16),
        input_output_aliases=aliases,
        compiler_params=_cparams(("arbitrary", "arbitrary", "arbitrary")),
        name=name,
    )(*args)


def _gates_kernel(h_ref, w_ref, alog_ref, dtb_ref, o_ref):
    z = jnp.dot(h_ref[0], w_ref[...], preferred_element_type=F32)
    lane = lax.broadcasted_iota(jnp.int32, z.shape, 1)
    beta = jax.nn.sigmoid(z)
    u = z + dtb_ref[...]
    softplus = jnp.maximum(u, 0.0) + jnp.log(1.0 + jnp.exp(-jnp.abs(u)))
    g = -jnp.exp(alog_ref[...]) * softplus
    o_ref[0] = jnp.where(lane < 2 * HEADS, beta, g)


def _gates_call(h, wg, alog, dtb):
    bsz, lt, d = h.shape
    tm = 256
    return pl.pallas_call(
        _gates_kernel,
        grid=(bsz, lt // tm),
        in_specs=[pl.BlockSpec((1, tm, d), lambda b, t: (b, t, 0)),
                  pl.BlockSpec((d, LANES), lambda b, t: (0, 0)),
                  pl.BlockSpec((1, LANES), lambda b, t: (0, 0)),
                  pl.BlockSpec((1, LANES), lambda b, t: (0, 0))],
        out_specs=pl.BlockSpec((1, tm, LANES), lambda b, t: (b, t, 0)),
        out_shape=jax.ShapeDtypeStruct((bsz, lt, LANES), F32),
        compiler_params=_cparams(("arbitrary", "arbitrary")),
        name="gates",
    )(h, wg, alog, dtb)


def _gproj_kernel(h_ref, w_ref, o_ref):
    j = pl.program_id(0)
    for r in range(h_ref.shape[1] // MXU_TILE):
        rows = slice(r * MXU_TILE, (r + 1) * MXU_TILE)
        z = jnp.dot(h_ref[0, rows, :], w_ref[0], preferred_element_type=F32)
        s = jax.nn.sigmoid(z)
        o_ref[0, 0, rows, :] = jnp.where(j < 2, z * s, s).astype(BF16)


def _gproj_call(h, w4, n_lat):
    bsz, _, d = h.shape
    nw = w4.shape[2]
    tm = PROJ_TILE if n_lat % PROJ_TILE == 0 else 256
    return pl.pallas_call(
        _gproj_kernel,
        grid=(4, bsz, n_lat // tm),
        in_specs=[pl.BlockSpec((1, tm, d), lambda j, b, t: (b, t, 0)),
                  pl.BlockSpec((1, d, nw), lambda j, b, t: (j, 0, 0))],
        out_specs=pl.BlockSpec((1, 1, tm, nw), lambda j, b, t: (j, b, t, 0)),
        out_shape=jax.ShapeDtypeStruct((4, bsz, n_lat, nw), BF16),
        compiler_params=_cparams(("arbitrary", "arbitrary", "arbitrary")),
        name="gproj",
    )(h, w4)


def _unit_tri_inverse(ls):
    c = ls[0].shape[0]
    ri = lax.broadcasted_iota(jnp.int32, (c, c), 0)
    ci = lax.broadcasted_iota(jnp.int32, (c, c), 1)
    same = lambda n: (ri // n) == (ci // n)
    eye = jnp.where(ri == ci, 1.0, 0.0)
    in2 = same(2)
    ts = [eye - jnp.where(in2, l, 0.0) for l in ls]
    n = 2
    while n < c:
        blk = same(2 * n) & jnp.logical_not(same(n))
        offs = [jnp.where(blk, l, 0.0).astype(BF16) for l in ls]
        prods = [_bdot(t, off) for t, off in zip(ts, offs)]
        ts = [t - _bdot(p, t) for p, t in zip(prods, ts)]
        n *= 2
    return ts


_NT = (((1,), (1,)), ((), ()))
_TN = (((0,), (0,)), ((), ()))


def _prep_kernel(qkv_ref, g_ref, gt_ref, tall_ref, wide_ref, el_ref, *, bsz):
    c = CHUNK
    ri = lax.broadcasted_iota(jnp.int32, (c, c), 0)
    ci = lax.broadcasted_iota(jnp.int32, (c, c), 1)
    tri_l = jnp.where(ri >= ci, 1.0, 0.0)
    tri_u = jnp.where(ri <= ci, 1.0, 0.0)
    incl = (ri >= ci, ri <= ci)
    strict = (ri > ci, ri < ci)
    hdot = functools.partial(jnp.dot, precision=HIGHEST, preferred_element_type=F32)

    gs = [g_ref[b] for b in range(bsz)]
    gts = [gt_ref[b, 0] for b in range(bsz)]
    gcol = [(hdot(tri_l, g), hdot(tri_u, g)) for g in gs]
    grow = [(hdot(gt, tri_u), hdot(gt, tri_l)) for gt in gts]
    glast = [(gcol[b][0][c - 1:c], gcol[b][1][0:1]) for b in range(bsz)]
    el_ref[...] = jnp.zeros_like(el_ref)
    for d in range(2):
        for b in range(bsz):
            el_ref[d * bsz + b:d * bsz + b + 1, :] = jnp.exp(glast[b][d])

    bh = [(b, hh) for b in range(bsz) for hh in range(HEADS)]
    hs = lambda hh: slice(hh * HEAD_DIM, (hh + 1) * HEAD_DIM)
    ks = [qkv_ref[1, b, :, hs(hh)] for b, hh in bh]
    qs = [qkv_ref[0, b, :, hs(hh)] for b, hh in bh]
    kk = [lax.dot_general(k, k, _NT, preferred_element_type=F32) for k in ks]
    qk = [lax.dot_general(q, k, _NT, preferred_element_type=F32) for q, k in zip(qs, ks)]
    inst = [(d, b, hh, i) for d in range(2) for i, (b, hh) in enumerate(bh)]
    decs, betas, gcs = [], [], []
    for d, b, hh, _ in inst:
        lg = 2 * HEADS + d * HEADS + hh
        gcs.append(gcol[b][d][:, lg:lg + 1])
        betas.append(gs[b][:, d * HEADS + hh:d * HEADS + hh + 1])
        decs.append(jnp.exp(jnp.where(incl[d], gcs[-1] - grow[b][d][lg:lg + 1, :], -jnp.inf)))
    ls = [jnp.where(strict[d], kk[i] * beta * dec, 0.0) for (d, _, _, i), beta, dec in zip(inst, betas, decs)]
    ts = _unit_tri_inverse(ls)
    for (d, b, hh, i), t, dec, beta, gc in zip(inst, ts, decs, betas, gcs):
        lg = 2 * HEADS + d * HEADS + hh
        eg = jnp.exp(gc)
        k = ks[i].astype(F32)
        tall_ref[d, b, hh, 0:c, :] = t.astype(BF16)
        tall_ref[d, b, hh, c:2 * c, :] = (qk[i] * dec).astype(BF16)
        tall_ref[d, b, hh, 2 * c:2 * c + HEAD_DIM, :] = (k * jnp.exp(glast[b][d][:, lg:lg + 1] - gc)).T.astype(BF16)
        wide_ref[d, b, hh, 0:c, :] = (qs[i].astype(F32) * eg).astype(BF16)
        wide_ref[d, b, hh, c:2 * c, :] = (k * (beta * eg)).astype(BF16)
        wide_ref[d, b, hh, 2 * c:3 * c, :] = (qkv_ref[2, b, :, hs(hh)].astype(F32) * beta).astype(BF16)


def _prep_call(qkv, gates, gates_t):
    _, bsz, lt, dn = qkv.shape
    nc = lt // CHUNK
    tall = (2, bsz, HEADS, 2 * CHUNK + HEAD_DIM, CHUNK)
    wide = (2, bsz, HEADS, 3 * CHUNK, HEAD_DIM)
    return pl.pallas_call(
        functools.partial(_prep_kernel, bsz=bsz),
        grid=(nc,),
        in_specs=[pl.BlockSpec((3, bsz, CHUNK, dn), lambda i: (0, 0, i, 0)),
                  pl.BlockSpec((bsz, CHUNK, LANES), lambda i: (0, i, 0)),
                  pl.BlockSpec((bsz, 1, 4 * HEADS, CHUNK), lambda i: (0, i, 0, 0))],
        out_specs=[pl.BlockSpec((None,) + tall, lambda i: (i, 0, 0, 0, 0, 0)),
                   pl.BlockSpec((None,) + wide, lambda i: (i, 0, 0, 0, 0, 0)),
                   pl.BlockSpec((None, SUBLANES, LANES), lambda i: (i, 0, 0))],
        out_shape=[jax.ShapeDtypeStruct((nc,) + tall, BF16),
                   jax.ShapeDtypeStruct((nc,) + wide, BF16),
                   jax.ShapeDtypeStruct((nc, SUBLANES, LANES), F32)],
        compiler_params=_cparams(("arbitrary",)),
        name="prep",
    )(qkv, gates, gates_t)


def _scan_kernel(tf_ref, tb_ref, wf_ref, wb_ref, ef_ref, eb_ref, of_ref, ob_ref, s_ref, *, bsz):
    i = pl.program_id(0)

    @pl.when(i == 0)
    def _():
        s_ref[...] = jnp.zeros_like(s_ref)

    c = CHUNK
    dirs = ((tf_ref, wf_ref, ef_ref, of_ref), (tb_ref, wb_ref, eb_ref, ob_ref))
    inst = [(d, b, hh) for d in range(2) for b in range(bsz) for hh in range(HEADS)]
    s_old = [s_ref[d, b, hh] for d, b, hh in inst]
    a1 = [jnp.dot(dirs[d][1][b, hh, 0:2 * c, :], s.astype(BF16), preferred_element_type=F32)
          for (d, b, hh), s in zip(inst, s_old)]
    v_new = [jnp.dot(dirs[d][0][b, hh, 0:c, :], (dirs[d][1][b, hh, 2 * c:3 * c, :].astype(F32) - a[c:]).astype(BF16),
                     preferred_element_type=F32).astype(BF16) for (d, b, hh), a in zip(inst, a1)]
    for (d, b, hh), a, vn, s in zip(inst, a1, v_new, s_old):
        t_ref = dirs[d][0]
        lg = 2 * HEADS + d * HEADS + hh
        o = a[:c] + jnp.dot(t_ref[b, hh, c:2 * c, :], vn, preferred_element_type=F32)
        dirs[d][3][b, :, hh * HEAD_DIM:(hh + 1) * HEAD_DIM] = o.astype(BF16)
        el = dirs[d][2][d * bsz + b:d * bsz + b + 1, lg:lg + 1]
        s_ref[d, b, hh] = s * el + jnp.dot(t_ref[b, hh, 2 * c:2 * c + HEAD_DIM, :], vn, preferred_element_type=F32)


def _scan_call(tall, wide, el, n_lat, n_ctx):
    nc, _, bsz = tall.shape[:3]
    dn = HEADS * HEAD_DIM
    ncx = n_lat // CHUNK
    ncc = n_ctx // CHUNK
    assert nc == ncx + ncc
    cf = lambda i: jnp.where(i < ncc, ncx + i, i - ncc)
    cb = lambda i: nc - 1 - i

    dspec = lambda arr, fn, d: pl.BlockSpec((None, None) + arr.shape[2:], lambda i: (fn(i), d, 0, 0, 0, 0))
    espec = lambda fn: pl.BlockSpec((None, SUBLANES, LANES), lambda i: (fn(i), 0, 0))
    ospec = lambda fn: pl.BlockSpec((bsz, CHUNK, dn), lambda i: (0, fn(i), 0))
    return pl.pallas_call(
        functools.partial(_scan_kernel, bsz=bsz),
        grid=(nc,),
        in_specs=[dspec(tall, cf, 0), dspec(tall, cb, 1), dspec(wide, cf, 0), dspec(wide, cb, 1),
                  espec(cf), espec(cb)],
        out_specs=[ospec(cf), ospec(cb)],
        out_shape=[jax.ShapeDtypeStruct((bsz, nc * CHUNK, dn), BF16)] * 2,
        scratch_shapes=[pltpu.VMEM((2, bsz, HEADS, HEAD_DIM, HEAD_DIM), F32)],
        compiler_params=_cparams(("arbitrary",)),
        name="scan",
    )(tall, tall, wide, wide, el, el)


def _hidden_kernel(pe_ref, w1_ref, b1_ref, w2_ref, b2_ref, w3_ref, b3_ref, f_ref, o_ref):
    f = f_ref[...]
    dot = functools.partial(jnp.dot, precision=HIGHEST, preferred_element_type=F32)
    a = jnp.sin(f * (dot(pe_ref[...], w1_ref[...]) + b1_ref[...]))
    a = jnp.sin(f * (dot(a, w2_ref[...]) + b2_ref[...]))
    o_ref[...] = jnp.sin(f * (dot(a, w3_ref[...]) + b3_ref[...]))


def _hidden_call(pe, w1p, b1, w2, b2, w3, b3, freq):
    rows, pw = pe.shape
    fw = 2 * w2.shape[0]
    tr = 512
    full = lambda a: pl.BlockSpec(a.shape, lambda t: (0,) * a.ndim)
    two = lambda a: jnp.tile(a.reshape(1, -1), (1, 2))
    diag2 = lambda a: jnp.kron(jnp.eye(2, dtype=a.dtype), a)
    args = (diag2(w1p), two(b1), diag2(w2), two(b2), diag2(w3), two(b3), two(freq))
    return pl.pallas_call(
        _hidden_kernel,
        grid=(rows // tr,),
        in_specs=[pl.BlockSpec((tr, pw), lambda t: (t, 0))] + [full(a) for a in args],
        out_specs=pl.BlockSpec((tr, fw), lambda t: (t, 0)),
        out_shape=jax.ShapeDtypeStruct((rows, fw), F32),
        compiler_params=_cparams(("arbitrary",)),
        name="hidden",
    )(pe, *args)


def _ftime_kernel(a_ref, t_ref, w_ref, dl_ref, h_ref, ss_ref, *, n_lat, tr):
    t = pl.program_id(0)
    h = _bdot(a_ref[...], w_ref[...]) * jnp.exp(-t_ref[...] * dl_ref[...])
    row = lax.broadcasted_iota(jnp.int32, (tr, 1), 0) + t * tr
    h = jnp.where(row == n_lat, 0.0, h)
    h_ref[...] = h.astype(h_ref.dtype)

    @pl.when(t == 0)
    def _():
        ss_ref[...] = jnp.zeros_like(ss_ref)

    ss_ref[...] += jnp.sum(h * h, axis=0, keepdims=True)


def _ftime_call(a3, tcol, wout2, deltas2, n_lat):
    rows = tcol.shape[0]
    fw = a3.shape[1]
    ncol = wout2.shape[2]
    tr = 512
    side = lambda t: (t * tr) // n_lat
    return pl.pallas_call(
        functools.partial(_ftime_kernel, n_lat=n_lat, tr=tr),
        grid=(rows // tr,),
        in_specs=[pl.BlockSpec((tr, fw), lambda t: (t % (n_lat // tr), 0)),
                  pl.BlockSpec((tr, 1), lambda t: (t, 0)),
                  pl.BlockSpec((None, fw, ncol), lambda t: (side(t), 0, 0)),
                  pl.BlockSpec((1, ncol), lambda t: (0, 0))],
        out_specs=[pl.BlockSpec((tr, ncol), lambda t: (t, 0)),
                   pl.BlockSpec((1, ncol), lambda t: (0, 0))],
        out_shape=[jax.ShapeDtypeStruct((rows, ncol), BF16), jax.ShapeDtypeStruct((1, ncol), F32)],
        compiler_params=_cparams(("arbitrary",)),
        name="ftime",
    )(a3, tcol, wout2, deltas2)


@functools.lru_cache(maxsize=None)
def _fft_tables(n_lat):
    n = 2 * n_lat
    n1t = n // FFT_INNER
    h1 = n1t // 2
    n2 = np.arange(FFT_INNER, dtype=np.int64)[:, None, None]
    k1 = np.arange(n1t, dtype=np.int64)[None, :, None]
    n1 = np.arange(h1, dtype=np.int64)[None, None, :]
    ang = (-2.0 * np.pi / n) * ((k1 * (FFT_INNER * n1 + n2)) % n)
    er, ei = np.cos(ang), np.sin(ang)
    e1 = np.concatenate([np.concatenate([er, -ei], axis=2), np.concatenate([ei, er], axis=2)], axis=1)
    e4 = np.transpose(e1, (0, 2, 1))
    sgn = np.where(np.arange(n1t) % 2 == 0, 1.0, -1.0)[None, :, None]
    ef = np.concatenate([np.concatenate([er, sgn * er], axis=2), np.concatenate([ei, sgn * ei], axis=2)], axis=1)
    a = np.arange(FFT_INNER, dtype=np.int64)
    ang2 = (-2.0 * np.pi / FFT_INNER) * ((a[:, None] * a[None, :]) % FFT_INNER)
    gr, gi = np.cos(ang2), np.sin(ang2)
    g2 = np.block([[gr, -gi], [gi, gr]])
    f32 = lambda x: np.ascontiguousarray(x, dtype=np.float32)
    return f32(e1), f32(e4), f32(ef), f32(g2), f32(g2.T)


def _hold(p, j, first, last, nsteps):
    active = (p >= first) & (p <= last) & (lax.rem(p - first, 2) == 0)
    return jnp.where(active, j, jnp.where(p < first, 0, nsteps - 1))


def _slab_read(x_ref, rows):
    return jnp.concatenate([x_ref[c, rows, :] for c in range(x_ref.shape[0])], axis=1)


def _slab_write(x_ref, rows, val):
    for c in range(x_ref.shape[0]):
        x_ref[c, rows, :] = val[:, c * LANES:(c + 1) * LANES]


def _spectrum_steps(x_ref, g2_ref, k1s, n1t, stride):
    cols = [jnp.concatenate([_slab_read(x_ref, pl.ds(k1, FFT_INNER, stride=stride)),
                             _slab_read(x_ref, pl.ds(n1t + k1, FFT_INNER, stride=stride))], axis=0).astype(BF16)
            for k1 in k1s]
    return [jnp.dot(g2_ref[...], c, preferred_element_type=F32) for c in cols]


def _ffft_kernel(h_ref, ef_ref, g2_ref, ss_ref, o_ref, x_ref, *, n1t, grp, kgrp, stride, inv_n):
    p = pl.program_id(1)
    j = pl.program_id(2)

    @pl.when(p == 0)
    def _():
        for j0 in range(0, grp, FFT_INTERLEAVE):
            jjs = list(range(j0, min(j0 + FFT_INTERLEAVE, grp)))
            us = [jnp.concatenate([h_ref[0, jj], h_ref[1, jj]], axis=0) for jj in jjs]
            slabs = [_bdot(ef_ref[jj], u) for jj, u in zip(jjs, us)]
            for jj, slab in zip(jjs, slabs):
                base = pl.multiple_of((j * grp + jj) * stride, SUBLANES)
                _slab_write(x_ref, pl.ds(base, 2 * n1t), slab)

    @pl.when(p == 1)
    def _():
        scale = lax.rsqrt(ss_ref[...] + EPS) * inv_n
        for j0 in range(0, kgrp, FFT_INTERLEAVE):
            jjs = list(range(j0, min(j0 + FFT_INTERLEAVE, kgrp)))
            xfs = _spectrum_steps(x_ref, g2_ref, [j * kgrp + jj for jj in jjs], n1t, stride)
            for jj, xf in zip(jjs, xfs):
                o_ref[jj] = (xf * scale).astype(BF16)


def _fft_groups(n1t, ns):
    ns = min(ns, n1t)
    assert FFT_INNER % ns == 0 and n1t % ns == 0
    return ns, FFT_INNER // ns, n1t // ns


def _ffft_call(hfull, ssq, ef, g2, n_lat):
    ncol = hfull.shape[1]
    n1t = 2 * n_lat // FFT_INNER
    h1 = n1t // 2
    ct = FFT_CHANNELS
    ns, grp, kgrp = _fft_groups(n1t, FILTER_FFT_STEPS)
    stride = 2 * n1t + SLAB_PAD
    hv = hfull.reshape(2, FFT_INNER, h1, ncol)
    nct = ncol // ct
    cpo = nct // 2
    return pl.pallas_call(
        functools.partial(_ffft_kernel, n1t=n1t, grp=grp, kgrp=kgrp, stride=stride, inv_n=1.0 / (2 * n_lat)),
        grid=(nct, 2, ns),
        in_specs=[pl.BlockSpec((2, grp, h1, ct), lambda c, p, j: (0, _hold(p, j, 0, 0, ns), 0, c)),
                  pl.BlockSpec((grp, 2 * n1t, n1t), lambda c, p, j: (_hold(p, j, 0, 0, ns), 0, 0)),
                  pl.BlockSpec((2 * FFT_INNER, 2 * FFT_INNER), lambda c, p, j: (0, 0)),
                  pl.BlockSpec((1, ct), lambda c, p, j: (0, c))],
        out_specs=pl.BlockSpec((None, kgrp, 2 * FFT_INNER, ct),
                               lambda c, p, j: (c // cpo, _hold(p, j, 1, 1, ns), 0, c % cpo)),
        out_shape=jax.ShapeDtypeStruct((2, n1t, 2 * FFT_INNER, ncol // 2), BF16),
        scratch_shapes=[pltpu.VMEM((ct // LANES, FFT_INNER * stride, LANES), F32)],
        compiler_params=_cparams(("arbitrary", "arbitrary", "arbitrary")),
        name="ffft",
    )(hv, ef, g2, ssq)


def _hyena_kernel(v_ref, x1_ref, x2_ref, hf_ref, e1_ref, e4_ref, g2_ref, g3_ref, bias_ref,
                  o_ref, x_ref, z_ref, *, n1t, grp, kgrp, stride):
    p = pl.program_id(1)
    j = pl.program_id(2)
    h1 = n1t // 2

    def gather(ref, jj):
        return jnp.concatenate([ref[0, jj], ref[1, jj]], axis=0).astype(F32)

    def slab_base(jj):
        return pl.multiple_of((j * grp + jj) * stride, SUBLANES)

    def time_rows(jjs):
        slabs = [_slab_read(x_ref, pl.ds(slab_base(jj), 2 * n1t)).astype(BF16) for jj in jjs]
        return [jnp.dot(e4_ref[jj], s, preferred_element_type=F32) for jj, s in zip(jjs, slabs)]

    def write_slabs(jjs, us):
        slabs = [jnp.dot(e1_ref[jj], u, preferred_element_type=F32) for jj, u in zip(jjs, us)]
        for jj, slab in zip(jjs, slabs):
            _slab_write(x_ref, pl.ds(slab_base(jj), 2 * n1t), slab)

    def chunks(n):
        return [list(range(j0, min(j0 + FFT_INTERLEAVE, n))) for j0 in range(0, n, FFT_INTERLEAVE)]

    @pl.when(p == 0)
    def _():
        for jjs in chunks(grp):
            write_slabs(jjs, [gather(v_ref, jj).astype(BF16) for jj in jjs])

    @pl.when((p == 1) | (p == 3))
    def _():
        for jjs in chunks(kgrp):
            k1s = [j * kgrp + jj for jj in jjs]
            xfs = _spectrum_steps(x_ref, g2_ref, k1s, n1t, stride)
            ys = []
            for jj, xf in zip(jjs, xfs):
                hf = hf_ref[jj].astype(F32)
                xr, xi = xf[:FFT_INNER], xf[FFT_INNER:]
                hr, hi = hf[:FFT_INNER], hf[FFT_INNER:]
                ys.append(jnp.concatenate([xr * hr - xi * hi, xr * hi + xi * hr], axis=0).astype(BF16))
            bbs = [jnp.dot(g3_ref[...], y, preferred_element_type=F32) for y in ys]
            for k1, bb in zip(k1s, bbs):
                _slab_write(x_ref, pl.ds(k1, FFT_INNER, stride=stride), bb[:FFT_INNER])
                _slab_write(x_ref, pl.ds(n1t + k1, FFT_INNER, stride=stride), bb[FFT_INNER:])

    @pl.when(p == 2)
    def _():
        for jjs in chunks(grp):
            z1s = [(gather(x1_ref, jj) * (y + gather(v_ref, jj) * bias_ref[0:1, :])).astype(BF16)
                   for jj, y in zip(jjs, time_rows(jjs))]
            for jj, z1 in zip(jjs, z1s):
                z_ref[j * grp + jj] = z1
            write_slabs(jjs, z1s)

    @pl.when(p == 4)
    def _():
        for jjs in chunks(grp):
            for jj, y in zip(jjs, time_rows(jjs)):
                out = gather(x2_ref, jj) * (y + z_ref[j * grp + jj].astype(F32) * bias_ref[1:2, :])
                o_ref[0, jj] = out[:h1].astype(o_ref.dtype)
                o_ref[1, jj] = out[h1:].astype(o_ref.dtype)


def _hyena_call(hyp, hf, e1, e4, g2, g3, bias, n_lat):
    _, bsz, _, ch = hyp.shape
    assert bsz == 2, "the two batch rows are packed as the real and imaginary parts of one FFT"
    n1t = 2 * n_lat // FFT_INNER
    h1 = n1t // 2
    ct = FFT_CHANNELS
    ns, grp, kgrp = _fft_groups(n1t, CONV_FFT_STEPS)
    stride = 2 * n1t + SLAB_PAD
    hv = jnp.transpose(hyp.reshape(3, bsz, h1, FFT_INNER, ch), (0, 1, 3, 2, 4))

    def hspec(which, first, last):
        return pl.BlockSpec((None, bsz, grp, h1, ct),
                            lambda c, p, j: (which, 0, _hold(p, j, first, last, ns), 0, c))

    def hf_map(c, p, j):
        order = jnp.where(p >= 2, 1, 0)
        step = jnp.where((p == 1) | (p == 3), j, jnp.where((p == 0) | (p == 2), 0, ns - 1))
        return (order, step, 0, c)

    out = pl.pallas_call(
        functools.partial(_hyena_kernel, n1t=n1t, grp=grp, kgrp=kgrp, stride=stride),
        grid=(ch // ct, 5, ns),
        in_specs=[hspec(0, 0, 2), hspec(1, 2, 2), hspec(2, 4, 4),
                  pl.BlockSpec((None, kgrp, 2 * FFT_INNER, ct), hf_map),
                  pl.BlockSpec((grp, 2 * n1t, n1t), lambda c, p, j: (_hold(p, j, 0, 2, ns), 0, 0)),
                  pl.BlockSpec((grp, n1t, 2 * n1t), lambda c, p, j: (_hold(p, j, 2, 4, ns), 0, 0)),
                  pl.BlockSpec((2 * FFT_INNER, 2 * FFT_INNER), lambda c, p, j: (0, 0)),
                  pl.BlockSpec((2 * FFT_INNER, 2 * FFT_INNER), lambda c, p, j: (0, 0)),
                  pl.BlockSpec((2, ct), lambda c, p, j: (0, c))],
        out_specs=pl.BlockSpec((bsz, grp, h1, ct), lambda c, p, j: (0, _hold(p, j, 4, 4, ns), 0, c)),
        out_shape=jax.ShapeDtypeStruct((bsz, FFT_INNER, h1, ch), BF16),
        scratch_shapes=[pltpu.VMEM((ct // LANES, FFT_INNER * stride, LANES), F32),
                        pltpu.VMEM((FFT_INNER, n1t, ct), BF16)],
        compiler_params=_cparams(("arbitrary", "arbitrary", "arbitrary"), VMEM_LIMIT_CONV),
        name="hyena",
    )(hv, hv, hv, hf, e1, e4, g2, g3, bias)
    return jnp.transpose(out, (0, 2, 1, 3)).reshape(bsz, n_lat, ch)


def _merge_kernel(of_ref, ob_ref, gp_ref, hy_ref, x_ref, mod_ref, dng_ref, wpa_ref, wpb_ref, wo_ref,
                  fg_ref, o_ref, oa_ref, *, d):
    b = pl.program_id(0)
    o = of_ref[0].astype(F32) + ob_ref[0].astype(F32)
    za = gp_ref[0, 0].astype(F32)
    for hh in range(HEADS):
        hs = slice(hh * HEAD_DIM, (hh + 1) * HEAD_DIM)
        oh = o[:, hs]
        ms = jnp.mean(oh * oh, axis=-1, keepdims=True)
        oa_ref[:, hs] = (oh * lax.rsqrt(ms + EPS) * dng_ref[...] * za[:, hs]).astype(BF16)
    y_a = jnp.dot(oa_ref[...], wpa_ref[...], preferred_element_type=F32)
    o_b = hy_ref[0].astype(F32) * gp_ref[1, 0].astype(F32)
    y_b = _bdot(o_b, wpb_ref[...])
    m = gp_ref[2, 0].astype(F32) * y_a + gp_ref[3, 0].astype(F32) * y_b
    y = _bdot(m, wo_ref[...])
    gate = mod_ref[pl.ds(b, 1), :][:, 2 * d:3 * d]
    xn = x_ref[0] + gate * y
    o_ref[0] = xn * lax.rsqrt(jnp.mean(xn * xn, axis=-1, keepdims=True) + EPS) * fg_ref[...]


def _merge_call(o_f, o_b, gp, hy, x, mod, dn_norm_g, w_pa, w_pb, w_out, final_g):
    bsz, n_lat, d = x.shape
    tm = 512 if n_lat % 512 == 0 else 256
    wfull = lambda a: pl.BlockSpec(a.shape, lambda b, t: (0,) * a.ndim)
    tok = pl.BlockSpec((1, tm, d), lambda b, t: (b, t, 0))
    return pl.pallas_call(
        functools.partial(_merge_kernel, d=d),
        grid=(bsz, n_lat // tm),
        in_specs=[tok, tok,
                  pl.BlockSpec((4, 1, tm, d), lambda b, t: (0, b, t, 0)),
                  tok, tok, wfull(mod),
                  pl.BlockSpec((1, HEAD_DIM), lambda b, t: (0, 0)),
                  wfull(w_pa), wfull(w_pb), wfull(w_out),
                  pl.BlockSpec((1, d), lambda b, t: (0, 0))],
        out_specs=tok,
        out_shape=jax.ShapeDtypeStruct((bsz, n_lat, d), F32),
        scratch_shapes=[pltpu.VMEM((tm, d), BF16)],
        compiler_params=_cparams(("arbitrary", "arbitrary")),
        name="merge",
    )(o_f, o_b, gp, hy, x, mod, dn_norm_g.reshape(1, HEAD_DIM), w_pa, w_pb, w_out, final_g.reshape(1, d))


def _position_features(n_lat):
    h1 = n_lat // FFT_INNER
    m = (np.arange(FFT_INNER)[:, None] + FFT_INNER * np.arange(h1)[None, :]).reshape(-1)
    lag = np.concatenate([m, n_lat - m]).astype(np.float64)
    lag[n_lat] = 0.0
    bands = (HY_EMB - 1) // 2
    t = (lag / (n_lat - 1))[:, None]
    wpos = (2.0 * math.pi / n_lat) * lag[:, None]
    fb = np.linspace(1e-4, bands - 1, bands)[None, :]
    pe = np.concatenate([t, np.cos(fb * wpos), -np.sin(fb * wpos)], axis=1)
    pe_pad = np.zeros((2 * n_lat, LANES), np.float32)
    pe_pad[:, :HY_EMB] = pe
    return pe_pad, np.ascontiguousarray(t, dtype=np.float32)


def kernel(x, c, ctx, c_ctx, w_mod, b_mod, norm_g, w_in, dn_conv_w, dn_a_log, dn_dt_bias, dn_norm_g,
           hy_conv_w, hy_conv_b, hy_f_w1, hy_f_b1, hy_f_w2, hy_f_b2, hy_f_w3, hy_f_b3, hy_f_wout,
           hy_f_freq, hy_bias, w_pa, w_pb, w_out, final_g):
    bsz, n_lat, d = x.shape
    n_ctx = ctx.shape[1]
    assert w_mod.shape[0] == 1, "single layer: the context stream is only read through its scan states"
    dn = HEADS * HEAD_DIM
    hy = hy_bias.shape[-1]
    assert n_lat % 256 == 0 and n_ctx % 256 == 0 and dn == d and hy == d

    cvec = jnp.zeros((SUBLANES, d), F32).at[:bsz].set(c).at[bsz].set(c_ctx)
    mod = _mod_call(cvec, w_mod[0], b_mod[0])

    tok = jnp.concatenate([x, ctx], axis=1)
    h = _hnorm_call(tok, mod, norm_g[0], n_lat, bsz)

    w = w_in[0].astype(BF16)
    o_qkv, o_gate, o_hy, o_gp = 0, 4 * dn, 4 * dn + 4 * HEADS, 4 * dn + 4 * HEADS + 3 * hy
    col3 = lambda a, off, n: jnp.transpose(a[:, off:off + 3 * n].reshape(a.shape[0], 3, n), (1, 0, 2))

    lt = n_lat + n_ctx
    tm = PROJ_TILE if n_lat % PROJ_TILE == 0 else 256
    w_qkv, cw_qkv, no_bias = col3(w, o_qkv, dn), col3(dn_conv_w[0], 0, dn), jnp.zeros((3, 1, dn), F32)
    qkv = _convproj_call(h, w_qkv, cw_qkv, no_bias, rowlen=GRID_W, tm=tm, tile0=0, ntiles=n_lat // tm,
                         out_rows=lt, qk_norm=True, name="qkv")
    qkv = _convproj_call(h, w_qkv, cw_qkv, no_bias, rowlen=n_ctx, tm=n_ctx, tile0=n_lat // n_ctx, ntiles=1,
                         out_rows=lt, qk_norm=True, prev=qkv, name="qkv_ctx")

    wg = jnp.zeros((d, LANES), BF16).at[:, :4 * HEADS].set(w[:, o_gate:o_gate + 4 * HEADS])
    lane_pad = lambda a: jnp.zeros((1, LANES), F32).at[0, 2 * HEADS:4 * HEADS].set(a.reshape(-1))
    gates = _gates_call(h, wg, lane_pad(dn_a_log[0]), lane_pad(dn_dt_bias[0]))
    gates_t = jnp.transpose(gates[:, :, :4 * HEADS].reshape(bsz, lt // CHUNK, CHUNK, 4 * HEADS), (0, 1, 3, 2))
    tall, wide, el = _prep_call(qkv, gates, gates_t)
    o_f, o_b = _scan_call(tall, wide, el, n_lat, n_ctx)

    hyp = _convproj_call(h, col3(w, o_hy, hy), col3(hy_conv_w[0], 0, hy), hy_conv_b[0].reshape(3, 1, hy),
                         rowlen=GRID_W, tm=tm, tile0=0, ntiles=n_lat // tm, out_rows=n_lat, qk_norm=False,
                         name="hyproj")
    w4 = jnp.stack([w[:, 3 * dn:4 * dn]] + [w[:, o_gp + i * d:o_gp + (i + 1) * d] for i in range(3)])
    gp = _gproj_call(h, w4, n_lat)

    pe, tcol = _position_features(n_lat)
    w1p = jnp.zeros((LANES, hy_f_w1.shape[-1]), F32).at[:HY_EMB].set(hy_f_w1[0])
    pe2 = jnp.asarray(np.concatenate([pe[:n_lat], pe[n_lat:]], axis=1))
    a3 = _hidden_call(pe2, w1p, hy_f_b1[0], hy_f_w2[0], hy_f_b2[0], hy_f_w3[0], hy_f_b3[0], hy_f_freq[0])
    deltas = np.abs(np.linspace(math.log(HY_DECAY_TARGET) / HY_SLOW_DECAY,
                                math.log(HY_DECAY_TARGET) / HY_FAST_DECAY, hy)).astype(np.float32)
    deltas2 = jnp.asarray(np.tile(deltas, 2)[None, :])
    wside = hy_f_wout[0].reshape(-1, 2, 2 * hy).transpose(1, 0, 2).astype(BF16)
    zside = jnp.zeros_like(wside[0])
    wout2 = jnp.stack([jnp.concatenate([wside[0], zside]), jnp.concatenate([zside, wside[1]])])
    hfull, ssq = _ftime_call(a3, jnp.asarray(tcol), wout2, deltas2, n_lat)
    e1, e4, ef, g2, g3 = (jnp.asarray(t).astype(BF16) for t in _fft_tables(n_lat))
    hf = _ffft_call(hfull, ssq, ef, g2, n_lat)

    yh = _hyena_call(hyp, hf, e1, e4, g2, g3, hy_bias[0], n_lat)

    return _merge_call(o_f, o_b, gp, yh, x, mod, dn_norm_g[0], w_pa[0].astype(BF16), w_pb[0].astype(BF16),
                       w_out[0].astype(BF16), final_g)
```

```python
import functools
import math

import numpy as np
import jax
import jax.numpy as jnp
from jax import lax
from jax.experimental import pallas as pl
from jax.experimental.pallas import tpu as pltpu

F32 = jnp.float32
BF16 = jnp.bfloat16
HIGHEST = lax.Precision.HIGHEST

EPS = 1e-6
HEADS = 8
HEAD_DIM = 128
CHUNK = 64
GRID_W = 64
HY_EMB = 33
HY_DECAY_TARGET = 1e-2
HY_FAST_DECAY = 0.3
HY_SLOW_DECAY = 1.5

LANES = 128
SUBLANES = 8
FFT_INNER = 128
CONV_FFT_STEPS = 8
FILTER_FFT_STEPS = 8
FFT_CHANNELS = 256
FFT_INTERLEAVE = 8
PROJ_TILE = 1024
NORM_TILE = 512
MXU_TILE = 256
SLAB_PAD = 8
VMEM_LIMIT = 56 * 1024 * 1024
VMEM_LIMIT_CONV = 62 * 1024 * 1024


def _cparams(sem, vmem=VMEM_LIMIT):
    return pltpu.CompilerParams(dimension_semantics=sem, vmem_limit_bytes=vmem)


def _bdot(a, b):
    return jnp.dot(a.astype(BF16), b.astype(BF16), preferred_element_type=F32)


def _silu(x):
    return x * jax.nn.sigmoid(x)


def _mod_kernel(c_ref, w_ref, b_ref, o_ref):
    s = _silu(c_ref[...])
    o_ref[...] = jnp.dot(s, w_ref[...], precision=HIGHEST, preferred_element_type=F32) + b_ref[...]


def _mod_call(cvec, w_mod, b_mod):
    rows, d = cvec.shape
    n = w_mod.shape[1]
    tn = 1024
    return pl.pallas_call(
        _mod_kernel,
        grid=(n // tn,),
        in_specs=[pl.BlockSpec((rows, d), lambda j: (0, 0)),
                  pl.BlockSpec((d, tn), lambda j: (0, j)),
                  pl.BlockSpec((1, tn), lambda j: (0, j))],
        out_specs=pl.BlockSpec((rows, tn), lambda j: (0, j)),
        out_shape=jax.ShapeDtypeStruct((rows, n), F32),
        compiler_params=_cparams(("arbitrary",)),
        name="mod",
    )(cvec, w_mod, b_mod.reshape(1, n))


def _hnorm_kernel(tok_ref, mod_ref, g_ref, w_ref, alog_ref, dtb_ref, *rest, d, mod_row):
    o_ref, og_ref = rest[-2:]
    row = pl.program_id(0) if mod_row is None else mod_row
    m = mod_ref[pl.ds(row, 1), :]
    x = tok_ref[0]
    y = x * lax.rsqrt(jnp.mean(x * x, axis=-1, keepdims=True) + EPS) * g_ref[...]
    h = (y * (1.0 + m[:, d:2 * d]) + m[:, 0:d]).astype(BF16)
    o_ref[0] = h
    z = jnp.dot(h, w_ref[...], preferred_element_type=F32)
    lane = lax.broadcasted_iota(jnp.int32, z.shape, 1)
    u = z + dtb_ref[...]
    softplus = jnp.maximum(u, 0.0) + jnp.log(1.0 + jnp.exp(-jnp.abs(u)))
    og_ref[0] = jnp.where(lane < 2 * HEADS, jax.nn.sigmoid(z), -jnp.exp(alog_ref[...]) * softplus)


def _hnorm_call(tok, mod, norm_g, wg, alog, dtb, *, tm, tile0, out_rows, mod_row=None, prev=None, name):
    bsz, rows, d = tok.shape
    full = lambda a: pl.BlockSpec(a.shape, lambda b, t: (0,) * a.ndim)
    args = [tok, mod, norm_g.reshape(1, d), wg, alog, dtb]
    in_specs = [pl.BlockSpec((1, tm, d), lambda b, t: (b, t, 0))] + [full(a) for a in args[1:]]
    aliases = {}
    if prev is not None:
        in_specs += [pl.BlockSpec(memory_space=pl.ANY)] * 2
        args += list(prev)
        aliases = {6: 0, 7: 1}
    return pl.pallas_call(
        functools.partial(_hnorm_kernel, d=d, mod_row=mod_row),
        grid=(bsz, rows // tm),
        in_specs=in_specs,
        out_specs=[pl.BlockSpec((1, tm, d), lambda b, t: (b, t + tile0, 0)),
                   pl.BlockSpec((1, tm, LANES), lambda b, t: (b, t + tile0, 0))],
        out_shape=[jax.ShapeDtypeStruct((bsz, out_rows, d), BF16),
                   jax.ShapeDtypeStruct((bsz, out_rows, LANES), F32)],
        input_output_aliases=aliases,
        compiler_params=_cparams(("arbitrary", "arbitrary")),
        name=name,
    )(*args)


@functools.lru_cache(maxsize=None)
def _shift_matrix(rowlen, taps, blk):
    t = np.arange(blk)
    mats = []
    for j in range(taps):
        d = j - taps // 2
        if d != 0:
            ok = ((t + d) >= 0) & ((t + d) < blk) & (t // rowlen == (t + d) // rowlen)
            s = np.zeros((blk, blk), np.float32)
            s[t[ok], t[ok] + d] = 1.0
            mats.append(s)
    return np.concatenate(mats, axis=1)


def _shift_block(rowlen, taps):
    return max(rowlen, MXU_TILE // (taps - 1) // rowlen * rowlen)


def _convproj_kernel(h_ref, w_ref, cw_ref, cb_ref, sh_ref, *rest, tm, qk_norm):
    o_ref = rest[-1]
    j = pl.program_id(0)
    taps = cw_ref.shape[1]
    mid = taps // 2
    blk = sh_ref.shape[0]
    mblk = max(blk, MXU_TILE)
    if qk_norm:
        scale = jnp.where(j == 0, HEAD_DIM ** -0.5, 1.0)
        is_qk = j < 2
    for r in range(tm // mblk):
        zm = jnp.dot(h_ref[0, r * mblk:(r + 1) * mblk, :], w_ref[0], preferred_element_type=F32)
        for g in range(mblk // blk):
            rows = slice(r * mblk + g * blk, r * mblk + (g + 1) * blk)
            zb = zm[g * blk:(g + 1) * blk]
            side = jnp.concatenate(
                [(zb * cw_ref[0, jt:jt + 1, :]).astype(BF16) for jt in range(taps) if jt != mid], axis=0)
            y = zb * cw_ref[0, mid:mid + 1, :] + jnp.dot(sh_ref[...], side, preferred_element_type=F32)
            if not qk_norm:
                o_ref[0, 0, rows, :] = (y + cb_ref[0]).astype(o_ref.dtype)
                continue
            y = _silu(y)
            for hh in range(HEADS):
                yh = y[:, hh * HEAD_DIM:(hh + 1) * HEAD_DIM]
                nrm = jnp.sum(yh * yh, axis=-1, keepdims=True)
                f = jnp.where(is_qk, lax.rsqrt(nrm + EPS) * scale, 1.0)
                o_ref[0, 0, rows, hh * HEAD_DIM:(hh + 1) * HEAD_DIM] = (yh * f).astype(o_ref.dtype)


def _convproj_call(h, w3, cw3, cb3, *, rowlen, tm, tile0, ntiles, out_rows, qk_norm, prev=None, name):
    bsz, _, d = h.shape
    nw = w3.shape[2]
    taps = cw3.shape[1]
    blk = _shift_block(rowlen, taps)
    assert tm % max(blk, MXU_TILE) == 0 and max(blk, MXU_TILE) % blk == 0
    sh = jnp.asarray(_shift_matrix(rowlen, taps, blk)).astype(BF16)
    in_specs = [pl.BlockSpec((1, tm, d), lambda j, b, t: (b, t + tile0, 0)),
                pl.BlockSpec((1, d, nw), lambda j, b, t: (j, 0, 0)),
                pl.BlockSpec((1, cw3.shape[1], nw), lambda j, b, t: (j, 0, 0)),
                pl.BlockSpec((1, 1, nw), lambda j, b, t: (j, 0, 0)),
                pl.BlockSpec(sh.shape, lambda j, b, t: (0, 0))]
    args = [h, w3, cw3, cb3, sh]
    aliases = {}
    if prev is not None:
        in_specs.append(pl.BlockSpec(memory_space=pl.ANY))
        args.append(prev)
        aliases = {5: 0}
    return pl.pallas_call(
        functools.partial(_convproj_kernel, tm=tm, qk_norm=qk_norm),
        grid=(3, bsz, ntiles),
        in_specs=in_specs,
        out_specs=pl.BlockSpec((1, 1, tm, nw), lambda j, b, t: (j, b, t + tile0, 0)),
        out_shape=jax.ShapeDtypeStruct((3, bsz, out_rows, nw), BF16),
        input_output_aliases=aliases,
        compiler_params=_cparams(("arbitrary", "arbitrary", "arbitrary")),
        name=name,
    )(*args)


def _gproj_kernel(h_ref, w_ref, o_ref):
    j = pl.program_id(0)
    for r in range(h_ref.shape[1] // MXU_TILE):
        rows = slice(r * MXU_TILE, (r + 1) * MXU_TILE)
        z = jnp.dot(h_ref[0, rows, :], w_ref[0], preferred_element_type=F32)
        s = jax.nn.sigmoid(z)
        o_ref[0, 0, rows, :] = jnp.where(j < 2, z * s, s).astype(BF16)


def _gproj_call(h, w4, n_lat):
    bsz, _, d = h.shape
    nw = w4.shape[2]
    tm = PROJ_TILE if n_lat % PROJ_TILE == 0 else 256
    return pl.pallas_call(
        _gproj_kernel,
        grid=(4, bsz, n_lat // tm),
        in_specs=[pl.BlockSpec((1, tm, d), lambda j, b, t: (b, t, 0)),
                  pl.BlockSpec((1, d, nw), lambda j, b, t: (j, 0, 0))],
        out_specs=pl.BlockSpec((1, 1, tm, nw), lambda j, b, t: (j, b, t, 0)),
        out_shape=jax.ShapeDtypeStruct((4, bsz, n_lat, nw), BF16),
        compiler_params=_cparams(("arbitrary", "arbitrary", "arbitrary")),
        name="gproj",
    )(h, w4)


def _unit_tri_inverse(ls, lowers, filler):
    c = ls[0].shape[0]
    ri = lax.broadcasted_iota(jnp.int32, (c, c), 0)
    ci = lax.broadcasted_iota(jnp.int32, (c, c), 1)
    same = lambda n: (ri // n) == (ci // n)
    eye = jnp.where(ri == ci, 1.0, 0.0)
    in2 = same(2)
    ts = [eye - jnp.where(in2, l, 0.0) for l in ls]
    n = 2
    nlevels = c.bit_length() - 2
    level = 0
    while n < c:
        blk = same(2 * n) & jnp.logical_not(same(n))
        offs = [jnp.where(blk, l, 0.0).astype(BF16) for l in ls]
        if n % SUBLANES:
            prods = [_bdot(t, off) for t, off in zip(ts, offs)]
            filler(level, nlevels)
            ts = [t - _bdot(p, t) for p, t in zip(prods, ts)]
        else:
            def moving(t, lower):
                first = n if lower else 0
                return jnp.concatenate([t[r:r + n] for r in range(first, c, 2 * n)], axis=0)

            def merged(t, upd, lower):
                pieces, k = [], 0
                for r in range(0, c, n):
                    if ((r // n) % 2 == 1) == lower:
                        pieces.append(upd[k * n:(k + 1) * n])
                        k += 1
                    else:
                        pieces.append(t[r:r + n])
                return jnp.concatenate(pieces, axis=0)

            rows = [moving(t, lo) for t, lo in zip(ts, lowers)]
            prods = [_bdot(r, off) for r, off in zip(rows, offs)]
            filler(level, nlevels)
            upds = [r - _bdot(p, t) for r, p, t in zip(rows, prods, ts)]
            ts = [merged(t, u, lo) for t, u, lo in zip(ts, upds, lowers)]
        n *= 2
        level += 1
    return ts


_NT = (((1,), (1,)), ((), ()))
_TN = (((0,), (0,)), ((), ()))


def _prep_kernel(qkv_ref, g_ref, gt_ref, tall_ref, wide_ref, el_ref, *, bsz):
    c = CHUNK
    ri = lax.broadcasted_iota(jnp.int32, (c, c), 0)
    ci = lax.broadcasted_iota(jnp.int32, (c, c), 1)
    tri_l = jnp.where(ri >= ci, 1.0, 0.0)
    tri_u = jnp.where(ri <= ci, 1.0, 0.0)
    incl = (ri >= ci, ri <= ci)
    strict = (ri > ci, ri < ci)
    hdot = functools.partial(jnp.dot, precision=HIGHEST, preferred_element_type=F32)

    gs = [g_ref[b] for b in range(bsz)]
    gts = [gt_ref[b, 0] for b in range(bsz)]
    gcol = [(hdot(tri_l, g), hdot(tri_u, g)) for g in gs]
    grow = [(hdot(gt, tri_u), hdot(gt, tri_l)) for gt in gts]
    glast = [(gcol[b][0][c - 1:c], gcol[b][1][0:1]) for b in range(bsz)]
    el_ref[...] = jnp.zeros_like(el_ref)
    for d in range(2):
        for b in range(bsz):
            el_ref[d * bsz + b:d * bsz + b + 1, :] = jnp.exp(glast[b][d])

    bh = [(b, hh) for b in range(bsz) for hh in range(HEADS)]
    hs = lambda hh: slice(hh * HEAD_DIM, (hh + 1) * HEAD_DIM)
    ks = [qkv_ref[1, b, :, hs(hh)] for b, hh in bh]
    qs = [qkv_ref[0, b, :, hs(hh)] for b, hh in bh]
    kk = [lax.dot_general(k, k, _NT, preferred_element_type=F32) for k in ks]
    qk = [lax.dot_general(q, k, _NT, preferred_element_type=F32) for q, k in zip(qs, ks)]
    inst = [(d, b, hh, i) for d in range(2) for i, (b, hh) in enumerate(bh)]
    decs, betas, gcs = [], [], []
    for d, b, hh, _ in inst:
        lg = 2 * HEADS + d * HEADS + hh
        gcs.append(gcol[b][d][:, lg:lg + 1])
        betas.append(gs[b][:, d * HEADS + hh:d * HEADS + hh + 1])
        decs.append(jnp.exp(jnp.where(incl[d], gcs[-1] - grow[b][d][lg:lg + 1, :], -jnp.inf)))
    ls = [jnp.where(strict[d], kk[i] * beta * dec, 0.0) for (d, _, _, i), beta, dec in zip(inst, betas, decs)]
    def scaled_operands(part, nparts):
        for (d, b, hh, i), dec, beta, gc in list(zip(inst, decs, betas, gcs))[part::nparts]:
            lg = 2 * HEADS + d * HEADS + hh
            eg = jnp.exp(gc)
            k = ks[i].astype(F32)
            tall_ref[d, b, hh, c:2 * c, :] = (qk[i] * dec).astype(BF16)
            tall_ref[d, b, hh, 2 * c:2 * c + HEAD_DIM, :] = (
                k * jnp.exp(glast[b][d][:, lg:lg + 1] - gc)).T.astype(BF16)
            wide_ref[d, b, hh, 0:c, :] = (qs[i].astype(F32) * eg).astype(BF16)
            wide_ref[d, b, hh, c:2 * c, :] = (k * (beta * eg)).astype(BF16)
            wide_ref[d, b, hh, 2 * c:3 * c, :] = (qkv_ref[2, b, :, hs(hh)].astype(F32) * beta).astype(BF16)

    ts = _unit_tri_inverse(ls, [d == 0 for d, _, _, _ in inst], scaled_operands)
    for (d, b, hh, _), t in zip(inst, ts):
        tall_ref[d, b, hh, 0:c, :] = t.astype(BF16)


def _prep_call(qkv, gates, gates_t):
    _, bsz, lt, dn = qkv.shape
    nc = lt // CHUNK
    tall = (2, bsz, HEADS, 2 * CHUNK + HEAD_DIM, CHUNK)
    wide = (2, bsz, HEADS, 3 * CHUNK, HEAD_DIM)
    return pl.pallas_call(
        functools.partial(_prep_kernel, bsz=bsz),
        grid=(nc,),
        in_specs=[pl.BlockSpec((3, bsz, CHUNK, dn), lambda i: (0, 0, i, 0)),
                  pl.BlockSpec((bsz, CHUNK, LANES), lambda i: (0, i, 0)),
                  pl.BlockSpec((bsz, 1, 4 * HEADS, CHUNK), lambda i: (0, i, 0, 0))],
        out_specs=[pl.BlockSpec((None,) + tall, lambda i: (i, 0, 0, 0, 0, 0)),
                   pl.BlockSpec((None,) + wide, lambda i: (i, 0, 0, 0, 0, 0)),
                   pl.BlockSpec((None, SUBLANES, LANES), lambda i: (i, 0, 0))],
        out_shape=[jax.ShapeDtypeStruct((nc,) + tall, BF16),
                   jax.ShapeDtypeStruct((nc,) + wide, BF16),
                   jax.ShapeDtypeStruct((nc, SUBLANES, LANES), F32)],
        compiler_params=_cparams(("arbitrary",)),
        name="prep",
    )(qkv, gates, gates_t)


def _scan_kernel(tf_ref, tb_ref, wf_ref, wb_ref, ef_ref, eb_ref, of_ref, ob_ref, s_ref, *, bsz):
    i = pl.program_id(0)

    @pl.when(i == 0)
    def _():
        s_ref[...] = jnp.zeros_like(s_ref)

    c = CHUNK
    dirs = ((tf_ref, wf_ref, ef_ref, of_ref), (tb_ref, wb_ref, eb_ref, ob_ref))
    inst = [(d, b, hh) for d in range(2) for b in range(bsz) for hh in range(HEADS)]
    s_old = [s_ref[d, b, hh] for d, b, hh in inst]
    a1 = [jnp.dot(dirs[d][1][b, hh, 0:2 * c, :], s.astype(BF16), preferred_element_type=F32)
          for (d, b, hh), s in zip(inst, s_old)]
    v_new = [jnp.dot(dirs[d][0][b, hh, 0:c, :], (dirs[d][1][b, hh, 2 * c:3 * c, :].astype(F32) - a[c:]).astype(BF16),
                     preferred_element_type=F32).astype(BF16) for (d, b, hh), a in zip(inst, a1)]
    for (d, b, hh), a, vn, s in zip(inst, a1, v_new, s_old):
        t_ref = dirs[d][0]
        lg = 2 * HEADS + d * HEADS + hh
        o = a[:c] + jnp.dot(t_ref[b, hh, c:2 * c, :], vn, preferred_element_type=F32)
        dirs[d][3][b, :, hh * HEAD_DIM:(hh + 1) * HEAD_DIM] = o.astype(BF16)
        el = dirs[d][2][d * bsz + b:d * bsz + b + 1, lg:lg + 1]
        s_ref[d, b, hh] = s * el + jnp.dot(t_ref[b, hh, 2 * c:2 * c + HEAD_DIM, :], vn, preferred_element_type=F32)


def _scan_call(tall, wide, el, n_lat, n_ctx):
    nc, _, bsz = tall.shape[:3]
    dn = HEADS * HEAD_DIM
    ncx = n_lat // CHUNK
    ncc = n_ctx // CHUNK
    assert nc == ncx + ncc
    cf = lambda i: jnp.where(i < ncc, ncx + i, i - ncc)
    cb = lambda i: nc - 1 - i

    dspec = lambda arr, fn, d: pl.BlockSpec((None, None) + arr.shape[2:], lambda i: (fn(i), d, 0, 0, 0, 0))
    espec = lambda fn: pl.BlockSpec((None, SUBLANES, LANES), lambda i: (fn(i), 0, 0))
    ospec = lambda fn: pl.BlockSpec((bsz, CHUNK, dn), lambda i: (0, fn(i), 0))
    return pl.pallas_call(
        functools.partial(_scan_kernel, bsz=bsz),
        grid=(nc,),
        in_specs=[dspec(tall, cf, 0), dspec(tall, cb, 1), dspec(wide, cf, 0), dspec(wide, cb, 1),
                  espec(cf), espec(cb)],
        out_specs=[ospec(cf), ospec(cb)],
        out_shape=[jax.ShapeDtypeStruct((bsz, nc * CHUNK, dn), BF16)] * 2,
        scratch_shapes=[pltpu.VMEM((2, bsz, HEADS, HEAD_DIM, HEAD_DIM), F32)],
        compiler_params=_cparams(("arbitrary",)),
        name="scan",
    )(tall, tall, wide, wide, el, el)


def _hidden_kernel(pe_ref, w1_ref, b1_ref, w2_ref, b2_ref, w3_ref, b3_ref, f_ref, o_ref):
    f = f_ref[...]
    dot = functools.partial(jnp.dot, precision=HIGHEST, preferred_element_type=F32)
    a = jnp.sin(f * (dot(pe_ref[...], w1_ref[...]) + b1_ref[...]))
    a = jnp.sin(f * (dot(a, w2_ref[...]) + b2_ref[...]))
    o_ref[...] = jnp.sin(f * (dot(a, w3_ref[...]) + b3_ref[...]))


def _hidden_call(pe, w1p, b1, w2, b2, w3, b3, freq):
    rows, pw = pe.shape
    fw = 2 * w2.shape[0]
    tr = 512
    full = lambda a: pl.BlockSpec(a.shape, lambda t: (0,) * a.ndim)
    two = lambda a: jnp.tile(a.reshape(1, -1), (1, 2))
    diag2 = lambda a: jnp.kron(jnp.eye(2, dtype=a.dtype), a)
    args = (diag2(w1p), two(b1), diag2(w2), two(b2), diag2(w3), two(b3), two(freq))
    return pl.pallas_call(
        _hidden_kernel,
        grid=(rows // tr,),
        in_specs=[pl.BlockSpec((tr, pw), lambda t: (t, 0))] + [full(a) for a in args],
        out_specs=pl.BlockSpec((tr, fw), lambda t: (t, 0)),
        out_shape=jax.ShapeDtypeStruct((rows, fw), F32),
        compiler_params=_cparams(("arbitrary",)),
        name="hidden",
    )(pe, *args)


def _ftime_kernel(a_ref, t_ref, w_ref, dl_ref, h_ref, ss_ref, *, n_lat, tr):
    t = pl.program_id(0)
    h = _bdot(a_ref[...], w_ref[...]) * jnp.exp(-t_ref[...] * dl_ref[...])
    row = lax.broadcasted_iota(jnp.int32, (tr, 1), 0) + t * tr
    h = jnp.where(row == n_lat, 0.0, h)
    h_ref[...] = h.astype(h_ref.dtype)

    @pl.when(t == 0)
    def _():
        ss_ref[...] = jnp.zeros_like(ss_ref)

    ss_ref[...] += jnp.sum(h * h, axis=0, keepdims=True)


def _ftime_call(a3, tcol, wout2, deltas2, n_lat):
    rows = tcol.shape[0]
    fw = a3.shape[1]
    ncol = wout2.shape[2]
    tr = 512
    side = lambda t: (t * tr) // n_lat
    return pl.pallas_call(
        functools.partial(_ftime_kernel, n_lat=n_lat, tr=tr),
        grid=(rows // tr,),
        in_specs=[pl.BlockSpec((tr, fw), lambda t: (t % (n_lat // tr), 0)),
                  pl.BlockSpec((tr, 1), lambda t: (t, 0)),
                  pl.BlockSpec((None, fw, ncol), lambda t: (side(t), 0, 0)),
                  pl.BlockSpec((1, ncol), lambda t: (0, 0))],
        out_specs=[pl.BlockSpec((tr, ncol), lambda t: (t, 0)),
                   pl.BlockSpec((1, ncol), lambda t: (0, 0))],
        out_shape=[jax.ShapeDtypeStruct((rows, ncol), BF16), jax.ShapeDtypeStruct((1, ncol), F32)],
        compiler_params=_cparams(("arbitrary",)),
        name="ftime",
    )(a3, tcol, wout2, deltas2)


@functools.lru_cache(maxsize=None)
def _fft_tables(n_lat):
    n = 2 * n_lat
    n1t = n // FFT_INNER
    h1 = n1t // 2
    n2 = np.arange(FFT_INNER, dtype=np.int64)[:, None, None]
    k1 = np.arange(n1t, dtype=np.int64)[None, :, None]
    n1 = np.arange(h1, dtype=np.int64)[None, None, :]
    ang = (-2.0 * np.pi / n) * ((k1 * (FFT_INNER * n1 + n2)) % n)
    er, ei = np.cos(ang), np.sin(ang)
    e1 = np.concatenate([np.concatenate([er, -ei], axis=2), np.concatenate([ei, er], axis=2)], axis=1)
    e4 = np.transpose(e1, (0, 2, 1))
    sgn = np.where(np.arange(n1t) % 2 == 0, 1.0, -1.0)[None, :, None]
    ef = np.concatenate([np.concatenate([er, sgn * er], axis=2), np.concatenate([ei, sgn * ei], axis=2)], axis=1)
    a = np.arange(FFT_INNER, dtype=np.int64)
    ang2 = (-2.0 * np.pi / FFT_INNER) * ((a[:, None] * a[None, :]) % FFT_INNER)
    gr, gi = np.cos(ang2), np.sin(ang2)
    g2 = np.block([[gr, -gi], [gi, gr]])
    f32 = lambda x: np.ascontiguousarray(x, dtype=np.float32)
    return f32(e1), f32(e4), f32(ef), f32(g2), f32(g2.T)


def _hold(p, j, first, last, nsteps):
    active = (p >= first) & (p <= last) & (lax.rem(p - first, 2) == 0)
    return jnp.where(active, j, jnp.where(p < first, 0, nsteps - 1))


def _slab_read(x_ref, rows):
    return jnp.concatenate([x_ref[c, rows, :] for c in range(x_ref.shape[0])], axis=1)


def _slab_write(x_ref, rows, val):
    for c in range(x_ref.shape[0]):
        x_ref[c, rows, :] = val[:, c * LANES:(c + 1) * LANES]


def _spectrum_steps(x_ref, g2_ref, k1s, n1t, stride):
    cols = [jnp.concatenate([_slab_read(x_ref, pl.ds(k1, FFT_INNER, stride=stride)),
                             _slab_read(x_ref, pl.ds(n1t + k1, FFT_INNER, stride=stride))], axis=0).astype(BF16)
            for k1 in k1s]
    return [jnp.dot(g2_ref[...], c, preferred_element_type=F32) for c in cols]


def _ffft_kernel(h_ref, ef_ref, g2_ref, ss_ref, o_ref, x_ref, *, n1t, grp, kgrp, stride, inv_n):
    p = pl.program_id(1)
    j = pl.program_id(2)

    @pl.when(p == 0)
    def _():
        for j0 in range(0, grp, FFT_INTERLEAVE):
            jjs = list(range(j0, min(j0 + FFT_INTERLEAVE, grp)))
            us = [jnp.concatenate([h_ref[0, jj], h_ref[1, jj]], axis=0) for jj in jjs]
            slabs = [_bdot(ef_ref[jj], u) for jj, u in zip(jjs, us)]
            for jj, slab in zip(jjs, slabs):
                base = pl.multiple_of((j * grp + jj) * stride, SUBLANES)
                _slab_write(x_ref, pl.ds(base, 2 * n1t), slab)

    @pl.when(p == 1)
    def _():
        scale = lax.rsqrt(ss_ref[...] + EPS) * inv_n
        for j0 in range(0, kgrp, FFT_INTERLEAVE):
            jjs = list(range(j0, min(j0 + FFT_INTERLEAVE, kgrp)))
            xfs = _spectrum_steps(x_ref, g2_ref, [j * kgrp + jj for jj in jjs], n1t, stride)
            for jj, xf in zip(jjs, xfs):
                o_ref[jj] = (xf * scale).astype(BF16)


def _fft_groups(n1t, ns):
    ns = min(ns, n1t)
    assert FFT_INNER % ns == 0 and n1t % ns == 0
    return ns, FFT_INNER // ns, n1t // ns


def _ffft_call(hfull, ssq, ef, g2, n_lat):
    ncol = hfull.shape[1]
    n1t = 2 * n_lat // FFT_INNER
    h1 = n1t // 2
    ct = FFT_CHANNELS
    ns, grp, kgrp = _fft_groups(n1t, FILTER_FFT_STEPS)
    stride = 2 * n1t + SLAB_PAD
    hv = hfull.reshape(2, FFT_INNER, h1, ncol)
    nct = ncol // ct
    cpo = nct // 2
    return pl.pallas_call(
        functools.partial(_ffft_kernel, n1t=n1t, grp=grp, kgrp=kgrp, stride=stride, inv_n=1.0 / (2 * n_lat)),
        grid=(nct, 2, ns),
        in_specs=[pl.BlockSpec((2, grp, h1, ct), lambda c, p, j: (0, _hold(p, j, 0, 0, ns), 0, c)),
                  pl.BlockSpec((grp, 2 * n1t, n1t), lambda c, p, j: (_hold(p, j, 0, 0, ns), 0, 0)),
                  pl.BlockSpec((2 * FFT_INNER, 2 * FFT_INNER), lambda c, p, j: (0, 0)),
                  pl.BlockSpec((1, ct), lambda c, p, j: (0, c))],
        out_specs=pl.BlockSpec((None, kgrp, 2 * FFT_INNER, ct),
                               lambda c, p, j: (c // cpo, _hold(p, j, 1, 1, ns), 0, c % cpo)),
        out_shape=jax.ShapeDtypeStruct((2, n1t, 2 * FFT_INNER, ncol // 2), BF16),
        scratch_shapes=[pltpu.VMEM((ct // LANES, FFT_INNER * stride, LANES), F32)],
        compiler_params=_cparams(("arbitrary", "arbitrary", "arbitrary")),
        name="ffft",
    )(hv, ef, g2, ssq)


def _hyena_kernel(v_ref, x1_ref, x2_ref, hf_ref, e1_ref, e4_ref, g2_ref, g3_ref, bias_ref,
                  o_ref, x_ref, z_ref, *, n1t, grp, kgrp, stride):
    p = pl.program_id(1)
    j = pl.program_id(2)
    h1 = n1t // 2

    def gather(ref, jj):
        return jnp.concatenate([ref[0, jj], ref[1, jj]], axis=0).astype(F32)

    def slab_base(jj):
        return pl.multiple_of((j * grp + jj) * stride, SUBLANES)

    def time_rows(jjs):
        slabs = [_slab_read(x_ref, pl.ds(slab_base(jj), 2 * n1t)).astype(BF16) for jj in jjs]
        return [jnp.dot(e4_ref[jj], s, preferred_element_type=F32) for jj, s in zip(jjs, slabs)]

    def write_slabs(jjs, us):
        slabs = [jnp.dot(e1_ref[jj], u, preferred_element_type=F32) for jj, u in zip(jjs, us)]
        for jj, slab in zip(jjs, slabs):
            _slab_write(x_ref, pl.ds(slab_base(jj), 2 * n1t), slab)

    def chunks(n):
        return [list(range(j0, min(j0 + FFT_INTERLEAVE, n))) for j0 in range(0, n, FFT_INTERLEAVE)]

    @pl.when(p == 0)
    def _():
        for jjs in chunks(grp):
            write_slabs(jjs, [gather(v_ref, jj).astype(BF16) for jj in jjs])

    @pl.when((p == 1) | (p == 3))
    def _():
        for jjs in chunks(kgrp):
            k1s = [j * kgrp + jj for jj in jjs]
            xfs = _spectrum_steps(x_ref, g2_ref, k1s, n1t, stride)
            ys = []
            for jj, xf in zip(jjs, xfs):
                hf = hf_ref[jj].astype(F32)
                xr, xi = xf[:FFT_INNER], xf[FFT_INNER:]
                hr, hi = hf[:FFT_INNER], hf[FFT_INNER:]
                ys.append(jnp.concatenate([xr * hr - xi * hi, xr * hi + xi * hr], axis=0).astype(BF16))
            bbs = [jnp.dot(g3_ref[...], y, preferred_element_type=F32) for y in ys]
            for k1, bb in zip(k1s, bbs):
                _slab_write(x_ref, pl.ds(k1, FFT_INNER, stride=stride), bb[:FFT_INNER])
                _slab_write(x_ref, pl.ds(n1t + k1, FFT_INNER, stride=stride), bb[FFT_INNER:])

    @pl.when(p == 2)
    def _():
        for jjs in chunks(grp):
            z1s = [(gather(x1_ref, jj) * (y + gather(v_ref, jj) * bias_ref[0:1, :])).astype(BF16)
                   for jj, y in zip(jjs, time_rows(jjs))]
            for jj, z1 in zip(jjs, z1s):
                z_ref[j * grp + jj] = z1
            write_slabs(jjs, z1s)

    @pl.when(p == 4)
    def _():
        for jjs in chunks(grp):
            for jj, y in zip(jjs, time_rows(jjs)):
                out = gather(x2_ref, jj) * (y + z_ref[j * grp + jj].astype(F32) * bias_ref[1:2, :])
                o_ref[0, jj] = out[:h1].astype(o_ref.dtype)
                o_ref[1, jj] = out[h1:].astype(o_ref.dtype)


def _hyena_call(hyp, hf, e1, e4, g2, g3, bias, n_lat):
    _, bsz, _, ch = hyp.shape
    assert bsz == 2, "the two batch rows are packed as the real and imaginary parts of one FFT"
    n1t = 2 * n_lat // FFT_INNER
    h1 = n1t // 2
    ct = FFT_CHANNELS
    ns, grp, kgrp = _fft_groups(n1t, CONV_FFT_STEPS)
    stride = 2 * n1t + SLAB_PAD
    hv = jnp.transpose(hyp.reshape(3, bsz, h1, FFT_INNER, ch), (0, 1, 3, 2, 4))

    def hspec(which, first, last):
        return pl.BlockSpec((None, bsz, grp, h1, ct),
                            lambda c, p, j: (which, 0, _hold(p, j, first, last, ns), 0, c))

    def hf_map(c, p, j):
        order = jnp.where(p >= 2, 1, 0)
        step = jnp.where((p == 1) | (p == 3), j, jnp.where((p == 0) | (p == 2), 0, ns - 1))
        return (order, step, 0, c)

    out = pl.pallas_call(
        functools.partial(_hyena_kernel, n1t=n1t, grp=grp, kgrp=kgrp, stride=stride),
        grid=(ch // ct, 5, ns),
        in_specs=[hspec(0, 0, 2), hspec(1, 2, 2), hspec(2, 4, 4),
                  pl.BlockSpec((None, kgrp, 2 * FFT_INNER, ct), hf_map),
                  pl.BlockSpec((grp, 2 * n1t, n1t), lambda c, p, j: (_hold(p, j, 0, 2, ns), 0, 0)),
                  pl.BlockSpec((grp, n1t, 2 * n1t), lambda c, p, j: (_hold(p, j, 2, 4, ns), 0, 0)),
                  pl.BlockSpec((2 * FFT_INNER, 2 * FFT_INNER), lambda c, p, j: (0, 0)),
                  pl.BlockSpec((2 * FFT_INNER, 2 * FFT_INNER), lambda c, p, j: (0, 0)),
                  pl.BlockSpec((2, ct), lambda c, p, j: (0, c))],
        out_specs=pl.BlockSpec((bsz, grp, h1, ct), lambda c, p, j: (0, _hold(p, j, 4, 4, ns), 0, c)),
        out_shape=jax.ShapeDtypeStruct((bsz, FFT_INNER, h1, ch), BF16),
        scratch_shapes=[pltpu.VMEM((ct // LANES, FFT_INNER * stride, LANES), F32),
                        pltpu.VMEM((FFT_INNER, n1t, ct), BF16)],
        compiler_params=_cparams(("arbitrary", "arbitrary", "arbitrary"), VMEM_LIMIT_CONV),
        name="hyena",
    )(hv, hv, hv, hf, e1, e4, g2, g3, bias)
    return jnp.transpose(out, (0, 2, 1, 3)).reshape(bsz, n_lat, ch)


def _merge_kernel(of_ref, ob_ref, gp_ref, hy_ref, x_ref, mod_ref, dng_ref, wpa_ref, wpb_ref, wo_ref,
                  fg_ref, o_ref, *, d):
    b = pl.program_id(0)
    blocks = [slice(r * MXU_TILE, (r + 1) * MXU_TILE) for r in range(o_ref.shape[1] // MXU_TILE)]
    f32 = lambda ref, *idx: ref[idx].astype(F32)

    def head_norm(rows):
        o = f32(of_ref, 0, rows) + f32(ob_ref, 0, rows)
        za = f32(gp_ref, 0, 0, rows)
        parts = []
        for hh in range(HEADS):
            hs = slice(hh * HEAD_DIM, (hh + 1) * HEAD_DIM)
            oh = o[:, hs]
            ms = jnp.mean(oh * oh, axis=-1, keepdims=True)
            parts.append((oh * lax.rsqrt(ms + EPS) * dng_ref[...] * za[:, hs]).astype(BF16))
        return jnp.concatenate(parts, axis=1)

    o_a = [head_norm(rows) for rows in blocks]
    y_a = [jnp.dot(a, wpa_ref[...], preferred_element_type=F32) for a in o_a]
    o_b = [(f32(hy_ref, 0, rows) * f32(gp_ref, 1, 0, rows)).astype(BF16) for rows in blocks]
    y_b = [jnp.dot(a, wpb_ref[...], preferred_element_type=F32) for a in o_b]
    m = [(f32(gp_ref, 2, 0, rows) * ya + f32(gp_ref, 3, 0, rows) * yb).astype(BF16)
         for rows, ya, yb in zip(blocks, y_a, y_b)]
    y = [jnp.dot(a, wo_ref[...], preferred_element_type=F32) for a in m]
    gate = mod_ref[pl.ds(b, 1), :][:, 2 * d:3 * d]
    for rows, yr in zip(blocks, y):
        xn = x_ref[0, rows, :] + gate * yr
        o_ref[0, rows, :] = xn * lax.rsqrt(jnp.mean(xn * xn, axis=-1, keepdims=True) + EPS) * fg_ref[...]


def _merge_call(o_f, o_b, gp, hy, x, mod, dn_norm_g, w_pa, w_pb, w_out, final_g):
    bsz, n_lat, d = x.shape
    tm = NORM_TILE if n_lat % NORM_TILE == 0 else 256
    wfull = lambda a: pl.BlockSpec(a.shape, lambda b, t: (0,) * a.ndim)
    tok = pl.BlockSpec((1, tm, d), lambda b, t: (b, t, 0))
    return pl.pallas_call(
        functools.partial(_merge_kernel, d=d),
        grid=(bsz, n_lat // tm),
        in_specs=[tok, tok,
                  pl.BlockSpec((4, 1, tm, d), lambda b, t: (0, b, t, 0)),
                  tok, tok, wfull(mod),
                  pl.BlockSpec((1, HEAD_DIM), lambda b, t: (0, 0)),
                  wfull(w_pa), wfull(w_pb), wfull(w_out),
                  pl.BlockSpec((1, d), lambda b, t: (0, 0))],
        out_specs=tok,
        out_shape=jax.ShapeDtypeStruct((bsz, n_lat, d), F32),
        compiler_params=_cparams(("arbitrary", "arbitrary")),
        name="merge",
    )(o_f, o_b, gp, hy, x, mod, dn_norm_g.reshape(1, HEAD_DIM), w_pa, w_pb, w_out, final_g.reshape(1, d))


def _position_features(n_lat):
    h1 = n_lat // FFT_INNER
    m = (np.arange(FFT_INNER)[:, None] + FFT_INNER * np.arange(h1)[None, :]).reshape(-1)
    lag = np.concatenate([m, n_lat - m]).astype(np.float64)
    lag[n_lat] = 0.0
    bands = (HY_EMB - 1) // 2
    t = (lag / (n_lat - 1))[:, None]
    wpos = (2.0 * math.pi / n_lat) * lag[:, None]
    fb = np.linspace(1e-4, bands - 1, bands)[None, :]
    pe = np.concatenate([t, np.cos(fb * wpos), -np.sin(fb * wpos)], axis=1)
    pe_pad = np.zeros((2 * n_lat, LANES), np.float32)
    pe_pad[:, :HY_EMB] = pe
    return pe_pad, np.ascontiguousarray(t, dtype=np.float32)


def kernel(x, c, ctx, c_ctx, w_mod, b_mod, norm_g, w_in, dn_conv_w, dn_a_log, dn_dt_bias, dn_norm_g,
           hy_conv_w, hy_conv_b, hy_f_w1, hy_f_b1, hy_f_w2, hy_f_b2, hy_f_w3, hy_f_b3, hy_f_wout,
           hy_f_freq, hy_bias, w_pa, w_pb, w_out, final_g):
    bsz, n_lat, d = x.shape
    n_ctx = ctx.shape[1]
    assert w_mod.shape[0] == 1, "single layer: the context stream is only read through its scan states"
    dn = HEADS * HEAD_DIM
    hy = hy_bias.shape[-1]
    assert n_lat % 256 == 0 and n_ctx % 256 == 0 and dn == d and hy == d

    cvec = jnp.zeros((SUBLANES, d), F32).at[:bsz].set(c).at[bsz].set(c_ctx)
    mod = _mod_call(cvec, w_mod[0], b_mod[0])

    w = w_in[0].astype(BF16)
    o_qkv, o_gate, o_hy, o_gp = 0, 4 * dn, 4 * dn + 4 * HEADS, 4 * dn + 4 * HEADS + 3 * hy
    col3 = lambda a, off, n: jnp.transpose(a[:, off:off + 3 * n].reshape(a.shape[0], 3, n), (1, 0, 2))
    lt = n_lat + n_ctx
    tm = PROJ_TILE if n_lat % PROJ_TILE == 0 else 256

    wg = jnp.zeros((d, LANES), BF16).at[:, :4 * HEADS].set(w[:, o_gate:o_gate + 4 * HEADS])
    lane_pad = lambda a: jnp.zeros((1, LANES), F32).at[0, 2 * HEADS:4 * HEADS].set(a.reshape(-1))
    gate_args = (wg, lane_pad(dn_a_log[0]), lane_pad(dn_dt_bias[0]))
    hg = _hnorm_call(x, mod, norm_g[0], *gate_args, tm=NORM_TILE, tile0=0, out_rows=lt, name="hnorm")
    h, gates = _hnorm_call(ctx, mod, norm_g[0], *gate_args, tm=n_ctx, tile0=n_lat // n_ctx, out_rows=lt,
                           mod_row=bsz, prev=hg, name="hnorm_ctx")

    w_qkv, cw_qkv, no_bias = col3(w, o_qkv, dn), col3(dn_conv_w[0], 0, dn), jnp.zeros((3, 1, dn), F32)
    qkv = _convproj_call(h, w_qkv, cw_qkv, no_bias, rowlen=GRID_W, tm=tm, tile0=0, ntiles=n_lat // tm,
                         out_rows=lt, qk_norm=True, name="qkv")
    qkv = _convproj_call(h, w_qkv, cw_qkv, no_bias, rowlen=n_ctx, tm=n_ctx, tile0=n_lat // n_ctx, ntiles=1,
                         out_rows=lt, qk_norm=True, prev=qkv, name="qkv_ctx")

    gates_t =jnp.transpose(gates[:, :, :4 * HEADS].reshape(bsz, lt // CHUNK, CHUNK, 4 * HEADS), (0, 1, 3, 2))
    tall, wide, el = _prep_call(qkv, gates, gates_t)
    o_f, o_b = _scan_call(tall, wide, el, n_lat, n_ctx)

    hyp = _convproj_call(h, col3(w, o_hy, hy), col3(hy_conv_w[0], 0, hy), hy_conv_b[0].reshape(3, 1, hy),
                         rowlen=GRID_W, tm=tm, tile0=0, ntiles=n_lat // tm, out_rows=n_lat, qk_norm=False,
                         name="hyproj")
    w4 = jnp.stack([w[:, 3 * dn:4 * dn]] + [w[:, o_gp + i * d:o_gp + (i + 1) * d] for i in range(3)])
    gp = _gproj_call(h, w4, n_lat)

    pe, tcol = _position_features(n_lat)
    w1p = jnp.zeros((LANES, hy_f_w1.shape[-1]), F32).at[:HY_EMB].set(hy_f_w1[0])
    pe2 = jnp.asarray(np.concatenate([pe[:n_lat], pe[n_lat:]], axis=1))
    a3 = _hidden_call(pe2, w1p, hy_f_b1[0], hy_f_w2[0], hy_f_b2[0], hy_f_w3[0], hy_f_b3[0], hy_f_freq[0])
    deltas = np.abs(np.linspace(math.log(HY_DECAY_TARGET) / HY_SLOW_DECAY,
                                math.log(HY_DECAY_TARGET) / HY_FAST_DECAY, hy)).astype(np.float32)
    deltas2 = jnp.asarray(np.tile(deltas, 2)[None, :])
    wside = hy_f_wout[0].reshape(-1, 2, 2 * hy).transpose(1, 0, 2).astype(BF16)
    zside = jnp.zeros_like(wside[0])
    wout2 = jnp.stack([jnp.concatenate([wside[0], zside]), jnp.concatenate([zside, wside[1]])])
    hfull, ssq = _ftime_call(a3, jnp.asarray(tcol), wout2, deltas2, n_lat)
    e1, e4, ef, g2, g3 = (jnp.asarray(t).astype(BF16) for t in _fft_tables(n_lat))
    hf = _ffft_call(hfull, ssq, ef, g2, n_lat)

    yh = _hyena_call(hyp, hf, e1, e4, g2, g3, hy_bias[0], n_lat)

    return _merge_call(o_f, o_b, gp, yh, x, mod, dn_norm_g[0], w_pa[0].astype(BF16), w_pb[0].astype(BF16),
                       w_out[0].astype(BF16), final_g)
```

```python
import functools
import math

import numpy as np
import jax
import jax.numpy as jnp
from jax import lax
from jax.experimental import pallas as pl
from jax.experimental.pallas import tpu as pltpu

F32 = jnp.float32
BF16 = jnp.bfloat16
HIGHEST = lax.Precision.HIGHEST

EPS = 1e-6
HEADS = 8
HEAD_DIM = 128
CHUNK = 64
GRID_W = 64
HY_EMB = 33
HY_DECAY_TARGET = 1e-2
HY_FAST_DECAY = 0.3
HY_SLOW_DECAY = 1.5

LANES = 128
SUBLANES = 8
FFT_INNER = 128
CONV_FFT_STEPS = 8
FILTER_FFT_STEPS = 8
FFT_CHANNELS = 256
FFT_INTERLEAVE = 8
PROJ_TILE = 1024
NORM_TILE = 512
MXU_TILE = 256
SLAB_PAD = 8
VMEM_LIMIT = 56 * 1024 * 1024
VMEM_LIMIT_CONV = 62 * 1024 * 1024


def _cparams(sem, vmem=VMEM_LIMIT):
    return pltpu.CompilerParams(dimension_semantics=sem, vmem_limit_bytes=vmem)


def _bdot(a, b):
    return jnp.dot(a.astype(BF16), b.astype(BF16), preferred_element_type=F32)


def _silu(x):
    return x * jax.nn.sigmoid(x)


def _mod_kernel(c_ref, w_ref, b_ref, o_ref):
    s = _silu(c_ref[...])
    o_ref[...] = jnp.dot(s, w_ref[...], precision=HIGHEST, preferred_element_type=F32) + b_ref[...]


def _mod_call(cvec, w_mod, b_mod):
    rows, d = cvec.shape
    n = w_mod.shape[1]
    tn = 1024
    return pl.pallas_call(
        _mod_kernel,
        grid=(n // tn,),
        in_specs=[pl.BlockSpec((rows, d), lambda j: (0, 0)),
                  pl.BlockSpec((d, tn), lambda j: (0, j)),
                  pl.BlockSpec((1, tn), lambda j: (0, j))],
        out_specs=pl.BlockSpec((rows, tn), lambda j: (0, j)),
        out_shape=jax.ShapeDtypeStruct((rows, n), F32),
        compiler_params=_cparams(("arbitrary",)),
        name="mod",
    )(cvec, w_mod, b_mod.reshape(1, n))


def _hnorm_kernel(tok_ref, mod_ref, g_ref, w_ref, alog_ref, dtb_ref, *rest, d, mod_row):
    o_ref, og_ref = rest[-2:]
    row = pl.program_id(0) if mod_row is None else mod_row
    m = mod_ref[pl.ds(row, 1), :]
    x = tok_ref[0]
    y = x * lax.rsqrt(jnp.mean(x * x, axis=-1, keepdims=True) + EPS) * g_ref[...]
    h = (y * (1.0 + m[:, d:2 * d]) + m[:, 0:d]).astype(BF16)
    o_ref[0] = h
    z = jnp.dot(h, w_ref[...], preferred_element_type=F32)
    lane = lax.broadcasted_iota(jnp.int32, z.shape, 1)
    u = z + dtb_ref[...]
    softplus = jnp.maximum(u, 0.0) + jnp.log(1.0 + jnp.exp(-jnp.abs(u)))
    og_ref[0] = jnp.where(lane < 2 * HEADS, jax.nn.sigmoid(z), -jnp.exp(alog_ref[...]) * softplus)


def _hnorm_call(tok, mod, norm_g, wg, alog, dtb, *, tm, tile0, out_rows, mod_row=None, prev=None, name):
    bsz, rows, d = tok.shape
    full = lambda a: pl.BlockSpec(a.shape, lambda b, t: (0,) * a.ndim)
    args = [tok, mod, norm_g.reshape(1, d), wg, alog, dtb]
    in_specs = [pl.BlockSpec((1, tm, d), lambda b, t: (b, t, 0))] + [full(a) for a in args[1:]]
    aliases = {}
    if prev is not None:
        in_specs += [pl.BlockSpec(memory_space=pl.ANY)] * 2
        args += list(prev)
        aliases = {6: 0, 7: 1}
    return pl.pallas_call(
        functools.partial(_hnorm_kernel, d=d, mod_row=mod_row),
        grid=(bsz, rows // tm),
        in_specs=in_specs,
        out_specs=[pl.BlockSpec((1, tm, d), lambda b, t: (b, t + tile0, 0)),
                   pl.BlockSpec((1, tm, LANES), lambda b, t: (b, t + tile0, 0))],
        out_shape=[jax.ShapeDtypeStruct((bsz, out_rows, d), BF16),
                   jax.ShapeDtypeStruct((bsz, out_rows, LANES), F32)],
        input_output_aliases=aliases,
        compiler_params=_cparams(("arbitrary", "arbitrary")),
        name=name,
    )(*args)


@functools.lru_cache(maxsize=None)
def _shift_matrix(rowlen, taps, blk):
    t = np.arange(blk)
    mats = []
    for j in range(taps):
        d = j - taps // 2
        if d != 0:
            ok = ((t + d) >= 0) & ((t + d) < blk) & (t // rowlen == (t + d) // rowlen)
            s = np.zeros((blk, blk), np.float32)
            s[t[ok], t[ok] + d] = 1.0
            mats.append(s)
    return np.concatenate(mats, axis=1)


def _shift_block(rowlen, taps):
    return max(rowlen, MXU_TILE // (taps - 1) // rowlen * rowlen)


def _convproj_kernel(h_ref, w_ref, cw_ref, cb_ref, sh_ref, *rest, tm, qk_norm):
    o_ref = rest[-1]
    j = pl.program_id(0)
    taps = cw_ref.shape[1]
    mid = taps // 2
    blk = sh_ref.shape[0]
    mblk = max(blk, MXU_TILE)
    if qk_norm:
        scale = jnp.where(j == 0, HEAD_DIM ** -0.5, 1.0)
        is_qk = j < 2
    for r in range(tm // mblk):
        zm = jnp.dot(h_ref[0, r * mblk:(r + 1) * mblk, :], w_ref[0], preferred_element_type=F32)
        for g in range(mblk // blk):
            rows = slice(r * mblk + g * blk, r * mblk + (g + 1) * blk)
            zb = zm[g * blk:(g + 1) * blk]
            side = jnp.concatenate(
                [(zb * cw_ref[0, jt:jt + 1, :]).astype(BF16) for jt in range(taps) if jt != mid], axis=0)
            y = zb * cw_ref[0, mid:mid + 1, :] + jnp.dot(sh_ref[...], side, preferred_element_type=F32)
            if not qk_norm:
                o_ref[0, 0, rows, :] = (y + cb_ref[0]).astype(o_ref.dtype)
                continue
            y = _silu(y)
            for hh in range(HEADS):
                yh = y[:, hh * HEAD_DIM:(hh + 1) * HEAD_DIM]
                nrm = jnp.sum(yh * yh, axis=-1, keepdims=True)
                f = jnp.where(is_qk, lax.rsqrt(nrm + EPS) * scale, 1.0)
                o_ref[0, 0, rows, hh * HEAD_DIM:(hh + 1) * HEAD_DIM] = (yh * f).astype(o_ref.dtype)


def _convproj_call(h, w3, cw3, cb3, *, rowlen, tm, tile0, ntiles, out_rows, qk_norm, prev=None, name):
    bsz, _, d = h.shape
    nw = w3.shape[2]
    taps = cw3.shape[1]
    blk = _shift_block(rowlen, taps)
    assert tm % max(blk, MXU_TILE) == 0 and max(blk, MXU_TILE) % blk == 0
    sh = jnp.asarray(_shift_matrix(rowlen, taps, blk)).astype(BF16)
    in_specs = [pl.BlockSpec((1, tm, d), lambda j, b, t: (b, t + tile0, 0)),
                pl.BlockSpec((1, d, nw), lambda j, b, t: (j, 0, 0)),
                pl.BlockSpec((1, cw3.shape[1], nw), lambda j, b, t: (j, 0, 0)),
                pl.BlockSpec((1, 1, nw), lambda j, b, t: (j, 0, 0)),
                pl.BlockSpec(sh.shape, lambda j, b, t: (0, 0))]
    args = [h, w3, cw3, cb3, sh]
    aliases = {}
    if prev is not None:
        in_specs.append(pl.BlockSpec(memory_space=pl.ANY))
        args.append(prev)
        aliases = {5: 0}
    return pl.pallas_call(
        functools.partial(_convproj_kernel, tm=tm, qk_norm=qk_norm),
        grid=(3, bsz, ntiles),
        in_specs=in_specs,
        out_specs=pl.BlockSpec((1, 1, tm, nw), lambda j, b, t: (j, b, t + tile0, 0)),
        out_shape=jax.ShapeDtypeStruct((3, bsz, out_rows, nw), BF16),
        input_output_aliases=aliases,
        compiler_params=_cparams(("arbitrary", "arbitrary", "arbitrary")),
        name=name,
    )(*args)


def _gproj_kernel(h_ref, w_ref, o_ref):
    j = pl.program_id(0)
    for r in range(h_ref.shape[1] // MXU_TILE):
        rows = slice(r * MXU_TILE, (r + 1) * MXU_TILE)
        z = jnp.dot(h_ref[0, rows, :], w_ref[0], preferred_element_type=F32)
        s = jax.nn.sigmoid(z)
        o_ref[0, 0, rows, :] = jnp.where(j < 2, z * s, s).astype(BF16)


def _gproj_call(h, w4, n_lat):
    bsz, _, d = h.shape
    nw = w4.shape[2]
    tm = PROJ_TILE if n_lat % PROJ_TILE == 0 else 256
    return pl.pallas_call(
        _gproj_kernel,
        grid=(4, bsz, n_lat // tm),
        in_specs=[pl.BlockSpec((1, tm, d), lambda j, b, t: (b, t, 0)),
                  pl.BlockSpec((1, d, nw), lambda j, b, t: (j, 0, 0))],
        out_specs=pl.BlockSpec((1, 1, tm, nw), lambda j, b, t: (j, b, t, 0)),
        out_shape=jax.ShapeDtypeStruct((4, bsz, n_lat, nw), BF16),
        compiler_params=_cparams(("arbitrary", "arbitrary", "arbitrary")),
        name="gproj",
    )(h, w4)


def _unit_tri_inverse(ls, lowers, filler):
    c = ls[0].shape[0]
    ri = lax.broadcasted_iota(jnp.int32, (c, c), 0)
    ci = lax.broadcasted_iota(jnp.int32, (c, c), 1)
    same = lambda n: (ri // n) == (ci // n)
    eye = jnp.where(ri == ci, 1.0, 0.0)
    in2 = same(2)
    ts = [eye - jnp.where(in2, l, 0.0) for l in ls]
    n = 2
    nlevels = c.bit_length() - 2
    level = 0
    while n < c:
        blk = same(2 * n) & jnp.logical_not(same(n))
        offs = [jnp.where(blk, l, 0.0).astype(BF16) for l in ls]
        if n % SUBLANES:
            prods = [_bdot(t, off) for t, off in zip(ts, offs)]
            filler(level, nlevels)
            ts = [t - _bdot(p, t) for p, t in zip(prods, ts)]
        else:
            def moving(t, lower):
                first = n if lower else 0
                return jnp.concatenate([t[r:r + n] for r in range(first, c, 2 * n)], axis=0)

            def merged(t, upd, lower):
                pieces, k = [], 0
                for r in range(0, c, n):
                    if ((r // n) % 2 == 1) == lower:
                        pieces.append(upd[k * n:(k + 1) * n])
                        k += 1
                    else:
                        pieces.append(t[r:r + n])
                return jnp.concatenate(pieces, axis=0)

            rows = [moving(t, lo) for t, lo in zip(ts, lowers)]
            prods = [_bdot(r, off) for r, off in zip(rows, offs)]
            filler(level, nlevels)
            upds = [r - _bdot(p, t) for r, p, t in zip(rows, prods, ts)]
            ts = [merged(t, u, lo) for t, u, lo in zip(ts, upds, lowers)]
        n *= 2
        level += 1
    return ts


_NT = (((1,), (1,)), ((), ()))
_TN = (((0,), (0,)), ((), ()))


def _prep_kernel(qkv_ref, g_ref, gt_ref, tall_ref, wide_ref, el_ref, *, bsz):
    c = CHUNK
    ri = lax.broadcasted_iota(jnp.int32, (c, c), 0)
    ci = lax.broadcasted_iota(jnp.int32, (c, c), 1)
    tri_l = jnp.where(ri >= ci, 1.0, 0.0)
    tri_u = jnp.where(ri <= ci, 1.0, 0.0)
    incl = (ri >= ci, ri <= ci)
    strict = (ri > ci, ri < ci)
    hdot = functools.partial(jnp.dot, precision=HIGHEST, preferred_element_type=F32)

    gs = [g_ref[b] for b in range(bsz)]
    gts = [gt_ref[b, 0] for b in range(bsz)]
    gcol = [(hdot(tri_l, g), hdot(tri_u, g)) for g in gs]
    grow = [(hdot(gt, tri_u), hdot(gt, tri_l)) for gt in gts]
    glast = [(gcol[b][0][c - 1:c], gcol[b][1][0:1]) for b in range(bsz)]
    el_ref[...] = jnp.zeros_like(el_ref)
    for d in range(2):
        for b in range(bsz):
            el_ref[d * bsz + b:d * bsz + b + 1, :] = jnp.exp(glast[b][d])

    bh = [(b, hh) for b in range(bsz) for hh in range(HEADS)]
    hs = lambda hh: slice(hh * HEAD_DIM, (hh + 1) * HEAD_DIM)
    ks = [qkv_ref[1, b, :, hs(hh)] for b, hh in bh]
    qs = [qkv_ref[0, b, :, hs(hh)] for b, hh in bh]
    kk = [lax.dot_general(k, k, _NT, preferred_element_type=F32) for k in ks]
    qk = [lax.dot_general(q, k, _NT, preferred_element_type=F32) for q, k in zip(qs, ks)]
    inst = [(d, b, hh, i) for d in range(2) for i, (b, hh) in enumerate(bh)]
    decs, betas, gcs = [], [], []
    for d, b, hh, _ in inst:
        lg = 2 * HEADS + d * HEADS + hh
        gcs.append(gcol[b][d][:, lg:lg + 1])
        betas.append(gs[b][:, d * HEADS + hh:d * HEADS + hh + 1])
        decs.append(jnp.exp(jnp.where(incl[d], gcs[-1] - grow[b][d][lg:lg + 1, :], -jnp.inf)))
    ls = [jnp.where(strict[d], kk[i] * beta * dec, 0.0) for (d, _, _, i), beta, dec in zip(inst, betas, decs)]
    def scaled_operands(part, nparts):
        for (d, b, hh, i), dec, beta, gc in list(zip(inst, decs, betas, gcs))[part::nparts]:
            lg = 2 * HEADS + d * HEADS + hh
            eg = jnp.exp(gc)
            k = ks[i].astype(F32)
            tall_ref[d, b, hh, :, c:2 * c] = (qk[i] * dec).astype(BF16)
            wide_ref[d, b, hh, 0:c, :] = (qs[i].astype(F32) * eg).astype(BF16)
            wide_ref[d, b, hh, c:2 * c, :] = (k * (beta * eg)).astype(BF16)
            wide_ref[d, b, hh, 2 * c:3 * c, :] = (qkv_ref[2, b, :, hs(hh)].astype(F32) * beta).astype(BF16)
            wide_ref[d, b, hh, 3 * c:4 * c, :] = (k * jnp.exp(glast[b][d][:, lg:lg + 1] - gc)).astype(BF16)

    ts = _unit_tri_inverse(ls, [d == 0 for d, _, _, _ in inst], scaled_operands)
    for (d, b, hh, _), t in zip(inst, ts):
        tall_ref[d, b, hh, :, 0:c] = t.astype(BF16)


def _prep_call(qkv, gates, gates_t):
    _, bsz, lt, dn = qkv.shape
    nc = lt // CHUNK
    tall = (2, bsz, HEADS, CHUNK, 2 * CHUNK)
    wide = (2, bsz, HEADS, 4 * CHUNK, HEAD_DIM)
    return pl.pallas_call(
        functools.partial(_prep_kernel, bsz=bsz),
        grid=(nc,),
        in_specs=[pl.BlockSpec((3, bsz, CHUNK, dn), lambda i: (0, 0, i, 0)),
                  pl.BlockSpec((bsz, CHUNK, LANES), lambda i: (0, i, 0)),
                  pl.BlockSpec((bsz, 1, 4 * HEADS, CHUNK), lambda i: (0, i, 0, 0))],
        out_specs=[pl.BlockSpec((None,) + tall, lambda i: (i, 0, 0, 0, 0, 0)),
                   pl.BlockSpec((None,) + wide, lambda i: (i, 0, 0, 0, 0, 0)),
                   pl.BlockSpec((None, SUBLANES, LANES), lambda i: (i, 0, 0))],
        out_shape=[jax.ShapeDtypeStruct((nc,) + tall, BF16),
                   jax.ShapeDtypeStruct((nc,) + wide, BF16),
                   jax.ShapeDtypeStruct((nc, SUBLANES, LANES), F32)],
        compiler_params=_cparams(("arbitrary",)),
        name="prep",
    )(qkv, gates, gates_t)


def _scan_kernel(tf_ref, tb_ref, wf_ref, wb_ref, ef_ref, eb_ref, of_ref, ob_ref, s_ref, *, bsz):
    i = pl.program_id(0)

    @pl.when(i == 0)
    def _():
        s_ref[...] = jnp.zeros_like(s_ref)

    c = CHUNK
    dirs = ((tf_ref, wf_ref, ef_ref, of_ref), (tb_ref, wb_ref, eb_ref, ob_ref))
    inst = [(d, b, hh) for d in range(2) for b in range(bsz) for hh in range(HEADS)]
    s_old = [s_ref[d, b, hh] for d, b, hh in inst]
    a1 = [jnp.dot(dirs[d][1][b, hh, 0:2 * c, :], s.astype(BF16), preferred_element_type=F32)
          for (d, b, hh), s in zip(inst, s_old)]
    v_new = [jnp.dot(dirs[d][0][b, hh, :, 0:c], (dirs[d][1][b, hh, 2 * c:3 * c, :].astype(F32) - a[c:]).astype(BF16),
                     preferred_element_type=F32).astype(BF16) for (d, b, hh), a in zip(inst, a1)]
    for (d, b, hh), a, vn, s in zip(inst, a1, v_new, s_old):
        lg = 2 * HEADS + d * HEADS + hh
        o = a[:c] + jnp.dot(dirs[d][0][b, hh, :, c:2 * c], vn, preferred_element_type=F32)
        dirs[d][3][b, :, hh * HEAD_DIM:(hh + 1) * HEAD_DIM] = o.astype(BF16)
        el = dirs[d][2][d * bsz + b:d * bsz + b + 1, lg:lg + 1]
        s_ref[d, b, hh] = s * el + lax.dot_general(dirs[d][1][b, hh, 3 * c:4 * c, :], vn, _TN,
                                                   preferred_element_type=F32)


def _scan_call(tall, wide, el, n_lat, n_ctx):
    nc, _, bsz = tall.shape[:3]
    dn = HEADS * HEAD_DIM
    ncx = n_lat // CHUNK
    ncc = n_ctx // CHUNK
    assert nc == ncx + ncc
    cf = lambda i: jnp.where(i < ncc, ncx + i, i - ncc)
    cb = lambda i: nc - 1 - i

    dspec = lambda arr, fn, d: pl.BlockSpec((None, None) + arr.shape[2:], lambda i: (fn(i), d, 0, 0, 0, 0))
    espec = lambda fn: pl.BlockSpec((None, SUBLANES, LANES), lambda i: (fn(i), 0, 0))
    ospec = lambda fn: pl.BlockSpec((bsz, CHUNK, dn), lambda i: (0, fn(i), 0))
    return pl.pallas_call(
        functools.partial(_scan_kernel, bsz=bsz),
        grid=(nc,),
        in_specs=[dspec(tall, cf, 0), dspec(tall, cb, 1), dspec(wide, cf, 0), dspec(wide, cb, 1),
                  espec(cf), espec(cb)],
        out_specs=[ospec(cf), ospec(cb)],
        out_shape=[jax.ShapeDtypeStruct((bsz, nc * CHUNK, dn), BF16)] * 2,
        scratch_shapes=[pltpu.VMEM((2, bsz, HEADS, HEAD_DIM, HEAD_DIM), F32)],
        compiler_params=_cparams(("arbitrary",)),
        name="scan",
    )(tall, tall, wide, wide, el, el)


def _hidden_kernel(pe_ref, w1_ref, b1_ref, w2_ref, b2_ref, w3_ref, b3_ref, f_ref, o_ref):
    f = f_ref[...]
    dot = functools.partial(jnp.dot, precision=HIGHEST, preferred_element_type=F32)
    a = jnp.sin(f * (dot(pe_ref[...], w1_ref[...]) + b1_ref[...]))
    a = jnp.sin(f * (dot(a, w2_ref[...]) + b2_ref[...]))
    o_ref[...] = jnp.sin(f * (dot(a, w3_ref[...]) + b3_ref[...]))


def _hidden_call(pe, w1p, b1, w2, b2, w3, b3, freq):
    rows, pw = pe.shape
    fw = 2 * w2.shape[0]
    tr = 512
    full = lambda a: pl.BlockSpec(a.shape, lambda t: (0,) * a.ndim)
    two = lambda a: jnp.tile(a.reshape(1, -1), (1, 2))
    diag2 = lambda a: jnp.kron(jnp.eye(2, dtype=a.dtype), a)
    args = (diag2(w1p), two(b1), diag2(w2), two(b2), diag2(w3), two(b3), two(freq))
    return pl.pallas_call(
        _hidden_kernel,
        grid=(rows // tr,),
        in_specs=[pl.BlockSpec((tr, pw), lambda t: (t, 0))] + [full(a) for a in args],
        out_specs=pl.BlockSpec((tr, fw), lambda t: (t, 0)),
        out_shape=jax.ShapeDtypeStruct((rows, fw), F32),
        compiler_params=_cparams(("arbitrary",)),
        name="hidden",
    )(pe, *args)


def _ftime_kernel(a_ref, t_ref, w_ref, dl_ref, h_ref, ss_ref, *, n_lat, tr):
    t = pl.program_id(0)
    h = _bdot(a_ref[...], w_ref[...]) * jnp.exp(-t_ref[...] * dl_ref[...])
    row = lax.broadcasted_iota(jnp.int32, (tr, 1), 0) + t * tr
    h = jnp.where(row == n_lat, 0.0, h)
    h_ref[...] = h.astype(h_ref.dtype)

    @pl.when(t == 0)
    def _():
        ss_ref[...] = jnp.zeros_like(ss_ref)

    ss_ref[...] += jnp.sum(h * h, axis=0, keepdims=True)


def _ftime_call(a3, tcol, wout2, deltas2, n_lat):
    rows = tcol.shape[0]
    fw = a3.shape[1]
    ncol = wout2.shape[2]
    tr = 512
    side = lambda t: (t * tr) // n_lat
    return pl.pallas_call(
        functools.partial(_ftime_kernel, n_lat=n_lat, tr=tr),
        grid=(rows // tr,),
        in_specs=[pl.BlockSpec((tr, fw), lambda t: (t % (n_lat // tr), 0)),
                  pl.BlockSpec((tr, 1), lambda t: (t, 0)),
                  pl.BlockSpec((None, fw, ncol), lambda t: (side(t), 0, 0)),
                  pl.BlockSpec((1, ncol), lambda t: (0, 0))],
        out_specs=[pl.BlockSpec((tr, ncol), lambda t: (t, 0)),
                   pl.BlockSpec((1, ncol), lambda t: (0, 0))],
        out_shape=[jax.ShapeDtypeStruct((rows, ncol), BF16), jax.ShapeDtypeStruct((1, ncol), F32)],
        compiler_params=_cparams(("arbitrary",)),
        name="ftime",
    )(a3, tcol, wout2, deltas2)


@functools.lru_cache(maxsize=None)
def _fft_tables(n_lat):
    n = 2 * n_lat
    n1t = n // FFT_INNER
    h1 = n1t // 2
    n2 = np.arange(FFT_INNER, dtype=np.int64)[:, None, None]
    k1 = np.arange(n1t, dtype=np.int64)[None, :, None]
    n1 = np.arange(h1, dtype=np.int64)[None, None, :]
    ang = (-2.0 * np.pi / n) * ((k1 * (FFT_INNER * n1 + n2)) % n)
    er, ei = np.cos(ang), np.sin(ang)
    e1 = np.concatenate([np.concatenate([er, -ei], axis=2), np.concatenate([ei, er], axis=2)], axis=1)
    e4 = np.transpose(e1, (0, 2, 1))
    sgn = np.where(np.arange(n1t) % 2 == 0, 1.0, -1.0)[None, :, None]
    ef = np.concatenate([np.concatenate([er, sgn * er], axis=2), np.concatenate([ei, sgn * ei], axis=2)], axis=1)
    a = np.arange(FFT_INNER, dtype=np.int64)
    ang2 = (-2.0 * np.pi / FFT_INNER) * ((a[:, None] * a[None, :]) % FFT_INNER)
    gr, gi = np.cos(ang2), np.sin(ang2)
    g2 = np.block([[gr, -gi], [gi, gr]])
    f32 = lambda x: np.ascontiguousarray(x, dtype=np.float32)
    return f32(e1), f32(e4), f32(ef), f32(g2), f32(g2.T)


def _hold(p, j, first, last, nsteps):
    active = (p >= first) & (p <= last) & (lax.rem(p - first, 2) == 0)
    return jnp.where(active, j, jnp.where(p < first, 0, nsteps - 1))


def _slab_read(x_ref, rows):
    return jnp.concatenate([x_ref[c, rows, :] for c in range(x_ref.shape[0])], axis=1)


def _slab_write(x_ref, rows, val):
    for c in range(x_ref.shape[0]):
        x_ref[c, rows, :] = val[:, c * LANES:(c + 1) * LANES]


def _spectrum_steps(x_ref, g2_ref, k1s, n1t, stride):
    cols = [jnp.concatenate([_slab_read(x_ref, pl.ds(k1, FFT_INNER, stride=stride)),
                             _slab_read(x_ref, pl.ds(n1t + k1, FFT_INNER, stride=stride))], axis=0).astype(BF16)
            for k1 in k1s]
    return [jnp.dot(g2_ref[...], c, preferred_element_type=F32) for c in cols]


def _ffft_kernel(h_ref, ef_ref, g2_ref, ss_ref, o_ref, x_ref, *, n1t, grp, kgrp, stride, inv_n):
    p = pl.program_id(1)
    j = pl.program_id(2)

    @pl.when(p == 0)
    def _():
        for j0 in range(0, grp, FFT_INTERLEAVE):
            jjs = list(range(j0, min(j0 + FFT_INTERLEAVE, grp)))
            us = [jnp.concatenate([h_ref[0, jj], h_ref[1, jj]], axis=0) for jj in jjs]
            slabs = [_bdot(ef_ref[jj], u) for jj, u in zip(jjs, us)]
            for jj, slab in zip(jjs, slabs):
                base = pl.multiple_of((j * grp + jj) * stride, SUBLANES)
                _slab_write(x_ref, pl.ds(base, 2 * n1t), slab)

    @pl.when(p == 1)
    def _():
        scale = lax.rsqrt(ss_ref[...] + EPS) * inv_n
        for j0 in range(0, kgrp, FFT_INTERLEAVE):
            jjs = list(range(j0, min(j0 + FFT_INTERLEAVE, kgrp)))
            xfs = _spectrum_steps(x_ref, g2_ref, [j * kgrp + jj for jj in jjs], n1t, stride)
            for jj, xf in zip(jjs, xfs):
                o_ref[jj] = (xf * scale).astype(BF16)


def _fft_groups(n1t, ns):
    ns = min(ns, n1t)
    assert FFT_INNER % ns == 0 and n1t % ns == 0
    return ns, FFT_INNER // ns, n1t // ns


def _ffft_call(hfull, ssq, ef, g2, n_lat):
    ncol = hfull.shape[1]
    n1t = 2 * n_lat // FFT_INNER
    h1 = n1t // 2
    ct = FFT_CHANNELS
    ns, grp, kgrp = _fft_groups(n1t, FILTER_FFT_STEPS)
    stride = 2 * n1t + SLAB_PAD
    hv = hfull.reshape(2, FFT_INNER, h1, ncol)
    nct = ncol // ct
    cpo = nct // 2
    return pl.pallas_call(
        functools.partial(_ffft_kernel, n1t=n1t, grp=grp, kgrp=kgrp, stride=stride, inv_n=1.0 / (2 * n_lat)),
        grid=(nct, 2, ns),
        in_specs=[pl.BlockSpec((2, grp, h1, ct), lambda c, p, j: (0, _hold(p, j, 0, 0, ns), 0, c)),
                  pl.BlockSpec((grp, 2 * n1t, n1t), lambda c, p, j: (_hold(p, j, 0, 0, ns), 0, 0)),
                  pl.BlockSpec((2 * FFT_INNER, 2 * FFT_INNER), lambda c, p, j: (0, 0)),
                  pl.BlockSpec((1, ct), lambda c, p, j: (0, c))],
        out_specs=pl.BlockSpec((None, kgrp, 2 * FFT_INNER, ct),
                               lambda c, p, j: (c // cpo, _hold(p, j, 1, 1, ns), 0, c % cpo)),
        out_shape=jax.ShapeDtypeStruct((2, n1t, 2 * FFT_INNER, ncol // 2), BF16),
        scratch_shapes=[pltpu.VMEM((ct // LANES, FFT_INNER * stride, LANES), F32)],
        compiler_params=_cparams(("arbitrary", "arbitrary", "arbitrary")),
        name="ffft",
    )(hv, ef, g2, ssq)


def _hyena_kernel(v_ref, x1_ref, x2_ref, hf_ref, e1_ref, e4_ref, g2_ref, g3_ref, bias_ref,
                  o_ref, x_ref, z_ref, *, n1t, grp, kgrp, stride):
    p = pl.program_id(1)
    j = pl.program_id(2)
    h1 = n1t // 2

    def gather(ref, jj):
        return jnp.concatenate([ref[0, jj], ref[1, jj]], axis=0).astype(F32)

    def slab_base(jj):
        return pl.multiple_of((j * grp + jj) * stride, SUBLANES)

    def time_rows(jjs):
        slabs = [_slab_read(x_ref, pl.ds(slab_base(jj), 2 * n1t)).astype(BF16) for jj in jjs]
        return [jnp.dot(e4_ref[jj], s, preferred_element_type=F32) for jj, s in zip(jjs, slabs)]

    def write_slabs(jjs, us):
        slabs = [jnp.dot(e1_ref[jj], u, preferred_element_type=F32) for jj, u in zip(jjs, us)]
        for jj, slab in zip(jjs, slabs):
            _slab_write(x_ref, pl.ds(slab_base(jj), 2 * n1t), slab)

    def chunks(n):
        return [list(range(j0, min(j0 + FFT_INTERLEAVE, n))) for j0 in range(0, n, FFT_INTERLEAVE)]

    @pl.when(p == 0)
    def _():
        for jjs in chunks(grp):
            write_slabs(jjs, [gather(v_ref, jj).astype(BF16) for jj in jjs])

    @pl.when((p == 1) | (p == 3))
    def _():
        for jjs in chunks(kgrp):
            k1s = [j * kgrp + jj for jj in jjs]
            xfs = _spectrum_steps(x_ref, g2_ref, k1s, n1t, stride)
            ys = []
            for jj, xf in zip(jjs, xfs):
                hf = hf_ref[jj].astype(F32)
                xr, xi = xf[:FFT_INNER], xf[FFT_INNER:]
                hr, hi = hf[:FFT_INNER], hf[FFT_INNER:]
                ys.append(jnp.concatenate([xr * hr - xi * hi, xr * hi + xi * hr], axis=0).astype(BF16))
            bbs = [jnp.dot(g3_ref[...], y, preferred_element_type=F32) for y in ys]
            for k1, bb in zip(k1s, bbs):
                _slab_write(x_ref, pl.ds(k1, FFT_INNER, stride=stride), bb[:FFT_INNER])
                _slab_write(x_ref, pl.ds(n1t + k1, FFT_INNER, stride=stride), bb[FFT_INNER:])

    @pl.when(p == 2)
    def _():
        for jjs in chunks(grp):
            z1s = [(gather(x1_ref, jj) * (y + gather(v_ref, jj) * bias_ref[0:1, :])).astype(BF16)
                   for jj, y in zip(jjs, time_rows(jjs))]
            for jj, z1 in zip(jjs, z1s):
                z_ref[j * grp + jj] = z1
            write_slabs(jjs, z1s)

    @pl.when(p == 4)
    def _():
        for jjs in chunks(grp):
            for jj, y in zip(jjs, time_rows(jjs)):
                out = gather(x2_ref, jj) * (y + z_ref[j * grp + jj].astype(F32) * bias_ref[1:2, :])
                o_ref[0, jj] = out[:h1].astype(o_ref.dtype)
                o_ref[1, jj] = out[h1:].astype(o_ref.dtype)


def _hyena_call(hyp, hf, e1, e4, g2, g3, bias, n_lat):
    _, bsz, _, ch = hyp.shape
    assert bsz == 2, "the two batch rows are packed as the real and imaginary parts of one FFT"
    n1t = 2 * n_lat // FFT_INNER
    h1 = n1t // 2
    ct = FFT_CHANNELS
    ns, grp, kgrp = _fft_groups(n1t, CONV_FFT_STEPS)
    stride = 2 * n1t + SLAB_PAD
    hv = jnp.transpose(hyp.reshape(3, bsz, h1, FFT_INNER, ch), (0, 1, 3, 2, 4))

    def hspec(which, first, last):
        return pl.BlockSpec((None, bsz, grp, h1, ct),
                            lambda c, p, j: (which, 0, _hold(p, j, first, last, ns), 0, c))

    def hf_map(c, p, j):
        order = jnp.where(p >= 2, 1, 0)
        step = jnp.where((p == 1) | (p == 3), j, jnp.where((p == 0) | (p == 2), 0, ns - 1))
        return (order, step, 0, c)

    out = pl.pallas_call(
        functools.partial(_hyena_kernel, n1t=n1t, grp=grp, kgrp=kgrp, stride=stride),
        grid=(ch // ct, 5, ns),
        in_specs=[hspec(0, 0, 2), hspec(1, 2, 2), hspec(2, 4, 4),
                  pl.BlockSpec((None, kgrp, 2 * FFT_INNER, ct), hf_map),
                  pl.BlockSpec((grp, 2 * n1t, n1t), lambda c, p, j: (_hold(p, j, 0, 2, ns), 0, 0)),
                  pl.BlockSpec((grp, n1t, 2 * n1t), lambda c, p, j: (_hold(p, j, 2, 4, ns), 0, 0)),
                  pl.BlockSpec((2 * FFT_INNER, 2 * FFT_INNER), lambda c, p, j: (0, 0)),
                  pl.BlockSpec((2 * FFT_INNER, 2 * FFT_INNER), lambda c, p, j: (0, 0)),
                  pl.BlockSpec((2, ct), lambda c, p, j: (0, c))],
        out_specs=pl.BlockSpec((bsz, grp, h1, ct), lambda c, p, j: (0, _hold(p, j, 4, 4, ns), 0, c)),
        out_shape=jax.ShapeDtypeStruct((bsz, FFT_INNER, h1, ch), BF16),
        scratch_shapes=[pltpu.VMEM((ct // LANES, FFT_INNER * stride, LANES), F32),
                        pltpu.VMEM((FFT_INNER, n1t, ct), BF16)],
        compiler_params=_cparams(("arbitrary", "arbitrary", "arbitrary"), VMEM_LIMIT_CONV),
        name="hyena",
    )(hv, hv, hv, hf, e1, e4, g2, g3, bias)
    return jnp.transpose(out, (0, 2, 1, 3)).reshape(bsz, n_lat, ch)


def _merge_kernel(of_ref, ob_ref, gp_ref, hy_ref, x_ref, mod_ref, dng_ref, wpa_ref, wpb_ref, wo_ref,
                  fg_ref, o_ref, *, d):
    b = pl.program_id(0)
    blocks = [slice(r * MXU_TILE, (r + 1) * MXU_TILE) for r in range(o_ref.shape[1] // MXU_TILE)]
    f32 = lambda ref, *idx: ref[idx].astype(F32)

    def head_norm(rows):
        o = f32(of_ref, 0, rows) + f32(ob_ref, 0, rows)
        za = f32(gp_ref, 0, 0, rows)
        parts = []
        for hh in range(HEADS):
            hs = slice(hh * HEAD_DIM, (hh + 1) * HEAD_DIM)
            oh = o[:, hs]
            ms = jnp.mean(oh * oh, axis=-1, keepdims=True)
            parts.append((oh * lax.rsqrt(ms + EPS) * dng_ref[...] * za[:, hs]).astype(BF16))
        return jnp.concatenate(parts, axis=1)

    o_a = [head_norm(rows) for rows in blocks]
    y_a = [jnp.dot(a, wpa_ref[...], preferred_element_type=F32) for a in o_a]
    o_b = [(f32(hy_ref, 0, rows) * f32(gp_ref, 1, 0, rows)).astype(BF16) for rows in blocks]
    y_b = [jnp.dot(a, wpb_ref[...], preferred_element_type=F32) for a in o_b]
    m = [(f32(gp_ref, 2, 0, rows) * ya + f32(gp_ref, 3, 0, rows) * yb).astype(BF16)
         for rows, ya, yb in zip(blocks, y_a, y_b)]
    y = [jnp.dot(a, wo_ref[...], preferred_element_type=F32) for a in m]
    gate = mod_ref[pl.ds(b, 1), :][:, 2 * d:3 * d]
    for rows, yr in zip(blocks, y):
        xn = x_ref[0, rows, :] + gate * yr
        o_ref[0, rows, :] = xn * lax.rsqrt(jnp.mean(xn * xn, axis=-1, keepdims=True) + EPS) * fg_ref[...]


def _merge_call(o_f, o_b, gp, hy, x, mod, dn_norm_g, w_pa, w_pb, w_out, final_g):
    bsz, n_lat, d = x.shape
    tm = NORM_TILE if n_lat % NORM_TILE == 0 else 256
    wfull = lambda a: pl.BlockSpec(a.shape, lambda b, t: (0,) * a.ndim)
    tok = pl.BlockSpec((1, tm, d), lambda b, t: (b, t, 0))
    return pl.pallas_call(
        functools.partial(_merge_kernel, d=d),
        grid=(bsz, n_lat // tm),
        in_specs=[tok, tok,
                  pl.BlockSpec((4, 1, tm, d), lambda b, t: (0, b, t, 0)),
                  tok, tok, wfull(mod),
                  pl.BlockSpec((1, HEAD_DIM), lambda b, t: (0, 0)),
                  wfull(w_pa), wfull(w_pb), wfull(w_out),
                  pl.BlockSpec((1, d), lambda b, t: (0, 0))],
        out_specs=tok,
        out_shape=jax.ShapeDtypeStruct((bsz, n_lat, d), F32),
        compiler_params=_cparams(("arbitrary", "arbitrary")),
        name="merge",
    )(o_f, o_b, gp, hy, x, mod, dn_norm_g.reshape(1, HEAD_DIM), w_pa, w_pb, w_out, final_g.reshape(1, d))


def _position_features(n_lat):
    h1 = n_lat // FFT_INNER
    m = (np.arange(FFT_INNER)[:, None] + FFT_INNER * np.arange(h1)[None, :]).reshape(-1)
    lag = np.concatenate([m, n_lat - m]).astype(np.float64)
    lag[n_lat] = 0.0
    bands = (HY_EMB - 1) // 2
    t = (lag / (n_lat - 1))[:, None]
    wpos = (2.0 * math.pi / n_lat) * lag[:, None]
    fb = np.linspace(1e-4, bands - 1, bands)[None, :]
    pe = np.concatenate([t, np.cos(fb * wpos), -np.sin(fb * wpos)], axis=1)
    pe_pad = np.zeros((2 * n_lat, LANES), np.float32)
    pe_pad[:, :HY_EMB] = pe
    return pe_pad, np.ascontiguousarray(t, dtype=np.float32)


def kernel(x, c, ctx, c_ctx, w_mod, b_mod, norm_g, w_in, dn_conv_w, dn_a_log, dn_dt_bias, dn_norm_g,
           hy_conv_w, hy_conv_b, hy_f_w1, hy_f_b1, hy_f_w2, hy_f_b2, hy_f_w3, hy_f_b3, hy_f_wout,
           hy_f_freq, hy_bias, w_pa, w_pb, w_out, final_g):
    bsz, n_lat, d = x.shape
    n_ctx = ctx.shape[1]
    assert w_mod.shape[0] == 1, "single layer: the context stream is only read through its scan states"
    dn = HEADS * HEAD_DIM
    hy = hy_bias.shape[-1]
    assert n_lat % 256 == 0 and n_ctx % 256 == 0 and dn == d and hy == d

    cvec = jnp.zeros((SUBLANES, d), F32).at[:bsz].set(c).at[bsz].set(c_ctx)
    mod = _mod_call(cvec, w_mod[0], b_mod[0])

    w = w_in[0].astype(BF16)
    o_qkv, o_gate, o_hy, o_gp = 0, 4 * dn, 4 * dn + 4 * HEADS, 4 * dn + 4 * HEADS + 3 * hy
    col3 = lambda a, off, n: jnp.transpose(a[:, off:off + 3 * n].reshape(a.shape[0], 3, n), (1, 0, 2))
    lt = n_lat + n_ctx
    tm = PROJ_TILE if n_lat % PROJ_TILE == 0 else 256

    wg = jnp.zeros((d, LANES), BF16).at[:, :4 * HEADS].set(w[:, o_gate:o_gate + 4 * HEADS])
    lane_pad = lambda a: jnp.zeros((1, LANES), F32).at[0, 2 * HEADS:4 * HEADS].set(a.reshape(-1))
    gate_args = (wg, lane_pad(dn_a_log[0]), lane_pad(dn_dt_bias[0]))
    hg = _hnorm_call(x, mod, norm_g[0], *gate_args, tm=NORM_TILE, tile0=0, out_rows=lt, name="hnorm")
    h, gates = _hnorm_call(ctx, mod, norm_g[0], *gate_args, tm=n_ctx, tile0=n_lat // n_ctx, out_rows=lt,
                           mod_row=bsz, prev=hg, name="hnorm_ctx")

    w_qkv, cw_qkv, no_bias = col3(w, o_qkv, dn), col3(dn_conv_w[0], 0, dn), jnp.zeros((3, 1, dn), F32)
    qkv = _convproj_call(h, w_qkv, cw_qkv, no_bias, rowlen=GRID_W, tm=tm, tile0=0, ntiles=n_lat // tm,
                         out_rows=lt, qk_norm=True, name="qkv")
    qkv = _convproj_call(h, w_qkv, cw_qkv, no_bias, rowlen=n_ctx, tm=n_ctx, tile0=n_lat // n_ctx, ntiles=1,
                         out_rows=lt, qk_norm=True, prev=qkv, name="qkv_ctx")

    gates_t =jnp.transpose(gates[:, :, :4 * HEADS].reshape(bsz, lt // CHUNK, CHUNK, 4 * HEADS), (0, 1, 3, 2))
    tall, wide, el = _prep_call(qkv, gates, gates_t)
    o_f, o_b = _scan_call(tall, wide, el, n_lat, n_ctx)

    hyp = _convproj_call(h, col3(w, o_hy, hy), col3(hy_conv_w[0], 0, hy), hy_conv_b[0].reshape(3, 1, hy),
                         rowlen=GRID_W, tm=tm, tile0=0, ntiles=n_lat // tm, out_rows=n_lat, qk_norm=False,
                         name="hyproj")
    w4 = jnp.stack([w[:, 3 * dn:4 * dn]] + [w[:, o_gp + i * d:o_gp + (i + 1) * d] for i in range(3)])
    gp = _gproj_call(h, w4, n_lat)

    pe, tcol = _position_features(n_lat)
    w1p = jnp.zeros((LANES, hy_f_w1.shape[-1]), F32).at[:HY_EMB].set(hy_f_w1[0])
    pe2 = jnp.asarray(np.concatenate([pe[:n_lat], pe[n_lat:]], axis=1))
    a3 = _hidden_call(pe2, w1p, hy_f_b1[0], hy_f_w2[0], hy_f_b2[0], hy_f_w3[0], hy_f_b3[0], hy_f_freq[0])
    deltas = np.abs(np.linspace(math.log(HY_DECAY_TARGET) / HY_SLOW_DECAY,
                                math.log(HY_DECAY_TARGET) / HY_FAST_DECAY, hy)).astype(np.float32)
    deltas2 = jnp.asarray(np.tile(deltas, 2)[None, :])
    wside = hy_f_wout[0].reshape(-1, 2, 2 * hy).transpose(1, 0, 2).astype(BF16)
    zside = jnp.zeros_like(wside[0])
    wout2 = jnp.stack([jnp.concatenate([wside[0], zside]), jnp.concatenate([zside, wside[1]])])
    hfull, ssq = _ftime_call(a3, jnp.asarray(tcol), wout2, deltas2, n_lat)
    e1, e4, ef, g2, g3 = (jnp.asarray(t).astype(BF16) for t in _fft_tables(n_lat))
    hf = _ffft_call(hfull, ssq, ef, g2, n_lat)

    yh = _hyena_call(hyp, hf, e1, e4, g2, g3, hy_bias[0], n_lat)

    return _merge_call(o_f, o_b, gp, yh, x, mod, dn_norm_g[0], w_pa[0].astype(BF16), w_pb[0].astype(BF16),
                       w_out[0].astype(BF16), final_g)
```

```python
import functools
import math

import numpy as np
import jax
import jax.numpy as jnp
from jax import lax
from jax.experimental import pallas as pl
from jax.experimental.pallas import tpu as pltpu

F32 = jnp.float32
BF16 = jnp.bfloat16
HIGHEST = lax.Precision.HIGHEST

EPS = 1e-6
HEADS = 8
HEAD_DIM = 128
CHUNK = 64
GRID_W = 64
HY_EMB = 33
HY_DECAY_TARGET = 1e-2
HY_FAST_DECAY = 0.3
HY_SLOW_DECAY = 1.5

LANES = 128
SUBLANES = 8
FFT_INNER = 128
CONV_FFT_STEPS = 8
FILTER_FFT_STEPS = 8
FFT_CHANNELS = 256
FFT_INTERLEAVE = 8
PROJ_TILE = 1024
NORM_TILE = 512
NORM_GROUP = 8
HYPROJ_N1 = 16
MXU_TILE = 256
SLAB_PAD = 8
VMEM_LIMIT = 56 * 1024 * 1024
VMEM_LIMIT_CONV = 62 * 1024 * 1024


def _cparams(sem, vmem=VMEM_LIMIT):
    return pltpu.CompilerParams(dimension_semantics=sem, vmem_limit_bytes=vmem)


def _bdot(a, b):
    return jnp.dot(a.astype(BF16), b.astype(BF16), preferred_element_type=F32)


def _silu(x):
    return x * jax.nn.sigmoid(x)


def _mod_kernel(c_ref, w_ref, b_ref, o_ref):
    s = _silu(c_ref[...])
    o_ref[...] = jnp.dot(s, w_ref[...], precision=HIGHEST, preferred_element_type=F32) + b_ref[...]


def _mod_call(cvec, w_mod, b_mod):
    rows, d = cvec.shape
    n = w_mod.shape[1]
    tn = 1024
    return pl.pallas_call(
        _mod_kernel,
        grid=(n // tn,),
        in_specs=[pl.BlockSpec((rows, d), lambda j: (0, 0)),
                  pl.BlockSpec((d, tn), lambda j: (0, j)),
                  pl.BlockSpec((1, tn), lambda j: (0, j))],
        out_specs=pl.BlockSpec((rows, tn), lambda j: (0, j)),
        out_shape=jax.ShapeDtypeStruct((rows, n), F32),
        compiler_params=_cparams(("arbitrary",)),
        name="mod",
    )(cvec, w_mod, b_mod.reshape(1, n))


def _norm_rows(x, m, g_ref, d):
    y = x * lax.rsqrt(jnp.mean(x * x, axis=-1, keepdims=True) + EPS) * g_ref[...]
    return (y * (1.0 + m[:, d:2 * d]) + m[:, 0:d]).astype(BF16)


def _gate_rows(h, w_ref, alog_ref, dtb_ref):
    z = jnp.dot(h, w_ref[...], preferred_element_type=F32)
    lane = lax.broadcasted_iota(jnp.int32, z.shape, 1)
    u = z + dtb_ref[...]
    softplus = jnp.maximum(u, 0.0) + jnp.log(1.0 + jnp.exp(-jnp.abs(u)))
    return jnp.where(lane < 2 * HEADS, jax.nn.sigmoid(z), -jnp.exp(alog_ref[...]) * softplus)


def _hnorm_ctx_kernel(tok_ref, mod_ref, g_ref, w_ref, alog_ref, dtb_ref, o_ref, og_ref, *, d, mod_row):
    m = mod_ref[pl.ds(mod_row, 1), :]
    h = _norm_rows(tok_ref[0], m, g_ref, d)
    o_ref[0] = h
    og_ref[0] = _gate_rows(h, w_ref, alog_ref, dtb_ref)


def _hnorm_ctx_call(tok, mod, norm_g, wg, alog, dtb, mod_row):
    bsz, rows, d = tok.shape
    full = lambda a: pl.BlockSpec(a.shape, lambda b: (0,) * a.ndim)
    args = [tok, mod, norm_g.reshape(1, d), wg, alog, dtb]
    return pl.pallas_call(
        functools.partial(_hnorm_ctx_kernel, d=d, mod_row=mod_row),
        grid=(bsz,),
        in_specs=[pl.BlockSpec((1, rows, d), lambda b: (b, 0, 0))] + [full(a) for a in args[1:]],
        out_specs=[pl.BlockSpec((1, rows, d), lambda b: (b, 0, 0)),
                   pl.BlockSpec((1, rows, LANES), lambda b: (b, 0, 0))],
        out_shape=[jax.ShapeDtypeStruct((bsz, rows, d), BF16), jax.ShapeDtypeStruct((bsz, rows, LANES), F32)],
        compiler_params=_cparams(("arbitrary",)),
        name="hnorm_ctx",
    )(*args)


def _hnorm_lat_kernel(x_ref, mod_ref, g_ref, w_ref, alog_ref, dtb_ref, ot_ref, op_ref, og_ref, *, d, grp):
    m = mod_ref[pl.ds(pl.program_id(0), 1), :]
    h1 = x_ref.shape[1]
    hs = [_norm_rows(x_ref[0, :, jj * d:(jj + 1) * d], m, g_ref, d) for jj in range(grp)]
    for jj, h in enumerate(hs):
        ot_ref[0, :, jj * d:(jj + 1) * d] = h
    hcat = jnp.concatenate(hs, axis=0)
    op_ref[0] = hcat
    gates = _gate_rows(hcat, w_ref, alog_ref, dtb_ref)
    for jj in range(grp):
        og_ref[0, :, jj * LANES:(jj + 1) * LANES] = gates[jj * h1:(jj + 1) * h1]


def _hnorm_lat_call(x, mod, norm_g, wg, alog, dtb):
    bsz, n_lat, d = x.shape
    h1 = n_lat // FFT_INNER
    grp = NORM_GROUP
    full = lambda a: pl.BlockSpec(a.shape, lambda b, t: (0,) * a.ndim)
    args = [x.reshape(bsz, h1, FFT_INNER * d), mod, norm_g.reshape(1, d), wg, alog, dtb]
    h_tok, h_perm, gates = pl.pallas_call(
        functools.partial(_hnorm_lat_kernel, d=d, grp=grp),
        grid=(bsz, FFT_INNER // grp),
        in_specs=[pl.BlockSpec((1, h1, grp * d), lambda b, t: (b, 0, t))] + [full(a) for a in args[1:]],
        out_specs=[pl.BlockSpec((1, h1, grp * d), lambda b, t: (b, 0, t)),
                   pl.BlockSpec((1, grp * h1, d), lambda b, t: (b, t, 0)),
                   pl.BlockSpec((1, h1, grp * LANES), lambda b, t: (b, 0, t))],
        out_shape=[jax.ShapeDtypeStruct((bsz, h1, FFT_INNER * d), BF16),
                   jax.ShapeDtypeStruct((bsz, n_lat, d), BF16),
                   jax.ShapeDtypeStruct((bsz, h1, FFT_INNER * LANES), F32)],
        compiler_params=_cparams(("arbitrary", "arbitrary")),
        name="hnorm",
    )(*args)
    return h_tok.reshape(bsz, n_lat, d), h_perm, gates.reshape(bsz, n_lat, LANES)


@functools.lru_cache(maxsize=None)
def _shift_matrix(rowlen, taps, blk):
    t = np.arange(blk)
    mats = []
    for j in range(taps):
        d = j - taps // 2
        if d != 0:
            ok = ((t + d) >= 0) & ((t + d) < blk) & (t // rowlen == (t + d) // rowlen)
            s = np.zeros((blk, blk), np.float32)
            s[t[ok], t[ok] + d] = 1.0
            mats.append(s)
    return np.concatenate(mats, axis=1)


def _shift_block(rowlen, taps):
    return max(rowlen, MXU_TILE // (taps - 1) // rowlen * rowlen)


def _convproj_kernel(h_ref, w_ref, cw_ref, cb_ref, sh_ref, *rest, tm, qk_norm):
    o_ref = rest[-1]
    j = pl.program_id(0)
    taps = cw_ref.shape[1]
    mid = taps // 2
    blk = sh_ref.shape[0]
    mblk = max(blk, MXU_TILE)
    if qk_norm:
        scale = jnp.where(j == 0, HEAD_DIM ** -0.5, 1.0)
        is_qk = j < 2
    for r in range(tm // mblk):
        zm = jnp.dot(h_ref[0, r * mblk:(r + 1) * mblk, :], w_ref[0], preferred_element_type=F32)
        for g in range(mblk // blk):
            rows = slice(r * mblk + g * blk, r * mblk + (g + 1) * blk)
            zb = zm[g * blk:(g + 1) * blk]
            side = jnp.concatenate(
                [(zb * cw_ref[0, jt:jt + 1, :]).astype(BF16) for jt in range(taps) if jt != mid], axis=0)
            y = zb * cw_ref[0, mid:mid + 1, :] + jnp.dot(sh_ref[...], side, preferred_element_type=F32)
            if not qk_norm:
                o_ref[0, 0, rows, :] = (y + cb_ref[0]).astype(o_ref.dtype)
                continue
            y = _silu(y)
            for hh in range(HEADS):
                yh = y[:, hh * HEAD_DIM:(hh + 1) * HEAD_DIM]
                nrm = jnp.sum(yh * yh, axis=-1, keepdims=True)
                f = jnp.where(is_qk, lax.rsqrt(nrm + EPS) * scale, 1.0)
                o_ref[0, 0, rows, hh * HEAD_DIM:(hh + 1) * HEAD_DIM] = (yh * f).astype(o_ref.dtype)


def _convproj_call(h, w3, cw3, cb3, *, rowlen, tm, tile0, out_rows, qk_norm, prev=None, name):
    bsz, rows, d = h.shape
    ntiles = rows // tm
    nw = w3.shape[2]
    taps = cw3.shape[1]
    blk = _shift_block(rowlen, taps)
    assert tm % max(blk, MXU_TILE) == 0 and max(blk, MXU_TILE) % blk == 0
    sh = jnp.asarray(_shift_matrix(rowlen, taps, blk)).astype(BF16)
    in_specs = [pl.BlockSpec((1, tm, d), lambda j, b, t: (b, t, 0)),
                pl.BlockSpec((1, d, nw), lambda j, b, t: (j, 0, 0)),
                pl.BlockSpec((1, cw3.shape[1], nw), lambda j, b, t: (j, 0, 0)),
                pl.BlockSpec((1, 1, nw), lambda j, b, t: (j, 0, 0)),
                pl.BlockSpec(sh.shape, lambda j, b, t: (0, 0))]
    args = [h, w3, cw3, cb3, sh]
    aliases = {}
    if prev is not None:
        in_specs.append(pl.BlockSpec(memory_space=pl.ANY))
        args.append(prev)
        aliases = {5: 0}
    return pl.pallas_call(
        functools.partial(_convproj_kernel, tm=tm, qk_norm=qk_norm),
        grid=(3, bsz, ntiles),
        in_specs=in_specs,
        out_specs=pl.BlockSpec((1, 1, tm, nw), lambda j, b, t: (j, b, t + tile0, 0)),
        out_shape=jax.ShapeDtypeStruct((3, bsz, out_rows, nw), BF16),
        input_output_aliases=aliases,
        compiler_params=_cparams(("arbitrary", "arbitrary", "arbitrary")),
        name=name,
    )(*args)


def _hyproj_kernel(h_ref, w_ref, cw_ref, cb_ref, o_ref):
    n2s, n1s, d = h_ref.shape[1:]
    taps = cw_ref.shape[1]
    hrows = h_ref[0].reshape(n2s * n1s, d)
    step = min(MXU_TILE, n2s * n1s)
    z = jnp.concatenate([jnp.dot(hrows[r:r + step], w_ref[0], preferred_element_type=F32)
                         for r in range(0, n2s * n1s, step)], axis=0)
    y = cb_ref[0] + z * cw_ref[0, taps // 2:taps // 2 + 1, :]
    for jt in range(taps):
        s = (jt - taps // 2) * n1s
        if s > 0:
            y = y + jnp.concatenate([z[s:], jnp.zeros((s, z.shape[1]), F32)], axis=0) * cw_ref[0, jt:jt + 1, :]
        elif s < 0:
            y = y + jnp.concatenate([jnp.zeros((-s, z.shape[1]), F32), z[:s]], axis=0) * cw_ref[0, jt:jt + 1, :]
    o_ref[0, 0] = y.astype(o_ref.dtype).reshape(n2s, n1s, y.shape[1])


def _hyproj_call(h_perm, w3, cw3, cb3):
    bsz, n_lat, d = h_perm.shape
    nw = w3.shape[2]
    h1 = n_lat // FFT_INNER
    n1s = min(HYPROJ_N1, h1)
    hv = h_perm.reshape(bsz, FFT_INNER, h1, d)
    return pl.pallas_call(
        _hyproj_kernel,
        grid=(3, bsz, FFT_INNER // GRID_W, h1 // n1s),
        in_specs=[pl.BlockSpec((1, GRID_W, n1s, d), lambda j, b, r, t: (b, r, t, 0)),
                  pl.BlockSpec((1, d, nw), lambda j, b, r, t: (j, 0, 0)),
                  pl.BlockSpec((1, cw3.shape[1], nw), lambda j, b, r, t: (j, 0, 0)),
                  pl.BlockSpec((1, 1, nw), lambda j, b, r, t: (j, 0, 0))],
        out_specs=pl.BlockSpec((1, 1, GRID_W, n1s, nw), lambda j, b, r, t: (j, b, r, t, 0)),
        out_shape=jax.ShapeDtypeStruct((3, bsz, FFT_INNER, h1, nw), BF16),
        compiler_params=_cparams(("arbitrary",) * 4),
        name="hyproj",
    )(hv, w3, cw3, cb3)


def _gproj_kernel(h_ref, w_ref, o_ref):
    j = pl.program_id(0)
    for r in range(h_ref.shape[1] // MXU_TILE):
        rows = slice(r * MXU_TILE, (r + 1) * MXU_TILE)
        z = jnp.dot(h_ref[0, rows, :], w_ref[0], preferred_element_type=F32)
        s = jax.nn.sigmoid(z)
        o_ref[0, 0, rows, :] = jnp.where(j < 2, z * s, s).astype(BF16)


def _gproj_call(h, w4, n_lat):
    bsz, _, d = h.shape
    nw = w4.shape[2]
    tm = PROJ_TILE if n_lat % PROJ_TILE == 0 else 256
    return pl.pallas_call(
        _gproj_kernel,
        grid=(4, bsz, n_lat // tm),
        in_specs=[pl.BlockSpec((1, tm, d), lambda j, b, t: (b, t, 0)),
                  pl.BlockSpec((1, d, nw), lambda j, b, t: (j, 0, 0))],
        out_specs=pl.BlockSpec((1, 1, tm, nw), lambda j, b, t: (j, b, t, 0)),
        out_shape=jax.ShapeDtypeStruct((4, bsz, n_lat, nw), BF16),
        compiler_params=_cparams(("arbitrary", "arbitrary", "arbitrary")),
        name="gproj",
    )(h, w4)


def _unit_tri_inverse(ls, lowers, filler):
    c = ls[0].shape[0]
    ri = lax.broadcasted_iota(jnp.int32, (c, c), 0)
    ci = lax.broadcasted_iota(jnp.int32, (c, c), 1)
    same = lambda n: (ri // n) == (ci // n)
    eye = jnp.where(ri == ci, 1.0, 0.0)
    in2 = same(2)
    ts = [eye - jnp.where(in2, l, 0.0) for l in ls]
    n = 2
    nlevels = c.bit_length() - 2
    level = 0
    while n < c:
        blk = same(2 * n) & jnp.logical_not(same(n))
        offs = [jnp.where(blk, l, 0.0).astype(BF16) for l in ls]
        if n % SUBLANES:
            prods = [_bdot(t, off) for t, off in zip(ts, offs)]
            filler(level, nlevels)
            ts = [t - _bdot(p, t) for p, t in zip(prods, ts)]
        else:
            def moving(t, lower):
                first = n if lower else 0
                return jnp.concatenate([t[r:r + n] for r in range(first, c, 2 * n)], axis=0)

            def merged(t, upd, lower):
                pieces, k = [], 0
                for r in range(0, c, n):
                    if ((r // n) % 2 == 1) == lower:
                        pieces.append(upd[k * n:(k + 1) * n])
                        k += 1
                    else:
                        pieces.append(t[r:r + n])
                return jnp.concatenate(pieces, axis=0)

            rows = [moving(t, lo) for t, lo in zip(ts, lowers)]
            prods = [_bdot(r, off) for r, off in zip(rows, offs)]
            filler(level, nlevels)
            upds = [r - _bdot(p, t) for r, p, t in zip(rows, prods, ts)]
            ts = [merged(t, u, lo) for t, u, lo in zip(ts, upds, lowers)]
        n *= 2
        level += 1
    return ts


_NT = (((1,), (1,)), ((), ()))
_TN = (((0,), (0,)), ((), ()))


def _prep_kernel(qkv_ref, g_ref, gt_ref, tall_ref, wide_ref, el_ref, *, bsz):
    c = CHUNK
    ri = lax.broadcasted_iota(jnp.int32, (c, c), 0)
    ci = lax.broadcasted_iota(jnp.int32, (c, c), 1)
    tri_l = jnp.where(ri >= ci, 1.0, 0.0)
    tri_u = jnp.where(ri <= ci, 1.0, 0.0)
    incl = (ri >= ci, ri <= ci)
    strict = (ri > ci, ri < ci)
    hdot = functools.partial(jnp.dot, precision=HIGHEST, preferred_element_type=F32)

    gs = [g_ref[b] for b in range(bsz)]
    gts = [gt_ref[b, 0] for b in range(bsz)]
    gcol = [(hdot(tri_l, g), hdot(tri_u, g)) for g in gs]
    grow = [(hdot(gt, tri_u), hdot(gt, tri_l)) for gt in gts]
    glast = [(gcol[b][0][c - 1:c], gcol[b][1][0:1]) for b in range(bsz)]
    el_ref[...] = jnp.zeros_like(el_ref)
    for d in range(2):
        for b in range(bsz):
            el_ref[d * bsz + b:d * bsz + b + 1, :] = jnp.exp(glast[b][d])

    bh = [(b, hh) for b in range(bsz) for hh in range(HEADS)]
    hs = lambda hh: slice(hh * HEAD_DIM, (hh + 1) * HEAD_DIM)
    ks = [qkv_ref[1, b, :, hs(hh)] for b, hh in bh]
    qs = [qkv_ref[0, b, :, hs(hh)] for b, hh in bh]
    kk = [lax.dot_general(k, k, _NT, preferred_element_type=F32) for k in ks]
    qk = [lax.dot_general(q, k, _NT, preferred_element_type=F32) for q, k in zip(qs, ks)]
    inst = [(d, b, hh, i) for d in range(2) for i, (b, hh) in enumerate(bh)]
    decs, betas, gcs = [], [], []
    for d, b, hh, _ in inst:
        lg = 2 * HEADS + d * HEADS + hh
        gcs.append(gcol[b][d][:, lg:lg + 1])
        betas.append(gs[b][:, d * HEADS + hh:d * HEADS + hh + 1])
        decs.append(jnp.exp(jnp.where(incl[d], gcs[-1] - grow[b][d][lg:lg + 1, :], -jnp.inf)))
    ls = [jnp.where(strict[d], kk[i] * beta * dec, 0.0) for (d, _, _, i), beta, dec in zip(inst, betas, decs)]
    def scaled_operands(part, nparts):
        for (d, b, hh, i), dec, beta, gc in list(zip(inst, decs, betas, gcs))[part::nparts]:
            lg = 2 * HEADS + d * HEADS + hh
            eg = jnp.exp(gc)
            k = ks[i].astype(F32)
            tall_ref[d, b, hh, :, c:2 * c] = (qk[i] * dec).astype(BF16)
            wide_ref[d, b, hh, 0:c, :] = (qs[i].astype(F32) * eg).astype(BF16)
            wide_ref[d, b, hh, c:2 * c, :] = (k * (beta * eg)).astype(BF16)
            wide_ref[d, b, hh, 2 * c:3 * c, :] = (qkv_ref[2, b, :, hs(hh)].astype(F32) * beta).astype(BF16)
            wide_ref[d, b, hh, 3 * c:4 * c, :] = (k * jnp.exp(glast[b][d][:, lg:lg + 1] - gc)).astype(BF16)

    ts = _unit_tri_inverse(ls, [d == 0 for d, _, _, _ in inst], scaled_operands)
    for (d, b, hh, _), t in zip(inst, ts):
        tall_ref[d, b, hh, :, 0:c] = t.astype(BF16)


def _prep_call(qkv, gates, gates_t):
    _, bsz, lt, dn = qkv.shape
    nc = lt // CHUNK
    tall = (2, bsz, HEADS, CHUNK, 2 * CHUNK)
    wide = (2, bsz, HEADS, 4 * CHUNK, HEAD_DIM)
    return pl.pallas_call(
        functools.partial(_prep_kernel, bsz=bsz),
        grid=(nc,),
        in_specs=[pl.BlockSpec((3, bsz, CHUNK, dn), lambda i: (0, 0, i, 0)),
                  pl.BlockSpec((bsz, CHUNK, LANES), lambda i: (0, i, 0)),
                  pl.BlockSpec((bsz, 1, 4 * HEADS, CHUNK), lambda i: (0, i, 0, 0))],
        out_specs=[pl.BlockSpec((None,) + tall, lambda i: (i, 0, 0, 0, 0, 0)),
                   pl.BlockSpec((None,) + wide, lambda i: (i, 0, 0, 0, 0, 0)),
                   pl.BlockSpec((None, SUBLANES, LANES), lambda i: (i, 0, 0))],
        out_shape=[jax.ShapeDtypeStruct((nc,) + tall, BF16),
                   jax.ShapeDtypeStruct((nc,) + wide, BF16),
                   jax.ShapeDtypeStruct((nc, SUBLANES, LANES), F32)],
        compiler_params=_cparams(("arbitrary",)),
        name="prep",
    )(qkv, gates, gates_t)


def _scan_kernel(tf_ref, tb_ref, wf_ref, wb_ref, ef_ref, eb_ref, of_ref, ob_ref, s_ref, *, bsz):
    i = pl.program_id(0)

    @pl.when(i == 0)
    def _():
        s_ref[...] = jnp.zeros_like(s_ref)

    c = CHUNK
    dirs = ((tf_ref, wf_ref, ef_ref, of_ref), (tb_ref, wb_ref, eb_ref, ob_ref))
    inst = [(d, b, hh) for d in range(2) for b in range(bsz) for hh in range(HEADS)]
    s_old = [s_ref[d, b, hh] for d, b, hh in inst]
    a1 = [jnp.dot(dirs[d][1][b, hh, 0:2 * c, :], s.astype(BF16), preferred_element_type=F32)
          for (d, b, hh), s in zip(inst, s_old)]
    v_new = [jnp.dot(dirs[d][0][b, hh, :, 0:c], (dirs[d][1][b, hh, 2 * c:3 * c, :].astype(F32) - a[c:]).astype(BF16),
                     preferred_element_type=F32).astype(BF16) for (d, b, hh), a in zip(inst, a1)]
    for (d, b, hh), a, vn, s in zip(inst, a1, v_new, s_old):
        lg = 2 * HEADS + d * HEADS + hh
        o = a[:c] + jnp.dot(dirs[d][0][b, hh, :, c:2 * c], vn, preferred_element_type=F32)
        dirs[d][3][b, :, hh * HEAD_DIM:(hh + 1) * HEAD_DIM] = o.astype(BF16)
        el = dirs[d][2][d * bsz + b:d * bsz + b + 1, lg:lg + 1]
        s_ref[d, b, hh] = s * el + lax.dot_general(dirs[d][1][b, hh, 3 * c:4 * c, :], vn, _TN,
                                                   preferred_element_type=F32)


def _scan_call(tall, wide, el, n_lat, n_ctx):
    nc, _, bsz = tall.shape[:3]
    dn = HEADS * HEAD_DIM
    ncx = n_lat // CHUNK
    ncc = n_ctx // CHUNK
    assert nc == ncx + ncc
    cf = lambda i: jnp.where(i < ncc, ncx + i, i - ncc)
    cb = lambda i: nc - 1 - i

    dspec = lambda arr, fn, d: pl.BlockSpec((None, None) + arr.shape[2:], lambda i: (fn(i), d, 0, 0, 0, 0))
    espec = lambda fn: pl.BlockSpec((None, SUBLANES, LANES), lambda i: (fn(i), 0, 0))
    ospec = lambda fn: pl.BlockSpec((bsz, CHUNK, dn), lambda i: (0, fn(i), 0))
    return pl.pallas_call(
        functools.partial(_scan_kernel, bsz=bsz),
        grid=(nc,),
        in_specs=[dspec(tall, cf, 0), dspec(tall, cb, 1), dspec(wide, cf, 0), dspec(wide, cb, 1),
                  espec(cf), espec(cb)],
        out_specs=[ospec(lambda i: jnp.maximum(i - ncc, 0)), ospec(lambda i: jnp.minimum(nc - 1 - i, ncx - 1))],
        out_shape=[jax.ShapeDtypeStruct((bsz, n_lat, dn), BF16)] * 2,
        scratch_shapes=[pltpu.VMEM((2, bsz, HEADS, HEAD_DIM, HEAD_DIM), F32)],
        compiler_params=_cparams(("arbitrary",)),
        name="scan",
    )(tall, tall, wide, wide, el, el)


def _hidden_kernel(pe_ref, w1_ref, b1_ref, w2_ref, b2_ref, w3_ref, b3_ref, f_ref, o_ref):
    f = f_ref[...]
    dot = functools.partial(jnp.dot, precision=HIGHEST, preferred_element_type=F32)
    a = jnp.sin(f * (dot(pe_ref[...], w1_ref[...]) + b1_ref[...]))
    a = jnp.sin(f * (dot(a, w2_ref[...]) + b2_ref[...]))
    o_ref[...] = jnp.sin(f * (dot(a, w3_ref[...]) + b3_ref[...]))


def _hidden_call(pe, w1p, b1, w2, b2, w3, b3, freq):
    rows, pw = pe.shape
    fw = 2 * w2.shape[0]
    tr = 512
    full = lambda a: pl.BlockSpec(a.shape, lambda t: (0,) * a.ndim)
    two = lambda a: jnp.tile(a.reshape(1, -1), (1, 2))
    diag2 = lambda a: jnp.kron(jnp.eye(2, dtype=a.dtype), a)
    args = (diag2(w1p), two(b1), diag2(w2), two(b2), diag2(w3), two(b3), two(freq))
    return pl.pallas_call(
        _hidden_kernel,
        grid=(rows // tr,),
        in_specs=[pl.BlockSpec((tr, pw), lambda t: (t, 0))] + [full(a) for a in args],
        out_specs=pl.BlockSpec((tr, fw), lambda t: (t, 0)),
        out_shape=jax.ShapeDtypeStruct((rows, fw), F32),
        compiler_params=_cparams(("arbitrary",)),
        name="hidden",
    )(pe, *args)


def _ftime_kernel(a_ref, t_ref, w_ref, dl_ref, h_ref, ss_ref, *, n_lat, tr):
    t = pl.program_id(0)
    h = _bdot(a_ref[...], w_ref[...]) * jnp.exp(-t_ref[...] * dl_ref[...])
    row = lax.broadcasted_iota(jnp.int32, (tr, 1), 0) + t * tr
    h = jnp.where(row == n_lat, 0.0, h)
    h_ref[...] = h.astype(h_ref.dtype)

    @pl.when(t == 0)
    def _():
        ss_ref[...] = jnp.zeros_like(ss_ref)

    ss_ref[...] += jnp.sum(h * h, axis=0, keepdims=True)


def _ftime_call(a3, tcol, wout2, deltas2, n_lat):
    rows = tcol.shape[0]
    fw = a3.shape[1]
    ncol = wout2.shape[2]
    tr = 512
    side = lambda t: (t * tr) // n_lat
    return pl.pallas_call(
        functools.partial(_ftime_kernel, n_lat=n_lat, tr=tr),
        grid=(rows // tr,),
        in_specs=[pl.BlockSpec((tr, fw), lambda t: (t % (n_lat // tr), 0)),
                  pl.BlockSpec((tr, 1), lambda t: (t, 0)),
                  pl.BlockSpec((None, fw, ncol), lambda t: (side(t), 0, 0)),
                  pl.BlockSpec((1, ncol), lambda t: (0, 0))],
        out_specs=[pl.BlockSpec((tr, ncol), lambda t: (t, 0)),
                   pl.BlockSpec((1, ncol), lambda t: (0, 0))],
        out_shape=[jax.ShapeDtypeStruct((rows, ncol), BF16), jax.ShapeDtypeStruct((1, ncol), F32)],
        compiler_params=_cparams(("arbitrary",)),
        name="ftime",
    )(a3, tcol, wout2, deltas2)


@functools.lru_cache(maxsize=None)
def _fft_tables(n_lat):
    n = 2 * n_lat
    n1t = n // FFT_INNER
    h1 = n1t // 2
    n2 = np.arange(FFT_INNER, dtype=np.int64)[:, None, None]
    k1 = np.arange(n1t, dtype=np.int64)[None, :, None]
    n1 = np.arange(h1, dtype=np.int64)[None, None, :]
    ang = (-2.0 * np.pi / n) * ((k1 * (FFT_INNER * n1 + n2)) % n)
    er, ei = np.cos(ang), np.sin(ang)
    e1 = np.concatenate([np.concatenate([er, -ei], axis=2), np.concatenate([ei, er], axis=2)], axis=1)
    e4 = np.transpose(e1, (0, 2, 1))
    sgn = np.where(np.arange(n1t) % 2 == 0, 1.0, -1.0)[None, :, None]
    ef = np.concatenate([np.concatenate([er, sgn * er], axis=2), np.concatenate([ei, sgn * ei], axis=2)], axis=1)
    a = np.arange(FFT_INNER, dtype=np.int64)
    ang2 = (-2.0 * np.pi / FFT_INNER) * ((a[:, None] * a[None, :]) % FFT_INNER)
    gr, gi = np.cos(ang2), np.sin(ang2)
    g2 = np.block([[gr, -gi], [gi, gr]])
    f32 = lambda x: np.ascontiguousarray(x, dtype=np.float32)
    return f32(e1), f32(e4), f32(ef), f32(g2), f32(g2.T)


def _hold(p, j, first, last, nsteps):
    active = (p >= first) & (p <= last) & (lax.rem(p - first, 2) == 0)
    return jnp.where(active, j, jnp.where(p < first, 0, nsteps - 1))


def _slab_read(x_ref, rows):
    return jnp.concatenate([x_ref[c, rows, :] for c in range(x_ref.shape[0])], axis=1)


def _slab_write(x_ref, rows, val):
    for c in range(x_ref.shape[0]):
        x_ref[c, rows, :] = val[:, c * LANES:(c + 1) * LANES]


def _spectrum_steps(x_ref, g2_ref, k1s, n1t, stride):
    cols = [jnp.concatenate([_slab_read(x_ref, pl.ds(k1, FFT_INNER, stride=stride)),
                             _slab_read(x_ref, pl.ds(n1t + k1, FFT_INNER, stride=stride))], axis=0).astype(BF16)
            for k1 in k1s]
    return [jnp.dot(g2_ref[...], c, preferred_element_type=F32) for c in cols]


def _ffft_kernel(h_ref, ef_ref, g2_ref, ss_ref, o_ref, x_ref, *, n1t, grp, kgrp, stride, inv_n):
    p = pl.program_id(1)
    j = pl.program_id(2)

    @pl.when(p == 0)
    def _():
        for j0 in range(0, grp, FFT_INTERLEAVE):
            jjs = list(range(j0, min(j0 + FFT_INTERLEAVE, grp)))
            us = [jnp.concatenate([h_ref[0, jj], h_ref[1, jj]], axis=0) for jj in jjs]
            slabs = [_bdot(ef_ref[jj], u) for jj, u in zip(jjs, us)]
            for jj, slab in zip(jjs, slabs):
                base = pl.multiple_of((j * grp + jj) * stride, SUBLANES)
                _slab_write(x_ref, pl.ds(base, 2 * n1t), slab)

    @pl.when(p == 1)
    def _():
        scale = lax.rsqrt(ss_ref[...] + EPS) * inv_n
        for j0 in range(0, kgrp, FFT_INTERLEAVE):
            jjs = list(range(j0, min(j0 + FFT_INTERLEAVE, kgrp)))
            xfs = _spectrum_steps(x_ref, g2_ref, [j * kgrp + jj for jj in jjs], n1t, stride)
            for jj, xf in zip(jjs, xfs):
                o_ref[jj] = (xf * scale).astype(BF16)


def _fft_groups(n1t, ns):
    ns = min(ns, n1t)
    assert FFT_INNER % ns == 0 and n1t % ns == 0
    return ns, FFT_INNER // ns, n1t // ns


def _ffft_call(hfull, ssq, ef, g2, n_lat):
    ncol = hfull.shape[1]
    n1t = 2 * n_lat // FFT_INNER
    h1 = n1t // 2
    ct = FFT_CHANNELS
    ns, grp, kgrp = _fft_groups(n1t, FILTER_FFT_STEPS)
    stride = 2 * n1t + SLAB_PAD
    hv = hfull.reshape(2, FFT_INNER, h1, ncol)
    nct = ncol // ct
    cpo = nct // 2
    return pl.pallas_call(
        functools.partial(_ffft_kernel, n1t=n1t, grp=grp, kgrp=kgrp, stride=stride, inv_n=1.0 / (2 * n_lat)),
        grid=(nct, 2, ns),
        in_specs=[pl.BlockSpec((2, grp, h1, ct), lambda c, p, j: (0, _hold(p, j, 0, 0, ns), 0, c)),
                  pl.BlockSpec((grp, 2 * n1t, n1t), lambda c, p, j: (_hold(p, j, 0, 0, ns), 0, 0)),
                  pl.BlockSpec((2 * FFT_INNER, 2 * FFT_INNER), lambda c, p, j: (0, 0)),
                  pl.BlockSpec((1, ct), lambda c, p, j: (0, c))],
        out_specs=pl.BlockSpec((None, kgrp, 2 * FFT_INNER, ct),
                               lambda c, p, j: (c // cpo, _hold(p, j, 1, 1, ns), 0, c % cpo)),
        out_shape=jax.ShapeDtypeStruct((2, n1t, 2 * FFT_INNER, ncol // 2), BF16),
        scratch_shapes=[pltpu.VMEM((ct // LANES, FFT_INNER * stride, LANES), F32)],
        compiler_params=_cparams(("arbitrary", "arbitrary", "arbitrary")),
        name="ffft",
    )(hv, ef, g2, ssq)


def _hyena_kernel(v_ref, x1_ref, x2_ref, hf_ref, e1_ref, e4_ref, g2_ref, g3_ref, bias_ref,
                  o_ref, x_ref, z_ref, *, n1t, grp, kgrp, stride):
    p = pl.program_id(1)
    j = pl.program_id(2)
    h1 = n1t // 2

    def gather(ref, jj):
        return jnp.concatenate([ref[0, jj], ref[1, jj]], axis=0).astype(F32)

    def slab_base(jj):
        return pl.multiple_of((j * grp + jj) * stride, SUBLANES)

    def time_rows(jjs):
        slabs = [_slab_read(x_ref, pl.ds(slab_base(jj), 2 * n1t)).astype(BF16) for jj in jjs]
        return [jnp.dot(e4_ref[jj], s, preferred_element_type=F32) for jj, s in zip(jjs, slabs)]

    def write_slabs(jjs, us):
        slabs = [jnp.dot(e1_ref[jj], u, preferred_element_type=F32) for jj, u in zip(jjs, us)]
        for jj, slab in zip(jjs, slabs):
            _slab_write(x_ref, pl.ds(slab_base(jj), 2 * n1t), slab)

    def chunks(n):
        return [list(range(j0, min(j0 + FFT_INTERLEAVE, n))) for j0 in range(0, n, FFT_INTERLEAVE)]

    @pl.when(p == 0)
    def _():
        for jjs in chunks(grp):
            write_slabs(jjs, [gather(v_ref, jj).astype(BF16) for jj in jjs])

    @pl.when((p == 1) | (p == 3))
    def _():
        for jjs in chunks(kgrp):
            k1s = [j * kgrp + jj for jj in jjs]
            xfs = _spectrum_steps(x_ref, g2_ref, k1s, n1t, stride)
            ys = []
            for jj, xf in zip(jjs, xfs):
                hf = hf_ref[jj].astype(F32)
                xr, xi = xf[:FFT_INNER], xf[FFT_INNER:]
                hr, hi = hf[:FFT_INNER], hf[FFT_INNER:]
                ys.append(jnp.concatenate([xr * hr - xi * hi, xr * hi + xi * hr], axis=0).astype(BF16))
            bbs = [jnp.dot(g3_ref[...], y, preferred_element_type=F32) for y in ys]
            for k1, bb in zip(k1s, bbs):
                _slab_write(x_ref, pl.ds(k1, FFT_INNER, stride=stride), bb[:FFT_INNER])
                _slab_write(x_ref, pl.ds(n1t + k1, FFT_INNER, stride=stride), bb[FFT_INNER:])

    @pl.when(p == 2)
    def _():
        for jjs in chunks(grp):
            z1s = [(gather(x1_ref, jj) * (y + gather(v_ref, jj) * bias_ref[0:1, :])).astype(BF16)
                   for jj, y in zip(jjs, time_rows(jjs))]
            for jj, z1 in zip(jjs, z1s):
                z_ref[j * grp + jj] = z1
            write_slabs(jjs, z1s)

    @pl.when(p == 4)
    def _():
        for jjs in chunks(grp):
            for jj, y in zip(jjs, time_rows(jjs)):
                out = gather(x2_ref, jj) * (y + z_ref[j * grp + jj].astype(F32) * bias_ref[1:2, :])
                o_ref[0, jj] = out[:h1].astype(o_ref.dtype)
                o_ref[1, jj] = out[h1:].astype(o_ref.dtype)


def _hyena_call(hv, hf, e1, e4, g2, g3, bias):
    _, bsz, _, h1, ch = hv.shape
    assert bsz == 2, "the two batch rows are packed as the real and imaginary parts of one FFT"
    n1t = 2 * h1
    ct = FFT_CHANNELS
    ns, grp, kgrp = _fft_groups(n1t, CONV_FFT_STEPS)
    stride = 2 * n1t + SLAB_PAD

    def hspec(which, first, last):
        return pl.BlockSpec((None, bsz, grp, h1, ct),
                            lambda c, p, j: (which, 0, _hold(p, j, first, last, ns), 0, c))

    def hf_map(c, p, j):
        order = jnp.where(p >= 2, 1, 0)
        step = jnp.where((p == 1) | (p == 3), j, jnp.where((p == 0) | (p == 2), 0, ns - 1))
        return (order, step, 0, c)

    return pl.pallas_call(
        functools.partial(_hyena_kernel, n1t=n1t, grp=grp, kgrp=kgrp, stride=stride),
        grid=(ch // ct, 5, ns),
        in_specs=[hspec(0, 0, 2), hspec(1, 2, 2), hspec(2, 4, 4),
                  pl.BlockSpec((None, kgrp, 2 * FFT_INNER, ct), hf_map),
                  pl.BlockSpec((grp, 2 * n1t, n1t), lambda c, p, j: (_hold(p, j, 0, 2, ns), 0, 0)),
                  pl.BlockSpec((grp, n1t, 2 * n1t), lambda c, p, j: (_hold(p, j, 2, 4, ns), 0, 0)),
                  pl.BlockSpec((2 * FFT_INNER, 2 * FFT_INNER), lambda c, p, j: (0, 0)),
                  pl.BlockSpec((2 * FFT_INNER, 2 * FFT_INNER), lambda c, p, j: (0, 0)),
                  pl.BlockSpec((2, ct), lambda c, p, j: (0, c))],
        out_specs=pl.BlockSpec((bsz, grp, h1, ct), lambda c, p, j: (0, _hold(p, j, 4, 4, ns), 0, c)),
        out_shape=jax.ShapeDtypeStruct((bsz, FFT_INNER, h1, ch), BF16),
        scratch_shapes=[pltpu.VMEM((ct // LANES, FFT_INNER * stride, LANES), F32),
                        pltpu.VMEM((FFT_INNER, n1t, ct), BF16)],
        compiler_params=_cparams(("arbitrary", "arbitrary", "arbitrary"), VMEM_LIMIT_CONV),
        name="hyena",
    )(hv, hv, hv, hf, e1, e4, g2, g3, bias)


def _merge_kernel(of_ref, ob_ref, gp_ref, hy_ref, x_ref, mod_ref, dng_ref, wpa_ref, wpb_ref, wo_ref,
                  fg_ref, o_ref, *, d, grp):
    b = pl.program_id(0)
    h1 = x_ref.shape[1]
    per = max(1, min(grp, MXU_TILE // h1))
    blocks = [(slice(j0 * h1, (j0 + per) * h1), range(j0, j0 + per)) for j0 in range(0, grp, per)]
    f32 = lambda ref, *idx: ref[idx].astype(F32)
    tok = lambda ref, jjs, w: jnp.concatenate([f32(ref, 0, slice(None), slice(jj * w, (jj + 1) * w)) for jj in jjs],
                                              axis=0)

    def head_norm(rows, jjs):
        o = tok(of_ref, jjs, d) + tok(ob_ref, jjs, d)
        za = f32(gp_ref, 0, 0, rows)
        parts = []
        for hh in range(HEADS):
            hs = slice(hh * HEAD_DIM, (hh + 1) * HEAD_DIM)
            oh = o[:, hs]
            ms = jnp.mean(oh * oh, axis=-1, keepdims=True)
            parts.append((oh * lax.rsqrt(ms + EPS) * dng_ref[...] * za[:, hs]).astype(BF16))
        return jnp.concatenate(parts, axis=1)

    o_a = [head_norm(rows, jjs) for rows, jjs in blocks]
    y_a = [jnp.dot(a, wpa_ref[...], preferred_element_type=F32) for a in o_a]
    o_b = [(f32(hy_ref, 0, rows) * f32(gp_ref, 1, 0, rows)).astype(BF16) for rows, _ in blocks]
    y_b = [jnp.dot(a, wpb_ref[...], preferred_element_type=F32) for a in o_b]
    m = [(f32(gp_ref, 2, 0, rows) * ya + f32(gp_ref, 3, 0, rows) * yb).astype(BF16)
         for (rows, _), ya, yb in zip(blocks, y_a, y_b)]
    y = [jnp.dot(a, wo_ref[...], preferred_element_type=F32) for a in m]
    gate = mod_ref[pl.ds(b, 1), :][:, 2 * d:3 * d]
    for (_, jjs), yr in zip(blocks, y):
        xn = tok(x_ref, jjs, d) + gate * yr
        res = xn * lax.rsqrt(jnp.mean(xn * xn, axis=-1, keepdims=True) + EPS) * fg_ref[...]
        for k, jj in enumerate(jjs):
            o_ref[0, :, jj * d:(jj + 1) * d] = res[k * h1:(k + 1) * h1]


def _merge_call(o_f, o_b, gp, hy, x, mod, dn_norm_g, w_pa, w_pb, w_out, final_g):
    bsz, n_lat, d = x.shape
    h1 = n_lat // FFT_INNER
    grp = NORM_GROUP
    wfull = lambda a: pl.BlockSpec(a.shape, lambda b, t: (0,) * a.ndim)
    lanes = lambda a: a.reshape(bsz, h1, FFT_INNER * d)
    tok = pl.BlockSpec((1, h1, grp * d), lambda b, t: (b, 0, t))
    out = pl.pallas_call(
        functools.partial(_merge_kernel, d=d, grp=grp),
        grid=(bsz, FFT_INNER // grp),
        in_specs=[tok, tok,
                  pl.BlockSpec((4, 1, grp * h1, d), lambda b, t: (0, b, t, 0)),
                  pl.BlockSpec((1, grp * h1, d), lambda b, t: (b, t, 0)),
                  tok, wfull(mod),
                  pl.BlockSpec((1, HEAD_DIM), lambda b, t: (0, 0)),
                  wfull(w_pa), wfull(w_pb), wfull(w_out),
                  pl.BlockSpec((1, d), lambda b, t: (0, 0))],
        out_specs=tok,
        out_shape=jax.ShapeDtypeStruct((bsz, h1, FFT_INNER * d), F32),
        compiler_params=_cparams(("arbitrary", "arbitrary")),
        name="merge",
    )(lanes(o_f), lanes(o_b), gp, hy.reshape(bsz, n_lat, d), lanes(x), mod, dn_norm_g.reshape(1, HEAD_DIM),
      w_pa, w_pb, w_out, final_g.reshape(1, d))
    return out.reshape(bsz, n_lat, d)


def _position_features(n_lat):
    h1 = n_lat // FFT_INNER
    m = (np.arange(FFT_INNER)[:, None] + FFT_INNER * np.arange(h1)[None, :]).reshape(-1)
    lag = np.concatenate([m, n_lat - m]).astype(np.float64)
    lag[n_lat] = 0.0
    bands = (HY_EMB - 1) // 2
    t = (lag / (n_lat - 1))[:, None]
    wpos = (2.0 * math.pi / n_lat) * lag[:, None]
    fb = np.linspace(1e-4, bands - 1, bands)[None, :]
    pe = np.concatenate([t, np.cos(fb * wpos), -np.sin(fb * wpos)], axis=1)
    pe_pad = np.zeros((2 * n_lat, LANES), np.float32)
    pe_pad[:, :HY_EMB] = pe
    return pe_pad, np.ascontiguousarray(t, dtype=np.float32)


def kernel(x, c, ctx, c_ctx, w_mod, b_mod, norm_g, w_in, dn_conv_w, dn_a_log, dn_dt_bias, dn_norm_g,
           hy_conv_w, hy_conv_b, hy_f_w1, hy_f_b1, hy_f_w2, hy_f_b2, hy_f_w3, hy_f_b3, hy_f_wout,
           hy_f_freq, hy_bias, w_pa, w_pb, w_out, final_g):
    bsz, n_lat, d = x.shape
    n_ctx = ctx.shape[1]
    assert w_mod.shape[0] == 1, "single layer: the context stream is only read through its scan states"
    dn = HEADS * HEAD_DIM
    hy = hy_bias.shape[-1]
    assert n_lat % 256 == 0 and n_ctx % 256 == 0 and dn == d and hy == d

    cvec = jnp.zeros((SUBLANES, d), F32).at[:bsz].set(c).at[bsz].set(c_ctx)
    mod = _mod_call(cvec, w_mod[0], b_mod[0])

    w = w_in[0].astype(BF16)
    o_qkv, o_gate, o_hy, o_gp = 0, 4 * dn, 4 * dn + 4 * HEADS, 4 * dn + 4 * HEADS + 3 * hy
    col3 = lambda a, off, n: jnp.transpose(a[:, off:off + 3 * n].reshape(a.shape[0], 3, n), (1, 0, 2))
    lt = n_lat + n_ctx
    tm = PROJ_TILE if n_lat % PROJ_TILE == 0 else 256

    wg = jnp.zeros((d, LANES), BF16).at[:, :4 * HEADS].set(w[:, o_gate:o_gate + 4 * HEADS])
    lane_pad = lambda a: jnp.zeros((1, LANES), F32).at[0, 2 * HEADS:4 * HEADS].set(a.reshape(-1))
    gate_args = (wg, lane_pad(dn_a_log[0]), lane_pad(dn_dt_bias[0]))
    h_tok, h_perm, g_lat = _hnorm_lat_call(x, mod, norm_g[0], *gate_args)
    h_ctx, g_ctx = _hnorm_ctx_call(ctx, mod, norm_g[0], *gate_args, mod_row=bsz)

    w_qkv, cw_qkv, no_bias = col3(w, o_qkv, dn), col3(dn_conv_w[0], 0, dn), jnp.zeros((3, 1, dn), F32)
    qkv = _convproj_call(h_tok, w_qkv, cw_qkv, no_bias, rowlen=GRID_W, tm=tm, tile0=0, out_rows=lt, qk_norm=True,
                         name="qkv")
    qkv = _convproj_call(h_ctx, w_qkv, cw_qkv, no_bias, rowlen=n_ctx, tm=n_ctx, tile0=n_lat // n_ctx, out_rows=lt,
                         qk_norm=True, prev=qkv, name="qkv_ctx")
    gates = jnp.concatenate([g_lat, g_ctx], axis=1)
    gates_t = jnp.transpose(gates[:, :, :4 * HEADS].reshape(bsz, lt // CHUNK, CHUNK, 4 * HEADS), (0, 1, 3, 2))
    tall, wide, el = _prep_call(qkv, gates, gates_t)
    o_f, o_b = _scan_call(tall, wide, el, n_lat, n_ctx)

    hyp = _hyproj_call(h_perm, col3(w, o_hy, hy), col3(hy_conv_w[0], 0, hy), hy_conv_b[0].reshape(3, 1, hy))
    w4 = jnp.stack([w[:, 3 * dn:4 * dn]] + [w[:, o_gp + i * d:o_gp + (i + 1) * d] for i in range(3)])
    gp = _gproj_call(h_perm, w4, n_lat)

    pe, tcol = _position_features(n_lat)
    w1p = jnp.zeros((LANES, hy_f_w1.shape[-1]), F32).at[:HY_EMB].set(hy_f_w1[0])
    pe2 = jnp.asarray(np.concatenate([pe[:n_lat], pe[n_lat:]], axis=1))
    a3 = _hidden_call(pe2, w1p, hy_f_b1[0], hy_f_w2[0], hy_f_b2[0], hy_f_w3[0], hy_f_b3[0], hy_f_freq[0])
    deltas = np.abs(np.linspace(math.log(HY_DECAY_TARGET) / HY_SLOW_DECAY,
                                math.log(HY_DECAY_TARGET) / HY_FAST_DECAY, hy)).astype(np.float32)
    deltas2 = jnp.asarray(np.tile(deltas, 2)[None, :])
    wside = hy_f_wout[0].reshape(-1, 2, 2 * hy).transpose(1, 0, 2).astype(BF16)
    zside = jnp.zeros_like(wside[0])
    wout2 = jnp.stack([jnp.concatenate([wside[0], zside]), jnp.concatenate([zside, wside[1]])])
    hfull, ssq = _ftime_call(a3, jnp.asarray(tcol), wout2, deltas2, n_lat)
    e1, e4, ef, g2, g3 = (jnp.asarray(t).astype(BF16) for t in _fft_tables(n_lat))
    hf = _ffft_call(hfull, ssq, ef, g2, n_lat)

    yh = _hyena_call(hyp, hf, e1, e4, g2, g3, hy_bias[0])

    return _merge_call(o_f, o_b, gp, yh, x, mod, dn_norm_g[0], w_pa[0].astype(BF16), w_pb[0].astype(BF16),
                       w_out[0].astype(BF16), final_g)
```

```python
import functools
import math

import numpy as np
import jax
import jax.numpy as jnp
from jax import lax
from jax.experimental import pallas as pl
from jax.experimental.pallas import tpu as pltpu

F32 = jnp.float32
BF16 = jnp.bfloat16
HIGHEST = lax.Precision.HIGHEST

EPS = 1e-6
HEADS = 8
HEAD_DIM = 128
CHUNK = 64
GRID_W = 64
HY_EMB = 33
HY_DECAY_TARGET = 1e-2
HY_FAST_DECAY = 0.3
HY_SLOW_DECAY = 1.5

LANES = 128
SUBLANES = 8
FFT_INNER = 128
CONV_FFT_STEPS = 8
FILTER_FFT_STEPS = 8
FFT_CHANNELS = 256
FFT_INTERLEAVE = 8
PROJ_TILE = 1024
NORM_TILE = 512
HYPROJ_N1 = 16
MXU_TILE = 256
SLAB_PAD = 8
VMEM_LIMIT = 56 * 1024 * 1024
VMEM_LIMIT_CONV = 62 * 1024 * 1024


def _cparams(sem, vmem=VMEM_LIMIT):
    return pltpu.CompilerParams(dimension_semantics=sem, vmem_limit_bytes=vmem)


def _bdot(a, b):
    return jnp.dot(a.astype(BF16), b.astype(BF16), preferred_element_type=F32)


def _silu(x):
    return x * jax.nn.sigmoid(x)


def _mod_kernel(c_ref, w_ref, b_ref, o_ref):
    s = _silu(c_ref[...])
    o_ref[...] = jnp.dot(s, w_ref[...], precision=HIGHEST, preferred_element_type=F32) + b_ref[...]


def _mod_call(cvec, w_mod, b_mod):
    rows, d = cvec.shape
    n = w_mod.shape[1]
    tn = 1024
    return pl.pallas_call(
        _mod_kernel,
        grid=(n // tn,),
        in_specs=[pl.BlockSpec((rows, d), lambda j: (0, 0)),
                  pl.BlockSpec((d, tn), lambda j: (0, j)),
                  pl.BlockSpec((1, tn), lambda j: (0, j))],
        out_specs=pl.BlockSpec((rows, tn), lambda j: (0, j)),
        out_shape=jax.ShapeDtypeStruct((rows, n), F32),
        compiler_params=_cparams(("arbitrary",)),
        name="mod",
    )(cvec, w_mod, b_mod.reshape(1, n))


def _norm_rows(x, m, g_ref, d):
    y = x * lax.rsqrt(jnp.mean(x * x, axis=-1, keepdims=True) + EPS) * g_ref[...]
    return (y * (1.0 + m[:, d:2 * d]) + m[:, 0:d]).astype(BF16)


def _gate_rows(h, w_ref, alog_ref, dtb_ref):
    z = jnp.dot(h, w_ref[...], preferred_element_type=F32)
    lane = lax.broadcasted_iota(jnp.int32, z.shape, 1)
    u = z + dtb_ref[...]
    softplus = jnp.maximum(u, 0.0) + jnp.log(1.0 + jnp.exp(-jnp.abs(u)))
    return jnp.where(lane < 2 * HEADS, jax.nn.sigmoid(z), -jnp.exp(alog_ref[...]) * softplus)


def _hnorm_kernel(tok_ref, mod_ref, g_ref, w_ref, alog_ref, dtb_ref, o_ref, og_ref, *, d, mod_row):
    row = pl.program_id(0) if mod_row is None else mod_row
    m = mod_ref[pl.ds(row, 1), :]
    h = _norm_rows(tok_ref[0], m, g_ref, d)
    o_ref[0] = h
    og_ref[0] = _gate_rows(h, w_ref, alog_ref, dtb_ref)


def _hnorm_call(tok, mod, norm_g, wg, alog, dtb, *, tm, mod_row=None, name):
    bsz, rows, d = tok.shape
    full = lambda a: pl.BlockSpec(a.shape, lambda b, t: (0,) * a.ndim)
    args = [tok, mod, norm_g.reshape(1, d), wg, alog, dtb]
    return pl.pallas_call(
        functools.partial(_hnorm_kernel, d=d, mod_row=mod_row),
        grid=(bsz, rows // tm),
        in_specs=[pl.BlockSpec((1, tm, d), lambda b, t: (b, t, 0))] + [full(a) for a in args[1:]],
        out_specs=[pl.BlockSpec((1, tm, d), lambda b, t: (b, t, 0)),
                   pl.BlockSpec((1, tm, LANES), lambda b, t: (b, t, 0))],
        out_shape=[jax.ShapeDtypeStruct((bsz, rows, d), BF16), jax.ShapeDtypeStruct((bsz, rows, LANES), F32)],
        compiler_params=_cparams(("arbitrary", "arbitrary")),
        name=name,
    )(*args)


@functools.lru_cache(maxsize=None)
def _shift_matrix(rowlen, taps, blk):
    t = np.arange(blk)
    mats = []
    for j in range(taps):
        d = j - taps // 2
        if d != 0:
            ok = ((t + d) >= 0) & ((t + d) < blk) & (t // rowlen == (t + d) // rowlen)
            s = np.zeros((blk, blk), np.float32)
            s[t[ok], t[ok] + d] = 1.0
            mats.append(s)
    return np.concatenate(mats, axis=1)


def _shift_block(rowlen, taps):
    return max(rowlen, MXU_TILE // (taps - 1) // rowlen * rowlen)


def _convproj_kernel(h_ref, w_ref, cw_ref, cb_ref, sh_ref, *rest, tm, qk_norm):
    o_ref = rest[-1]
    j = pl.program_id(0)
    taps = cw_ref.shape[1]
    mid = taps // 2
    blk = sh_ref.shape[0]
    mblk = max(blk, MXU_TILE)
    if qk_norm:
        scale = jnp.where(j == 0, HEAD_DIM ** -0.5, 1.0)
        is_qk = j < 2
    for r in range(tm // mblk):
        zm = jnp.dot(h_ref[0, r * mblk:(r + 1) * mblk, :], w_ref[0], preferred_element_type=F32)
        for g in range(mblk // blk):
            rows = slice(r * mblk + g * blk, r * mblk + (g + 1) * blk)
            zb = zm[g * blk:(g + 1) * blk]
            side = jnp.concatenate(
                [(zb * cw_ref[0, jt:jt + 1, :]).astype(BF16) for jt in range(taps) if jt != mid], axis=0)
            y = zb * cw_ref[0, mid:mid + 1, :] + jnp.dot(sh_ref[...], side, preferred_element_type=F32)
            if not qk_norm:
                o_ref[0, 0, rows, :] = (y + cb_ref[0]).astype(o_ref.dtype)
                continue
            y = _silu(y)
            for hh in range(HEADS):
                yh = y[:, hh * HEAD_DIM:(hh + 1) * HEAD_DIM]
                nrm = jnp.sum(yh * yh, axis=-1, keepdims=True)
                f = jnp.where(is_qk, lax.rsqrt(nrm + EPS) * scale, 1.0)
                o_ref[0, 0, rows, hh * HEAD_DIM:(hh + 1) * HEAD_DIM] = (yh * f).astype(o_ref.dtype)


def _convproj_call(h, w3, cw3, cb3, *, rowlen, tm, tile0, out_rows, qk_norm, prev=None, name):
    bsz, rows, d = h.shape
    ntiles = rows // tm
    nw = w3.shape[2]
    taps = cw3.shape[1]
    blk = _shift_block(rowlen, taps)
    assert tm % max(blk, MXU_TILE) == 0 and max(blk, MXU_TILE) % blk == 0
    sh = jnp.asarray(_shift_matrix(rowlen, taps, blk)).astype(BF16)
    in_specs = [pl.BlockSpec((1, tm, d), lambda j, b, t: (b, t, 0)),
                pl.BlockSpec((1, d, nw), lambda j, b, t: (j, 0, 0)),
                pl.BlockSpec((1, cw3.shape[1], nw), lambda j, b, t: (j, 0, 0)),
                pl.BlockSpec((1, 1, nw), lambda j, b, t: (j, 0, 0)),
                pl.BlockSpec(sh.shape, lambda j, b, t: (0, 0))]
    args = [h, w3, cw3, cb3, sh]
    aliases = {}
    if prev is not None:
        in_specs.append(pl.BlockSpec(memory_space=pl.ANY))
        args.append(prev)
        aliases = {5: 0}
    return pl.pallas_call(
        functools.partial(_convproj_kernel, tm=tm, qk_norm=qk_norm),
        grid=(3, bsz, ntiles),
        in_specs=in_specs,
        out_specs=pl.BlockSpec((1, 1, tm, nw), lambda j, b, t: (j, b, t + tile0, 0)),
        out_shape=jax.ShapeDtypeStruct((3, bsz, out_rows, nw), BF16),
        input_output_aliases=aliases,
        compiler_params=_cparams(("arbitrary", "arbitrary", "arbitrary")),
        name=name,
    )(*args)


def _hyproj_kernel(h_ref, w_ref, cw_ref, cb_ref, o_ref):
    n2s, n1s, d = h_ref.shape[1:]
    taps = cw_ref.shape[1]
    hrows = h_ref[0].reshape(n2s * n1s, d)
    step = min(MXU_TILE, n2s * n1s)
    z = jnp.concatenate([jnp.dot(hrows[r:r + step], w_ref[0], preferred_element_type=F32)
                         for r in range(0, n2s * n1s, step)], axis=0)
    y = cb_ref[0] + z * cw_ref[0, taps // 2:taps // 2 + 1, :]
    for jt in range(taps):
        s = (jt - taps // 2) * n1s
        if s > 0:
            y = y + jnp.concatenate([z[s:], jnp.zeros((s, z.shape[1]), F32)], axis=0) * cw_ref[0, jt:jt + 1, :]
        elif s < 0:
            y = y + jnp.concatenate([jnp.zeros((-s, z.shape[1]), F32), z[:s]], axis=0) * cw_ref[0, jt:jt + 1, :]
    o_ref[0, 0] = y.astype(o_ref.dtype).reshape(n2s, n1s, y.shape[1])


def _hyproj_call(h_perm, w3, cw3, cb3):
    bsz, n_lat, d = h_perm.shape
    nw = w3.shape[2]
    h1 = n_lat // FFT_INNER
    n1s = min(HYPROJ_N1, h1)
    hv = h_perm.reshape(bsz, FFT_INNER, h1, d)
    return pl.pallas_call(
        _hyproj_kernel,
        grid=(3, bsz, FFT_INNER // GRID_W, h1 // n1s),
        in_specs=[pl.BlockSpec((1, GRID_W, n1s, d), lambda j, b, r, t: (b, r, t, 0)),
                  pl.BlockSpec((1, d, nw), lambda j, b, r, t: (j, 0, 0)),
                  pl.BlockSpec((1, cw3.shape[1], nw), lambda j, b, r, t: (j, 0, 0)),
                  pl.BlockSpec((1, 1, nw), lambda j, b, r, t: (j, 0, 0))],
        out_specs=pl.BlockSpec((1, 1, GRID_W, n1s, nw), lambda j, b, r, t: (j, b, r, t, 0)),
        out_shape=jax.ShapeDtypeStruct((3, bsz, FFT_INNER, h1, nw), BF16),
        compiler_params=_cparams(("arbitrary",) * 4),
        name="hyproj",
    )(hv, w3, cw3, cb3)


def _gproj_kernel(h_ref, w_ref, o_ref):
    j = pl.program_id(0)
    for r in range(h_ref.shape[1] // MXU_TILE):
        rows = slice(r * MXU_TILE, (r + 1) * MXU_TILE)
        z = jnp.dot(h_ref[0, rows, :], w_ref[0], preferred_element_type=F32)
        s = jax.nn.sigmoid(z)
        o_ref[0, 0, rows, :] = jnp.where(j < 2, z * s, s).astype(BF16)


def _gproj_call(h, w4, n_lat):
    bsz, _, d = h.shape
    nw = w4.shape[2]
    tm = PROJ_TILE if n_lat % PROJ_TILE == 0 else 256
    return pl.pallas_call(
        _gproj_kernel,
        grid=(4, bsz, n_lat // tm),
        in_specs=[pl.BlockSpec((1, tm, d), lambda j, b, t: (b, t, 0)),
                  pl.BlockSpec((1, d, nw), lambda j, b, t: (j, 0, 0))],
        out_specs=pl.BlockSpec((1, 1, tm, nw), lambda j, b, t: (j, b, t, 0)),
        out_shape=jax.ShapeDtypeStruct((4, bsz, n_lat, nw), BF16),
        compiler_params=_cparams(("arbitrary", "arbitrary", "arbitrary")),
        name="gproj",
    )(h, w4)


def _unit_tri_inverse(ls, lowers, filler):
    c = ls[0].shape[0]
    ri = lax.broadcasted_iota(jnp.int32, (c, c), 0)
    ci = lax.broadcasted_iota(jnp.int32, (c, c), 1)
    same = lambda n: (ri // n) == (ci // n)
    eye = jnp.where(ri == ci, 1.0, 0.0)
    in2 = same(2)
    ts = [eye - jnp.where(in2, l, 0.0) for l in ls]
    n = 2
    nlevels = c.bit_length() - 2
    level = 0
    while n < c:
        blk = same(2 * n) & jnp.logical_not(same(n))
        offs = [jnp.where(blk, l, 0.0).astype(BF16) for l in ls]
        if n % SUBLANES:
            prods = [_bdot(t, off) for t, off in zip(ts, offs)]
            filler(level, nlevels)
            ts = [t - _bdot(p, t) for p, t in zip(prods, ts)]
        else:
            def moving(t, lower):
                first = n if lower else 0
                return jnp.concatenate([t[r:r + n] for r in range(first, c, 2 * n)], axis=0)

            def merged(t, upd, lower):
                pieces, k = [], 0
                for r in range(0, c, n):
                    if ((r // n) % 2 == 1) == lower:
                        pieces.append(upd[k * n:(k + 1) * n])
                        k += 1
                    else:
                        pieces.append(t[r:r + n])
                return jnp.concatenate(pieces, axis=0)

            rows = [moving(t, lo) for t, lo in zip(ts, lowers)]
            prods = [_bdot(r, off) for r, off in zip(rows, offs)]
            filler(level, nlevels)
            upds = [r - _bdot(p, t) for r, p, t in zip(rows, prods, ts)]
            ts = [merged(t, u, lo) for t, u, lo in zip(ts, upds, lowers)]
        n *= 2
        level += 1
    return ts


_NT = (((1,), (1,)), ((), ()))
_TN = (((0,), (0,)), ((), ()))


def _prep_kernel(qkv_ref, g_ref, gt_ref, tall_ref, wide_ref, el_ref, *, bsz):
    c = CHUNK
    ri = lax.broadcasted_iota(jnp.int32, (c, c), 0)
    ci = lax.broadcasted_iota(jnp.int32, (c, c), 1)
    tri_l = jnp.where(ri >= ci, 1.0, 0.0)
    tri_u = jnp.where(ri <= ci, 1.0, 0.0)
    incl = (ri >= ci, ri <= ci)
    strict = (ri > ci, ri < ci)
    hdot = functools.partial(jnp.dot, precision=HIGHEST, preferred_element_type=F32)

    gs = [g_ref[b] for b in range(bsz)]
    gts = [gt_ref[b, 0] for b in range(bsz)]
    gcol = [(hdot(tri_l, g), hdot(tri_u, g)) for g in gs]
    grow = [(hdot(gt, tri_u), hdot(gt, tri_l)) for gt in gts]
    glast = [(gcol[b][0][c - 1:c], gcol[b][1][0:1]) for b in range(bsz)]
    el_ref[...] = jnp.zeros_like(el_ref)
    for d in range(2):
        for b in range(bsz):
            el_ref[d * bsz + b:d * bsz + b + 1, :] = jnp.exp(glast[b][d])

    bh = [(b, hh) for b in range(bsz) for hh in range(HEADS)]
    hs = lambda hh: slice(hh * HEAD_DIM, (hh + 1) * HEAD_DIM)
    ks = [qkv_ref[1, b, :, hs(hh)] for b, hh in bh]
    qs = [qkv_ref[0, b, :, hs(hh)] for b, hh in bh]
    kk = [lax.dot_general(k, k, _NT, preferred_element_type=F32) for k in ks]
    qk = [lax.dot_general(q, k, _NT, preferred_element_type=F32) for q, k in zip(qs, ks)]
    inst = [(d, b, hh, i) for d in range(2) for i, (b, hh) in enumerate(bh)]
    decs, betas, gcs = [], [], []
    for d, b, hh, _ in inst:
        lg = 2 * HEADS + d * HEADS + hh
        gcs.append(gcol[b][d][:, lg:lg + 1])
        betas.append(gs[b][:, d * HEADS + hh:d * HEADS + hh + 1])
        decs.append(jnp.exp(jnp.where(incl[d], gcs[-1] - grow[b][d][lg:lg + 1, :], -jnp.inf)))
    ls = [jnp.where(strict[d], kk[i] * beta * dec, 0.0) for (d, _, _, i), beta, dec in zip(inst, betas, decs)]
    def scaled_operands(part, nparts):
        for (d, b, hh, i), dec, beta, gc in list(zip(inst, decs, betas, gcs))[part::nparts]:
            lg = 2 * HEADS + d * HEADS + hh
            eg = jnp.exp(gc)
            k = ks[i].astype(F32)
            tall_ref[d, b, hh, :, c:2 * c] = (qk[i] * dec).astype(BF16)
            wide_ref[d, b, hh, 0:c, :] = (qs[i].astype(F32) * eg).astype(BF16)
            wide_ref[d, b, hh, c:2 * c, :] = (k * (beta * eg)).astype(BF16)
            wide_ref[d, b, hh, 2 * c:3 * c, :] = (qkv_ref[2, b, :, hs(hh)].astype(F32) * beta).astype(BF16)
            wide_ref[d, b, hh, 3 * c:4 * c, :] = (k * jnp.exp(glast[b][d][:, lg:lg + 1] - gc)).astype(BF16)

    ts = _unit_tri_inverse(ls, [d == 0 for d, _, _, _ in inst], scaled_operands)
    for (d, b, hh, _), t in zip(inst, ts):
        tall_ref[d, b, hh, :, 0:c] = t.astype(BF16)


def _prep_call(qkv, gates, gates_t):
    _, bsz, lt, dn = qkv.shape
    nc = lt // CHUNK
    tall = (2, bsz, HEADS, CHUNK, 2 * CHUNK)
    wide = (2, bsz, HEADS, 4 * CHUNK, HEAD_DIM)
    return pl.pallas_call(
        functools.partial(_prep_kernel, bsz=bsz),
        grid=(nc,),
        in_specs=[pl.BlockSpec((3, bsz, CHUNK, dn), lambda i: (0, 0, i, 0)),
                  pl.BlockSpec((bsz, CHUNK, LANES), lambda i: (0, i, 0)),
                  pl.BlockSpec((bsz, 1, 4 * HEADS, CHUNK), lambda i: (0, i, 0, 0))],
        out_specs=[pl.BlockSpec((None,) + tall, lambda i: (i, 0, 0, 0, 0, 0)),
                   pl.BlockSpec((None,) + wide, lambda i: (i, 0, 0, 0, 0, 0)),
                   pl.BlockSpec((None, SUBLANES, LANES), lambda i: (i, 0, 0))],
        out_shape=[jax.ShapeDtypeStruct((nc,) + tall, BF16),
                   jax.ShapeDtypeStruct((nc,) + wide, BF16),
                   jax.ShapeDtypeStruct((nc, SUBLANES, LANES), F32)],
        compiler_params=_cparams(("arbitrary",)),
        name="prep",
    )(qkv, gates, gates_t)


def _scan_kernel(tf_ref, tb_ref, wf_ref, wb_ref, ef_ref, eb_ref, of_ref, ob_ref, s_ref, *, bsz):
    i = pl.program_id(0)

    @pl.when(i == 0)
    def _():
        s_ref[...] = jnp.zeros_like(s_ref)

    c = CHUNK
    dirs = ((tf_ref, wf_ref, ef_ref, of_ref), (tb_ref, wb_ref, eb_ref, ob_ref))
    inst = [(d, b, hh) for d in range(2) for b in range(bsz) for hh in range(HEADS)]
    s_old = [s_ref[d, b, hh] for d, b, hh in inst]
    a1 = [jnp.dot(dirs[d][1][b, hh, 0:2 * c, :], s.astype(BF16), preferred_element_type=F32)
          for (d, b, hh), s in zip(inst, s_old)]
    v_new = [jnp.dot(dirs[d][0][b, hh, :, 0:c], (dirs[d][1][b, hh, 2 * c:3 * c, :].astype(F32) - a[c:]).astype(BF16),
                     preferred_element_type=F32).astype(BF16) for (d, b, hh), a in zip(inst, a1)]
    for (d, b, hh), a, vn, s in zip(inst, a1, v_new, s_old):
        lg = 2 * HEADS + d * HEADS + hh
        o = a[:c] + jnp.dot(dirs[d][0][b, hh, :, c:2 * c], vn, preferred_element_type=F32)
        dirs[d][3][b, :, hh * HEAD_DIM:(hh + 1) * HEAD_DIM] = o.astype(BF16)
        el = dirs[d][2][d * bsz + b:d * bsz + b + 1, lg:lg + 1]
        s_ref[d, b, hh] = s * el + lax.dot_general(dirs[d][1][b, hh, 3 * c:4 * c, :], vn, _TN,
                                                   preferred_element_type=F32)


def _scan_call(tall, wide, el, n_lat, n_ctx):
    nc, _, bsz = tall.shape[:3]
    dn = HEADS * HEAD_DIM
    ncx = n_lat // CHUNK
    ncc = n_ctx // CHUNK
    assert nc == ncx + ncc
    cf = lambda i: jnp.where(i < ncc, ncx + i, i - ncc)
    cb = lambda i: nc - 1 - i

    dspec = lambda arr, fn, d: pl.BlockSpec((None, None) + arr.shape[2:], lambda i: (fn(i), d, 0, 0, 0, 0))
    espec = lambda fn: pl.BlockSpec((None, SUBLANES, LANES), lambda i: (fn(i), 0, 0))
    ospec = lambda fn: pl.BlockSpec((bsz, CHUNK, dn), lambda i: (0, fn(i), 0))
    return pl.pallas_call(
        functools.partial(_scan_kernel, bsz=bsz),
        grid=(nc,),
        in_specs=[dspec(tall, cf, 0), dspec(tall, cb, 1), dspec(wide, cf, 0), dspec(wide, cb, 1),
                  espec(cf), espec(cb)],
        out_specs=[ospec(lambda i: jnp.maximum(i - ncc, 0)), ospec(lambda i: jnp.minimum(nc - 1 - i, ncx - 1))],
        out_shape=[jax.ShapeDtypeStruct((bsz, n_lat, dn), BF16)] * 2,
        scratch_shapes=[pltpu.VMEM((2, bsz, HEADS, HEAD_DIM, HEAD_DIM), F32)],
        compiler_params=_cparams(("arbitrary",)),
        name="scan",
    )(tall, tall, wide, wide, el, el)


def _hidden_kernel(pe_ref, w1_ref, b1_ref, w2_ref, b2_ref, w3_ref, b3_ref, f_ref, o_ref):
    f = f_ref[...]
    dot = functools.partial(jnp.dot, precision=HIGHEST, preferred_element_type=F32)
    a = jnp.sin(f * (dot(pe_ref[...], w1_ref[...]) + b1_ref[...]))
    a = jnp.sin(f * (dot(a, w2_ref[...]) + b2_ref[...]))
    o_ref[...] = jnp.sin(f * (dot(a, w3_ref[...]) + b3_ref[...]))


def _hidden_call(pe, w1p, b1, w2, b2, w3, b3, freq):
    rows, pw = pe.shape
    fw = 2 * w2.shape[0]
    tr = 512
    full = lambda a: pl.BlockSpec(a.shape, lambda t: (0,) * a.ndim)
    two = lambda a: jnp.tile(a.reshape(1, -1), (1, 2))
    diag2 = lambda a: jnp.kron(jnp.eye(2, dtype=a.dtype), a)
    args = (diag2(w1p), two(b1), diag2(w2), two(b2), diag2(w3), two(b3), two(freq))
    return pl.pallas_call(
        _hidden_kernel,
        grid=(rows // tr,),
        in_specs=[pl.BlockSpec((tr, pw), lambda t: (t, 0))] + [full(a) for a in args],
        out_specs=pl.BlockSpec((tr, fw), lambda t: (t, 0)),
        out_shape=jax.ShapeDtypeStruct((rows, fw), F32),
        compiler_params=_cparams(("arbitrary",)),
        name="hidden",
    )(pe, *args)


def _ftime_kernel(a_ref, t_ref, w_ref, dl_ref, h_ref, ss_ref, *, n_lat, tr):
    t = pl.program_id(0)
    h = _bdot(a_ref[...], w_ref[...]) * jnp.exp(-t_ref[...] * dl_ref[...])
    row = lax.broadcasted_iota(jnp.int32, (tr, 1), 0) + t * tr
    h = jnp.where(row == n_lat, 0.0, h)
    h_ref[...] = h.astype(h_ref.dtype)

    @pl.when(t == 0)
    def _():
        ss_ref[...] = jnp.zeros_like(ss_ref)

    ss_ref[...] += jnp.sum(h * h, axis=0, keepdims=True)


def _ftime_call(a3, tcol, wout2, deltas2, n_lat):
    rows = tcol.shape[0]
    fw = a3.shape[1]
    ncol = wout2.shape[2]
    tr = 512
    side = lambda t: (t * tr) // n_lat
    return pl.pallas_call(
        functools.partial(_ftime_kernel, n_lat=n_lat, tr=tr),
        grid=(rows // tr,),
        in_specs=[pl.BlockSpec((tr, fw), lambda t: (t % (n_lat // tr), 0)),
                  pl.BlockSpec((tr, 1), lambda t: (t, 0)),
                  pl.BlockSpec((None, fw, ncol), lambda t: (side(t), 0, 0)),
                  pl.BlockSpec((1, ncol), lambda t: (0, 0))],
        out_specs=[pl.BlockSpec((tr, ncol), lambda t: (t, 0)),
                   pl.BlockSpec((1, ncol), lambda t: (0, 0))],
        out_shape=[jax.ShapeDtypeStruct((rows, ncol), BF16), jax.ShapeDtypeStruct((1, ncol), F32)],
        compiler_params=_cparams(("arbitrary",)),
        name="ftime",
    )(a3, tcol, wout2, deltas2)


@functools.lru_cache(maxsize=None)
def _fft_tables(n_lat):
    n = 2 * n_lat
    n1t = n // FFT_INNER
    h1 = n1t // 2
    n2 = np.arange(FFT_INNER, dtype=np.int64)[:, None, None]
    k1 = np.arange(n1t, dtype=np.int64)[None, :, None]
    n1 = np.arange(h1, dtype=np.int64)[None, None, :]
    ang = (-2.0 * np.pi / n) * ((k1 * (FFT_INNER * n1 + n2)) % n)
    er, ei = np.cos(ang), np.sin(ang)
    e1 = np.concatenate([np.concatenate([er, -ei], axis=2), np.concatenate([ei, er], axis=2)], axis=1)
    e4 = np.transpose(e1, (0, 2, 1))
    sgn = np.where(np.arange(n1t) % 2 == 0, 1.0, -1.0)[None, :, None]
    ef = np.concatenate([np.concatenate([er, sgn * er], axis=2), np.concatenate([ei, sgn * ei], axis=2)], axis=1)
    a = np.arange(FFT_INNER, dtype=np.int64)
    ang2 = (-2.0 * np.pi / FFT_INNER) * ((a[:, None] * a[None, :]) % FFT_INNER)
    gr, gi = np.cos(ang2), np.sin(ang2)
    g2 = np.block([[gr, -gi], [gi, gr]])
    f32 = lambda x: np.ascontiguousarray(x, dtype=np.float32)
    return f32(e1), f32(e4), f32(ef), f32(g2), f32(g2.T)


def _hold(p, j, first, last, nsteps):
    active = (p >= first) & (p <= last) & (lax.rem(p - first, 2) == 0)
    return jnp.where(active, j, jnp.where(p < first, 0, nsteps - 1))


def _slab_read(x_ref, rows):
    return jnp.concatenate([x_ref[c, rows, :] for c in range(x_ref.shape[0])], axis=1)


def _slab_write(x_ref, rows, val):
    for c in range(x_ref.shape[0]):
        x_ref[c, rows, :] = val[:, c * LANES:(c + 1) * LANES]


def _spectrum_steps(x_ref, g2_ref, k1s, n1t, stride):
    cols = [jnp.concatenate([_slab_read(x_ref, pl.ds(k1, FFT_INNER, stride=stride)),
                             _slab_read(x_ref, pl.ds(n1t + k1, FFT_INNER, stride=stride))], axis=0).astype(BF16)
            for k1 in k1s]
    return [jnp.dot(g2_ref[...], c, preferred_element_type=F32) for c in cols]


def _ffft_kernel(h_ref, ef_ref, g2_ref, ss_ref, o_ref, x_ref, *, n1t, grp, kgrp, stride, inv_n):
    p = pl.program_id(1)
    j = pl.program_id(2)

    @pl.when(p == 0)
    def _():
        for j0 in range(0, grp, FFT_INTERLEAVE):
            jjs = list(range(j0, min(j0 + FFT_INTERLEAVE, grp)))
            us = [jnp.concatenate([h_ref[0, jj], h_ref[1, jj]], axis=0) for jj in jjs]
            slabs = [_bdot(ef_ref[jj], u) for jj, u in zip(jjs, us)]
            for jj, slab in zip(jjs, slabs):
                base = pl.multiple_of((j * grp + jj) * stride, SUBLANES)
                _slab_write(x_ref, pl.ds(base, 2 * n1t), slab)

    @pl.when(p == 1)
    def _():
        scale = lax.rsqrt(ss_ref[...] + EPS) * inv_n
        for j0 in range(0, kgrp, FFT_INTERLEAVE):
            jjs = list(range(j0, min(j0 + FFT_INTERLEAVE, kgrp)))
            xfs = _spectrum_steps(x_ref, g2_ref, [j * kgrp + jj for jj in jjs], n1t, stride)
            for jj, xf in zip(jjs, xfs):
                o_ref[jj] = (xf * scale).astype(BF16)


def _fft_groups(n1t, ns):
    ns = min(ns, n1t)
    assert FFT_INNER % ns == 0 and n1t % ns == 0
    return ns, FFT_INNER // ns, n1t // ns


def _ffft_call(hfull, ssq, ef, g2, n_lat):
    ncol = hfull.shape[1]
    n1t = 2 * n_lat // FFT_INNER
    h1 = n1t // 2
    ct = FFT_CHANNELS
    ns, grp, kgrp = _fft_groups(n1t, FILTER_FFT_STEPS)
    stride = 2 * n1t + SLAB_PAD
    hv = hfull.reshape(2, FFT_INNER, h1, ncol)
    nct = ncol // ct
    cpo = nct // 2
    return pl.pallas_call(
        functools.partial(_ffft_kernel, n1t=n1t, grp=grp, kgrp=kgrp, stride=stride, inv_n=1.0 / (2 * n_lat)),
        grid=(nct, 2, ns),
        in_specs=[pl.BlockSpec((2, grp, h1, ct), lambda c, p, j: (0, _hold(p, j, 0, 0, ns), 0, c)),
                  pl.BlockSpec((grp, 2 * n1t, n1t), lambda c, p, j: (_hold(p, j, 0, 0, ns), 0, 0)),
                  pl.BlockSpec((2 * FFT_INNER, 2 * FFT_INNER), lambda c, p, j: (0, 0)),
                  pl.BlockSpec((1, ct), lambda c, p, j: (0, c))],
        out_specs=pl.BlockSpec((None, kgrp, 2 * FFT_INNER, ct),
                               lambda c, p, j: (c // cpo, _hold(p, j, 1, 1, ns), 0, c % cpo)),
        out_shape=jax.ShapeDtypeStruct((2, n1t, 2 * FFT_INNER, ncol // 2), BF16),
        scratch_shapes=[pltpu.VMEM((ct // LANES, FFT_INNER * stride, LANES), F32)],
        compiler_params=_cparams(("arbitrary", "arbitrary", "arbitrary")),
        name="ffft",
    )(hv, ef, g2, ssq)


def _hyena_kernel(v_ref, x1_ref, x2_ref, hf_ref, e1_ref, e4_ref, g2_ref, g3_ref, bias_ref,
                  o_ref, x_ref, z_ref, *, n1t, grp, kgrp, stride):
    p = pl.program_id(1)
    j = pl.program_id(2)
    h1 = n1t // 2

    def gather(ref, jj):
        return jnp.concatenate([ref[0, jj], ref[1, jj]], axis=0).astype(F32)

    def slab_base(jj):
        return pl.multiple_of((j * grp + jj) * stride, SUBLANES)

    def time_rows(jjs):
        slabs = [_slab_read(x_ref, pl.ds(slab_base(jj), 2 * n1t)).astype(BF16) for jj in jjs]
        return [jnp.dot(e4_ref[jj], s, preferred_element_type=F32) for jj, s in zip(jjs, slabs)]

    def write_slabs(jjs, us):
        slabs = [jnp.dot(e1_ref[jj], u, preferred_element_type=F32) for jj, u in zip(jjs, us)]
        for jj, slab in zip(jjs, slabs):
            _slab_write(x_ref, pl.ds(slab_base(jj), 2 * n1t), slab)

    def chunks(n):
        return [list(range(j0, min(j0 + FFT_INTERLEAVE, n))) for j0 in range(0, n, FFT_INTERLEAVE)]

    @pl.when(p == 0)
    def _():
        for jjs in chunks(grp):
            write_slabs(jjs, [gather(v_ref, jj).astype(BF16) for jj in jjs])

    @pl.when((p == 1) | (p == 3))
    def _():
        for jjs in chunks(kgrp):
            k1s = [j * kgrp + jj for jj in jjs]
            xfs = _spectrum_steps(x_ref, g2_ref, k1s, n1t, stride)
            ys = []
            for jj, xf in zip(jjs, xfs):
                hf = hf_ref[jj].astype(F32)
                xr, xi = xf[:FFT_INNER], xf[FFT_INNER:]
                hr, hi = hf[:FFT_INNER], hf[FFT_INNER:]
                ys.append(jnp.concatenate([xr * hr - xi * hi, xr * hi + xi * hr], axis=0).astype(BF16))
            bbs = [jnp.dot(g3_ref[...], y, preferred_element_type=F32) for y in ys]
            for k1, bb in zip(k1s, bbs):
                _slab_write(x_ref, pl.ds(k1, FFT_INNER, stride=stride), bb[:FFT_INNER])
                _slab_write(x_ref, pl.ds(n1t + k1, FFT_INNER, stride=stride), bb[FFT_INNER:])

    @pl.when(p == 2)
    def _():
        for jjs in chunks(grp):
            z1s = [(gather(x1_ref, jj) * (y + gather(v_ref, jj) * bias_ref[0:1, :])).astype(BF16)
                   for jj, y in zip(jjs, time_rows(jjs))]
            for jj, z1 in zip(jjs, z1s):
                z_ref[j * grp + jj] = z1
            write_slabs(jjs, z1s)

    @pl.when(p == 4)
    def _():
        for jjs in chunks(grp):
            for jj, y in zip(jjs, time_rows(jjs)):
                out = gather(x2_ref, jj) * (y + z_ref[j * grp + jj].astype(F32) * bias_ref[1:2, :])
                o_ref[0, jj] = out[:h1].astype(o_ref.dtype)
                o_ref[1, jj] = out[h1:].astype(o_ref.dtype)


def _hyena_call(hv, hf, e1, e4, g2, g3, bias):
    _, bsz, _, h1, ch = hv.shape
    assert bsz == 2, "the two batch rows are packed as the real and imaginary parts of one FFT"
    n1t = 2 * h1
    ct = FFT_CHANNELS
    ns, grp, kgrp = _fft_groups(n1t, CONV_FFT_STEPS)
    stride = 2 * n1t + SLAB_PAD

    def hspec(which, first, last):
        return pl.BlockSpec((None, bsz, grp, h1, ct),
                            lambda c, p, j: (which, 0, _hold(p, j, first, last, ns), 0, c))

    def hf_map(c, p, j):
        order = jnp.where(p >= 2, 1, 0)
        step = jnp.where((p == 1) | (p == 3), j, jnp.where((p == 0) | (p == 2), 0, ns - 1))
        return (order, step, 0, c)

    return pl.pallas_call(
        functools.partial(_hyena_kernel, n1t=n1t, grp=grp, kgrp=kgrp, stride=stride),
        grid=(ch // ct, 5, ns),
        in_specs=[hspec(0, 0, 2), hspec(1, 2, 2), hspec(2, 4, 4),
                  pl.BlockSpec((None, kgrp, 2 * FFT_INNER, ct), hf_map),
                  pl.BlockSpec((grp, 2 * n1t, n1t), lambda c, p, j: (_hold(p, j, 0, 2, ns), 0, 0)),
                  pl.BlockSpec((grp, n1t, 2 * n1t), lambda c, p, j: (_hold(p, j, 2, 4, ns), 0, 0)),
                  pl.BlockSpec((2 * FFT_INNER, 2 * FFT_INNER), lambda c, p, j: (0, 0)),
                  pl.BlockSpec((2 * FFT_INNER, 2 * FFT_INNER), lambda c, p, j: (0, 0)),
                  pl.BlockSpec((2, ct), lambda c, p, j: (0, c))],
        out_specs=pl.BlockSpec((bsz, grp, h1, ct), lambda c, p, j: (0, _hold(p, j, 4, 4, ns), 0, c)),
        out_shape=jax.ShapeDtypeStruct((bsz, FFT_INNER, h1, ch), BF16),
        scratch_shapes=[pltpu.VMEM((ct // LANES, FFT_INNER * stride, LANES), F32),
                        pltpu.VMEM((FFT_INNER, n1t, ct), BF16)],
        compiler_params=_cparams(("arbitrary", "arbitrary", "arbitrary"), VMEM_LIMIT_CONV),
        name="hyena",
    )(hv, hv, hv, hf, e1, e4, g2, g3, bias)


def _merge_kernel(of_ref, ob_ref, gp_ref, hy_ref, x_ref, mod_ref, dng_ref, wpa_ref, wpb_ref, wo_ref,
                  fg_ref, o_ref, *, d):
    b = pl.program_id(0)
    blocks = [slice(r * MXU_TILE, (r + 1) * MXU_TILE) for r in range(o_ref.shape[1] // MXU_TILE)]
    f32 = lambda ref, *idx: ref[idx].astype(F32)

    def head_norm(rows):
        o = f32(of_ref, 0, rows) + f32(ob_ref, 0, rows)
        za = f32(gp_ref, 0, 0, rows)
        parts = []
        for hh in range(HEADS):
            hs = slice(hh * HEAD_DIM, (hh + 1) * HEAD_DIM)
            oh = o[:, hs]
            ms = jnp.mean(oh * oh, axis=-1, keepdims=True)
            parts.append((oh * lax.rsqrt(ms + EPS) * dng_ref[...] * za[:, hs]).astype(BF16))
        return jnp.concatenate(parts, axis=1)

    o_a = [head_norm(rows) for rows in blocks]
    y_a = [jnp.dot(a, wpa_ref[...], preferred_element_type=F32) for a in o_a]
    o_b = [(f32(hy_ref, 0, rows) * f32(gp_ref, 1, 0, rows)).astype(BF16) for rows in blocks]
    y_b = [jnp.dot(a, wpb_ref[...], preferred_element_type=F32) for a in o_b]
    m = [(f32(gp_ref, 2, 0, rows) * ya + f32(gp_ref, 3, 0, rows) * yb).astype(BF16)
         for rows, ya, yb in zip(blocks, y_a, y_b)]
    y = [jnp.dot(a, wo_ref[...], preferred_element_type=F32) for a in m]
    gate = mod_ref[pl.ds(b, 1), :][:, 2 * d:3 * d]
    for rows, yr in zip(blocks, y):
        xn = x_ref[0, rows, :] + gate * yr
        o_ref[0, rows, :] = xn * lax.rsqrt(jnp.mean(xn * xn, axis=-1, keepdims=True) + EPS) * fg_ref[...]


def _merge_call(o_f, o_b, gp, hy, x, mod, dn_norm_g, w_pa, w_pb, w_out, final_g):
    bsz, n_lat, d = x.shape
    tm = NORM_TILE if n_lat % NORM_TILE == 0 else 256
    wfull = lambda a: pl.BlockSpec(a.shape, lambda b, t: (0,) * a.ndim)
    tok = pl.BlockSpec((1, tm, d), lambda b, t: (b, t, 0))
    return pl.pallas_call(
        functools.partial(_merge_kernel, d=d),
        grid=(bsz, n_lat // tm),
        in_specs=[tok, tok,
                  pl.BlockSpec((4, 1, tm, d), lambda b, t: (0, b, t, 0)),
                  tok, tok, wfull(mod),
                  pl.BlockSpec((1, HEAD_DIM), lambda b, t: (0, 0)),
                  wfull(w_pa), wfull(w_pb), wfull(w_out),
                  pl.BlockSpec((1, d), lambda b, t: (0, 0))],
        out_specs=tok,
        out_shape=jax.ShapeDtypeStruct((bsz, n_lat, d), F32),
        compiler_params=_cparams(("arbitrary", "arbitrary")),
        name="merge",
    )(o_f, o_b, gp, hy, x, mod, dn_norm_g.reshape(1, HEAD_DIM), w_pa, w_pb, w_out, final_g.reshape(1, d))


def _position_features(n_lat):
    h1 = n_lat // FFT_INNER
    m = (np.arange(FFT_INNER)[:, None] + FFT_INNER * np.arange(h1)[None, :]).reshape(-1)
    lag = np.concatenate([m, n_lat - m]).astype(np.float64)
    lag[n_lat] = 0.0
    bands = (HY_EMB - 1) // 2
    t = (lag / (n_lat - 1))[:, None]
    wpos = (2.0 * math.pi / n_lat) * lag[:, None]
    fb = np.linspace(1e-4, bands - 1, bands)[None, :]
    pe = np.concatenate([t, np.cos(fb * wpos), -np.sin(fb * wpos)], axis=1)
    pe_pad = np.zeros((2 * n_lat, LANES), np.float32)
    pe_pad[:, :HY_EMB] = pe
    return pe_pad, np.ascontiguousarray(t, dtype=np.float32)


def kernel(x, c, ctx, c_ctx, w_mod, b_mod, norm_g, w_in, dn_conv_w, dn_a_log, dn_dt_bias, dn_norm_g,
           hy_conv_w, hy_conv_b, hy_f_w1, hy_f_b1, hy_f_w2, hy_f_b2, hy_f_w3, hy_f_b3, hy_f_wout,
           hy_f_freq, hy_bias, w_pa, w_pb, w_out, final_g):
    bsz, n_lat, d = x.shape
    n_ctx = ctx.shape[1]
    assert w_mod.shape[0] == 1, "single layer: the context stream is only read through its scan states"
    dn = HEADS * HEAD_DIM
    hy = hy_bias.shape[-1]
    assert n_lat % 256 == 0 and n_ctx % 256 == 0 and dn == d and hy == d

    cvec = jnp.zeros((SUBLANES, d), F32).at[:bsz].set(c).at[bsz].set(c_ctx)
    mod = _mod_call(cvec, w_mod[0], b_mod[0])

    w = w_in[0].astype(BF16)
    o_qkv, o_gate, o_hy, o_gp = 0, 4 * dn, 4 * dn + 4 * HEADS, 4 * dn + 4 * HEADS + 3 * hy
    col3 = lambda a, off, n: jnp.transpose(a[:, off:off + 3 * n].reshape(a.shape[0], 3, n), (1, 0, 2))
    lt = n_lat + n_ctx
    tm = PROJ_TILE if n_lat % PROJ_TILE == 0 else 256

    wg = jnp.zeros((d, LANES), BF16).at[:, :4 * HEADS].set(w[:, o_gate:o_gate + 4 * HEADS])
    lane_pad = lambda a: jnp.zeros((1, LANES), F32).at[0, 2 * HEADS:4 * HEADS].set(a.reshape(-1))
    gate_args = (wg, lane_pad(dn_a_log[0]), lane_pad(dn_dt_bias[0]))
    h_tok, g_lat = _hnorm_call(x, mod, norm_g[0], *gate_args, tm=NORM_TILE, name="hnorm")
    h_ctx, g_ctx = _hnorm_call(ctx, mod, norm_g[0], *gate_args, tm=n_ctx, mod_row=bsz, name="hnorm_ctx")
    h1 = n_lat // FFT_INNER
    h_perm = jnp.transpose(h_tok.reshape(bsz, h1, FFT_INNER, d), (0, 2, 1, 3)).reshape(bsz, n_lat, d)

    w_qkv, cw_qkv, no_bias = col3(w, o_qkv, dn), col3(dn_conv_w[0], 0, dn), jnp.zeros((3, 1, dn), F32)
    qkv = _convproj_call(h_tok, w_qkv, cw_qkv, no_bias, rowlen=GRID_W, tm=tm, tile0=0, out_rows=lt, qk_norm=True,
                         name="qkv")
    qkv = _convproj_call(h_ctx, w_qkv, cw_qkv, no_bias, rowlen=n_ctx, tm=n_ctx, tile0=n_lat // n_ctx, out_rows=lt,
                         qk_norm=True, prev=qkv, name="qkv_ctx")
    gates = jnp.concatenate([g_lat, g_ctx], axis=1)
    gates_t = jnp.transpose(gates[:, :, :4 * HEADS].reshape(bsz, lt // CHUNK, CHUNK, 4 * HEADS), (0, 1, 3, 2))
    tall, wide, el = _prep_call(qkv, gates, gates_t)
    o_f, o_b = _scan_call(tall, wide, el, n_lat, n_ctx)

    hyp = _hyproj_call(h_perm, col3(w, o_hy, hy), col3(hy_conv_w[0], 0, hy), hy_conv_b[0].reshape(3, 1, hy))
    w4 = jnp.stack([w[:, 3 * dn:4 * dn]] + [w[:, o_gp + i * d:o_gp + (i + 1) * d] for i in range(3)])
    gp = _gproj_call(h_tok, w4, n_lat)

    pe, tcol = _position_features(n_lat)
    w1p = jnp.zeros((LANES, hy_f_w1.shape[-1]), F32).at[:HY_EMB].set(hy_f_w1[0])
    pe2 = jnp.asarray(np.concatenate([pe[:n_lat], pe[n_lat:]], axis=1))
    a3 = _hidden_call(pe2, w1p, hy_f_b1[0], hy_f_w2[0], hy_f_b2[0], hy_f_w3[0], hy_f_b3[0], hy_f_freq[0])
    deltas = np.abs(np.linspace(math.log(HY_DECAY_TARGET) / HY_SLOW_DECAY,
                                math.log(HY_DECAY_TARGET) / HY_FAST_DECAY, hy)).astype(np.float32)
    deltas2 = jnp.asarray(np.tile(deltas, 2)[None, :])
    wside = hy_f_wout[0].reshape(-1, 2, 2 * hy).transpose(1, 0, 2).astype(BF16)
    zside = jnp.zeros_like(wside[0])
    wout2 = jnp.stack([jnp.concatenate([wside[0], zside]), jnp.concatenate([zside, wside[1]])])
    hfull, ssq = _ftime_call(a3, jnp.asarray(tcol), wout2, deltas2, n_lat)
    e1, e4, ef, g2, g3 = (jnp.asarray(t).astype(BF16) for t in _fft_tables(n_lat))
    hf = _ffft_call(hfull, ssq, ef, g2, n_lat)

    yh = _hyena_call(hyp, hf, e1, e4, g2, g3, hy_bias[0])
    yh = jnp.transpose(yh, (0, 2, 1, 3)).reshape(bsz, n_lat, hy)

    return _merge_call(o_f, o_b, gp, yh, x, mod, dn_norm_g[0], w_pa[0].astype(BF16), w_pb[0].astype(BF16),
                       w_out[0].astype(BF16), final_g)
```

```python
import functools
import math

import numpy as np
import jax
import jax.numpy as jnp
from jax import lax
from jax.experimental import pallas as pl
from jax.experimental.pallas import tpu as pltpu

F32 = jnp.float32
BF16 = jnp.bfloat16
HIGHEST = lax.Precision.HIGHEST

EPS = 1e-6
HEADS = 8
HEAD_DIM = 128
CHUNK = 64
GRID_W = 64
HY_EMB = 33
HY_DECAY_TARGET = 1e-2
HY_FAST_DECAY = 0.3
HY_SLOW_DECAY = 1.5

LANES = 128
SUBLANES = 8
FFT_INNER = 128
CONV_FFT_STEPS = 8
FILTER_FFT_STEPS = 8
FFT_CHANNELS = 256
FFT_INTERLEAVE = 8
PROJ_TILE = 1024
NORM_TILE = 512
SCAN_CHUNKS = 2
HYPROJ_N1 = 16
MXU_TILE = 256
SLAB_PAD = 8
VMEM_LIMIT = 56 * 1024 * 1024
VMEM_LIMIT_CONV = 62 * 1024 * 1024


def _cparams(sem, vmem=VMEM_LIMIT):
    return pltpu.CompilerParams(dimension_semantics=sem, vmem_limit_bytes=vmem)


def _bdot(a, b):
    return jnp.dot(a.astype(BF16), b.astype(BF16), preferred_element_type=F32)


def _silu(x):
    return x * jax.nn.sigmoid(x)


def _mod_kernel(c_ref, w_ref, b_ref, o_ref):
    s = _silu(c_ref[...])
    o_ref[...] = jnp.dot(s, w_ref[...], precision=HIGHEST, preferred_element_type=F32) + b_ref[...]


def _mod_call(cvec, w_mod, b_mod):
    rows, d = cvec.shape
    n = w_mod.shape[1]
    tn = 1024
    return pl.pallas_call(
        _mod_kernel,
        grid=(n // tn,),
        in_specs=[pl.BlockSpec((rows, d), lambda j: (0, 0)),
                  pl.BlockSpec((d, tn), lambda j: (0, j)),
                  pl.BlockSpec((1, tn), lambda j: (0, j))],
        out_specs=pl.BlockSpec((rows, tn), lambda j: (0, j)),
        out_shape=jax.ShapeDtypeStruct((rows, n), F32),
        compiler_params=_cparams(("arbitrary",)),
        name="mod",
    )(cvec, w_mod, b_mod.reshape(1, n))


def _norm_rows(x, m, g_ref, d):
    y = x * lax.rsqrt(jnp.mean(x * x, axis=-1, keepdims=True) + EPS) * g_ref[...]
    return (y * (1.0 + m[:, d:2 * d]) + m[:, 0:d]).astype(BF16)


def _gate_rows(h, w_ref, alog_ref, dtb_ref):
    z = jnp.dot(h, w_ref[...], preferred_element_type=F32)
    lane = lax.broadcasted_iota(jnp.int32, z.shape, 1)
    u = z + dtb_ref[...]
    softplus = jnp.maximum(u, 0.0) + jnp.log(1.0 + jnp.exp(-jnp.abs(u)))
    return jnp.where(lane < 2 * HEADS, jax.nn.sigmoid(z), -jnp.exp(alog_ref[...]) * softplus)


def _hnorm_kernel(tok_ref, mod_ref, g_ref, w_ref, alog_ref, dtb_ref, *rest, d, mod_row):
    o_ref, og_ref = rest[-2:]
    row = pl.program_id(0) if mod_row is None else mod_row
    m = mod_ref[pl.ds(row, 1), :]
    h = _norm_rows(tok_ref[0], m, g_ref, d)
    o_ref[0] = h
    og_ref[0] = _gate_rows(h, w_ref, alog_ref, dtb_ref)


def _hnorm_call(tok, mod, norm_g, wg, alog, dtb, *, tm, gate_rows, gate_tile0, mod_row=None, prev_gates=None, name):
    bsz, rows, d = tok.shape
    full = lambda a: pl.BlockSpec(a.shape, lambda b, t: (0,) * a.ndim)
    args = [tok, mod, norm_g.reshape(1, d), wg, alog, dtb]
    in_specs = [pl.BlockSpec((1, tm, d), lambda b, t: (b, t, 0))] + [full(a) for a in args[1:]]
    aliases = {}
    if prev_gates is not None:
        in_specs.append(pl.BlockSpec(memory_space=pl.ANY))
        args.append(prev_gates)
        aliases = {6: 1}
    return pl.pallas_call(
        functools.partial(_hnorm_kernel, d=d, mod_row=mod_row),
        grid=(bsz, rows // tm),
        in_specs=in_specs,
        out_specs=[pl.BlockSpec((1, tm, d), lambda b, t: (b, t, 0)),
                   pl.BlockSpec((1, tm, LANES), lambda b, t: (b, t + gate_tile0, 0))],
        out_shape=[jax.ShapeDtypeStruct((bsz, rows, d), BF16), jax.ShapeDtypeStruct((bsz, gate_rows, LANES), F32)],
        input_output_aliases=aliases,
        compiler_params=_cparams(("arbitrary", "arbitrary")),
        name=name,
    )(*args)


@functools.lru_cache(maxsize=None)
def _shift_matrix(rowlen, taps, blk):
    t = np.arange(blk)
    mats = []
    for j in range(taps):
        d = j - taps // 2
        if d != 0:
            ok = ((t + d) >= 0) & ((t + d) < blk) & (t // rowlen == (t + d) // rowlen)
            s = np.zeros((blk, blk), np.float32)
            s[t[ok], t[ok] + d] = 1.0
            mats.append(s)
    return np.concatenate(mats, axis=1)


def _shift_block(rowlen, taps):
    return max(rowlen, MXU_TILE // (taps - 1) // rowlen * rowlen)


def _convproj_kernel(h_ref, w_ref, cw_ref, cb_ref, sh_ref, *rest, tm, qk_norm):
    o_ref = rest[-1]
    j = pl.program_id(0)
    taps = cw_ref.shape[1]
    mid = taps // 2
    blk = sh_ref.shape[0]
    mblk = max(blk, MXU_TILE)
    if qk_norm:
        na = jnp.where(j == 0, float(HEAD_DIM), jnp.where(j == 1, 1.0, 0.0))
        nb = jnp.where(j == 0, EPS * HEAD_DIM, jnp.where(j == 1, EPS, 1.0))
    for r in range(tm // mblk):
        zm = jnp.dot(h_ref[0, r * mblk:(r + 1) * mblk, :], w_ref[0], preferred_element_type=F32)
        for g in range(mblk // blk):
            rows = slice(r * mblk + g * blk, r * mblk + (g + 1) * blk)
            zb = zm[g * blk:(g + 1) * blk]
            side = jnp.concatenate(
                [(zb * cw_ref[0, jt:jt + 1, :]).astype(BF16) for jt in range(taps) if jt != mid], axis=0)
            y = zb * cw_ref[0, mid:mid + 1, :] + jnp.dot(sh_ref[...], side, preferred_element_type=F32)
            if not qk_norm:
                o_ref[0, 0, rows, :] = (y + cb_ref[0]).astype(o_ref.dtype)
                continue
            y = _silu(y)
            for hh in range(HEADS):
                yh = y[:, hh * HEAD_DIM:(hh + 1) * HEAD_DIM]
                nrm = jnp.sum(yh * yh, axis=-1, keepdims=True)
                o_ref[0, 0, rows, hh * HEAD_DIM:(hh + 1) * HEAD_DIM] = (
                    yh * lax.rsqrt(nrm * na + nb)).astype(o_ref.dtype)


def _convproj_call(h, w3, cw3, cb3, *, rowlen, tm, tile0, out_rows, qk_norm, prev=None, name):
    bsz, rows, d = h.shape
    ntiles = rows // tm
    nw = w3.shape[2]
    taps = cw3.shape[1]
    blk = _shift_block(rowlen, taps)
    assert tm % max(blk, MXU_TILE) == 0 and max(blk, MXU_TILE) % blk == 0
    sh = jnp.asarray(_shift_matrix(rowlen, taps, blk)).astype(BF16)
    in_specs = [pl.BlockSpec((1, tm, d), lambda j, b, t: (b, t, 0)),
                pl.BlockSpec((1, d, nw), lambda j, b, t: (j, 0, 0)),
                pl.BlockSpec((1, cw3.shape[1], nw), lambda j, b, t: (j, 0, 0)),
                pl.BlockSpec((1, 1, nw), lambda j, b, t: (j, 0, 0)),
                pl.BlockSpec(sh.shape, lambda j, b, t: (0, 0))]
    args = [h, w3, cw3, cb3, sh]
    aliases = {}
    if prev is not None:
        in_specs.append(pl.BlockSpec(memory_space=pl.ANY))
        args.append(prev)
        aliases = {5: 0}
    return pl.pallas_call(
        functools.partial(_convproj_kernel, tm=tm, qk_norm=qk_norm),
        grid=(3, bsz, ntiles),
        in_specs=in_specs,
        out_specs=pl.BlockSpec((1, 1, tm, nw), lambda j, b, t: (j, b, t + tile0, 0)),
        out_shape=jax.ShapeDtypeStruct((3, bsz, out_rows, nw), BF16),
        input_output_aliases=aliases,
        compiler_params=_cparams(("arbitrary", "arbitrary", "arbitrary")),
        name=name,
    )(*args)


def _hyproj_kernel(h_ref, w_ref, cw_ref, cb_ref, o_ref):
    n2s, n1s, d = h_ref.shape[1:]
    taps = cw_ref.shape[1]
    hrows = h_ref[0].reshape(n2s * n1s, d)
    step = min(MXU_TILE, n2s * n1s)
    z = jnp.concatenate([jnp.dot(hrows[r:r + step], w_ref[0], preferred_element_type=F32)
                         for r in range(0, n2s * n1s, step)], axis=0)
    y = cb_ref[0] + z * cw_ref[0, taps // 2:taps // 2 + 1, :]
    for jt in range(taps):
        s = (jt - taps // 2) * n1s
        if s > 0:
            y = y + jnp.concatenate([z[s:], jnp.zeros((s, z.shape[1]), F32)], axis=0) * cw_ref[0, jt:jt + 1, :]
        elif s < 0:
            y = y + jnp.concatenate([jnp.zeros((-s, z.shape[1]), F32), z[:s]], axis=0) * cw_ref[0, jt:jt + 1, :]
    o_ref[0, 0] = y.astype(o_ref.dtype).reshape(n2s, n1s, y.shape[1])


def _hyproj_call(h_perm, w3, cw3, cb3):
    bsz, n_lat, d = h_perm.shape
    nw = w3.shape[2]
    h1 = n_lat // FFT_INNER
    n1s = min(HYPROJ_N1, h1)
    hv = h_perm.reshape(bsz, FFT_INNER, h1, d)
    return pl.pallas_call(
        _hyproj_kernel,
        grid=(3, bsz, FFT_INNER // GRID_W, h1 // n1s),
        in_specs=[pl.BlockSpec((1, GRID_W, n1s, d), lambda j, b, r, t: (b, r, t, 0)),
                  pl.BlockSpec((1, d, nw), lambda j, b, r, t: (j, 0, 0)),
                  pl.BlockSpec((1, cw3.shape[1], nw), lambda j, b, r, t: (j, 0, 0)),
                  pl.BlockSpec((1, 1, nw), lambda j, b, r, t: (j, 0, 0))],
        out_specs=pl.BlockSpec((1, 1, GRID_W, n1s, nw), lambda j, b, r, t: (j, b, r, t, 0)),
        out_shape=jax.ShapeDtypeStruct((3, bsz, FFT_INNER, h1, nw), BF16),
        compiler_params=_cparams(("arbitrary",) * 4),
        name="hyproj",
    )(hv, w3, cw3, cb3)


def _gproj_kernel(h_ref, w_ref, o_ref):
    j = pl.program_id(0)
    for r in range(h_ref.shape[1] // MXU_TILE):
        rows = slice(r * MXU_TILE, (r + 1) * MXU_TILE)
        z = jnp.dot(h_ref[0, rows, :], w_ref[0], preferred_element_type=F32)
        s = jax.nn.sigmoid(z)
        o_ref[0, 0, rows, :] = jnp.where(j < 2, z * s, s).astype(BF16)


def _gproj_call(h, w4, n_lat):
    bsz, _, d = h.shape
    nw = w4.shape[2]
    tm = PROJ_TILE if n_lat % PROJ_TILE == 0 else 256
    return pl.pallas_call(
        _gproj_kernel,
        grid=(4, bsz, n_lat // tm),
        in_specs=[pl.BlockSpec((1, tm, d), lambda j, b, t: (b, t, 0)),
                  pl.BlockSpec((1, d, nw), lambda j, b, t: (j, 0, 0))],
        out_specs=pl.BlockSpec((1, 1, tm, nw), lambda j, b, t: (j, b, t, 0)),
        out_shape=jax.ShapeDtypeStruct((4, bsz, n_lat, nw), BF16),
        compiler_params=_cparams(("arbitrary", "arbitrary", "arbitrary")),
        name="gproj",
    )(h, w4)


def _unit_tri_inverse(ls, lowers, filler):
    c = ls[0].shape[0]
    ri = lax.broadcasted_iota(jnp.int32, (c, c), 0)
    ci = lax.broadcasted_iota(jnp.int32, (c, c), 1)
    same = lambda n: (ri // n) == (ci // n)
    eye = jnp.where(ri == ci, 1.0, 0.0)
    in2 = same(2)
    ts = [eye - jnp.where(in2, l, 0.0) for l in ls]
    n = 2
    nlevels = c.bit_length() - 2
    level = 0
    while n < c:
        blk = same(2 * n) & jnp.logical_not(same(n))
        offs = [jnp.where(blk, l, 0.0).astype(BF16) for l in ls]
        if n % SUBLANES:
            prods = [_bdot(t, off) for t, off in zip(ts, offs)]
            filler(level, nlevels)
            ts = [t - _bdot(p, t) for p, t in zip(prods, ts)]
        else:
            def moving(t, lower):
                first = n if lower else 0
                return jnp.concatenate([t[r:r + n] for r in range(first, c, 2 * n)], axis=0)

            def merged(t, upd, lower):
                pieces, k = [], 0
                for r in range(0, c, n):
                    if ((r // n) % 2 == 1) == lower:
                        pieces.append(upd[k * n:(k + 1) * n])
                        k += 1
                    else:
                        pieces.append(t[r:r + n])
                return jnp.concatenate(pieces, axis=0)

            rows = [moving(t, lo) for t, lo in zip(ts, lowers)]
            prods = [_bdot(r, off) for r, off in zip(rows, offs)]
            filler(level, nlevels)
            upds = [r - _bdot(p, t) for r, p, t in zip(rows, prods, ts)]
            ts = [merged(t, u, lo) for t, u, lo in zip(ts, upds, lowers)]
        n *= 2
        level += 1
    return ts


_NT = (((1,), (1,)), ((), ()))
_TN = (((0,), (0,)), ((), ()))


def _prep_kernel(qkv_ref, g_ref, gt_ref, tall_ref, wide_ref, el_ref, *, bsz):
    c = CHUNK
    ri = lax.broadcasted_iota(jnp.int32, (c, c), 0)
    ci = lax.broadcasted_iota(jnp.int32, (c, c), 1)
    tri_l = jnp.where(ri >= ci, 1.0, 0.0)
    tri_u = jnp.where(ri <= ci, 1.0, 0.0)
    incl = (ri >= ci, ri <= ci)
    strict = (ri > ci, ri < ci)
    hdot = functools.partial(jnp.dot, precision=HIGHEST, preferred_element_type=F32)

    gs = [g_ref[b] for b in range(bsz)]
    gts = [gt_ref[b, 0] for b in range(bsz)]
    gcol = [(hdot(tri_l, g), hdot(tri_u, g)) for g in gs]
    grow = [(hdot(gt, tri_u), hdot(gt, tri_l)) for gt in gts]
    glast = [(gcol[b][0][c - 1:c], gcol[b][1][0:1]) for b in range(bsz)]
    el_ref[...] = jnp.zeros_like(el_ref)
    for d in range(2):
        for b in range(bsz):
            el_ref[d * bsz + b:d * bsz + b + 1, :] = jnp.exp(glast[b][d])

    bh = [(b, hh) for b in range(bsz) for hh in range(HEADS)]
    hs = lambda hh: slice(hh * HEAD_DIM, (hh + 1) * HEAD_DIM)
    ks = [qkv_ref[1, b, :, hs(hh)] for b, hh in bh]
    qs = [qkv_ref[0, b, :, hs(hh)] for b, hh in bh]
    kk = [lax.dot_general(k, k, _NT, preferred_element_type=F32) for k in ks]
    qk = [lax.dot_general(q, k, _NT, preferred_element_type=F32) for q, k in zip(qs, ks)]
    inst = [(d, b, hh, i) for d in range(2) for i, (b, hh) in enumerate(bh)]
    decs, betas, gcs = [], [], []
    for d, b, hh, _ in inst:
        lg = 2 * HEADS + d * HEADS + hh
        gcs.append(gcol[b][d][:, lg:lg + 1])
        betas.append(gs[b][:, d * HEADS + hh:d * HEADS + hh + 1])
        decs.append(jnp.exp(jnp.where(incl[d], gcs[-1] - grow[b][d][lg:lg + 1, :], -jnp.inf)))
    ls = [jnp.where(strict[d], kk[i] * beta * dec, 0.0) for (d, _, _, i), beta, dec in zip(inst, betas, decs)]
    def scaled_operands(part, nparts):
        for (d, b, hh, i), dec, beta, gc in list(zip(inst, decs, betas, gcs))[part::nparts]:
            lg = 2 * HEADS + d * HEADS + hh
            eg = jnp.exp(gc)
            k = ks[i].astype(F32)
            tall_ref[d, b, hh, :, c:2 * c] = (qk[i] * dec).astype(BF16)
            wide_ref[d, b, hh, 0:c, :] = (qs[i].astype(F32) * eg).astype(BF16)
            wide_ref[d, b, hh, c:2 * c, :] = (k * (beta * eg)).astype(BF16)
            wide_ref[d, b, hh, 2 * c:3 * c, :] = (qkv_ref[2, b, :, hs(hh)].astype(F32) * beta).astype(BF16)
            wide_ref[d, b, hh, 3 * c:4 * c, :] = (k * jnp.exp(glast[b][d][:, lg:lg + 1] - gc)).astype(BF16)

    ts = _unit_tri_inverse(ls, [d == 0 for d, _, _, _ in inst], scaled_operands)
    for (d, b, hh, _), t in zip(inst, ts):
        tall_ref[d, b, hh, :, 0:c] = t.astype(BF16)


def _prep_call(qkv, gates, gates_t):
    _, bsz, lt, dn = qkv.shape
    nc = lt // CHUNK
    tall = (2, bsz, HEADS, CHUNK, 2 * CHUNK)
    wide = (2, bsz, HEADS, 4 * CHUNK, HEAD_DIM)
    return pl.pallas_call(
        functools.partial(_prep_kernel, bsz=bsz),
        grid=(nc,),
        in_specs=[pl.BlockSpec((3, bsz, CHUNK, dn), lambda i: (0, 0, i, 0)),
                  pl.BlockSpec((bsz, CHUNK, LANES), lambda i: (0, i, 0)),
                  pl.BlockSpec((bsz, 1, 4 * HEADS, CHUNK), lambda i: (0, i, 0, 0))],
        out_specs=[pl.BlockSpec((None,) + tall, lambda i: (i, 0, 0, 0, 0, 0)),
                   pl.BlockSpec((None,) + wide, lambda i: (i, 0, 0, 0, 0, 0)),
                   pl.BlockSpec((None, SUBLANES, LANES), lambda i: (i, 0, 0))],
        out_shape=[jax.ShapeDtypeStruct((nc,) + tall, BF16),
                   jax.ShapeDtypeStruct((nc,) + wide, BF16),
                   jax.ShapeDtypeStruct((nc, SUBLANES, LANES), F32)],
        compiler_params=_cparams(("arbitrary",)),
        name="prep",
    )(qkv, gates, gates_t)


def _scan_kernel(tf_ref, tb_ref, wf_ref, wb_ref, ef_ref, eb_ref, of_ref, ob_ref, s_ref, *, bsz):
    i = pl.program_id(0)

    @pl.when(i == 0)
    def _():
        s_ref[...] = jnp.zeros_like(s_ref)

    c = CHUNK
    per = tf_ref.shape[0]
    dirs = ((tf_ref, wf_ref, ef_ref, of_ref), (tb_ref, wb_ref, eb_ref, ob_ref))
    inst = [(d, b, hh) for d in range(2) for b in range(bsz) for hh in range(HEADS)]
    state = [s_ref[d, b, hh] for d, b, hh in inst]
    for sub in range(per):
        at = (sub, per - 1 - sub)
        a1 = [jnp.dot(dirs[d][1][at[d], b, hh, 0:2 * c, :], s.astype(BF16), preferred_element_type=F32)
              for (d, b, hh), s in zip(inst, state)]
        v_new = [jnp.dot(dirs[d][0][at[d], b, hh, :, 0:c],
                         (dirs[d][1][at[d], b, hh, 2 * c:3 * c, :].astype(F32) - a[c:]).astype(BF16),
                         preferred_element_type=F32).astype(BF16) for (d, b, hh), a in zip(inst, a1)]
        new_state = []
        for (d, b, hh), a, vn, s in zip(inst, a1, v_new, state):
            lg = 2 * HEADS + d * HEADS + hh
            o = a[:c] + jnp.dot(dirs[d][0][at[d], b, hh, :, c:2 * c], vn, preferred_element_type=F32)
            dirs[d][3][b, at[d] * c:(at[d] + 1) * c, hh * HEAD_DIM:(hh + 1) * HEAD_DIM] = o.astype(BF16)
            el = dirs[d][2][at[d], d * bsz + b:d * bsz + b + 1, lg:lg + 1]
            new_state.append(s * el + lax.dot_general(dirs[d][1][at[d], b, hh, 3 * c:4 * c, :], vn, _TN,
                                                      preferred_element_type=F32))
        state = new_state
    for (d, b, hh), s in zip(inst, state):
        s_ref[d, b, hh] = s


def _scan_call(tall, wide, el, n_lat, n_ctx):
    nc, _, bsz = tall.shape[:3]
    dn = HEADS * HEAD_DIM
    per = SCAN_CHUNKS
    ncx, ncc, ns = n_lat // CHUNK // per, n_ctx // CHUNK // per, nc // per
    assert nc == (ncx + ncc) * per
    cf = lambda i: jnp.where(i < ncc, ncx + i, i - ncc)
    cb = lambda i: ns - 1 - i

    dspec = lambda arr, fn, d: pl.BlockSpec((per, None) + arr.shape[2:], lambda i: (fn(i), d, 0, 0, 0, 0))
    espec = lambda fn: pl.BlockSpec((per, SUBLANES, LANES), lambda i: (fn(i), 0, 0))
    ospec = lambda fn: pl.BlockSpec((bsz, per * CHUNK, dn), lambda i: (0, fn(i), 0))
    return pl.pallas_call(
        functools.partial(_scan_kernel, bsz=bsz),
        grid=(ns,),
        in_specs=[dspec(tall, cf, 0), dspec(tall, cb, 1), dspec(wide, cf, 0), dspec(wide, cb, 1),
                  espec(cf), espec(cb)],
        out_specs=[ospec(lambda i: jnp.maximum(i - ncc, 0)), ospec(lambda i: jnp.minimum(ns - 1 - i, ncx - 1))],
        out_shape=[jax.ShapeDtypeStruct((bsz, n_lat, dn), BF16)] * 2,
        scratch_shapes=[pltpu.VMEM((2, bsz, HEADS, HEAD_DIM, HEAD_DIM), F32)],
        compiler_params=_cparams(("arbitrary",)),
        name="scan",
    )(tall, tall, wide, wide, el, el)


def _hidden_kernel(pe_ref, w1_ref, b1_ref, w2_ref, b2_ref, w3_ref, b3_ref, f_ref, o_ref):
    f = f_ref[...]
    dot = functools.partial(jnp.dot, precision=HIGHEST, preferred_element_type=F32)
    a = jnp.sin(f * (dot(pe_ref[...], w1_ref[...]) + b1_ref[...]))
    a = jnp.sin(f * (dot(a, w2_ref[...]) + b2_ref[...]))
    o_ref[...] = jnp.sin(f * (dot(a, w3_ref[...]) + b3_ref[...]))


def _hidden_call(pe, w1p, b1, w2, b2, w3, b3, freq):
    rows, pw = pe.shape
    fw = 2 * w2.shape[0]
    tr = 512
    full = lambda a: pl.BlockSpec(a.shape, lambda t: (0,) * a.ndim)
    two = lambda a: jnp.tile(a.reshape(1, -1), (1, 2))
    diag2 = lambda a: jnp.kron(jnp.eye(2, dtype=a.dtype), a)
    args = (diag2(w1p), two(b1), diag2(w2), two(b2), diag2(w3), two(b3), two(freq))
    return pl.pallas_call(
        _hidden_kernel,
        grid=(rows // tr,),
        in_specs=[pl.BlockSpec((tr, pw), lambda t: (t, 0))] + [full(a) for a in args],
        out_specs=pl.BlockSpec((tr, fw), lambda t: (t, 0)),
        out_shape=jax.ShapeDtypeStruct((rows, fw), F32),
        compiler_params=_cparams(("arbitrary",)),
        name="hidden",
    )(pe, *args)


def _ftime_kernel(a_ref, t_ref, w_ref, dl_ref, h_ref, ss_ref, *, n_lat, tr):
    t = pl.program_id(0)
    h = _bdot(a_ref[...], w_ref[...]) * jnp.exp(-t_ref[...] * dl_ref[...])
    row = lax.broadcasted_iota(jnp.int32, (tr, 1), 0) + t * tr
    h = jnp.where(row == n_lat, 0.0, h)
    h_ref[...] = h.astype(h_ref.dtype)

    @pl.when(t == 0)
    def _():
        ss_ref[...] = jnp.zeros_like(ss_ref)

    ss_ref[...] += jnp.sum(h * h, axis=0, keepdims=True)


def _ftime_call(a3, tcol, wout2, deltas2, n_lat):
    rows = tcol.shape[0]
    fw = a3.shape[1]
    ncol = wout2.shape[2]
    tr = 512
    side = lambda t: (t * tr) // n_lat
    return pl.pallas_call(
        functools.partial(_ftime_kernel, n_lat=n_lat, tr=tr),
        grid=(rows // tr,),
        in_specs=[pl.BlockSpec((tr, fw), lambda t: (t % (n_lat // tr), 0)),
                  pl.BlockSpec((tr, 1), lambda t: (t, 0)),
                  pl.BlockSpec((None, fw, ncol), lambda t: (side(t), 0, 0)),
                  pl.BlockSpec((1, ncol), lambda t: (0, 0))],
        out_specs=[pl.BlockSpec((tr, ncol), lambda t: (t, 0)),
                   pl.BlockSpec((1, ncol), lambda t: (0, 0))],
        out_shape=[jax.ShapeDtypeStruct((rows, ncol), BF16), jax.ShapeDtypeStruct((1, ncol), F32)],
        compiler_params=_cparams(("arbitrary",)),
        name="ftime",
    )(a3, tcol, wout2, deltas2)


@functools.lru_cache(maxsize=None)
def _fft_tables(n_lat):
    n = 2 * n_lat
    n1t = n // FFT_INNER
    h1 = n1t // 2
    n2 = np.arange(FFT_INNER, dtype=np.int64)[:, None, None]
    k1 = np.arange(n1t, dtype=np.int64)[None, :, None]
    n1 = np.arange(h1, dtype=np.int64)[None, None, :]
    ang = (-2.0 * np.pi / n) * ((k1 * (FFT_INNER * n1 + n2)) % n)
    er, ei = np.cos(ang), np.sin(ang)
    e1 = np.concatenate([np.concatenate([er, -ei], axis=2), np.concatenate([ei, er], axis=2)], axis=1)
    e4 = np.transpose(e1, (0, 2, 1))
    sgn = np.where(np.arange(n1t) % 2 == 0, 1.0, -1.0)[None, :, None]
    ef = np.concatenate([np.concatenate([er, sgn * er], axis=2), np.concatenate([ei, sgn * ei], axis=2)], axis=1)
    a = np.arange(FFT_INNER, dtype=np.int64)
    ang2 = (-2.0 * np.pi / FFT_INNER) * ((a[:, None] * a[None, :]) % FFT_INNER)
    gr, gi = np.cos(ang2), np.sin(ang2)
    g2 = np.block([[gr, -gi], [gi, gr]])
    f32 = lambda x: np.ascontiguousarray(x, dtype=np.float32)
    return f32(e1), f32(e4), f32(ef), f32(g2), f32(g2.T)


def _hold(p, j, first, last, nsteps):
    active = (p >= first) & (p <= last) & (lax.rem(p - first, 2) == 0)
    return jnp.where(active, j, jnp.where(p < first, 0, nsteps - 1))


def _slab_read(x_ref, rows):
    return jnp.concatenate([x_ref[c, rows, :] for c in range(x_ref.shape[0])], axis=1)


def _slab_write(x_ref, rows, val):
    for c in range(x_ref.shape[0]):
        x_ref[c, rows, :] = val[:, c * LANES:(c + 1) * LANES]


def _spectrum_steps(x_ref, g2_ref, k1s, n1t, stride):
    cols = [jnp.concatenate([_slab_read(x_ref, pl.ds(k1, FFT_INNER, stride=stride)),
                             _slab_read(x_ref, pl.ds(n1t + k1, FFT_INNER, stride=stride))], axis=0).astype(BF16)
            for k1 in k1s]
    return [jnp.dot(g2_ref[...], c, preferred_element_type=F32) for c in cols]


def _ffft_kernel(h_ref, ef_ref, g2_ref, ss_ref, o_ref, x_ref, *, n1t, grp, kgrp, stride, inv_n):
    p = pl.program_id(1)
    j = pl.program_id(2)

    @pl.when(p == 0)
    def _():
        for j0 in range(0, grp, FFT_INTERLEAVE):
            jjs = list(range(j0, min(j0 + FFT_INTERLEAVE, grp)))
            us = [jnp.concatenate([h_ref[0, jj], h_ref[1, jj]], axis=0) for jj in jjs]
            slabs = [_bdot(ef_ref[jj], u) for jj, u in zip(jjs, us)]
            for jj, slab in zip(jjs, slabs):
                base = pl.multiple_of((j * grp + jj) * stride, SUBLANES)
                _slab_write(x_ref, pl.ds(base, 2 * n1t), slab)

    @pl.when(p == 1)
    def _():
        scale = lax.rsqrt(ss_ref[...] + EPS) * inv_n
        for j0 in range(0, kgrp, FFT_INTERLEAVE):
            jjs = list(range(j0, min(j0 + FFT_INTERLEAVE, kgrp)))
            xfs = _spectrum_steps(x_ref, g2_ref, [j * kgrp + jj for jj in jjs], n1t, stride)
            for jj, xf in zip(jjs, xfs):
                o_ref[jj] = (xf * scale).astype(BF16)


def _fft_groups(n1t, ns):
    ns = min(ns, n1t)
    assert FFT_INNER % ns == 0 and n1t % ns == 0
    return ns, FFT_INNER // ns, n1t // ns


def _ffft_call(hfull, ssq, ef, g2, n_lat):
    ncol = hfull.shape[1]
    n1t = 2 * n_lat // FFT_INNER
    h1 = n1t // 2
    ct = FFT_CHANNELS
    ns, grp, kgrp = _fft_groups(n1t, FILTER_FFT_STEPS)
    stride = 2 * n1t + SLAB_PAD
    hv = hfull.reshape(2, FFT_INNER, h1, ncol)
    nct = ncol // ct
    cpo = nct // 2
    return pl.pallas_call(
        functools.partial(_ffft_kernel, n1t=n1t, grp=grp, kgrp=kgrp, stride=stride, inv_n=1.0 / (2 * n_lat)),
        grid=(nct, 2, ns),
        in_specs=[pl.BlockSpec((2, grp, h1, ct), lambda c, p, j: (0, _hold(p, j, 0, 0, ns), 0, c)),
                  pl.BlockSpec((grp, 2 * n1t, n1t), lambda c, p, j: (_hold(p, j, 0, 0, ns), 0, 0)),
                  pl.BlockSpec((2 * FFT_INNER, 2 * FFT_INNER), lambda c, p, j: (0, 0)),
                  pl.BlockSpec((1, ct), lambda c, p, j: (0, c))],
        out_specs=pl.BlockSpec((None, kgrp, 2 * FFT_INNER, ct),
                               lambda c, p, j: (c // cpo, _hold(p, j, 1, 1, ns), 0, c % cpo)),
        out_shape=jax.ShapeDtypeStruct((2, n1t, 2 * FFT_INNER, ncol // 2), BF16),
        scratch_shapes=[pltpu.VMEM((ct // LANES, FFT_INNER * stride, LANES), F32)],
        compiler_params=_cparams(("arbitrary", "arbitrary", "arbitrary")),
        name="ffft",
    )(hv, ef, g2, ssq)


def _hyena_kernel(v_ref, x1_ref, x2_ref, hf_ref, e1_ref, e4_ref, g2_ref, g3_ref, bias_ref,
                  o_ref, x_ref, z_ref, *, n1t, grp, kgrp, stride):
    p = pl.program_id(1)
    j = pl.program_id(2)
    h1 = n1t // 2

    def gather(ref, jj):
        return jnp.concatenate([ref[0, jj], ref[1, jj]], axis=0).astype(F32)

    def slab_base(jj):
        return pl.multiple_of((j * grp + jj) * stride, SUBLANES)

    def time_rows(jjs):
        slabs = [_slab_read(x_ref, pl.ds(slab_base(jj), 2 * n1t)).astype(BF16) for jj in jjs]
        return [jnp.dot(e4_ref[jj], s, preferred_element_type=F32) for jj, s in zip(jjs, slabs)]

    def write_slabs(jjs, us):
        slabs = [jnp.dot(e1_ref[jj], u, preferred_element_type=F32) for jj, u in zip(jjs, us)]
        for jj, slab in zip(jjs, slabs):
            _slab_write(x_ref, pl.ds(slab_base(jj), 2 * n1t), slab)

    def chunks(n):
        return [list(range(j0, min(j0 + FFT_INTERLEAVE, n))) for j0 in range(0, n, FFT_INTERLEAVE)]

    @pl.when(p == 0)
    def _():
        for jjs in chunks(grp):
            write_slabs(jjs, [gather(v_ref, jj).astype(BF16) for jj in jjs])

    @pl.when((p == 1) | (p == 3))
    def _():
        for jjs in chunks(kgrp):
            k1s = [j * kgrp + jj for jj in jjs]
            xfs = _spectrum_steps(x_ref, g2_ref, k1s, n1t, stride)
            ys = []
            for jj, xf in zip(jjs, xfs):
                hf = hf_ref[jj].astype(F32)
                xr, xi = xf[:FFT_INNER], xf[FFT_INNER:]
                hr, hi = hf[:FFT_INNER], hf[FFT_INNER:]
                ys.append(jnp.concatenate([xr * hr - xi * hi, xr * hi + xi * hr], axis=0).astype(BF16))
            bbs = [jnp.dot(g3_ref[...], y, preferred_element_type=F32) for y in ys]
            for k1, bb in zip(k1s, bbs):
                _slab_write(x_ref, pl.ds(k1, FFT_INNER, stride=stride), bb[:FFT_INNER])
                _slab_write(x_ref, pl.ds(n1t + k1, FFT_INNER, stride=stride), bb[FFT_INNER:])

    @pl.when(p == 2)
    def _():
        for jjs in chunks(grp):
            z1s = [(gather(x1_ref, jj) * (y + gather(v_ref, jj) * bias_ref[0:1, :])).astype(BF16)
                   for jj, y in zip(jjs, time_rows(jjs))]
            for jj, z1 in zip(jjs, z1s):
                z_ref[j * grp + jj] = z1
            write_slabs(jjs, z1s)

    @pl.when(p == 4)
    def _():
        for jjs in chunks(grp):
            for jj, y in zip(jjs, time_rows(jjs)):
                out = gather(x2_ref, jj) * (y + z_ref[j * grp + jj].astype(F32) * bias_ref[1:2, :])
                o_ref[0, jj] = out[:h1].astype(o_ref.dtype)
                o_ref[1, jj] = out[h1:].astype(o_ref.dtype)


def _hyena_call(hv, hf, e1, e4, g2, g3, bias):
    _, bsz, _, h1, ch = hv.shape
    assert bsz == 2, "the two batch rows are packed as the real and imaginary parts of one FFT"
    n1t = 2 * h1
    ct = FFT_CHANNELS
    ns, grp, kgrp = _fft_groups(n1t, CONV_FFT_STEPS)
    stride = 2 * n1t + SLAB_PAD

    def hspec(which, first, last):
        return pl.BlockSpec((None, bsz, grp, h1, ct),
                            lambda c, p, j: (which, 0, _hold(p, j, first, last, ns), 0, c))

    def hf_map(c, p, j):
        order = jnp.where(p >= 2, 1, 0)
        step = jnp.where((p == 1) | (p == 3), j, jnp.where((p == 0) | (p == 2), 0, ns - 1))
        return (order, step, 0, c)

    return pl.pallas_call(
        functools.partial(_hyena_kernel, n1t=n1t, grp=grp, kgrp=kgrp, stride=stride),
        grid=(ch // ct, 5, ns),
        in_specs=[hspec(0, 0, 2), hspec(1, 2, 2), hspec(2, 4, 4),
                  pl.BlockSpec((None, kgrp, 2 * FFT_INNER, ct), hf_map),
                  pl.BlockSpec((grp, 2 * n1t, n1t), lambda c, p, j: (_hold(p, j, 0, 2, ns), 0, 0)),
                  pl.BlockSpec((grp, n1t, 2 * n1t), lambda c, p, j: (_hold(p, j, 2, 4, ns), 0, 0)),
                  pl.BlockSpec((2 * FFT_INNER, 2 * FFT_INNER), lambda c, p, j: (0, 0)),
                  pl.BlockSpec((2 * FFT_INNER, 2 * FFT_INNER), lambda c, p, j: (0, 0)),
                  pl.BlockSpec((2, ct), lambda c, p, j: (0, c))],
        out_specs=pl.BlockSpec((bsz, grp, h1, ct), lambda c, p, j: (0, _hold(p, j, 4, 4, ns), 0, c)),
        out_shape=jax.ShapeDtypeStruct((bsz, FFT_INNER, h1, ch), BF16),
        scratch_shapes=[pltpu.VMEM((ct // LANES, FFT_INNER * stride, LANES), F32),
                        pltpu.VMEM((FFT_INNER, n1t, ct), BF16)],
        compiler_params=_cparams(("arbitrary", "arbitrary", "arbitrary"), VMEM_LIMIT_CONV),
        name="hyena",
    )(hv, hv, hv, hf, e1, e4, g2, g3, bias)


def _merge_kernel(of_ref, ob_ref, gp_ref, hy_ref, x_ref, mod_ref, dng_ref, wpa_ref, wpb_ref, wo_ref,
                  fg_ref, o_ref, *, d):
    b = pl.program_id(0)
    blocks = [slice(r * MXU_TILE, (r + 1) * MXU_TILE) for r in range(o_ref.shape[1] // MXU_TILE)]
    f32 = lambda ref, *idx: ref[idx].astype(F32)

    def head_norm(rows):
        o = f32(of_ref, 0, rows) + f32(ob_ref, 0, rows)
        za = f32(gp_ref, 0, 0, rows)
        parts = []
        for hh in range(HEADS):
            hs = slice(hh * HEAD_DIM, (hh + 1) * HEAD_DIM)
            oh = o[:, hs]
            ms = jnp.mean(oh * oh, axis=-1, keepdims=True)
            parts.append((oh * lax.rsqrt(ms + EPS) * dng_ref[...] * za[:, hs]).astype(BF16))
        return jnp.concatenate(parts, axis=1)

    o_a = [head_norm(rows) for rows in blocks]
    y_a = [jnp.dot(a, wpa_ref[...], preferred_element_type=F32) for a in o_a]
    o_b = [(f32(hy_ref, 0, rows) * f32(gp_ref, 1, 0, rows)).astype(BF16) for rows in blocks]
    y_b = [jnp.dot(a, wpb_ref[...], preferred_element_type=F32) for a in o_b]
    m = [(f32(gp_ref, 2, 0, rows) * ya + f32(gp_ref, 3, 0, rows) * yb).astype(BF16)
         for rows, ya, yb in zip(blocks, y_a, y_b)]
    y = [jnp.dot(a, wo_ref[...], preferred_element_type=F32) for a in m]
    gate = mod_ref[pl.ds(b, 1), :][:, 2 * d:3 * d]
    for rows, yr in zip(blocks, y):
        xn = x_ref[0, rows, :] + gate * yr
        o_ref[0, rows, :] = xn * lax.rsqrt(jnp.mean(xn * xn, axis=-1, keepdims=True) + EPS) * fg_ref[...]


def _merge_call(o_f, o_b, gp, hy, x, mod, dn_norm_g, w_pa, w_pb, w_out, final_g):
    bsz, n_lat, d = x.shape
    tm = NORM_TILE if n_lat % NORM_TILE == 0 else 256
    wfull = lambda a: pl.BlockSpec(a.shape, lambda b, t: (0,) * a.ndim)
    tok = pl.BlockSpec((1, tm, d), lambda b, t: (b, t, 0))
    return pl.pallas_call(
        functools.partial(_merge_kernel, d=d),
        grid=(bsz, n_lat // tm),
        in_specs=[tok, tok,
                  pl.BlockSpec((4, 1, tm, d), lambda b, t: (0, b, t, 0)),
                  tok, tok, wfull(mod),
                  pl.BlockSpec((1, HEAD_DIM), lambda b, t: (0, 0)),
                  wfull(w_pa), wfull(w_pb), wfull(w_out),
                  pl.BlockSpec((1, d), lambda b, t: (0, 0))],
        out_specs=tok,
        out_shape=jax.ShapeDtypeStruct((bsz, n_lat, d), F32),
        compiler_params=_cparams(("arbitrary", "arbitrary")),
        name="merge",
    )(o_f, o_b, gp, hy, x, mod, dn_norm_g.reshape(1, HEAD_DIM), w_pa, w_pb, w_out, final_g.reshape(1, d))


def _position_features(n_lat):
    h1 = n_lat // FFT_INNER
    m = (np.arange(FFT_INNER)[:, None] + FFT_INNER * np.arange(h1)[None, :]).reshape(-1)
    lag = np.concatenate([m, n_lat - m]).astype(np.float64)
    lag[n_lat] = 0.0
    bands = (HY_EMB - 1) // 2
    t = (lag / (n_lat - 1))[:, None]
    wpos = (2.0 * math.pi / n_lat) * lag[:, None]
    fb = np.linspace(1e-4, bands - 1, bands)[None, :]
    pe = np.concatenate([t, np.cos(fb * wpos), -np.sin(fb * wpos)], axis=1)
    pe_pad = np.zeros((2 * n_lat, LANES), np.float32)
    pe_pad[:, :HY_EMB] = pe
    return pe_pad, np.ascontiguousarray(t, dtype=np.float32)


def kernel(x, c, ctx, c_ctx, w_mod, b_mod, norm_g, w_in, dn_conv_w, dn_a_log, dn_dt_bias, dn_norm_g,
           hy_conv_w, hy_conv_b, hy_f_w1, hy_f_b1, hy_f_w2, hy_f_b2, hy_f_w3, hy_f_b3, hy_f_wout,
           hy_f_freq, hy_bias, w_pa, w_pb, w_out, final_g):
    bsz, n_lat, d = x.shape
    n_ctx = ctx.shape[1]
    assert w_mod.shape[0] == 1, "single layer: the context stream is only read through its scan states"
    dn = HEADS * HEAD_DIM
    hy = hy_bias.shape[-1]
    assert n_lat % 256 == 0 and n_ctx % 256 == 0 and dn == d and hy == d

    cvec = jnp.zeros((SUBLANES, d), F32).at[:bsz].set(c).at[bsz].set(c_ctx)
    mod = _mod_call(cvec, w_mod[0], b_mod[0])

    w = w_in[0].astype(BF16)
    o_qkv, o_gate, o_hy, o_gp = 0, 4 * dn, 4 * dn + 4 * HEADS, 4 * dn + 4 * HEADS + 3 * hy
    col3 = lambda a, off, n: jnp.transpose(a[:, off:off + 3 * n].reshape(a.shape[0], 3, n), (1, 0, 2))
    lt = n_lat + n_ctx
    tm = PROJ_TILE if n_lat % PROJ_TILE == 0 else 256

    wg = jnp.zeros((d, LANES), BF16).at[:, :4 * HEADS].set(w[:, o_gate:o_gate + 4 * HEADS])
    lane_pad = lambda a: jnp.zeros((1, LANES), F32).at[0, 2 * HEADS:4 * HEADS].set(a.reshape(-1))
    gate_args = (wg, lane_pad(dn_a_log[0]), lane_pad(dn_dt_bias[0]))
    h_tok, gates = _hnorm_call(x, mod, norm_g[0], *gate_args, tm=NORM_TILE, gate_rows=lt, gate_tile0=0, name="hnorm")
    h_ctx, gates = _hnorm_call(ctx, mod, norm_g[0], *gate_args, tm=n_ctx, gate_rows=lt, gate_tile0=n_lat // n_ctx,
                               mod_row=bsz, prev_gates=gates, name="hnorm_ctx")
    h1 = n_lat // FFT_INNER
    h_perm = jnp.transpose(h_tok.reshape(bsz, h1, FFT_INNER, d), (0, 2, 1, 3)).reshape(bsz, n_lat, d)

    w_qkv, cw_qkv, no_bias = col3(w, o_qkv, dn), col3(dn_conv_w[0], 0, dn), jnp.zeros((3, 1, dn), F32)
    qkv = _convproj_call(h_tok, w_qkv, cw_qkv, no_bias, rowlen=GRID_W, tm=tm, tile0=0, out_rows=lt, qk_norm=True,
                         name="qkv")
    qkv = _convproj_call(h_ctx, w_qkv, cw_qkv, no_bias, rowlen=n_ctx, tm=n_ctx, tile0=n_lat // n_ctx, out_rows=lt,
                         qk_norm=True, prev=qkv, name="qkv_ctx")
    gates_t = jnp.transpose(gates[:, :, :4 * HEADS].reshape(bsz, lt // CHUNK, CHUNK, 4 * HEADS), (0, 1, 3, 2))
    tall, wide, el = _prep_call(qkv, gates, gates_t)
    o_f, o_b = _scan_call(tall, wide, el, n_lat, n_ctx)

    hyp = _hyproj_call(h_perm, col3(w, o_hy, hy), col3(hy_conv_w[0], 0, hy), hy_conv_b[0].reshape(3, 1, hy))
    w4 = jnp.stack([w[:, 3 * dn:4 * dn]] + [w[:, o_gp + i * d:o_gp + (i + 1) * d] for i in range(3)])
    gp = _gproj_call(h_tok, w4, n_lat)

    pe, tcol = _position_features(n_lat)
    w1p = jnp.zeros((LANES, hy_f_w1.shape[-1]), F32).at[:HY_EMB].set(hy_f_w1[0])
    pe2 = jnp.asarray(np.concatenate([pe[:n_lat], pe[n_lat:]], axis=1))
    a3 = _hidden_call(pe2, w1p, hy_f_b1[0], hy_f_w2[0], hy_f_b2[0], hy_f_w3[0], hy_f_b3[0], hy_f_freq[0])
    deltas = np.abs(np.linspace(math.log(HY_DECAY_TARGET) / HY_SLOW_DECAY,
                                math.log(HY_DECAY_TARGET) / HY_FAST_DECAY, hy)).astype(np.float32)
    deltas2 = jnp.asarray(np.tile(deltas, 2)[None, :])
    wside = hy_f_wout[0].reshape(-1, 2, 2 * hy).transpose(1, 0, 2).astype(BF16)
    zside = jnp.zeros_like(wside[0])
    wout2 = jnp.stack([jnp.concatenate([wside[0], zside]), jnp.concatenate([zside, wside[1]])])
    hfull, ssq = _ftime_call(a3, jnp.asarray(tcol), wout2, deltas2, n_lat)
    e1, e4, ef, g2, g3 = (jnp.asarray(t).astype(BF16) for t in _fft_tables(n_lat))
    hf = _ffft_call(hfull, ssq, ef, g2, n_lat)

    yh = _hyena_call(hyp, hf, e1, e4, g2, g3, hy_bias[0])
    yh = jnp.transpose(yh, (0, 2, 1, 3)).reshape(bsz, n_lat, hy)

    return _merge_call(o_f, o_b, gp, yh, x, mod, dn_norm_g[0], w_pa[0].astype(BF16), w_pb[0].astype(BF16),
                       w_out[0].astype(BF16), final_g)
```

```python
import functools
import math

import numpy as np
import jax
import jax.numpy as jnp
from jax import lax
from jax.experimental import pallas as pl
from jax.experimental.pallas import tpu as pltpu

F32 = jnp.float32
BF16 = jnp.bfloat16
HIGHEST = lax.Precision.HIGHEST

EPS = 1e-6
HEADS = 8
HEAD_DIM = 128
CHUNK = 64
GRID_W = 64
HY_EMB = 33
HY_DECAY_TARGET = 1e-2
HY_FAST_DECAY = 0.3
HY_SLOW_DECAY = 1.5

LANES = 128
SUBLANES = 8
FFT_INNER = 128
CONV_FFT_STEPS = 8
FILTER_FFT_STEPS = 8
FFT_CHANNELS = 256
FFT_INTERLEAVE = 8
PROJ_TILE = 1024
NORM_TILE = 512
SCAN_CHUNKS = 4
PREP_CHUNKS = 2
HYPROJ_N1 = 16
MXU_TILE = 256
SLAB_PAD = 8
VMEM_LIMIT = 56 * 1024 * 1024
VMEM_LIMIT_CONV = 62 * 1024 * 1024


def _cparams(sem, vmem=VMEM_LIMIT):
    return pltpu.CompilerParams(dimension_semantics=sem, vmem_limit_bytes=vmem)


def _bdot(a, b):
    return jnp.dot(a.astype(BF16), b.astype(BF16), preferred_element_type=F32)


def _silu(x):
    return x * jax.nn.sigmoid(x)


def _mod_kernel(c_ref, w_ref, b_ref, o_ref):
    s = _silu(c_ref[...])
    o_ref[...] = jnp.dot(s, w_ref[...], precision=HIGHEST, preferred_element_type=F32) + b_ref[...]


def _mod_call(cvec, w_mod, b_mod):
    rows, d = cvec.shape
    n = w_mod.shape[1]
    tn = 1024
    return pl.pallas_call(
        _mod_kernel,
        grid=(n // tn,),
        in_specs=[pl.BlockSpec((rows, d), lambda j: (0, 0)),
                  pl.BlockSpec((d, tn), lambda j: (0, j)),
                  pl.BlockSpec((1, tn), lambda j: (0, j))],
        out_specs=pl.BlockSpec((rows, tn), lambda j: (0, j)),
        out_shape=jax.ShapeDtypeStruct((rows, n), F32),
        compiler_params=_cparams(("arbitrary",)),
        name="mod",
    )(cvec, w_mod, b_mod.reshape(1, n))


def _norm_rows(x, m, g_ref, d):
    y = x * lax.rsqrt(jnp.mean(x * x, axis=-1, keepdims=True) + EPS) * g_ref[...]
    return (y * (1.0 + m[:, d:2 * d]) + m[:, 0:d]).astype(BF16)


def _gate_rows(h, w_ref, alog_ref, dtb_ref):
    z = jnp.dot(h, w_ref[...], preferred_element_type=F32)
    lane = lax.broadcasted_iota(jnp.int32, z.shape, 1)
    u = z + dtb_ref[...]
    softplus = jnp.maximum(u, 0.0) + jnp.log(1.0 + jnp.exp(-jnp.abs(u)))
    return jnp.where(lane < 2 * HEADS, jax.nn.sigmoid(z), -jnp.exp(alog_ref[...]) * softplus)


def _hnorm_kernel(tok_ref, mod_ref, g_ref, w_ref, alog_ref, dtb_ref, *rest, d, mod_row):
    o_ref, og_ref = rest[-2:]
    row = pl.program_id(0) if mod_row is None else mod_row
    m = mod_ref[pl.ds(row, 1), :]
    h = _norm_rows(tok_ref[0], m, g_ref, d)
    o_ref[0] = h
    og_ref[0] = _gate_rows(h, w_ref, alog_ref, dtb_ref)


def _hnorm_call(tok, mod, norm_g, wg, alog, dtb, *, tm, gate_rows, gate_tile0, mod_row=None, prev_gates=None, name):
    bsz, rows, d = tok.shape
    full = lambda a: pl.BlockSpec(a.shape, lambda b, t: (0,) * a.ndim)
    args = [tok, mod, norm_g.reshape(1, d), wg, alog, dtb]
    in_specs = [pl.BlockSpec((1, tm, d), lambda b, t: (b, t, 0))] + [full(a) for a in args[1:]]
    aliases = {}
    if prev_gates is not None:
        in_specs.append(pl.BlockSpec(memory_space=pl.ANY))
        args.append(prev_gates)
        aliases = {6: 1}
    return pl.pallas_call(
        functools.partial(_hnorm_kernel, d=d, mod_row=mod_row),
        grid=(bsz, rows // tm),
        in_specs=in_specs,
        out_specs=[pl.BlockSpec((1, tm, d), lambda b, t: (b, t, 0)),
                   pl.BlockSpec((1, tm, LANES), lambda b, t: (b, t + gate_tile0, 0))],
        out_shape=[jax.ShapeDtypeStruct((bsz, rows, d), BF16), jax.ShapeDtypeStruct((bsz, gate_rows, LANES), F32)],
        input_output_aliases=aliases,
        compiler_params=_cparams(("arbitrary", "arbitrary")),
        name=name,
    )(*args)


@functools.lru_cache(maxsize=None)
def _shift_matrix(rowlen, taps, blk):
    t = np.arange(blk)
    mats = []
    for j in range(taps):
        d = j - taps // 2
        if d != 0:
            ok = ((t + d) >= 0) & ((t + d) < blk) & (t // rowlen == (t + d) // rowlen)
            s = np.zeros((blk, blk), np.float32)
            s[t[ok], t[ok] + d] = 1.0
            mats.append(s)
    return np.concatenate(mats, axis=1)


def _shift_block(rowlen, taps):
    return max(rowlen, MXU_TILE // (taps - 1) // rowlen * rowlen)


def _convproj_kernel(h_ref, w_ref, cw_ref, cb_ref, sh_ref, *rest, tm, qk_norm):
    o_ref = rest[-1]
    j = pl.program_id(0)
    taps = cw_ref.shape[1]
    mid = taps // 2
    blk = sh_ref.shape[0]
    mblk = max(blk, MXU_TILE)
    if qk_norm:
        na = jnp.where(j == 0, float(HEAD_DIM), jnp.where(j == 1, 1.0, 0.0))
        nb = jnp.where(j == 0, EPS * HEAD_DIM, jnp.where(j == 1, EPS, 1.0))
    for r in range(tm // mblk):
        zm = jnp.dot(h_ref[0, r * mblk:(r + 1) * mblk, :], w_ref[0], preferred_element_type=F32)
        for g in range(mblk // blk):
            rows = slice(r * mblk + g * blk, r * mblk + (g + 1) * blk)
            zb = zm[g * blk:(g + 1) * blk]
            side = jnp.concatenate(
                [(zb * cw_ref[0, jt:jt + 1, :]).astype(BF16) for jt in range(taps) if jt != mid], axis=0)
            y = zb * cw_ref[0, mid:mid + 1, :] + jnp.dot(sh_ref[...], side, preferred_element_type=F32)
            if not qk_norm:
                o_ref[0, 0, rows, :] = (y + cb_ref[0]).astype(o_ref.dtype)
                continue
            y = _silu(y)
            for hh in range(HEADS):
                yh = y[:, hh * HEAD_DIM:(hh + 1) * HEAD_DIM]
                nrm = jnp.sum(yh * yh, axis=-1, keepdims=True)
                o_ref[0, 0, rows, hh * HEAD_DIM:(hh + 1) * HEAD_DIM] = (
                    yh * lax.rsqrt(nrm * na + nb)).astype(o_ref.dtype)


def _convproj_call(h, w3, cw3, cb3, *, rowlen, tm, tile0, out_rows, qk_norm, prev=None, name):
    bsz, rows, d = h.shape
    ntiles = rows // tm
    nw = w3.shape[2]
    taps = cw3.shape[1]
    blk = _shift_block(rowlen, taps)
    assert tm % max(blk, MXU_TILE) == 0 and max(blk, MXU_TILE) % blk == 0
    sh = jnp.asarray(_shift_matrix(rowlen, taps, blk)).astype(BF16)
    in_specs = [pl.BlockSpec((1, tm, d), lambda j, b, t: (b, t, 0)),
                pl.BlockSpec((1, d, nw), lambda j, b, t: (j, 0, 0)),
                pl.BlockSpec((1, cw3.shape[1], nw), lambda j, b, t: (j, 0, 0)),
                pl.BlockSpec((1, 1, nw), lambda j, b, t: (j, 0, 0)),
                pl.BlockSpec(sh.shape, lambda j, b, t: (0, 0))]
    args = [h, w3, cw3, cb3, sh]
    aliases = {}
    if prev is not None:
        in_specs.append(pl.BlockSpec(memory_space=pl.ANY))
        args.append(prev)
        aliases = {5: 0}
    return pl.pallas_call(
        functools.partial(_convproj_kernel, tm=tm, qk_norm=qk_norm),
        grid=(3, bsz, ntiles),
        in_specs=in_specs,
        out_specs=pl.BlockSpec((1, 1, tm, nw), lambda j, b, t: (j, b, t + tile0, 0)),
        out_shape=jax.ShapeDtypeStruct((3, bsz, out_rows, nw), BF16),
        input_output_aliases=aliases,
        compiler_params=_cparams(("arbitrary", "arbitrary", "arbitrary")),
        name=name,
    )(*args)


def _hyproj_kernel(h_ref, w_ref, cw_ref, cb_ref, o_ref):
    n2s, n1s, d = h_ref.shape[1:]
    taps = cw_ref.shape[1]
    hrows = h_ref[0].reshape(n2s * n1s, d)
    step = min(MXU_TILE, n2s * n1s)
    z = jnp.concatenate([jnp.dot(hrows[r:r + step], w_ref[0], preferred_element_type=F32)
                         for r in range(0, n2s * n1s, step)], axis=0)
    y = cb_ref[0] + z * cw_ref[0, taps // 2:taps // 2 + 1, :]
    for jt in range(taps):
        s = (jt - taps // 2) * n1s
        if s > 0:
            y = y + jnp.concatenate([z[s:], jnp.zeros((s, z.shape[1]), F32)], axis=0) * cw_ref[0, jt:jt + 1, :]
        elif s < 0:
            y = y + jnp.concatenate([jnp.zeros((-s, z.shape[1]), F32), z[:s]], axis=0) * cw_ref[0, jt:jt + 1, :]
    o_ref[0, 0] = y.astype(o_ref.dtype).reshape(n2s, n1s, y.shape[1])


def _hyproj_call(h_perm, w3, cw3, cb3):
    bsz, n_lat, d = h_perm.shape
    nw = w3.shape[2]
    h1 = n_lat // FFT_INNER
    n1s = min(HYPROJ_N1, h1)
    hv = h_perm.reshape(bsz, FFT_INNER, h1, d)
    return pl.pallas_call(
        _hyproj_kernel,
        grid=(3, bsz, FFT_INNER // GRID_W, h1 // n1s),
        in_specs=[pl.BlockSpec((1, GRID_W, n1s, d), lambda j, b, r, t: (b, r, t, 0)),
                  pl.BlockSpec((1, d, nw), lambda j, b, r, t: (j, 0, 0)),
                  pl.BlockSpec((1, cw3.shape[1], nw), lambda j, b, r, t: (j, 0, 0)),
                  pl.BlockSpec((1, 1, nw), lambda j, b, r, t: (j, 0, 0))],
        out_specs=pl.BlockSpec((1, 1, GRID_W, n1s, nw), lambda j, b, r, t: (j, b, r, t, 0)),
        out_shape=jax.ShapeDtypeStruct((3, bsz, FFT_INNER, h1, nw), BF16),
        compiler_params=_cparams(("arbitrary",) * 4),
        name="hyproj",
    )(hv, w3, cw3, cb3)


def _gproj_kernel(h_ref, w_ref, o_ref):
    j = pl.program_id(0)
    for r in range(h_ref.shape[1] // MXU_TILE):
        rows = slice(r * MXU_TILE, (r + 1) * MXU_TILE)
        z = jnp.dot(h_ref[0, rows, :], w_ref[0], preferred_element_type=F32)
        s = jax.nn.sigmoid(z)
        o_ref[0, 0, rows, :] = jnp.where(j < 2, z * s, s).astype(BF16)


def _gproj_call(h, w4, n_lat):
    bsz, _, d = h.shape
    nw = w4.shape[2]
    tm = PROJ_TILE if n_lat % PROJ_TILE == 0 else 256
    return pl.pallas_call(
        _gproj_kernel,
        grid=(4, bsz, n_lat // tm),
        in_specs=[pl.BlockSpec((1, tm, d), lambda j, b, t: (b, t, 0)),
                  pl.BlockSpec((1, d, nw), lambda j, b, t: (j, 0, 0))],
        out_specs=pl.BlockSpec((1, 1, tm, nw), lambda j, b, t: (j, b, t, 0)),
        out_shape=jax.ShapeDtypeStruct((4, bsz, n_lat, nw), BF16),
        compiler_params=_cparams(("arbitrary", "arbitrary", "arbitrary")),
        name="gproj",
    )(h, w4)


def _unit_tri_inverse(ls, lowers, filler):
    c = ls[0].shape[0]
    ri = lax.broadcasted_iota(jnp.int32, (c, c), 0)
    ci = lax.broadcasted_iota(jnp.int32, (c, c), 1)
    same = lambda n: (ri // n) == (ci // n)
    eye = jnp.where(ri == ci, 1.0, 0.0)
    in2 = same(2)
    ts = [eye - jnp.where(in2, l, 0.0) for l in ls]
    n = 2
    nlevels = c.bit_length() - 2
    level = 0
    while n < c:
        blk = same(2 * n) & jnp.logical_not(same(n))
        offs = [jnp.where(blk, l, 0.0).astype(BF16) for l in ls]
        if n % SUBLANES:
            prods = [_bdot(t, off) for t, off in zip(ts, offs)]
            filler(level, nlevels)
            ts = [t - _bdot(p, t) for p, t in zip(prods, ts)]
        else:
            def moving(t, lower):
                first = n if lower else 0
                return jnp.concatenate([t[r:r + n] for r in range(first, c, 2 * n)], axis=0)

            def merged(t, upd, lower):
                pieces, k = [], 0
                for r in range(0, c, n):
                    if ((r // n) % 2 == 1) == lower:
                        pieces.append(upd[k * n:(k + 1) * n])
                        k += 1
                    else:
                        pieces.append(t[r:r + n])
                return jnp.concatenate(pieces, axis=0)

            rows = [moving(t, lo) for t, lo in zip(ts, lowers)]
            prods = [_bdot(r, off) for r, off in zip(rows, offs)]
            filler(level, nlevels)
            upds = [r - _bdot(p, t) for r, p, t in zip(rows, prods, ts)]
            ts = [merged(t, u, lo) for t, u, lo in zip(ts, upds, lowers)]
        n *= 2
        level += 1
    return ts


_NT = (((1,), (1,)), ((), ()))
_TN = (((0,), (0,)), ((), ()))


def _prep_kernel(qkv_ref, g_ref, gt_ref, tall_ref, wide_ref, el_ref, *, bsz):
    c = CHUNK
    ri = lax.broadcasted_iota(jnp.int32, (c, c), 0)
    ci = lax.broadcasted_iota(jnp.int32, (c, c), 1)
    tri_l = jnp.where(ri >= ci, 1.0, 0.0)
    tri_u = jnp.where(ri <= ci, 1.0, 0.0)
    incl = (ri >= ci, ri <= ci)
    strict = (ri > ci, ri < ci)
    hdot = functools.partial(jnp.dot, precision=HIGHEST, preferred_element_type=F32)

    per = tall_ref.shape[0]
    cb = [(ch, b) for ch in range(per) for b in range(bsz)]
    rows = lambda ch: slice(ch * c, (ch + 1) * c)
    gs = [g_ref[b, rows(ch), :] for ch, b in cb]
    gts = [gt_ref[b, ch] for ch, b in cb]
    gcol = [(hdot(tri_l, g), hdot(tri_u, g)) for g in gs]
    grow = [(hdot(gt, tri_u), hdot(gt, tri_l)) for gt in gts]
    glast = [(gc[0][c - 1:c], gc[1][0:1]) for gc in gcol]
    el_ref[...] = jnp.zeros_like(el_ref)
    for p, (ch, b) in enumerate(cb):
        for d in range(2):
            el_ref[ch, d * bsz + b:d * bsz + b + 1, :] = jnp.exp(glast[p][d])

    pbh = [(p, hh) for p in range(len(cb)) for hh in range(HEADS)]
    hs = lambda hh: slice(hh * HEAD_DIM, (hh + 1) * HEAD_DIM)
    tok = lambda w, p, hh: qkv_ref[w, cb[p][1], rows(cb[p][0]), hs(hh)]
    ks = [tok(1, p, hh) for p, hh in pbh]
    qs = [tok(0, p, hh) for p, hh in pbh]
    kk = [lax.dot_general(k, k, _NT, preferred_element_type=F32) for k in ks]
    qk = [lax.dot_general(q, k, _NT, preferred_element_type=F32) for q, k in zip(qs, ks)]
    inst = [(d, p, hh, i) for d in range(2) for i, (p, hh) in enumerate(pbh)]
    decs, betas, gcs = [], [], []
    for d, p, hh, _ in inst:
        lg = 2 * HEADS + d * HEADS + hh
        gcs.append(gcol[p][d][:, lg:lg + 1])
        betas.append(gs[p][:, d * HEADS + hh:d * HEADS + hh + 1])
        decs.append(jnp.exp(jnp.where(incl[d], gcs[-1] - grow[p][d][lg:lg + 1, :], -jnp.inf)))
    ls = [jnp.where(strict[d], kk[i] * beta * dec, 0.0) for (d, _, _, i), beta, dec in zip(inst, betas, decs)]

    def scaled_operands(part, nparts):
        for (d, p, hh, i), dec, beta, gc in list(zip(inst, decs, betas, gcs))[part::nparts]:
            ch, b = cb[p]
            lg = 2 * HEADS + d * HEADS + hh
            eg = jnp.exp(gc)
            k = ks[i].astype(F32)
            tall_ref[ch, d, b, hh, :, c:2 * c] = (qk[i] * dec).astype(BF16)
            wide_ref[ch, d, b, hh, 0:c, :] = (qs[i].astype(F32) * eg).astype(BF16)
            wide_ref[ch, d, b, hh, c:2 * c, :] = (k * (beta * eg)).astype(BF16)
            wide_ref[ch, d, b, hh, 2 * c:3 * c, :] = (tok(2, p, hh).astype(F32) * beta).astype(BF16)
            wide_ref[ch, d, b, hh, 3 * c:4 * c, :] = (k * jnp.exp(glast[p][d][:, lg:lg + 1] - gc)).astype(BF16)

    ts = _unit_tri_inverse(ls, [d == 0 for d, _, _, _ in inst], scaled_operands)
    for (d, p, hh, _), t in zip(inst, ts):
        tall_ref[cb[p][0], d, cb[p][1], hh, :, 0:c] = t.astype(BF16)


def _prep_call(qkv, gates, gates_t):
    _, bsz, lt, dn = qkv.shape
    nc = lt // CHUNK
    per = PREP_CHUNKS
    assert nc % per == 0
    tall = (2, bsz, HEADS, CHUNK, 2 * CHUNK)
    wide = (2, bsz, HEADS, 4 * CHUNK, HEAD_DIM)
    return pl.pallas_call(
        functools.partial(_prep_kernel, bsz=bsz),
        grid=(nc // per,),
        in_specs=[pl.BlockSpec((3, bsz, per * CHUNK, dn), lambda i: (0, 0, i, 0)),
                  pl.BlockSpec((bsz, per * CHUNK, LANES), lambda i: (0, i, 0)),
                  pl.BlockSpec((bsz, per, 4 * HEADS, CHUNK), lambda i: (0, i, 0, 0))],
        out_specs=[pl.BlockSpec((per,) + tall, lambda i: (i, 0, 0, 0, 0, 0)),
                   pl.BlockSpec((per,) + wide, lambda i: (i, 0, 0, 0, 0, 0)),
                   pl.BlockSpec((per, SUBLANES, LANES), lambda i: (i, 0, 0))],
        out_shape=[jax.ShapeDtypeStruct((nc,) + tall, BF16),
                   jax.ShapeDtypeStruct((nc,) + wide, BF16),
                   jax.ShapeDtypeStruct((nc, SUBLANES, LANES), F32)],
        compiler_params=_cparams(("arbitrary",)),
        name="prep",
    )(qkv, gates, gates_t)


def _scan_kernel(tf_ref, tb_ref, wf_ref, wb_ref, ef_ref, eb_ref, of_ref, ob_ref, s_ref, *, bsz):
    i = pl.program_id(0)

    @pl.when(i == 0)
    def _():
        s_ref[...] = jnp.zeros_like(s_ref)

    c = CHUNK
    per = tf_ref.shape[0]
    dirs = ((tf_ref, wf_ref, ef_ref, of_ref), (tb_ref, wb_ref, eb_ref, ob_ref))
    inst = [(d, b, hh) for d in range(2) for b in range(bsz) for hh in range(HEADS)]
    state = [s_ref[d, b, hh] for d, b, hh in inst]
    for sub in range(per):
        at = (sub, per - 1 - sub)
        a1 = [jnp.dot(dirs[d][1][at[d], b, hh, 0:2 * c, :], s.astype(BF16), preferred_element_type=F32)
              for (d, b, hh), s in zip(inst, state)]
        v_new = [jnp.dot(dirs[d][0][at[d], b, hh, :, 0:c],
                         (dirs[d][1][at[d], b, hh, 2 * c:3 * c, :].astype(F32) - a[c:]).astype(BF16),
                         preferred_element_type=F32).astype(BF16) for (d, b, hh), a in zip(inst, a1)]
        new_state = []
        for (d, b, hh), a, vn, s in zip(inst, a1, v_new, state):
            lg = 2 * HEADS + d * HEADS + hh
            o = a[:c] + jnp.dot(dirs[d][0][at[d], b, hh, :, c:2 * c], vn, preferred_element_type=F32)
            dirs[d][3][b, at[d] * c:(at[d] + 1) * c, hh * HEAD_DIM:(hh + 1) * HEAD_DIM] = o.astype(BF16)
            el = dirs[d][2][at[d], d * bsz + b:d * bsz + b + 1, lg:lg + 1]
            new_state.append(s * el + lax.dot_general(dirs[d][1][at[d], b, hh, 3 * c:4 * c, :], vn, _TN,
                                                      preferred_element_type=F32))
        state = new_state
    for (d, b, hh), s in zip(inst, state):
        s_ref[d, b, hh] = s


def _scan_call(tall, wide, el, n_lat, n_ctx):
    nc, _, bsz = tall.shape[:3]
    dn = HEADS * HEAD_DIM
    per = SCAN_CHUNKS
    ncx, ncc, ns = n_lat // CHUNK // per, n_ctx // CHUNK // per, nc // per
    assert nc == (ncx + ncc) * per
    cf = lambda i: jnp.where(i < ncc, ncx + i, i - ncc)
    cb = lambda i: ns - 1 - i

    dspec = lambda arr, fn, d: pl.BlockSpec((per, None) + arr.shape[2:], lambda i: (fn(i), d, 0, 0, 0, 0))
    espec = lambda fn: pl.BlockSpec((per, SUBLANES, LANES), lambda i: (fn(i), 0, 0))
    ospec = lambda fn: pl.BlockSpec((bsz, per * CHUNK, dn), lambda i: (0, fn(i), 0))
    return pl.pallas_call(
        functools.partial(_scan_kernel, bsz=bsz),
        grid=(ns,),
        in_specs=[dspec(tall, cf, 0), dspec(tall, cb, 1), dspec(wide, cf, 0), dspec(wide, cb, 1),
                  espec(cf), espec(cb)],
        out_specs=[ospec(lambda i: jnp.maximum(i - ncc, 0)), ospec(lambda i: jnp.minimum(ns - 1 - i, ncx - 1))],
        out_shape=[jax.ShapeDtypeStruct((bsz, n_lat, dn), BF16)] * 2,
        scratch_shapes=[pltpu.VMEM((2, bsz, HEADS, HEAD_DIM, HEAD_DIM), F32)],
        compiler_params=_cparams(("arbitrary",)),
        name="scan",
    )(tall, tall, wide, wide, el, el)


def _hidden_kernel(pe_ref, w1_ref, b1_ref, w2_ref, b2_ref, w3_ref, b3_ref, f_ref, o_ref):
    f = f_ref[...]
    dot = functools.partial(jnp.dot, precision=HIGHEST, preferred_element_type=F32)
    a = jnp.sin(f * (dot(pe_ref[...], w1_ref[...]) + b1_ref[...]))
    a = jnp.sin(f * (dot(a, w2_ref[...]) + b2_ref[...]))
    o_ref[...] = jnp.sin(f * (dot(a, w3_ref[...]) + b3_ref[...]))


def _hidden_call(pe, w1p, b1, w2, b2, w3, b3, freq):
    rows, pw = pe.shape
    fw = 2 * w2.shape[0]
    tr = 512
    full = lambda a: pl.BlockSpec(a.shape, lambda t: (0,) * a.ndim)
    two = lambda a: jnp.tile(a.reshape(1, -1), (1, 2))
    diag2 = lambda a: jnp.kron(jnp.eye(2, dtype=a.dtype), a)
    args = (diag2(w1p), two(b1), diag2(w2), two(b2), diag2(w3), two(b3), two(freq))
    return pl.pallas_call(
        _hidden_kernel,
        grid=(rows // tr,),
        in_specs=[pl.BlockSpec((tr, pw), lambda t: (t, 0))] + [full(a) for a in args],
        out_specs=pl.BlockSpec((tr, fw), lambda t: (t, 0)),
        out_shape=jax.ShapeDtypeStruct((rows, fw), F32),
        compiler_params=_cparams(("arbitrary",)),
        name="hidden",
    )(pe, *args)


def _ftime_kernel(a_ref, t_ref, w_ref, dl_ref, h_ref, ss_ref, *, n_lat, tr):
    t = pl.program_id(0)
    h = _bdot(a_ref[...], w_ref[...]) * jnp.exp(-t_ref[...] * dl_ref[...])
    row = lax.broadcasted_iota(jnp.int32, (tr, 1), 0) + t * tr
    h = jnp.where(row == n_lat, 0.0, h)
    h_ref[...] = h.astype(h_ref.dtype)

    @pl.when(t == 0)
    def _():
        ss_ref[...] = jnp.zeros_like(ss_ref)

    ss_ref[...] += jnp.sum(h * h, axis=0, keepdims=True)


def _ftime_call(a3, tcol, wout2, deltas2, n_lat):
    rows = tcol.shape[0]
    fw = a3.shape[1]
    ncol = wout2.shape[2]
    tr = 512
    side = lambda t: (t * tr) // n_lat
    return pl.pallas_call(
        functools.partial(_ftime_kernel, n_lat=n_lat, tr=tr),
        grid=(rows // tr,),
        in_specs=[pl.BlockSpec((tr, fw), lambda t: (t % (n_lat // tr), 0)),
                  pl.BlockSpec((tr, 1), lambda t: (t, 0)),
                  pl.BlockSpec((None, fw, ncol), lambda t: (side(t), 0, 0)),
                  pl.BlockSpec((1, ncol), lambda t: (0, 0))],
        out_specs=[pl.BlockSpec((tr, ncol), lambda t: (t, 0)),
                   pl.BlockSpec((1, ncol), lambda t: (0, 0))],
        out_shape=[jax.ShapeDtypeStruct((rows, ncol), BF16), jax.ShapeDtypeStruct((1, ncol), F32)],
        compiler_params=_cparams(("arbitrary",)),
        name="ftime",
    )(a3, tcol, wout2, deltas2)


@functools.lru_cache(maxsize=None)
def _fft_tables(n_lat):
    n = 2 * n_lat
    n1t = n // FFT_INNER
    h1 = n1t // 2
    n2 = np.arange(FFT_INNER, dtype=np.int64)[:, None, None]
    k1 = np.arange(n1t, dtype=np.int64)[None, :, None]
    n1 = np.arange(h1, dtype=np.int64)[None, None, :]
    ang = (-2.0 * np.pi / n) * ((k1 * (FFT_INNER * n1 + n2)) % n)
    er, ei = np.cos(ang), np.sin(ang)
    e1 = np.concatenate([np.concatenate([er, -ei], axis=2), np.concatenate([ei, er], axis=2)], axis=1)
    e4 = np.transpose(e1, (0, 2, 1))
    sgn = np.where(np.arange(n1t) % 2 == 0, 1.0, -1.0)[None, :, None]
    ef = np.concatenate([np.concatenate([er, sgn * er], axis=2), np.concatenate([ei, sgn * ei], axis=2)], axis=1)
    a = np.arange(FFT_INNER, dtype=np.int64)
    ang2 = (-2.0 * np.pi / FFT_INNER) * ((a[:, None] * a[None, :]) % FFT_INNER)
    gr, gi = np.cos(ang2), np.sin(ang2)
    g2 = np.block([[gr, -gi], [gi, gr]])
    f32 = lambda x: np.ascontiguousarray(x, dtype=np.float32)
    return f32(e1), f32(e4), f32(ef), f32(g2), f32(g2.T)


def _hold(p, j, first, last, nsteps):
    active = (p >= first) & (p <= last) & (lax.rem(p - first, 2) == 0)
    return jnp.where(active, j, jnp.where(p < first, 0, nsteps - 1))


def _slab_read(x_ref, rows):
    return jnp.concatenate([x_ref[c, rows, :] for c in range(x_ref.shape[0])], axis=1)


def _slab_write(x_ref, rows, val):
    for c in range(x_ref.shape[0]):
        x_ref[c, rows, :] = val[:, c * LANES:(c + 1) * LANES]


def _spectrum_steps(x_ref, g2_ref, k1s, n1t, stride):
    cols = [jnp.concatenate([_slab_read(x_ref, pl.ds(k1, FFT_INNER, stride=stride)),
                             _slab_read(x_ref, pl.ds(n1t + k1, FFT_INNER, stride=stride))], axis=0).astype(BF16)
            for k1 in k1s]
    return [jnp.dot(g2_ref[...], c, preferred_element_type=F32) for c in cols]


def _ffft_kernel(h_ref, ef_ref, g2_ref, ss_ref, o_ref, x_ref, *, n1t, grp, kgrp, stride, inv_n):
    p = pl.program_id(1)
    j = pl.program_id(2)

    @pl.when(p == 0)
    def _():
        for j0 in range(0, grp, FFT_INTERLEAVE):
            jjs = list(range(j0, min(j0 + FFT_INTERLEAVE, grp)))
            us = [jnp.concatenate([h_ref[0, jj], h_ref[1, jj]], axis=0) for jj in jjs]
            slabs = [_bdot(ef_ref[jj], u) for jj, u in zip(jjs, us)]
            for jj, slab in zip(jjs, slabs):
                base = pl.multiple_of((j * grp + jj) * stride, SUBLANES)
                _slab_write(x_ref, pl.ds(base, 2 * n1t), slab)

    @pl.when(p == 1)
    def _():
        scale = lax.rsqrt(ss_ref[...] + EPS) * inv_n
        for j0 in range(0, kgrp, FFT_INTERLEAVE):
            jjs = list(range(j0, min(j0 + FFT_INTERLEAVE, kgrp)))
            xfs = _spectrum_steps(x_ref, g2_ref, [j * kgrp + jj for jj in jjs], n1t, stride)
            for jj, xf in zip(jjs, xfs):
                o_ref[jj] = (xf * scale).astype(BF16)


def _fft_groups(n1t, ns):
    ns = min(ns, n1t)
    assert FFT_INNER % ns == 0 and n1t % ns == 0
    return ns, FFT_INNER // ns, n1t // ns


def _ffft_call(hfull, ssq, ef, g2, n_lat):
    ncol = hfull.shape[1]
    n1t = 2 * n_lat // FFT_INNER
    h1 = n1t // 2
    ct = FFT_CHANNELS
    ns, grp, kgrp = _fft_groups(n1t, FILTER_FFT_STEPS)
    stride = 2 * n1t + SLAB_PAD
    hv = hfull.reshape(2, FFT_INNER, h1, ncol)
    nct = ncol // ct
    cpo = nct // 2
    return pl.pallas_call(
        functools.partial(_ffft_kernel, n1t=n1t, grp=grp, kgrp=kgrp, stride=stride, inv_n=1.0 / (2 * n_lat)),
        grid=(nct, 2, ns),
        in_specs=[pl.BlockSpec((2, grp, h1, ct), lambda c, p, j: (0, _hold(p, j, 0, 0, ns), 0, c)),
                  pl.BlockSpec((grp, 2 * n1t, n1t), lambda c, p, j: (_hold(p, j, 0, 0, ns), 0, 0)),
                  pl.BlockSpec((2 * FFT_INNER, 2 * FFT_INNER), lambda c, p, j: (0, 0)),
                  pl.BlockSpec((1, ct), lambda c, p, j: (0, c))],
        out_specs=pl.BlockSpec((None, kgrp, 2 * FFT_INNER, ct),
                               lambda c, p, j: (c // cpo, _hold(p, j, 1, 1, ns), 0, c % cpo)),
        out_shape=jax.ShapeDtypeStruct((2, n1t, 2 * FFT_INNER, ncol // 2), BF16),
        scratch_shapes=[pltpu.VMEM((ct // LANES, FFT_INNER * stride, LANES), F32)],
        compiler_params=_cparams(("arbitrary", "arbitrary", "arbitrary")),
        name="ffft",
    )(hv, ef, g2, ssq)


def _hyena_kernel(v_ref, x1_ref, x2_ref, hf_ref, e1_ref, e4_ref, g2_ref, g3_ref, bias_ref,
                  o_ref, x_ref, z_ref, *, n1t, grp, kgrp, stride):
    p = pl.program_id(1)
    j = pl.program_id(2)
    h1 = n1t // 2

    def gather(ref, jj):
        return jnp.concatenate([ref[0, jj], ref[1, jj]], axis=0).astype(F32)

    def slab_base(jj):
        return pl.multiple_of((j * grp + jj) * stride, SUBLANES)

    def time_rows(jjs):
        slabs = [_slab_read(x_ref, pl.ds(slab_base(jj), 2 * n1t)).astype(BF16) for jj in jjs]
        return [jnp.dot(e4_ref[jj], s, preferred_element_type=F32) for jj, s in zip(jjs, slabs)]

    def write_slabs(jjs, us):
        slabs = [jnp.dot(e1_ref[jj], u, preferred_element_type=F32) for jj, u in zip(jjs, us)]
        for jj, slab in zip(jjs, slabs):
            _slab_write(x_ref, pl.ds(slab_base(jj), 2 * n1t), slab)

    def chunks(n):
        return [list(range(j0, min(j0 + FFT_INTERLEAVE, n))) for j0 in range(0, n, FFT_INTERLEAVE)]

    @pl.when(p == 0)
    def _():
        for jjs in chunks(grp):
            write_slabs(jjs, [gather(v_ref, jj).astype(BF16) for jj in jjs])

    @pl.when((p == 1) | (p == 3))
    def _():
        for jjs in chunks(kgrp):
            k1s = [j * kgrp + jj for jj in jjs]
            xfs = _spectrum_steps(x_ref, g2_ref, k1s, n1t, stride)
            ys = []
            for jj, xf in zip(jjs, xfs):
                hf = hf_ref[jj].astype(F32)
                xr, xi = xf[:FFT_INNER], xf[FFT_INNER:]
                hr, hi = hf[:FFT_INNER], hf[FFT_INNER:]
                ys.append(jnp.concatenate([xr * hr - xi * hi, xr * hi + xi * hr], axis=0).astype(BF16))
            bbs = [jnp.dot(g3_ref[...], y, preferred_element_type=F32) for y in ys]
            for k1, bb in zip(k1s, bbs):
                _slab_write(x_ref, pl.ds(k1, FFT_INNER, stride=stride), bb[:FFT_INNER])
                _slab_write(x_ref, pl.ds(n1t + k1, FFT_INNER, stride=stride), bb[FFT_INNER:])

    @pl.when(p == 2)
    def _():
        for jjs in chunks(grp):
            z1s = [(gather(x1_ref, jj) * (y + gather(v_ref, jj) * bias_ref[0:1, :])).astype(BF16)
                   for jj, y in zip(jjs, time_rows(jjs))]
            for jj, z1 in zip(jjs, z1s):
                z_ref[j * grp + jj] = z1
            write_slabs(jjs, z1s)

    @pl.when(p == 4)
    def _():
        for jjs in chunks(grp):
            for jj, y in zip(jjs, time_rows(jjs)):
                out = gather(x2_ref, jj) * (y + z_ref[j * grp + jj].astype(F32) * bias_ref[1:2, :])
                o_ref[0, jj] = out[:h1].astype(o_ref.dtype)
                o_ref[1, jj] = out[h1:].astype(o_ref.dtype)


def _hyena_call(hv, hf, e1, e4, g2, g3, bias):
    _, bsz, _, h1, ch = hv.shape
    assert bsz == 2, "the two batch rows are packed as the real and imaginary parts of one FFT"
    n1t = 2 * h1
    ct = FFT_CHANNELS
    ns, grp, kgrp = _fft_groups(n1t, CONV_FFT_STEPS)
    stride = 2 * n1t + SLAB_PAD

    def hspec(which, first, last):
        return pl.BlockSpec((None, bsz, grp, h1, ct),
                            lambda c, p, j: (which, 0, _hold(p, j, first, last, ns), 0, c))

    def hf_map(c, p, j):
        order = jnp.where(p >= 2, 1, 0)
        step = jnp.where((p == 1) | (p == 3), j, jnp.where((p == 0) | (p == 2), 0, ns - 1))
        return (order, step, 0, c)

    return pl.pallas_call(
        functools.partial(_hyena_kernel, n1t=n1t, grp=grp, kgrp=kgrp, stride=stride),
        grid=(ch // ct, 5, ns),
        in_specs=[hspec(0, 0, 2), hspec(1, 2, 2), hspec(2, 4, 4),
                  pl.BlockSpec((None, kgrp, 2 * FFT_INNER, ct), hf_map),
                  pl.BlockSpec((grp, 2 * n1t, n1t), lambda c, p, j: (_hold(p, j, 0, 2, ns), 0, 0)),
                  pl.BlockSpec((grp, n1t, 2 * n1t), lambda c, p, j: (_hold(p, j, 2, 4, ns), 0, 0)),
                  pl.BlockSpec((2 * FFT_INNER, 2 * FFT_INNER), lambda c, p, j: (0, 0)),
                  pl.BlockSpec((2 * FFT_INNER, 2 * FFT_INNER), lambda c, p, j: (0, 0)),
                  pl.BlockSpec((2, ct), lambda c, p, j: (0, c))],
        out_specs=pl.BlockSpec((bsz, grp, h1, ct), lambda c, p, j: (0, _hold(p, j, 4, 4, ns), 0, c)),
        out_shape=jax.ShapeDtypeStruct((bsz, FFT_INNER, h1, ch), BF16),
        scratch_shapes=[pltpu.VMEM((ct // LANES, FFT_INNER * stride, LANES), F32),
                        pltpu.VMEM((FFT_INNER, n1t, ct), BF16)],
        compiler_params=_cparams(("arbitrary", "arbitrary", "arbitrary"), VMEM_LIMIT_CONV),
        name="hyena",
    )(hv, hv, hv, hf, e1, e4, g2, g3, bias)


def _merge_kernel(of_ref, ob_ref, gp_ref, hy_ref, x_ref, mod_ref, dng_ref, wpa_ref, wpb_ref, wo_ref,
                  fg_ref, o_ref, *, d):
    b = pl.program_id(0)
    blocks = [slice(r * MXU_TILE, (r + 1) * MXU_TILE) for r in range(o_ref.shape[1] // MXU_TILE)]
    f32 = lambda ref, *idx: ref[idx].astype(F32)

    def head_norm(rows):
        o = f32(of_ref, 0, rows) + f32(ob_ref, 0, rows)
        za = f32(gp_ref, 0, 0, rows)
        parts = []
        for hh in range(HEADS):
            hs = slice(hh * HEAD_DIM, (hh + 1) * HEAD_DIM)
            oh = o[:, hs]
            ms = jnp.mean(oh * oh, axis=-1, keepdims=True)
            parts.append((oh * lax.rsqrt(ms + EPS) * dng_ref[...] * za[:, hs]).astype(BF16))
        return jnp.concatenate(parts, axis=1)

    o_a = [head_norm(rows) for rows in blocks]
    y_a = [jnp.dot(a, wpa_ref[...], preferred_element_type=F32) for a in o_a]
    o_b = [(f32(hy_ref, 0, rows) * f32(gp_ref, 1, 0, rows)).astype(BF16) for rows in blocks]
    y_b = [jnp.dot(a, wpb_ref[...], preferred_element_type=F32) for a in o_b]
    m = [(f32(gp_ref, 2, 0, rows) * ya + f32(gp_ref, 3, 0, rows) * yb).astype(BF16)
         for rows, ya, yb in zip(blocks, y_a, y_b)]
    y = [jnp.dot(a, wo_ref[...], preferred_element_type=F32) for a in m]
    gate = mod_ref[pl.ds(b, 1), :][:, 2 * d:3 * d]
    for rows, yr in zip(blocks, y):
        xn = x_ref[0, rows, :] + gate * yr
        o_ref[0, rows, :] = xn * lax.rsqrt(jnp.mean(xn * xn, axis=-1, keepdims=True) + EPS) * fg_ref[...]


def _merge_call(o_f, o_b, gp, hy, x, mod, dn_norm_g, w_pa, w_pb, w_out, final_g):
    bsz, n_lat, d = x.shape
    tm = NORM_TILE if n_lat % NORM_TILE == 0 else 256
    wfull = lambda a: pl.BlockSpec(a.shape, lambda b, t: (0,) * a.ndim)
    tok = pl.BlockSpec((1, tm, d), lambda b, t: (b, t, 0))
    return pl.pallas_call(
        functools.partial(_merge_kernel, d=d),
        grid=(bsz, n_lat // tm),
        in_specs=[tok, tok,
                  pl.BlockSpec((4, 1, tm, d), lambda b, t: (0, b, t, 0)),
                  tok, tok, wfull(mod),
                  pl.BlockSpec((1, HEAD_DIM), lambda b, t: (0, 0)),
                  wfull(w_pa), wfull(w_pb), wfull(w_out),
                  pl.BlockSpec((1, d), lambda b, t: (0, 0))],
        out_specs=tok,
        out_shape=jax.ShapeDtypeStruct((bsz, n_lat, d), F32),
        compiler_params=_cparams(("arbitrary", "arbitrary")),
        name="merge",
    )(o_f, o_b, gp, hy, x, mod, dn_norm_g.reshape(1, HEAD_DIM), w_pa, w_pb, w_out, final_g.reshape(1, d))


def _position_features(n_lat):
    h1 = n_lat // FFT_INNER
    m = (np.arange(FFT_INNER)[:, None] + FFT_INNER * np.arange(h1)[None, :]).reshape(-1)
    lag = np.concatenate([m, n_lat - m]).astype(np.float64)
    lag[n_lat] = 0.0
    bands = (HY_EMB - 1) // 2
    t = (lag / (n_lat - 1))[:, None]
    wpos = (2.0 * math.pi / n_lat) * lag[:, None]
    fb = np.linspace(1e-4, bands - 1, bands)[None, :]
    pe = np.concatenate([t, np.cos(fb * wpos), -np.sin(fb * wpos)], axis=1)
    pe_pad = np.zeros((2 * n_lat, LANES), np.float32)
    pe_pad[:, :HY_EMB] = pe
    return pe_pad, np.ascontiguousarray(t, dtype=np.float32)


def kernel(x, c, ctx, c_ctx, w_mod, b_mod, norm_g, w_in, dn_conv_w, dn_a_log, dn_dt_bias, dn_norm_g,
           hy_conv_w, hy_conv_b, hy_f_w1, hy_f_b1, hy_f_w2, hy_f_b2, hy_f_w3, hy_f_b3, hy_f_wout,
           hy_f_freq, hy_bias, w_pa, w_pb, w_out, final_g):
    bsz, n_lat, d = x.shape
    n_ctx = ctx.shape[1]
    assert w_mod.shape[0] == 1, "single layer: the context stream is only read through its scan states"
    dn = HEADS * HEAD_DIM
    hy = hy_bias.shape[-1]
    assert n_lat % 256 == 0 and n_ctx % 256 == 0 and dn == d and hy == d

    cvec = jnp.zeros((SUBLANES, d), F32).at[:bsz].set(c).at[bsz].set(c_ctx)
    mod = _mod_call(cvec, w_mod[0], b_mod[0])

    w = w_in[0].astype(BF16)
    o_qkv, o_gate, o_hy, o_gp = 0, 4 * dn, 4 * dn + 4 * HEADS, 4 * dn + 4 * HEADS + 3 * hy
    col3 = lambda a, off, n: jnp.transpose(a[:, off:off + 3 * n].reshape(a.shape[0], 3, n), (1, 0, 2))
    lt = n_lat + n_ctx
    tm = PROJ_TILE if n_lat % PROJ_TILE == 0 else 256

    wg = jnp.zeros((d, LANES), BF16).at[:, :4 * HEADS].set(w[:, o_gate:o_gate + 4 * HEADS])
    lane_pad = lambda a: jnp.zeros((1, LANES), F32).at[0, 2 * HEADS:4 * HEADS].set(a.reshape(-1))
    gate_args = (wg, lane_pad(dn_a_log[0]), lane_pad(dn_dt_bias[0]))
    h_tok, gates = _hnorm_call(x, mod, norm_g[0], *gate_args, tm=NORM_TILE, gate_rows=lt, gate_tile0=0, name="hnorm")
    h_ctx, gates = _hnorm_call(ctx, mod, norm_g[0], *gate_args, tm=n_ctx, gate_rows=lt, gate_tile0=n_lat // n_ctx,
                               mod_row=bsz, prev_gates=gates, name="hnorm_ctx")
    h1 = n_lat // FFT_INNER
    h_perm = jnp.transpose(h_tok.reshape(bsz, h1, FFT_INNER, d), (0, 2, 1, 3)).reshape(bsz, n_lat, d)

    w_qkv, cw_qkv, no_bias = col3(w, o_qkv, dn), col3(dn_conv_w[0], 0, dn), jnp.zeros((3, 1, dn), F32)
    qkv = _convproj_call(h_tok, w_qkv, cw_qkv, no_bias, rowlen=GRID_W, tm=tm, tile0=0, out_rows=lt, qk_norm=True,
                         name="qkv")
    qkv = _convproj_call(h_ctx, w_qkv, cw_qkv, no_bias, rowlen=n_ctx, tm=n_ctx, tile0=n_lat // n_ctx, out_rows=lt,
                         qk_norm=True, prev=qkv, name="qkv_ctx")
    gates_t = jnp.transpose(gates[:, :, :4 * HEADS].reshape(bsz, lt // CHUNK, CHUNK, 4 * HEADS), (0, 1, 3, 2))
    tall, wide, el = _prep_call(qkv, gates, gates_t)
    o_f, o_b = _scan_call(tall, wide, el, n_lat, n_ctx)

    hyp = _hyproj_call(h_perm, col3(w, o_hy, hy), col3(hy_conv_w[0], 0, hy), hy_conv_b[0].reshape(3, 1, hy))
    w4 = jnp.stack([w[:, 3 * dn:4 * dn]] + [w[:, o_gp + i * d:o_gp + (i + 1) * d] for i in range(3)])
    gp = _gproj_call(h_tok, w4, n_lat)

    pe, tcol = _position_features(n_lat)
    w1p = jnp.zeros((LANES, hy_f_w1.shape[-1]), F32).at[:HY_EMB].set(hy_f_w1[0])
    pe2 = jnp.asarray(np.concatenate([pe[:n_lat], pe[n_lat:]], axis=1))
    a3 = _hidden_call(pe2, w1p, hy_f_b1[0], hy_f_w2[0], hy_f_b2[0], hy_f_w3[0], hy_f_b3[0], hy_f_freq[0])
    deltas = np.abs(np.linspace(math.log(HY_DECAY_TARGET) / HY_SLOW_DECAY,
                                math.log(HY_DECAY_TARGET) / HY_FAST_DECAY, hy)).astype(np.float32)
    deltas2 = jnp.asarray(np.tile(deltas, 2)[None, :])
    wside = hy_f_wout[0].reshape(-1, 2, 2 * hy).transpose(1, 0, 2).astype(BF16)
    zside = jnp.zeros_like(wside[0])
    wout2 = jnp.stack([jnp.concatenate([wside[0], zside]), jnp.concatenate([zside, wside[1]])])
    hfull, ssq = _ftime_call(a3, jnp.asarray(tcol), wout2, deltas2, n_lat)
    e1, e4, ef, g2, g3 = (jnp.asarray(t).astype(BF16) for t in _fft_tables(n_lat))
    hf = _ffft_call(hfull, ssq, ef, g2, n_lat)

    yh = _hyena_call(hyp, hf, e1, e4, g2, g3, hy_bias[0])
    yh = jnp.transpose(yh, (0, 2, 1, 3)).reshape(bsz, n_lat, hy)

    return _merge_call(o_f, o_b, gp, yh, x, mod, dn_norm_g[0], w_pa[0].astype(BF16), w_pb[0].astype(BF16),
                       w_out[0].astype(BF16), final_g)
```

```python
import functools
import math

import numpy as np
import jax
import jax.numpy as jnp
from jax import lax
from jax.experimental import pallas as pl
from jax.experimental.pallas import tpu as pltpu

F32 = jnp.float32
BF16 = jnp.bfloat16
HIGHEST = lax.Precision.HIGHEST

EPS = 1e-6
HEADS = 8
HEAD_DIM = 128
CHUNK = 64
GRID_W = 64
HY_EMB = 33
HY_DECAY_TARGET = 1e-2
HY_FAST_DECAY = 0.3
HY_SLOW_DECAY = 1.5

LANES = 128
SUBLANES = 8
FFT_INNER = 128
CONV_FFT_STEPS = 8
FILTER_FFT_STEPS = 8
FFT_CHANNELS = 256
FFT_INTERLEAVE = 8
PROJ_TILE = 1024
NORM_TILE = 512
SCAN_CHUNKS = 4
PREP_CHUNKS = 1
HYPROJ_N1 = 16
MXU_TILE = 256
SLAB_PAD = 8
VMEM_LIMIT = 56 * 1024 * 1024
VMEM_LIMIT_CONV = 62 * 1024 * 1024


def _cparams(sem, vmem=VMEM_LIMIT):
    return pltpu.CompilerParams(dimension_semantics=sem, vmem_limit_bytes=vmem)


def _bdot(a, b):
    return jnp.dot(a.astype(BF16), b.astype(BF16), preferred_element_type=F32)


def _silu(x):
    return x * jax.nn.sigmoid(x)


def _mod_kernel(c_ref, w_ref, b_ref, o_ref):
    s = _silu(c_ref[...])
    o_ref[...] = jnp.dot(s, w_ref[...], precision=HIGHEST, preferred_element_type=F32) + b_ref[...]


def _mod_call(cvec, w_mod, b_mod):
    rows, d = cvec.shape
    n = w_mod.shape[1]
    tn = 1024
    return pl.pallas_call(
        _mod_kernel,
        grid=(n // tn,),
        in_specs=[pl.BlockSpec((rows, d), lambda j: (0, 0)),
                  pl.BlockSpec((d, tn), lambda j: (0, j)),
                  pl.BlockSpec((1, tn), lambda j: (0, j))],
        out_specs=pl.BlockSpec((rows, tn), lambda j: (0, j)),
        out_shape=jax.ShapeDtypeStruct((rows, n), F32),
        compiler_params=_cparams(("arbitrary",)),
        name="mod",
    )(cvec, w_mod, b_mod.reshape(1, n))


def _norm_rows(x, m, g_ref, d):
    y = x * lax.rsqrt(jnp.mean(x * x, axis=-1, keepdims=True) + EPS) * g_ref[...]
    return (y * (1.0 + m[:, d:2 * d]) + m[:, 0:d]).astype(BF16)


def _gate_rows(h, w_ref, alog_ref, dtb_ref):
    z = jnp.dot(h, w_ref[...], preferred_element_type=F32)
    lane = lax.broadcasted_iota(jnp.int32, z.shape, 1)
    u = z + dtb_ref[...]
    softplus = jnp.maximum(u, 0.0) + jnp.log(1.0 + jnp.exp(-jnp.abs(u)))
    return jnp.where(lane < 2 * HEADS, jax.nn.sigmoid(z), -jnp.exp(alog_ref[...]) * softplus)


def _hnorm_kernel(tok_ref, mod_ref, g_ref, w_ref, alog_ref, dtb_ref, *rest, d, mod_row):
    o_ref, og_ref = rest[-2:]
    row = pl.program_id(0) if mod_row is None else mod_row
    m = mod_ref[pl.ds(row, 1), :]
    h = _norm_rows(tok_ref[0], m, g_ref, d)
    o_ref[0] = h
    og_ref[0] = _gate_rows(h, w_ref, alog_ref, dtb_ref)


def _hnorm_call(tok, mod, norm_g, wg, alog, dtb, *, tm, gate_rows, gate_tile0, mod_row=None, prev_gates=None, name):
    bsz, rows, d = tok.shape
    full = lambda a: pl.BlockSpec(a.shape, lambda b, t: (0,) * a.ndim)
    args = [tok, mod, norm_g.reshape(1, d), wg, alog, dtb]
    in_specs = [pl.BlockSpec((1, tm, d), lambda b, t: (b, t, 0))] + [full(a) for a in args[1:]]
    aliases = {}
    if prev_gates is not None:
        in_specs.append(pl.BlockSpec(memory_space=pl.ANY))
        args.append(prev_gates)
        aliases = {6: 1}
    return pl.pallas_call(
        functools.partial(_hnorm_kernel, d=d, mod_row=mod_row),
        grid=(bsz, rows // tm),
        in_specs=in_specs,
        out_specs=[pl.BlockSpec((1, tm, d), lambda b, t: (b, t, 0)),
                   pl.BlockSpec((1, tm, LANES), lambda b, t: (b, t + gate_tile0, 0))],
        out_shape=[jax.ShapeDtypeStruct((bsz, rows, d), BF16), jax.ShapeDtypeStruct((bsz, gate_rows, LANES), F32)],
        input_output_aliases=aliases,
        compiler_params=_cparams(("arbitrary", "arbitrary")),
        name=name,
    )(*args)


@functools.lru_cache(maxsize=None)
def _shift_matrix(rowlen, taps, blk):
    t = np.arange(blk)
    mats = []
    for j in range(taps):
        d = j - taps // 2
        if d != 0:
            ok = ((t + d) >= 0) & ((t + d) < blk) & (t // rowlen == (t + d) // rowlen)
            s = np.zeros((blk, blk), np.float32)
            s[t[ok], t[ok] + d] = 1.0
            mats.append(s)
    return np.concatenate(mats, axis=1)


def _shift_block(rowlen, taps):
    return max(rowlen, MXU_TILE // (taps - 1) // rowlen * rowlen)


def _convproj_kernel(h_ref, w_ref, cw_ref, sh_ref, *rest, tm):
    o_ref = rest[-1]
    j = pl.program_id(0)
    taps = cw_ref.shape[1]
    mid = taps // 2
    blk = sh_ref.shape[0]
    mblk = max(blk, MXU_TILE)
    na = jnp.where(j == 0, float(HEAD_DIM), jnp.where(j == 1, 1.0, 0.0))
    nb = jnp.where(j == 0, EPS * HEAD_DIM, jnp.where(j == 1, EPS, 1.0))
    cw16 = cw_ref[0].astype(BF16)
    for r in range(tm // mblk):
        zm = jnp.dot(h_ref[0, r * mblk:(r + 1) * mblk, :], w_ref[0], preferred_element_type=F32)
        for g in range(mblk // blk):
            rows = slice(r * mblk + g * blk, r * mblk + (g + 1) * blk)
            zb = zm[g * blk:(g + 1) * blk]
            zb16 = zb.astype(BF16)
            side = jnp.concatenate([zb16 * cw16[jt:jt + 1, :] for jt in range(taps) if jt != mid], axis=0)
            y = zb * cw_ref[0, mid:mid + 1, :] + jnp.dot(sh_ref[...], side, preferred_element_type=F32)
            y = _silu(y)
            for hh in range(HEADS):
                yh = y[:, hh * HEAD_DIM:(hh + 1) * HEAD_DIM]
                nrm = jnp.sum(yh * yh, axis=-1, keepdims=True)
                o_ref[0, 0, rows, hh * HEAD_DIM:(hh + 1) * HEAD_DIM] = (
                    yh * lax.rsqrt(nrm * na + nb)).astype(o_ref.dtype)


def _convproj_call(h, w3, cw3, *, rowlen, tm, tile0, out_rows, prev=None, name):
    bsz, rows, d = h.shape
    ntiles = rows // tm
    nw = w3.shape[2]
    taps = cw3.shape[1]
    blk = _shift_block(rowlen, taps)
    assert tm % max(blk, MXU_TILE) == 0 and max(blk, MXU_TILE) % blk == 0
    sh = jnp.asarray(_shift_matrix(rowlen, taps, blk)).astype(BF16)
    in_specs = [pl.BlockSpec((1, tm, d), lambda j, b, t: (b, t, 0)),
                pl.BlockSpec((1, d, nw), lambda j, b, t: (j, 0, 0)),
                pl.BlockSpec((1, cw3.shape[1], nw), lambda j, b, t: (j, 0, 0)),
                pl.BlockSpec(sh.shape, lambda j, b, t: (0, 0))]
    args = [h, w3, cw3, sh]
    aliases = {}
    if prev is not None:
        in_specs.append(pl.BlockSpec(memory_space=pl.ANY))
        args.append(prev)
        aliases = {4: 0}
    return pl.pallas_call(
        functools.partial(_convproj_kernel, tm=tm),
        grid=(3, bsz, ntiles),
        in_specs=in_specs,
        out_specs=pl.BlockSpec((1, 1, tm, nw), lambda j, b, t: (j, b, t + tile0, 0)),
        out_shape=jax.ShapeDtypeStruct((3, bsz, out_rows, nw), BF16),
        input_output_aliases=aliases,
        compiler_params=_cparams(("arbitrary", "arbitrary", "arbitrary")),
        name=name,
    )(*args)


def _hyproj_kernel(h_ref, w_ref, cw_ref, cb_ref, o_ref):
    n2s, n1s, d = h_ref.shape[1:]
    taps = cw_ref.shape[1]
    hrows = h_ref[0].reshape(n2s * n1s, d)
    step = min(MXU_TILE, n2s * n1s)
    z = jnp.concatenate([jnp.dot(hrows[r:r + step], w_ref[0], preferred_element_type=F32)
                         for r in range(0, n2s * n1s, step)], axis=0)
    y = cb_ref[0] + z * cw_ref[0, taps // 2:taps // 2 + 1, :]
    for jt in range(taps):
        s = (jt - taps // 2) * n1s
        if s > 0:
            y = y + jnp.concatenate([z[s:], jnp.zeros((s, z.shape[1]), F32)], axis=0) * cw_ref[0, jt:jt + 1, :]
        elif s < 0:
            y = y + jnp.concatenate([jnp.zeros((-s, z.shape[1]), F32), z[:s]], axis=0) * cw_ref[0, jt:jt + 1, :]
    o_ref[0, 0] = y.astype(o_ref.dtype).reshape(n2s, n1s, y.shape[1])


def _hyproj_call(h_perm, w3, cw3, cb3):
    bsz, n_lat, d = h_perm.shape
    nw = w3.shape[2]
    h1 = n_lat // FFT_INNER
    n1s = min(HYPROJ_N1, h1)
    hv = h_perm.reshape(bsz, FFT_INNER, h1, d)
    return pl.pallas_call(
        _hyproj_kernel,
        grid=(3, bsz, FFT_INNER // GRID_W, h1 // n1s),
        in_specs=[pl.BlockSpec((1, GRID_W, n1s, d), lambda j, b, r, t: (b, r, t, 0)),
                  pl.BlockSpec((1, d, nw), lambda j, b, r, t: (j, 0, 0)),
                  pl.BlockSpec((1, cw3.shape[1], nw), lambda j, b, r, t: (j, 0, 0)),
                  pl.BlockSpec((1, 1, nw), lambda j, b, r, t: (j, 0, 0))],
        out_specs=pl.BlockSpec((1, 1, GRID_W, n1s, nw), lambda j, b, r, t: (j, b, r, t, 0)),
        out_shape=jax.ShapeDtypeStruct((3, bsz, FFT_INNER, h1, nw), BF16),
        compiler_params=_cparams(("arbitrary",) * 4),
        name="hyproj",
    )(hv, w3, cw3, cb3)


def _gproj_kernel(h_ref, w_ref, o_ref):
    j = pl.program_id(0)
    for r in range(h_ref.shape[1] // MXU_TILE):
        rows = slice(r * MXU_TILE, (r + 1) * MXU_TILE)
        z = jnp.dot(h_ref[0, rows, :], w_ref[0], preferred_element_type=F32)
        s = jax.nn.sigmoid(z)
        o_ref[0, 0, rows, :] = jnp.where(j < 2, z * s, s).astype(BF16)


def _gproj_call(h, w4, n_lat):
    bsz, _, d = h.shape
    nw = w4.shape[2]
    tm = PROJ_TILE if n_lat % PROJ_TILE == 0 else 256
    return pl.pallas_call(
        _gproj_kernel,
        grid=(4, bsz, n_lat // tm),
        in_specs=[pl.BlockSpec((1, tm, d), lambda j, b, t: (b, t, 0)),
                  pl.BlockSpec((1, d, nw), lambda j, b, t: (j, 0, 0))],
        out_specs=pl.BlockSpec((1, 1, tm, nw), lambda j, b, t: (j, b, t, 0)),
        out_shape=jax.ShapeDtypeStruct((4, bsz, n_lat, nw), BF16),
        compiler_params=_cparams(("arbitrary", "arbitrary", "arbitrary")),
        name="gproj",
    )(h, w4)


def _unit_tri_inverse(ls, lowers, filler):
    c = ls[0].shape[0]
    ri = lax.broadcasted_iota(jnp.int32, (c, c), 0)
    ci = lax.broadcasted_iota(jnp.int32, (c, c), 1)
    same = lambda n: (ri // n) == (ci // n)
    eye = jnp.where(ri == ci, 1.0, 0.0)
    in2 = same(2)
    ts = [eye - jnp.where(in2, l, 0.0) for l in ls]
    n = 2
    nlevels = c.bit_length() - 2
    level = 0
    while n < c:
        blk = same(2 * n) & jnp.logical_not(same(n))
        offs = [jnp.where(blk, l, 0.0).astype(BF16) for l in ls]
        if n % SUBLANES:
            prods = [_bdot(t, off) for t, off in zip(ts, offs)]
            filler(level, nlevels)
            ts = [t - _bdot(p, t) for p, t in zip(prods, ts)]
        else:
            def moving(t, lower):
                first = n if lower else 0
                return jnp.concatenate([t[r:r + n] for r in range(first, c, 2 * n)], axis=0)

            def merged(t, upd, lower):
                pieces, k = [], 0
                for r in range(0, c, n):
                    if ((r // n) % 2 == 1) == lower:
                        pieces.append(upd[k * n:(k + 1) * n])
                        k += 1
                    else:
                        pieces.append(t[r:r + n])
                return jnp.concatenate(pieces, axis=0)

            rows = [moving(t, lo) for t, lo in zip(ts, lowers)]
            prods = [_bdot(r, off) for r, off in zip(rows, offs)]
            filler(level, nlevels)
            upds = [r - _bdot(p, t) for r, p, t in zip(rows, prods, ts)]
            ts = [merged(t, u, lo) for t, u, lo in zip(ts, upds, lowers)]
        n *= 2
        level += 1
    return ts


_NT = (((1,), (1,)), ((), ()))
_TN = (((0,), (0,)), ((), ()))


def _prep_kernel(qkv_ref, g_ref, gt_ref, tall_ref, wide_ref, el_ref, *, bsz):
    c = CHUNK
    ri = lax.broadcasted_iota(jnp.int32, (c, c), 0)
    ci = lax.broadcasted_iota(jnp.int32, (c, c), 1)
    tri_l = jnp.where(ri >= ci, 1.0, 0.0)
    tri_u = jnp.where(ri <= ci, 1.0, 0.0)
    incl = (ri >= ci, ri <= ci)
    strict = (ri > ci, ri < ci)
    hdot = functools.partial(jnp.dot, precision=HIGHEST, preferred_element_type=F32)

    per = tall_ref.shape[0]
    cb = [(ch, b) for ch in range(per) for b in range(bsz)]
    rows = lambda ch: slice(ch * c, (ch + 1) * c)
    gs = [g_ref[b, rows(ch), :] for ch, b in cb]
    gts = [gt_ref[b, ch] for ch, b in cb]
    gcol = [(hdot(tri_l, g), hdot(tri_u, g)) for g in gs]
    grow = [(hdot(gt, tri_u), hdot(gt, tri_l)) for gt in gts]
    glast = [(gc[0][c - 1:c], gc[1][0:1]) for gc in gcol]
    el_ref[...] = jnp.zeros_like(el_ref)
    for p, (ch, b) in enumerate(cb):
        for d in range(2):
            el_ref[ch, d * bsz + b:d * bsz + b + 1, :] = jnp.exp(glast[p][d])

    pbh = [(p, hh) for p in range(len(cb)) for hh in range(HEADS)]
    hs = lambda hh: slice(hh * HEAD_DIM, (hh + 1) * HEAD_DIM)
    tok = lambda w, p, hh: qkv_ref[w, cb[p][1], rows(cb[p][0]), hs(hh)]
    ks = [tok(1, p, hh) for p, hh in pbh]
    qs = [tok(0, p, hh) for p, hh in pbh]
    kk = [lax.dot_general(k, k, _NT, preferred_element_type=F32) for k in ks]
    qk = [lax.dot_general(q, k, _NT, preferred_element_type=F32) for q, k in zip(qs, ks)]
    inst = [(d, p, hh, i) for d in range(2) for i, (p, hh) in enumerate(pbh)]
    decs, betas, gcs = [], [], []
    for d, p, hh, _ in inst:
        lg = 2 * HEADS + d * HEADS + hh
        gcs.append(gcol[p][d][:, lg:lg + 1])
        betas.append(gs[p][:, d * HEADS + hh:d * HEADS + hh + 1])
        decs.append(jnp.exp(jnp.where(incl[d], gcs[-1] - grow[p][d][lg:lg + 1, :], -jnp.inf)))
    ls = [jnp.where(strict[d], kk[i] * beta * dec, 0.0) for (d, _, _, i), beta, dec in zip(inst, betas, decs)]

    def scaled_operands(part, nparts):
        for (d, p, hh, i), dec, beta, gc in list(zip(inst, decs, betas, gcs))[part::nparts]:
            ch, b = cb[p]
            lg = 2 * HEADS + d * HEADS + hh
            eg = jnp.exp(gc)
            k = ks[i].astype(F32)
            tall_ref[ch, d, b, hh, :, c:2 * c] = (qk[i] * dec).astype(BF16)
            wide_ref[ch, d, b, hh, 0:c, :] = (qs[i].astype(F32) * eg).astype(BF16)
            wide_ref[ch, d, b, hh, c:2 * c, :] = (k * (beta * eg)).astype(BF16)
            wide_ref[ch, d, b, hh, 2 * c:3 * c, :] = (tok(2, p, hh).astype(F32) * beta).astype(BF16)
            wide_ref[ch, d, b, hh, 3 * c:4 * c, :] = (k * jnp.exp(glast[p][d][:, lg:lg + 1] - gc)).astype(BF16)

    ts = _unit_tri_inverse(ls, [d == 0 for d, _, _, _ in inst], scaled_operands)
    for (d, p, hh, _), t in zip(inst, ts):
        tall_ref[cb[p][0], d, cb[p][1], hh, :, 0:c] = t.astype(BF16)


def _prep_call(qkv, gates, gates_t):
    _, bsz, lt, dn = qkv.shape
    nc = lt // CHUNK
    per = PREP_CHUNKS
    assert nc % per == 0
    tall = (2, bsz, HEADS, CHUNK, 2 * CHUNK)
    wide = (2, bsz, HEADS, 4 * CHUNK, HEAD_DIM)
    return pl.pallas_call(
        functools.partial(_prep_kernel, bsz=bsz),
        grid=(nc // per,),
        in_specs=[pl.BlockSpec((3, bsz, per * CHUNK, dn), lambda i: (0, 0, i, 0)),
                  pl.BlockSpec((bsz, per * CHUNK, LANES), lambda i: (0, i, 0)),
                  pl.BlockSpec((bsz, per, 4 * HEADS, CHUNK), lambda i: (0, i, 0, 0))],
        out_specs=[pl.BlockSpec((per,) + tall, lambda i: (i, 0, 0, 0, 0, 0)),
                   pl.BlockSpec((per,) + wide, lambda i: (i, 0, 0, 0, 0, 0)),
                   pl.BlockSpec((per, SUBLANES, LANES), lambda i: (i, 0, 0))],
        out_shape=[jax.ShapeDtypeStruct((nc,) + tall, BF16),
                   jax.ShapeDtypeStruct((nc,) + wide, BF16),
                   jax.ShapeDtypeStruct((nc, SUBLANES, LANES), F32)],
        compiler_params=_cparams(("arbitrary",)),
        name="prep",
    )(qkv, gates, gates_t)


def _scan_kernel(tf_ref, tb_ref, wf_ref, wb_ref, ef_ref, eb_ref, of_ref, ob_ref, s_ref, *, bsz):
    i = pl.program_id(0)

    @pl.when(i == 0)
    def _():
        s_ref[...] = jnp.zeros_like(s_ref)

    c = CHUNK
    per = tf_ref.shape[0]
    dirs = ((tf_ref, wf_ref, ef_ref, of_ref), (tb_ref, wb_ref, eb_ref, ob_ref))
    inst = [(d, b, hh) for d in range(2) for b in range(bsz) for hh in range(HEADS)]
    state = [s_ref[d, b, hh] for d, b, hh in inst]
    for sub in range(per):
        at = (sub, per - 1 - sub)
        a1 = [jnp.dot(dirs[d][1][at[d], b, hh, 0:2 * c, :], s.astype(BF16), preferred_element_type=F32)
              for (d, b, hh), s in zip(inst, state)]
        v_new = [jnp.dot(dirs[d][0][at[d], b, hh, :, 0:c],
                         (dirs[d][1][at[d], b, hh, 2 * c:3 * c, :].astype(F32) - a[c:]).astype(BF16),
                         preferred_element_type=F32).astype(BF16) for (d, b, hh), a in zip(inst, a1)]
        new_state = []
        for (d, b, hh), a, vn, s in zip(inst, a1, v_new, state):
            lg = 2 * HEADS + d * HEADS + hh
            o = a[:c] + jnp.dot(dirs[d][0][at[d], b, hh, :, c:2 * c], vn, preferred_element_type=F32)
            dirs[d][3][b, at[d] * c:(at[d] + 1) * c, hh * HEAD_DIM:(hh + 1) * HEAD_DIM] = o.astype(BF16)
            el = dirs[d][2][at[d], d * bsz + b:d * bsz + b + 1, lg:lg + 1]
            new_state.append(s * el + lax.dot_general(dirs[d][1][at[d], b, hh, 3 * c:4 * c, :], vn, _TN,
                                                      preferred_element_type=F32))
        state = new_state
    for (d, b, hh), s in zip(inst, state):
        s_ref[d, b, hh] = s


def _scan_call(tall, wide, el, n_lat, n_ctx):
    nc, _, bsz = tall.shape[:3]
    dn = HEADS * HEAD_DIM
    per = SCAN_CHUNKS
    ncx, ncc, ns = n_lat // CHUNK // per, n_ctx // CHUNK // per, nc // per
    assert nc == (ncx + ncc) * per
    cf = lambda i: jnp.where(i < ncc, ncx + i, i - ncc)
    cb = lambda i: ns - 1 - i

    dspec = lambda arr, fn, d: pl.BlockSpec((per, None) + arr.shape[2:], lambda i: (fn(i), d, 0, 0, 0, 0))
    espec = lambda fn: pl.BlockSpec((per, SUBLANES, LANES), lambda i: (fn(i), 0, 0))
    ospec = lambda fn: pl.BlockSpec((bsz, per * CHUNK, dn), lambda i: (0, fn(i), 0))
    return pl.pallas_call(
        functools.partial(_scan_kernel, bsz=bsz),
        grid=(ns,),
        in_specs=[dspec(tall, cf, 0), dspec(tall, cb, 1), dspec(wide, cf, 0), dspec(wide, cb, 1),
                  espec(cf), espec(cb)],
        out_specs=[ospec(lambda i: jnp.maximum(i - ncc, 0)), ospec(lambda i: jnp.minimum(ns - 1 - i, ncx - 1))],
        out_shape=[jax.ShapeDtypeStruct((bsz, n_lat, dn), BF16)] * 2,
        scratch_shapes=[pltpu.VMEM((2, bsz, HEADS, HEAD_DIM, HEAD_DIM), F32)],
        compiler_params=_cparams(("arbitrary",)),
        name="scan",
    )(tall, tall, wide, wide, el, el)


def _hidden_kernel(pe_ref, w1_ref, b1_ref, w2_ref, b2_ref, w3_ref, b3_ref, f_ref, o_ref):
    f = f_ref[...]
    dot = functools.partial(jnp.dot, precision=HIGHEST, preferred_element_type=F32)
    a = jnp.sin(f * (dot(pe_ref[...], w1_ref[...]) + b1_ref[...]))
    a = jnp.sin(f * (dot(a, w2_ref[...]) + b2_ref[...]))
    o_ref[...] = jnp.sin(f * (dot(a, w3_ref[...]) + b3_ref[...]))


def _hidden_call(pe, w1p, b1, w2, b2, w3, b3, freq):
    rows, pw = pe.shape
    fw = 2 * w2.shape[0]
    tr = 512
    full = lambda a: pl.BlockSpec(a.shape, lambda t: (0,) * a.ndim)
    two = lambda a: jnp.tile(a.reshape(1, -1), (1, 2))
    diag2 = lambda a: jnp.kron(jnp.eye(2, dtype=a.dtype), a)
    args = (diag2(w1p), two(b1), diag2(w2), two(b2), diag2(w3), two(b3), two(freq))
    return pl.pallas_call(
        _hidden_kernel,
        grid=(rows // tr,),
        in_specs=[pl.BlockSpec((tr, pw), lambda t: (t, 0))] + [full(a) for a in args],
        out_specs=pl.BlockSpec((tr, fw), lambda t: (t, 0)),
        out_shape=jax.ShapeDtypeStruct((rows, fw), F32),
        compiler_params=_cparams(("arbitrary",)),
        name="hidden",
    )(pe, *args)


def _ftime_kernel(a_ref, t_ref, w_ref, dl_ref, h_ref, ss_ref, *, n_lat, tr):
    t = pl.program_id(0)
    h = _bdot(a_ref[...], w_ref[...]) * jnp.exp(-t_ref[...] * dl_ref[...])
    row = lax.broadcasted_iota(jnp.int32, (tr, 1), 0) + t * tr
    h = jnp.where(row == n_lat, 0.0, h)
    h_ref[...] = h.astype(h_ref.dtype)

    @pl.when(t == 0)
    def _():
        ss_ref[...] = jnp.zeros_like(ss_ref)

    ss_ref[...] += jnp.sum(h * h, axis=0, keepdims=True)


def _ftime_call(a3, tcol, wout2, deltas2, n_lat):
    rows = tcol.shape[0]
    fw = a3.shape[1]
    ncol = wout2.shape[2]
    tr = 512
    side = lambda t: (t * tr) // n_lat
    return pl.pallas_call(
        functools.partial(_ftime_kernel, n_lat=n_lat, tr=tr),
        grid=(rows // tr,),
        in_specs=[pl.BlockSpec((tr, fw), lambda t: (t % (n_lat // tr), 0)),
                  pl.BlockSpec((tr, 1), lambda t: (t, 0)),
                  pl.BlockSpec((None, fw, ncol), lambda t: (side(t), 0, 0)),
                  pl.BlockSpec((1, ncol), lambda t: (0, 0))],
        out_specs=[pl.BlockSpec((tr, ncol), lambda t: (t, 0)),
                   pl.BlockSpec((1, ncol), lambda t: (0, 0))],
        out_shape=[jax.ShapeDtypeStruct((rows, ncol), BF16), jax.ShapeDtypeStruct((1, ncol), F32)],
        compiler_params=_cparams(("arbitrary",)),
        name="ftime",
    )(a3, tcol, wout2, deltas2)


@functools.lru_cache(maxsize=None)
def _fft_tables(n_lat):
    n = 2 * n_lat
    n1t = n // FFT_INNER
    h1 = n1t // 2
    n2 = np.arange(FFT_INNER, dtype=np.int64)[:, None, None]
    k1 = np.arange(n1t, dtype=np.int64)[None, :, None]
    n1 = np.arange(h1, dtype=np.int64)[None, None, :]
    ang = (-2.0 * np.pi / n) * ((k1 * (FFT_INNER * n1 + n2)) % n)
    er, ei = np.cos(ang), np.sin(ang)
    e1 = np.concatenate([np.concatenate([er, -ei], axis=2), np.concatenate([ei, er], axis=2)], axis=1)
    e4 = np.transpose(e1, (0, 2, 1))
    sgn = np.where(np.arange(n1t) % 2 == 0, 1.0, -1.0)[None, :, None]
    ef = np.concatenate([np.concatenate([er, sgn * er], axis=2), np.concatenate([ei, sgn * ei], axis=2)], axis=1)
    a = np.arange(FFT_INNER, dtype=np.int64)
    ang2 = (-2.0 * np.pi / FFT_INNER) * ((a[:, None] * a[None, :]) % FFT_INNER)
    gr, gi = np.cos(ang2), np.sin(ang2)
    g2 = np.block([[gr, -gi], [gi, gr]])
    f32 = lambda x: np.ascontiguousarray(x, dtype=np.float32)
    return f32(e1), f32(e4), f32(ef), f32(g2), f32(g2.T)


def _hold(p, j, first, last, nsteps):
    active = (p >= first) & (p <= last) & (lax.rem(p - first, 2) == 0)
    return jnp.where(active, j, jnp.where(p < first, 0, nsteps - 1))


def _slab_read(x_ref, rows):
    return jnp.concatenate([x_ref[c, rows, :] for c in range(x_ref.shape[0])], axis=1)


def _slab_write(x_ref, rows, val):
    for c in range(x_ref.shape[0]):
        x_ref[c, rows, :] = val[:, c * LANES:(c + 1) * LANES]


def _spectrum_steps(x_ref, g2_ref, k1s, n1t, stride):
    cols = [jnp.concatenate([_slab_read(x_ref, pl.ds(k1, FFT_INNER, stride=stride)),
                             _slab_read(x_ref, pl.ds(n1t + k1, FFT_INNER, stride=stride))], axis=0).astype(BF16)
            for k1 in k1s]
    return [jnp.dot(g2_ref[...], c, preferred_element_type=F32) for c in cols]


def _ffft_kernel(h_ref, ef_ref, g2_ref, ss_ref, o_ref, x_ref, *, n1t, grp, kgrp, stride, inv_n):
    p = pl.program_id(1)
    j = pl.program_id(2)

    @pl.when(p == 0)
    def _():
        for j0 in range(0, grp, FFT_INTERLEAVE):
            jjs = list(range(j0, min(j0 + FFT_INTERLEAVE, grp)))
            us = [jnp.concatenate([h_ref[0, jj], h_ref[1, jj]], axis=0) for jj in jjs]
            slabs = [_bdot(ef_ref[jj], u) for jj, u in zip(jjs, us)]
            for jj, slab in zip(jjs, slabs):
                base = pl.multiple_of((j * grp + jj) * stride, SUBLANES)
                _slab_write(x_ref, pl.ds(base, 2 * n1t), slab)

    @pl.when(p == 1)
    def _():
        scale = lax.rsqrt(ss_ref[...] + EPS) * inv_n
        for j0 in range(0, kgrp, FFT_INTERLEAVE):
            jjs = list(range(j0, min(j0 + FFT_INTERLEAVE, kgrp)))
            xfs = _spectrum_steps(x_ref, g2_ref, [j * kgrp + jj for jj in jjs], n1t, stride)
            for jj, xf in zip(jjs, xfs):
                o_ref[jj] = (xf * scale).astype(BF16)


def _fft_groups(n1t, ns):
    ns = min(ns, n1t)
    assert FFT_INNER % ns == 0 and n1t % ns == 0
    return ns, FFT_INNER // ns, n1t // ns


def _ffft_call(hfull, ssq, ef, g2, n_lat):
    ncol = hfull.shape[1]
    n1t = 2 * n_lat // FFT_INNER
    h1 = n1t // 2
    ct = FFT_CHANNELS
    ns, grp, kgrp = _fft_groups(n1t, FILTER_FFT_STEPS)
    stride = 2 * n1t + SLAB_PAD
    hv = hfull.reshape(2, FFT_INNER, h1, ncol)
    nct = ncol // ct
    cpo = nct // 2
    return pl.pallas_call(
        functools.partial(_ffft_kernel, n1t=n1t, grp=grp, kgrp=kgrp, stride=stride, inv_n=1.0 / (2 * n_lat)),
        grid=(nct, 2, ns),
        in_specs=[pl.BlockSpec((2, grp, h1, ct), lambda c, p, j: (0, _hold(p, j, 0, 0, ns), 0, c)),
                  pl.BlockSpec((grp, 2 * n1t, n1t), lambda c, p, j: (_hold(p, j, 0, 0, ns), 0, 0)),
                  pl.BlockSpec((2 * FFT_INNER, 2 * FFT_INNER), lambda c, p, j: (0, 0)),
                  pl.BlockSpec((1, ct), lambda c, p, j: (0, c))],
        out_specs=pl.BlockSpec((None, kgrp, 2 * FFT_INNER, ct),
                               lambda c, p, j: (c // cpo, _hold(p, j, 1, 1, ns), 0, c % cpo)),
        out_shape=jax.ShapeDtypeStruct((2, n1t, 2 * FFT_INNER, ncol // 2), BF16),
        scratch_shapes=[pltpu.VMEM((ct // LANES, FFT_INNER * stride, LANES), F32)],
        compiler_params=_cparams(("arbitrary", "arbitrary", "arbitrary")),
        name="ffft",
    )(hv, ef, g2, ssq)


def _hyena_kernel(v_ref, x1_ref, x2_ref, hf_ref, e1_ref, e4_ref, g2_ref, g3_ref, bias_ref,
                  o_ref, x_ref, z_ref, *, n1t, grp, kgrp, stride):
    p = pl.program_id(1)
    j = pl.program_id(2)
    h1 = n1t // 2

    def gather(ref, jj):
        return jnp.concatenate([ref[0, jj], ref[1, jj]], axis=0).astype(F32)

    def slab_base(jj):
        return pl.multiple_of((j * grp + jj) * stride, SUBLANES)

    def time_rows(jjs):
        slabs = [_slab_read(x_ref, pl.ds(slab_base(jj), 2 * n1t)).astype(BF16) for jj in jjs]
        return [jnp.dot(e4_ref[jj], s, preferred_element_type=F32) for jj, s in zip(jjs, slabs)]

    def write_slabs(jjs, us):
        slabs = [jnp.dot(e1_ref[jj], u, preferred_element_type=F32) for jj, u in zip(jjs, us)]
        for jj, slab in zip(jjs, slabs):
            _slab_write(x_ref, pl.ds(slab_base(jj), 2 * n1t), slab)

    def chunks(n):
        return [list(range(j0, min(j0 + FFT_INTERLEAVE, n))) for j0 in range(0, n, FFT_INTERLEAVE)]

    @pl.when(p == 0)
    def _():
        for jjs in chunks(grp):
            write_slabs(jjs, [gather(v_ref, jj).astype(BF16) for jj in jjs])

    @pl.when((p == 1) | (p == 3))
    def _():
        for jjs in chunks(kgrp):
            k1s = [j * kgrp + jj for jj in jjs]
            xfs = _spectrum_steps(x_ref, g2_ref, k1s, n1t, stride)
            ys = []
            for jj, xf in zip(jjs, xfs):
                hf = hf_ref[jj].astype(F32)
                xr, xi = xf[:FFT_INNER], xf[FFT_INNER:]
                hr, hi = hf[:FFT_INNER], hf[FFT_INNER:]
                ys.append(jnp.concatenate([xr * hr - xi * hi, xr * hi + xi * hr], axis=0).astype(BF16))
            bbs = [jnp.dot(g3_ref[...], y, preferred_element_type=F32) for y in ys]
            for k1, bb in zip(k1s, bbs):
                _slab_write(x_ref, pl.ds(k1, FFT_INNER, stride=stride), bb[:FFT_INNER])
                _slab_write(x_ref, pl.ds(n1t + k1, FFT_INNER, stride=stride), bb[FFT_INNER:])

    @pl.when(p == 2)
    def _():
        for jjs in chunks(grp):
            z1s = [(gather(x1_ref, jj) * (y + gather(v_ref, jj) * bias_ref[0:1, :])).astype(BF16)
                   for jj, y in zip(jjs, time_rows(jjs))]
            for jj, z1 in zip(jjs, z1s):
                z_ref[j * grp + jj] = z1
            write_slabs(jjs, z1s)

    @pl.when(p == 4)
    def _():
        for jjs in chunks(grp):
            for jj, y in zip(jjs, time_rows(jjs)):
                out = gather(x2_ref, jj) * (y + z_ref[j * grp + jj].astype(F32) * bias_ref[1:2, :])
                o_ref[0, jj] = out[:h1].astype(o_ref.dtype)
                o_ref[1, jj] = out[h1:].astype(o_ref.dtype)


def _hyena_call(hv, hf, e1, e4, g2, g3, bias):
    _, bsz, _, h1, ch = hv.shape
    assert bsz == 2, "the two batch rows are packed as the real and imaginary parts of one FFT"
    n1t = 2 * h1
    ct = FFT_CHANNELS
    ns, grp, kgrp = _fft_groups(n1t, CONV_FFT_STEPS)
    stride = 2 * n1t + SLAB_PAD

    def hspec(which, first, last):
        return pl.BlockSpec((None, bsz, grp, h1, ct),
                            lambda c, p, j: (which, 0, _hold(p, j, first, last, ns), 0, c))

    def hf_map(c, p, j):
        order = jnp.where(p >= 2, 1, 0)
        step = jnp.where((p == 1) | (p == 3), j, jnp.where((p == 0) | (p == 2), 0, ns - 1))
        return (order, step, 0, c)

    return pl.pallas_call(
        functools.partial(_hyena_kernel, n1t=n1t, grp=grp, kgrp=kgrp, stride=stride),
        grid=(ch // ct, 5, ns),
        in_specs=[hspec(0, 0, 2), hspec(1, 2, 2), hspec(2, 4, 4),
                  pl.BlockSpec((None, kgrp, 2 * FFT_INNER, ct), hf_map),
                  pl.BlockSpec((grp, 2 * n1t, n1t), lambda c, p, j: (_hold(p, j, 0, 2, ns), 0, 0)),
                  pl.BlockSpec((grp, n1t, 2 * n1t), lambda c, p, j: (_hold(p, j, 2, 4, ns), 0, 0)),
                  pl.BlockSpec((2 * FFT_INNER, 2 * FFT_INNER), lambda c, p, j: (0, 0)),
                  pl.BlockSpec((2 * FFT_INNER, 2 * FFT_INNER), lambda c, p, j: (0, 0)),
                  pl.BlockSpec((2, ct), lambda c, p, j: (0, c))],
        out_specs=pl.BlockSpec((bsz, grp, h1, ct), lambda c, p, j: (0, _hold(p, j, 4, 4, ns), 0, c)),
        out_shape=jax.ShapeDtypeStruct((bsz, FFT_INNER, h1, ch), BF16),
        scratch_shapes=[pltpu.VMEM((ct // LANES, FFT_INNER * stride, LANES), F32),
                        pltpu.VMEM((FFT_INNER, n1t, ct), BF16)],
        compiler_params=_cparams(("arbitrary", "arbitrary", "arbitrary"), VMEM_LIMIT_CONV),
        name="hyena",
    )(hv, hv, hv, hf, e1, e4, g2, g3, bias)


def _merge_kernel(of_ref, ob_ref, gp_ref, hy_ref, x_ref, mod_ref, dng_ref, wpa_ref, wpb_ref, wo_ref,
                  fg_ref, o_ref, *, d):
    b = pl.program_id(0)
    blocks = [slice(r * MXU_TILE, (r + 1) * MXU_TILE) for r in range(o_ref.shape[1] // MXU_TILE)]
    f32 = lambda ref, *idx: ref[idx].astype(F32)

    def head_norm(rows):
        o = f32(of_ref, 0, rows) + f32(ob_ref, 0, rows)
        za = f32(gp_ref, 0, 0, rows)
        parts = []
        for hh in range(HEADS):
            hs = slice(hh * HEAD_DIM, (hh + 1) * HEAD_DIM)
            oh = o[:, hs]
            ms = jnp.mean(oh * oh, axis=-1, keepdims=True)
            parts.append((oh * lax.rsqrt(ms + EPS) * dng_ref[...] * za[:, hs]).astype(BF16))
        return jnp.concatenate(parts, axis=1)

    o_a = [head_norm(rows) for rows in blocks]
    y_a = [jnp.dot(a, wpa_ref[...], preferred_element_type=F32) for a in o_a]
    o_b = [(f32(hy_ref, 0, rows) * f32(gp_ref, 1, 0, rows)).astype(BF16) for rows in blocks]
    y_b = [jnp.dot(a, wpb_ref[...], preferred_element_type=F32) for a in o_b]
    m = [(f32(gp_ref, 2, 0, rows) * ya + f32(gp_ref, 3, 0, rows) * yb).astype(BF16)
         for rows, ya, yb in zip(blocks, y_a, y_b)]
    y = [jnp.dot(a, wo_ref[...], preferred_element_type=F32) for a in m]
    gate = mod_ref[pl.ds(b, 1), :][:, 2 * d:3 * d]
    for rows, yr in zip(blocks, y):
        xn = x_ref[0, rows, :] + gate * yr
        o_ref[0, rows, :] = xn * lax.rsqrt(jnp.mean(xn * xn, axis=-1, keepdims=True) + EPS) * fg_ref[...]


def _merge_call(o_f, o_b, gp, hy, x, mod, dn_norm_g, w_pa, w_pb, w_out, final_g):
    bsz, n_lat, d = x.shape
    tm = NORM_TILE if n_lat % NORM_TILE == 0 else 256
    wfull = lambda a: pl.BlockSpec(a.shape, lambda b, t: (0,) * a.ndim)
    tok = pl.BlockSpec((1, tm, d), lambda b, t: (b, t, 0))
    return pl.pallas_call(
        functools.partial(_merge_kernel, d=d),
        grid=(bsz, n_lat // tm),
        in_specs=[tok, tok,
                  pl.BlockSpec((4, 1, tm, d), lambda b, t: (0, b, t, 0)),
                  tok, tok, wfull(mod),
                  pl.BlockSpec((1, HEAD_DIM), lambda b, t: (0, 0)),
                  wfull(w_pa), wfull(w_pb), wfull(w_out),
                  pl.BlockSpec((1, d), lambda b, t: (0, 0))],
        out_specs=tok,
        out_shape=jax.ShapeDtypeStruct((bsz, n_lat, d), F32),
        compiler_params=_cparams(("arbitrary", "arbitrary")),
        name="merge",
    )(o_f, o_b, gp, hy, x, mod, dn_norm_g.reshape(1, HEAD_DIM), w_pa, w_pb, w_out, final_g.reshape(1, d))


def _position_features(n_lat):
    h1 = n_lat // FFT_INNER
    m = (np.arange(FFT_INNER)[:, None] + FFT_INNER * np.arange(h1)[None, :]).reshape(-1)
    lag = np.concatenate([m, n_lat - m]).astype(np.float64)
    lag[n_lat] = 0.0
    bands = (HY_EMB - 1) // 2
    t = (lag / (n_lat - 1))[:, None]
    wpos = (2.0 * math.pi / n_lat) * lag[:, None]
    fb = np.linspace(1e-4, bands - 1, bands)[None, :]
    pe = np.concatenate([t, np.cos(fb * wpos), -np.sin(fb * wpos)], axis=1)
    pe_pad = np.zeros((2 * n_lat, LANES), np.float32)
    pe_pad[:, :HY_EMB] = pe
    return pe_pad, np.ascontiguousarray(t, dtype=np.float32)


def kernel(x, c, ctx, c_ctx, w_mod, b_mod, norm_g, w_in, dn_conv_w, dn_a_log, dn_dt_bias, dn_norm_g,
           hy_conv_w, hy_conv_b, hy_f_w1, hy_f_b1, hy_f_w2, hy_f_b2, hy_f_w3, hy_f_b3, hy_f_wout,
           hy_f_freq, hy_bias, w_pa, w_pb, w_out, final_g):
    bsz, n_lat, d = x.shape
    n_ctx = ctx.shape[1]
    assert w_mod.shape[0] == 1, "single layer: the context stream is only read through its scan states"
    dn = HEADS * HEAD_DIM
    hy = hy_bias.shape[-1]
    assert n_lat % 256 == 0 and n_ctx % 256 == 0 and dn == d and hy == d

    cvec = jnp.zeros((SUBLANES, d), F32).at[:bsz].set(c).at[bsz].set(c_ctx)
    mod = _mod_call(cvec, w_mod[0], b_mod[0])

    w = w_in[0].astype(BF16)
    o_qkv, o_gate, o_hy, o_gp = 0, 4 * dn, 4 * dn + 4 * HEADS, 4 * dn + 4 * HEADS + 3 * hy
    col3 = lambda a, off, n: jnp.transpose(a[:, off:off + 3 * n].reshape(a.shape[0], 3, n), (1, 0, 2))
    lt = n_lat + n_ctx
    tm = PROJ_TILE if n_lat % PROJ_TILE == 0 else 256

    wg = jnp.zeros((d, LANES), BF16).at[:, :4 * HEADS].set(w[:, o_gate:o_gate + 4 * HEADS])
    lane_pad = lambda a: jnp.zeros((1, LANES), F32).at[0, 2 * HEADS:4 * HEADS].set(a.reshape(-1))
    gate_args = (wg, lane_pad(dn_a_log[0]), lane_pad(dn_dt_bias[0]))
    h_tok, gates = _hnorm_call(x, mod, norm_g[0], *gate_args, tm=NORM_TILE, gate_rows=lt, gate_tile0=0, name="hnorm")
    h_ctx, gates = _hnorm_call(ctx, mod, norm_g[0], *gate_args, tm=n_ctx, gate_rows=lt, gate_tile0=n_lat // n_ctx,
                               mod_row=bsz, prev_gates=gates, name="hnorm_ctx")
    h1 = n_lat // FFT_INNER
    h_perm = jnp.transpose(h_tok.reshape(bsz, h1, FFT_INNER, d), (0, 2, 1, 3)).reshape(bsz, n_lat, d)

    w_qkv, cw_qkv = col3(w, o_qkv, dn), col3(dn_conv_w[0], 0, dn)
    qkv = _convproj_call(h_tok, w_qkv, cw_qkv, rowlen=GRID_W, tm=tm, tile0=0, out_rows=lt, name="qkv")
    qkv = _convproj_call(h_ctx, w_qkv, cw_qkv, rowlen=n_ctx, tm=n_ctx, tile0=n_lat // n_ctx, out_rows=lt,
                         prev=qkv, name="qkv_ctx")
    gates_t = jnp.transpose(gates[:, :, :4 * HEADS].reshape(bsz, lt // CHUNK, CHUNK, 4 * HEADS), (0, 1, 3, 2))
    tall, wide, el = _prep_call(qkv, gates, gates_t)
    o_f, o_b = _scan_call(tall, wide, el, n_lat, n_ctx)

    hyp = _hyproj_call(h_perm, col3(w, o_hy, hy), col3(hy_conv_w[0], 0, hy), hy_conv_b[0].reshape(3, 1, hy))
    w4 = jnp.stack([w[:, 3 * dn:4 * dn]] + [w[:, o_gp + i * d:o_gp + (i + 1) * d] for i in range(3)])
    gp = _gproj_call(h_tok, w4, n_lat)

    pe, tcol = _position_features(n_lat)
    w1p = jnp.zeros((LANES, hy_f_w1.shape[-1]), F32).at[:HY_EMB].set(hy_f_w1[0])
    pe2 = jnp.asarray(np.concatenate([pe[:n_lat], pe[n_lat:]], axis=1))
    a3 = _hidden_call(pe2, w1p, hy_f_b1[0], hy_f_w2[0], hy_f_b2[0], hy_f_w3[0], hy_f_b3[0], hy_f_freq[0])
    deltas = np.abs(np.linspace(math.log(HY_DECAY_TARGET) / HY_SLOW_DECAY,
                                math.log(HY_DECAY_TARGET) / HY_FAST_DECAY, hy)).astype(np.float32)
    deltas2 = jnp.asarray(np.tile(deltas, 2)[None, :])
    wside = hy_f_wout[0].reshape(-1, 2, 2 * hy).transpose(1, 0, 2).astype(BF16)
    zside = jnp.zeros_like(wside[0])
    wout2 = jnp.stack([jnp.concatenate([wside[0], zside]), jnp.concatenate([zside, wside[1]])])
    hfull, ssq = _ftime_call(a3, jnp.asarray(tcol), wout2, deltas2, n_lat)
    e1, e4, ef, g2, g3 = (jnp.asarray(t).astype(BF16) for t in _fft_tables(n_lat))
    hf = _ffft_call(hfull, ssq, ef, g2, n_lat)

    yh = _hyena_call(hyp, hf, e1, e4, g2, g3, hy_bias[0])
    yh = jnp.transpose(yh, (0, 2, 1, 3)).reshape(bsz, n_lat, hy)

    return _merge_call(o_f, o_b, gp, yh, x, mod, dn_norm_g[0], w_pa[0].astype(BF16), w_pb[0].astype(BF16),
                       w_out[0].astype(BF16), final_g)
```

```python
import functools
import math

import numpy as np
import jax
import jax.numpy as jnp
from jax import lax
from jax.experimental import pallas as pl
from jax.experimental.pallas import tpu as pltpu

F32 = jnp.float32
BF16 = jnp.bfloat16
HIGHEST = lax.Precision.HIGHEST

EPS = 1e-6
HEADS = 8
HEAD_DIM = 128
CHUNK = 64
GRID_W = 64
HY_EMB = 33
HY_DECAY_TARGET = 1e-2
HY_FAST_DECAY = 0.3
HY_SLOW_DECAY = 1.5

LANES = 128
SUBLANES = 8
FFT_INNER = 128
CONV_FFT_STEPS = 8
FILTER_FFT_STEPS = 8
FFT_CHANNELS = 256
FFT_INTERLEAVE = 8
PROJ_TILE = 1024
NORM_TILE = 512
SCAN_CHUNKS = 4
PREP_CHUNKS = 1
HYPROJ_N1 = 16
MXU_TILE = 256
SLAB_PAD = 8
VMEM_LIMIT = 56 * 1024 * 1024
VMEM_LIMIT_CONV = 62 * 1024 * 1024


def _cparams(sem, vmem=VMEM_LIMIT):
    return pltpu.CompilerParams(dimension_semantics=sem, vmem_limit_bytes=vmem)


def _bdot(a, b):
    return jnp.dot(a.astype(BF16), b.astype(BF16), preferred_element_type=F32)


def _silu(x):
    return x * jax.nn.sigmoid(x)


def _mod_kernel(c_ref, w_ref, b_ref, o_ref):
    s = _silu(c_ref[...])
    o_ref[...] = jnp.dot(s, w_ref[...], precision=HIGHEST, preferred_element_type=F32) + b_ref[...]


def _mod_call(cvec, w_mod, b_mod):
    rows, d = cvec.shape
    n = w_mod.shape[1]
    tn = 1024
    return pl.pallas_call(
        _mod_kernel,
        grid=(n // tn,),
        in_specs=[pl.BlockSpec((rows, d), lambda j: (0, 0)),
                  pl.BlockSpec((d, tn), lambda j: (0, j)),
                  pl.BlockSpec((1, tn), lambda j: (0, j))],
        out_specs=pl.BlockSpec((rows, tn), lambda j: (0, j)),
        out_shape=jax.ShapeDtypeStruct((rows, n), F32),
        compiler_params=_cparams(("arbitrary",)),
        name="mod",
    )(cvec, w_mod, b_mod.reshape(1, n))


def _norm_rows(x, m, g_ref, d):
    y = x * lax.rsqrt(jnp.mean(x * x, axis=-1, keepdims=True) + EPS) * g_ref[...]
    return (y * (1.0 + m[:, d:2 * d]) + m[:, 0:d]).astype(BF16)


def _gate_rows(h, w_ref, alog_ref, dtb_ref):
    z = jnp.dot(h, w_ref[...], preferred_element_type=F32)
    lane = lax.broadcasted_iota(jnp.int32, z.shape, 1)
    u = z + dtb_ref[...]
    softplus = jnp.maximum(u, 0.0) + jnp.log(1.0 + jnp.exp(-jnp.abs(u)))
    return jnp.where(lane < 2 * HEADS, jax.nn.sigmoid(z), -jnp.exp(alog_ref[...]) * softplus)


def _hnorm_kernel(tok_ref, mod_ref, g_ref, w_ref, alog_ref, dtb_ref, *rest, d, mod_row):
    o_ref, og_ref = rest[-2:]
    row = pl.program_id(0) if mod_row is None else mod_row
    m = mod_ref[pl.ds(row, 1), :]
    h = _norm_rows(tok_ref[0], m, g_ref, d)
    o_ref[0] = h
    og_ref[0] = _gate_rows(h, w_ref, alog_ref, dtb_ref)


def _hnorm_call(tok, mod, norm_g, wg, alog, dtb, *, tm, gate_rows, gate_tile0, mod_row=None, prev_gates=None, name):
    bsz, rows, d = tok.shape
    full = lambda a: pl.BlockSpec(a.shape, lambda b, t: (0,) * a.ndim)
    args = [tok, mod, norm_g.reshape(1, d), wg, alog, dtb]
    in_specs = [pl.BlockSpec((1, tm, d), lambda b, t: (b, t, 0))] + [full(a) for a in args[1:]]
    aliases = {}
    if prev_gates is not None:
        in_specs.append(pl.BlockSpec(memory_space=pl.ANY))
        args.append(prev_gates)
        aliases = {6: 1}
    return pl.pallas_call(
        functools.partial(_hnorm_kernel, d=d, mod_row=mod_row),
        grid=(bsz, rows // tm),
        in_specs=in_specs,
        out_specs=[pl.BlockSpec((1, tm, d), lambda b, t: (b, t, 0)),
                   pl.BlockSpec((1, tm, LANES), lambda b, t: (b, t + gate_tile0, 0))],
        out_shape=[jax.ShapeDtypeStruct((bsz, rows, d), BF16), jax.ShapeDtypeStruct((bsz, gate_rows, LANES), F32)],
        input_output_aliases=aliases,
        compiler_params=_cparams(("arbitrary", "arbitrary")),
        name=name,
    )(*args)


@functools.lru_cache(maxsize=None)
def _shift_matrix(rowlen, taps, blk):
    t = np.arange(blk)
    mats = []
    for j in range(taps):
        d = j - taps // 2
        if d != 0:
            ok = ((t + d) >= 0) & ((t + d) < blk) & (t // rowlen == (t + d) // rowlen)
            s = np.zeros((blk, blk), np.float32)
            s[t[ok], t[ok] + d] = 1.0
            mats.append(s)
    return np.concatenate(mats, axis=1)


def _shift_block(rowlen, taps):
    return max(rowlen, MXU_TILE // (taps - 1) // rowlen * rowlen)


def _convproj_kernel(h_ref, w_ref, cw_ref, sh_ref, *rest, tm):
    o_ref = rest[-1]
    j = pl.program_id(0)
    taps = cw_ref.shape[1]
    mid = taps // 2
    blk = sh_ref.shape[0]
    mblk = max(blk, MXU_TILE)
    na = jnp.where(j == 0, float(HEAD_DIM), jnp.where(j == 1, 1.0, 0.0))
    nb = jnp.where(j == 0, EPS * HEAD_DIM, jnp.where(j == 1, EPS, 1.0))
    cw16 = cw_ref[0].astype(BF16)
    for r in range(tm // mblk):
        zm = jnp.dot(h_ref[0, r * mblk:(r + 1) * mblk, :], w_ref[0], preferred_element_type=F32)
        for g in range(mblk // blk):
            rows = slice(r * mblk + g * blk, r * mblk + (g + 1) * blk)
            zb = zm[g * blk:(g + 1) * blk]
            zb16 = zb.astype(BF16)
            side = jnp.concatenate([zb16 * cw16[jt:jt + 1, :] for jt in range(taps) if jt != mid], axis=0)
            y = zb * cw_ref[0, mid:mid + 1, :] + jnp.dot(sh_ref[...], side, preferred_element_type=F32)
            y = _silu(y)
            for hh in range(HEADS):
                yh = y[:, hh * HEAD_DIM:(hh + 1) * HEAD_DIM]
                nrm = jnp.sum(yh * yh, axis=-1, keepdims=True)
                o_ref[0, 0, rows, hh * HEAD_DIM:(hh + 1) * HEAD_DIM] = (
                    yh * lax.rsqrt(nrm * na + nb)).astype(o_ref.dtype)


def _convproj_call(h, w3, cw3, *, rowlen, tm, tile0, out_rows, prev=None, name):
    bsz, rows, d = h.shape
    ntiles = rows // tm
    nw = w3.shape[2]
    taps = cw3.shape[1]
    blk = _shift_block(rowlen, taps)
    assert tm % max(blk, MXU_TILE) == 0 and max(blk, MXU_TILE) % blk == 0
    sh = jnp.asarray(_shift_matrix(rowlen, taps, blk)).astype(BF16)
    in_specs = [pl.BlockSpec((1, tm, d), lambda j, b, t: (b, t, 0)),
                pl.BlockSpec((1, d, nw), lambda j, b, t: (j, 0, 0)),
                pl.BlockSpec((1, cw3.shape[1], nw), lambda j, b, t: (j, 0, 0)),
                pl.BlockSpec(sh.shape, lambda j, b, t: (0, 0))]
    args = [h, w3, cw3, sh]
    aliases = {}
    if prev is not None:
        in_specs.append(pl.BlockSpec(memory_space=pl.ANY))
        args.append(prev)
        aliases = {4: 0}
    return pl.pallas_call(
        functools.partial(_convproj_kernel, tm=tm),
        grid=(3, bsz, ntiles),
        in_specs=in_specs,
        out_specs=pl.BlockSpec((1, 1, tm, nw), lambda j, b, t: (j, b, t + tile0, 0)),
        out_shape=jax.ShapeDtypeStruct((3, bsz, out_rows, nw), BF16),
        input_output_aliases=aliases,
        compiler_params=_cparams(("arbitrary", "arbitrary", "arbitrary")),
        name=name,
    )(*args)


def _hyproj_kernel(h_ref, w_ref, cw_ref, cb_ref, o_ref):
    n2s, n1s, d = h_ref.shape[1:]
    taps = cw_ref.shape[1]
    hrows = h_ref[0].reshape(n2s * n1s, d)
    step = min(MXU_TILE, n2s * n1s)
    z = jnp.concatenate([jnp.dot(hrows[r:r + step], w_ref[0], preferred_element_type=F32)
                         for r in range(0, n2s * n1s, step)], axis=0)
    y = cb_ref[0] + z * cw_ref[0, taps // 2:taps // 2 + 1, :]
    for jt in range(taps):
        s = (jt - taps // 2) * n1s
        if s > 0:
            y = y + jnp.concatenate([z[s:], jnp.zeros((s, z.shape[1]), F32)], axis=0) * cw_ref[0, jt:jt + 1, :]
        elif s < 0:
            y = y + jnp.concatenate([jnp.zeros((-s, z.shape[1]), F32), z[:s]], axis=0) * cw_ref[0, jt:jt + 1, :]
    ct = o_ref.shape[-1]
    for cc in range(o_ref.shape[2]):
        o_ref[0, 0, cc] = y[:, cc * ct:(cc + 1) * ct].astype(o_ref.dtype).reshape(n2s, n1s, ct)


def _hyproj_call(h_perm, w3, cw3, cb3):
    bsz, n_lat, d = h_perm.shape
    nw = w3.shape[2]
    h1 = n_lat // FFT_INNER
    n1s = min(HYPROJ_N1, h1)
    hv = h_perm.reshape(bsz, FFT_INNER, h1, d)
    return pl.pallas_call(
        _hyproj_kernel,
        grid=(3, bsz, FFT_INNER // GRID_W, h1 // n1s),
        in_specs=[pl.BlockSpec((1, GRID_W, n1s, d), lambda j, b, r, t: (b, r, t, 0)),
                  pl.BlockSpec((1, d, nw), lambda j, b, r, t: (j, 0, 0)),
                  pl.BlockSpec((1, cw3.shape[1], nw), lambda j, b, r, t: (j, 0, 0)),
                  pl.BlockSpec((1, 1, nw), lambda j, b, r, t: (j, 0, 0))],
        out_specs=pl.BlockSpec((1, 1, nw // FFT_CHANNELS, GRID_W, n1s, FFT_CHANNELS),
                               lambda j, b, r, t: (j, b, 0, r, t, 0)),
        out_shape=jax.ShapeDtypeStruct((3, bsz, nw // FFT_CHANNELS, FFT_INNER, h1, FFT_CHANNELS), BF16),
        compiler_params=_cparams(("arbitrary",) * 4),
        name="hyproj",
    )(hv, w3, cw3, cb3)


def _gproj_kernel(h_ref, w_ref, o_ref):
    j = pl.program_id(0)
    for r in range(h_ref.shape[1] // MXU_TILE):
        rows = slice(r * MXU_TILE, (r + 1) * MXU_TILE)
        z = jnp.dot(h_ref[0, rows, :], w_ref[0], preferred_element_type=F32)
        s = jax.nn.sigmoid(z)
        o_ref[0, 0, rows, :] = jnp.where(j < 2, z * s, s).astype(BF16)


def _gproj_call(h, w4, n_lat):
    bsz, _, d = h.shape
    nw = w4.shape[2]
    tm = PROJ_TILE if n_lat % PROJ_TILE == 0 else 256
    return pl.pallas_call(
        _gproj_kernel,
        grid=(4, bsz, n_lat // tm),
        in_specs=[pl.BlockSpec((1, tm, d), lambda j, b, t: (b, t, 0)),
                  pl.BlockSpec((1, d, nw), lambda j, b, t: (j, 0, 0))],
        out_specs=pl.BlockSpec((1, 1, tm, nw), lambda j, b, t: (j, b, t, 0)),
        out_shape=jax.ShapeDtypeStruct((4, bsz, n_lat, nw), BF16),
        compiler_params=_cparams(("arbitrary", "arbitrary", "arbitrary")),
        name="gproj",
    )(h, w4)


def _unit_tri_inverse(ls, lowers, filler):
    c = ls[0].shape[0]
    ri = lax.broadcasted_iota(jnp.int32, (c, c), 0)
    ci = lax.broadcasted_iota(jnp.int32, (c, c), 1)
    same = lambda n: (ri // n) == (ci // n)
    eye = jnp.where(ri == ci, 1.0, 0.0)
    in2 = same(2)
    ts = [eye - jnp.where(in2, l, 0.0) for l in ls]
    n = 2
    nlevels = c.bit_length() - 2
    level = 0
    while n < c:
        blk = same(2 * n) & jnp.logical_not(same(n))
        offs = [jnp.where(blk, l, 0.0).astype(BF16) for l in ls]
        if n % SUBLANES:
            prods = [_bdot(t, off) for t, off in zip(ts, offs)]
            filler(level, nlevels)
            ts = [t - _bdot(p, t) for p, t in zip(prods, ts)]
        else:
            def moving(t, lower):
                first = n if lower else 0
                return jnp.concatenate([t[r:r + n] for r in range(first, c, 2 * n)], axis=0)

            def merged(t, upd, lower):
                pieces, k = [], 0
                for r in range(0, c, n):
                    if ((r // n) % 2 == 1) == lower:
                        pieces.append(upd[k * n:(k + 1) * n])
                        k += 1
                    else:
                        pieces.append(t[r:r + n])
                return jnp.concatenate(pieces, axis=0)

            rows = [moving(t, lo) for t, lo in zip(ts, lowers)]
            prods = [_bdot(r, off) for r, off in zip(rows, offs)]
            filler(level, nlevels)
            upds = [r - _bdot(p, t) for r, p, t in zip(rows, prods, ts)]
            ts = [merged(t, u, lo) for t, u, lo in zip(ts, upds, lowers)]
        n *= 2
        level += 1
    return ts


_NT = (((1,), (1,)), ((), ()))
_TN = (((0,), (0,)), ((), ()))


def _prep_kernel(qkv_ref, g_ref, gt_ref, tall_ref, wide_ref, el_ref, *, bsz):
    c = CHUNK
    ri = lax.broadcasted_iota(jnp.int32, (c, c), 0)
    ci = lax.broadcasted_iota(jnp.int32, (c, c), 1)
    tri_l = jnp.where(ri >= ci, 1.0, 0.0)
    tri_u = jnp.where(ri <= ci, 1.0, 0.0)
    incl = (ri >= ci, ri <= ci)
    strict = (ri > ci, ri < ci)
    hdot = functools.partial(jnp.dot, precision=HIGHEST, preferred_element_type=F32)

    per = tall_ref.shape[0]
    cb = [(ch, b) for ch in range(per) for b in range(bsz)]
    rows = lambda ch: slice(ch * c, (ch + 1) * c)
    gs = [g_ref[b, rows(ch), :] for ch, b in cb]
    gts = [gt_ref[b, ch] for ch, b in cb]
    gcol = [(hdot(tri_l, g), hdot(tri_u, g)) for g in gs]
    grow = [(hdot(gt, tri_u), hdot(gt, tri_l)) for gt in gts]
    glast = [(gc[0][c - 1:c], gc[1][0:1]) for gc in gcol]
    el_ref[...] = jnp.zeros_like(el_ref)
    for p, (ch, b) in enumerate(cb):
        for d in range(2):
            el_ref[ch, d * bsz + b:d * bsz + b + 1, :] = jnp.exp(glast[p][d])

    pbh = [(p, hh) for p in range(len(cb)) for hh in range(HEADS)]
    hs = lambda hh: slice(hh * HEAD_DIM, (hh + 1) * HEAD_DIM)
    tok = lambda w, p, hh: qkv_ref[w, cb[p][1], rows(cb[p][0]), hs(hh)]
    ks = [tok(1, p, hh) for p, hh in pbh]
    qs = [tok(0, p, hh) for p, hh in pbh]
    kk = [lax.dot_general(k, k, _NT, preferred_element_type=F32) for k in ks]
    qk = [lax.dot_general(q, k, _NT, preferred_element_type=F32) for q, k in zip(qs, ks)]
    inst = [(d, p, hh, i) for d in range(2) for i, (p, hh) in enumerate(pbh)]
    decs, betas, gcs = [], [], []
    for d, p, hh, _ in inst:
        lg = 2 * HEADS + d * HEADS + hh
        gcs.append(gcol[p][d][:, lg:lg + 1])
        betas.append(gs[p][:, d * HEADS + hh:d * HEADS + hh + 1])
        decs.append(jnp.exp(jnp.where(incl[d], gcs[-1] - grow[p][d][lg:lg + 1, :], -jnp.inf)))
    ls = [jnp.where(strict[d], kk[i] * beta * dec, 0.0) for (d, _, _, i), beta, dec in zip(inst, betas, decs)]

    def scaled_operands(part, nparts):
        for (d, p, hh, i), dec, beta, gc in list(zip(inst, decs, betas, gcs))[part::nparts]:
            ch, b = cb[p]
            lg = 2 * HEADS + d * HEADS + hh
            eg = jnp.exp(gc)
            k = ks[i].astype(F32)
            tall_ref[ch, d, b, hh, :, c:2 * c] = (qk[i] * dec).astype(BF16)
            wide_ref[ch, d, b, hh, 0:c, :] = (qs[i].astype(F32) * eg).astype(BF16)
            wide_ref[ch, d, b, hh, c:2 * c, :] = (k * (beta * eg)).astype(BF16)
            wide_ref[ch, d, b, hh, 2 * c:3 * c, :] = (tok(2, p, hh).astype(F32) * beta).astype(BF16)
            wide_ref[ch, d, b, hh, 3 * c:4 * c, :] = (k * jnp.exp(glast[p][d][:, lg:lg + 1] - gc)).astype(BF16)

    ts = _unit_tri_inverse(ls, [d == 0 for d, _, _, _ in inst], scaled_operands)
    for (d, p, hh, _), t in zip(inst, ts):
        tall_ref[cb[p][0], d, cb[p][1], hh, :, 0:c] = t.astype(BF16)


def _prep_call(qkv, gates, gates_t):
    _, bsz, lt, dn = qkv.shape
    nc = lt // CHUNK
    per = PREP_CHUNKS
    assert nc % per == 0
    tall = (2, bsz, HEADS, CHUNK, 2 * CHUNK)
    wide = (2, bsz, HEADS, 4 * CHUNK, HEAD_DIM)
    return pl.pallas_call(
        functools.partial(_prep_kernel, bsz=bsz),
        grid=(nc // per,),
        in_specs=[pl.BlockSpec((3, bsz, per * CHUNK, dn), lambda i: (0, 0, i, 0)),
                  pl.BlockSpec((bsz, per * CHUNK, LANES), lambda i: (0, i, 0)),
                  pl.BlockSpec((bsz, per, 4 * HEADS, CHUNK), lambda i: (0, i, 0, 0))],
        out_specs=[pl.BlockSpec((per,) + tall, lambda i: (i, 0, 0, 0, 0, 0)),
                   pl.BlockSpec((per,) + wide, lambda i: (i, 0, 0, 0, 0, 0)),
                   pl.BlockSpec((per, SUBLANES, LANES), lambda i: (i, 0, 0))],
        out_shape=[jax.ShapeDtypeStruct((nc,) + tall, BF16),
                   jax.ShapeDtypeStruct((nc,) + wide, BF16),
                   jax.ShapeDtypeStruct((nc, SUBLANES, LANES), F32)],
        compiler_params=_cparams(("arbitrary",)),
        name="prep",
    )(qkv, gates, gates_t)


def _scan_kernel(tf_ref, tb_ref, wf_ref, wb_ref, ef_ref, eb_ref, of_ref, ob_ref, s_ref, *, bsz):
    i = pl.program_id(0)

    @pl.when(i == 0)
    def _():
        s_ref[...] = jnp.zeros_like(s_ref)

    c = CHUNK
    per = tf_ref.shape[0]
    dirs = ((tf_ref, wf_ref, ef_ref, of_ref), (tb_ref, wb_ref, eb_ref, ob_ref))
    inst = [(d, b, hh) for d in range(2) for b in range(bsz) for hh in range(HEADS)]
    state = [s_ref[d, b, hh] for d, b, hh in inst]
    for sub in range(per):
        at = (sub, per - 1 - sub)
        a1 = [jnp.dot(dirs[d][1][at[d], b, hh, 0:2 * c, :], s.astype(BF16), preferred_element_type=F32)
              for (d, b, hh), s in zip(inst, state)]
        v_new = [jnp.dot(dirs[d][0][at[d], b, hh, :, 0:c],
                         (dirs[d][1][at[d], b, hh, 2 * c:3 * c, :].astype(F32) - a[c:]).astype(BF16),
                         preferred_element_type=F32).astype(BF16) for (d, b, hh), a in zip(inst, a1)]
        new_state = []
        for (d, b, hh), a, vn, s in zip(inst, a1, v_new, state):
            lg = 2 * HEADS + d * HEADS + hh
            o = a[:c] + jnp.dot(dirs[d][0][at[d], b, hh, :, c:2 * c], vn, preferred_element_type=F32)
            dirs[d][3][b, at[d] * c:(at[d] + 1) * c, hh * HEAD_DIM:(hh + 1) * HEAD_DIM] = o.astype(BF16)
            el = dirs[d][2][at[d], d * bsz + b:d * bsz + b + 1, lg:lg + 1]
            new_state.append(s * el + lax.dot_general(dirs[d][1][at[d], b, hh, 3 * c:4 * c, :], vn, _TN,
                                                      preferred_element_type=F32))
        state = new_state
    for (d, b, hh), s in zip(inst, state):
        s_ref[d, b, hh] = s


def _scan_call(tall, wide, el, n_lat, n_ctx):
    nc, _, bsz = tall.shape[:3]
    dn = HEADS * HEAD_DIM
    per = SCAN_CHUNKS
    ncx, ncc, ns = n_lat // CHUNK // per, n_ctx // CHUNK // per, nc // per
    assert nc == (ncx + ncc) * per
    cf = lambda i: jnp.where(i < ncc, ncx + i, i - ncc)
    cb = lambda i: ns - 1 - i

    dspec = lambda arr, fn, d: pl.BlockSpec((per, None) + arr.shape[2:], lambda i: (fn(i), d, 0, 0, 0, 0))
    espec = lambda fn: pl.BlockSpec((per, SUBLANES, LANES), lambda i: (fn(i), 0, 0))
    ospec = lambda fn: pl.BlockSpec((bsz, per * CHUNK, dn), lambda i: (0, fn(i), 0))
    return pl.pallas_call(
        functools.partial(_scan_kernel, bsz=bsz),
        grid=(ns,),
        in_specs=[dspec(tall, cf, 0), dspec(tall, cb, 1), dspec(wide, cf, 0), dspec(wide, cb, 1),
                  espec(cf), espec(cb)],
        out_specs=[ospec(lambda i: jnp.maximum(i - ncc, 0)), ospec(lambda i: jnp.minimum(ns - 1 - i, ncx - 1))],
        out_shape=[jax.ShapeDtypeStruct((bsz, n_lat, dn), BF16)] * 2,
        scratch_shapes=[pltpu.VMEM((2, bsz, HEADS, HEAD_DIM, HEAD_DIM), F32)],
        compiler_params=_cparams(("arbitrary",)),
        name="scan",
    )(tall, tall, wide, wide, el, el)


def _hidden_kernel(pe_ref, w1_ref, b1_ref, w2_ref, b2_ref, w3_ref, b3_ref, f_ref, o_ref):
    f = f_ref[...]
    dot = functools.partial(jnp.dot, precision=HIGHEST, preferred_element_type=F32)
    a = jnp.sin(f * (dot(pe_ref[...], w1_ref[...]) + b1_ref[...]))
    a = jnp.sin(f * (dot(a, w2_ref[...]) + b2_ref[...]))
    o_ref[...] = jnp.sin(f * (dot(a, w3_ref[...]) + b3_ref[...]))


def _hidden_call(pe, w1p, b1, w2, b2, w3, b3, freq):
    rows, pw = pe.shape
    fw = 2 * w2.shape[0]
    tr = 512
    full = lambda a: pl.BlockSpec(a.shape, lambda t: (0,) * a.ndim)
    two = lambda a: jnp.tile(a.reshape(1, -1), (1, 2))
    diag2 = lambda a: jnp.kron(jnp.eye(2, dtype=a.dtype), a)
    args = (diag2(w1p), two(b1), diag2(w2), two(b2), diag2(w3), two(b3), two(freq))
    return pl.pallas_call(
        _hidden_kernel,
        grid=(rows // tr,),
        in_specs=[pl.BlockSpec((tr, pw), lambda t: (t, 0))] + [full(a) for a in args],
        out_specs=pl.BlockSpec((tr, fw), lambda t: (t, 0)),
        out_shape=jax.ShapeDtypeStruct((rows, fw), F32),
        compiler_params=_cparams(("arbitrary",)),
        name="hidden",
    )(pe, *args)


def _ftime_kernel(a_ref, t_ref, w_ref, dl_ref, h_ref, ss_ref, *, n_lat, tr):
    t = pl.program_id(0)
    h = _bdot(a_ref[...], w_ref[...]) * jnp.exp(-t_ref[...] * dl_ref[...])
    row = lax.broadcasted_iota(jnp.int32, (tr, 1), 0) + t * tr
    h = jnp.where(row == n_lat, 0.0, h)
    h_ref[...] = h.astype(h_ref.dtype)

    @pl.when(t == 0)
    def _():
        ss_ref[...] = jnp.zeros_like(ss_ref)

    ss_ref[...] += jnp.sum(h * h, axis=0, keepdims=True)


def _ftime_call(a3, tcol, wout2, deltas2, n_lat):
    rows = tcol.shape[0]
    fw = a3.shape[1]
    ncol = wout2.shape[2]
    tr = 512
    side = lambda t: (t * tr) // n_lat
    return pl.pallas_call(
        functools.partial(_ftime_kernel, n_lat=n_lat, tr=tr),
        grid=(rows // tr,),
        in_specs=[pl.BlockSpec((tr, fw), lambda t: (t % (n_lat // tr), 0)),
                  pl.BlockSpec((tr, 1), lambda t: (t, 0)),
                  pl.BlockSpec((None, fw, ncol), lambda t: (side(t), 0, 0)),
                  pl.BlockSpec((1, ncol), lambda t: (0, 0))],
        out_specs=[pl.BlockSpec((tr, ncol), lambda t: (t, 0)),
                   pl.BlockSpec((1, ncol), lambda t: (0, 0))],
        out_shape=[jax.ShapeDtypeStruct((rows, ncol), BF16), jax.ShapeDtypeStruct((1, ncol), F32)],
        compiler_params=_cparams(("arbitrary",)),
        name="ftime",
    )(a3, tcol, wout2, deltas2)


@functools.lru_cache(maxsize=None)
def _fft_tables(n_lat):
    n = 2 * n_lat
    n1t = n // FFT_INNER
    h1 = n1t // 2
    n2 = np.arange(FFT_INNER, dtype=np.int64)[:, None, None]
    k1 = np.arange(n1t, dtype=np.int64)[None, :, None]
    n1 = np.arange(h1, dtype=np.int64)[None, None, :]
    ang = (-2.0 * np.pi / n) * ((k1 * (FFT_INNER * n1 + n2)) % n)
    er, ei = np.cos(ang), np.sin(ang)
    e1 = np.concatenate([np.concatenate([er, -ei], axis=2), np.concatenate([ei, er], axis=2)], axis=1)
    e4 = np.transpose(e1, (0, 2, 1))
    sgn = np.where(np.arange(n1t) % 2 == 0, 1.0, -1.0)[None, :, None]
    ef = np.concatenate([np.concatenate([er, sgn * er], axis=2), np.concatenate([ei, sgn * ei], axis=2)], axis=1)
    a = np.arange(FFT_INNER, dtype=np.int64)
    ang2 = (-2.0 * np.pi / FFT_INNER) * ((a[:, None] * a[None, :]) % FFT_INNER)
    gr, gi = np.cos(ang2), np.sin(ang2)
    g2 = np.block([[gr, -gi], [gi, gr]])
    f32 = lambda x: np.ascontiguousarray(x, dtype=np.float32)
    return f32(e1), f32(e4), f32(ef), f32(g2), f32(g2.T)


def _hold(p, j, first, last, nsteps):
    active = (p >= first) & (p <= last) & (lax.rem(p - first, 2) == 0)
    return jnp.where(active, j, jnp.where(p < first, 0, nsteps - 1))


def _slab_read(x_ref, rows):
    return jnp.concatenate([x_ref[c, rows, :] for c in range(x_ref.shape[0])], axis=1)


def _slab_write(x_ref, rows, val):
    for c in range(x_ref.shape[0]):
        x_ref[c, rows, :] = val[:, c * LANES:(c + 1) * LANES]


def _spectrum_steps(x_ref, g2_ref, k1s, n1t, stride):
    cols = [jnp.concatenate([_slab_read(x_ref, pl.ds(k1, FFT_INNER, stride=stride)),
                             _slab_read(x_ref, pl.ds(n1t + k1, FFT_INNER, stride=stride))], axis=0).astype(BF16)
            for k1 in k1s]
    return [jnp.dot(g2_ref[...], c, preferred_element_type=F32) for c in cols]


def _ffft_kernel(h_ref, ef_ref, g2_ref, ss_ref, o_ref, x_ref, *, n1t, grp, kgrp, stride, inv_n):
    p = pl.program_id(1)
    j = pl.program_id(2)

    @pl.when(p == 0)
    def _():
        for j0 in range(0, grp, FFT_INTERLEAVE):
            jjs = list(range(j0, min(j0 + FFT_INTERLEAVE, grp)))
            us = [jnp.concatenate([h_ref[0, jj], h_ref[1, jj]], axis=0) for jj in jjs]
            slabs = [_bdot(ef_ref[jj], u) for jj, u in zip(jjs, us)]
            for jj, slab in zip(jjs, slabs):
                base = pl.multiple_of((j * grp + jj) * stride, SUBLANES)
                _slab_write(x_ref, pl.ds(base, 2 * n1t), slab)

    @pl.when(p == 1)
    def _():
        scale = lax.rsqrt(ss_ref[...] + EPS) * inv_n
        for j0 in range(0, kgrp, FFT_INTERLEAVE):
            jjs = list(range(j0, min(j0 + FFT_INTERLEAVE, kgrp)))
            xfs = _spectrum_steps(x_ref, g2_ref, [j * kgrp + jj for jj in jjs], n1t, stride)
            for jj, xf in zip(jjs, xfs):
                o_ref[jj] = (xf * scale).astype(BF16)


def _fft_groups(n1t, ns):
    ns = min(ns, n1t)
    assert FFT_INNER % ns == 0 and n1t % ns == 0
    return ns, FFT_INNER // ns, n1t // ns


def _ffft_call(hfull, ssq, ef, g2, n_lat):
    ncol = hfull.shape[1]
    n1t = 2 * n_lat // FFT_INNER
    h1 = n1t // 2
    ct = FFT_CHANNELS
    ns, grp, kgrp = _fft_groups(n1t, FILTER_FFT_STEPS)
    stride = 2 * n1t + SLAB_PAD
    hv = hfull.reshape(2, FFT_INNER, h1, ncol)
    nct = ncol // ct
    cpo = nct // 2
    return pl.pallas_call(
        functools.partial(_ffft_kernel, n1t=n1t, grp=grp, kgrp=kgrp, stride=stride, inv_n=1.0 / (2 * n_lat)),
        grid=(nct, 2, ns),
        in_specs=[pl.BlockSpec((2, grp, h1, ct), lambda c, p, j: (0, _hold(p, j, 0, 0, ns), 0, c)),
                  pl.BlockSpec((grp, 2 * n1t, n1t), lambda c, p, j: (_hold(p, j, 0, 0, ns), 0, 0)),
                  pl.BlockSpec((2 * FFT_INNER, 2 * FFT_INNER), lambda c, p, j: (0, 0)),
                  pl.BlockSpec((1, ct), lambda c, p, j: (0, c))],
        out_specs=pl.BlockSpec((None, None, kgrp, 2 * FFT_INNER, ct),
                               lambda c, p, j: (c // cpo, c % cpo, _hold(p, j, 1, 1, ns), 0, 0)),
        out_shape=jax.ShapeDtypeStruct((2, cpo, n1t, 2 * FFT_INNER, ct), BF16),
        scratch_shapes=[pltpu.VMEM((ct // LANES, FFT_INNER * stride, LANES), F32)],
        compiler_params=_cparams(("arbitrary", "arbitrary", "arbitrary")),
        name="ffft",
    )(hv, ef, g2, ssq)


def _hyena_kernel(v_ref, x1_ref, x2_ref, hf_ref, e1_ref, e4_ref, g2_ref, g3_ref, bias_ref,
                  o_ref, x_ref, z_ref, *, n1t, grp, kgrp, stride):
    p = pl.program_id(1)
    j = pl.program_id(2)
    h1 = n1t // 2

    def gather(ref, jj):
        return jnp.concatenate([ref[0, jj], ref[1, jj]], axis=0).astype(F32)

    def slab_base(jj):
        return pl.multiple_of((j * grp + jj) * stride, SUBLANES)

    def time_rows(jjs):
        slabs = [_slab_read(x_ref, pl.ds(slab_base(jj), 2 * n1t)).astype(BF16) for jj in jjs]
        return [jnp.dot(e4_ref[jj], s, preferred_element_type=F32) for jj, s in zip(jjs, slabs)]

    def write_slabs(jjs, us):
        slabs = [jnp.dot(e1_ref[jj], u, preferred_element_type=F32) for jj, u in zip(jjs, us)]
        for jj, slab in zip(jjs, slabs):
            _slab_write(x_ref, pl.ds(slab_base(jj), 2 * n1t), slab)

    def chunks(n):
        return [list(range(j0, min(j0 + FFT_INTERLEAVE, n))) for j0 in range(0, n, FFT_INTERLEAVE)]

    @pl.when(p == 0)
    def _():
        for jjs in chunks(grp):
            write_slabs(jjs, [gather(v_ref, jj).astype(BF16) for jj in jjs])

    @pl.when((p == 1) | (p == 3))
    def _():
        for jjs in chunks(kgrp):
            k1s = [j * kgrp + jj for jj in jjs]
            xfs = _spectrum_steps(x_ref, g2_ref, k1s, n1t, stride)
            ys = []
            for jj, xf in zip(jjs, xfs):
                hf = hf_ref[jj].astype(F32)
                xr, xi = xf[:FFT_INNER], xf[FFT_INNER:]
                hr, hi = hf[:FFT_INNER], hf[FFT_INNER:]
                ys.append(jnp.concatenate([xr * hr - xi * hi, xr * hi + xi * hr], axis=0).astype(BF16))
            bbs = [jnp.dot(g3_ref[...], y, preferred_element_type=F32) for y in ys]
            for k1, bb in zip(k1s, bbs):
                _slab_write(x_ref, pl.ds(k1, FFT_INNER, stride=stride), bb[:FFT_INNER])
                _slab_write(x_ref, pl.ds(n1t + k1, FFT_INNER, stride=stride), bb[FFT_INNER:])

    @pl.when(p == 2)
    def _():
        for jjs in chunks(grp):
            z1s = [(gather(x1_ref, jj) * (y + gather(v_ref, jj) * bias_ref[0:1, :])).astype(BF16)
                   for jj, y in zip(jjs, time_rows(jjs))]
            for jj, z1 in zip(jjs, z1s):
                z_ref[j * grp + jj] = z1
            write_slabs(jjs, z1s)

    @pl.when(p == 4)
    def _():
        for jjs in chunks(grp):
            for jj, y in zip(jjs, time_rows(jjs)):
                out = gather(x2_ref, jj) * (y + z_ref[j * grp + jj].astype(F32) * bias_ref[1:2, :])
                o_ref[0, jj] = out[:h1].astype(o_ref.dtype)
                o_ref[1, jj] = out[h1:].astype(o_ref.dtype)


def _hyena_call(hv, hf, e1, e4, g2, g3, bias):
    _, bsz, nct, _, h1, ct = hv.shape
    assert bsz == 2, "the two batch rows are packed as the real and imaginary parts of one FFT"
    n1t = 2 * h1
    ns, grp, kgrp = _fft_groups(n1t, CONV_FFT_STEPS)
    stride = 2 * n1t + SLAB_PAD

    def hspec(which, first, last):
        return pl.BlockSpec((None, bsz, None, grp, h1, ct),
                            lambda c, p, j: (which, 0, c, _hold(p, j, first, last, ns), 0, 0))

    def hf_map(c, p, j):
        order = jnp.where(p >= 2, 1, 0)
        step = jnp.where((p == 1) | (p == 3), j, jnp.where((p == 0) | (p == 2), 0, ns - 1))
        return (order, c, step, 0, 0)

    return pl.pallas_call(
        functools.partial(_hyena_kernel, n1t=n1t, grp=grp, kgrp=kgrp, stride=stride),
        grid=(nct, 5, ns),
        in_specs=[hspec(0, 0, 2), hspec(1, 2, 2), hspec(2, 4, 4),
                  pl.BlockSpec((None, None, kgrp, 2 * FFT_INNER, ct), hf_map),
                  pl.BlockSpec((grp, 2 * n1t, n1t), lambda c, p, j: (_hold(p, j, 0, 2, ns), 0, 0)),
                  pl.BlockSpec((grp, n1t, 2 * n1t), lambda c, p, j: (_hold(p, j, 2, 4, ns), 0, 0)),
                  pl.BlockSpec((2 * FFT_INNER, 2 * FFT_INNER), lambda c, p, j: (0, 0)),
                  pl.BlockSpec((2 * FFT_INNER, 2 * FFT_INNER), lambda c, p, j: (0, 0)),
                  pl.BlockSpec((2, ct), lambda c, p, j: (0, c))],
        out_specs=pl.BlockSpec((bsz, None, grp, h1, ct), lambda c, p, j: (0, c, _hold(p, j, 4, 4, ns), 0, 0)),
        out_shape=jax.ShapeDtypeStruct((bsz, nct, FFT_INNER, h1, ct), BF16),
        scratch_shapes=[pltpu.VMEM((ct // LANES, FFT_INNER * stride, LANES), F32),
                        pltpu.VMEM((FFT_INNER, n1t, ct), BF16)],
        compiler_params=_cparams(("arbitrary", "arbitrary", "arbitrary"), VMEM_LIMIT_CONV),
        name="hyena",
    )(hv, hv, hv, hf, e1, e4, g2, g3, bias)


def _merge_kernel(of_ref, ob_ref, gp_ref, hy_ref, x_ref, mod_ref, dng_ref, wpa_ref, wpb_ref, wo_ref,
                  fg_ref, o_ref, *, d):
    b = pl.program_id(0)
    blocks = [slice(r * MXU_TILE, (r + 1) * MXU_TILE) for r in range(o_ref.shape[1] // MXU_TILE)]
    f32 = lambda ref, *idx: ref[idx].astype(F32)

    def head_norm(rows):
        o = f32(of_ref, 0, rows) + f32(ob_ref, 0, rows)
        za = f32(gp_ref, 0, 0, rows)
        parts = []
        for hh in range(HEADS):
            hs = slice(hh * HEAD_DIM, (hh + 1) * HEAD_DIM)
            oh = o[:, hs]
            ms = jnp.mean(oh * oh, axis=-1, keepdims=True)
            parts.append((oh * lax.rsqrt(ms + EPS) * dng_ref[...] * za[:, hs]).astype(BF16))
        return jnp.concatenate(parts, axis=1)

    o_a = [head_norm(rows) for rows in blocks]
    y_a = [jnp.dot(a, wpa_ref[...], preferred_element_type=F32) for a in o_a]
    o_b = [(f32(hy_ref, 0, rows) * f32(gp_ref, 1, 0, rows)).astype(BF16) for rows in blocks]
    y_b = [jnp.dot(a, wpb_ref[...], preferred_element_type=F32) for a in o_b]
    m = [(f32(gp_ref, 2, 0, rows) * ya + f32(gp_ref, 3, 0, rows) * yb).astype(BF16)
         for rows, ya, yb in zip(blocks, y_a, y_b)]
    y = [jnp.dot(a, wo_ref[...], preferred_element_type=F32) for a in m]
    gate = mod_ref[pl.ds(b, 1), :][:, 2 * d:3 * d]
    for rows, yr in zip(blocks, y):
        xn = x_ref[0, rows, :] + gate * yr
        o_ref[0, rows, :] = xn * lax.rsqrt(jnp.mean(xn * xn, axis=-1, keepdims=True) + EPS) * fg_ref[...]


def _merge_call(o_f, o_b, gp, hy, x, mod, dn_norm_g, w_pa, w_pb, w_out, final_g):
    bsz, n_lat, d = x.shape
    tm = NORM_TILE if n_lat % NORM_TILE == 0 else 256
    wfull = lambda a: pl.BlockSpec(a.shape, lambda b, t: (0,) * a.ndim)
    tok = pl.BlockSpec((1, tm, d), lambda b, t: (b, t, 0))
    return pl.pallas_call(
        functools.partial(_merge_kernel, d=d),
        grid=(bsz, n_lat // tm),
        in_specs=[tok, tok,
                  pl.BlockSpec((4, 1, tm, d), lambda b, t: (0, b, t, 0)),
                  tok, tok, wfull(mod),
                  pl.BlockSpec((1, HEAD_DIM), lambda b, t: (0, 0)),
                  wfull(w_pa), wfull(w_pb), wfull(w_out),
                  pl.BlockSpec((1, d), lambda b, t: (0, 0))],
        out_specs=tok,
        out_shape=jax.ShapeDtypeStruct((bsz, n_lat, d), F32),
        compiler_params=_cparams(("arbitrary", "arbitrary")),
        name="merge",
    )(o_f, o_b, gp, hy, x, mod, dn_norm_g.reshape(1, HEAD_DIM), w_pa, w_pb, w_out, final_g.reshape(1, d))


def _position_features(n_lat):
    h1 = n_lat // FFT_INNER
    m = (np.arange(FFT_INNER)[:, None] + FFT_INNER * np.arange(h1)[None, :]).reshape(-1)
    lag = np.concatenate([m, n_lat - m]).astype(np.float64)
    lag[n_lat] = 0.0
    bands = (HY_EMB - 1) // 2
    t = (lag / (n_lat - 1))[:, None]
    wpos = (2.0 * math.pi / n_lat) * lag[:, None]
    fb = np.linspace(1e-4, bands - 1, bands)[None, :]
    pe = np.concatenate([t, np.cos(fb * wpos), -np.sin(fb * wpos)], axis=1)
    pe_pad = np.zeros((2 * n_lat, LANES), np.float32)
    pe_pad[:, :HY_EMB] = pe
    return pe_pad, np.ascontiguousarray(t, dtype=np.float32)


def kernel(x, c, ctx, c_ctx, w_mod, b_mod, norm_g, w_in, dn_conv_w, dn_a_log, dn_dt_bias, dn_norm_g,
           hy_conv_w, hy_conv_b, hy_f_w1, hy_f_b1, hy_f_w2, hy_f_b2, hy_f_w3, hy_f_b3, hy_f_wout,
           hy_f_freq, hy_bias, w_pa, w_pb, w_out, final_g):
    bsz, n_lat, d = x.shape
    n_ctx = ctx.shape[1]
    assert w_mod.shape[0] == 1, "single layer: the context stream is only read through its scan states"
    dn = HEADS * HEAD_DIM
    hy = hy_bias.shape[-1]
    assert n_lat % 256 == 0 and n_ctx % 256 == 0 and dn == d and hy == d

    cvec = jnp.zeros((SUBLANES, d), F32).at[:bsz].set(c).at[bsz].set(c_ctx)
    mod = _mod_call(cvec, w_mod[0], b_mod[0])

    w = w_in[0].astype(BF16)
    o_qkv, o_gate, o_hy, o_gp = 0, 4 * dn, 4 * dn + 4 * HEADS, 4 * dn + 4 * HEADS + 3 * hy
    col3 = lambda a, off, n: jnp.transpose(a[:, off:off + 3 * n].reshape(a.shape[0], 3, n), (1, 0, 2))
    lt = n_lat + n_ctx
    tm = PROJ_TILE if n_lat % PROJ_TILE == 0 else 256

    wg = jnp.zeros((d, LANES), BF16).at[:, :4 * HEADS].set(w[:, o_gate:o_gate + 4 * HEADS])
    lane_pad = lambda a: jnp.zeros((1, LANES), F32).at[0, 2 * HEADS:4 * HEADS].set(a.reshape(-1))
    gate_args = (wg, lane_pad(dn_a_log[0]), lane_pad(dn_dt_bias[0]))
    h_tok, gates = _hnorm_call(x, mod, norm_g[0], *gate_args, tm=NORM_TILE, gate_rows=lt, gate_tile0=0, name="hnorm")
    h_ctx, gates = _hnorm_call(ctx, mod, norm_g[0], *gate_args, tm=n_ctx, gate_rows=lt, gate_tile0=n_lat // n_ctx,
                               mod_row=bsz, prev_gates=gates, name="hnorm_ctx")
    h1 = n_lat // FFT_INNER
    h_perm = jnp.transpose(h_tok.reshape(bsz, h1, FFT_INNER, d), (0, 2, 1, 3)).reshape(bsz, n_lat, d)

    w_qkv, cw_qkv = col3(w, o_qkv, dn), col3(dn_conv_w[0], 0, dn)
    qkv = _convproj_call(h_tok, w_qkv, cw_qkv, rowlen=GRID_W, tm=tm, tile0=0, out_rows=lt, name="qkv")
    qkv = _convproj_call(h_ctx, w_qkv, cw_qkv, rowlen=n_ctx, tm=n_ctx, tile0=n_lat // n_ctx, out_rows=lt,
                         prev=qkv, name="qkv_ctx")
    gates_t = jnp.transpose(gates[:, :, :4 * HEADS].reshape(bsz, lt // CHUNK, CHUNK, 4 * HEADS), (0, 1, 3, 2))
    tall, wide, el = _prep_call(qkv, gates, gates_t)
    o_f, o_b = _scan_call(tall, wide, el, n_lat, n_ctx)

    hyp = _hyproj_call(h_perm, col3(w, o_hy, hy), col3(hy_conv_w[0], 0, hy), hy_conv_b[0].reshape(3, 1, hy))
    w4 = jnp.stack([w[:, 3 * dn:4 * dn]] + [w[:, o_gp + i * d:o_gp + (i + 1) * d] for i in range(3)])
    gp = _gproj_call(h_tok, w4, n_lat)

    pe, tcol = _position_features(n_lat)
    w1p = jnp.zeros((LANES, hy_f_w1.shape[-1]), F32).at[:HY_EMB].set(hy_f_w1[0])
    pe2 = jnp.asarray(np.concatenate([pe[:n_lat], pe[n_lat:]], axis=1))
    a3 = _hidden_call(pe2, w1p, hy_f_b1[0], hy_f_w2[0], hy_f_b2[0], hy_f_w3[0], hy_f_b3[0], hy_f_freq[0])
    deltas = np.abs(np.linspace(math.log(HY_DECAY_TARGET) / HY_SLOW_DECAY,
                                math.log(HY_DECAY_TARGET) / HY_FAST_DECAY, hy)).astype(np.float32)
    deltas2 = jnp.asarray(np.tile(deltas, 2)[None, :])
    wside = hy_f_wout[0].reshape(-1, 2, 2 * hy).transpose(1, 0, 2).astype(BF16)
    zside = jnp.zeros_like(wside[0])
    wout2 = jnp.stack([jnp.concatenate([wside[0], zside]), jnp.concatenate([zside, wside[1]])])
    hfull, ssq = _ftime_call(a3, jnp.asarray(tcol), wout2, deltas2, n_lat)
    e1, e4, ef, g2, g3 = (jnp.asarray(t).astype(BF16) for t in _fft_tables(n_lat))
    hf = _ffft_call(hfull, ssq, ef, g2, n_lat)

    yh = _hyena_call(hyp, hf, e1, e4, g2, g3, hy_bias[0])
    yh = jnp.transpose(yh, (0, 3, 2, 1, 4)).reshape(bsz, n_lat, hy)

    return _merge_call(o_f, o_b, gp, yh, x, mod, dn_norm_g[0], w_pa[0].astype(BF16), w_pb[0].astype(BF16),
                       w_out[0].astype(BF16), final_g)
```

```python
import functools
import math

import numpy as np
import jax
import jax.numpy as jnp
from jax import lax
from jax.experimental import pallas as pl
from jax.experimental.pallas import tpu as pltpu

F32 = jnp.float32
BF16 = jnp.bfloat16
HIGHEST = lax.Precision.HIGHEST

EPS = 1e-6
HEADS = 8
HEAD_DIM = 128
CHUNK = 64
GRID_W = 64
HY_EMB = 33
HY_DECAY_TARGET = 1e-2
HY_FAST_DECAY = 0.3
HY_SLOW_DECAY = 1.5

LANES = 128
SUBLANES = 8
FFT_INNER = 128
CONV_FFT_STEPS = 8
FILTER_FFT_STEPS = 4
FFT_CHANNELS = 256
FFT_INTERLEAVE = 8
PROJ_TILE = 1024
NORM_TILE = 512
SCAN_CHUNKS = 4
PREP_CHUNKS = 1
HYPROJ_N1 = 16
MXU_TILE = 256
SLAB_PAD = 8
VMEM_LIMIT = 56 * 1024 * 1024
VMEM_LIMIT_CONV = 62 * 1024 * 1024


def _cparams(sem, vmem=VMEM_LIMIT):
    return pltpu.CompilerParams(dimension_semantics=sem, vmem_limit_bytes=vmem)


def _bdot(a, b):
    return jnp.dot(a.astype(BF16), b.astype(BF16), preferred_element_type=F32)


def _silu(x):
    return x * jax.nn.sigmoid(x)


def _mod_kernel(c_ref, w_ref, b_ref, o_ref):
    s = _silu(c_ref[...])
    o_ref[...] = jnp.dot(s, w_ref[...], precision=HIGHEST, preferred_element_type=F32) + b_ref[...]


def _mod_call(cvec, w_mod, b_mod):
    rows, d = cvec.shape
    n = w_mod.shape[1]
    tn = 1024
    return pl.pallas_call(
        _mod_kernel,
        grid=(n // tn,),
        in_specs=[pl.BlockSpec((rows, d), lambda j: (0, 0)),
                  pl.BlockSpec((d, tn), lambda j: (0, j)),
                  pl.BlockSpec((1, tn), lambda j: (0, j))],
        out_specs=pl.BlockSpec((rows, tn), lambda j: (0, j)),
        out_shape=jax.ShapeDtypeStruct((rows, n), F32),
        compiler_params=_cparams(("arbitrary",)),
        name="mod",
    )(cvec, w_mod, b_mod.reshape(1, n))


def _norm_rows(x, m, g_ref, d):
    y = x * lax.rsqrt(jnp.mean(x * x, axis=-1, keepdims=True) + EPS) * g_ref[...]
    return (y * (1.0 + m[:, d:2 * d]) + m[:, 0:d]).astype(BF16)


def _gate_rows(h, w_ref, alog_ref, dtb_ref):
    z = jnp.dot(h, w_ref[...], preferred_element_type=F32)
    lane = lax.broadcasted_iota(jnp.int32, z.shape, 1)
    u = z + dtb_ref[...]
    softplus = jnp.maximum(u, 0.0) + jnp.log(1.0 + jnp.exp(-jnp.abs(u)))
    return jnp.where(lane < 2 * HEADS, jax.nn.sigmoid(z), -jnp.exp(alog_ref[...]) * softplus)


def _hnorm_kernel(tok_ref, mod_ref, g_ref, w_ref, alog_ref, dtb_ref, *rest, d, mod_row):
    o_ref, og_ref = rest[-2:]
    row = pl.program_id(0) if mod_row is None else mod_row
    m = mod_ref[pl.ds(row, 1), :]
    h = _norm_rows(tok_ref[0], m, g_ref, d)
    o_ref[0] = h
    og_ref[0] = _gate_rows(h, w_ref, alog_ref, dtb_ref)


def _hnorm_call(tok, mod, norm_g, wg, alog, dtb, *, tm, gate_rows, gate_tile0, mod_row=None, prev_gates=None, name):
    bsz, rows, d = tok.shape
    full = lambda a: pl.BlockSpec(a.shape, lambda b, t: (0,) * a.ndim)
    args = [tok, mod, norm_g.reshape(1, d), wg, alog, dtb]
    in_specs = [pl.BlockSpec((1, tm, d), lambda b, t: (b, t, 0))] + [full(a) for a in args[1:]]
    aliases = {}
    if prev_gates is not None:
        in_specs.append(pl.BlockSpec(memory_space=pl.ANY))
        args.append(prev_gates)
        aliases = {6: 1}
    return pl.pallas_call(
        functools.partial(_hnorm_kernel, d=d, mod_row=mod_row),
        grid=(bsz, rows // tm),
        in_specs=in_specs,
        out_specs=[pl.BlockSpec((1, tm, d), lambda b, t: (b, t, 0)),
                   pl.BlockSpec((1, tm, LANES), lambda b, t: (b, t + gate_tile0, 0))],
        out_shape=[jax.ShapeDtypeStruct((bsz, rows, d), BF16), jax.ShapeDtypeStruct((bsz, gate_rows, LANES), F32)],
        input_output_aliases=aliases,
        compiler_params=_cparams(("arbitrary", "arbitrary")),
        name=name,
    )(*args)


@functools.lru_cache(maxsize=None)
def _shift_matrix(rowlen, taps, blk):
    t = np.arange(blk)
    mats = []
    for j in range(taps):
        d = j - taps // 2
        if d != 0:
            ok = ((t + d) >= 0) & ((t + d) < blk) & (t // rowlen == (t + d) // rowlen)
            s = np.zeros((blk, blk), np.float32)
            s[t[ok], t[ok] + d] = 1.0
            mats.append(s)
    return np.concatenate(mats, axis=1)


def _shift_block(rowlen, taps):
    return max(rowlen, MXU_TILE // (taps - 1) // rowlen * rowlen)


def _convproj_kernel(h_ref, w_ref, cw_ref, sh_ref, *rest, tm):
    o_ref = rest[-1]
    j = pl.program_id(0)
    taps = cw_ref.shape[1]
    mid = taps // 2
    blk = sh_ref.shape[0]
    mblk = max(blk, MXU_TILE)
    na = jnp.where(j == 0, float(HEAD_DIM), jnp.where(j == 1, 1.0, 0.0))
    nb = jnp.where(j == 0, EPS * HEAD_DIM, jnp.where(j == 1, EPS, 1.0))
    cw16 = cw_ref[0].astype(BF16)
    for r in range(tm // mblk):
        zm = jnp.dot(h_ref[0, r * mblk:(r + 1) * mblk, :], w_ref[0], preferred_element_type=F32)
        for g in range(mblk // blk):
            rows = slice(r * mblk + g * blk, r * mblk + (g + 1) * blk)
            zb = zm[g * blk:(g + 1) * blk]
            zb16 = zb.astype(BF16)
            side = jnp.concatenate([zb16 * cw16[jt:jt + 1, :] for jt in range(taps) if jt != mid], axis=0)
            y = zb * cw_ref[0, mid:mid + 1, :] + jnp.dot(sh_ref[...], side, preferred_element_type=F32)
            y = _silu(y)
            for hh in range(HEADS):
                yh = y[:, hh * HEAD_DIM:(hh + 1) * HEAD_DIM]
                nrm = jnp.sum(yh * yh, axis=-1, keepdims=True)
                o_ref[0, 0, rows, hh * HEAD_DIM:(hh + 1) * HEAD_DIM] = (
                    yh * lax.rsqrt(nrm * na + nb)).astype(o_ref.dtype)


def _convproj_call(h, w3, cw3, *, rowlen, tm, tile0, out_rows, prev=None, name):
    bsz, rows, d = h.shape
    ntiles = rows // tm
    nw = w3.shape[2]
    taps = cw3.shape[1]
    blk = _shift_block(rowlen, taps)
    assert tm % max(blk, MXU_TILE) == 0 and max(blk, MXU_TILE) % blk == 0
    sh = jnp.asarray(_shift_matrix(rowlen, taps, blk)).astype(BF16)
    in_specs = [pl.BlockSpec((1, tm, d), lambda j, b, t: (b, t, 0)),
                pl.BlockSpec((1, d, nw), lambda j, b, t: (j, 0, 0)),
                pl.BlockSpec((1, cw3.shape[1], nw), lambda j, b, t: (j, 0, 0)),
                pl.BlockSpec(sh.shape, lambda j, b, t: (0, 0))]
    args = [h, w3, cw3, sh]
    aliases = {}
    if prev is not None:
        in_specs.append(pl.BlockSpec(memory_space=pl.ANY))
        args.append(prev)
        aliases = {4: 0}
    return pl.pallas_call(
        functools.partial(_convproj_kernel, tm=tm),
        grid=(3, bsz, ntiles),
        in_specs=in_specs,
        out_specs=pl.BlockSpec((1, 1, tm, nw), lambda j, b, t: (j, b, t + tile0, 0)),
        out_shape=jax.ShapeDtypeStruct((3, bsz, out_rows, nw), BF16),
        input_output_aliases=aliases,
        compiler_params=_cparams(("arbitrary", "arbitrary", "arbitrary")),
        name=name,
    )(*args)


def _hyproj_kernel(h_ref, w_ref, cw_ref, cb_ref, o_ref):
    n2s, n1s, d = h_ref.shape[1:]
    taps = cw_ref.shape[1]
    hrows = h_ref[0].reshape(n2s * n1s, d)
    step = min(MXU_TILE, n2s * n1s)
    z = jnp.concatenate([jnp.dot(hrows[r:r + step], w_ref[0], preferred_element_type=F32)
                         for r in range(0, n2s * n1s, step)], axis=0)
    y = cb_ref[0] + z * cw_ref[0, taps // 2:taps // 2 + 1, :]
    for jt in range(taps):
        s = (jt - taps // 2) * n1s
        if s > 0:
            y = y + jnp.concatenate([z[s:], jnp.zeros((s, z.shape[1]), F32)], axis=0) * cw_ref[0, jt:jt + 1, :]
        elif s < 0:
            y = y + jnp.concatenate([jnp.zeros((-s, z.shape[1]), F32), z[:s]], axis=0) * cw_ref[0, jt:jt + 1, :]
    o_ref[0, 0] = y.astype(o_ref.dtype).reshape(n2s, n1s, y.shape[1])


def _hyproj_call(h_perm, w3, cw3, cb3):
    bsz, n_lat, d = h_perm.shape
    nw = w3.shape[2]
    h1 = n_lat // FFT_INNER
    n1s = min(HYPROJ_N1, h1)
    hv = h_perm.reshape(bsz, FFT_INNER, h1, d)
    return pl.pallas_call(
        _hyproj_kernel,
        grid=(3, bsz, FFT_INNER // GRID_W, h1 // n1s),
        in_specs=[pl.BlockSpec((1, GRID_W, n1s, d), lambda j, b, r, t: (b, r, t, 0)),
                  pl.BlockSpec((1, d, nw), lambda j, b, r, t: (j, 0, 0)),
                  pl.BlockSpec((1, cw3.shape[1], nw), lambda j, b, r, t: (j, 0, 0)),
                  pl.BlockSpec((1, 1, nw), lambda j, b, r, t: (j, 0, 0))],
        out_specs=pl.BlockSpec((1, 1, GRID_W, n1s, nw), lambda j, b, r, t: (j, b, r, t, 0)),
        out_shape=jax.ShapeDtypeStruct((3, bsz, FFT_INNER, h1, nw), BF16),
        compiler_params=_cparams(("arbitrary",) * 4),
        name="hyproj",
    )(hv, w3, cw3, cb3)


def _gproj_kernel(h_ref, w_ref, o_ref):
    j = pl.program_id(0)
    for r in range(h_ref.shape[1] // MXU_TILE):
        rows = slice(r * MXU_TILE, (r + 1) * MXU_TILE)
        z = jnp.dot(h_ref[0, rows, :], w_ref[0], preferred_element_type=F32)
        s = jax.nn.sigmoid(z)
        o_ref[0, 0, rows, :] = jnp.where(j < 2, z * s, s).astype(BF16)


def _gproj_call(h, w4, n_lat):
    bsz, _, d = h.shape
    nw = w4.shape[2]
    tm = PROJ_TILE if n_lat % PROJ_TILE == 0 else 256
    return pl.pallas_call(
        _gproj_kernel,
        grid=(4, bsz, n_lat // tm),
        in_specs=[pl.BlockSpec((1, tm, d), lambda j, b, t: (b, t, 0)),
                  pl.BlockSpec((1, d, nw), lambda j, b, t: (j, 0, 0))],
        out_specs=pl.BlockSpec((1, 1, tm, nw), lambda j, b, t: (j, b, t, 0)),
        out_shape=jax.ShapeDtypeStruct((4, bsz, n_lat, nw), BF16),
        compiler_params=_cparams(("arbitrary", "arbitrary", "arbitrary")),
        name="gproj",
    )(h, w4)


def _unit_tri_inverse(ls, lowers, filler):
    c = ls[0].shape[0]
    ri = lax.broadcasted_iota(jnp.int32, (c, c), 0)
    ci = lax.broadcasted_iota(jnp.int32, (c, c), 1)
    same = lambda n: (ri // n) == (ci // n)
    eye = jnp.where(ri == ci, 1.0, 0.0)
    in2 = same(2)
    ts = [eye - jnp.where(in2, l, 0.0) for l in ls]
    n = 2
    nlevels = c.bit_length() - 2
    level = 0
    while n < c:
        blk = same(2 * n) & jnp.logical_not(same(n))
        offs = [jnp.where(blk, l, 0.0).astype(BF16) for l in ls]
        if n % SUBLANES:
            prods = [_bdot(t, off) for t, off in zip(ts, offs)]
            filler(level, nlevels)
            ts = [t - _bdot(p, t) for p, t in zip(prods, ts)]
        else:
            def moving(t, lower):
                first = n if lower else 0
                return jnp.concatenate([t[r:r + n] for r in range(first, c, 2 * n)], axis=0)

            def merged(t, upd, lower):
                pieces, k = [], 0
                for r in range(0, c, n):
                    if ((r // n) % 2 == 1) == lower:
                        pieces.append(upd[k * n:(k + 1) * n])
                        k += 1
                    else:
                        pieces.append(t[r:r + n])
                return jnp.concatenate(pieces, axis=0)

            rows = [moving(t, lo) for t, lo in zip(ts, lowers)]
            prods = [_bdot(r, off) for r, off in zip(rows, offs)]
            filler(level, nlevels)
            upds = [r - _bdot(p, t) for r, p, t in zip(rows, prods, ts)]
            ts = [merged(t, u, lo) for t, u, lo in zip(ts, upds, lowers)]
        n *= 2
        level += 1
    return ts


_NT = (((1,), (1,)), ((), ()))
_TN = (((0,), (0,)), ((), ()))


def _prep_kernel(qkv_ref, g_ref, gt_ref, tall_ref, wide_ref, el_ref, *, bsz):
    c = CHUNK
    ri = lax.broadcasted_iota(jnp.int32, (c, c), 0)
    ci = lax.broadcasted_iota(jnp.int32, (c, c), 1)
    tri_l = jnp.where(ri >= ci, 1.0, 0.0)
    tri_u = jnp.where(ri <= ci, 1.0, 0.0)
    incl = (ri >= ci, ri <= ci)
    strict = (ri > ci, ri < ci)
    hdot = functools.partial(jnp.dot, precision=HIGHEST, preferred_element_type=F32)

    per = tall_ref.shape[0]
    cb = [(ch, b) for ch in range(per) for b in range(bsz)]
    rows = lambda ch: slice(ch * c, (ch + 1) * c)
    gs = [g_ref[b, rows(ch), :] for ch, b in cb]
    gts = [gt_ref[b, ch] for ch, b in cb]
    gcol = [(hdot(tri_l, g), hdot(tri_u, g)) for g in gs]
    grow = [(hdot(gt, tri_u), hdot(gt, tri_l)) for gt in gts]
    glast = [(gc[0][c - 1:c], gc[1][0:1]) for gc in gcol]
    el_ref[...] = jnp.zeros_like(el_ref)
    for p, (ch, b) in enumerate(cb):
        for d in range(2):
            el_ref[ch, d * bsz + b:d * bsz + b + 1, :] = jnp.exp(glast[p][d])

    pbh = [(p, hh) for p in range(len(cb)) for hh in range(HEADS)]
    hs = lambda hh: slice(hh * HEAD_DIM, (hh + 1) * HEAD_DIM)
    tok = lambda w, p, hh: qkv_ref[w, cb[p][1], rows(cb[p][0]), hs(hh)]
    ks = [tok(1, p, hh) for p, hh in pbh]
    qs = [tok(0, p, hh) for p, hh in pbh]
    kk = [lax.dot_general(k, k, _NT, preferred_element_type=F32) for k in ks]
    qk = [lax.dot_general(q, k, _NT, preferred_element_type=F32) for q, k in zip(qs, ks)]
    inst = [(d, p, hh, i) for d in range(2) for i, (p, hh) in enumerate(pbh)]
    decs, betas, gcs = [], [], []
    for d, p, hh, _ in inst:
        lg = 2 * HEADS + d * HEADS + hh
        gcs.append(gcol[p][d][:, lg:lg + 1])
        betas.append(jnp.broadcast_to(gs[p][:, d * HEADS + hh:d * HEADS + hh + 1], (c, HEAD_DIM)))
        decs.append(jnp.exp(jnp.where(incl[d], gcs[-1] - grow[p][d][lg:lg + 1, :], -jnp.inf)))
    ls = [jnp.where(strict[d], kk[i] * beta[:, 0:c] * dec, 0.0)
          for (d, _, _, i), beta, dec in zip(inst, betas, decs)]

    def scaled_operands(part, nparts):
        for (d, p, hh, i), dec, beta, gc in list(zip(inst, decs, betas, gcs))[part::nparts]:
            ch, b = cb[p]
            lg = 2 * HEADS + d * HEADS + hh
            eg = jnp.broadcast_to(jnp.exp(gc), (c, HEAD_DIM))
            k = ks[i].astype(F32)
            tall_ref[ch, d, b, hh, :, c:2 * c] = (qk[i] * dec).astype(BF16)
            wide_ref[ch, d, b, hh, 0:c, :] = (qs[i].astype(F32) * eg).astype(BF16)
            wide_ref[ch, d, b, hh, c:2 * c, :] = (k * beta * eg).astype(BF16)
            wide_ref[ch, d, b, hh, 2 * c:3 * c, :] = (tok(2, p, hh).astype(F32) * beta).astype(BF16)
            wide_ref[ch, d, b, hh, 3 * c:4 * c, :] = (k * jnp.exp(glast[p][d][:, lg:lg + 1] - gc)).astype(BF16)

    ts = _unit_tri_inverse(ls, [d == 0 for d, _, _, _ in inst], scaled_operands)
    for (d, p, hh, _), t in zip(inst, ts):
        tall_ref[cb[p][0], d, cb[p][1], hh, :, 0:c] = t.astype(BF16)


def _prep_call(qkv, gates, gates_t):
    _, bsz, lt, dn = qkv.shape
    nc = lt // CHUNK
    per = PREP_CHUNKS
    assert nc % per == 0
    tall = (2, bsz, HEADS, CHUNK, 2 * CHUNK)
    wide = (2, bsz, HEADS, 4 * CHUNK, HEAD_DIM)
    return pl.pallas_call(
        functools.partial(_prep_kernel, bsz=bsz),
        grid=(nc // per,),
        in_specs=[pl.BlockSpec((3, bsz, per * CHUNK, dn), lambda i: (0, 0, i, 0)),
                  pl.BlockSpec((bsz, per * CHUNK, LANES), lambda i: (0, i, 0)),
                  pl.BlockSpec((bsz, per, 4 * HEADS, CHUNK), lambda i: (0, i, 0, 0))],
        out_specs=[pl.BlockSpec((per,) + tall, lambda i: (i, 0, 0, 0, 0, 0)),
                   pl.BlockSpec((per,) + wide, lambda i: (i, 0, 0, 0, 0, 0)),
                   pl.BlockSpec((per, SUBLANES, LANES), lambda i: (i, 0, 0))],
        out_shape=[jax.ShapeDtypeStruct((nc,) + tall, BF16),
                   jax.ShapeDtypeStruct((nc,) + wide, BF16),
                   jax.ShapeDtypeStruct((nc, SUBLANES, LANES), F32)],
        compiler_params=_cparams(("arbitrary",)),
        name="prep",
    )(qkv, gates, gates_t)


def _scan_kernel(tf_ref, tb_ref, wf_ref, wb_ref, ef_ref, eb_ref, of_ref, ob_ref, s_ref, *, bsz):
    i = pl.program_id(0)

    @pl.when(i == 0)
    def _():
        s_ref[...] = jnp.zeros_like(s_ref)

    c = CHUNK
    per = tf_ref.shape[0]
    dirs = ((tf_ref, wf_ref, ef_ref, of_ref), (tb_ref, wb_ref, eb_ref, ob_ref))
    inst = [(d, b, hh) for d in range(2) for b in range(bsz) for hh in range(HEADS)]
    state = [s_ref[d, b, hh] for d, b, hh in inst]
    for sub in range(per):
        at = (sub, per - 1 - sub)
        a1 = [jnp.dot(dirs[d][1][at[d], b, hh, 0:2 * c, :], s.astype(BF16), preferred_element_type=F32)
              for (d, b, hh), s in zip(inst, state)]
        v_new = [jnp.dot(dirs[d][0][at[d], b, hh, :, 0:c],
                         (dirs[d][1][at[d], b, hh, 2 * c:3 * c, :].astype(F32) - a[c:]).astype(BF16),
                         preferred_element_type=F32).astype(BF16) for (d, b, hh), a in zip(inst, a1)]
        new_state = []
        for (d, b, hh), a, vn, s in zip(inst, a1, v_new, state):
            lg = 2 * HEADS + d * HEADS + hh
            o = a[:c] + jnp.dot(dirs[d][0][at[d], b, hh, :, c:2 * c], vn, preferred_element_type=F32)
            dirs[d][3][b, at[d] * c:(at[d] + 1) * c, hh * HEAD_DIM:(hh + 1) * HEAD_DIM] = o.astype(BF16)
            el = dirs[d][2][at[d], d * bsz + b:d * bsz + b + 1, lg:lg + 1]
            new_state.append(s * el + lax.dot_general(dirs[d][1][at[d], b, hh, 3 * c:4 * c, :], vn, _TN,
                                                      preferred_element_type=F32))
        state = new_state
    for (d, b, hh), s in zip(inst, state):
        s_ref[d, b, hh] = s


def _scan_call(tall, wide, el, n_lat, n_ctx):
    nc, _, bsz = tall.shape[:3]
    dn = HEADS * HEAD_DIM
    per = SCAN_CHUNKS
    ncx, ncc, ns = n_lat // CHUNK // per, n_ctx // CHUNK // per, nc // per
    assert nc == (ncx + ncc) * per
    cf = lambda i: jnp.where(i < ncc, ncx + i, i - ncc)
    cb = lambda i: ns - 1 - i

    dspec = lambda arr, fn, d: pl.BlockSpec((per, None) + arr.shape[2:], lambda i: (fn(i), d, 0, 0, 0, 0))
    espec = lambda fn: pl.BlockSpec((per, SUBLANES, LANES), lambda i: (fn(i), 0, 0))
    ospec = lambda fn: pl.BlockSpec((bsz, per * CHUNK, dn), lambda i: (0, fn(i), 0))
    return pl.pallas_call(
        functools.partial(_scan_kernel, bsz=bsz),
        grid=(ns,),
        in_specs=[dspec(tall, cf, 0), dspec(tall, cb, 1), dspec(wide, cf, 0), dspec(wide, cb, 1),
                  espec(cf), espec(cb)],
        out_specs=[ospec(lambda i: jnp.maximum(i - ncc, 0)), ospec(lambda i: jnp.minimum(ns - 1 - i, ncx - 1))],
        out_shape=[jax.ShapeDtypeStruct((bsz, n_lat, dn), BF16)] * 2,
        scratch_shapes=[pltpu.VMEM((2, bsz, HEADS, HEAD_DIM, HEAD_DIM), F32)],
        compiler_params=_cparams(("arbitrary",)),
        name="scan",
    )(tall, tall, wide, wide, el, el)


def _hidden_kernel(pe_ref, w1_ref, b1_ref, w2_ref, b2_ref, w3_ref, b3_ref, f_ref, o_ref):
    f = f_ref[...]
    dot = functools.partial(jnp.dot, precision=HIGHEST, preferred_element_type=F32)
    a = jnp.sin(f * (dot(pe_ref[...], w1_ref[...]) + b1_ref[...]))
    a = jnp.sin(f * (dot(a, w2_ref[...]) + b2_ref[...]))
    o_ref[...] = jnp.sin(f * (dot(a, w3_ref[...]) + b3_ref[...]))


def _hidden_call(pe, w1p, b1, w2, b2, w3, b3, freq):
    rows, pw = pe.shape
    fw = 2 * w2.shape[0]
    tr = 512
    full = lambda a: pl.BlockSpec(a.shape, lambda t: (0,) * a.ndim)
    two = lambda a: jnp.tile(a.reshape(1, -1), (1, 2))
    diag2 = lambda a: jnp.kron(jnp.eye(2, dtype=a.dtype), a)
    args = (diag2(w1p), two(b1), diag2(w2), two(b2), diag2(w3), two(b3), two(freq))
    return pl.pallas_call(
        _hidden_kernel,
        grid=(rows // tr,),
        in_specs=[pl.BlockSpec((tr, pw), lambda t: (t, 0))] + [full(a) for a in args],
        out_specs=pl.BlockSpec((tr, fw), lambda t: (t, 0)),
        out_shape=jax.ShapeDtypeStruct((rows, fw), F32),
        compiler_params=_cparams(("arbitrary",)),
        name="hidden",
    )(pe, *args)


def _ftime_kernel(a_ref, t_ref, w_ref, dl_ref, h_ref, ss_ref, *, n_lat, tr):
    t = pl.program_id(0)
    h = _bdot(a_ref[...], w_ref[...]) * jnp.exp(-t_ref[...] * dl_ref[...])
    row = lax.broadcasted_iota(jnp.int32, (tr, 1), 0) + t * tr
    h = jnp.where(row == n_lat, 0.0, h)
    h_ref[...] = h.astype(h_ref.dtype)

    @pl.when(t == 0)
    def _():
        ss_ref[...] = jnp.zeros_like(ss_ref)

    ss_ref[...] += jnp.sum(h * h, axis=0, keepdims=True)


def _ftime_call(a3, tcol, wout2, deltas2, n_lat):
    rows = tcol.shape[0]
    fw = a3.shape[1]
    ncol = wout2.shape[2]
    tr = 512
    side = lambda t: (t * tr) // n_lat
    return pl.pallas_call(
        functools.partial(_ftime_kernel, n_lat=n_lat, tr=tr),
        grid=(rows // tr,),
        in_specs=[pl.BlockSpec((tr, fw), lambda t: (t % (n_lat // tr), 0)),
                  pl.BlockSpec((tr, 1), lambda t: (t, 0)),
                  pl.BlockSpec((None, fw, ncol), lambda t: (side(t), 0, 0)),
                  pl.BlockSpec((1, ncol), lambda t: (0, 0))],
        out_specs=[pl.BlockSpec((tr, ncol), lambda t: (t, 0)),
                   pl.BlockSpec((1, ncol), lambda t: (0, 0))],
        out_shape=[jax.ShapeDtypeStruct((rows, ncol), BF16), jax.ShapeDtypeStruct((1, ncol), F32)],
        compiler_params=_cparams(("arbitrary",)),
        name="ftime",
    )(a3, tcol, wout2, deltas2)


@functools.lru_cache(maxsize=None)
def _fft_tables(n_lat):
    n = 2 * n_lat
    n1t = n // FFT_INNER
    h1 = n1t // 2
    n2 = np.arange(FFT_INNER, dtype=np.int64)[:, None, None]
    k1 = np.arange(n1t, dtype=np.int64)[None, :, None]
    n1 = np.arange(h1, dtype=np.int64)[None, None, :]
    ang = (-2.0 * np.pi / n) * ((k1 * (FFT_INNER * n1 + n2)) % n)
    er, ei = np.cos(ang), np.sin(ang)
    e1 = np.concatenate([np.concatenate([er, -ei], axis=2), np.concatenate([ei, er], axis=2)], axis=1)
    e4 = np.transpose(e1, (0, 2, 1))
    sgn = np.where(np.arange(n1t) % 2 == 0, 1.0, -1.0)[None, :, None]
    ef = np.concatenate([np.concatenate([er, sgn * er], axis=2), np.concatenate([ei, sgn * ei], axis=2)], axis=1)
    a = np.arange(FFT_INNER, dtype=np.int64)
    ang2 = (-2.0 * np.pi / FFT_INNER) * ((a[:, None] * a[None, :]) % FFT_INNER)
    gr, gi = np.cos(ang2), np.sin(ang2)
    g2 = np.block([[gr, -gi], [gi, gr]])
    f32 = lambda x: np.ascontiguousarray(x, dtype=np.float32)
    return f32(e1), f32(e4), f32(ef), f32(g2), f32(g2.T)


def _hold(p, j, first, last, nsteps):
    active = (p >= first) & (p <= last) & (lax.rem(p - first, 2) == 0)
    return jnp.where(active, j, jnp.where(p < first, 0, nsteps - 1))


def _slab_read(x_ref, rows):
    return jnp.concatenate([x_ref[c, rows, :] for c in range(x_ref.shape[0])], axis=1)


def _slab_write(x_ref, rows, val):
    for c in range(x_ref.shape[0]):
        x_ref[c, rows, :] = val[:, c * LANES:(c + 1) * LANES]


def _spectrum_steps(x_ref, g2_ref, k1s, n1t, stride):
    cols = [jnp.concatenate([_slab_read(x_ref, pl.ds(k1, FFT_INNER, stride=stride)),
                             _slab_read(x_ref, pl.ds(n1t + k1, FFT_INNER, stride=stride))], axis=0).astype(BF16)
            for k1 in k1s]
    return [jnp.dot(g2_ref[...], c, preferred_element_type=F32) for c in cols]


def _ffft_kernel(h_ref, ef_ref, g2_ref, ss_ref, o_ref, x_ref, *, n1t, grp, kgrp, stride, inv_n):
    p = pl.program_id(1)
    j = pl.program_id(2)

    @pl.when(p == 0)
    def _():
        for j0 in range(0, grp, FFT_INTERLEAVE):
            jjs = list(range(j0, min(j0 + FFT_INTERLEAVE, grp)))
            us = [jnp.concatenate([h_ref[0, jj], h_ref[1, jj]], axis=0) for jj in jjs]
            slabs = [_bdot(ef_ref[jj], u) for jj, u in zip(jjs, us)]
            for jj, slab in zip(jjs, slabs):
                base = pl.multiple_of((j * grp + jj) * stride, SUBLANES)
                _slab_write(x_ref, pl.ds(base, 2 * n1t), slab)

    @pl.when(p == 1)
    def _():
        scale = lax.rsqrt(ss_ref[...] + EPS) * inv_n
        for j0 in range(0, kgrp, FFT_INTERLEAVE):
            jjs = list(range(j0, min(j0 + FFT_INTERLEAVE, kgrp)))
            xfs = _spectrum_steps(x_ref, g2_ref, [j * kgrp + jj for jj in jjs], n1t, stride)
            for jj, xf in zip(jjs, xfs):
                o_ref[jj] = (xf * scale).astype(BF16)


def _fft_groups(n1t, ns):
    ns = min(ns, n1t)
    assert FFT_INNER % ns == 0 and n1t % ns == 0
    return ns, FFT_INNER // ns, n1t // ns


def _ffft_call(hfull, ssq, ef, g2, n_lat):
    ncol = hfull.shape[1]
    n1t = 2 * n_lat // FFT_INNER
    h1 = n1t // 2
    ct = FFT_CHANNELS
    ns, grp, kgrp = _fft_groups(n1t, FILTER_FFT_STEPS)
    stride = 2 * n1t + SLAB_PAD
    hv = hfull.reshape(2, FFT_INNER, h1, ncol)
    nct = ncol // ct
    cpo = nct // 2
    return pl.pallas_call(
        functools.partial(_ffft_kernel, n1t=n1t, grp=grp, kgrp=kgrp, stride=stride, inv_n=1.0 / (2 * n_lat)),
        grid=(nct, 2, ns),
        in_specs=[pl.BlockSpec((2, grp, h1, ct), lambda c, p, j: (0, _hold(p, j, 0, 0, ns), 0, c)),
                  pl.BlockSpec((grp, 2 * n1t, n1t), lambda c, p, j: (_hold(p, j, 0, 0, ns), 0, 0)),
                  pl.BlockSpec((2 * FFT_INNER, 2 * FFT_INNER), lambda c, p, j: (0, 0)),
                  pl.BlockSpec((1, ct), lambda c, p, j: (0, c))],
        out_specs=pl.BlockSpec((None, kgrp, 2 * FFT_INNER, ct),
                               lambda c, p, j: (c // cpo, _hold(p, j, 1, 1, ns), 0, c % cpo)),
        out_shape=jax.ShapeDtypeStruct((2, n1t, 2 * FFT_INNER, ncol // 2), BF16),
        scratch_shapes=[pltpu.VMEM((ct // LANES, FFT_INNER * stride, LANES), F32)],
        compiler_params=_cparams(("arbitrary", "arbitrary", "arbitrary")),
        name="ffft",
    )(hv, ef, g2, ssq)


def _hyena_kernel(v_ref, x1_ref, x2_ref, hf_ref, e1_ref, e4_ref, g2_ref, g3_ref, bias_ref,
                  o_ref, x_ref, z_ref, *, n1t, grp, kgrp, stride):
    p = pl.program_id(1)
    j = pl.program_id(2)
    h1 = n1t // 2

    def gather(ref, jj):
        return jnp.concatenate([ref[0, jj], ref[1, jj]], axis=0).astype(F32)

    def slab_base(jj):
        return pl.multiple_of((j * grp + jj) * stride, SUBLANES)

    def time_rows(jjs):
        slabs = [_slab_read(x_ref, pl.ds(slab_base(jj), 2 * n1t)).astype(BF16) for jj in jjs]
        return [jnp.dot(e4_ref[jj], s, preferred_element_type=F32) for jj, s in zip(jjs, slabs)]

    def write_slabs(jjs, us):
        slabs = [jnp.dot(e1_ref[jj], u, preferred_element_type=F32) for jj, u in zip(jjs, us)]
        for jj, slab in zip(jjs, slabs):
            _slab_write(x_ref, pl.ds(slab_base(jj), 2 * n1t), slab)

    def chunks(n):
        return [list(range(j0, min(j0 + FFT_INTERLEAVE, n))) for j0 in range(0, n, FFT_INTERLEAVE)]

    @pl.when(p == 0)
    def _():
        for jjs in chunks(grp):
            write_slabs(jjs, [gather(v_ref, jj).astype(BF16) for jj in jjs])

    @pl.when((p == 1) | (p == 3))
    def _():
        for jjs in chunks(kgrp):
            k1s = [j * kgrp + jj for jj in jjs]
            xfs = _spectrum_steps(x_ref, g2_ref, k1s, n1t, stride)
            ys = []
            for jj, xf in zip(jjs, xfs):
                hf = hf_ref[jj].astype(F32)
                xr, xi = xf[:FFT_INNER], xf[FFT_INNER:]
                hr, hi = hf[:FFT_INNER], hf[FFT_INNER:]
                ys.append(jnp.concatenate([xr * hr - xi * hi, xr * hi + xi * hr], axis=0).astype(BF16))
            bbs = [jnp.dot(g3_ref[...], y, preferred_element_type=F32) for y in ys]
            for k1, bb in zip(k1s, bbs):
                _slab_write(x_ref, pl.ds(k1, FFT_INNER, stride=stride), bb[:FFT_INNER])
                _slab_write(x_ref, pl.ds(n1t + k1, FFT_INNER, stride=stride), bb[FFT_INNER:])

    @pl.when(p == 2)
    def _():
        for jjs in chunks(grp):
            z1s = [(gather(x1_ref, jj) * (y + gather(v_ref, jj) * bias_ref[0:1, :])).astype(BF16)
                   for jj, y in zip(jjs, time_rows(jjs))]
            for jj, z1 in zip(jjs, z1s):
                z_ref[j * grp + jj] = z1
            write_slabs(jjs, z1s)

    @pl.when(p == 4)
    def _():
        for jjs in chunks(grp):
            for jj, y in zip(jjs, time_rows(jjs)):
                out = gather(x2_ref, jj) * (y + z_ref[j * grp + jj].astype(F32) * bias_ref[1:2, :])
                o_ref[0, jj] = out[:h1].astype(o_ref.dtype)
                o_ref[1, jj] = out[h1:].astype(o_ref.dtype)


def _hyena_call(hv, hf, e1, e4, g2, g3, bias):
    _, bsz, _, h1, ch = hv.shape
    assert bsz == 2, "the two batch rows are packed as the real and imaginary parts of one FFT"
    n1t = 2 * h1
    ct = FFT_CHANNELS
    ns, grp, kgrp = _fft_groups(n1t, CONV_FFT_STEPS)
    stride = 2 * n1t + SLAB_PAD

    def hspec(which, first, last):
        return pl.BlockSpec((None, bsz, grp, h1, ct),
                            lambda c, p, j: (which, 0, _hold(p, j, first, last, ns), 0, c))

    def hf_map(c, p, j):
        order = jnp.where(p >= 2, 1, 0)
        step = jnp.where((p == 1) | (p == 3), j, jnp.where((p == 0) | (p == 2), 0, ns - 1))
        return (order, step, 0, c)

    return pl.pallas_call(
        functools.partial(_hyena_kernel, n1t=n1t, grp=grp, kgrp=kgrp, stride=stride),
        grid=(ch // ct, 5, ns),
        in_specs=[hspec(0, 0, 2), hspec(1, 2, 2), hspec(2, 4, 4),
                  pl.BlockSpec((None, kgrp, 2 * FFT_INNER, ct), hf_map),
                  pl.BlockSpec((grp, 2 * n1t, n1t), lambda c, p, j: (_hold(p, j, 0, 2, ns), 0, 0)),
                  pl.BlockSpec((grp, n1t, 2 * n1t), lambda c, p, j: (_hold(p, j, 2, 4, ns), 0, 0)),
                  pl.BlockSpec((2 * FFT_INNER, 2 * FFT_INNER), lambda c, p, j: (0, 0)),
                  pl.BlockSpec((2 * FFT_INNER, 2 * FFT_INNER), lambda c, p, j: (0, 0)),
                  pl.BlockSpec((2, ct), lambda c, p, j: (0, c))],
        out_specs=pl.BlockSpec((bsz, grp, h1, ct), lambda c, p, j: (0, _hold(p, j, 4, 4, ns), 0, c)),
        out_shape=jax.ShapeDtypeStruct((bsz, FFT_INNER, h1, ch), BF16),
        scratch_shapes=[pltpu.VMEM((ct // LANES, FFT_INNER * stride, LANES), F32),
                        pltpu.VMEM((FFT_INNER, n1t, ct), BF16)],
        compiler_params=_cparams(("arbitrary", "arbitrary", "arbitrary"), VMEM_LIMIT_CONV),
        name="hyena",
    )(hv, hv, hv, hf, e1, e4, g2, g3, bias)


def _merge_kernel(of_ref, ob_ref, gp_ref, hy_ref, x_ref, mod_ref, dng_ref, wpa_ref, wpb_ref, wo_ref,
                  fg_ref, o_ref, *, d):
    b = pl.program_id(0)
    blocks = [slice(r * MXU_TILE, (r + 1) * MXU_TILE) for r in range(o_ref.shape[1] // MXU_TILE)]
    f32 = lambda ref, *idx: ref[idx].astype(F32)

    def head_norm(rows):
        o = f32(of_ref, 0, rows) + f32(ob_ref, 0, rows)
        za = f32(gp_ref, 0, 0, rows)
        parts = []
        for hh in range(HEADS):
            hs = slice(hh * HEAD_DIM, (hh + 1) * HEAD_DIM)
            oh = o[:, hs]
            ms = jnp.mean(oh * oh, axis=-1, keepdims=True)
            parts.append((oh * lax.rsqrt(ms + EPS) * dng_ref[...] * za[:, hs]).astype(BF16))
        return jnp.concatenate(parts, axis=1)

    o_a = [head_norm(rows) for rows in blocks]
    y_a = [jnp.dot(a, wpa_ref[...], preferred_element_type=F32) for a in o_a]
    o_b = [(f32(hy_ref, 0, rows) * f32(gp_ref, 1, 0, rows)).astype(BF16) for rows in blocks]
    y_b = [jnp.dot(a, wpb_ref[...], preferred_element_type=F32) for a in o_b]
    m = [(f32(gp_ref, 2, 0, rows) * ya + f32(gp_ref, 3, 0, rows) * yb).astype(BF16)
         for rows, ya, yb in zip(blocks, y_a, y_b)]
    y = [jnp.dot(a, wo_ref[...], preferred_element_type=F32) for a in m]
    gate = mod_ref[pl.ds(b, 1), :][:, 2 * d:3 * d]
    for rows, yr in zip(blocks, y):
        xn = x_ref[0, rows, :] + gate * yr
        o_ref[0, rows, :] = xn * lax.rsqrt(jnp.mean(xn * xn, axis=-1, keepdims=True) + EPS) * fg_ref[...]


def _merge_call(o_f, o_b, gp, hy, x, mod, dn_norm_g, w_pa, w_pb, w_out, final_g):
    bsz, n_lat, d = x.shape
    tm = NORM_TILE if n_lat % NORM_TILE == 0 else 256
    wfull = lambda a: pl.BlockSpec(a.shape, lambda b, t: (0,) * a.ndim)
    tok = pl.BlockSpec((1, tm, d), lambda b, t: (b, t, 0))
    return pl.pallas_call(
        functools.partial(_merge_kernel, d=d),
        grid=(bsz, n_lat // tm),
        in_specs=[tok, tok,
                  pl.BlockSpec((4, 1, tm, d), lambda b, t: (0, b, t, 0)),
                  tok, tok, wfull(mod),
                  pl.BlockSpec((1, HEAD_DIM), lambda b, t: (0, 0)),
                  wfull(w_pa), wfull(w_pb), wfull(w_out),
                  pl.BlockSpec((1, d), lambda b, t: (0, 0))],
        out_specs=tok,
        out_shape=jax.ShapeDtypeStruct((bsz, n_lat, d), F32),
        compiler_params=_cparams(("arbitrary", "arbitrary")),
        name="merge",
    )(o_f, o_b, gp, hy, x, mod, dn_norm_g.reshape(1, HEAD_DIM), w_pa, w_pb, w_out, final_g.reshape(1, d))


def _position_features(n_lat):
    h1 = n_lat // FFT_INNER
    m = (np.arange(FFT_INNER)[:, None] + FFT_INNER * np.arange(h1)[None, :]).reshape(-1)
    lag = np.concatenate([m, n_lat - m]).astype(np.float64)
    lag[n_lat] = 0.0
    bands = (HY_EMB - 1) // 2
    t = (lag / (n_lat - 1))[:, None]
    wpos = (2.0 * math.pi / n_lat) * lag[:, None]
    fb = np.linspace(1e-4, bands - 1, bands)[None, :]
    pe = np.concatenate([t, np.cos(fb * wpos), -np.sin(fb * wpos)], axis=1)
    pe_pad = np.zeros((2 * n_lat, LANES), np.float32)
    pe_pad[:, :HY_EMB] = pe
    return pe_pad, np.ascontiguousarray(t, dtype=np.float32)


def kernel(x, c, ctx, c_ctx, w_mod, b_mod, norm_g, w_in, dn_conv_w, dn_a_log, dn_dt_bias, dn_norm_g,
           hy_conv_w, hy_conv_b, hy_f_w1, hy_f_b1, hy_f_w2, hy_f_b2, hy_f_w3, hy_f_b3, hy_f_wout,
           hy_f_freq, hy_bias, w_pa, w_pb, w_out, final_g):
    bsz, n_lat, d = x.shape
    n_ctx = ctx.shape[1]
    assert w_mod.shape[0] == 1, "single layer: the context stream is only read through its scan states"
    dn = HEADS * HEAD_DIM
    hy = hy_bias.shape[-1]
    assert n_lat % 256 == 0 and n_ctx % 256 == 0 and dn == d and hy == d

    cvec = jnp.zeros((SUBLANES, d), F32).at[:bsz].set(c).at[bsz].set(c_ctx)
    mod = _mod_call(cvec, w_mod[0], b_mod[0])

    w = w_in[0].astype(BF16)
    o_qkv, o_gate, o_hy, o_gp = 0, 4 * dn, 4 * dn + 4 * HEADS, 4 * dn + 4 * HEADS + 3 * hy
    col3 = lambda a, off, n: jnp.transpose(a[:, off:off + 3 * n].reshape(a.shape[0], 3, n), (1, 0, 2))
    lt = n_lat + n_ctx
    tm = PROJ_TILE if n_lat % PROJ_TILE == 0 else 256

    wg = jnp.zeros((d, LANES), BF16).at[:, :4 * HEADS].set(w[:, o_gate:o_gate + 4 * HEADS])
    lane_pad = lambda a: jnp.zeros((1, LANES), F32).at[0, 2 * HEADS:4 * HEADS].set(a.reshape(-1))
    gate_args = (wg, lane_pad(dn_a_log[0]), lane_pad(dn_dt_bias[0]))
    h_tok, gates = _hnorm_call(x, mod, norm_g[0], *gate_args, tm=NORM_TILE, gate_rows=lt, gate_tile0=0, name="hnorm")
    h_ctx, gates = _hnorm_call(ctx, mod, norm_g[0], *gate_args, tm=n_ctx, gate_rows=lt, gate_tile0=n_lat // n_ctx,
                               mod_row=bsz, prev_gates=gates, name="hnorm_ctx")
    h1 = n_lat // FFT_INNER
    h_perm = jnp.transpose(h_tok.reshape(bsz, h1, FFT_INNER, d), (0, 2, 1, 3)).reshape(bsz, n_lat, d)

    w_qkv, cw_qkv = col3(w, o_qkv, dn), col3(dn_conv_w[0], 0, dn)
    qkv = _convproj_call(h_tok, w_qkv, cw_qkv, rowlen=GRID_W, tm=tm, tile0=0, out_rows=lt, name="qkv")
    qkv = _convproj_call(h_ctx, w_qkv, cw_qkv, rowlen=n_ctx, tm=n_ctx, tile0=n_lat // n_ctx, out_rows=lt,
                         prev=qkv, name="qkv_ctx")
    gates_t = jnp.transpose(gates[:, :, :4 * HEADS].reshape(bsz, lt // CHUNK, CHUNK, 4 * HEADS), (0, 1, 3, 2))
    tall, wide, el = _prep_call(qkv, gates, gates_t)
    o_f, o_b = _scan_call(tall, wide, el, n_lat, n_ctx)

    hyp = _hyproj_call(h_perm, col3(w, o_hy, hy), col3(hy_conv_w[0], 0, hy), hy_conv_b[0].reshape(3, 1, hy))
    w4 = jnp.stack([w[:, 3 * dn:4 * dn]] + [w[:, o_gp + i * d:o_gp + (i + 1) * d] for i in range(3)])
    gp = _gproj_call(h_tok, w4, n_lat)

    pe, tcol = _position_features(n_lat)
    w1p = jnp.zeros((LANES, hy_f_w1.shape[-1]), F32).at[:HY_EMB].set(hy_f_w1[0])
    pe2 = jnp.asarray(np.concatenate([pe[:n_lat], pe[n_lat:]], axis=1))
    a3 = _hidden_call(pe2, w1p, hy_f_b1[0], hy_f_w2[0], hy_f_b2[0], hy_f_w3[0], hy_f_b3[0], hy_f_freq[0])
    deltas = np.abs(np.linspace(math.log(HY_DECAY_TARGET) / HY_SLOW_DECAY,
                                math.log(HY_DECAY_TARGET) / HY_FAST_DECAY, hy)).astype(np.float32)
    deltas2 = jnp.asarray(np.tile(deltas, 2)[None, :])
    wside = hy_f_wout[0].reshape(-1, 2, 2 * hy).transpose(1, 0, 2).astype(BF16)
    zside = jnp.zeros_like(wside[0])
    wout2 = jnp.stack([jnp.concatenate([wside[0], zside]), jnp.concatenate([zside, wside[1]])])
    hfull, ssq = _ftime_call(a3, jnp.asarray(tcol), wout2, deltas2, n_lat)
    e1, e4, ef, g2, g3 = (jnp.asarray(t).astype(BF16) for t in _fft_tables(n_lat))
    hf = _ffft_call(hfull, ssq, ef, g2, n_lat)

    yh = _hyena_call(hyp, hf, e1, e4, g2, g3, hy_bias[0])
    yh = jnp.transpose(yh, (0, 2, 1, 3)).reshape(bsz, n_lat, hy)

    return _merge_call(o_f, o_b, gp, yh, x, mod, dn_norm_g[0], w_pa[0].astype(BF16), w_pb[0].astype(BF16),
                       w_out[0].astype(BF16), final_g)
```

```python
import functools
import math

import numpy as np
import jax
import jax.numpy as jnp
from jax import lax
from jax.experimental import pallas as pl
from jax.experimental.pallas import tpu as pltpu

F32 = jnp.float32
BF16 = jnp.bfloat16
HIGHEST = lax.Precision.HIGHEST

EPS = 1e-6
HEADS = 8
HEAD_DIM = 128
CHUNK = 64
GRID_W = 64
HY_EMB = 33
HY_DECAY_TARGET = 1e-2
HY_FAST_DECAY = 0.3
HY_SLOW_DECAY = 1.5

LANES = 128
SUBLANES = 8
FFT_INNER = 128
CONV_FFT_STEPS = 8
FILTER_FFT_STEPS = 4
FFT_CHANNELS = 256
FFT_INTERLEAVE = 8
PROJ_TILE = 1024
NORM_TILE = 512
SCAN_CHUNKS = 4
PREP_CHUNKS = 1
HYPROJ_N1 = 16
MXU_TILE = 256
SLAB_PAD = 8
VMEM_LIMIT = 56 * 1024 * 1024
VMEM_LIMIT_CONV = 62 * 1024 * 1024


def _cparams(sem, vmem=VMEM_LIMIT):
    return pltpu.CompilerParams(dimension_semantics=sem, vmem_limit_bytes=vmem)


def _bdot(a, b):
    return jnp.dot(a.astype(BF16), b.astype(BF16), preferred_element_type=F32)


def _silu(x):
    return x * jax.nn.sigmoid(x)


def _mod_kernel(c_ref, w_ref, b_ref, o_ref):
    s = _silu(c_ref[...])
    o_ref[...] = jnp.dot(s, w_ref[...], precision=HIGHEST, preferred_element_type=F32) + b_ref[...]


def _mod_call(cvec, w_mod, b_mod):
    rows, d = cvec.shape
    n = w_mod.shape[1]
    tn = 1024
    return pl.pallas_call(
        _mod_kernel,
        grid=(n // tn,),
        in_specs=[pl.BlockSpec((rows, d), lambda j: (0, 0)),
                  pl.BlockSpec((d, tn), lambda j: (0, j)),
                  pl.BlockSpec((1, tn), lambda j: (0, j))],
        out_specs=pl.BlockSpec((rows, tn), lambda j: (0, j)),
        out_shape=jax.ShapeDtypeStruct((rows, n), F32),
        compiler_params=_cparams(("arbitrary",)),
        name="mod",
    )(cvec, w_mod, b_mod.reshape(1, n))


def _norm_rows(x, m, g_ref, d):
    y = x * lax.rsqrt(jnp.mean(x * x, axis=-1, keepdims=True) + EPS) * g_ref[...]
    return (y * (1.0 + m[:, d:2 * d]) + m[:, 0:d]).astype(BF16)


def _gate_rows(h, w_ref, alog_ref, dtb_ref):
    z = jnp.dot(h, w_ref[...], preferred_element_type=F32)
    lane = lax.broadcasted_iota(jnp.int32, z.shape, 1)
    u = z + dtb_ref[...]
    softplus = jnp.maximum(u, 0.0) + jnp.log(1.0 + jnp.exp(-jnp.abs(u)))
    return jnp.where(lane < 2 * HEADS, jax.nn.sigmoid(z), -jnp.exp(alog_ref[...]) * softplus)


def _hnorm_kernel(tok_ref, mod_ref, g_ref, w_ref, alog_ref, dtb_ref, *rest, d, mod_row):
    o_ref, og_ref = rest[-2:]
    row = pl.program_id(0) if mod_row is None else mod_row
    m = mod_ref[pl.ds(row, 1), :]
    h = _norm_rows(tok_ref[0], m, g_ref, d)
    o_ref[0] = h
    og_ref[0] = _gate_rows(h, w_ref, alog_ref, dtb_ref)


def _hnorm_call(tok, mod, norm_g, wg, alog, dtb, *, tm, gate_rows, gate_tile0, mod_row=None, prev_gates=None, name):
    bsz, rows, d = tok.shape
    full = lambda a: pl.BlockSpec(a.shape, lambda b, t: (0,) * a.ndim)
    args = [tok, mod, norm_g.reshape(1, d), wg, alog, dtb]
    in_specs = [pl.BlockSpec((1, tm, d), lambda b, t: (b, t, 0))] + [full(a) for a in args[1:]]
    aliases = {}
    if prev_gates is not None:
        in_specs.append(pl.BlockSpec(memory_space=pl.ANY))
        args.append(prev_gates)
        aliases = {6: 1}
    return pl.pallas_call(
        functools.partial(_hnorm_kernel, d=d, mod_row=mod_row),
        grid=(bsz, rows // tm),
        in_specs=in_specs,
        out_specs=[pl.BlockSpec((1, tm, d), lambda b, t: (b, t, 0)),
                   pl.BlockSpec((1, tm, LANES), lambda b, t: (b, t + gate_tile0, 0))],
        out_shape=[jax.ShapeDtypeStruct((bsz, rows, d), BF16), jax.ShapeDtypeStruct((bsz, gate_rows, LANES), F32)],
        input_output_aliases=aliases,
        compiler_params=_cparams(("arbitrary", "arbitrary")),
        name=name,
    )(*args)


@functools.lru_cache(maxsize=None)
def _shift_matrix(rowlen, taps, blk):
    t = np.arange(blk)
    mats = []
    for j in range(taps):
        d = j - taps // 2
        if d != 0:
            ok = ((t + d) >= 0) & ((t + d) < blk) & (t // rowlen == (t + d) // rowlen)
            s = np.zeros((blk, blk), np.float32)
            s[t[ok], t[ok] + d] = 1.0
            mats.append(s)
    return np.concatenate(mats, axis=1)


def _shift_block(rowlen, taps):
    return max(rowlen, MXU_TILE // (taps - 1) // rowlen * rowlen)


def _bf16_weights(w_ref, wb_ref, first):
    @pl.when(first)
    def _():
        wb_ref[...] = w_ref[...].astype(BF16)


def _convproj_kernel(h_ref, w32_ref, cw_ref, sh_ref, *rest, tm):
    o_ref, w_ref = rest[-2:]
    j = pl.program_id(0)
    _bf16_weights(w32_ref, w_ref, (pl.program_id(1) == 0) & (pl.program_id(2) == 0))
    taps = cw_ref.shape[1]
    mid = taps // 2
    blk = sh_ref.shape[0]
    mblk = max(blk, MXU_TILE)
    na = jnp.where(j == 0, float(HEAD_DIM), jnp.where(j == 1, 1.0, 0.0))
    nb = jnp.where(j == 0, EPS * HEAD_DIM, jnp.where(j == 1, EPS, 1.0))
    cw16 = cw_ref[0].astype(BF16)
    for r in range(tm // mblk):
        zm = jnp.dot(h_ref[0, r * mblk:(r + 1) * mblk, :], w_ref[...], preferred_element_type=F32)
        for g in range(mblk // blk):
            rows = slice(r * mblk + g * blk, r * mblk + (g + 1) * blk)
            zb = zm[g * blk:(g + 1) * blk]
            zb16 = zb.astype(BF16)
            side = jnp.concatenate([zb16 * cw16[jt:jt + 1, :] for jt in range(taps) if jt != mid], axis=0)
            y = zb * cw_ref[0, mid:mid + 1, :] + jnp.dot(sh_ref[...], side, preferred_element_type=F32)
            y = _silu(y)
            for hh in range(HEADS):
                yh = y[:, hh * HEAD_DIM:(hh + 1) * HEAD_DIM]
                nrm = jnp.sum(yh * yh, axis=-1, keepdims=True)
                o_ref[0, 0, rows, hh * HEAD_DIM:(hh + 1) * HEAD_DIM] = (
                    yh * lax.rsqrt(nrm * na + nb)).astype(o_ref.dtype)


def _convproj_call(h, w, cw3, *, rowlen, tm, tile0, out_rows, prev=None, name):
    bsz, rows, d = h.shape
    ntiles = rows // tm
    nw = cw3.shape[2]
    taps = cw3.shape[1]
    blk = _shift_block(rowlen, taps)
    assert tm % max(blk, MXU_TILE) == 0 and max(blk, MXU_TILE) % blk == 0
    sh = jnp.asarray(_shift_matrix(rowlen, taps, blk)).astype(BF16)
    in_specs = [pl.BlockSpec((1, tm, d), lambda j, b, t: (b, t, 0)),
                pl.BlockSpec((d, nw), lambda j, b, t: (0, j)),
                pl.BlockSpec((1, cw3.shape[1], nw), lambda j, b, t: (j, 0, 0)),
                pl.BlockSpec(sh.shape, lambda j, b, t: (0, 0))]
    args = [h, w, cw3, sh]
    aliases = {}
    if prev is not None:
        in_specs.append(pl.BlockSpec(memory_space=pl.ANY))
        args.append(prev)
        aliases = {4: 0}
    return pl.pallas_call(
        functools.partial(_convproj_kernel, tm=tm),
        grid=(3, bsz, ntiles),
        in_specs=in_specs,
        out_specs=pl.BlockSpec((1, 1, tm, nw), lambda j, b, t: (j, b, t + tile0, 0)),
        out_shape=jax.ShapeDtypeStruct((3, bsz, out_rows, nw), BF16),
        scratch_shapes=[pltpu.VMEM((d, nw), BF16)],
        input_output_aliases=aliases,
        compiler_params=_cparams(("arbitrary", "arbitrary", "arbitrary")),
        name=name,
    )(*args)


def _hyproj_kernel(h_ref, w32_ref, cw_ref, cb_ref, o_ref, w_ref):
    _bf16_weights(w32_ref, w_ref, (pl.program_id(1) == 0) & (pl.program_id(2) == 0) & (pl.program_id(3) == 0))
    n2s, n1s, d = h_ref.shape[1:]
    taps = cw_ref.shape[1]
    hrows = h_ref[0].reshape(n2s * n1s, d)
    step = min(MXU_TILE, n2s * n1s)
    z = jnp.concatenate([jnp.dot(hrows[r:r + step], w_ref[...], preferred_element_type=F32)
                         for r in range(0, n2s * n1s, step)], axis=0)
    y = cb_ref[0] + z * cw_ref[0, taps // 2:taps // 2 + 1, :]
    for jt in range(taps):
        s = (jt - taps // 2) * n1s
        if s > 0:
            y = y + jnp.concatenate([z[s:], jnp.zeros((s, z.shape[1]), F32)], axis=0) * cw_ref[0, jt:jt + 1, :]
        elif s < 0:
            y = y + jnp.concatenate([jnp.zeros((-s, z.shape[1]), F32), z[:s]], axis=0) * cw_ref[0, jt:jt + 1, :]
    o_ref[0, 0] = y.astype(o_ref.dtype).reshape(n2s, n1s, y.shape[1])


def _hyproj_call(h_perm, w, cw3, cb3):
    bsz, n_lat, d = h_perm.shape
    nw = cw3.shape[2]
    h1 = n_lat // FFT_INNER
    n1s = min(HYPROJ_N1, h1)
    hv = h_perm.reshape(bsz, FFT_INNER, h1, d)
    return pl.pallas_call(
        _hyproj_kernel,
        grid=(3, bsz, FFT_INNER // GRID_W, h1 // n1s),
        in_specs=[pl.BlockSpec((1, GRID_W, n1s, d), lambda j, b, r, t: (b, r, t, 0)),
                  pl.BlockSpec((d, nw), lambda j, b, r, t: (0, j)),
                  pl.BlockSpec((1, cw3.shape[1], nw), lambda j, b, r, t: (j, 0, 0)),
                  pl.BlockSpec((1, 1, nw), lambda j, b, r, t: (j, 0, 0))],
        out_specs=pl.BlockSpec((1, 1, GRID_W, n1s, nw), lambda j, b, r, t: (j, b, r, t, 0)),
        out_shape=jax.ShapeDtypeStruct((3, bsz, FFT_INNER, h1, nw), BF16),
        scratch_shapes=[pltpu.VMEM((d, nw), BF16)],
        compiler_params=_cparams(("arbitrary",) * 4),
        name="hyproj",
    )(hv, w, cw3, cb3)


def _gproj_kernel(h_ref, w32_ref, o_ref, w_ref):
    j = pl.program_id(0)
    _bf16_weights(w32_ref, w_ref, (pl.program_id(1) == 0) & (pl.program_id(2) == 0))
    for r in range(h_ref.shape[1] // MXU_TILE):
        rows = slice(r * MXU_TILE, (r + 1) * MXU_TILE)
        z = jnp.dot(h_ref[0, rows, :], w_ref[...], preferred_element_type=F32)
        s = jax.nn.sigmoid(z)
        o_ref[0, 0, rows, :] = jnp.where(j < 2, z * s, s).astype(BF16)


def _gproj_call(h, w, n_lat):
    bsz, _, d = h.shape
    nw = w.shape[1] // 4
    tm = PROJ_TILE if n_lat % PROJ_TILE == 0 else 256
    return pl.pallas_call(
        _gproj_kernel,
        grid=(4, bsz, n_lat // tm),
        in_specs=[pl.BlockSpec((1, tm, d), lambda j, b, t: (b, t, 0)),
                  pl.BlockSpec((d, nw), lambda j, b, t: (0, j))],
        out_specs=pl.BlockSpec((1, 1, tm, nw), lambda j, b, t: (j, b, t, 0)),
        out_shape=jax.ShapeDtypeStruct((4, bsz, n_lat, nw), BF16),
        scratch_shapes=[pltpu.VMEM((d, nw), BF16)],
        compiler_params=_cparams(("arbitrary", "arbitrary", "arbitrary")),
        name="gproj",
    )(h, w)


def _unit_tri_inverse(ls, lowers, filler):
    c = ls[0].shape[0]
    ri = lax.broadcasted_iota(jnp.int32, (c, c), 0)
    ci = lax.broadcasted_iota(jnp.int32, (c, c), 1)
    same = lambda n: (ri // n) == (ci // n)
    eye = jnp.where(ri == ci, 1.0, 0.0)
    in2 = same(2)
    ts = [eye - jnp.where(in2, l, 0.0) for l in ls]
    n = 2
    nlevels = c.bit_length() - 2
    level = 0
    while n < c:
        blk = same(2 * n) & jnp.logical_not(same(n))
        offs = [jnp.where(blk, l, 0.0).astype(BF16) for l in ls]
        if n % SUBLANES:
            prods = [_bdot(t, off) for t, off in zip(ts, offs)]
            filler(level, nlevels)
            ts = [t - _bdot(p, t) for p, t in zip(prods, ts)]
        else:
            def moving(t, lower):
                first = n if lower else 0
                return jnp.concatenate([t[r:r + n] for r in range(first, c, 2 * n)], axis=0)

            def merged(t, upd, lower):
                pieces, k = [], 0
                for r in range(0, c, n):
                    if ((r // n) % 2 == 1) == lower:
                        pieces.append(upd[k * n:(k + 1) * n])
                        k += 1
                    else:
                        pieces.append(t[r:r + n])
                return jnp.concatenate(pieces, axis=0)

            rows = [moving(t, lo) for t, lo in zip(ts, lowers)]
            prods = [_bdot(r, off) for r, off in zip(rows, offs)]
            filler(level, nlevels)
            upds = [r - _bdot(p, t) for r, p, t in zip(rows, prods, ts)]
            ts = [merged(t, u, lo) for t, u, lo in zip(ts, upds, lowers)]
        n *= 2
        level += 1
    return ts


_NT = (((1,), (1,)), ((), ()))
_TN = (((0,), (0,)), ((), ()))


def _prep_kernel(qkv_ref, g_ref, gt_ref, tall_ref, wide_ref, el_ref, *, bsz):
    c = CHUNK
    ri = lax.broadcasted_iota(jnp.int32, (c, c), 0)
    ci = lax.broadcasted_iota(jnp.int32, (c, c), 1)
    tri_l = jnp.where(ri >= ci, 1.0, 0.0)
    tri_u = jnp.where(ri <= ci, 1.0, 0.0)
    incl = (ri >= ci, ri <= ci)
    strict = (ri > ci, ri < ci)
    hdot = functools.partial(jnp.dot, precision=HIGHEST, preferred_element_type=F32)

    per = tall_ref.shape[0]
    cb = [(ch, b) for ch in range(per) for b in range(bsz)]
    rows = lambda ch: slice(ch * c, (ch + 1) * c)
    gs = [g_ref[b, rows(ch), :] for ch, b in cb]
    gts = [gt_ref[b, ch] for ch, b in cb]
    gcol = [(hdot(tri_l, g), hdot(tri_u, g)) for g in gs]
    grow = [(hdot(gt, tri_u), hdot(gt, tri_l)) for gt in gts]
    glast = [(gc[0][c - 1:c], gc[1][0:1]) for gc in gcol]
    el_ref[...] = jnp.zeros_like(el_ref)
    for p, (ch, b) in enumerate(cb):
        for d in range(2):
            el_ref[ch, d * bsz + b:d * bsz + b + 1, :] = jnp.exp(glast[p][d])

    pbh = [(p, hh) for p in range(len(cb)) for hh in range(HEADS)]
    hs = lambda hh: slice(hh * HEAD_DIM, (hh + 1) * HEAD_DIM)
    tok = lambda w, p, hh: qkv_ref[w, cb[p][1], rows(cb[p][0]), hs(hh)]
    ks = [tok(1, p, hh) for p, hh in pbh]
    qs = [tok(0, p, hh) for p, hh in pbh]
    kk = [lax.dot_general(k, k, _NT, preferred_element_type=F32) for k in ks]
    qk = [lax.dot_general(q, k, _NT, preferred_element_type=F32) for q, k in zip(qs, ks)]
    inst = [(d, p, hh, i) for d in range(2) for i, (p, hh) in enumerate(pbh)]
    decs, betas, gcs = [], [], []
    for d, p, hh, _ in inst:
        lg = 2 * HEADS + d * HEADS + hh
        gcs.append(gcol[p][d][:, lg:lg + 1])
        betas.append(jnp.broadcast_to(gs[p][:, d * HEADS + hh:d * HEADS + hh + 1], (c, HEAD_DIM)))
        decs.append(jnp.exp(jnp.where(incl[d], gcs[-1] - grow[p][d][lg:lg + 1, :], -jnp.inf)))
    ls = [jnp.where(strict[d], kk[i] * beta[:, 0:c] * dec, 0.0)
          for (d, _, _, i), beta, dec in zip(inst, betas, decs)]

    def scaled_operands(part, nparts):
        for (d, p, hh, i), dec, beta, gc in list(zip(inst, decs, betas, gcs))[part::nparts]:
            ch, b = cb[p]
            lg = 2 * HEADS + d * HEADS + hh
            eg = jnp.broadcast_to(jnp.exp(gc), (c, HEAD_DIM))
            k = ks[i].astype(F32)
            tall_ref[ch, d, b, hh, :, c:2 * c] = (qk[i] * dec).astype(BF16)
            wide_ref[ch, d, b, hh, 0:c, :] = (qs[i].astype(F32) * eg).astype(BF16)
            wide_ref[ch, d, b, hh, c:2 * c, :] = (k * beta * eg).astype(BF16)
            wide_ref[ch, d, b, hh, 2 * c:3 * c, :] = (tok(2, p, hh).astype(F32) * beta).astype(BF16)
            wide_ref[ch, d, b, hh, 3 * c:4 * c, :] = (k * jnp.exp(glast[p][d][:, lg:lg + 1] - gc)).astype(BF16)

    ts = _unit_tri_inverse(ls, [d == 0 for d, _, _, _ in inst], scaled_operands)
    for (d, p, hh, _), t in zip(inst, ts):
        tall_ref[cb[p][0], d, cb[p][1], hh, :, 0:c] = t.astype(BF16)


def _prep_call(qkv, gates, gates_t):
    _, bsz, lt, dn = qkv.shape
    nc = lt // CHUNK
    per = PREP_CHUNKS
    assert nc % per == 0
    tall = (2, bsz, HEADS, CHUNK, 2 * CHUNK)
    wide = (2, bsz, HEADS, 4 * CHUNK, HEAD_DIM)
    return pl.pallas_call(
        functools.partial(_prep_kernel, bsz=bsz),
        grid=(nc // per,),
        in_specs=[pl.BlockSpec((3, bsz, per * CHUNK, dn), lambda i: (0, 0, i, 0)),
                  pl.BlockSpec((bsz, per * CHUNK, LANES), lambda i: (0, i, 0)),
                  pl.BlockSpec((bsz, per, 4 * HEADS, CHUNK), lambda i: (0, i, 0, 0))],
        out_specs=[pl.BlockSpec((per,) + tall, lambda i: (i, 0, 0, 0, 0, 0)),
                   pl.BlockSpec((per,) + wide, lambda i: (i, 0, 0, 0, 0, 0)),
                   pl.BlockSpec((per, SUBLANES, LANES), lambda i: (i, 0, 0))],
        out_shape=[jax.ShapeDtypeStruct((nc,) + tall, BF16),
                   jax.ShapeDtypeStruct((nc,) + wide, BF16),
                   jax.ShapeDtypeStruct((nc, SUBLANES, LANES), F32)],
        compiler_params=_cparams(("arbitrary",)),
        name="prep",
    )(qkv, gates, gates_t)


def _scan_kernel(tf_ref, tb_ref, wf_ref, wb_ref, ef_ref, eb_ref, of_ref, ob_ref, s_ref, *, bsz):
    i = pl.program_id(0)

    @pl.when(i == 0)
    def _():
        s_ref[...] = jnp.zeros_like(s_ref)

    c = CHUNK
    per = tf_ref.shape[0]
    dirs = ((tf_ref, wf_ref, ef_ref, of_ref), (tb_ref, wb_ref, eb_ref, ob_ref))
    inst = [(d, b, hh) for d in range(2) for b in range(bsz) for hh in range(HEADS)]
    state = [s_ref[d, b, hh] for d, b, hh in inst]
    for sub in range(per):
        at = (sub, per - 1 - sub)
        a1 = [jnp.dot(dirs[d][1][at[d], b, hh, 0:2 * c, :], s.astype(BF16), preferred_element_type=F32)
              for (d, b, hh), s in zip(inst, state)]
        v_new = [jnp.dot(dirs[d][0][at[d], b, hh, :, 0:c],
                         (dirs[d][1][at[d], b, hh, 2 * c:3 * c, :].astype(F32) - a[c:]).astype(BF16),
                         preferred_element_type=F32).astype(BF16) for (d, b, hh), a in zip(inst, a1)]
        new_state = []
        for (d, b, hh), a, vn, s in zip(inst, a1, v_new, state):
            lg = 2 * HEADS + d * HEADS + hh
            o = a[:c] + jnp.dot(dirs[d][0][at[d], b, hh, :, c:2 * c], vn, preferred_element_type=F32)
            dirs[d][3][b, at[d] * c:(at[d] + 1) * c, hh * HEAD_DIM:(hh + 1) * HEAD_DIM] = o.astype(BF16)
            el = dirs[d][2][at[d], d * bsz + b:d * bsz + b + 1, lg:lg + 1]
            new_state.append(s * el + lax.dot_general(dirs[d][1][at[d], b, hh, 3 * c:4 * c, :], vn, _TN,
                                                      preferred_element_type=F32))
        state = new_state
    for (d, b, hh), s in zip(inst, state):
        s_ref[d, b, hh] = s


def _scan_call(tall, wide, el, n_lat, n_ctx):
    nc, _, bsz = tall.shape[:3]
    dn = HEADS * HEAD_DIM
    per = SCAN_CHUNKS
    ncx, ncc, ns = n_lat // CHUNK // per, n_ctx // CHUNK // per, nc // per
    assert nc == (ncx + ncc) * per
    cf = lambda i: jnp.where(i < ncc, ncx + i, i - ncc)
    cb = lambda i: ns - 1 - i

    dspec = lambda arr, fn, d: pl.BlockSpec((per, None) + arr.shape[2:], lambda i: (fn(i), d, 0, 0, 0, 0))
    espec = lambda fn: pl.BlockSpec((per, SUBLANES, LANES), lambda i: (fn(i), 0, 0))
    ospec = lambda fn: pl.BlockSpec((bsz, per * CHUNK, dn), lambda i: (0, fn(i), 0))
    return pl.pallas_call(
        functools.partial(_scan_kernel, bsz=bsz),
        grid=(ns,),
        in_specs=[dspec(tall, cf, 0), dspec(tall, cb, 1), dspec(wide, cf, 0), dspec(wide, cb, 1),
                  espec(cf), espec(cb)],
        out_specs=[ospec(lambda i: jnp.maximum(i - ncc, 0)), ospec(lambda i: jnp.minimum(ns - 1 - i, ncx - 1))],
        out_shape=[jax.ShapeDtypeStruct((bsz, n_lat, dn), BF16)] * 2,
        scratch_shapes=[pltpu.VMEM((2, bsz, HEADS, HEAD_DIM, HEAD_DIM), F32)],
        compiler_params=_cparams(("arbitrary",)),
        name="scan",
    )(tall, tall, wide, wide, el, el)


def _hidden_kernel(pe_ref, w1_ref, b1_ref, w2_ref, b2_ref, w3_ref, b3_ref, f_ref, o_ref):
    f = f_ref[...]
    dot = functools.partial(jnp.dot, precision=HIGHEST, preferred_element_type=F32)
    a = jnp.sin(f * (dot(pe_ref[...], w1_ref[...]) + b1_ref[...]))
    a = jnp.sin(f * (dot(a, w2_ref[...]) + b2_ref[...]))
    o_ref[...] = jnp.sin(f * (dot(a, w3_ref[...]) + b3_ref[...]))


def _hidden_call(pe, w1p, b1, w2, b2, w3, b3, freq):
    rows, pw = pe.shape
    fw = 2 * w2.shape[0]
    tr = 512
    full = lambda a: pl.BlockSpec(a.shape, lambda t: (0,) * a.ndim)
    two = lambda a: jnp.tile(a.reshape(1, -1), (1, 2))
    diag2 = lambda a: jnp.kron(jnp.eye(2, dtype=a.dtype), a)
    args = (diag2(w1p), two(b1), diag2(w2), two(b2), diag2(w3), two(b3), two(freq))
    return pl.pallas_call(
        _hidden_kernel,
        grid=(rows // tr,),
        in_specs=[pl.BlockSpec((tr, pw), lambda t: (t, 0))] + [full(a) for a in args],
        out_specs=pl.BlockSpec((tr, fw), lambda t: (t, 0)),
        out_shape=jax.ShapeDtypeStruct((rows, fw), F32),
        compiler_params=_cparams(("arbitrary",)),
        name="hidden",
    )(pe, *args)


def _ftime_kernel(a_ref, t_ref, w_ref, dl_ref, h_ref, ss_ref, *, n_lat, tr):
    t = pl.program_id(0)
    h = _bdot(a_ref[...], w_ref[...]) * jnp.exp(-t_ref[...] * dl_ref[...])
    row = lax.broadcasted_iota(jnp.int32, (tr, 1), 0) + t * tr
    h = jnp.where(row == n_lat, 0.0, h)
    h_ref[...] = h.astype(h_ref.dtype)

    @pl.when(t == 0)
    def _():
        ss_ref[...] = jnp.zeros_like(ss_ref)

    ss_ref[...] += jnp.sum(h * h, axis=0, keepdims=True)


def _ftime_call(a3, tcol, wout2, deltas2, n_lat):
    rows = tcol.shape[0]
    fw = a3.shape[1]
    ncol = wout2.shape[2]
    tr = 512
    side = lambda t: (t * tr) // n_lat
    return pl.pallas_call(
        functools.partial(_ftime_kernel, n_lat=n_lat, tr=tr),
        grid=(rows // tr,),
        in_specs=[pl.BlockSpec((tr, fw), lambda t: (t % (n_lat // tr), 0)),
                  pl.BlockSpec((tr, 1), lambda t: (t, 0)),
                  pl.BlockSpec((None, fw, ncol), lambda t: (side(t), 0, 0)),
                  pl.BlockSpec((1, ncol), lambda t: (0, 0))],
        out_specs=[pl.BlockSpec((tr, ncol), lambda t: (t, 0)),
                   pl.BlockSpec((1, ncol), lambda t: (0, 0))],
        out_shape=[jax.ShapeDtypeStruct((rows, ncol), BF16), jax.ShapeDtypeStruct((1, ncol), F32)],
        compiler_params=_cparams(("arbitrary",)),
        name="ftime",
    )(a3, tcol, wout2, deltas2)


@functools.lru_cache(maxsize=None)
def _fft_tables(n_lat):
    n = 2 * n_lat
    n1t = n // FFT_INNER
    h1 = n1t // 2
    n2 = np.arange(FFT_INNER, dtype=np.int64)[:, None, None]
    k1 = np.arange(n1t, dtype=np.int64)[None, :, None]
    n1 = np.arange(h1, dtype=np.int64)[None, None, :]
    ang = (-2.0 * np.pi / n) * ((k1 * (FFT_INNER * n1 + n2)) % n)
    er, ei = np.cos(ang), np.sin(ang)
    e1 = np.concatenate([np.concatenate([er, -ei], axis=2), np.concatenate([ei, er], axis=2)], axis=1)
    e4 = np.transpose(e1, (0, 2, 1))
    sgn = np.where(np.arange(n1t) % 2 == 0, 1.0, -1.0)[None, :, None]
    ef = np.concatenate([np.concatenate([er, sgn * er], axis=2), np.concatenate([ei, sgn * ei], axis=2)], axis=1)
    a = np.arange(FFT_INNER, dtype=np.int64)
    ang2 = (-2.0 * np.pi / FFT_INNER) * ((a[:, None] * a[None, :]) % FFT_INNER)
    gr, gi = np.cos(ang2), np.sin(ang2)
    g2 = np.block([[gr, -gi], [gi, gr]])
    f32 = lambda x: np.ascontiguousarray(x, dtype=np.float32)
    return f32(e1), f32(e4), f32(ef), f32(g2), f32(g2.T)


def _hold(p, j, first, last, nsteps):
    active = (p >= first) & (p <= last) & (lax.rem(p - first, 2) == 0)
    return jnp.where(active, j, jnp.where(p < first, 0, nsteps - 1))


def _slab_read(x_ref, rows):
    return jnp.concatenate([x_ref[c, rows, :] for c in range(x_ref.shape[0])], axis=1)


def _slab_write(x_ref, rows, val):
    for c in range(x_ref.shape[0]):
        x_ref[c, rows, :] = val[:, c * LANES:(c + 1) * LANES]


def _spectrum_steps(x_ref, g2_ref, k1s, n1t, stride):
    cols = [jnp.concatenate([_slab_read(x_ref, pl.ds(k1, FFT_INNER, stride=stride)),
                             _slab_read(x_ref, pl.ds(n1t + k1, FFT_INNER, stride=stride))], axis=0).astype(BF16)
            for k1 in k1s]
    return [jnp.dot(g2_ref[...], c, preferred_element_type=F32) for c in cols]


def _ffft_kernel(h_ref, ef_ref, g2_ref, ss_ref, o_ref, x_ref, *, n1t, grp, kgrp, stride, inv_n):
    p = pl.program_id(1)
    j = pl.program_id(2)

    @pl.when(p == 0)
    def _():
        for j0 in range(0, grp, FFT_INTERLEAVE):
            jjs = list(range(j0, min(j0 + FFT_INTERLEAVE, grp)))
            us = [jnp.concatenate([h_ref[0, jj], h_ref[1, jj]], axis=0) for jj in jjs]
            slabs = [_bdot(ef_ref[jj], u) for jj, u in zip(jjs, us)]
            for jj, slab in zip(jjs, slabs):
                base = pl.multiple_of((j * grp + jj) * stride, SUBLANES)
                _slab_write(x_ref, pl.ds(base, 2 * n1t), slab)

    @pl.when(p == 1)
    def _():
        scale = lax.rsqrt(ss_ref[...] + EPS) * inv_n
        for j0 in range(0, kgrp, FFT_INTERLEAVE):
            jjs = list(range(j0, min(j0 + FFT_INTERLEAVE, kgrp)))
            xfs = _spectrum_steps(x_ref, g2_ref, [j * kgrp + jj for jj in jjs], n1t, stride)
            for jj, xf in zip(jjs, xfs):
                o_ref[jj] = (xf * scale).astype(BF16)


def _fft_groups(n1t, ns):
    ns = min(ns, n1t)
    assert FFT_INNER % ns == 0 and n1t % ns == 0
    return ns, FFT_INNER // ns, n1t // ns


def _ffft_call(hfull, ssq, ef, g2, n_lat):
    ncol = hfull.shape[1]
    n1t = 2 * n_lat // FFT_INNER
    h1 = n1t // 2
    ct = FFT_CHANNELS
    ns, grp, kgrp = _fft_groups(n1t, FILTER_FFT_STEPS)
    stride = 2 * n1t + SLAB_PAD
    hv = hfull.reshape(2, FFT_INNER, h1, ncol)
    nct = ncol // ct
    cpo = nct // 2
    return pl.pallas_call(
        functools.partial(_ffft_kernel, n1t=n1t, grp=grp, kgrp=kgrp, stride=stride, inv_n=1.0 / (2 * n_lat)),
        grid=(nct, 2, ns),
        in_specs=[pl.BlockSpec((2, grp, h1, ct), lambda c, p, j: (0, _hold(p, j, 0, 0, ns), 0, c)),
                  pl.BlockSpec((grp, 2 * n1t, n1t), lambda c, p, j: (_hold(p, j, 0, 0, ns), 0, 0)),
                  pl.BlockSpec((2 * FFT_INNER, 2 * FFT_INNER), lambda c, p, j: (0, 0)),
                  pl.BlockSpec((1, ct), lambda c, p, j: (0, c))],
        out_specs=pl.BlockSpec((None, kgrp, 2 * FFT_INNER, ct),
                               lambda c, p, j: (c // cpo, _hold(p, j, 1, 1, ns), 0, c % cpo)),
        out_shape=jax.ShapeDtypeStruct((2, n1t, 2 * FFT_INNER, ncol // 2), BF16),
        scratch_shapes=[pltpu.VMEM((ct // LANES, FFT_INNER * stride, LANES), F32)],
        compiler_params=_cparams(("arbitrary", "arbitrary", "arbitrary")),
        name="ffft",
    )(hv, ef, g2, ssq)


def _hyena_kernel(v_ref, x1_ref, x2_ref, hf_ref, e1_ref, e4_ref, g2_ref, g3_ref, bias_ref,
                  o_ref, x_ref, z_ref, *, n1t, grp, kgrp, stride):
    p = pl.program_id(1)
    j = pl.program_id(2)
    h1 = n1t // 2

    def gather(ref, jj):
        return jnp.concatenate([ref[0, jj], ref[1, jj]], axis=0).astype(F32)

    def slab_base(jj):
        return pl.multiple_of((j * grp + jj) * stride, SUBLANES)

    def time_rows(jjs):
        slabs = [_slab_read(x_ref, pl.ds(slab_base(jj), 2 * n1t)).astype(BF16) for jj in jjs]
        return [jnp.dot(e4_ref[jj], s, preferred_element_type=F32) for jj, s in zip(jjs, slabs)]

    def write_slabs(jjs, us):
        slabs = [jnp.dot(e1_ref[jj], u, preferred_element_type=F32) for jj, u in zip(jjs, us)]
        for jj, slab in zip(jjs, slabs):
            _slab_write(x_ref, pl.ds(slab_base(jj), 2 * n1t), slab)

    def chunks(n):
        return [list(range(j0, min(j0 + FFT_INTERLEAVE, n))) for j0 in range(0, n, FFT_INTERLEAVE)]

    @pl.when(p == 0)
    def _():
        for jjs in chunks(grp):
            write_slabs(jjs, [gather(v_ref, jj).astype(BF16) for jj in jjs])

    @pl.when((p == 1) | (p == 3))
    def _():
        for jjs in chunks(kgrp):
            k1s = [j * kgrp + jj for jj in jjs]
            xfs = _spectrum_steps(x_ref, g2_ref, k1s, n1t, stride)
            ys = []
            for jj, xf in zip(jjs, xfs):
                hf = hf_ref[jj].astype(F32)
                xr, xi = xf[:FFT_INNER], xf[FFT_INNER:]
                hr, hi = hf[:FFT_INNER], hf[FFT_INNER:]
                ys.append(jnp.concatenate([xr * hr - xi * hi, xr * hi + xi * hr], axis=0).astype(BF16))
            bbs = [jnp.dot(g3_ref[...], y, preferred_element_type=F32) for y in ys]
            for k1, bb in zip(k1s, bbs):
                _slab_write(x_ref, pl.ds(k1, FFT_INNER, stride=stride), bb[:FFT_INNER])
                _slab_write(x_ref, pl.ds(n1t + k1, FFT_INNER, stride=stride), bb[FFT_INNER:])

    @pl.when(p == 2)
    def _():
        for jjs in chunks(grp):
            z1s = [(gather(x1_ref, jj) * (y + gather(v_ref, jj) * bias_ref[0:1, :])).astype(BF16)
                   for jj, y in zip(jjs, time_rows(jjs))]
            for jj, z1 in zip(jjs, z1s):
                z_ref[j * grp + jj] = z1
            write_slabs(jjs, z1s)

    @pl.when(p == 4)
    def _():
        for jjs in chunks(grp):
            for jj, y in zip(jjs, time_rows(jjs)):
                out = gather(x2_ref, jj) * (y + z_ref[j * grp + jj].astype(F32) * bias_ref[1:2, :])
                o_ref[0, jj] = out[:h1].astype(o_ref.dtype)
                o_ref[1, jj] = out[h1:].astype(o_ref.dtype)


def _hyena_call(hv, hf, e1, e4, g2, g3, bias):
    _, bsz, _, h1, ch = hv.shape
    assert bsz == 2, "the two batch rows are packed as the real and imaginary parts of one FFT"
    n1t = 2 * h1
    ct = FFT_CHANNELS
    ns, grp, kgrp = _fft_groups(n1t, CONV_FFT_STEPS)
    stride = 2 * n1t + SLAB_PAD

    def hspec(which, first, last):
        return pl.BlockSpec((None, bsz, grp, h1, ct),
                            lambda c, p, j: (which, 0, _hold(p, j, first, last, ns), 0, c))

    def hf_map(c, p, j):
        order = jnp.where(p >= 2, 1, 0)
        step = jnp.where((p == 1) | (p == 3), j, jnp.where((p == 0) | (p == 2), 0, ns - 1))
        return (order, step, 0, c)

    return pl.pallas_call(
        functools.partial(_hyena_kernel, n1t=n1t, grp=grp, kgrp=kgrp, stride=stride),
        grid=(ch // ct, 5, ns),
        in_specs=[hspec(0, 0, 2), hspec(1, 2, 2), hspec(2, 4, 4),
                  pl.BlockSpec((None, kgrp, 2 * FFT_INNER, ct), hf_map),
                  pl.BlockSpec((grp, 2 * n1t, n1t), lambda c, p, j: (_hold(p, j, 0, 2, ns), 0, 0)),
                  pl.BlockSpec((grp, n1t, 2 * n1t), lambda c, p, j: (_hold(p, j, 2, 4, ns), 0, 0)),
                  pl.BlockSpec((2 * FFT_INNER, 2 * FFT_INNER), lambda c, p, j: (0, 0)),
                  pl.BlockSpec((2 * FFT_INNER, 2 * FFT_INNER), lambda c, p, j: (0, 0)),
                  pl.BlockSpec((2, ct), lambda c, p, j: (0, c))],
        out_specs=pl.BlockSpec((bsz, grp, h1, ct), lambda c, p, j: (0, _hold(p, j, 4, 4, ns), 0, c)),
        out_shape=jax.ShapeDtypeStruct((bsz, FFT_INNER, h1, ch), BF16),
        scratch_shapes=[pltpu.VMEM((ct // LANES, FFT_INNER * stride, LANES), F32),
                        pltpu.VMEM((FFT_INNER, n1t, ct), BF16)],
        compiler_params=_cparams(("arbitrary", "arbitrary", "arbitrary"), VMEM_LIMIT_CONV),
        name="hyena",
    )(hv, hv, hv, hf, e1, e4, g2, g3, bias)


def _merge_kernel(of_ref, ob_ref, gp_ref, hy_ref, x_ref, mod_ref, dng_ref, wpa_ref, wpb_ref, wo_ref,
                  fg_ref, o_ref, *, d):
    b = pl.program_id(0)
    blocks = [slice(r * MXU_TILE, (r + 1) * MXU_TILE) for r in range(o_ref.shape[1] // MXU_TILE)]
    f32 = lambda ref, *idx: ref[idx].astype(F32)

    def head_norm(rows):
        o = f32(of_ref, 0, rows) + f32(ob_ref, 0, rows)
        za = f32(gp_ref, 0, 0, rows)
        parts = []
        for hh in range(HEADS):
            hs = slice(hh * HEAD_DIM, (hh + 1) * HEAD_DIM)
            oh = o[:, hs]
            ms = jnp.mean(oh * oh, axis=-1, keepdims=True)
            parts.append((oh * lax.rsqrt(ms + EPS) * dng_ref[...] * za[:, hs]).astype(BF16))
        return jnp.concatenate(parts, axis=1)

    o_a = [head_norm(rows) for rows in blocks]
    y_a = [jnp.dot(a, wpa_ref[...], preferred_element_type=F32) for a in o_a]
    o_b = [(f32(hy_ref, 0, rows) * f32(gp_ref, 1, 0, rows)).astype(BF16) for rows in blocks]
    y_b = [jnp.dot(a, wpb_ref[...], preferred_element_type=F32) for a in o_b]
    m = [(f32(gp_ref, 2, 0, rows) * ya + f32(gp_ref, 3, 0, rows) * yb).astype(BF16)
         for rows, ya, yb in zip(blocks, y_a, y_b)]
    y = [jnp.dot(a, wo_ref[...], preferred_element_type=F32) for a in m]
    gate = mod_ref[pl.ds(b, 1), :][:, 2 * d:3 * d]
    for rows, yr in zip(blocks, y):
        xn = x_ref[0, rows, :] + gate * yr
        o_ref[0, rows, :] = xn * lax.rsqrt(jnp.mean(xn * xn, axis=-1, keepdims=True) + EPS) * fg_ref[...]


def _merge_call(o_f, o_b, gp, hy, x, mod, dn_norm_g, w_pa, w_pb, w_out, final_g):
    bsz, n_lat, d = x.shape
    tm = NORM_TILE if n_lat % NORM_TILE == 0 else 256
    wfull = lambda a: pl.BlockSpec(a.shape, lambda b, t: (0,) * a.ndim)
    tok = pl.BlockSpec((1, tm, d), lambda b, t: (b, t, 0))
    return pl.pallas_call(
        functools.partial(_merge_kernel, d=d),
        grid=(bsz, n_lat // tm),
        in_specs=[tok, tok,
                  pl.BlockSpec((4, 1, tm, d), lambda b, t: (0, b, t, 0)),
                  tok, tok, wfull(mod),
                  pl.BlockSpec((1, HEAD_DIM), lambda b, t: (0, 0)),
                  wfull(w_pa), wfull(w_pb), wfull(w_out),
                  pl.BlockSpec((1, d), lambda b, t: (0, 0))],
        out_specs=tok,
        out_shape=jax.ShapeDtypeStruct((bsz, n_lat, d), F32),
        compiler_params=_cparams(("arbitrary", "arbitrary")),
        name="merge",
    )(o_f, o_b, gp, hy, x, mod, dn_norm_g.reshape(1, HEAD_DIM), w_pa, w_pb, w_out, final_g.reshape(1, d))


def _position_features(n_lat):
    h1 = n_lat // FFT_INNER
    m = (np.arange(FFT_INNER)[:, None] + FFT_INNER * np.arange(h1)[None, :]).reshape(-1)
    lag = np.concatenate([m, n_lat - m]).astype(np.float64)
    lag[n_lat] = 0.0
    bands = (HY_EMB - 1) // 2
    t = (lag / (n_lat - 1))[:, None]
    wpos = (2.0 * math.pi / n_lat) * lag[:, None]
    fb = np.linspace(1e-4, bands - 1, bands)[None, :]
    pe = np.concatenate([t, np.cos(fb * wpos), -np.sin(fb * wpos)], axis=1)
    pe_pad = np.zeros((2 * n_lat, LANES), np.float32)
    pe_pad[:, :HY_EMB] = pe
    return pe_pad, np.ascontiguousarray(t, dtype=np.float32)


def kernel(x, c, ctx, c_ctx, w_mod, b_mod, norm_g, w_in, dn_conv_w, dn_a_log, dn_dt_bias, dn_norm_g,
           hy_conv_w, hy_conv_b, hy_f_w1, hy_f_b1, hy_f_w2, hy_f_b2, hy_f_w3, hy_f_b3, hy_f_wout,
           hy_f_freq, hy_bias, w_pa, w_pb, w_out, final_g):
    bsz, n_lat, d = x.shape
    n_ctx = ctx.shape[1]
    assert w_mod.shape[0] == 1, "single layer: the context stream is only read through its scan states"
    dn = HEADS * HEAD_DIM
    hy = hy_bias.shape[-1]
    assert n_lat % 256 == 0 and n_ctx % 256 == 0 and dn == d and hy == d

    cvec = jnp.zeros((SUBLANES, d), F32).at[:bsz].set(c).at[bsz].set(c_ctx)
    mod = _mod_call(cvec, w_mod[0], b_mod[0])

    w = w_in[0]
    o_qkv, o_gate, o_hy, o_gp = 0, 4 * dn, 4 * dn + 4 * HEADS, 4 * dn + 4 * HEADS + 3 * hy
    col3 = lambda a, off, n: jnp.transpose(a[:, off:off + 3 * n].reshape(a.shape[0], 3, n), (1, 0, 2))
    lt = n_lat + n_ctx
    tm = PROJ_TILE if n_lat % PROJ_TILE == 0 else 256

    wg = jnp.zeros((d, LANES), BF16).at[:, :4 * HEADS].set(w[:, o_gate:o_gate + 4 * HEADS].astype(BF16))
    lane_pad = lambda a: jnp.zeros((1, LANES), F32).at[0, 2 * HEADS:4 * HEADS].set(a.reshape(-1))
    gate_args = (wg, lane_pad(dn_a_log[0]), lane_pad(dn_dt_bias[0]))
    h_tok, gates = _hnorm_call(x, mod, norm_g[0], *gate_args, tm=NORM_TILE, gate_rows=lt, gate_tile0=0, name="hnorm")
    h_ctx, gates = _hnorm_call(ctx, mod, norm_g[0], *gate_args, tm=n_ctx, gate_rows=lt, gate_tile0=n_lat // n_ctx,
                               mod_row=bsz, prev_gates=gates, name="hnorm_ctx")
    h1 = n_lat // FFT_INNER
    h_perm = jnp.transpose(h_tok.reshape(bsz, h1, FFT_INNER, d), (0, 2, 1, 3)).reshape(bsz, n_lat, d)

    assert o_qkv == 0
    cw_qkv = col3(dn_conv_w[0], 0, dn)
    qkv = _convproj_call(h_tok, w, cw_qkv, rowlen=GRID_W, tm=tm, tile0=0, out_rows=lt, name="qkv")
    qkv = _convproj_call(h_ctx, w, cw_qkv, rowlen=n_ctx, tm=n_ctx, tile0=n_lat // n_ctx, out_rows=lt,
                         prev=qkv, name="qkv_ctx")
    gates_t = jnp.transpose(gates[:, :, :4 * HEADS].reshape(bsz, lt // CHUNK, CHUNK, 4 * HEADS), (0, 1, 3, 2))
    tall, wide, el = _prep_call(qkv, gates, gates_t)
    o_f, o_b = _scan_call(tall, wide, el, n_lat, n_ctx)

    hyp = _hyproj_call(h_perm, w[:, o_hy:o_hy + 3 * hy], col3(hy_conv_w[0], 0, hy), hy_conv_b[0].reshape(3, 1, hy))
    w_gates = jnp.concatenate([w[:, 3 * dn:4 * dn], w[:, o_gp:o_gp + 3 * d]], axis=1)
    gp = _gproj_call(h_tok, w_gates, n_lat)

    pe, tcol = _position_features(n_lat)
    w1p = jnp.zeros((LANES, hy_f_w1.shape[-1]), F32).at[:HY_EMB].set(hy_f_w1[0])
    pe2 = jnp.asarray(np.concatenate([pe[:n_lat], pe[n_lat:]], axis=1))
    a3 = _hidden_call(pe2, w1p, hy_f_b1[0], hy_f_w2[0], hy_f_b2[0], hy_f_w3[0], hy_f_b3[0], hy_f_freq[0])
    deltas = np.abs(np.linspace(math.log(HY_DECAY_TARGET) / HY_SLOW_DECAY,
                                math.log(HY_DECAY_TARGET) / HY_FAST_DECAY, hy)).astype(np.float32)
    deltas2 = jnp.asarray(np.tile(deltas, 2)[None, :])
    wside = hy_f_wout[0].reshape(-1, 2, 2 * hy).transpose(1, 0, 2).astype(BF16)
    zside = jnp.zeros_like(wside[0])
    wout2 = jnp.stack([jnp.concatenate([wside[0], zside]), jnp.concatenate([zside, wside[1]])])
    hfull, ssq = _ftime_call(a3, jnp.asarray(tcol), wout2, deltas2, n_lat)
    e1, e4, ef, g2, g3 = (jnp.asarray(t).astype(BF16) for t in _fft_tables(n_lat))
    hf = _ffft_call(hfull, ssq, ef, g2, n_lat)

    yh = _hyena_call(hyp, hf, e1, e4, g2, g3, hy_bias[0])
    yh = jnp.transpose(yh, (0, 2, 1, 3)).reshape(bsz, n_lat, hy)

    return _merge_call(o_f, o_b, gp, yh, x, mod, dn_norm_g[0], w_pa[0].astype(BF16), w_pb[0].astype(BF16),
                       w_out[0].astype(BF16), final_g)
```

```python
import functools
import math

import numpy as np
import jax
import jax.numpy as jnp
from jax import lax
from jax.experimental import pallas as pl
from jax.experimental.pallas import tpu as pltpu

F32 = jnp.float32
BF16 = jnp.bfloat16
HIGHEST = lax.Precision.HIGHEST

EPS = 1e-6
HEADS = 8
HEAD_DIM = 128
CHUNK = 64
GRID_W = 64
HY_EMB = 33
HY_DECAY_TARGET = 1e-2
HY_FAST_DECAY = 0.3
HY_SLOW_DECAY = 1.5

LANES = 128
SUBLANES = 8
FFT_INNER = 128
CONV_FFT_STEPS = 8
FILTER_FFT_STEPS = 4
FFT_CHANNELS = 256
FFT_INTERLEAVE = 8
PROJ_TILE = 1024
NORM_TILE = 512
SCAN_CHUNKS = 4
PREP_CHUNKS = 1
HYPROJ_N1 = 16
MXU_TILE = 256
SLAB_PAD = 8
VMEM_LIMIT = 56 * 1024 * 1024
VMEM_LIMIT_CONV = 62 * 1024 * 1024


def _cparams(sem, vmem=VMEM_LIMIT):
    return pltpu.CompilerParams(dimension_semantics=sem, vmem_limit_bytes=vmem)


def _bdot(a, b):
    return jnp.dot(a.astype(BF16), b.astype(BF16), preferred_element_type=F32)


def _silu(x):
    return x * jax.nn.sigmoid(x)


def _mod_kernel(c_ref, w_ref, b_ref, o_ref):
    s = _silu(c_ref[...])
    o_ref[...] = jnp.dot(s, w_ref[...], precision=HIGHEST, preferred_element_type=F32) + b_ref[...]


def _mod_call(cvec, w_mod, b_mod):
    rows, d = cvec.shape
    n = w_mod.shape[1]
    tn = 1024
    return pl.pallas_call(
        _mod_kernel,
        grid=(n // tn,),
        in_specs=[pl.BlockSpec((rows, d), lambda j: (0, 0)),
                  pl.BlockSpec((d, tn), lambda j: (0, j)),
                  pl.BlockSpec((1, tn), lambda j: (0, j))],
        out_specs=pl.BlockSpec((rows, tn), lambda j: (0, j)),
        out_shape=jax.ShapeDtypeStruct((rows, n), F32),
        compiler_params=_cparams(("arbitrary",)),
        name="mod",
    )(cvec, w_mod, b_mod.reshape(1, n))


def _norm_rows(x, m, g_ref, d):
    y = x * lax.rsqrt(jnp.mean(x * x, axis=-1, keepdims=True) + EPS) * g_ref[...]
    return (y * (1.0 + m[:, d:2 * d]) + m[:, 0:d]).astype(BF16)


def _gate_rows(h, w_ref, alog_ref, dtb_ref):
    z = jnp.dot(h, w_ref[...], preferred_element_type=F32)
    lane = lax.broadcasted_iota(jnp.int32, z.shape, 1)
    u = z + dtb_ref[...]
    softplus = jnp.maximum(u, 0.0) + jnp.log(1.0 + jnp.exp(-jnp.abs(u)))
    return jnp.where(lane < 2 * HEADS, jax.nn.sigmoid(z), -jnp.exp(alog_ref[...]) * softplus)


def _hnorm_kernel(tok_ref, mod_ref, g_ref, w_ref, alog_ref, dtb_ref, *rest, d, mod_row):
    o_ref, og_ref = rest[-2:]
    row = pl.program_id(0) if mod_row is None else mod_row
    m = mod_ref[pl.ds(row, 1), :]
    h = _norm_rows(tok_ref[0], m, g_ref, d)
    o_ref[0] = h
    og_ref[0] = _gate_rows(h, w_ref, alog_ref, dtb_ref)


def _hnorm_call(tok, mod, norm_g, wg, alog, dtb, *, tm, gate_rows, gate_tile0, mod_row=None, prev_gates=None, name):
    bsz, rows, d = tok.shape
    full = lambda a: pl.BlockSpec(a.shape, lambda b, t: (0,) * a.ndim)
    args = [tok, mod, norm_g.reshape(1, d), wg, alog, dtb]
    in_specs = [pl.BlockSpec((1, tm, d), lambda b, t: (b, t, 0))] + [full(a) for a in args[1:]]
    aliases = {}
    if prev_gates is not None:
        in_specs.append(pl.BlockSpec(memory_space=pl.ANY))
        args.append(prev_gates)
        aliases = {6: 1}
    return pl.pallas_call(
        functools.partial(_hnorm_kernel, d=d, mod_row=mod_row),
        grid=(bsz, rows // tm),
        in_specs=in_specs,
        out_specs=[pl.BlockSpec((1, tm, d), lambda b, t: (b, t, 0)),
                   pl.BlockSpec((1, tm, LANES), lambda b, t: (b, t + gate_tile0, 0))],
        out_shape=[jax.ShapeDtypeStruct((bsz, rows, d), BF16), jax.ShapeDtypeStruct((bsz, gate_rows, LANES), F32)],
        input_output_aliases=aliases,
        compiler_params=_cparams(("arbitrary", "arbitrary")),
        name=name,
    )(*args)


@functools.lru_cache(maxsize=None)
def _shift_matrix(rowlen, taps, blk):
    t = np.arange(blk)
    mats = []
    for j in range(taps):
        d = j - taps // 2
        if d != 0:
            ok = ((t + d) >= 0) & ((t + d) < blk) & (t // rowlen == (t + d) // rowlen)
            s = np.zeros((blk, blk), np.float32)
            s[t[ok], t[ok] + d] = 1.0
            mats.append(s)
    return np.concatenate(mats, axis=1)


def _shift_block(rowlen, taps):
    return max(rowlen, MXU_TILE // (taps - 1) // rowlen * rowlen)


def _convproj_kernel(h_ref, w_ref, cw_ref, sh_ref, *rest, tm):
    o_ref = rest[-1]
    j = pl.program_id(0)
    taps = cw_ref.shape[1]
    mid = taps // 2
    blk = sh_ref.shape[0]
    mblk = max(blk, min(tm, 2 * MXU_TILE))
    na = jnp.where(j == 0, float(HEAD_DIM), jnp.where(j == 1, 1.0, 0.0))
    nb = jnp.where(j == 0, EPS * HEAD_DIM, jnp.where(j == 1, EPS, 1.0))
    cw16 = cw_ref[0].astype(BF16)
    for r in range(tm // mblk):
        zm = jnp.dot(h_ref[0, r * mblk:(r + 1) * mblk, :], w_ref[0], preferred_element_type=F32)
        for g in range(mblk // blk):
            rows = slice(r * mblk + g * blk, r * mblk + (g + 1) * blk)
            zb = zm[g * blk:(g + 1) * blk]
            zb16 = zb.astype(BF16)
            side = jnp.concatenate([zb16 * cw16[jt:jt + 1, :] for jt in range(taps) if jt != mid], axis=0)
            y = zb * cw_ref[0, mid:mid + 1, :] + jnp.dot(sh_ref[...], side, preferred_element_type=F32)
            y = _silu(y)
            for hh in range(HEADS):
                yh = y[:, hh * HEAD_DIM:(hh + 1) * HEAD_DIM]
                nrm = jnp.sum(yh * yh, axis=-1, keepdims=True)
                o_ref[0, 0, rows, hh * HEAD_DIM:(hh + 1) * HEAD_DIM] = (
                    yh * lax.rsqrt(nrm * na + nb)).astype(o_ref.dtype)


def _convproj_call(h, w3, cw3, *, rowlen, tm, tile0, out_rows, prev=None, name):
    bsz, rows, d = h.shape
    ntiles = rows // tm
    nw = w3.shape[2]
    taps = cw3.shape[1]
    blk = _shift_block(rowlen, taps)
    assert tm % max(blk, MXU_TILE) == 0 and max(blk, MXU_TILE) % blk == 0
    sh = jnp.asarray(_shift_matrix(rowlen, taps, blk)).astype(BF16)
    in_specs = [pl.BlockSpec((1, tm, d), lambda j, b, t: (b, t, 0)),
                pl.BlockSpec((1, d, nw), lambda j, b, t: (j, 0, 0)),
                pl.BlockSpec((1, cw3.shape[1], nw), lambda j, b, t: (j, 0, 0)),
                pl.BlockSpec(sh.shape, lambda j, b, t: (0, 0))]
    args = [h, w3, cw3, sh]
    aliases = {}
    if prev is not None:
        in_specs.append(pl.BlockSpec(memory_space=pl.ANY))
        args.append(prev)
        aliases = {4: 0}
    return pl.pallas_call(
        functools.partial(_convproj_kernel, tm=tm),
        grid=(3, bsz, ntiles),
        in_specs=in_specs,
        out_specs=pl.BlockSpec((1, 1, tm, nw), lambda j, b, t: (j, b, t + tile0, 0)),
        out_shape=jax.ShapeDtypeStruct((3, bsz, out_rows, nw), BF16),
        input_output_aliases=aliases,
        compiler_params=_cparams(("arbitrary", "arbitrary", "arbitrary")),
        name=name,
    )(*args)


def _hyproj_kernel(h_ref, w_ref, cw_ref, cb_ref, o_ref):
    n2s, n1s, d = h_ref.shape[1:]
    taps = cw_ref.shape[1]
    hrows = h_ref[0].reshape(n2s * n1s, d)
    step = min(MXU_TILE, n2s * n1s)
    z = jnp.concatenate([jnp.dot(hrows[r:r + step], w_ref[0], preferred_element_type=F32)
                         for r in range(0, n2s * n1s, step)], axis=0)
    y = cb_ref[0] + z * cw_ref[0, taps // 2:taps // 2 + 1, :]
    for jt in range(taps):
        s = (jt - taps // 2) * n1s
        if s > 0:
            y = y + jnp.concatenate([z[s:], jnp.zeros((s, z.shape[1]), F32)], axis=0) * cw_ref[0, jt:jt + 1, :]
        elif s < 0:
            y = y + jnp.concatenate([jnp.zeros((-s, z.shape[1]), F32), z[:s]], axis=0) * cw_ref[0, jt:jt + 1, :]
    o_ref[0, 0] = y.astype(o_ref.dtype).reshape(n2s, n1s, y.shape[1])


def _hyproj_call(h_perm, w3, cw3, cb3):
    bsz, n_lat, d = h_perm.shape
    nw = w3.shape[2]
    h1 = n_lat // FFT_INNER
    n1s = min(HYPROJ_N1, h1)
    hv = h_perm.reshape(bsz, FFT_INNER, h1, d)
    return pl.pallas_call(
        _hyproj_kernel,
        grid=(3, bsz, FFT_INNER // GRID_W, h1 // n1s),
        in_specs=[pl.BlockSpec((1, GRID_W, n1s, d), lambda j, b, r, t: (b, r, t, 0)),
                  pl.BlockSpec((1, d, nw), lambda j, b, r, t: (j, 0, 0)),
                  pl.BlockSpec((1, cw3.shape[1], nw), lambda j, b, r, t: (j, 0, 0)),
                  pl.BlockSpec((1, 1, nw), lambda j, b, r, t: (j, 0, 0))],
        out_specs=pl.BlockSpec((1, 1, GRID_W, n1s, nw), lambda j, b, r, t: (j, b, r, t, 0)),
        out_shape=jax.ShapeDtypeStruct((3, bsz, FFT_INNER, h1, nw), BF16),
        compiler_params=_cparams(("arbitrary",) * 4),
        name="hyproj",
    )(hv, w3, cw3, cb3)


def _gproj_kernel(h_ref, w_ref, o_ref):
    j = pl.program_id(0)
    for r in range(h_ref.shape[1] // MXU_TILE):
        rows = slice(r * MXU_TILE, (r + 1) * MXU_TILE)
        z = jnp.dot(h_ref[0, rows, :], w_ref[0], preferred_element_type=F32)
        s = jax.nn.sigmoid(z)
        o_ref[0, 0, rows, :] = jnp.where(j < 2, z * s, s).astype(BF16)


def _gproj_call(h, w4, n_lat):
    bsz, _, d = h.shape
    nw = w4.shape[2]
    tm = PROJ_TILE if n_lat % PROJ_TILE == 0 else 256
    return pl.pallas_call(
        _gproj_kernel,
        grid=(4, bsz, n_lat // tm),
        in_specs=[pl.BlockSpec((1, tm, d), lambda j, b, t: (b, t, 0)),
                  pl.BlockSpec((1, d, nw), lambda j, b, t: (j, 0, 0))],
        out_specs=pl.BlockSpec((1, 1, tm, nw), lambda j, b, t: (j, b, t, 0)),
        out_shape=jax.ShapeDtypeStruct((4, bsz, n_lat, nw), BF16),
        compiler_params=_cparams(("arbitrary", "arbitrary", "arbitrary")),
        name="gproj",
    )(h, w4)


def _unit_tri_inverse(ls, lowers, filler):
    c = ls[0].shape[0]
    ri = lax.broadcasted_iota(jnp.int32, (c, c), 0)
    ci = lax.broadcasted_iota(jnp.int32, (c, c), 1)
    same = lambda n: (ri // n) == (ci // n)
    eye = jnp.where(ri == ci, 1.0, 0.0)
    in2 = same(2)
    ts = [eye - jnp.where(in2, l, 0.0) for l in ls]
    n = 2
    nlevels = c.bit_length() - 2
    level = 0
    while n < c:
        blk = same(2 * n) & jnp.logical_not(same(n))
        offs = [jnp.where(blk, l, 0.0).astype(BF16) for l in ls]
        if n % SUBLANES:
            prods = [_bdot(t, off) for t, off in zip(ts, offs)]
            filler(level, nlevels)
            ts = [t - _bdot(p, t) for p, t in zip(prods, ts)]
        else:
            def moving(t, lower):
                first = n if lower else 0
                return jnp.concatenate([t[r:r + n] for r in range(first, c, 2 * n)], axis=0)

            def merged(t, upd, lower):
                pieces, k = [], 0
                for r in range(0, c, n):
                    if ((r // n) % 2 == 1) == lower:
                        pieces.append(upd[k * n:(k + 1) * n])
                        k += 1
                    else:
                        pieces.append(t[r:r + n])
                return jnp.concatenate(pieces, axis=0)

            rows = [moving(t, lo) for t, lo in zip(ts, lowers)]
            prods = [_bdot(r, off) for r, off in zip(rows, offs)]
            filler(level, nlevels)
            upds = [r - _bdot(p, t) for r, p, t in zip(rows, prods, ts)]
            ts = [merged(t, u, lo) for t, u, lo in zip(ts, upds, lowers)]
        n *= 2
        level += 1
    return ts


_NT = (((1,), (1,)), ((), ()))
_TN = (((0,), (0,)), ((), ()))


def _prep_kernel(qkv_ref, g_ref, gt_ref, tall_ref, wide_ref, el_ref, *, bsz):
    c = CHUNK
    ri = lax.broadcasted_iota(jnp.int32, (c, c), 0)
    ci = lax.broadcasted_iota(jnp.int32, (c, c), 1)
    tri_l = jnp.where(ri >= ci, 1.0, 0.0)
    tri_u = jnp.where(ri <= ci, 1.0, 0.0)
    incl = (ri >= ci, ri <= ci)
    strict = (ri > ci, ri < ci)
    hdot = functools.partial(jnp.dot, precision=HIGHEST, preferred_element_type=F32)

    per = tall_ref.shape[0]
    cb = [(ch, b) for ch in range(per) for b in range(bsz)]
    rows = lambda ch: slice(ch * c, (ch + 1) * c)
    gs = [g_ref[b, rows(ch), :] for ch, b in cb]
    gts = [gt_ref[b, ch] for ch, b in cb]
    gcol = [(hdot(tri_l, g), hdot(tri_u, g)) for g in gs]
    grow = [(hdot(gt, tri_u), hdot(gt, tri_l)) for gt in gts]
    glast = [(gc[0][c - 1:c], gc[1][0:1]) for gc in gcol]
    el_ref[...] = jnp.zeros_like(el_ref)
    for p, (ch, b) in enumerate(cb):
        for d in range(2):
            el_ref[ch, d * bsz + b:d * bsz + b + 1, :] = jnp.exp(glast[p][d])

    pbh = [(p, hh) for p in range(len(cb)) for hh in range(HEADS)]
    hs = lambda hh: slice(hh * HEAD_DIM, (hh + 1) * HEAD_DIM)
    tok = lambda w, p, hh: qkv_ref[w, cb[p][1], rows(cb[p][0]), hs(hh)]
    ks = [tok(1, p, hh) for p, hh in pbh]
    qs = [tok(0, p, hh) for p, hh in pbh]
    kk = [lax.dot_general(k, k, _NT, preferred_element_type=F32) for k in ks]
    qk = [lax.dot_general(q, k, _NT, preferred_element_type=F32) for q, k in zip(qs, ks)]
    inst = [(d, p, hh, i) for d in range(2) for i, (p, hh) in enumerate(pbh)]
    decs, betas, gcs = [], [], []
    for d, p, hh, _ in inst:
        lg = 2 * HEADS + d * HEADS + hh
        gcs.append(gcol[p][d][:, lg:lg + 1])
        betas.append(jnp.broadcast_to(gs[p][:, d * HEADS + hh:d * HEADS + hh + 1], (c, HEAD_DIM)))
        decs.append(jnp.exp(jnp.where(incl[d], gcs[-1] - grow[p][d][lg:lg + 1, :], -jnp.inf)))
    ls = [jnp.where(strict[d], kk[i] * beta[:, 0:c] * dec, 0.0)
          for (d, _, _, i), beta, dec in zip(inst, betas, decs)]

    def scaled_operands(part, nparts):
        for (d, p, hh, i), dec, beta, gc in list(zip(inst, decs, betas, gcs))[part::nparts]:
            ch, b = cb[p]
            lg = 2 * HEADS + d * HEADS + hh
            eg = jnp.broadcast_to(jnp.exp(gc), (c, HEAD_DIM))
            k = ks[i].astype(F32)
            tall_ref[ch, d, b, hh, :, c:2 * c] = (qk[i] * dec).astype(BF16)
            wide_ref[ch, d, b, hh, 0:c, :] = (qs[i].astype(F32) * eg).astype(BF16)
            wide_ref[ch, d, b, hh, c:2 * c, :] = (k * beta * eg).astype(BF16)
            wide_ref[ch, d, b, hh, 2 * c:3 * c, :] = (tok(2, p, hh).astype(F32) * beta).astype(BF16)
            wide_ref[ch, d, b, hh, 3 * c:4 * c, :] = (k * jnp.exp(glast[p][d][:, lg:lg + 1] - gc)).astype(BF16)

    ts = _unit_tri_inverse(ls, [d == 0 for d, _, _, _ in inst], scaled_operands)
    for (d, p, hh, _), t in zip(inst, ts):
        tall_ref[cb[p][0], d, cb[p][1], hh, :, 0:c] = t.astype(BF16)


def _prep_call(qkv, gates, gates_t):
    _, bsz, lt, dn = qkv.shape
    nc = lt // CHUNK
    per = PREP_CHUNKS
    assert nc % per == 0
    tall = (2, bsz, HEADS, CHUNK, 2 * CHUNK)
    wide = (2, bsz, HEADS, 4 * CHUNK, HEAD_DIM)
    return pl.pallas_call(
        functools.partial(_prep_kernel, bsz=bsz),
        grid=(nc // per,),
        in_specs=[pl.BlockSpec((3, bsz, per * CHUNK, dn), lambda i: (0, 0, i, 0)),
                  pl.BlockSpec((bsz, per * CHUNK, LANES), lambda i: (0, i, 0)),
                  pl.BlockSpec((bsz, per, 4 * HEADS, CHUNK), lambda i: (0, i, 0, 0))],
        out_specs=[pl.BlockSpec((per,) + tall, lambda i: (i, 0, 0, 0, 0, 0)),
                   pl.BlockSpec((per,) + wide, lambda i: (i, 0, 0, 0, 0, 0)),
                   pl.BlockSpec((per, SUBLANES, LANES), lambda i: (i, 0, 0))],
        out_shape=[jax.ShapeDtypeStruct((nc,) + tall, BF16),
                   jax.ShapeDtypeStruct((nc,) + wide, BF16),
                   jax.ShapeDtypeStruct((nc, SUBLANES, LANES), F32)],
        compiler_params=_cparams(("arbitrary",)),
        name="prep",
    )(qkv, gates, gates_t)


def _scan_kernel(tf_ref, tb_ref, wf_ref, wb_ref, ef_ref, eb_ref, of_ref, ob_ref, s_ref, *, bsz):
    i = pl.program_id(0)

    @pl.when(i == 0)
    def _():
        s_ref[...] = jnp.zeros_like(s_ref)

    c = CHUNK
    per = tf_ref.shape[0]
    dirs = ((tf_ref, wf_ref, ef_ref, of_ref), (tb_ref, wb_ref, eb_ref, ob_ref))
    inst = [(d, b, hh) for d in range(2) for b in range(bsz) for hh in range(HEADS)]
    state = [s_ref[d, b, hh] for d, b, hh in inst]
    for sub in range(per):
        at = (sub, per - 1 - sub)
        a1 = [jnp.dot(dirs[d][1][at[d], b, hh, 0:2 * c, :], s.astype(BF16), preferred_element_type=F32)
              for (d, b, hh), s in zip(inst, state)]
        v_new = [jnp.dot(dirs[d][0][at[d], b, hh, :, 0:c],
                         (dirs[d][1][at[d], b, hh, 2 * c:3 * c, :].astype(F32) - a[c:]).astype(BF16),
                         preferred_element_type=F32).astype(BF16) for (d, b, hh), a in zip(inst, a1)]
        new_state = []
        for (d, b, hh), a, vn, s in zip(inst, a1, v_new, state):
            lg = 2 * HEADS + d * HEADS + hh
            o = a[:c] + jnp.dot(dirs[d][0][at[d], b, hh, :, c:2 * c], vn, preferred_element_type=F32)
            dirs[d][3][b, at[d] * c:(at[d] + 1) * c, hh * HEAD_DIM:(hh + 1) * HEAD_DIM] = o.astype(BF16)
            el = dirs[d][2][at[d], d * bsz + b:d * bsz + b + 1, lg:lg + 1]
            new_state.append(s * el + lax.dot_general(dirs[d][1][at[d], b, hh, 3 * c:4 * c, :], vn, _TN,
                                                      preferred_element_type=F32))
        state = new_state
    for (d, b, hh), s in zip(inst, state):
        s_ref[d, b, hh] = s


def _scan_call(tall, wide, el, n_lat, n_ctx):
    nc, _, bsz = tall.shape[:3]
    dn = HEADS * HEAD_DIM
    per = SCAN_CHUNKS
    ncx, ncc, ns = n_lat // CHUNK // per, n_ctx // CHUNK // per, nc // per
    assert nc == (ncx + ncc) * per
    cf = lambda i: jnp.where(i < ncc, ncx + i, i - ncc)
    cb = lambda i: ns - 1 - i

    dspec = lambda arr, fn, d: pl.BlockSpec((per, None) + arr.shape[2:], lambda i: (fn(i), d, 0, 0, 0, 0))
    espec = lambda fn: pl.BlockSpec((per, SUBLANES, LANES), lambda i: (fn(i), 0, 0))
    ospec = lambda fn: pl.BlockSpec((bsz, per * CHUNK, dn), lambda i: (0, fn(i), 0))
    return pl.pallas_call(
        functools.partial(_scan_kernel, bsz=bsz),
        grid=(ns,),
        in_specs=[dspec(tall, cf, 0), dspec(tall, cb, 1), dspec(wide, cf, 0), dspec(wide, cb, 1),
                  espec(cf), espec(cb)],
        out_specs=[ospec(lambda i: jnp.maximum(i - ncc, 0)), ospec(lambda i: jnp.minimum(ns - 1 - i, ncx - 1))],
        out_shape=[jax.ShapeDtypeStruct((bsz, n_lat, dn), BF16)] * 2,
        scratch_shapes=[pltpu.VMEM((2, bsz, HEADS, HEAD_DIM, HEAD_DIM), F32)],
        compiler_params=_cparams(("arbitrary",)),
        name="scan",
    )(tall, tall, wide, wide, el, el)


def _hidden_kernel(pe_ref, w1_ref, b1_ref, w2_ref, b2_ref, w3_ref, b3_ref, f_ref, o_ref):
    f = f_ref[...]
    dot = functools.partial(jnp.dot, precision=HIGHEST, preferred_element_type=F32)
    a = jnp.sin(f * (dot(pe_ref[...], w1_ref[...]) + b1_ref[...]))
    a = jnp.sin(f * (dot(a, w2_ref[...]) + b2_ref[...]))
    o_ref[...] = jnp.sin(f * (dot(a, w3_ref[...]) + b3_ref[...]))


def _hidden_call(pe, w1p, b1, w2, b2, w3, b3, freq):
    rows, pw = pe.shape
    fw = 2 * w2.shape[0]
    tr = 512
    full = lambda a: pl.BlockSpec(a.shape, lambda t: (0,) * a.ndim)
    two = lambda a: jnp.tile(a.reshape(1, -1), (1, 2))
    diag2 = lambda a: jnp.kron(jnp.eye(2, dtype=a.dtype), a)
    args = (diag2(w1p), two(b1), diag2(w2), two(b2), diag2(w3), two(b3), two(freq))
    return pl.pallas_call(
        _hidden_kernel,
        grid=(rows // tr,),
        in_specs=[pl.BlockSpec((tr, pw), lambda t: (t, 0))] + [full(a) for a in args],
        out_specs=pl.BlockSpec((tr, fw), lambda t: (t, 0)),
        out_shape=jax.ShapeDtypeStruct((rows, fw), F32),
        compiler_params=_cparams(("arbitrary",)),
        name="hidden",
    )(pe, *args)


def _ftime_kernel(a_ref, t_ref, w_ref, dl_ref, h_ref, ss_ref, *, n_lat, tr):
    t = pl.program_id(0)
    h = _bdot(a_ref[...], w_ref[...]) * jnp.exp(-t_ref[...] * dl_ref[...])
    row = lax.broadcasted_iota(jnp.int32, (tr, 1), 0) + t * tr
    h = jnp.where(row == n_lat, 0.0, h)
    h_ref[...] = h.astype(h_ref.dtype)

    @pl.when(t == 0)
    def _():
        ss_ref[...] = jnp.zeros_like(ss_ref)

    ss_ref[...] += jnp.sum(h * h, axis=0, keepdims=True)


def _ftime_call(a3, tcol, wout2, deltas2, n_lat):
    rows = tcol.shape[0]
    fw = a3.shape[1]
    ncol = wout2.shape[2]
    tr = 512
    side = lambda t: (t * tr) // n_lat
    return pl.pallas_call(
        functools.partial(_ftime_kernel, n_lat=n_lat, tr=tr),
        grid=(rows // tr,),
        in_specs=[pl.BlockSpec((tr, fw), lambda t: (t % (n_lat // tr), 0)),
                  pl.BlockSpec((tr, 1), lambda t: (t, 0)),
                  pl.BlockSpec((None, fw, ncol), lambda t: (side(t), 0, 0)),
                  pl.BlockSpec((1, ncol), lambda t: (0, 0))],
        out_specs=[pl.BlockSpec((tr, ncol), lambda t: (t, 0)),
                   pl.BlockSpec((1, ncol), lambda t: (0, 0))],
        out_shape=[jax.ShapeDtypeStruct((rows, ncol), BF16), jax.ShapeDtypeStruct((1, ncol), F32)],
        compiler_params=_cparams(("arbitrary",)),
        name="ftime",
    )(a3, tcol, wout2, deltas2)


@functools.lru_cache(maxsize=None)
def _fft_tables(n_lat):
    n = 2 * n_lat
    n1t = n // FFT_INNER
    h1 = n1t // 2
    n2 = np.arange(FFT_INNER, dtype=np.int64)[:, None, None]
    k1 = np.arange(n1t, dtype=np.int64)[None, :, None]
    n1 = np.arange(h1, dtype=np.int64)[None, None, :]
    ang = (-2.0 * np.pi / n) * ((k1 * (FFT_INNER * n1 + n2)) % n)
    er, ei = np.cos(ang), np.sin(ang)
    e1 = np.concatenate([np.concatenate([er, -ei], axis=2), np.concatenate([ei, er], axis=2)], axis=1)
    e4 = np.transpose(e1, (0, 2, 1))
    sgn = np.where(np.arange(n1t) % 2 == 0, 1.0, -1.0)[None, :, None]
    ef = np.concatenate([np.concatenate([er, sgn * er], axis=2), np.concatenate([ei, sgn * ei], axis=2)], axis=1)
    a = np.arange(FFT_INNER, dtype=np.int64)
    ang2 = (-2.0 * np.pi / FFT_INNER) * ((a[:, None] * a[None, :]) % FFT_INNER)
    gr, gi = np.cos(ang2), np.sin(ang2)
    g2 = np.block([[gr, -gi], [gi, gr]])
    f32 = lambda x: np.ascontiguousarray(x, dtype=np.float32)
    return f32(e1), f32(e4), f32(ef), f32(g2), f32(g2.T)


def _hold(p, j, first, last, nsteps):
    active = (p >= first) & (p <= last) & (lax.rem(p - first, 2) == 0)
    return jnp.where(active, j, jnp.where(p < first, 0, nsteps - 1))


def _slab_read(x_ref, rows):
    return jnp.concatenate([x_ref[c, rows, :] for c in range(x_ref.shape[0])], axis=1)


def _slab_write(x_ref, rows, val):
    for c in range(x_ref.shape[0]):
        x_ref[c, rows, :] = val[:, c * LANES:(c + 1) * LANES]


def _spectrum_steps(x_ref, g2_ref, k1s, n1t, stride):
    cols = [jnp.concatenate([_slab_read(x_ref, pl.ds(k1, FFT_INNER, stride=stride)),
                             _slab_read(x_ref, pl.ds(n1t + k1, FFT_INNER, stride=stride))], axis=0).astype(BF16)
            for k1 in k1s]
    return [jnp.dot(g2_ref[...], c, preferred_element_type=F32) for c in cols]


def _ffft_kernel(h_ref, ef_ref, g2_ref, ss_ref, o_ref, x_ref, *, n1t, grp, kgrp, stride, inv_n):
    p = pl.program_id(1)
    j = pl.program_id(2)

    @pl.when(p == 0)
    def _():
        for j0 in range(0, grp, FFT_INTERLEAVE):
            jjs = list(range(j0, min(j0 + FFT_INTERLEAVE, grp)))
            us = [jnp.concatenate([h_ref[0, jj], h_ref[1, jj]], axis=0) for jj in jjs]
            slabs = [_bdot(ef_ref[jj], u) for jj, u in zip(jjs, us)]
            for jj, slab in zip(jjs, slabs):
                base = pl.multiple_of((j * grp + jj) * stride, SUBLANES)
                _slab_write(x_ref, pl.ds(base, 2 * n1t), slab)

    @pl.when(p == 1)
    def _():
        scale = lax.rsqrt(ss_ref[...] + EPS) * inv_n
        for j0 in range(0, kgrp, FFT_INTERLEAVE):
            jjs = list(range(j0, min(j0 + FFT_INTERLEAVE, kgrp)))
            xfs = _spectrum_steps(x_ref, g2_ref, [j * kgrp + jj for jj in jjs], n1t, stride)
            for jj, xf in zip(jjs, xfs):
                o_ref[jj] = (xf * scale).astype(BF16)


def _fft_groups(n1t, ns):
    ns = min(ns, n1t)
    assert FFT_INNER % ns == 0 and n1t % ns == 0
    return ns, FFT_INNER // ns, n1t // ns


def _ffft_call(hfull, ssq, ef, g2, n_lat):
    ncol = hfull.shape[1]
    n1t = 2 * n_lat // FFT_INNER
    h1 = n1t // 2
    ct = FFT_CHANNELS
    ns, grp, kgrp = _fft_groups(n1t, FILTER_FFT_STEPS)
    stride = 2 * n1t + SLAB_PAD
    hv = hfull.reshape(2, FFT_INNER, h1, ncol)
    nct = ncol // ct
    cpo = nct // 2
    return pl.pallas_call(
        functools.partial(_ffft_kernel, n1t=n1t, grp=grp, kgrp=kgrp, stride=stride, inv_n=1.0 / (2 * n_lat)),
        grid=(nct, 2, ns),
        in_specs=[pl.BlockSpec((2, grp, h1, ct), lambda c, p, j: (0, _hold(p, j, 0, 0, ns), 0, c)),
                  pl.BlockSpec((grp, 2 * n1t, n1t), lambda c, p, j: (_hold(p, j, 0, 0, ns), 0, 0)),
                  pl.BlockSpec((2 * FFT_INNER, 2 * FFT_INNER), lambda c, p, j: (0, 0)),
                  pl.BlockSpec((1, ct), lambda c, p, j: (0, c))],
        out_specs=pl.BlockSpec((None, kgrp, 2 * FFT_INNER, ct),
                               lambda c, p, j: (c // cpo, _hold(p, j, 1, 1, ns), 0, c % cpo)),
        out_shape=jax.ShapeDtypeStruct((2, n1t, 2 * FFT_INNER, ncol // 2), BF16),
        scratch_shapes=[pltpu.VMEM((ct // LANES, FFT_INNER * stride, LANES), F32)],
        compiler_params=_cparams(("arbitrary", "arbitrary", "arbitrary")),
        name="ffft",
    )(hv, ef, g2, ssq)


def _hyena_kernel(v_ref, x1_ref, x2_ref, hf_ref, e1_ref, e4_ref, g2_ref, g3_ref, bias_ref,
                  o_ref, x_ref, z_ref, *, n1t, grp, kgrp, stride):
    p = pl.program_id(1)
    j = pl.program_id(2)
    h1 = n1t // 2

    def gather(ref, jj):
        return jnp.concatenate([ref[0, jj], ref[1, jj]], axis=0).astype(F32)

    def slab_base(jj):
        return pl.multiple_of((j * grp + jj) * stride, SUBLANES)

    def time_rows(jjs):
        slabs = [_slab_read(x_ref, pl.ds(slab_base(jj), 2 * n1t)).astype(BF16) for jj in jjs]
        return [jnp.dot(e4_ref[jj], s, preferred_element_type=F32) for jj, s in zip(jjs, slabs)]

    def write_slabs(jjs, us):
        slabs = [jnp.dot(e1_ref[jj], u, preferred_element_type=F32) for jj, u in zip(jjs, us)]
        for jj, slab in zip(jjs, slabs):
            _slab_write(x_ref, pl.ds(slab_base(jj), 2 * n1t), slab)

    def chunks(n):
        return [list(range(j0, min(j0 + FFT_INTERLEAVE, n))) for j0 in range(0, n, FFT_INTERLEAVE)]

    @pl.when(p == 0)
    def _():
        for jjs in chunks(grp):
            write_slabs(jjs, [gather(v_ref, jj).astype(BF16) for jj in jjs])

    @pl.when((p == 1) | (p == 3))
    def _():
        for jjs in chunks(kgrp):
            k1s = [j * kgrp + jj for jj in jjs]
            xfs = _spectrum_steps(x_ref, g2_ref, k1s, n1t, stride)
            ys = []
            for jj, xf in zip(jjs, xfs):
                hf = hf_ref[jj].astype(F32)
                xr, xi = xf[:FFT_INNER], xf[FFT_INNER:]
                hr, hi = hf[:FFT_INNER], hf[FFT_INNER:]
                ys.append(jnp.concatenate([xr * hr - xi * hi, xr * hi + xi * hr], axis=0).astype(BF16))
            bbs = [jnp.dot(g3_ref[...], y, preferred_element_type=F32) for y in ys]
            for k1, bb in zip(k1s, bbs):
                _slab_write(x_ref, pl.ds(k1, FFT_INNER, stride=stride), bb[:FFT_INNER])
                _slab_write(x_ref, pl.ds(n1t + k1, FFT_INNER, stride=stride), bb[FFT_INNER:])

    @pl.when(p == 2)
    def _():
        for jjs in chunks(grp):
            z1s = [(gather(x1_ref, jj) * (y + gather(v_ref, jj) * bias_ref[0:1, :])).astype(BF16)
                   for jj, y in zip(jjs, time_rows(jjs))]
            for jj, z1 in zip(jjs, z1s):
                z_ref[j * grp + jj] = z1
            write_slabs(jjs, z1s)

    @pl.when(p == 4)
    def _():
        for jjs in chunks(grp):
            for jj, y in zip(jjs, time_rows(jjs)):
                out = gather(x2_ref, jj) * (y + z_ref[j * grp + jj].astype(F32) * bias_ref[1:2, :])
                o_ref[0, jj] = out[:h1].astype(o_ref.dtype)
                o_ref[1, jj] = out[h1:].astype(o_ref.dtype)


def _hyena_call(hv, hf, e1, e4, g2, g3, bias):
    _, bsz, _, h1, ch = hv.shape
    assert bsz == 2, "the two batch rows are packed as the real and imaginary parts of one FFT"
    n1t = 2 * h1
    ct = FFT_CHANNELS
    ns, grp, kgrp = _fft_groups(n1t, CONV_FFT_STEPS)
    stride = 2 * n1t + SLAB_PAD

    def hspec(which, first, last):
        return pl.BlockSpec((None, bsz, grp, h1, ct),
                            lambda c, p, j: (which, 0, _hold(p, j, first, last, ns), 0, c))

    def hf_map(c, p, j):
        order = jnp.where(p >= 2, 1, 0)
        step = jnp.where((p == 1) | (p == 3), j, jnp.where((p == 0) | (p == 2), 0, ns - 1))
        return (order, step, 0, c)

    return pl.pallas_call(
        functools.partial(_hyena_kernel, n1t=n1t, grp=grp, kgrp=kgrp, stride=stride),
        grid=(ch // ct, 5, ns),
        in_specs=[hspec(0, 0, 2), hspec(1, 2, 2), hspec(2, 4, 4),
                  pl.BlockSpec((None, kgrp, 2 * FFT_INNER, ct), hf_map),
                  pl.BlockSpec((grp, 2 * n1t, n1t), lambda c, p, j: (_hold(p, j, 0, 2, ns), 0, 0)),
                  pl.BlockSpec((grp, n1t, 2 * n1t), lambda c, p, j: (_hold(p, j, 2, 4, ns), 0, 0)),
                  pl.BlockSpec((2 * FFT_INNER, 2 * FFT_INNER), lambda c, p, j: (0, 0)),
                  pl.BlockSpec((2 * FFT_INNER, 2 * FFT_INNER), lambda c, p, j: (0, 0)),
                  pl.BlockSpec((2, ct), lambda c, p, j: (0, c))],
        out_specs=pl.BlockSpec((bsz, grp, h1, ct), lambda c, p, j: (0, _hold(p, j, 4, 4, ns), 0, c)),
        out_shape=jax.ShapeDtypeStruct((bsz, FFT_INNER, h1, ch), BF16),
        scratch_shapes=[pltpu.VMEM((ct // LANES, FFT_INNER * stride, LANES), F32),
                        pltpu.VMEM((FFT_INNER, n1t, ct), BF16)],
        compiler_params=_cparams(("arbitrary", "arbitrary", "arbitrary"), VMEM_LIMIT_CONV),
        name="hyena",
    )(hv, hv, hv, hf, e1, e4, g2, g3, bias)


def _merge_kernel(of_ref, ob_ref, gp_ref, hy_ref, x_ref, mod_ref, dng_ref, wpa_ref, wpb_ref, wo_ref,
                  fg_ref, o_ref, *, d):
    b = pl.program_id(0)
    blocks = [slice(r * MXU_TILE, (r + 1) * MXU_TILE) for r in range(o_ref.shape[1] // MXU_TILE)]
    f32 = lambda ref, *idx: ref[idx].astype(F32)

    def head_norm(rows):
        o = f32(of_ref, 0, rows) + f32(ob_ref, 0, rows)
        za = f32(gp_ref, 0, 0, rows)
        parts = []
        for hh in range(HEADS):
            hs = slice(hh * HEAD_DIM, (hh + 1) * HEAD_DIM)
            oh = o[:, hs]
            ms = jnp.mean(oh * oh, axis=-1, keepdims=True)
            parts.append((oh * lax.rsqrt(ms + EPS) * dng_ref[...] * za[:, hs]).astype(BF16))
        return jnp.concatenate(parts, axis=1)

    o_a = [head_norm(rows) for rows in blocks]
    y_a = [jnp.dot(a, wpa_ref[...], preferred_element_type=F32) for a in o_a]
    o_b = [(f32(hy_ref, 0, rows) * f32(gp_ref, 1, 0, rows)).astype(BF16) for rows in blocks]
    y_b = [jnp.dot(a, wpb_ref[...], preferred_element_type=F32) for a in o_b]
    m = [(f32(gp_ref, 2, 0, rows) * ya + f32(gp_ref, 3, 0, rows) * yb).astype(BF16)
         for rows, ya, yb in zip(blocks, y_a, y_b)]
    y = [jnp.dot(a, wo_ref[...], preferred_element_type=F32) for a in m]
    gate = mod_ref[pl.ds(b, 1), :][:, 2 * d:3 * d]
    for rows, yr in zip(blocks, y):
        xn = x_ref[0, rows, :] + gate * yr
        o_ref[0, rows, :] = xn * lax.rsqrt(jnp.mean(xn * xn, axis=-1, keepdims=True) + EPS) * fg_ref[...]


def _merge_call(o_f, o_b, gp, hy, x, mod, dn_norm_g, w_pa, w_pb, w_out, final_g):
    bsz, n_lat, d = x.shape
    tm = NORM_TILE if n_lat % NORM_TILE == 0 else 256
    wfull = lambda a: pl.BlockSpec(a.shape, lambda b, t: (0,) * a.ndim)
    tok = pl.BlockSpec((1, tm, d), lambda b, t: (b, t, 0))
    return pl.pallas_call(
        functools.partial(_merge_kernel, d=d),
        grid=(bsz, n_lat // tm),
        in_specs=[tok, tok,
                  pl.BlockSpec((4, 1, tm, d), lambda b, t: (0, b, t, 0)),
                  tok, tok, wfull(mod),
                  pl.BlockSpec((1, HEAD_DIM), lambda b, t: (0, 0)),
                  wfull(w_pa), wfull(w_pb), wfull(w_out),
                  pl.BlockSpec((1, d), lambda b, t: (0, 0))],
        out_specs=tok,
        out_shape=jax.ShapeDtypeStruct((bsz, n_lat, d), F32),
        compiler_params=_cparams(("arbitrary", "arbitrary")),
        name="merge",
    )(o_f, o_b, gp, hy, x, mod, dn_norm_g.reshape(1, HEAD_DIM), w_pa, w_pb, w_out, final_g.reshape(1, d))


def _position_features(n_lat):
    h1 = n_lat // FFT_INNER
    m = (np.arange(FFT_INNER)[:, None] + FFT_INNER * np.arange(h1)[None, :]).reshape(-1)
    lag = np.concatenate([m, n_lat - m]).astype(np.float64)
    lag[n_lat] = 0.0
    bands = (HY_EMB - 1) // 2
    t = (lag / (n_lat - 1))[:, None]
    wpos = (2.0 * math.pi / n_lat) * lag[:, None]
    fb = np.linspace(1e-4, bands - 1, bands)[None, :]
    pe = np.concatenate([t, np.cos(fb * wpos), -np.sin(fb * wpos)], axis=1)
    pe_pad = np.zeros((2 * n_lat, LANES), np.float32)
    pe_pad[:, :HY_EMB] = pe
    return pe_pad, np.ascontiguousarray(t, dtype=np.float32)


def kernel(x, c, ctx, c_ctx, w_mod, b_mod, norm_g, w_in, dn_conv_w, dn_a_log, dn_dt_bias, dn_norm_g,
           hy_conv_w, hy_conv_b, hy_f_w1, hy_f_b1, hy_f_w2, hy_f_b2, hy_f_w3, hy_f_b3, hy_f_wout,
           hy_f_freq, hy_bias, w_pa, w_pb, w_out, final_g):
    bsz, n_lat, d = x.shape
    n_ctx = ctx.shape[1]
    assert w_mod.shape[0] == 1, "single layer: the context stream is only read through its scan states"
    dn = HEADS * HEAD_DIM
    hy = hy_bias.shape[-1]
    assert n_lat % 256 == 0 and n_ctx % 256 == 0 and dn == d and hy == d

    cvec = jnp.zeros((SUBLANES, d), F32).at[:bsz].set(c).at[bsz].set(c_ctx)
    mod = _mod_call(cvec, w_mod[0], b_mod[0])

    w = w_in[0].astype(BF16)
    o_qkv, o_gate, o_hy, o_gp = 0, 4 * dn, 4 * dn + 4 * HEADS, 4 * dn + 4 * HEADS + 3 * hy
    col3 = lambda a, off, n: jnp.transpose(a[:, off:off + 3 * n].reshape(a.shape[0], 3, n), (1, 0, 2))
    lt = n_lat + n_ctx
    tm = PROJ_TILE if n_lat % PROJ_TILE == 0 else 256

    wg = jnp.zeros((d, LANES), BF16).at[:, :4 * HEADS].set(w[:, o_gate:o_gate + 4 * HEADS])
    lane_pad = lambda a: jnp.zeros((1, LANES), F32).at[0, 2 * HEADS:4 * HEADS].set(a.reshape(-1))
    gate_args = (wg, lane_pad(dn_a_log[0]), lane_pad(dn_dt_bias[0]))
    h_tok, gates = _hnorm_call(x, mod, norm_g[0], *gate_args, tm=NORM_TILE, gate_rows=lt, gate_tile0=0, name="hnorm")
    h_ctx, gates = _hnorm_call(ctx, mod, norm_g[0], *gate_args, tm=n_ctx, gate_rows=lt, gate_tile0=n_lat // n_ctx,
                               mod_row=bsz, prev_gates=gates, name="hnorm_ctx")
    h1 = n_lat // FFT_INNER
    h_perm = jnp.transpose(h_tok.reshape(bsz, h1, FFT_INNER, d), (0, 2, 1, 3)).reshape(bsz, n_lat, d)

    w_qkv, cw_qkv = col3(w, o_qkv, dn), col3(dn_conv_w[0], 0, dn)
    qkv = _convproj_call(h_tok, w_qkv, cw_qkv, rowlen=GRID_W, tm=tm, tile0=0, out_rows=lt, name="qkv")
    qkv = _convproj_call(h_ctx, w_qkv, cw_qkv, rowlen=n_ctx, tm=n_ctx, tile0=n_lat // n_ctx, out_rows=lt,
                         prev=qkv, name="qkv_ctx")
    gates_t = jnp.transpose(gates[:, :, :4 * HEADS].reshape(bsz, lt // CHUNK, CHUNK, 4 * HEADS), (0, 1, 3, 2))
    tall, wide, el = _prep_call(qkv, gates, gates_t)
    o_f, o_b = _scan_call(tall, wide, el, n_lat, n_ctx)

    hyp = _hyproj_call(h_perm, col3(w, o_hy, hy), col3(hy_conv_w[0], 0, hy), hy_conv_b[0].reshape(3, 1, hy))
    w4 = jnp.stack([w[:, 3 * dn:4 * dn]] + [w[:, o_gp + i * d:o_gp + (i + 1) * d] for i in range(3)])
    gp = _gproj_call(h_tok, w4, n_lat)

    pe, tcol = _position_features(n_lat)
    w1p = jnp.zeros((LANES, hy_f_w1.shape[-1]), F32).at[:HY_EMB].set(hy_f_w1[0])
    pe2 = jnp.asarray(np.concatenate([pe[:n_lat], pe[n_lat:]], axis=1))
    a3 = _hidden_call(pe2, w1p, hy_f_b1[0], hy_f_w2[0], hy_f_b2[0], hy_f_w3[0], hy_f_b3[0], hy_f_freq[0])
    deltas = np.abs(np.linspace(math.log(HY_DECAY_TARGET) / HY_SLOW_DECAY,
                                math.log(HY_DECAY_TARGET) / HY_FAST_DECAY, hy)).astype(np.float32)
    deltas2 = jnp.asarray(np.tile(deltas, 2)[None, :])
    wside = hy_f_wout[0].reshape(-1, 2, 2 * hy).transpose(1, 0, 2).astype(BF16)
    zside = jnp.zeros_like(wside[0])
    wout2 = jnp.stack([jnp.concatenate([wside[0], zside]), jnp.concatenate([zside, wside[1]])])
    hfull, ssq = _ftime_call(a3, jnp.asarray(tcol), wout2, deltas2, n_lat)
    e1, e4, ef, g2, g3 = (jnp.asarray(t).astype(BF16) for t in _fft_tables(n_lat))
    hf = _ffft_call(hfull, ssq, ef, g2, n_lat)

    yh = _hyena_call(hyp, hf, e1, e4, g2, g3, hy_bias[0])
    yh = jnp.transpose(yh, (0, 2, 1, 3)).reshape(bsz, n_lat, hy)

    return _merge_call(o_f, o_b, gp, yh, x, mod, dn_norm_g[0], w_pa[0].astype(BF16), w_pb[0].astype(BF16),
                       w_out[0].astype(BF16), final_g)
```

```python
import functools
import math

import numpy as np
import jax
import jax.numpy as jnp
from jax import lax
from jax.experimental import pallas as pl
from jax.experimental.pallas import tpu as pltpu

F32 = jnp.float32
BF16 = jnp.bfloat16
HIGHEST = lax.Precision.HIGHEST

EPS = 1e-6
HEADS = 8
HEAD_DIM = 128
CHUNK = 64
GRID_W = 64
HY_EMB = 33
HY_DECAY_TARGET = 1e-2
HY_FAST_DECAY = 0.3
HY_SLOW_DECAY = 1.5

LANES = 128
SUBLANES = 8
FFT_INNER = 128
CONV_FFT_STEPS = 8
FILTER_FFT_STEPS = 4
FFT_CHANNELS = 256
FFT_INTERLEAVE = 8
PROJ_TILE = 2048
NORM_TILE = 1024
MERGE_TILE = 512
SCAN_CHUNKS = 4
PREP_CHUNKS = 1
HYPROJ_N1 = 32
MXU_TILE = 256
SLAB_PAD = 8
VMEM_LIMIT = 56 * 1024 * 1024
VMEM_LIMIT_CONV = 62 * 1024 * 1024


def _cparams(sem, vmem=VMEM_LIMIT):
    return pltpu.CompilerParams(dimension_semantics=sem, vmem_limit_bytes=vmem)


def _bdot(a, b):
    return jnp.dot(a.astype(BF16), b.astype(BF16), preferred_element_type=F32)


def _silu(x):
    return x * jax.nn.sigmoid(x)


def _mod_kernel(c_ref, w_ref, b_ref, o_ref):
    s = _silu(c_ref[...])
    o_ref[...] = jnp.dot(s, w_ref[...], precision=HIGHEST, preferred_element_type=F32) + b_ref[...]


def _mod_call(cvec, w_mod, b_mod):
    rows, d = cvec.shape
    n = w_mod.shape[1]
    tn = 1024
    return pl.pallas_call(
        _mod_kernel,
        grid=(n // tn,),
        in_specs=[pl.BlockSpec((rows, d), lambda j: (0, 0)),
                  pl.BlockSpec((d, tn), lambda j: (0, j)),
                  pl.BlockSpec((1, tn), lambda j: (0, j))],
        out_specs=pl.BlockSpec((rows, tn), lambda j: (0, j)),
        out_shape=jax.ShapeDtypeStruct((rows, n), F32),
        compiler_params=_cparams(("arbitrary",)),
        name="mod",
    )(cvec, w_mod, b_mod.reshape(1, n))


def _norm_rows(x, m, g_ref, d):
    y = x * lax.rsqrt(jnp.mean(x * x, axis=-1, keepdims=True) + EPS) * g_ref[...]
    return (y * (1.0 + m[:, d:2 * d]) + m[:, 0:d]).astype(BF16)


def _gate_rows(h, w_ref, alog_ref, dtb_ref):
    z = jnp.dot(h, w_ref[...], preferred_element_type=F32)
    lane = lax.broadcasted_iota(jnp.int32, z.shape, 1)
    u = z + dtb_ref[...]
    softplus = jnp.maximum(u, 0.0) + jnp.log(1.0 + jnp.exp(-jnp.abs(u)))
    return jnp.where(lane < 2 * HEADS, jax.nn.sigmoid(z), -jnp.exp(alog_ref[...]) * softplus)


def _hnorm_kernel(tok_ref, mod_ref, g_ref, w_ref, alog_ref, dtb_ref, *rest, d, mod_row):
    o_ref, og_ref = rest[-2:]
    row = pl.program_id(0) if mod_row is None else mod_row
    m = mod_ref[pl.ds(row, 1), :]
    h = _norm_rows(tok_ref[0], m, g_ref, d)
    o_ref[0] = h
    og_ref[0] = _gate_rows(h, w_ref, alog_ref, dtb_ref)


def _hnorm_call(tok, mod, norm_g, wg, alog, dtb, *, tm, gate_rows, gate_tile0, mod_row=None, prev_gates=None, name):
    bsz, rows, d = tok.shape
    full = lambda a: pl.BlockSpec(a.shape, lambda b, t: (0,) * a.ndim)
    args = [tok, mod, norm_g.reshape(1, d), wg, alog, dtb]
    in_specs = [pl.BlockSpec((1, tm, d), lambda b, t: (b, t, 0))] + [full(a) for a in args[1:]]
    aliases = {}
    if prev_gates is not None:
        in_specs.append(pl.BlockSpec(memory_space=pl.ANY))
        args.append(prev_gates)
        aliases = {6: 1}
    return pl.pallas_call(
        functools.partial(_hnorm_kernel, d=d, mod_row=mod_row),
        grid=(bsz, rows // tm),
        in_specs=in_specs,
        out_specs=[pl.BlockSpec((1, tm, d), lambda b, t: (b, t, 0)),
                   pl.BlockSpec((1, tm, LANES), lambda b, t: (b, t + gate_tile0, 0))],
        out_shape=[jax.ShapeDtypeStruct((bsz, rows, d), BF16), jax.ShapeDtypeStruct((bsz, gate_rows, LANES), F32)],
        input_output_aliases=aliases,
        compiler_params=_cparams(("arbitrary", "arbitrary")),
        name=name,
    )(*args)


@functools.lru_cache(maxsize=None)
def _shift_matrix(rowlen, taps, blk):
    t = np.arange(blk)
    mats = []
    for j in range(taps):
        d = j - taps // 2
        if d != 0:
            ok = ((t + d) >= 0) & ((t + d) < blk) & (t // rowlen == (t + d) // rowlen)
            s = np.zeros((blk, blk), np.float32)
            s[t[ok], t[ok] + d] = 1.0
            mats.append(s)
    return np.concatenate(mats, axis=1)


def _shift_block(rowlen, taps):
    return max(rowlen, MXU_TILE // (taps - 1) // rowlen * rowlen)


def _convproj_kernel(h_ref, w_ref, cw_ref, sh_ref, *rest, tm):
    o_ref = rest[-1]
    j = pl.program_id(0)
    taps = cw_ref.shape[1]
    mid = taps // 2
    blk = sh_ref.shape[0]
    mblk = max(blk, min(tm, 2 * MXU_TILE))
    na = jnp.where(j == 0, float(HEAD_DIM), jnp.where(j == 1, 1.0, 0.0))
    nb = jnp.where(j == 0, EPS * HEAD_DIM, jnp.where(j == 1, EPS, 1.0))
    cw16 = cw_ref[0].astype(BF16)
    for r in range(tm // mblk):
        zm = jnp.dot(h_ref[0, r * mblk:(r + 1) * mblk, :], w_ref[0], preferred_element_type=F32)
        for g in range(mblk // blk):
            rows = slice(r * mblk + g * blk, r * mblk + (g + 1) * blk)
            zb = zm[g * blk:(g + 1) * blk]
            zb16 = zb.astype(BF16)
            side = jnp.concatenate([zb16 * cw16[jt:jt + 1, :] for jt in range(taps) if jt != mid], axis=0)
            y = zb * cw_ref[0, mid:mid + 1, :] + jnp.dot(sh_ref[...], side, preferred_element_type=F32)
            y = _silu(y)
            for hh in range(HEADS):
                yh = y[:, hh * HEAD_DIM:(hh + 1) * HEAD_DIM]
                nrm = jnp.sum(yh * yh, axis=-1, keepdims=True)
                o_ref[0, 0, rows, hh * HEAD_DIM:(hh + 1) * HEAD_DIM] = (
                    yh * lax.rsqrt(nrm * na + nb)).astype(o_ref.dtype)


def _convproj_call(h, w3, cw3, *, rowlen, tm, tile0, out_rows, prev=None, name):
    bsz, rows, d = h.shape
    ntiles = rows // tm
    nw = w3.shape[2]
    taps = cw3.shape[1]
    blk = _shift_block(rowlen, taps)
    assert tm % max(blk, MXU_TILE) == 0 and max(blk, MXU_TILE) % blk == 0
    sh = jnp.asarray(_shift_matrix(rowlen, taps, blk)).astype(BF16)
    in_specs = [pl.BlockSpec((1, tm, d), lambda j, b, t: (b, t, 0)),
                pl.BlockSpec((1, d, nw), lambda j, b, t: (j, 0, 0)),
                pl.BlockSpec((1, cw3.shape[1], nw), lambda j, b, t: (j, 0, 0)),
                pl.BlockSpec(sh.shape, lambda j, b, t: (0, 0))]
    args = [h, w3, cw3, sh]
    aliases = {}
    if prev is not None:
        in_specs.append(pl.BlockSpec(memory_space=pl.ANY))
        args.append(prev)
        aliases = {4: 0}
    return pl.pallas_call(
        functools.partial(_convproj_kernel, tm=tm),
        grid=(3, bsz, ntiles),
        in_specs=in_specs,
        out_specs=pl.BlockSpec((1, 1, tm, nw), lambda j, b, t: (j, b, t + tile0, 0)),
        out_shape=jax.ShapeDtypeStruct((3, bsz, out_rows, nw), BF16),
        input_output_aliases=aliases,
        compiler_params=_cparams(("arbitrary", "arbitrary", "arbitrary")),
        name=name,
    )(*args)


def _hyproj_kernel(h_ref, w_ref, cw_ref, cb_ref, o_ref):
    n2s, n1s, d = h_ref.shape[1:]
    taps = cw_ref.shape[1]
    hrows = h_ref[0].reshape(n2s * n1s, d)
    step = min(MXU_TILE, n2s * n1s)
    z = jnp.concatenate([jnp.dot(hrows[r:r + step], w_ref[0], preferred_element_type=F32)
                         for r in range(0, n2s * n1s, step)], axis=0)
    y = cb_ref[0] + z * cw_ref[0, taps // 2:taps // 2 + 1, :]
    for jt in range(taps):
        s = (jt - taps // 2) * n1s
        if s > 0:
            y = y + jnp.concatenate([z[s:], jnp.zeros((s, z.shape[1]), F32)], axis=0) * cw_ref[0, jt:jt + 1, :]
        elif s < 0:
            y = y + jnp.concatenate([jnp.zeros((-s, z.shape[1]), F32), z[:s]], axis=0) * cw_ref[0, jt:jt + 1, :]
    o_ref[0, 0] = y.astype(o_ref.dtype).reshape(n2s, n1s, y.shape[1])


def _hyproj_call(h_perm, w3, cw3, cb3):
    bsz, n_lat, d = h_perm.shape
    nw = w3.shape[2]
    h1 = n_lat // FFT_INNER
    n1s = min(HYPROJ_N1, h1)
    hv = h_perm.reshape(bsz, FFT_INNER, h1, d)
    return pl.pallas_call(
        _hyproj_kernel,
        grid=(3, bsz, FFT_INNER // GRID_W, h1 // n1s),
        in_specs=[pl.BlockSpec((1, GRID_W, n1s, d), lambda j, b, r, t: (b, r, t, 0)),
                  pl.BlockSpec((1, d, nw), lambda j, b, r, t: (j, 0, 0)),
                  pl.BlockSpec((1, cw3.shape[1], nw), lambda j, b, r, t: (j, 0, 0)),
                  pl.BlockSpec((1, 1, nw), lambda j, b, r, t: (j, 0, 0))],
        out_specs=pl.BlockSpec((1, 1, GRID_W, n1s, nw), lambda j, b, r, t: (j, b, r, t, 0)),
        out_shape=jax.ShapeDtypeStruct((3, bsz, FFT_INNER, h1, nw), BF16),
        compiler_params=_cparams(("arbitrary",) * 4),
        name="hyproj",
    )(hv, w3, cw3, cb3)


def _gproj_kernel(h_ref, w_ref, o_ref):
    j = pl.program_id(0)
    for r in range(h_ref.shape[1] // MXU_TILE):
        rows = slice(r * MXU_TILE, (r + 1) * MXU_TILE)
        z = jnp.dot(h_ref[0, rows, :], w_ref[0], preferred_element_type=F32)
        s = jax.nn.sigmoid(z)
        o_ref[0, 0, rows, :] = jnp.where(j < 2, z * s, s).astype(BF16)


def _gproj_call(h, w4, n_lat):
    bsz, _, d = h.shape
    nw = w4.shape[2]
    tm = PROJ_TILE if n_lat % PROJ_TILE == 0 else 256
    return pl.pallas_call(
        _gproj_kernel,
        grid=(4, bsz, n_lat // tm),
        in_specs=[pl.BlockSpec((1, tm, d), lambda j, b, t: (b, t, 0)),
                  pl.BlockSpec((1, d, nw), lambda j, b, t: (j, 0, 0))],
        out_specs=pl.BlockSpec((1, 1, tm, nw), lambda j, b, t: (j, b, t, 0)),
        out_shape=jax.ShapeDtypeStruct((4, bsz, n_lat, nw), BF16),
        compiler_params=_cparams(("arbitrary", "arbitrary", "arbitrary")),
        name="gproj",
    )(h, w4)


def _unit_tri_inverse(ls, lowers, filler):
    c = ls[0].shape[0]
    ri = lax.broadcasted_iota(jnp.int32, (c, c), 0)
    ci = lax.broadcasted_iota(jnp.int32, (c, c), 1)
    same = lambda n: (ri // n) == (ci // n)
    eye = jnp.where(ri == ci, 1.0, 0.0)
    in2 = same(2)
    ts = [eye - jnp.where(in2, l, 0.0) for l in ls]
    n = 2
    nlevels = c.bit_length() - 2
    level = 0
    while n < c:
        blk = same(2 * n) & jnp.logical_not(same(n))
        offs = [jnp.where(blk, l, 0.0).astype(BF16) for l in ls]
        if n % SUBLANES:
            prods = [_bdot(t, off) for t, off in zip(ts, offs)]
            filler(level, nlevels)
            ts = [t - _bdot(p, t) for p, t in zip(prods, ts)]
        else:
            def moving(t, lower):
                first = n if lower else 0
                return jnp.concatenate([t[r:r + n] for r in range(first, c, 2 * n)], axis=0)

            def merged(t, upd, lower):
                pieces, k = [], 0
                for r in range(0, c, n):
                    if ((r // n) % 2 == 1) == lower:
                        pieces.append(upd[k * n:(k + 1) * n])
                        k += 1
                    else:
                        pieces.append(t[r:r + n])
                return jnp.concatenate(pieces, axis=0)

            rows = [moving(t, lo) for t, lo in zip(ts, lowers)]
            prods = [_bdot(r, off) for r, off in zip(rows, offs)]
            filler(level, nlevels)
            upds = [r - _bdot(p, t) for r, p, t in zip(rows, prods, ts)]
            ts = [merged(t, u, lo) for t, u, lo in zip(ts, upds, lowers)]
        n *= 2
        level += 1
    return ts


_NT = (((1,), (1,)), ((), ()))
_TN = (((0,), (0,)), ((), ()))


def _prep_kernel(qkv_ref, g_ref, gt_ref, tall_ref, wide_ref, el_ref, *, bsz):
    c = CHUNK
    ri = lax.broadcasted_iota(jnp.int32, (c, c), 0)
    ci = lax.broadcasted_iota(jnp.int32, (c, c), 1)
    tri_l = jnp.where(ri >= ci, 1.0, 0.0)
    tri_u = jnp.where(ri <= ci, 1.0, 0.0)
    incl = (ri >= ci, ri <= ci)
    strict = (ri > ci, ri < ci)
    hdot = functools.partial(jnp.dot, precision=HIGHEST, preferred_element_type=F32)

    per = tall_ref.shape[0]
    cb = [(ch, b) for ch in range(per) for b in range(bsz)]
    rows = lambda ch: slice(ch * c, (ch + 1) * c)
    gs = [g_ref[b, rows(ch), :] for ch, b in cb]
    gts = [gt_ref[b, ch] for ch, b in cb]
    gcol = [(hdot(tri_l, g), hdot(tri_u, g)) for g in gs]
    grow = [(hdot(gt, tri_u), hdot(gt, tri_l)) for gt in gts]
    glast = [(gc[0][c - 1:c], gc[1][0:1]) for gc in gcol]
    el_ref[...] = jnp.zeros_like(el_ref)
    for p, (ch, b) in enumerate(cb):
        for d in range(2):
            el_ref[ch, d * bsz + b:d * bsz + b + 1, :] = jnp.exp(glast[p][d])

    pbh = [(p, hh) for p in range(len(cb)) for hh in range(HEADS)]
    hs = lambda hh: slice(hh * HEAD_DIM, (hh + 1) * HEAD_DIM)
    tok = lambda w, p, hh: qkv_ref[w, cb[p][1], rows(cb[p][0]), hs(hh)]
    ks = [tok(1, p, hh) for p, hh in pbh]
    qs = [tok(0, p, hh) for p, hh in pbh]
    kk = [lax.dot_general(k, k, _NT, preferred_element_type=F32) for k in ks]
    qk = [lax.dot_general(q, k, _NT, preferred_element_type=F32) for q, k in zip(qs, ks)]
    inst = [(d, p, hh, i) for d in range(2) for i, (p, hh) in enumerate(pbh)]
    decs, betas, gcs = [], [], []
    for d, p, hh, _ in inst:
        lg = 2 * HEADS + d * HEADS + hh
        gcs.append(gcol[p][d][:, lg:lg + 1])
        betas.append(jnp.broadcast_to(gs[p][:, d * HEADS + hh:d * HEADS + hh + 1], (c, HEAD_DIM)))
        decs.append(jnp.exp(jnp.where(incl[d], gcs[-1] - grow[p][d][lg:lg + 1, :], -jnp.inf)))
    ls = [jnp.where(strict[d], kk[i] * beta[:, 0:c] * dec, 0.0)
          for (d, _, _, i), beta, dec in zip(inst, betas, decs)]

    def scaled_operands(part, nparts):
        for (d, p, hh, i), dec, beta, gc in list(zip(inst, decs, betas, gcs))[part::nparts]:
            ch, b = cb[p]
            lg = 2 * HEADS + d * HEADS + hh
            eg = jnp.broadcast_to(jnp.exp(gc), (c, HEAD_DIM))
            k = ks[i].astype(F32)
            tall_ref[ch, d, b, hh, :, c:2 * c] = (qk[i] * dec).astype(BF16)
            wide_ref[ch, d, b, hh, 0:c, :] = (qs[i].astype(F32) * eg).astype(BF16)
            wide_ref[ch, d, b, hh, c:2 * c, :] = (k * beta * eg).astype(BF16)
            wide_ref[ch, d, b, hh, 2 * c:3 * c, :] = (tok(2, p, hh).astype(F32) * beta).astype(BF16)
            wide_ref[ch, d, b, hh, 3 * c:4 * c, :] = (k * jnp.exp(glast[p][d][:, lg:lg + 1] - gc)).astype(BF16)

    ts = _unit_tri_inverse(ls, [d == 0 for d, _, _, _ in inst], scaled_operands)
    for (d, p, hh, _), t in zip(inst, ts):
        tall_ref[cb[p][0], d, cb[p][1], hh, :, 0:c] = t.astype(BF16)


def _prep_call(qkv, gates, gates_t):
    _, bsz, lt, dn = qkv.shape
    nc = lt // CHUNK
    per = PREP_CHUNKS
    assert nc % per == 0
    tall = (2, bsz, HEADS, CHUNK, 2 * CHUNK)
    wide = (2, bsz, HEADS, 4 * CHUNK, HEAD_DIM)
    return pl.pallas_call(
        functools.partial(_prep_kernel, bsz=bsz),
        grid=(nc // per,),
        in_specs=[pl.BlockSpec((3, bsz, per * CHUNK, dn), lambda i: (0, 0, i, 0)),
                  pl.BlockSpec((bsz, per * CHUNK, LANES), lambda i: (0, i, 0)),
                  pl.BlockSpec((bsz, per, 4 * HEADS, CHUNK), lambda i: (0, i, 0, 0))],
        out_specs=[pl.BlockSpec((per,) + tall, lambda i: (i, 0, 0, 0, 0, 0)),
                   pl.BlockSpec((per,) + wide, lambda i: (i, 0, 0, 0, 0, 0)),
                   pl.BlockSpec((per, SUBLANES, LANES), lambda i: (i, 0, 0))],
        out_shape=[jax.ShapeDtypeStruct((nc,) + tall, BF16),
                   jax.ShapeDtypeStruct((nc,) + wide, BF16),
                   jax.ShapeDtypeStruct((nc, SUBLANES, LANES), F32)],
        compiler_params=_cparams(("arbitrary",)),
        name="prep",
    )(qkv, gates, gates_t)


def _scan_kernel(tf_ref, tb_ref, wf_ref, wb_ref, ef_ref, eb_ref, of_ref, ob_ref, s_ref, *, bsz):
    i = pl.program_id(0)

    @pl.when(i == 0)
    def _():
        s_ref[...] = jnp.zeros_like(s_ref)

    c = CHUNK
    per = tf_ref.shape[0]
    dirs = ((tf_ref, wf_ref, ef_ref, of_ref), (tb_ref, wb_ref, eb_ref, ob_ref))
    inst = [(d, b, hh) for d in range(2) for b in range(bsz) for hh in range(HEADS)]
    state = [s_ref[d, b, hh] for d, b, hh in inst]
    for sub in range(per):
        at = (sub, per - 1 - sub)
        a1 = [jnp.dot(dirs[d][1][at[d], b, hh, 0:2 * c, :], s.astype(BF16), preferred_element_type=F32)
              for (d, b, hh), s in zip(inst, state)]
        v_new = [jnp.dot(dirs[d][0][at[d], b, hh, :, 0:c],
                         (dirs[d][1][at[d], b, hh, 2 * c:3 * c, :].astype(F32) - a[c:]).astype(BF16),
                         preferred_element_type=F32).astype(BF16) for (d, b, hh), a in zip(inst, a1)]
        new_state = []
        for (d, b, hh), a, vn, s in zip(inst, a1, v_new, state):
            lg = 2 * HEADS + d * HEADS + hh
            o = a[:c] + jnp.dot(dirs[d][0][at[d], b, hh, :, c:2 * c], vn, preferred_element_type=F32)
            dirs[d][3][b, at[d] * c:(at[d] + 1) * c, hh * HEAD_DIM:(hh + 1) * HEAD_DIM] = o.astype(BF16)
            el = dirs[d][2][at[d], d * bsz + b:d * bsz + b + 1, lg:lg + 1]
            new_state.append(s * el + lax.dot_general(dirs[d][1][at[d], b, hh, 3 * c:4 * c, :], vn, _TN,
                                                      preferred_element_type=F32))
        state = new_state
    for (d, b, hh), s in zip(inst, state):
        s_ref[d, b, hh] = s


def _scan_call(tall, wide, el, n_lat, n_ctx):
    nc, _, bsz = tall.shape[:3]
    dn = HEADS * HEAD_DIM
    per = SCAN_CHUNKS
    ncx, ncc, ns = n_lat // CHUNK // per, n_ctx // CHUNK // per, nc // per
    assert nc == (ncx + ncc) * per
    cf = lambda i: jnp.where(i < ncc, ncx + i, i - ncc)
    cb = lambda i: ns - 1 - i

    dspec = lambda arr, fn, d: pl.BlockSpec((per, None) + arr.shape[2:], lambda i: (fn(i), d, 0, 0, 0, 0))
    espec = lambda fn: pl.BlockSpec((per, SUBLANES, LANES), lambda i: (fn(i), 0, 0))
    ospec = lambda fn: pl.BlockSpec((bsz, per * CHUNK, dn), lambda i: (0, fn(i), 0))
    return pl.pallas_call(
        functools.partial(_scan_kernel, bsz=bsz),
        grid=(ns,),
        in_specs=[dspec(tall, cf, 0), dspec(tall, cb, 1), dspec(wide, cf, 0), dspec(wide, cb, 1),
                  espec(cf), espec(cb)],
        out_specs=[ospec(lambda i: jnp.maximum(i - ncc, 0)), ospec(lambda i: jnp.minimum(ns - 1 - i, ncx - 1))],
        out_shape=[jax.ShapeDtypeStruct((bsz, n_lat, dn), BF16)] * 2,
        scratch_shapes=[pltpu.VMEM((2, bsz, HEADS, HEAD_DIM, HEAD_DIM), F32)],
        compiler_params=_cparams(("arbitrary",)),
        name="scan",
    )(tall, tall, wide, wide, el, el)


def _hidden_kernel(pe_ref, w1_ref, b1_ref, w2_ref, b2_ref, w3_ref, b3_ref, f_ref, o_ref):
    f = f_ref[...]
    dot = functools.partial(jnp.dot, precision=HIGHEST, preferred_element_type=F32)
    a = jnp.sin(f * (dot(pe_ref[...], w1_ref[...]) + b1_ref[...]))
    a = jnp.sin(f * (dot(a, w2_ref[...]) + b2_ref[...]))
    o_ref[...] = jnp.sin(f * (dot(a, w3_ref[...]) + b3_ref[...]))


def _hidden_call(pe, w1p, b1, w2, b2, w3, b3, freq):
    rows, pw = pe.shape
    fw = 2 * w2.shape[0]
    tr = 512
    full = lambda a: pl.BlockSpec(a.shape, lambda t: (0,) * a.ndim)
    two = lambda a: jnp.tile(a.reshape(1, -1), (1, 2))
    diag2 = lambda a: jnp.kron(jnp.eye(2, dtype=a.dtype), a)
    args = (diag2(w1p), two(b1), diag2(w2), two(b2), diag2(w3), two(b3), two(freq))
    return pl.pallas_call(
        _hidden_kernel,
        grid=(rows // tr,),
        in_specs=[pl.BlockSpec((tr, pw), lambda t: (t, 0))] + [full(a) for a in args],
        out_specs=pl.BlockSpec((tr, fw), lambda t: (t, 0)),
        out_shape=jax.ShapeDtypeStruct((rows, fw), F32),
        compiler_params=_cparams(("arbitrary",)),
        name="hidden",
    )(pe, *args)


def _ftime_kernel(a_ref, t_ref, w_ref, dl_ref, h_ref, ss_ref, *, n_lat, tr):
    t = pl.program_id(0)
    h = _bdot(a_ref[...], w_ref[...]) * jnp.exp(-t_ref[...] * dl_ref[...])
    row = lax.broadcasted_iota(jnp.int32, (tr, 1), 0) + t * tr
    h = jnp.where(row == n_lat, 0.0, h)
    h_ref[...] = h.astype(h_ref.dtype)

    @pl.when(t == 0)
    def _():
        ss_ref[...] = jnp.zeros_like(ss_ref)

    ss_ref[...] += jnp.sum(h * h, axis=0, keepdims=True)


def _ftime_call(a3, tcol, wout2, deltas2, n_lat):
    rows = tcol.shape[0]
    fw = a3.shape[1]
    ncol = wout2.shape[2]
    tr = 512
    side = lambda t: (t * tr) // n_lat
    return pl.pallas_call(
        functools.partial(_ftime_kernel, n_lat=n_lat, tr=tr),
        grid=(rows // tr,),
        in_specs=[pl.BlockSpec((tr, fw), lambda t: (t % (n_lat // tr), 0)),
                  pl.BlockSpec((tr, 1), lambda t: (t, 0)),
                  pl.BlockSpec((None, fw, ncol), lambda t: (side(t), 0, 0)),
                  pl.BlockSpec((1, ncol), lambda t: (0, 0))],
        out_specs=[pl.BlockSpec((tr, ncol), lambda t: (t, 0)),
                   pl.BlockSpec((1, ncol), lambda t: (0, 0))],
        out_shape=[jax.ShapeDtypeStruct((rows, ncol), BF16), jax.ShapeDtypeStruct((1, ncol), F32)],
        compiler_params=_cparams(("arbitrary",)),
        name="ftime",
    )(a3, tcol, wout2, deltas2)


@functools.lru_cache(maxsize=None)
def _fft_tables(n_lat):
    n = 2 * n_lat
    n1t = n // FFT_INNER
    h1 = n1t // 2
    n2 = np.arange(FFT_INNER, dtype=np.int64)[:, None, None]
    k1 = np.arange(n1t, dtype=np.int64)[None, :, None]
    n1 = np.arange(h1, dtype=np.int64)[None, None, :]
    ang = (-2.0 * np.pi / n) * ((k1 * (FFT_INNER * n1 + n2)) % n)
    er, ei = np.cos(ang), np.sin(ang)
    e1 = np.concatenate([np.concatenate([er, -ei], axis=2), np.concatenate([ei, er], axis=2)], axis=1)
    e4 = np.transpose(e1, (0, 2, 1))
    sgn = np.where(np.arange(n1t) % 2 == 0, 1.0, -1.0)[None, :, None]
    ef = np.concatenate([np.concatenate([er, sgn * er], axis=2), np.concatenate([ei, sgn * ei], axis=2)], axis=1)
    a = np.arange(FFT_INNER, dtype=np.int64)
    ang2 = (-2.0 * np.pi / FFT_INNER) * ((a[:, None] * a[None, :]) % FFT_INNER)
    gr, gi = np.cos(ang2), np.sin(ang2)
    g2 = np.block([[gr, -gi], [gi, gr]])
    f32 = lambda x: np.ascontiguousarray(x, dtype=np.float32)
    return f32(e1), f32(e4), f32(ef), f32(g2), f32(g2.T)


def _hold(p, j, first, last, nsteps):
    active = (p >= first) & (p <= last) & (lax.rem(p - first, 2) == 0)
    return jnp.where(active, j, jnp.where(p < first, 0, nsteps - 1))


def _slab_read(x_ref, rows):
    return jnp.concatenate([x_ref[c, rows, :] for c in range(x_ref.shape[0])], axis=1)


def _slab_write(x_ref, rows, val):
    for c in range(x_ref.shape[0]):
        x_ref[c, rows, :] = val[:, c * LANES:(c + 1) * LANES]


def _spectrum_steps(x_ref, g2_ref, k1s, n1t, stride):
    cols = [jnp.concatenate([_slab_read(x_ref, pl.ds(k1, FFT_INNER, stride=stride)),
                             _slab_read(x_ref, pl.ds(n1t + k1, FFT_INNER, stride=stride))], axis=0).astype(BF16)
            for k1 in k1s]
    return [jnp.dot(g2_ref[...], c, preferred_element_type=F32) for c in cols]


def _ffft_kernel(h_ref, ef_ref, g2_ref, ss_ref, o_ref, x_ref, *, n1t, grp, kgrp, stride, inv_n):
    p = pl.program_id(1)
    j = pl.program_id(2)

    @pl.when(p == 0)
    def _():
        for j0 in range(0, grp, FFT_INTERLEAVE):
            jjs = list(range(j0, min(j0 + FFT_INTERLEAVE, grp)))
            us = [jnp.concatenate([h_ref[0, jj], h_ref[1, jj]], axis=0) for jj in jjs]
            slabs = [_bdot(ef_ref[jj], u) for jj, u in zip(jjs, us)]
            for jj, slab in zip(jjs, slabs):
                base = pl.multiple_of((j * grp + jj) * stride, SUBLANES)
                _slab_write(x_ref, pl.ds(base, 2 * n1t), slab)

    @pl.when(p == 1)
    def _():
        scale = lax.rsqrt(ss_ref[...] + EPS) * inv_n
        for j0 in range(0, kgrp, FFT_INTERLEAVE):
            jjs = list(range(j0, min(j0 + FFT_INTERLEAVE, kgrp)))
            xfs = _spectrum_steps(x_ref, g2_ref, [j * kgrp + jj for jj in jjs], n1t, stride)
            for jj, xf in zip(jjs, xfs):
                o_ref[jj] = (xf * scale).astype(BF16)


def _fft_groups(n1t, ns):
    ns = min(ns, n1t)
    assert FFT_INNER % ns == 0 and n1t % ns == 0
    return ns, FFT_INNER // ns, n1t // ns


def _ffft_call(hfull, ssq, ef, g2, n_lat):
    ncol = hfull.shape[1]
    n1t = 2 * n_lat // FFT_INNER
    h1 = n1t // 2
    ct = FFT_CHANNELS
    ns, grp, kgrp = _fft_groups(n1t, FILTER_FFT_STEPS)
    stride = 2 * n1t + SLAB_PAD
    hv = hfull.reshape(2, FFT_INNER, h1, ncol)
    nct = ncol // ct
    cpo = nct // 2
    return pl.pallas_call(
        functools.partial(_ffft_kernel, n1t=n1t, grp=grp, kgrp=kgrp, stride=stride, inv_n=1.0 / (2 * n_lat)),
        grid=(nct, 2, ns),
        in_specs=[pl.BlockSpec((2, grp, h1, ct), lambda c, p, j: (0, _hold(p, j, 0, 0, ns), 0, c)),
                  pl.BlockSpec((grp, 2 * n1t, n1t), lambda c, p, j: (_hold(p, j, 0, 0, ns), 0, 0)),
                  pl.BlockSpec((2 * FFT_INNER, 2 * FFT_INNER), lambda c, p, j: (0, 0)),
                  pl.BlockSpec((1, ct), lambda c, p, j: (0, c))],
        out_specs=pl.BlockSpec((None, kgrp, 2 * FFT_INNER, ct),
                               lambda c, p, j: (c // cpo, _hold(p, j, 1, 1, ns), 0, c % cpo)),
        out_shape=jax.ShapeDtypeStruct((2, n1t, 2 * FFT_INNER, ncol // 2), BF16),
        scratch_shapes=[pltpu.VMEM((ct // LANES, FFT_INNER * stride, LANES), F32)],
        compiler_params=_cparams(("arbitrary", "arbitrary", "arbitrary")),
        name="ffft",
    )(hv, ef, g2, ssq)


def _hyena_kernel(v_ref, x1_ref, x2_ref, hf_ref, e1_ref, e4_ref, g2_ref, g3_ref, bias_ref,
                  o_ref, x_ref, z_ref, *, n1t, grp, kgrp, stride):
    p = pl.program_id(1)
    j = pl.program_id(2)
    h1 = n1t // 2

    def gather(ref, jj):
        return jnp.concatenate([ref[0, jj], ref[1, jj]], axis=0).astype(F32)

    def slab_base(jj):
        return pl.multiple_of((j * grp + jj) * stride, SUBLANES)

    def time_rows(jjs):
        slabs = [_slab_read(x_ref, pl.ds(slab_base(jj), 2 * n1t)).astype(BF16) for jj in jjs]
        return [jnp.dot(e4_ref[jj], s, preferred_element_type=F32) for jj, s in zip(jjs, slabs)]

    def write_slabs(jjs, us):
        slabs = [jnp.dot(e1_ref[jj], u, preferred_element_type=F32) for jj, u in zip(jjs, us)]
        for jj, slab in zip(jjs, slabs):
            _slab_write(x_ref, pl.ds(slab_base(jj), 2 * n1t), slab)

    def chunks(n):
        return [list(range(j0, min(j0 + FFT_INTERLEAVE, n))) for j0 in range(0, n, FFT_INTERLEAVE)]

    @pl.when(p == 0)
    def _():
        for jjs in chunks(grp):
            write_slabs(jjs, [gather(v_ref, jj).astype(BF16) for jj in jjs])

    @pl.when((p == 1) | (p == 3))
    def _():
        for jjs in chunks(kgrp):
            k1s = [j * kgrp + jj for jj in jjs]
            xfs = _spectrum_steps(x_ref, g2_ref, k1s, n1t, stride)
            ys = []
            for jj, xf in zip(jjs, xfs):
                hf = hf_ref[jj].astype(F32)
                xr, xi = xf[:FFT_INNER], xf[FFT_INNER:]
                hr, hi = hf[:FFT_INNER], hf[FFT_INNER:]
                ys.append(jnp.concatenate([xr * hr - xi * hi, xr * hi + xi * hr], axis=0).astype(BF16))
            bbs = [jnp.dot(g3_ref[...], y, preferred_element_type=F32) for y in ys]
            for k1, bb in zip(k1s, bbs):
                _slab_write(x_ref, pl.ds(k1, FFT_INNER, stride=stride), bb[:FFT_INNER])
                _slab_write(x_ref, pl.ds(n1t + k1, FFT_INNER, stride=stride), bb[FFT_INNER:])

    @pl.when(p == 2)
    def _():
        for jjs in chunks(grp):
            z1s = [(gather(x1_ref, jj) * (y + gather(v_ref, jj) * bias_ref[0:1, :])).astype(BF16)
                   for jj, y in zip(jjs, time_rows(jjs))]
            for jj, z1 in zip(jjs, z1s):
                z_ref[j * grp + jj] = z1
            write_slabs(jjs, z1s)

    @pl.when(p == 4)
    def _():
        for jjs in chunks(grp):
            for jj, y in zip(jjs, time_rows(jjs)):
                out = gather(x2_ref, jj) * (y + z_ref[j * grp + jj].astype(F32) * bias_ref[1:2, :])
                o_ref[0, jj] = out[:h1].astype(o_ref.dtype)
                o_ref[1, jj] = out[h1:].astype(o_ref.dtype)


def _hyena_call(hv, hf, e1, e4, g2, g3, bias):
    _, bsz, _, h1, ch = hv.shape
    assert bsz == 2, "the two batch rows are packed as the real and imaginary parts of one FFT"
    n1t = 2 * h1
    ct = FFT_CHANNELS
    ns, grp, kgrp = _fft_groups(n1t, CONV_FFT_STEPS)
    stride = 2 * n1t + SLAB_PAD

    def hspec(which, first, last):
        return pl.BlockSpec((None, bsz, grp, h1, ct),
                            lambda c, p, j: (which, 0, _hold(p, j, first, last, ns), 0, c))

    def hf_map(c, p, j):
        order = jnp.where(p >= 2, 1, 0)
        step = jnp.where((p == 1) | (p == 3), j, jnp.where((p == 0) | (p == 2), 0, ns - 1))
        return (order, step, 0, c)

    return pl.pallas_call(
        functools.partial(_hyena_kernel, n1t=n1t, grp=grp, kgrp=kgrp, stride=stride),
        grid=(ch // ct, 5, ns),
        in_specs=[hspec(0, 0, 2), hspec(1, 2, 2), hspec(2, 4, 4),
                  pl.BlockSpec((None, kgrp, 2 * FFT_INNER, ct), hf_map),
                  pl.BlockSpec((grp, 2 * n1t, n1t), lambda c, p, j: (_hold(p, j, 0, 2, ns), 0, 0)),
                  pl.BlockSpec((grp, n1t, 2 * n1t), lambda c, p, j: (_hold(p, j, 2, 4, ns), 0, 0)),
                  pl.BlockSpec((2 * FFT_INNER, 2 * FFT_INNER), lambda c, p, j: (0, 0)),
                  pl.BlockSpec((2 * FFT_INNER, 2 * FFT_INNER), lambda c, p, j: (0, 0)),
                  pl.BlockSpec((2, ct), lambda c, p, j: (0, c))],
        out_specs=pl.BlockSpec((bsz, grp, h1, ct), lambda c, p, j: (0, _hold(p, j, 4, 4, ns), 0, c)),
        out_shape=jax.ShapeDtypeStruct((bsz, FFT_INNER, h1, ch), BF16),
        scratch_shapes=[pltpu.VMEM((ct // LANES, FFT_INNER * stride, LANES), F32),
                        pltpu.VMEM((FFT_INNER, n1t, ct), BF16)],
        compiler_params=_cparams(("arbitrary", "arbitrary", "arbitrary"), VMEM_LIMIT_CONV),
        name="hyena",
    )(hv, hv, hv, hf, e1, e4, g2, g3, bias)


def _merge_kernel(of_ref, ob_ref, gp_ref, hy_ref, x_ref, mod_ref, dng_ref, wpa_ref, wpb_ref, wo_ref,
                  fg_ref, o_ref, *, d):
    b = pl.program_id(0)
    blocks = [slice(r * MXU_TILE, (r + 1) * MXU_TILE) for r in range(o_ref.shape[1] // MXU_TILE)]
    f32 = lambda ref, *idx: ref[idx].astype(F32)

    def head_norm(rows):
        o = f32(of_ref, 0, rows) + f32(ob_ref, 0, rows)
        za = f32(gp_ref, 0, 0, rows)
        parts = []
        for hh in range(HEADS):
            hs = slice(hh * HEAD_DIM, (hh + 1) * HEAD_DIM)
            oh = o[:, hs]
            ms = jnp.mean(oh * oh, axis=-1, keepdims=True)
            parts.append((oh * lax.rsqrt(ms + EPS) * dng_ref[...] * za[:, hs]).astype(BF16))
        return jnp.concatenate(parts, axis=1)

    o_a = [head_norm(rows) for rows in blocks]
    y_a = [jnp.dot(a, wpa_ref[...], preferred_element_type=F32) for a in o_a]
    o_b = [(f32(hy_ref, 0, rows) * f32(gp_ref, 1, 0, rows)).astype(BF16) for rows in blocks]
    y_b = [jnp.dot(a, wpb_ref[...], preferred_element_type=F32) for a in o_b]
    m = [(f32(gp_ref, 2, 0, rows) * ya + f32(gp_ref, 3, 0, rows) * yb).astype(BF16)
         for rows, ya, yb in zip(blocks, y_a, y_b)]
    y = [jnp.dot(a, wo_ref[...], preferred_element_type=F32) for a in m]
    gate = mod_ref[pl.ds(b, 1), :][:, 2 * d:3 * d]
    for rows, yr in zip(blocks, y):
        xn = x_ref[0, rows, :] + gate * yr
        o_ref[0, rows, :] = xn * lax.rsqrt(jnp.mean(xn * xn, axis=-1, keepdims=True) + EPS) * fg_ref[...]


def _merge_call(o_f, o_b, gp, hy, x, mod, dn_norm_g, w_pa, w_pb, w_out, final_g):
    bsz, n_lat, d = x.shape
    tm = MERGE_TILE if n_lat % MERGE_TILE == 0 else 256
    wfull = lambda a: pl.BlockSpec(a.shape, lambda b, t: (0,) * a.ndim)
    tok = pl.BlockSpec((1, tm, d), lambda b, t: (b, t, 0))
    return pl.pallas_call(
        functools.partial(_merge_kernel, d=d),
        grid=(bsz, n_lat // tm),
        in_specs=[tok, tok,
                  pl.BlockSpec((4, 1, tm, d), lambda b, t: (0, b, t, 0)),
                  tok, tok, wfull(mod),
                  pl.BlockSpec((1, HEAD_DIM), lambda b, t: (0, 0)),
                  wfull(w_pa), wfull(w_pb), wfull(w_out),
                  pl.BlockSpec((1, d), lambda b, t: (0, 0))],
        out_specs=tok,
        out_shape=jax.ShapeDtypeStruct((bsz, n_lat, d), F32),
        compiler_params=_cparams(("arbitrary", "arbitrary")),
        name="merge",
    )(o_f, o_b, gp, hy, x, mod, dn_norm_g.reshape(1, HEAD_DIM), w_pa, w_pb, w_out, final_g.reshape(1, d))


def _position_features(n_lat):
    h1 = n_lat // FFT_INNER
    m = (np.arange(FFT_INNER)[:, None] + FFT_INNER * np.arange(h1)[None, :]).reshape(-1)
    lag = np.concatenate([m, n_lat - m]).astype(np.float64)
    lag[n_lat] = 0.0
    bands = (HY_EMB - 1) // 2
    t = (lag / (n_lat - 1))[:, None]
    wpos = (2.0 * math.pi / n_lat) * lag[:, None]
    fb = np.linspace(1e-4, bands - 1, bands)[None, :]
    pe = np.concatenate([t, np.cos(fb * wpos), -np.sin(fb * wpos)], axis=1)
    pe_pad = np.zeros((2 * n_lat, LANES), np.float32)
    pe_pad[:, :HY_EMB] = pe
    return pe_pad, np.ascontiguousarray(t, dtype=np.float32)


def kernel(x, c, ctx, c_ctx, w_mod, b_mod, norm_g, w_in, dn_conv_w, dn_a_log, dn_dt_bias, dn_norm_g,
           hy_conv_w, hy_conv_b, hy_f_w1, hy_f_b1, hy_f_w2, hy_f_b2, hy_f_w3, hy_f_b3, hy_f_wout,
           hy_f_freq, hy_bias, w_pa, w_pb, w_out, final_g):
    bsz, n_lat, d = x.shape
    n_ctx = ctx.shape[1]
    assert w_mod.shape[0] == 1, "single layer: the context stream is only read through its scan states"
    dn = HEADS * HEAD_DIM
    hy = hy_bias.shape[-1]
    assert n_lat % 256 == 0 and n_ctx % 256 == 0 and dn == d and hy == d

    cvec = jnp.zeros((SUBLANES, d), F32).at[:bsz].set(c).at[bsz].set(c_ctx)
    mod = _mod_call(cvec, w_mod[0], b_mod[0])

    w = w_in[0].astype(BF16)
    o_qkv, o_gate, o_hy, o_gp = 0, 4 * dn, 4 * dn + 4 * HEADS, 4 * dn + 4 * HEADS + 3 * hy
    col3 = lambda a, off, n: jnp.transpose(a[:, off:off + 3 * n].reshape(a.shape[0], 3, n), (1, 0, 2))
    lt = n_lat + n_ctx
    tm = PROJ_TILE if n_lat % PROJ_TILE == 0 else 256

    wg = jnp.zeros((d, LANES), BF16).at[:, :4 * HEADS].set(w[:, o_gate:o_gate + 4 * HEADS])
    lane_pad = lambda a: jnp.zeros((1, LANES), F32).at[0, 2 * HEADS:4 * HEADS].set(a.reshape(-1))
    gate_args = (wg, lane_pad(dn_a_log[0]), lane_pad(dn_dt_bias[0]))
    h_tok, gates = _hnorm_call(x, mod, norm_g[0], *gate_args, tm=NORM_TILE, gate_rows=lt, gate_tile0=0, name="hnorm")
    h_ctx, gates = _hnorm_call(ctx, mod, norm_g[0], *gate_args, tm=n_ctx, gate_rows=lt, gate_tile0=n_lat // n_ctx,
                               mod_row=bsz, prev_gates=gates, name="hnorm_ctx")
    h1 = n_lat // FFT_INNER
    h_perm = jnp.transpose(h_tok.reshape(bsz, h1, FFT_INNER, d), (0, 2, 1, 3)).reshape(bsz, n_lat, d)

    w_qkv, cw_qkv = col3(w, o_qkv, dn), col3(dn_conv_w[0], 0, dn)
    qkv = _convproj_call(h_tok, w_qkv, cw_qkv, rowlen=GRID_W, tm=tm, tile0=0, out_rows=lt, name="qkv")
    qkv = _convproj_call(h_ctx, w_qkv, cw_qkv, rowlen=n_ctx, tm=n_ctx, tile0=n_lat // n_ctx, out_rows=lt,
                         prev=qkv, name="qkv_ctx")
    gates_t = jnp.transpose(gates[:, :, :4 * HEADS].reshape(bsz, lt // CHUNK, CHUNK, 4 * HEADS), (0, 1, 3, 2))
    tall, wide, el = _prep_call(qkv, gates, gates_t)
    o_f, o_b = _scan_call(tall, wide, el, n_lat, n_ctx)

    hyp = _hyproj_call(h_perm, col3(w, o_hy, hy), col3(hy_conv_w[0], 0, hy), hy_conv_b[0].reshape(3, 1, hy))
    w4 = jnp.stack([w[:, 3 * dn:4 * dn]] + [w[:, o_gp + i * d:o_gp + (i + 1) * d] for i in range(3)])
    gp = _gproj_call(h_tok, w4, n_lat)

    pe, tcol = _position_features(n_lat)
    w1p = jnp.zeros((LANES, hy_f_w1.shape[-1]), F32).at[:HY_EMB].set(hy_f_w1[0])
    pe2 = jnp.asarray(np.concatenate([pe[:n_lat], pe[n_lat:]], axis=1))
    a3 = _hidden_call(pe2, w1p, hy_f_b1[0], hy_f_w2[0], hy_f_b2[0], hy_f_w3[0], hy_f_b3[0], hy_f_freq[0])
    deltas = np.abs(np.linspace(math.log(HY_DECAY_TARGET) / HY_SLOW_DECAY,
                                math.log(HY_DECAY_TARGET) / HY_FAST_DECAY, hy)).astype(np.float32)
    deltas2 = jnp.asarray(np.tile(deltas, 2)[None, :])
    wside = hy_f_wout[0].reshape(-1, 2, 2 * hy).transpose(1, 0, 2).astype(BF16)
    zside = jnp.zeros_like(wside[0])
    wout2 = jnp.stack([jnp.concatenate([wside[0], zside]), jnp.concatenate([zside, wside[1]])])
    hfull, ssq = _ftime_call(a3, jnp.asarray(tcol), wout2, deltas2, n_lat)
    e1, e4, ef, g2, g3 = (jnp.asarray(t).astype(BF16) for t in _fft_tables(n_lat))
    hf = _ffft_call(hfull, ssq, ef, g2, n_lat)

    yh = _hyena_call(hyp, hf, e1, e4, g2, g3, hy_bias[0])
    yh = jnp.transpose(yh, (0, 2, 1, 3)).reshape(bsz, n_lat, hy)

    return _merge_call(o_f, o_b, gp, yh, x, mod, dn_norm_g[0], w_pa[0].astype(BF16), w_pb[0].astype(BF16),
                       w_out[0].astype(BF16), final_g)
```

```python
import functools
import math

import numpy as np
import jax
import jax.numpy as jnp
from jax import lax
from jax.experimental import pallas as pl
from jax.experimental.pallas import tpu as pltpu

F32 = jnp.float32
BF16 = jnp.bfloat16
HIGHEST = lax.Precision.HIGHEST

EPS = 1e-6
HEADS = 8
HEAD_DIM = 128
CHUNK = 64
GRID_W = 64
HY_EMB = 33
HY_DECAY_TARGET = 1e-2
HY_FAST_DECAY = 0.3
HY_SLOW_DECAY = 1.5

LANES = 128
SUBLANES = 8
FFT_INNER = 128
CONV_FFT_STEPS = 8
FILTER_FFT_STEPS = 4
FFT_CHANNELS = 256
FFT_INTERLEAVE = 8
PROJ_TILE = 4096
NORM_TILE = 1024
FILTER_TILE = 1024
MERGE_TILE = 512
SCAN_CHUNKS = 4
PREP_CHUNKS = 1
HYPROJ_N1 = 32
MXU_TILE = 256
SLAB_PAD = 8
VMEM_LIMIT = 56 * 1024 * 1024
VMEM_LIMIT_CONV = 62 * 1024 * 1024


def _cparams(sem, vmem=VMEM_LIMIT):
    return pltpu.CompilerParams(dimension_semantics=sem, vmem_limit_bytes=vmem)


def _bdot(a, b):
    return jnp.dot(a.astype(BF16), b.astype(BF16), preferred_element_type=F32)


def _silu(x):
    return x * jax.nn.sigmoid(x)


def _mod_kernel(c_ref, w_ref, b_ref, o_ref):
    s = _silu(c_ref[...])
    o_ref[...] = jnp.dot(s, w_ref[...], precision=HIGHEST, preferred_element_type=F32) + b_ref[...]


def _mod_call(cvec, w_mod, b_mod):
    rows, d = cvec.shape
    n = w_mod.shape[1]
    tn = 1024
    return pl.pallas_call(
        _mod_kernel,
        grid=(n // tn,),
        in_specs=[pl.BlockSpec((rows, d), lambda j: (0, 0)),
                  pl.BlockSpec((d, tn), lambda j: (0, j)),
                  pl.BlockSpec((1, tn), lambda j: (0, j))],
        out_specs=pl.BlockSpec((rows, tn), lambda j: (0, j)),
        out_shape=jax.ShapeDtypeStruct((rows, n), F32),
        compiler_params=_cparams(("arbitrary",)),
        name="mod",
    )(cvec, w_mod, b_mod.reshape(1, n))


def _norm_rows(x, m, g_ref, d):
    y = x * lax.rsqrt(jnp.mean(x * x, axis=-1, keepdims=True) + EPS) * g_ref[...]
    return (y * (1.0 + m[:, d:2 * d]) + m[:, 0:d]).astype(BF16)


def _gate_rows(h, w_ref, alog_ref, dtb_ref):
    z = jnp.dot(h, w_ref[...], preferred_element_type=F32)
    lane = lax.broadcasted_iota(jnp.int32, z.shape, 1)
    u = z + dtb_ref[...]
    softplus = jnp.maximum(u, 0.0) + jnp.log(1.0 + jnp.exp(-jnp.abs(u)))
    return jnp.where(lane < 2 * HEADS, jax.nn.sigmoid(z), -jnp.exp(alog_ref[...]) * softplus)


def _hnorm_kernel(tok_ref, mod_ref, g_ref, w_ref, alog_ref, dtb_ref, *rest, d, mod_row):
    o_ref, og_ref = rest[-2:]
    row = pl.program_id(0) if mod_row is None else mod_row
    m = mod_ref[pl.ds(row, 1), :]
    h = _norm_rows(tok_ref[0], m, g_ref, d)
    o_ref[0] = h
    og_ref[0] = _gate_rows(h, w_ref, alog_ref, dtb_ref)


def _hnorm_call(tok, mod, norm_g, wg, alog, dtb, *, tm, gate_rows, gate_tile0, mod_row=None, prev_gates=None, name):
    bsz, rows, d = tok.shape
    full = lambda a: pl.BlockSpec(a.shape, lambda b, t: (0,) * a.ndim)
    args = [tok, mod, norm_g.reshape(1, d), wg, alog, dtb]
    in_specs = [pl.BlockSpec((1, tm, d), lambda b, t: (b, t, 0))] + [full(a) for a in args[1:]]
    aliases = {}
    if prev_gates is not None:
        in_specs.append(pl.BlockSpec(memory_space=pl.ANY))
        args.append(prev_gates)
        aliases = {6: 1}
    return pl.pallas_call(
        functools.partial(_hnorm_kernel, d=d, mod_row=mod_row),
        grid=(bsz, rows // tm),
        in_specs=in_specs,
        out_specs=[pl.BlockSpec((1, tm, d), lambda b, t: (b, t, 0)),
                   pl.BlockSpec((1, tm, LANES), lambda b, t: (b, t + gate_tile0, 0))],
        out_shape=[jax.ShapeDtypeStruct((bsz, rows, d), BF16), jax.ShapeDtypeStruct((bsz, gate_rows, LANES), F32)],
        input_output_aliases=aliases,
        compiler_params=_cparams(("arbitrary", "arbitrary")),
        name=name,
    )(*args)


@functools.lru_cache(maxsize=None)
def _shift_matrix(rowlen, taps, blk):
    t = np.arange(blk)
    mats = []
    for j in range(taps):
        d = j - taps // 2
        if d != 0:
            ok = ((t + d) >= 0) & ((t + d) < blk) & (t // rowlen == (t + d) // rowlen)
            s = np.zeros((blk, blk), np.float32)
            s[t[ok], t[ok] + d] = 1.0
            mats.append(s)
    return np.concatenate(mats, axis=1)


def _shift_block(rowlen, taps):
    return max(rowlen, MXU_TILE // (taps - 1) // rowlen * rowlen)


def _convproj_kernel(h_ref, w_ref, cw_ref, sh_ref, *rest, tm):
    o_ref = rest[-1]
    j = pl.program_id(0)
    taps = cw_ref.shape[1]
    mid = taps // 2
    blk = sh_ref.shape[0]
    mblk = max(blk, min(tm, 2 * MXU_TILE))
    na = jnp.where(j == 0, float(HEAD_DIM), jnp.where(j == 1, 1.0, 0.0))
    nb = jnp.where(j == 0, EPS * HEAD_DIM, jnp.where(j == 1, EPS, 1.0))
    cw16 = cw_ref[0].astype(BF16)
    for r in range(tm // mblk):
        zm = jnp.dot(h_ref[0, r * mblk:(r + 1) * mblk, :], w_ref[0], preferred_element_type=F32)
        for g in range(mblk // blk):
            rows = slice(r * mblk + g * blk, r * mblk + (g + 1) * blk)
            zb = zm[g * blk:(g + 1) * blk]
            zb16 = zb.astype(BF16)
            side = jnp.concatenate([zb16 * cw16[jt:jt + 1, :] for jt in range(taps) if jt != mid], axis=0)
            y = zb * cw_ref[0, mid:mid + 1, :] + jnp.dot(sh_ref[...], side, preferred_element_type=F32)
            y = _silu(y)
            for hh in range(HEADS):
                yh = y[:, hh * HEAD_DIM:(hh + 1) * HEAD_DIM]
                nrm = jnp.sum(yh * yh, axis=-1, keepdims=True)
                o_ref[0, 0, rows, hh * HEAD_DIM:(hh + 1) * HEAD_DIM] = (
                    yh * lax.rsqrt(nrm * na + nb)).astype(o_ref.dtype)


def _convproj_call(h, w3, cw3, *, rowlen, tm, tile0, out_rows, prev=None, name):
    bsz, rows, d = h.shape
    ntiles = rows // tm
    nw = w3.shape[2]
    taps = cw3.shape[1]
    blk = _shift_block(rowlen, taps)
    assert tm % max(blk, MXU_TILE) == 0 and max(blk, MXU_TILE) % blk == 0
    sh = jnp.asarray(_shift_matrix(rowlen, taps, blk)).astype(BF16)
    in_specs = [pl.BlockSpec((1, tm, d), lambda j, b, t: (b, t, 0)),
                pl.BlockSpec((1, d, nw), lambda j, b, t: (j, 0, 0)),
                pl.BlockSpec((1, cw3.shape[1], nw), lambda j, b, t: (j, 0, 0)),
                pl.BlockSpec(sh.shape, lambda j, b, t: (0, 0))]
    args = [h, w3, cw3, sh]
    aliases = {}
    if prev is not None:
        in_specs.append(pl.BlockSpec(memory_space=pl.ANY))
        args.append(prev)
        aliases = {4: 0}
    return pl.pallas_call(
        functools.partial(_convproj_kernel, tm=tm),
        grid=(3, bsz, ntiles),
        in_specs=in_specs,
        out_specs=pl.BlockSpec((1, 1, tm, nw), lambda j, b, t: (j, b, t + tile0, 0)),
        out_shape=jax.ShapeDtypeStruct((3, bsz, out_rows, nw), BF16),
        input_output_aliases=aliases,
        compiler_params=_cparams(("arbitrary", "arbitrary", "arbitrary")),
        name=name,
    )(*args)


def _hyproj_kernel(h_ref, w_ref, cw_ref, cb_ref, o_ref):
    n2s, n1s, d = h_ref.shape[1:]
    taps = cw_ref.shape[1]
    hrows = h_ref[0].reshape(n2s * n1s, d)
    step = min(MXU_TILE, n2s * n1s)
    z = jnp.concatenate([jnp.dot(hrows[r:r + step], w_ref[0], preferred_element_type=F32)
                         for r in range(0, n2s * n1s, step)], axis=0)
    y = cb_ref[0] + z * cw_ref[0, taps // 2:taps // 2 + 1, :]
    for jt in range(taps):
        s = (jt - taps // 2) * n1s
        if s > 0:
            y = y + jnp.concatenate([z[s:], jnp.zeros((s, z.shape[1]), F32)], axis=0) * cw_ref[0, jt:jt + 1, :]
        elif s < 0:
            y = y + jnp.concatenate([jnp.zeros((-s, z.shape[1]), F32), z[:s]], axis=0) * cw_ref[0, jt:jt + 1, :]
    o_ref[0, 0] = y.astype(o_ref.dtype).reshape(n2s, n1s, y.shape[1])


def _hyproj_call(h_perm, w3, cw3, cb3):
    bsz, n_lat, d = h_perm.shape
    nw = w3.shape[2]
    h1 = n_lat // FFT_INNER
    n1s = min(HYPROJ_N1, h1)
    hv = h_perm.reshape(bsz, FFT_INNER, h1, d)
    return pl.pallas_call(
        _hyproj_kernel,
        grid=(3, bsz, FFT_INNER // GRID_W, h1 // n1s),
        in_specs=[pl.BlockSpec((1, GRID_W, n1s, d), lambda j, b, r, t: (b, r, t, 0)),
                  pl.BlockSpec((1, d, nw), lambda j, b, r, t: (j, 0, 0)),
                  pl.BlockSpec((1, cw3.shape[1], nw), lambda j, b, r, t: (j, 0, 0)),
                  pl.BlockSpec((1, 1, nw), lambda j, b, r, t: (j, 0, 0))],
        out_specs=pl.BlockSpec((1, 1, GRID_W, n1s, nw), lambda j, b, r, t: (j, b, r, t, 0)),
        out_shape=jax.ShapeDtypeStruct((3, bsz, FFT_INNER, h1, nw), BF16),
        compiler_params=_cparams(("arbitrary",) * 4),
        name="hyproj",
    )(hv, w3, cw3, cb3)


def _gproj_kernel(h_ref, w_ref, o_ref):
    j = pl.program_id(0)
    for r in range(h_ref.shape[1] // MXU_TILE):
        rows = slice(r * MXU_TILE, (r + 1) * MXU_TILE)
        z = jnp.dot(h_ref[0, rows, :], w_ref[0], preferred_element_type=F32)
        s = jax.nn.sigmoid(z)
        o_ref[0, 0, rows, :] = jnp.where(j < 2, z * s, s).astype(BF16)


def _gproj_call(h, w4, n_lat):
    bsz, _, d = h.shape
    nw = w4.shape[2]
    tm = PROJ_TILE if n_lat % PROJ_TILE == 0 else 256
    return pl.pallas_call(
        _gproj_kernel,
        grid=(4, bsz, n_lat // tm),
        in_specs=[pl.BlockSpec((1, tm, d), lambda j, b, t: (b, t, 0)),
                  pl.BlockSpec((1, d, nw), lambda j, b, t: (j, 0, 0))],
        out_specs=pl.BlockSpec((1, 1, tm, nw), lambda j, b, t: (j, b, t, 0)),
        out_shape=jax.ShapeDtypeStruct((4, bsz, n_lat, nw), BF16),
        compiler_params=_cparams(("arbitrary", "arbitrary", "arbitrary")),
        name="gproj",
    )(h, w4)


def _unit_tri_inverse(ls, lowers, filler):
    c = ls[0].shape[0]
    ri = lax.broadcasted_iota(jnp.int32, (c, c), 0)
    ci = lax.broadcasted_iota(jnp.int32, (c, c), 1)
    same = lambda n: (ri // n) == (ci // n)
    eye = jnp.where(ri == ci, 1.0, 0.0)
    in2 = same(2)
    ts = [eye - jnp.where(in2, l, 0.0) for l in ls]
    n = 2
    nlevels = c.bit_length() - 2
    level = 0
    while n < c:
        blk = same(2 * n) & jnp.logical_not(same(n))
        offs = [jnp.where(blk, l, 0.0).astype(BF16) for l in ls]
        if n % SUBLANES:
            prods = [_bdot(t, off) for t, off in zip(ts, offs)]
            filler(level, nlevels)
            ts = [t - _bdot(p, t) for p, t in zip(prods, ts)]
        else:
            def moving(t, lower):
                first = n if lower else 0
                return jnp.concatenate([t[r:r + n] for r in range(first, c, 2 * n)], axis=0)

            def merged(t, upd, lower):
                pieces, k = [], 0
                for r in range(0, c, n):
                    if ((r // n) % 2 == 1) == lower:
                        pieces.append(upd[k * n:(k + 1) * n])
                        k += 1
                    else:
                        pieces.append(t[r:r + n])
                return jnp.concatenate(pieces, axis=0)

            rows = [moving(t, lo) for t, lo in zip(ts, lowers)]
            prods = [_bdot(r, off) for r, off in zip(rows, offs)]
            filler(level, nlevels)
            upds = [r - _bdot(p, t) for r, p, t in zip(rows, prods, ts)]
            ts = [merged(t, u, lo) for t, u, lo in zip(ts, upds, lowers)]
        n *= 2
        level += 1
    return ts


_NT = (((1,), (1,)), ((), ()))
_TN = (((0,), (0,)), ((), ()))


def _prep_kernel(qkv_ref, g_ref, gt_ref, tall_ref, wide_ref, el_ref, *, bsz):
    c = CHUNK
    ri = lax.broadcasted_iota(jnp.int32, (c, c), 0)
    ci = lax.broadcasted_iota(jnp.int32, (c, c), 1)
    tri_l = jnp.where(ri >= ci, 1.0, 0.0)
    tri_u = jnp.where(ri <= ci, 1.0, 0.0)
    incl = (ri >= ci, ri <= ci)
    strict = (ri > ci, ri < ci)
    hdot = functools.partial(jnp.dot, precision=HIGHEST, preferred_element_type=F32)

    per = tall_ref.shape[0]
    cb = [(ch, b) for ch in range(per) for b in range(bsz)]
    rows = lambda ch: slice(ch * c, (ch + 1) * c)
    gs = [g_ref[b, rows(ch), :] for ch, b in cb]
    gts = [gt_ref[b, ch] for ch, b in cb]
    gcol = [(hdot(tri_l, g), hdot(tri_u, g)) for g in gs]
    grow = [(hdot(gt, tri_u), hdot(gt, tri_l)) for gt in gts]
    glast = [(gc[0][c - 1:c], gc[1][0:1]) for gc in gcol]
    el_ref[...] = jnp.zeros_like(el_ref)
    for p, (ch, b) in enumerate(cb):
        for d in range(2):
            el_ref[ch, d * bsz + b:d * bsz + b + 1, :] = jnp.exp(glast[p][d])

    pbh = [(p, hh) for p in range(len(cb)) for hh in range(HEADS)]
    hs = lambda hh: slice(hh * HEAD_DIM, (hh + 1) * HEAD_DIM)
    tok = lambda w, p, hh: qkv_ref[w, cb[p][1], rows(cb[p][0]), hs(hh)]
    ks = [tok(1, p, hh) for p, hh in pbh]
    qs = [tok(0, p, hh) for p, hh in pbh]
    kk = [lax.dot_general(k, k, _NT, preferred_element_type=F32) for k in ks]
    qk = [lax.dot_general(q, k, _NT, preferred_element_type=F32) for q, k in zip(qs, ks)]
    inst = [(d, p, hh, i) for d in range(2) for i, (p, hh) in enumerate(pbh)]
    decs, betas, gcs = [], [], []
    for d, p, hh, _ in inst:
        lg = 2 * HEADS + d * HEADS + hh
        gcs.append(gcol[p][d][:, lg:lg + 1])
        betas.append(jnp.broadcast_to(gs[p][:, d * HEADS + hh:d * HEADS + hh + 1], (c, HEAD_DIM)))
        decs.append(jnp.exp(jnp.where(incl[d], gcs[-1] - grow[p][d][lg:lg + 1, :], -jnp.inf)))
    ls = [jnp.where(strict[d], kk[i] * beta[:, 0:c] * dec, 0.0)
          for (d, _, _, i), beta, dec in zip(inst, betas, decs)]

    def scaled_operands(part, nparts):
        for (d, p, hh, i), dec, beta, gc in list(zip(inst, decs, betas, gcs))[part::nparts]:
            ch, b = cb[p]
            lg = 2 * HEADS + d * HEADS + hh
            eg = jnp.broadcast_to(jnp.exp(gc), (c, HEAD_DIM))
            k = ks[i].astype(F32)
            tall_ref[ch, d, b, hh, :, c:2 * c] = (qk[i] * dec).astype(BF16)
            wide_ref[ch, d, b, hh, 0:c, :] = (qs[i].astype(F32) * eg).astype(BF16)
            wide_ref[ch, d, b, hh, c:2 * c, :] = (k * beta * eg).astype(BF16)
            wide_ref[ch, d, b, hh, 2 * c:3 * c, :] = (tok(2, p, hh).astype(F32) * beta).astype(BF16)
            wide_ref[ch, d, b, hh, 3 * c:4 * c, :] = (k * jnp.exp(glast[p][d][:, lg:lg + 1] - gc)).astype(BF16)

    ts = _unit_tri_inverse(ls, [d == 0 for d, _, _, _ in inst], scaled_operands)
    for (d, p, hh, _), t in zip(inst, ts):
        tall_ref[cb[p][0], d, cb[p][1], hh, :, 0:c] = t.astype(BF16)


def _prep_call(qkv, gates, gates_t):
    _, bsz, lt, dn = qkv.shape
    nc = lt // CHUNK
    per = PREP_CHUNKS
    assert nc % per == 0
    tall = (2, bsz, HEADS, CHUNK, 2 * CHUNK)
    wide = (2, bsz, HEADS, 4 * CHUNK, HEAD_DIM)
    return pl.pallas_call(
        functools.partial(_prep_kernel, bsz=bsz),
        grid=(nc // per,),
        in_specs=[pl.BlockSpec((3, bsz, per * CHUNK, dn), lambda i: (0, 0, i, 0)),
                  pl.BlockSpec((bsz, per * CHUNK, LANES), lambda i: (0, i, 0)),
                  pl.BlockSpec((bsz, per, 4 * HEADS, CHUNK), lambda i: (0, i, 0, 0))],
        out_specs=[pl.BlockSpec((per,) + tall, lambda i: (i, 0, 0, 0, 0, 0)),
                   pl.BlockSpec((per,) + wide, lambda i: (i, 0, 0, 0, 0, 0)),
                   pl.BlockSpec((per, SUBLANES, LANES), lambda i: (i, 0, 0))],
        out_shape=[jax.ShapeDtypeStruct((nc,) + tall, BF16),
                   jax.ShapeDtypeStruct((nc,) + wide, BF16),
                   jax.ShapeDtypeStruct((nc, SUBLANES, LANES), F32)],
        compiler_params=_cparams(("arbitrary",)),
        name="prep",
    )(qkv, gates, gates_t)


def _scan_kernel(tf_ref, tb_ref, wf_ref, wb_ref, ef_ref, eb_ref, of_ref, ob_ref, s_ref, *, bsz):
    i = pl.program_id(0)

    @pl.when(i == 0)
    def _():
        s_ref[...] = jnp.zeros_like(s_ref)

    c = CHUNK
    per = tf_ref.shape[0]
    dirs = ((tf_ref, wf_ref, ef_ref, of_ref), (tb_ref, wb_ref, eb_ref, ob_ref))
    inst = [(d, b, hh) for d in range(2) for b in range(bsz) for hh in range(HEADS)]
    state = [s_ref[d, b, hh] for d, b, hh in inst]
    for sub in range(per):
        at = (sub, per - 1 - sub)
        a1 = [jnp.dot(dirs[d][1][at[d], b, hh, 0:2 * c, :], s.astype(BF16), preferred_element_type=F32)
              for (d, b, hh), s in zip(inst, state)]
        v_new = [jnp.dot(dirs[d][0][at[d], b, hh, :, 0:c],
                         (dirs[d][1][at[d], b, hh, 2 * c:3 * c, :].astype(F32) - a[c:]).astype(BF16),
                         preferred_element_type=F32).astype(BF16) for (d, b, hh), a in zip(inst, a1)]
        new_state = []
        for (d, b, hh), a, vn, s in zip(inst, a1, v_new, state):
            lg = 2 * HEADS + d * HEADS + hh
            o = a[:c] + jnp.dot(dirs[d][0][at[d], b, hh, :, c:2 * c], vn, preferred_element_type=F32)
            dirs[d][3][b, at[d] * c:(at[d] + 1) * c, hh * HEAD_DIM:(hh + 1) * HEAD_DIM] = o.astype(BF16)
            el = dirs[d][2][at[d], d * bsz + b:d * bsz + b + 1, lg:lg + 1]
            new_state.append(s * el + lax.dot_general(dirs[d][1][at[d], b, hh, 3 * c:4 * c, :], vn, _TN,
                                                      preferred_element_type=F32))
        state = new_state
    for (d, b, hh), s in zip(inst, state):
        s_ref[d, b, hh] = s


def _scan_call(tall, wide, el, n_lat, n_ctx):
    nc, _, bsz = tall.shape[:3]
    dn = HEADS * HEAD_DIM
    per = SCAN_CHUNKS
    ncx, ncc, ns = n_lat // CHUNK // per, n_ctx // CHUNK // per, nc // per
    assert nc == (ncx + ncc) * per
    cf = lambda i: jnp.where(i < ncc, ncx + i, i - ncc)
    cb = lambda i: ns - 1 - i

    dspec = lambda arr, fn, d: pl.BlockSpec((per, None) + arr.shape[2:], lambda i: (fn(i), d, 0, 0, 0, 0))
    espec = lambda fn: pl.BlockSpec((per, SUBLANES, LANES), lambda i: (fn(i), 0, 0))
    ospec = lambda fn: pl.BlockSpec((bsz, per * CHUNK, dn), lambda i: (0, fn(i), 0))
    return pl.pallas_call(
        functools.partial(_scan_kernel, bsz=bsz),
        grid=(ns,),
        in_specs=[dspec(tall, cf, 0), dspec(tall, cb, 1), dspec(wide, cf, 0), dspec(wide, cb, 1),
                  espec(cf), espec(cb)],
        out_specs=[ospec(lambda i: jnp.maximum(i - ncc, 0)), ospec(lambda i: jnp.minimum(ns - 1 - i, ncx - 1))],
        out_shape=[jax.ShapeDtypeStruct((bsz, n_lat, dn), BF16)] * 2,
        scratch_shapes=[pltpu.VMEM((2, bsz, HEADS, HEAD_DIM, HEAD_DIM), F32)],
        compiler_params=_cparams(("arbitrary",)),
        name="scan",
    )(tall, tall, wide, wide, el, el)


def _hidden_kernel(pe_ref, w1_ref, b1_ref, w2_ref, b2_ref, w3_ref, b3_ref, f_ref, o_ref):
    f = f_ref[...]
    dot = functools.partial(jnp.dot, precision=HIGHEST, preferred_element_type=F32)
    a = jnp.sin(f * (dot(pe_ref[...], w1_ref[...]) + b1_ref[...]))
    a = jnp.sin(f * (dot(a, w2_ref[...]) + b2_ref[...]))
    o_ref[...] = jnp.sin(f * (dot(a, w3_ref[...]) + b3_ref[...]))


def _hidden_call(pe, w1p, b1, w2, b2, w3, b3, freq):
    rows, pw = pe.shape
    fw = 2 * w2.shape[0]
    tr = FILTER_TILE
    full = lambda a: pl.BlockSpec(a.shape, lambda t: (0,) * a.ndim)
    two = lambda a: jnp.tile(a.reshape(1, -1), (1, 2))
    diag2 = lambda a: jnp.kron(jnp.eye(2, dtype=a.dtype), a)
    args = (diag2(w1p), two(b1), diag2(w2), two(b2), diag2(w3), two(b3), two(freq))
    return pl.pallas_call(
        _hidden_kernel,
        grid=(rows // tr,),
        in_specs=[pl.BlockSpec((tr, pw), lambda t: (t, 0))] + [full(a) for a in args],
        out_specs=pl.BlockSpec((tr, fw), lambda t: (t, 0)),
        out_shape=jax.ShapeDtypeStruct((rows, fw), F32),
        compiler_params=_cparams(("arbitrary",)),
        name="hidden",
    )(pe, *args)


def _ftime_kernel(a_ref, t_ref, w_ref, dl_ref, h_ref, ss_ref, *, n_lat, tr):
    t = pl.program_id(0)
    h = _bdot(a_ref[...], w_ref[...]) * jnp.exp(-t_ref[...] * dl_ref[...])
    row = lax.broadcasted_iota(jnp.int32, (tr, 1), 0) + t * tr
    h = jnp.where(row == n_lat, 0.0, h)
    h_ref[...] = h.astype(h_ref.dtype)

    @pl.when(t == 0)
    def _():
        ss_ref[...] = jnp.zeros_like(ss_ref)

    ss_ref[...] += jnp.sum(h * h, axis=0, keepdims=True)


def _ftime_call(a3, tcol, wout2, deltas2, n_lat):
    rows = tcol.shape[0]
    fw = a3.shape[1]
    ncol = wout2.shape[2]
    tr = FILTER_TILE
    side = lambda t: (t * tr) // n_lat
    return pl.pallas_call(
        functools.partial(_ftime_kernel, n_lat=n_lat, tr=tr),
        grid=(rows // tr,),
        in_specs=[pl.BlockSpec((tr, fw), lambda t: (t % (n_lat // tr), 0)),
                  pl.BlockSpec((tr, 1), lambda t: (t, 0)),
                  pl.BlockSpec((None, fw, ncol), lambda t: (side(t), 0, 0)),
                  pl.BlockSpec((1, ncol), lambda t: (0, 0))],
        out_specs=[pl.BlockSpec((tr, ncol), lambda t: (t, 0)),
                   pl.BlockSpec((1, ncol), lambda t: (0, 0))],
        out_shape=[jax.ShapeDtypeStruct((rows, ncol), BF16), jax.ShapeDtypeStruct((1, ncol), F32)],
        compiler_params=_cparams(("arbitrary",)),
        name="ftime",
    )(a3, tcol, wout2, deltas2)


@functools.lru_cache(maxsize=None)
def _fft_tables(n_lat):
    n = 2 * n_lat
    n1t = n // FFT_INNER
    h1 = n1t // 2
    n2 = np.arange(FFT_INNER, dtype=np.int64)[:, None, None]
    k1 = np.arange(n1t, dtype=np.int64)[None, :, None]
    n1 = np.arange(h1, dtype=np.int64)[None, None, :]
    ang = (-2.0 * np.pi / n) * ((k1 * (FFT_INNER * n1 + n2)) % n)
    er, ei = np.cos(ang), np.sin(ang)
    e1 = np.concatenate([np.concatenate([er, -ei], axis=2), np.concatenate([ei, er], axis=2)], axis=1)
    e4 = np.transpose(e1, (0, 2, 1))
    sgn = np.where(np.arange(n1t) % 2 == 0, 1.0, -1.0)[None, :, None]
    ef = np.concatenate([np.concatenate([er, sgn * er], axis=2), np.concatenate([ei, sgn * ei], axis=2)], axis=1)
    a = np.arange(FFT_INNER, dtype=np.int64)
    ang2 = (-2.0 * np.pi / FFT_INNER) * ((a[:, None] * a[None, :]) % FFT_INNER)
    gr, gi = np.cos(ang2), np.sin(ang2)
    g2 = np.block([[gr, -gi], [gi, gr]])
    f32 = lambda x: np.ascontiguousarray(x, dtype=np.float32)
    return f32(e1), f32(e4), f32(ef), f32(g2), f32(g2.T)


def _hold(p, j, first, last, nsteps):
    active = (p >= first) & (p <= last) & (lax.rem(p - first, 2) == 0)
    return jnp.where(active, j, jnp.where(p < first, 0, nsteps - 1))


def _slab_read(x_ref, rows):
    return jnp.concatenate([x_ref[c, rows, :] for c in range(x_ref.shape[0])], axis=1)


def _slab_write(x_ref, rows, val):
    for c in range(x_ref.shape[0]):
        x_ref[c, rows, :] = val[:, c * LANES:(c + 1) * LANES]


def _spectrum_steps(x_ref, g2_ref, k1s, n1t, stride):
    cols = [jnp.concatenate([_slab_read(x_ref, pl.ds(k1, FFT_INNER, stride=stride)),
                             _slab_read(x_ref, pl.ds(n1t + k1, FFT_INNER, stride=stride))], axis=0).astype(BF16)
            for k1 in k1s]
    return [jnp.dot(g2_ref[...], c, preferred_element_type=F32) for c in cols]


def _ffft_kernel(h_ref, ef_ref, g2_ref, ss_ref, o_ref, x_ref, *, n1t, grp, kgrp, stride, inv_n):
    p = pl.program_id(1)
    j = pl.program_id(2)

    @pl.when(p == 0)
    def _():
        for j0 in range(0, grp, FFT_INTERLEAVE):
            jjs = list(range(j0, min(j0 + FFT_INTERLEAVE, grp)))
            us = [jnp.concatenate([h_ref[0, jj], h_ref[1, jj]], axis=0) for jj in jjs]
            slabs = [_bdot(ef_ref[jj], u) for jj, u in zip(jjs, us)]
            for jj, slab in zip(jjs, slabs):
                base = pl.multiple_of((j * grp + jj) * stride, SUBLANES)
                _slab_write(x_ref, pl.ds(base, 2 * n1t), slab)

    @pl.when(p == 1)
    def _():
        scale = lax.rsqrt(ss_ref[...] + EPS) * inv_n
        for j0 in range(0, kgrp, FFT_INTERLEAVE):
            jjs = list(range(j0, min(j0 + FFT_INTERLEAVE, kgrp)))
            xfs = _spectrum_steps(x_ref, g2_ref, [j * kgrp + jj for jj in jjs], n1t, stride)
            for jj, xf in zip(jjs, xfs):
                o_ref[jj] = (xf * scale).astype(BF16)


def _fft_groups(n1t, ns):
    ns = min(ns, n1t)
    assert FFT_INNER % ns == 0 and n1t % ns == 0
    return ns, FFT_INNER // ns, n1t // ns


def _ffft_call(hfull, ssq, ef, g2, n_lat):
    ncol = hfull.shape[1]
    n1t = 2 * n_lat // FFT_INNER
    h1 = n1t // 2
    ct = FFT_CHANNELS
    ns, grp, kgrp = _fft_groups(n1t, FILTER_FFT_STEPS)
    stride = 2 * n1t + SLAB_PAD
    hv = hfull.reshape(2, FFT_INNER, h1, ncol)
    nct = ncol // ct
    cpo = nct // 2
    return pl.pallas_call(
        functools.partial(_ffft_kernel, n1t=n1t, grp=grp, kgrp=kgrp, stride=stride, inv_n=1.0 / (2 * n_lat)),
        grid=(nct, 2, ns),
        in_specs=[pl.BlockSpec((2, grp, h1, ct), lambda c, p, j: (0, _hold(p, j, 0, 0, ns), 0, c)),
                  pl.BlockSpec((grp, 2 * n1t, n1t), lambda c, p, j: (_hold(p, j, 0, 0, ns), 0, 0)),
                  pl.BlockSpec((2 * FFT_INNER, 2 * FFT_INNER), lambda c, p, j: (0, 0)),
                  pl.BlockSpec((1, ct), lambda c, p, j: (0, c))],
        out_specs=pl.BlockSpec((None, kgrp, 2 * FFT_INNER, ct),
                               lambda c, p, j: (c // cpo, _hold(p, j, 1, 1, ns), 0, c % cpo)),
        out_shape=jax.ShapeDtypeStruct((2, n1t, 2 * FFT_INNER, ncol // 2), BF16),
        scratch_shapes=[pltpu.VMEM((ct // LANES, FFT_INNER * stride, LANES), F32)],
        compiler_params=_cparams(("arbitrary", "arbitrary", "arbitrary")),
        name="ffft",
    )(hv, ef, g2, ssq)


def _hyena_kernel(v_ref, x1_ref, x2_ref, hf_ref, e1_ref, e4_ref, g2_ref, g3_ref, bias_ref,
                  o_ref, x_ref, z_ref, *, n1t, grp, kgrp, stride):
    p = pl.program_id(1)
    j = pl.program_id(2)
    h1 = n1t // 2

    def gather(ref, jj):
        return jnp.concatenate([ref[0, jj], ref[1, jj]], axis=0).astype(F32)

    def slab_base(jj):
        return pl.multiple_of((j * grp + jj) * stride, SUBLANES)

    def time_rows(jjs):
        slabs = [_slab_read(x_ref, pl.ds(slab_base(jj), 2 * n1t)).astype(BF16) for jj in jjs]
        return [jnp.dot(e4_ref[jj], s, preferred_element_type=F32) for jj, s in zip(jjs, slabs)]

    def write_slabs(jjs, us):
        slabs = [jnp.dot(e1_ref[jj], u, preferred_element_type=F32) for jj, u in zip(jjs, us)]
        for jj, slab in zip(jjs, slabs):
            _slab_write(x_ref, pl.ds(slab_base(jj), 2 * n1t), slab)

    def chunks(n):
        return [list(range(j0, min(j0 + FFT_INTERLEAVE, n))) for j0 in range(0, n, FFT_INTERLEAVE)]

    @pl.when(p == 0)
    def _():
        for jjs in chunks(grp):
            write_slabs(jjs, [gather(v_ref, jj).astype(BF16) for jj in jjs])

    @pl.when((p == 1) | (p == 3))
    def _():
        for jjs in chunks(kgrp):
            k1s = [j * kgrp + jj for jj in jjs]
            xfs = _spectrum_steps(x_ref, g2_ref, k1s, n1t, stride)
            ys = []
            for jj, xf in zip(jjs, xfs):
                hf = hf_ref[jj].astype(F32)
                xr, xi = xf[:FFT_INNER], xf[FFT_INNER:]
                hr, hi = hf[:FFT_INNER], hf[FFT_INNER:]
                ys.append(jnp.concatenate([xr * hr - xi * hi, xr * hi + xi * hr], axis=0).astype(BF16))
            bbs = [jnp.dot(g3_ref[...], y, preferred_element_type=F32) for y in ys]
            for k1, bb in zip(k1s, bbs):
                _slab_write(x_ref, pl.ds(k1, FFT_INNER, stride=stride), bb[:FFT_INNER])
                _slab_write(x_ref, pl.ds(n1t + k1, FFT_INNER, stride=stride), bb[FFT_INNER:])

    @pl.when(p == 2)
    def _():
        for jjs in chunks(grp):
            z1s = [(gather(x1_ref, jj) * (y + gather(v_ref, jj) * bias_ref[0:1, :])).astype(BF16)
                   for jj, y in zip(jjs, time_rows(jjs))]
            for jj, z1 in zip(jjs, z1s):
                z_ref[j * grp + jj] = z1
            write_slabs(jjs, z1s)

    @pl.when(p == 4)
    def _():
        for jjs in chunks(grp):
            for jj, y in zip(jjs, time_rows(jjs)):
                out = gather(x2_ref, jj) * (y + z_ref[j * grp + jj].astype(F32) * bias_ref[1:2, :])
                o_ref[0, jj] = out[:h1].astype(o_ref.dtype)
                o_ref[1, jj] = out[h1:].astype(o_ref.dtype)


def _hyena_call(hv, hf, e1, e4, g2, g3, bias):
    _, bsz, _, h1, ch = hv.shape
    assert bsz == 2, "the two batch rows are packed as the real and imaginary parts of one FFT"
    n1t = 2 * h1
    ct = FFT_CHANNELS
    ns, grp, kgrp = _fft_groups(n1t, CONV_FFT_STEPS)
    stride = 2 * n1t + SLAB_PAD

    def hspec(which, first, last):
        return pl.BlockSpec((None, bsz, grp, h1, ct),
                            lambda c, p, j: (which, 0, _hold(p, j, first, last, ns), 0, c))

    def hf_map(c, p, j):
        order = jnp.where(p >= 2, 1, 0)
        step = jnp.where((p == 1) | (p == 3), j, jnp.where((p == 0) | (p == 2), 0, ns - 1))
        return (order, step, 0, c)

    return pl.pallas_call(
        functools.partial(_hyena_kernel, n1t=n1t, grp=grp, kgrp=kgrp, stride=stride),
        grid=(ch // ct, 5, ns),
        in_specs=[hspec(0, 0, 2), hspec(1, 2, 2), hspec(2, 4, 4),
                  pl.BlockSpec((None, kgrp, 2 * FFT_INNER, ct), hf_map),
                  pl.BlockSpec((grp, 2 * n1t, n1t), lambda c, p, j: (_hold(p, j, 0, 2, ns), 0, 0)),
                  pl.BlockSpec((grp, n1t, 2 * n1t), lambda c, p, j: (_hold(p, j, 2, 4, ns), 0, 0)),
                  pl.BlockSpec((2 * FFT_INNER, 2 * FFT_INNER), lambda c, p, j: (0, 0)),
                  pl.BlockSpec((2 * FFT_INNER, 2 * FFT_INNER), lambda c, p, j: (0, 0)),
                  pl.BlockSpec((2, ct), lambda c, p, j: (0, c))],
        out_specs=pl.BlockSpec((bsz, grp, h1, ct), lambda c, p, j: (0, _hold(p, j, 4, 4, ns), 0, c)),
        out_shape=jax.ShapeDtypeStruct((bsz, FFT_INNER, h1, ch), BF16),
        scratch_shapes=[pltpu.VMEM((ct // LANES, FFT_INNER * stride, LANES), F32),
                        pltpu.VMEM((FFT_INNER, n1t, ct), BF16)],
        compiler_params=_cparams(("arbitrary", "arbitrary", "arbitrary"), VMEM_LIMIT_CONV),
        name="hyena",
    )(hv, hv, hv, hf, e1, e4, g2, g3, bias)


def _merge_kernel(of_ref, ob_ref, gp_ref, hy_ref, x_ref, mod_ref, dng_ref, wpa_ref, wpb_ref, wo_ref,
                  fg_ref, o_ref, *, d):
    b = pl.program_id(0)
    blocks = [slice(r * MXU_TILE, (r + 1) * MXU_TILE) for r in range(o_ref.shape[1] // MXU_TILE)]
    f32 = lambda ref, *idx: ref[idx].astype(F32)

    def head_norm(rows):
        o = f32(of_ref, 0, rows) + f32(ob_ref, 0, rows)
        za = f32(gp_ref, 0, 0, rows)
        parts = []
        for hh in range(HEADS):
            hs = slice(hh * HEAD_DIM, (hh + 1) * HEAD_DIM)
            oh = o[:, hs]
            ms = jnp.mean(oh * oh, axis=-1, keepdims=True)
            parts.append((oh * lax.rsqrt(ms + EPS) * dng_ref[...] * za[:, hs]).astype(BF16))
        return jnp.concatenate(parts, axis=1)

    o_a = [head_norm(rows) for rows in blocks]
    y_a = [jnp.dot(a, wpa_ref[...], preferred_element_type=F32) for a in o_a]
    o_b = [(f32(hy_ref, 0, rows) * f32(gp_ref, 1, 0, rows)).astype(BF16) for rows in blocks]
    y_b = [jnp.dot(a, wpb_ref[...], preferred_element_type=F32) for a in o_b]
    m = [(f32(gp_ref, 2, 0, rows) * ya + f32(gp_ref, 3, 0, rows) * yb).astype(BF16)
         for rows, ya, yb in zip(blocks, y_a, y_b)]
    y = [jnp.dot(a, wo_ref[...], preferred_element_type=F32) for a in m]
    gate = mod_ref[pl.ds(b, 1), :][:, 2 * d:3 * d]
    for rows, yr in zip(blocks, y):
        xn = x_ref[0, rows, :] + gate * yr
        o_ref[0, rows, :] = xn * lax.rsqrt(jnp.mean(xn * xn, axis=-1, keepdims=True) + EPS) * fg_ref[...]


def _merge_call(o_f, o_b, gp, hy, x, mod, dn_norm_g, w_pa, w_pb, w_out, final_g):
    bsz, n_lat, d = x.shape
    tm = MERGE_TILE if n_lat % MERGE_TILE == 0 else 256
    wfull = lambda a: pl.BlockSpec(a.shape, lambda b, t: (0,) * a.ndim)
    tok = pl.BlockSpec((1, tm, d), lambda b, t: (b, t, 0))
    return pl.pallas_call(
        functools.partial(_merge_kernel, d=d),
        grid=(bsz, n_lat // tm),
        in_specs=[tok, tok,
                  pl.BlockSpec((4, 1, tm, d), lambda b, t: (0, b, t, 0)),
                  tok, tok, wfull(mod),
                  pl.BlockSpec((1, HEAD_DIM), lambda b, t: (0, 0)),
                  wfull(w_pa), wfull(w_pb), wfull(w_out),
                  pl.BlockSpec((1, d), lambda b, t: (0, 0))],
        out_specs=tok,
        out_shape=jax.ShapeDtypeStruct((bsz, n_lat, d), F32),
        compiler_params=_cparams(("arbitrary", "arbitrary")),
        name="merge",
    )(o_f, o_b, gp, hy, x, mod, dn_norm_g.reshape(1, HEAD_DIM), w_pa, w_pb, w_out, final_g.reshape(1, d))


def _position_features(n_lat):
    h1 = n_lat // FFT_INNER
    m = (np.arange(FFT_INNER)[:, None] + FFT_INNER * np.arange(h1)[None, :]).reshape(-1)
    lag = np.concatenate([m, n_lat - m]).astype(np.float64)
    lag[n_lat] = 0.0
    bands = (HY_EMB - 1) // 2
    t = (lag / (n_lat - 1))[:, None]
    wpos = (2.0 * math.pi / n_lat) * lag[:, None]
    fb = np.linspace(1e-4, bands - 1, bands)[None, :]
    pe = np.concatenate([t, np.cos(fb * wpos), -np.sin(fb * wpos)], axis=1)
    pe_pad = np.zeros((2 * n_lat, LANES), np.float32)
    pe_pad[:, :HY_EMB] = pe
    return pe_pad, np.ascontiguousarray(t, dtype=np.float32)


def kernel(x, c, ctx, c_ctx, w_mod, b_mod, norm_g, w_in, dn_conv_w, dn_a_log, dn_dt_bias, dn_norm_g,
           hy_conv_w, hy_conv_b, hy_f_w1, hy_f_b1, hy_f_w2, hy_f_b2, hy_f_w3, hy_f_b3, hy_f_wout,
           hy_f_freq, hy_bias, w_pa, w_pb, w_out, final_g):
    bsz, n_lat, d = x.shape
    n_ctx = ctx.shape[1]
    assert w_mod.shape[0] == 1, "single layer: the context stream is only read through its scan states"
    dn = HEADS * HEAD_DIM
    hy = hy_bias.shape[-1]
    assert n_lat % 256 == 0 and n_ctx % 256 == 0 and dn == d and hy == d

    cvec = jnp.zeros((SUBLANES, d), F32).at[:bsz].set(c).at[bsz].set(c_ctx)
    mod = _mod_call(cvec, w_mod[0], b_mod[0])

    w = w_in[0].astype(BF16)
    o_qkv, o_gate, o_hy, o_gp = 0, 4 * dn, 4 * dn + 4 * HEADS, 4 * dn + 4 * HEADS + 3 * hy
    col3 = lambda a, off, n: jnp.transpose(a[:, off:off + 3 * n].reshape(a.shape[0], 3, n), (1, 0, 2))
    lt = n_lat + n_ctx
    tm = PROJ_TILE if n_lat % PROJ_TILE == 0 else 256

    wg = jnp.zeros((d, LANES), BF16).at[:, :4 * HEADS].set(w[:, o_gate:o_gate + 4 * HEADS])
    lane_pad = lambda a: jnp.zeros((1, LANES), F32).at[0, 2 * HEADS:4 * HEADS].set(a.reshape(-1))
    gate_args = (wg, lane_pad(dn_a_log[0]), lane_pad(dn_dt_bias[0]))
    h_tok, gates = _hnorm_call(x, mod, norm_g[0], *gate_args, tm=NORM_TILE, gate_rows=lt, gate_tile0=0, name="hnorm")
    h_ctx, gates = _hnorm_call(ctx, mod, norm_g[0], *gate_args, tm=n_ctx, gate_rows=lt, gate_tile0=n_lat // n_ctx,
                               mod_row=bsz, prev_gates=gates, name="hnorm_ctx")
    h1 = n_lat // FFT_INNER
    h_perm = jnp.transpose(h_tok.reshape(bsz, h1, FFT_INNER, d), (0, 2, 1, 3)).reshape(bsz, n_lat, d)

    w_qkv, cw_qkv = col3(w, o_qkv, dn), col3(dn_conv_w[0], 0, dn)
    qkv = _convproj_call(h_tok, w_qkv, cw_qkv, rowlen=GRID_W, tm=tm, tile0=0, out_rows=lt, name="qkv")
    qkv = _convproj_call(h_ctx, w_qkv, cw_qkv, rowlen=n_ctx, tm=n_ctx, tile0=n_lat // n_ctx, out_rows=lt,
                         prev=qkv, name="qkv_ctx")
    gates_t = jnp.transpose(gates[:, :, :4 * HEADS].reshape(bsz, lt // CHUNK, CHUNK, 4 * HEADS), (0, 1, 3, 2))
    tall, wide, el = _prep_call(qkv, gates, gates_t)
    o_f, o_b = _scan_call(tall, wide, el, n_lat, n_ctx)

    hyp = _hyproj_call(h_perm, col3(w, o_hy, hy), col3(hy_conv_w[0], 0, hy), hy_conv_b[0].reshape(3, 1, hy))
    w4 = jnp.stack([w[:, 3 * dn:4 * dn]] + [w[:, o_gp + i * d:o_gp + (i + 1) * d] for i in range(3)])
    gp = _gproj_call(h_tok, w4, n_lat)

    pe, tcol = _position_features(n_lat)
    w1p = jnp.zeros((LANES, hy_f_w1.shape[-1]), F32).at[:HY_EMB].set(hy_f_w1[0])
    pe2 = jnp.asarray(np.concatenate([pe[:n_lat], pe[n_lat:]], axis=1))
    a3 = _hidden_call(pe2, w1p, hy_f_b1[0], hy_f_w2[0], hy_f_b2[0], hy_f_w3[0], hy_f_b3[0], hy_f_freq[0])
    deltas = np.abs(np.linspace(math.log(HY_DECAY_TARGET) / HY_SLOW_DECAY,
                                math.log(HY_DECAY_TARGET) / HY_FAST_DECAY, hy)).astype(np.float32)
    deltas2 = jnp.asarray(np.tile(deltas, 2)[None, :])
    wside = hy_f_wout[0].reshape(-1, 2, 2 * hy).transpose(1, 0, 2).astype(BF16)
    zside = jnp.zeros_like(wside[0])
    wout2 = jnp.stack([jnp.concatenate([wside[0], zside]), jnp.concatenate([zside, wside[1]])])
    hfull, ssq = _ftime_call(a3, jnp.asarray(tcol), wout2, deltas2, n_lat)
    e1, e4, ef, g2, g3 = (jnp.asarray(t).astype(BF16) for t in _fft_tables(n_lat))
    hf = _ffft_call(hfull, ssq, ef, g2, n_lat)

    yh = _hyena_call(hyp, hf, e1, e4, g2, g3, hy_bias[0])
    yh = jnp.transpose(yh, (0, 2, 1, 3)).reshape(bsz, n_lat, hy)

    return _merge_call(o_f, o_b, gp, yh, x, mod, dn_norm_g[0], w_pa[0].astype(BF16), w_pb[0].astype(BF16),
                       w_out[0].astype(BF16), final_g)
```

```python
import functools
import math

import numpy as np
import jax
import jax.numpy as jnp
from jax import lax
from jax.experimental import pallas as pl
from jax.experimental.pallas import tpu as pltpu

F32 = jnp.float32
BF16 = jnp.bfloat16
HIGHEST = lax.Precision.HIGHEST

EPS = 1e-6
HEADS = 8
HEAD_DIM = 128
CHUNK = 64
GRID_W = 64
HY_EMB = 33
HY_DECAY_TARGET = 1e-2
HY_FAST_DECAY = 0.3
HY_SLOW_DECAY = 1.5

LANES = 128
SUBLANES = 8
FFT_INNER = 128
CONV_FFT_STEPS = 8
FILTER_FFT_STEPS = 4
FFT_CHANNELS = 256
FFT_INTERLEAVE = 8
PROJ_TILE = 4096
NORM_TILE = 1024
FILTER_TILE = 1024
MERGE_TILE = 512
SCAN_CHUNKS = 4
PREP_CHUNKS = 1
HYPROJ_N1 = 32
MXU_TILE = 256
SLAB_PAD = 8
VMEM_LIMIT = 56 * 1024 * 1024
VMEM_LIMIT_CONV = 62 * 1024 * 1024


def _cparams(sem, vmem=VMEM_LIMIT, fuse_inputs=None):
    return pltpu.CompilerParams(dimension_semantics=sem, vmem_limit_bytes=vmem, allow_input_fusion=fuse_inputs)


def _bdot(a, b):
    return jnp.dot(a.astype(BF16), b.astype(BF16), preferred_element_type=F32)


def _silu(x):
    return x * jax.nn.sigmoid(x)


def _mod_kernel(c_ref, w_ref, b_ref, o_ref):
    s = _silu(c_ref[...])
    o_ref[...] = jnp.dot(s, w_ref[...], precision=HIGHEST, preferred_element_type=F32) + b_ref[...]


def _mod_call(cvec, w_mod, b_mod):
    rows, d = cvec.shape
    n = w_mod.shape[1]
    tn = 1024
    return pl.pallas_call(
        _mod_kernel,
        grid=(n // tn,),
        in_specs=[pl.BlockSpec((rows, d), lambda j: (0, 0)),
                  pl.BlockSpec((d, tn), lambda j: (0, j)),
                  pl.BlockSpec((1, tn), lambda j: (0, j))],
        out_specs=pl.BlockSpec((rows, tn), lambda j: (0, j)),
        out_shape=jax.ShapeDtypeStruct((rows, n), F32),
        compiler_params=_cparams(("arbitrary",)),
        name="mod",
    )(cvec, w_mod, b_mod.reshape(1, n))


def _norm_rows(x, m, g_ref, d):
    y = x * lax.rsqrt(jnp.mean(x * x, axis=-1, keepdims=True) + EPS) * g_ref[...]
    return (y * (1.0 + m[:, d:2 * d]) + m[:, 0:d]).astype(BF16)


def _gate_rows(h, w_ref, alog_ref, dtb_ref):
    z = jnp.dot(h, w_ref[...], preferred_element_type=F32)
    lane = lax.broadcasted_iota(jnp.int32, z.shape, 1)
    u = z + dtb_ref[...]
    softplus = jnp.maximum(u, 0.0) + jnp.log(1.0 + jnp.exp(-jnp.abs(u)))
    return jnp.where(lane < 2 * HEADS, jax.nn.sigmoid(z), -jnp.exp(alog_ref[...]) * softplus)


def _hnorm_kernel(tok_ref, mod_ref, g_ref, w_ref, alog_ref, dtb_ref, *rest, d, mod_row):
    o_ref, og_ref = rest[-2:]
    row = pl.program_id(0) if mod_row is None else mod_row
    m = mod_ref[pl.ds(row, 1), :]
    h = _norm_rows(tok_ref[0], m, g_ref, d)
    o_ref[0] = h
    og_ref[0] = _gate_rows(h, w_ref, alog_ref, dtb_ref)


def _hnorm_call(tok, mod, norm_g, wg, alog, dtb, *, tm, gate_rows, gate_tile0, mod_row=None, prev_gates=None, name):
    bsz, rows, d = tok.shape
    full = lambda a: pl.BlockSpec(a.shape, lambda b, t: (0,) * a.ndim)
    args = [tok, mod, norm_g.reshape(1, d), wg, alog, dtb]
    in_specs = [pl.BlockSpec((1, tm, d), lambda b, t: (b, t, 0))] + [full(a) for a in args[1:]]
    aliases = {}
    if prev_gates is not None:
        in_specs.append(pl.BlockSpec(memory_space=pl.ANY))
        args.append(prev_gates)
        aliases = {6: 1}
    return pl.pallas_call(
        functools.partial(_hnorm_kernel, d=d, mod_row=mod_row),
        grid=(bsz, rows // tm),
        in_specs=in_specs,
        out_specs=[pl.BlockSpec((1, tm, d), lambda b, t: (b, t, 0)),
                   pl.BlockSpec((1, tm, LANES), lambda b, t: (b, t + gate_tile0, 0))],
        out_shape=[jax.ShapeDtypeStruct((bsz, rows, d), BF16), jax.ShapeDtypeStruct((bsz, gate_rows, LANES), F32)],
        input_output_aliases=aliases,
        compiler_params=_cparams(("arbitrary", "arbitrary")),
        name=name,
    )(*args)


@functools.lru_cache(maxsize=None)
def _shift_matrix(rowlen, taps, blk):
    t = np.arange(blk)
    mats = []
    for j in range(taps):
        d = j - taps // 2
        if d != 0:
            ok = ((t + d) >= 0) & ((t + d) < blk) & (t // rowlen == (t + d) // rowlen)
            s = np.zeros((blk, blk), np.float32)
            s[t[ok], t[ok] + d] = 1.0
            mats.append(s)
    return np.concatenate(mats, axis=1)


def _shift_block(rowlen, taps):
    return max(rowlen, MXU_TILE // (taps - 1) // rowlen * rowlen)


def _convproj_kernel(h_ref, w_ref, cw_ref, sh_ref, *rest, tm):
    o_ref = rest[-1]
    j = pl.program_id(0)
    taps = cw_ref.shape[1]
    mid = taps // 2
    blk = sh_ref.shape[0]
    mblk = max(blk, min(tm, 2 * MXU_TILE))
    na = jnp.where(j == 0, float(HEAD_DIM), jnp.where(j == 1, 1.0, 0.0))
    nb = jnp.where(j == 0, EPS * HEAD_DIM, jnp.where(j == 1, EPS, 1.0))
    cw16 = cw_ref[0].astype(BF16)
    for r in range(tm // mblk):
        zm = jnp.dot(h_ref[0, r * mblk:(r + 1) * mblk, :], w_ref[0], preferred_element_type=F32)
        for g in range(mblk // blk):
            rows = slice(r * mblk + g * blk, r * mblk + (g + 1) * blk)
            zb = zm[g * blk:(g + 1) * blk]
            zb16 = zb.astype(BF16)
            side = jnp.concatenate([zb16 * cw16[jt:jt + 1, :] for jt in range(taps) if jt != mid], axis=0)
            y = zb * cw_ref[0, mid:mid + 1, :] + jnp.dot(sh_ref[...], side, preferred_element_type=F32)
            y = _silu(y)
            for hh in range(HEADS):
                yh = y[:, hh * HEAD_DIM:(hh + 1) * HEAD_DIM]
                nrm = jnp.sum(yh * yh, axis=-1, keepdims=True)
                o_ref[0, 0, rows, hh * HEAD_DIM:(hh + 1) * HEAD_DIM] = (
                    yh * lax.rsqrt(nrm * na + nb)).astype(o_ref.dtype)


def _convproj_call(h, w3, cw3, *, rowlen, tm, tile0, out_rows, prev=None, name):
    bsz, rows, d = h.shape
    ntiles = rows // tm
    nw = w3.shape[2]
    taps = cw3.shape[1]
    blk = _shift_block(rowlen, taps)
    assert tm % max(blk, MXU_TILE) == 0 and max(blk, MXU_TILE) % blk == 0
    sh = jnp.asarray(_shift_matrix(rowlen, taps, blk)).astype(BF16)
    in_specs = [pl.BlockSpec((1, tm, d), lambda j, b, t: (b, t, 0)),
                pl.BlockSpec((1, d, nw), lambda j, b, t: (j, 0, 0)),
                pl.BlockSpec((1, cw3.shape[1], nw), lambda j, b, t: (j, 0, 0)),
                pl.BlockSpec(sh.shape, lambda j, b, t: (0, 0))]
    args = [h, w3, cw3, sh]
    aliases = {}
    if prev is not None:
        in_specs.append(pl.BlockSpec(memory_space=pl.ANY))
        args.append(prev)
        aliases = {4: 0}
    return pl.pallas_call(
        functools.partial(_convproj_kernel, tm=tm),
        grid=(3, bsz, ntiles),
        in_specs=in_specs,
        out_specs=pl.BlockSpec((1, 1, tm, nw), lambda j, b, t: (j, b, t + tile0, 0)),
        out_shape=jax.ShapeDtypeStruct((3, bsz, out_rows, nw), BF16),
        input_output_aliases=aliases,
        compiler_params=_cparams(("arbitrary", "arbitrary", "arbitrary")),
        name=name,
    )(*args)


def _hyproj_kernel(h_ref, w_ref, cw_ref, cb_ref, o_ref):
    n2s, n1s, d = h_ref.shape[1:]
    taps = cw_ref.shape[1]
    hrows = h_ref[0].reshape(n2s * n1s, d)
    step = min(MXU_TILE, n2s * n1s)
    z = jnp.concatenate([jnp.dot(hrows[r:r + step], w_ref[0], preferred_element_type=F32)
                         for r in range(0, n2s * n1s, step)], axis=0)
    y = cb_ref[0] + z * cw_ref[0, taps // 2:taps // 2 + 1, :]
    for jt in range(taps):
        s = (jt - taps // 2) * n1s
        if s > 0:
            y = y + jnp.concatenate([z[s:], jnp.zeros((s, z.shape[1]), F32)], axis=0) * cw_ref[0, jt:jt + 1, :]
        elif s < 0:
            y = y + jnp.concatenate([jnp.zeros((-s, z.shape[1]), F32), z[:s]], axis=0) * cw_ref[0, jt:jt + 1, :]
    o_ref[0, 0] = y.astype(o_ref.dtype).reshape(n2s, n1s, y.shape[1])


def _hyproj_call(h_perm, w3, cw3, cb3):
    bsz, n_lat, d = h_perm.shape
    nw = w3.shape[2]
    h1 = n_lat // FFT_INNER
    n1s = min(HYPROJ_N1, h1)
    hv = h_perm.reshape(bsz, FFT_INNER, h1, d)
    return pl.pallas_call(
        _hyproj_kernel,
        grid=(3, bsz, FFT_INNER // GRID_W, h1 // n1s),
        in_specs=[pl.BlockSpec((1, GRID_W, n1s, d), lambda j, b, r, t: (b, r, t, 0)),
                  pl.BlockSpec((1, d, nw), lambda j, b, r, t: (j, 0, 0)),
                  pl.BlockSpec((1, cw3.shape[1], nw), lambda j, b, r, t: (j, 0, 0)),
                  pl.BlockSpec((1, 1, nw), lambda j, b, r, t: (j, 0, 0))],
        out_specs=pl.BlockSpec((1, 1, GRID_W, n1s, nw), lambda j, b, r, t: (j, b, r, t, 0)),
        out_shape=jax.ShapeDtypeStruct((3, bsz, FFT_INNER, h1, nw), BF16),
        compiler_params=_cparams(("arbitrary",) * 4, fuse_inputs=[False, True, False, False]),
        name="hyproj",
    )(hv, w3, cw3, cb3)


def _gproj_kernel(h_ref, w_ref, o_ref):
    j = pl.program_id(0)
    for r in range(h_ref.shape[1] // MXU_TILE):
        rows = slice(r * MXU_TILE, (r + 1) * MXU_TILE)
        z = jnp.dot(h_ref[0, rows, :], w_ref[0], preferred_element_type=F32)
        s = jax.nn.sigmoid(z)
        o_ref[0, 0, rows, :] = jnp.where(j < 2, z * s, s).astype(BF16)


def _gproj_call(h, w4, n_lat):
    bsz, _, d = h.shape
    nw = w4.shape[2]
    tm = PROJ_TILE if n_lat % PROJ_TILE == 0 else 256
    return pl.pallas_call(
        _gproj_kernel,
        grid=(4, bsz, n_lat // tm),
        in_specs=[pl.BlockSpec((1, tm, d), lambda j, b, t: (b, t, 0)),
                  pl.BlockSpec((1, d, nw), lambda j, b, t: (j, 0, 0))],
        out_specs=pl.BlockSpec((1, 1, tm, nw), lambda j, b, t: (j, b, t, 0)),
        out_shape=jax.ShapeDtypeStruct((4, bsz, n_lat, nw), BF16),
        compiler_params=_cparams(("arbitrary", "arbitrary", "arbitrary"), fuse_inputs=[False, True]),
        name="gproj",
    )(h, w4)


def _unit_tri_inverse(ls, lowers, filler):
    c = ls[0].shape[0]
    ri = lax.broadcasted_iota(jnp.int32, (c, c), 0)
    ci = lax.broadcasted_iota(jnp.int32, (c, c), 1)
    same = lambda n: (ri // n) == (ci // n)
    eye = jnp.where(ri == ci, 1.0, 0.0)
    in2 = same(2)
    ts = [eye - jnp.where(in2, l, 0.0) for l in ls]
    n = 2
    nlevels = c.bit_length() - 2
    level = 0
    while n < c:
        blk = same(2 * n) & jnp.logical_not(same(n))
        offs = [jnp.where(blk, l, 0.0).astype(BF16) for l in ls]
        if n % SUBLANES:
            prods = [_bdot(t, off) for t, off in zip(ts, offs)]
            filler(level, nlevels)
            ts = [t - _bdot(p, t) for p, t in zip(prods, ts)]
        else:
            def moving(t, lower):
                first = n if lower else 0
                return jnp.concatenate([t[r:r + n] for r in range(first, c, 2 * n)], axis=0)

            def merged(t, upd, lower):
                pieces, k = [], 0
                for r in range(0, c, n):
                    if ((r // n) % 2 == 1) == lower:
                        pieces.append(upd[k * n:(k + 1) * n])
                        k += 1
                    else:
                        pieces.append(t[r:r + n])
                return jnp.concatenate(pieces, axis=0)

            rows = [moving(t, lo) for t, lo in zip(ts, lowers)]
            prods = [_bdot(r, off) for r, off in zip(rows, offs)]
            filler(level, nlevels)
            upds = [r - _bdot(p, t) for r, p, t in zip(rows, prods, ts)]
            ts = [merged(t, u, lo) for t, u, lo in zip(ts, upds, lowers)]
        n *= 2
        level += 1
    return ts


_NT = (((1,), (1,)), ((), ()))
_TN = (((0,), (0,)), ((), ()))


def _prep_kernel(qkv_ref, g_ref, gt_ref, tall_ref, wide_ref, el_ref, *, bsz):
    c = CHUNK
    ri = lax.broadcasted_iota(jnp.int32, (c, c), 0)
    ci = lax.broadcasted_iota(jnp.int32, (c, c), 1)
    tri_l = jnp.where(ri >= ci, 1.0, 0.0)
    tri_u = jnp.where(ri <= ci, 1.0, 0.0)
    incl = (ri >= ci, ri <= ci)
    strict = (ri > ci, ri < ci)
    hdot = functools.partial(jnp.dot, precision=HIGHEST, preferred_element_type=F32)

    per = tall_ref.shape[0]
    cb = [(ch, b) for ch in range(per) for b in range(bsz)]
    rows = lambda ch: slice(ch * c, (ch + 1) * c)
    gs = [g_ref[b, rows(ch), :] for ch, b in cb]
    gts = [gt_ref[b, ch] for ch, b in cb]
    gcol = [(hdot(tri_l, g), hdot(tri_u, g)) for g in gs]
    grow = [(hdot(gt, tri_u), hdot(gt, tri_l)) for gt in gts]
    glast = [(gc[0][c - 1:c], gc[1][0:1]) for gc in gcol]
    el_ref[...] = jnp.zeros_like(el_ref)
    for p, (ch, b) in enumerate(cb):
        for d in range(2):
            el_ref[ch, d * bsz + b:d * bsz + b + 1, :] = jnp.exp(glast[p][d])

    pbh = [(p, hh) for p in range(len(cb)) for hh in range(HEADS)]
    hs = lambda hh: slice(hh * HEAD_DIM, (hh + 1) * HEAD_DIM)
    tok = lambda w, p, hh: qkv_ref[w, cb[p][1], rows(cb[p][0]), hs(hh)]
    ks = [tok(1, p, hh) for p, hh in pbh]
    qs = [tok(0, p, hh) for p, hh in pbh]
    kk = [lax.dot_general(k, k, _NT, preferred_element_type=F32) for k in ks]
    qk = [lax.dot_general(q, k, _NT, preferred_element_type=F32) for q, k in zip(qs, ks)]
    inst = [(d, p, hh, i) for d in range(2) for i, (p, hh) in enumerate(pbh)]
    decs, betas, gcs = [], [], []
    for d, p, hh, _ in inst:
        lg = 2 * HEADS + d * HEADS + hh
        gcs.append(gcol[p][d][:, lg:lg + 1])
        betas.append(jnp.broadcast_to(gs[p][:, d * HEADS + hh:d * HEADS + hh + 1], (c, HEAD_DIM)))
        decs.append(jnp.exp(jnp.where(incl[d], gcs[-1] - grow[p][d][lg:lg + 1, :], -jnp.inf)))
    ls = [jnp.where(strict[d], kk[i] * beta[:, 0:c] * dec, 0.0)
          for (d, _, _, i), beta, dec in zip(inst, betas, decs)]

    def scaled_operands(part, nparts):
        for (d, p, hh, i), dec, beta, gc in list(zip(inst, decs, betas, gcs))[part::nparts]:
            ch, b = cb[p]
            lg = 2 * HEADS + d * HEADS + hh
            eg = jnp.broadcast_to(jnp.exp(gc), (c, HEAD_DIM))
            k = ks[i].astype(F32)
            tall_ref[ch, d, b, hh, :, c:2 * c] = (qk[i] * dec).astype(BF16)
            wide_ref[ch, d, b, hh, 0:c, :] = (qs[i].astype(F32) * eg).astype(BF16)
            wide_ref[ch, d, b, hh, c:2 * c, :] = (k * beta * eg).astype(BF16)
            wide_ref[ch, d, b, hh, 2 * c:3 * c, :] = (tok(2, p, hh).astype(F32) * beta).astype(BF16)
            wide_ref[ch, d, b, hh, 3 * c:4 * c, :] = (k * jnp.exp(glast[p][d][:, lg:lg + 1] - gc)).astype(BF16)

    ts = _unit_tri_inverse(ls, [d == 0 for d, _, _, _ in inst], scaled_operands)
    for (d, p, hh, _), t in zip(inst, ts):
        tall_ref[cb[p][0], d, cb[p][1], hh, :, 0:c] = t.astype(BF16)


def _prep_call(qkv, gates, gates_t):
    _, bsz, lt, dn = qkv.shape
    nc = lt // CHUNK
    per = PREP_CHUNKS
    assert nc % per == 0
    tall = (2, bsz, HEADS, CHUNK, 2 * CHUNK)
    wide = (2, bsz, HEADS, 4 * CHUNK, HEAD_DIM)
    return pl.pallas_call(
        functools.partial(_prep_kernel, bsz=bsz),
        grid=(nc // per,),
        in_specs=[pl.BlockSpec((3, bsz, per * CHUNK, dn), lambda i: (0, 0, i, 0)),
                  pl.BlockSpec((bsz, per * CHUNK, LANES), lambda i: (0, i, 0)),
                  pl.BlockSpec((bsz, per, 4 * HEADS, CHUNK), lambda i: (0, i, 0, 0))],
        out_specs=[pl.BlockSpec((per,) + tall, lambda i: (i, 0, 0, 0, 0, 0)),
                   pl.BlockSpec((per,) + wide, lambda i: (i, 0, 0, 0, 0, 0)),
                   pl.BlockSpec((per, SUBLANES, LANES), lambda i: (i, 0, 0))],
        out_shape=[jax.ShapeDtypeStruct((nc,) + tall, BF16),
                   jax.ShapeDtypeStruct((nc,) + wide, BF16),
                   jax.ShapeDtypeStruct((nc, SUBLANES, LANES), F32)],
        compiler_params=_cparams(("arbitrary",)),
        name="prep",
    )(qkv, gates, gates_t)


def _scan_kernel(tf_ref, tb_ref, wf_ref, wb_ref, ef_ref, eb_ref, of_ref, ob_ref, s_ref, *, bsz):
    i = pl.program_id(0)

    @pl.when(i == 0)
    def _():
        s_ref[...] = jnp.zeros_like(s_ref)

    c = CHUNK
    per = tf_ref.shape[0]
    dirs = ((tf_ref, wf_ref, ef_ref, of_ref), (tb_ref, wb_ref, eb_ref, ob_ref))
    inst = [(d, b, hh) for d in range(2) for b in range(bsz) for hh in range(HEADS)]
    state = [s_ref[d, b, hh] for d, b, hh in inst]
    for sub in range(per):
        at = (sub, per - 1 - sub)
        a1 = [jnp.dot(dirs[d][1][at[d], b, hh, 0:2 * c, :], s.astype(BF16), preferred_element_type=F32)
              for (d, b, hh), s in zip(inst, state)]
        v_new = [jnp.dot(dirs[d][0][at[d], b, hh, :, 0:c],
                         (dirs[d][1][at[d], b, hh, 2 * c:3 * c, :].astype(F32) - a[c:]).astype(BF16),
                         preferred_element_type=F32).astype(BF16) for (d, b, hh), a in zip(inst, a1)]
        new_state = []
        for (d, b, hh), a, vn, s in zip(inst, a1, v_new, state):
            lg = 2 * HEADS + d * HEADS + hh
            o = a[:c] + jnp.dot(dirs[d][0][at[d], b, hh, :, c:2 * c], vn, preferred_element_type=F32)
            dirs[d][3][b, at[d] * c:(at[d] + 1) * c, hh * HEAD_DIM:(hh + 1) * HEAD_DIM] = o.astype(BF16)
            el = dirs[d][2][at[d], d * bsz + b:d * bsz + b + 1, lg:lg + 1]
            new_state.append(s * el + lax.dot_general(dirs[d][1][at[d], b, hh, 3 * c:4 * c, :], vn, _TN,
                                                      preferred_element_type=F32))
        state = new_state
    for (d, b, hh), s in zip(inst, state):
        s_ref[d, b, hh] = s


def _scan_call(tall, wide, el, n_lat, n_ctx):
    nc, _, bsz = tall.shape[:3]
    dn = HEADS * HEAD_DIM
    per = SCAN_CHUNKS
    ncx, ncc, ns = n_lat // CHUNK // per, n_ctx // CHUNK // per, nc // per
    assert nc == (ncx + ncc) * per
    cf = lambda i: jnp.where(i < ncc, ncx + i, i - ncc)
    cb = lambda i: ns - 1 - i

    dspec = lambda arr, fn, d: pl.BlockSpec((per, None) + arr.shape[2:], lambda i: (fn(i), d, 0, 0, 0, 0))
    espec = lambda fn: pl.BlockSpec((per, SUBLANES, LANES), lambda i: (fn(i), 0, 0))
    ospec = lambda fn: pl.BlockSpec((bsz, per * CHUNK, dn), lambda i: (0, fn(i), 0))
    return pl.pallas_call(
        functools.partial(_scan_kernel, bsz=bsz),
        grid=(ns,),
        in_specs=[dspec(tall, cf, 0), dspec(tall, cb, 1), dspec(wide, cf, 0), dspec(wide, cb, 1),
                  espec(cf), espec(cb)],
        out_specs=[ospec(lambda i: jnp.maximum(i - ncc, 0)), ospec(lambda i: jnp.minimum(ns - 1 - i, ncx - 1))],
        out_shape=[jax.ShapeDtypeStruct((bsz, n_lat, dn), BF16)] * 2,
        scratch_shapes=[pltpu.VMEM((2, bsz, HEADS, HEAD_DIM, HEAD_DIM), F32)],
        compiler_params=_cparams(("arbitrary",)),
        name="scan",
    )(tall, tall, wide, wide, el, el)


def _hidden_kernel(pe_ref, w1_ref, b1_ref, w2_ref, b2_ref, w3_ref, b3_ref, f_ref, o_ref):
    f = f_ref[...]
    dot = functools.partial(jnp.dot, precision=HIGHEST, preferred_element_type=F32)
    a = jnp.sin(f * (dot(pe_ref[...], w1_ref[...]) + b1_ref[...]))
    a = jnp.sin(f * (dot(a, w2_ref[...]) + b2_ref[...]))
    o_ref[...] = jnp.sin(f * (dot(a, w3_ref[...]) + b3_ref[...]))


def _hidden_call(pe, w1p, b1, w2, b2, w3, b3, freq):
    rows, pw = pe.shape
    fw = 2 * w2.shape[0]
    tr = FILTER_TILE
    full = lambda a: pl.BlockSpec(a.shape, lambda t: (0,) * a.ndim)
    two = lambda a: jnp.tile(a.reshape(1, -1), (1, 2))
    diag2 = lambda a: jnp.kron(jnp.eye(2, dtype=a.dtype), a)
    args = (diag2(w1p), two(b1), diag2(w2), two(b2), diag2(w3), two(b3), two(freq))
    return pl.pallas_call(
        _hidden_kernel,
        grid=(rows // tr,),
        in_specs=[pl.BlockSpec((tr, pw), lambda t: (t, 0))] + [full(a) for a in args],
        out_specs=pl.BlockSpec((tr, fw), lambda t: (t, 0)),
        out_shape=jax.ShapeDtypeStruct((rows, fw), F32),
        compiler_params=_cparams(("arbitrary",)),
        name="hidden",
    )(pe, *args)


def _ftime_kernel(a_ref, t_ref, w_ref, dl_ref, h_ref, ss_ref, *, n_lat, tr):
    t = pl.program_id(0)
    h = _bdot(a_ref[...], w_ref[...]) * jnp.exp(-t_ref[...] * dl_ref[...])
    row = lax.broadcasted_iota(jnp.int32, (tr, 1), 0) + t * tr
    h = jnp.where(row == n_lat, 0.0, h)
    h_ref[...] = h.astype(h_ref.dtype)

    @pl.when(t == 0)
    def _():
        ss_ref[...] = jnp.zeros_like(ss_ref)

    ss_ref[...] += jnp.sum(h * h, axis=0, keepdims=True)


def _ftime_call(a3, tcol, wout2, deltas2, n_lat):
    rows = tcol.shape[0]
    fw = a3.shape[1]
    ncol = wout2.shape[2]
    tr = FILTER_TILE
    side = lambda t: (t * tr) // n_lat
    return pl.pallas_call(
        functools.partial(_ftime_kernel, n_lat=n_lat, tr=tr),
        grid=(rows // tr,),
        in_specs=[pl.BlockSpec((tr, fw), lambda t: (t % (n_lat // tr), 0)),
                  pl.BlockSpec((tr, 1), lambda t: (t, 0)),
                  pl.BlockSpec((None, fw, ncol), lambda t: (side(t), 0, 0)),
                  pl.BlockSpec((1, ncol), lambda t: (0, 0))],
        out_specs=[pl.BlockSpec((tr, ncol), lambda t: (t, 0)),
                   pl.BlockSpec((1, ncol), lambda t: (0, 0))],
        out_shape=[jax.ShapeDtypeStruct((rows, ncol), BF16), jax.ShapeDtypeStruct((1, ncol), F32)],
        compiler_params=_cparams(("arbitrary",)),
        name="ftime",
    )(a3, tcol, wout2, deltas2)


@functools.lru_cache(maxsize=None)
def _fft_tables(n_lat):
    n = 2 * n_lat
    n1t = n // FFT_INNER
    h1 = n1t // 2
    n2 = np.arange(FFT_INNER, dtype=np.int64)[:, None, None]
    k1 = np.arange(n1t, dtype=np.int64)[None, :, None]
    n1 = np.arange(h1, dtype=np.int64)[None, None, :]
    ang = (-2.0 * np.pi / n) * ((k1 * (FFT_INNER * n1 + n2)) % n)
    er, ei = np.cos(ang), np.sin(ang)
    e1 = np.concatenate([np.concatenate([er, -ei], axis=2), np.concatenate([ei, er], axis=2)], axis=1)
    e4 = np.transpose(e1, (0, 2, 1))
    sgn = np.where(np.arange(n1t) % 2 == 0, 1.0, -1.0)[None, :, None]
    ef = np.concatenate([np.concatenate([er, sgn * er], axis=2), np.concatenate([ei, sgn * ei], axis=2)], axis=1)
    a = np.arange(FFT_INNER, dtype=np.int64)
    ang2 = (-2.0 * np.pi / FFT_INNER) * ((a[:, None] * a[None, :]) % FFT_INNER)
    gr, gi = np.cos(ang2), np.sin(ang2)
    g2 = np.block([[gr, -gi], [gi, gr]])
    f32 = lambda x: np.ascontiguousarray(x, dtype=np.float32)
    return f32(e1), f32(e4), f32(ef), f32(g2), f32(g2.T)


def _hold(p, j, first, last, nsteps):
    active = (p >= first) & (p <= last) & (lax.rem(p - first, 2) == 0)
    return jnp.where(active, j, jnp.where(p < first, 0, nsteps - 1))


def _slab_read(x_ref, rows):
    return jnp.concatenate([x_ref[c, rows, :] for c in range(x_ref.shape[0])], axis=1)


def _slab_write(x_ref, rows, val):
    for c in range(x_ref.shape[0]):
        x_ref[c, rows, :] = val[:, c * LANES:(c + 1) * LANES]


def _spectrum_steps(x_ref, g2_ref, k1s, n1t, stride):
    cols = [jnp.concatenate([_slab_read(x_ref, pl.ds(k1, FFT_INNER, stride=stride)),
                             _slab_read(x_ref, pl.ds(n1t + k1, FFT_INNER, stride=stride))], axis=0).astype(BF16)
            for k1 in k1s]
    return [jnp.dot(g2_ref[...], c, preferred_element_type=F32) for c in cols]


def _ffft_kernel(h_ref, ef_ref, g2_ref, ss_ref, o_ref, x_ref, *, n1t, grp, kgrp, stride, inv_n):
    p = pl.program_id(1)
    j = pl.program_id(2)

    @pl.when(p == 0)
    def _():
        for j0 in range(0, grp, FFT_INTERLEAVE):
            jjs = list(range(j0, min(j0 + FFT_INTERLEAVE, grp)))
            us = [jnp.concatenate([h_ref[0, jj], h_ref[1, jj]], axis=0) for jj in jjs]
            slabs = [_bdot(ef_ref[jj], u) for jj, u in zip(jjs, us)]
            for jj, slab in zip(jjs, slabs):
                base = pl.multiple_of((j * grp + jj) * stride, SUBLANES)
                _slab_write(x_ref, pl.ds(base, 2 * n1t), slab)

    @pl.when(p == 1)
    def _():
        scale = lax.rsqrt(ss_ref[...] + EPS) * inv_n
        for j0 in range(0, kgrp, FFT_INTERLEAVE):
            jjs = list(range(j0, min(j0 + FFT_INTERLEAVE, kgrp)))
            xfs = _spectrum_steps(x_ref, g2_ref, [j * kgrp + jj for jj in jjs], n1t, stride)
            for jj, xf in zip(jjs, xfs):
                o_ref[jj] = (xf * scale).astype(BF16)


def _fft_groups(n1t, ns):
    ns = min(ns, n1t)
    assert FFT_INNER % ns == 0 and n1t % ns == 0
    return ns, FFT_INNER // ns, n1t // ns


def _ffft_call(hfull, ssq, ef, g2, n_lat):
    ncol = hfull.shape[1]
    n1t = 2 * n_lat // FFT_INNER
    h1 = n1t // 2
    ct = FFT_CHANNELS
    ns, grp, kgrp = _fft_groups(n1t, FILTER_FFT_STEPS)
    stride = 2 * n1t + SLAB_PAD
    hv = hfull.reshape(2, FFT_INNER, h1, ncol)
    nct = ncol // ct
    cpo = nct // 2
    return pl.pallas_call(
        functools.partial(_ffft_kernel, n1t=n1t, grp=grp, kgrp=kgrp, stride=stride, inv_n=1.0 / (2 * n_lat)),
        grid=(nct, 2, ns),
        in_specs=[pl.BlockSpec((2, grp, h1, ct), lambda c, p, j: (0, _hold(p, j, 0, 0, ns), 0, c)),
                  pl.BlockSpec((grp, 2 * n1t, n1t), lambda c, p, j: (_hold(p, j, 0, 0, ns), 0, 0)),
                  pl.BlockSpec((2 * FFT_INNER, 2 * FFT_INNER), lambda c, p, j: (0, 0)),
                  pl.BlockSpec((1, ct), lambda c, p, j: (0, c))],
        out_specs=pl.BlockSpec((None, kgrp, 2 * FFT_INNER, ct),
                               lambda c, p, j: (c // cpo, _hold(p, j, 1, 1, ns), 0, c % cpo)),
        out_shape=jax.ShapeDtypeStruct((2, n1t, 2 * FFT_INNER, ncol // 2), BF16),
        scratch_shapes=[pltpu.VMEM((ct // LANES, FFT_INNER * stride, LANES), F32)],
        compiler_params=_cparams(("arbitrary", "arbitrary", "arbitrary")),
        name="ffft",
    )(hv, ef, g2, ssq)


def _hyena_kernel(v_ref, x1_ref, x2_ref, hf_ref, e1_ref, e4_ref, g2_ref, g3_ref, bias_ref,
                  o_ref, x_ref, z_ref, *, n1t, grp, kgrp, stride):
    p = pl.program_id(1)
    j = pl.program_id(2)
    h1 = n1t // 2

    def gather(ref, jj):
        return jnp.concatenate([ref[0, jj], ref[1, jj]], axis=0).astype(F32)

    def slab_base(jj):
        return pl.multiple_of((j * grp + jj) * stride, SUBLANES)

    def time_rows(jjs):
        slabs = [_slab_read(x_ref, pl.ds(slab_base(jj), 2 * n1t)).astype(BF16) for jj in jjs]
        return [jnp.dot(e4_ref[jj], s, preferred_element_type=F32) for jj, s in zip(jjs, slabs)]

    def write_slabs(jjs, us):
        slabs = [jnp.dot(e1_ref[jj], u, preferred_element_type=F32) for jj, u in zip(jjs, us)]
        for jj, slab in zip(jjs, slabs):
            _slab_write(x_ref, pl.ds(slab_base(jj), 2 * n1t), slab)

    def chunks(n):
        return [list(range(j0, min(j0 + FFT_INTERLEAVE, n))) for j0 in range(0, n, FFT_INTERLEAVE)]

    @pl.when(p == 0)
    def _():
        for jjs in chunks(grp):
            write_slabs(jjs, [gather(v_ref, jj).astype(BF16) for jj in jjs])

    @pl.when((p == 1) | (p == 3))
    def _():
        for jjs in chunks(kgrp):
            k1s = [j * kgrp + jj for jj in jjs]
            xfs = _spectrum_steps(x_ref, g2_ref, k1s, n1t, stride)
            ys = []
            for jj, xf in zip(jjs, xfs):
                hf = hf_ref[jj].astype(F32)
                xr, xi = xf[:FFT_INNER], xf[FFT_INNER:]
                hr, hi = hf[:FFT_INNER], hf[FFT_INNER:]
                ys.append(jnp.concatenate([xr * hr - xi * hi, xr * hi + xi * hr], axis=0).astype(BF16))
            bbs = [jnp.dot(g3_ref[...], y, preferred_element_type=F32) for y in ys]
            for k1, bb in zip(k1s, bbs):
                _slab_write(x_ref, pl.ds(k1, FFT_INNER, stride=stride), bb[:FFT_INNER])
                _slab_write(x_ref, pl.ds(n1t + k1, FFT_INNER, stride=stride), bb[FFT_INNER:])

    @pl.when(p == 2)
    def _():
        for jjs in chunks(grp):
            z1s = [(gather(x1_ref, jj) * (y + gather(v_ref, jj) * bias_ref[0:1, :])).astype(BF16)
                   for jj, y in zip(jjs, time_rows(jjs))]
            for jj, z1 in zip(jjs, z1s):
                z_ref[j * grp + jj] = z1
            write_slabs(jjs, z1s)

    @pl.when(p == 4)
    def _():
        for jjs in chunks(grp):
            for jj, y in zip(jjs, time_rows(jjs)):
                out = gather(x2_ref, jj) * (y + z_ref[j * grp + jj].astype(F32) * bias_ref[1:2, :])
                o_ref[0, jj] = out[:h1].astype(o_ref.dtype)
                o_ref[1, jj] = out[h1:].astype(o_ref.dtype)


def _hyena_call(hv, hf, e1, e4, g2, g3, bias):
    _, bsz, _, h1, ch = hv.shape
    assert bsz == 2, "the two batch rows are packed as the real and imaginary parts of one FFT"
    n1t = 2 * h1
    ct = FFT_CHANNELS
    ns, grp, kgrp = _fft_groups(n1t, CONV_FFT_STEPS)
    stride = 2 * n1t + SLAB_PAD

    def hspec(which, first, last):
        return pl.BlockSpec((None, bsz, grp, h1, ct),
                            lambda c, p, j: (which, 0, _hold(p, j, first, last, ns), 0, c))

    def hf_map(c, p, j):
        order = jnp.where(p >= 2, 1, 0)
        step = jnp.where((p == 1) | (p == 3), j, jnp.where((p == 0) | (p == 2), 0, ns - 1))
        return (order, step, 0, c)

    return pl.pallas_call(
        functools.partial(_hyena_kernel, n1t=n1t, grp=grp, kgrp=kgrp, stride=stride),
        grid=(ch // ct, 5, ns),
        in_specs=[hspec(0, 0, 2), hspec(1, 2, 2), hspec(2, 4, 4),
                  pl.BlockSpec((None, kgrp, 2 * FFT_INNER, ct), hf_map),
                  pl.BlockSpec((grp, 2 * n1t, n1t), lambda c, p, j: (_hold(p, j, 0, 2, ns), 0, 0)),
                  pl.BlockSpec((grp, n1t, 2 * n1t), lambda c, p, j: (_hold(p, j, 2, 4, ns), 0, 0)),
                  pl.BlockSpec((2 * FFT_INNER, 2 * FFT_INNER), lambda c, p, j: (0, 0)),
                  pl.BlockSpec((2 * FFT_INNER, 2 * FFT_INNER), lambda c, p, j: (0, 0)),
                  pl.BlockSpec((2, ct), lambda c, p, j: (0, c))],
        out_specs=pl.BlockSpec((bsz, grp, h1, ct), lambda c, p, j: (0, _hold(p, j, 4, 4, ns), 0, c)),
        out_shape=jax.ShapeDtypeStruct((bsz, FFT_INNER, h1, ch), BF16),
        scratch_shapes=[pltpu.VMEM((ct // LANES, FFT_INNER * stride, LANES), F32),
                        pltpu.VMEM((FFT_INNER, n1t, ct), BF16)],
        compiler_params=_cparams(("arbitrary", "arbitrary", "arbitrary"), VMEM_LIMIT_CONV),
        name="hyena",
    )(hv, hv, hv, hf, e1, e4, g2, g3, bias)


def _merge_kernel(of_ref, ob_ref, gp_ref, hy_ref, x_ref, mod_ref, dng_ref, wpa_ref, wpb_ref, wo_ref,
                  fg_ref, o_ref, *, d):
    b = pl.program_id(0)
    blocks = [slice(r * MXU_TILE, (r + 1) * MXU_TILE) for r in range(o_ref.shape[1] // MXU_TILE)]
    f32 = lambda ref, *idx: ref[idx].astype(F32)

    def head_norm(rows):
        o = f32(of_ref, 0, rows) + f32(ob_ref, 0, rows)
        za = f32(gp_ref, 0, 0, rows)
        parts = []
        for hh in range(HEADS):
            hs = slice(hh * HEAD_DIM, (hh + 1) * HEAD_DIM)
            oh = o[:, hs]
            ms = jnp.mean(oh * oh, axis=-1, keepdims=True)
            parts.append((oh * lax.rsqrt(ms + EPS) * dng_ref[...] * za[:, hs]).astype(BF16))
        return jnp.concatenate(parts, axis=1)

    o_a = [head_norm(rows) for rows in blocks]
    y_a = [jnp.dot(a, wpa_ref[...], preferred_element_type=F32) for a in o_a]
    o_b = [(f32(hy_ref, 0, rows) * f32(gp_ref, 1, 0, rows)).astype(BF16) for rows in blocks]
    y_b = [jnp.dot(a, wpb_ref[...], preferred_element_type=F32) for a in o_b]
    m = [(f32(gp_ref, 2, 0, rows) * ya + f32(gp_ref, 3, 0, rows) * yb).astype(BF16)
         for rows, ya, yb in zip(blocks, y_a, y_b)]
    y = [jnp.dot(a, wo_ref[...], preferred_element_type=F32) for a in m]
    gate = mod_ref[pl.ds(b, 1), :][:, 2 * d:3 * d]
    for rows, yr in zip(blocks, y):
        xn = x_ref[0, rows, :] + gate * yr
        o_ref[0, rows, :] = xn * lax.rsqrt(jnp.mean(xn * xn, axis=-1, keepdims=True) + EPS) * fg_ref[...]


def _merge_call(o_f, o_b, gp, hy, x, mod, dn_norm_g, w_pa, w_pb, w_out, final_g):
    bsz, n_lat, d = x.shape
    tm = MERGE_TILE if n_lat % MERGE_TILE == 0 else 256
    wfull = lambda a: pl.BlockSpec(a.shape, lambda b, t: (0,) * a.ndim)
    tok = pl.BlockSpec((1, tm, d), lambda b, t: (b, t, 0))
    return pl.pallas_call(
        functools.partial(_merge_kernel, d=d),
        grid=(bsz, n_lat // tm),
        in_specs=[tok, tok,
                  pl.BlockSpec((4, 1, tm, d), lambda b, t: (0, b, t, 0)),
                  tok, tok, wfull(mod),
                  pl.BlockSpec((1, HEAD_DIM), lambda b, t: (0, 0)),
                  wfull(w_pa), wfull(w_pb), wfull(w_out),
                  pl.BlockSpec((1, d), lambda b, t: (0, 0))],
        out_specs=tok,
        out_shape=jax.ShapeDtypeStruct((bsz, n_lat, d), F32),
        compiler_params=_cparams(("arbitrary", "arbitrary")),
        name="merge",
    )(o_f, o_b, gp, hy, x, mod, dn_norm_g.reshape(1, HEAD_DIM), w_pa, w_pb, w_out, final_g.reshape(1, d))


def _position_features(n_lat):
    h1 = n_lat // FFT_INNER
    m = (np.arange(FFT_INNER)[:, None] + FFT_INNER * np.arange(h1)[None, :]).reshape(-1)
    lag = np.concatenate([m, n_lat - m]).astype(np.float64)
    lag[n_lat] = 0.0
    bands = (HY_EMB - 1) // 2
    t = (lag / (n_lat - 1))[:, None]
    wpos = (2.0 * math.pi / n_lat) * lag[:, None]
    fb = np.linspace(1e-4, bands - 1, bands)[None, :]
    pe = np.concatenate([t, np.cos(fb * wpos), -np.sin(fb * wpos)], axis=1)
    pe_pad = np.zeros((2 * n_lat, LANES), np.float32)
    pe_pad[:, :HY_EMB] = pe
    return pe_pad, np.ascontiguousarray(t, dtype=np.float32)


def kernel(x, c, ctx, c_ctx, w_mod, b_mod, norm_g, w_in, dn_conv_w, dn_a_log, dn_dt_bias, dn_norm_g,
           hy_conv_w, hy_conv_b, hy_f_w1, hy_f_b1, hy_f_w2, hy_f_b2, hy_f_w3, hy_f_b3, hy_f_wout,
           hy_f_freq, hy_bias, w_pa, w_pb, w_out, final_g):
    bsz, n_lat, d = x.shape
    n_ctx = ctx.shape[1]
    assert w_mod.shape[0] == 1, "single layer: the context stream is only read through its scan states"
    dn = HEADS * HEAD_DIM
    hy = hy_bias.shape[-1]
    assert n_lat % 256 == 0 and n_ctx % 256 == 0 and dn == d and hy == d

    cvec = jnp.zeros((SUBLANES, d), F32).at[:bsz].set(c).at[bsz].set(c_ctx)
    mod = _mod_call(cvec, w_mod[0], b_mod[0])

    w = w_in[0].astype(BF16)
    o_qkv, o_gate, o_hy, o_gp = 0, 4 * dn, 4 * dn + 4 * HEADS, 4 * dn + 4 * HEADS + 3 * hy
    col3 = lambda a, off, n: jnp.transpose(a[:, off:off + 3 * n].reshape(a.shape[0], 3, n), (1, 0, 2))
    lt = n_lat + n_ctx
    tm = PROJ_TILE if n_lat % PROJ_TILE == 0 else 256

    wg = jnp.zeros((d, LANES), BF16).at[:, :4 * HEADS].set(w[:, o_gate:o_gate + 4 * HEADS])
    lane_pad = lambda a: jnp.zeros((1, LANES), F32).at[0, 2 * HEADS:4 * HEADS].set(a.reshape(-1))
    gate_args = (wg, lane_pad(dn_a_log[0]), lane_pad(dn_dt_bias[0]))
    h_tok, gates = _hnorm_call(x, mod, norm_g[0], *gate_args, tm=NORM_TILE, gate_rows=lt, gate_tile0=0, name="hnorm")
    h_ctx, gates = _hnorm_call(ctx, mod, norm_g[0], *gate_args, tm=n_ctx, gate_rows=lt, gate_tile0=n_lat // n_ctx,
                               mod_row=bsz, prev_gates=gates, name="hnorm_ctx")
    h1 = n_lat // FFT_INNER
    h_perm = jnp.transpose(h_tok.reshape(bsz, h1, FFT_INNER, d), (0, 2, 1, 3)).reshape(bsz, n_lat, d)

    w_qkv, cw_qkv = col3(w, o_qkv, dn), col3(dn_conv_w[0], 0, dn)
    qkv = _convproj_call(h_tok, w_qkv, cw_qkv, rowlen=GRID_W, tm=tm, tile0=0, out_rows=lt, name="qkv")
    qkv = _convproj_call(h_ctx, w_qkv, cw_qkv, rowlen=n_ctx, tm=n_ctx, tile0=n_lat // n_ctx, out_rows=lt,
                         prev=qkv, name="qkv_ctx")
    gates_t = jnp.transpose(gates[:, :, :4 * HEADS].reshape(bsz, lt // CHUNK, CHUNK, 4 * HEADS), (0, 1, 3, 2))
    tall, wide, el = _prep_call(qkv, gates, gates_t)
    o_f, o_b = _scan_call(tall, wide, el, n_lat, n_ctx)

    hyp = _hyproj_call(h_perm, col3(w, o_hy, hy), col3(hy_conv_w[0], 0, hy), hy_conv_b[0].reshape(3, 1, hy))
    w4 = jnp.stack([w[:, 3 * dn:4 * dn]] + [w[:, o_gp + i * d:o_gp + (i + 1) * d] for i in range(3)])
    gp = _gproj_call(h_tok, w4, n_lat)

    pe, tcol = _position_features(n_lat)
    w1p = jnp.zeros((LANES, hy_f_w1.shape[-1]), F32).at[:HY_EMB].set(hy_f_w1[0])
    pe2 = jnp.asarray(np.concatenate([pe[:n_lat], pe[n_lat:]], axis=1))
    a3 = _hidden_call(pe2, w1p, hy_f_b1[0], hy_f_w2[0], hy_f_b2[0], hy_f_w3[0], hy_f_b3[0], hy_f_freq[0])
    deltas = np.abs(np.linspace(math.log(HY_DECAY_TARGET) / HY_SLOW_DECAY,
                                math.log(HY_DECAY_TARGET) / HY_FAST_DECAY, hy)).astype(np.float32)
    deltas2 = jnp.asarray(np.tile(deltas, 2)[None, :])
    wside = hy_f_wout[0].reshape(-1, 2, 2 * hy).transpose(1, 0, 2).astype(BF16)
    zside = jnp.zeros_like(wside[0])
    wout2 = jnp.stack([jnp.concatenate([wside[0], zside]), jnp.concatenate([zside, wside[1]])])
    hfull, ssq = _ftime_call(a3, jnp.asarray(tcol), wout2, deltas2, n_lat)
    e1, e4, ef, g2, g3 = (jnp.asarray(t).astype(BF16) for t in _fft_tables(n_lat))
    hf = _ffft_call(hfull, ssq, ef, g2, n_lat)

    yh = _hyena_call(hyp, hf, e1, e4, g2, g3, hy_bias[0])
    yh = jnp.transpose(yh, (0, 2, 1, 3)).reshape(bsz, n_lat, hy)

    return _merge_call(o_f, o_b, gp, yh, x, mod, dn_norm_g[0], w_pa[0].astype(BF16), w_pb[0].astype(BF16),
                       w_out[0].astype(BF16), final_g)
```
